```python
import math
import jax
import jax.numpy as jnp
from jax import lax
import numpy as np


D_MODEL = 1024
BATCH = 16
SEQ = 2048
DEPTH = 2

N_A_LAYERS = DEPTH // 2
N_B_LAYERS = DEPTH - N_A_LAYERS

SSM_GROUP = 16
SSM_GROUPS = D_MODEL // SSM_GROUP
SSM_STATE = 64
DT_MIN = 0.001
DT_MAX = 0.1

HEAD_DIM = 64
HEADS_PER_GROUP = D_MODEL // HEAD_DIM
DILATED_GROUPS = ((128, 1), (512, 4), (2048, 16))
N_DIL = len(DILATED_GROUPS)
BAND = DILATED_GROUPS[0][0] // DILATED_GROUPS[0][1]
MAX_DIL = max(d for _, d in DILATED_GROUPS)
ATT_WIDTH = N_DIL * HEADS_PER_GROUP * HEAD_DIM
MERGED_WIDTH = HEADS_PER_GROUP * HEAD_DIM
NEG_BIG = -1e30

REL_BUCKETS = 32
REL_MAX_DIST = 2048

D_FF = 2816
CONV_WIDTH = 3

DN_ALPHA = (2.0 * DEPTH) ** 0.25
DN_BETA = (8.0 * DEPTH) ** -0.25
LN_EPS = 1e-5

kernel_name = 'yoco_s5_dilated_attn_deepnorm_trunk'


def layer_norm(x, gain, bias):
    xf = x.astype(jnp.float32)
    mu = jnp.mean(xf, axis=-1, keepdims=True)
    var = jnp.mean(jnp.square(xf - mu), axis=-1, keepdims=True)
    y = (xf - mu) * lax.rsqrt(var + LN_EPS) * gain.astype(jnp.float32) + bias.astype(jnp.float32)
    return y.astype(x.dtype)


def post_norm(x, f, gain, bias):
    return layer_norm(DN_ALPHA * x + f.astype(x.dtype), gain, bias)


def _complex_affine_combine(e1, e2):
    a1r, a1i, b1r, b1i = e1
    a2r, a2i, b2r, b2i = e2
    return (a2r * a1r - a2i * a1i,
            a2r * a1i + a2i * a1r,
            a2r * b1r - a2i * b1i + b2r,
            a2r * b1i + a2i * b1r + b2i)


def s5_mixer(x, lam_re, lam_im, log_dt, b_re, b_im, c_re, c_im, d_skip, w_glu, b_glu, w_out):
    f32 = jnp.float32
    bsz, seq, _ = x.shape
    u = x.astype(f32).reshape(bsz, seq, SSM_GROUPS, SSM_GROUP)
    lr, li = lam_re.astype(f32), lam_im.astype(f32)
    dt = jnp.exp(log_dt.astype(f32))[:, None]
    mag = jnp.exp(lr * dt)
    ab_r, ab_i = mag * jnp.cos(li * dt), mag * jnp.sin(li * dt)
    den = lr * lr + li * li
    nr = ab_r - 1.0
    co_r = (nr * lr + ab_i * li) / den
    co_i = (ab_i * lr - nr * li) / den
    br, bi = b_re.astype(f32), b_im.astype(f32)
    bb_r = co_r[..., None] * br - co_i[..., None] * bi
    bb_i = co_r[..., None] * bi + co_i[..., None] * br
    bu_r = jnp.einsum('blgh,gph->blgp', u, bb_r)
    bu_i = jnp.einsum('blgh,gph->blgp', u, bb_i)
    a_r = jnp.broadcast_to(ab_r, (1, seq, SSM_GROUPS, SSM_STATE))
    a_i = jnp.broadcast_to(ab_i, (1, seq, SSM_GROUPS, SSM_STATE))
    _, _, h_r, h_i = lax.associative_scan(_complex_affine_combine, (a_r, a_i, bu_r, bu_i), axis=1)
    y = (jnp.einsum('blgp,ghp->blgh', h_r, c_re.astype(f32))
         - jnp.einsum('blgp,ghp->blgh', h_i, c_im.astype(f32))
         + d_skip.astype(f32) * u)
    y = jax.nn.gelu(y.reshape(bsz, seq, D_MODEL))
    g = y * jax.nn.sigmoid(y @ w_glu.astype(f32) + b_glu.astype(f32))
    return (g @ w_out.astype(f32)).astype(x.dtype)


def conv_glu_ffn(x, w_up, conv_w, conv_b, w_down):
    seq = x.shape[1]
    hcat = x @ w_up
    hp = jnp.pad(hcat, ((0, 0), (CONV_WIDTH - 1, 0), (0, 0)))
    hcat = conv_b + sum(conv_w[k] * hp[:, CONV_WIDTH - 1 - k:CONV_WIDTH - 1 - k + seq]
                        for k in range(CONV_WIDTH))
    val, gate = jnp.split(hcat, 2, axis=-1)
    return (jax.nn.silu(gate) * val) @ w_down


def _padded_len(seq):
    span = BAND * MAX_DIL
    return -(-seq // span) * span


def _to_residue_blocks(t, dil):
    bsz, lp, h, e = t.shape
    m = lp // dil
    t = t.reshape(bsz, m, dil, h, e).transpose(0, 2, 1, 3, 4)
    return t.reshape(bsz, dil, m // BAND, BAND, h, e)


def _from_residue_blocks(t, dil):
    bsz = t.shape[0]
    rest = t.shape[4:]
    t = t.reshape((bsz, dil, -1) + rest)
    t = jnp.moveaxis(t, 1, 2)
    return t.reshape((bsz, -1) + rest)


def _with_previous_block(t):
    prev = jnp.concatenate([jnp.zeros_like(t[:, :, :1]), t[:, :, :-1]], axis=2)
    return jnp.concatenate([prev, t], axis=3)


def _t5_bucket(dist):
    exact = REL_BUCKETS // 2
    d = np.maximum(dist, 1).astype(np.float32)
    large = exact + (np.log(d / exact) / math.log(REL_MAX_DIST / exact)
                     * (REL_BUCKETS - exact)).astype(np.int64)
    large = np.minimum(large, REL_BUCKETS - 1)
    return np.where(dist < exact, dist, large).astype(np.int32)


def _group_bias_mask(rel_bias, g, dil, n_blocks):
    steps = np.arange(BAND)[:, None] + BAND - np.arange(2 * BAND)[None, :]
    bucket = _t5_bucket(np.maximum(steps, 0) * dil)
    cols = rel_bias[:, g * HEADS_PER_GROUP:(g + 1) * HEADS_PER_GROUP]
    bias = jnp.transpose(cols[bucket], (2, 0, 1)).astype(jnp.float32)
    in_band = (steps >= 0) & (steps <= BAND)
    has_prev = (np.arange(n_blocks)[:, None, None] > 0) | (np.arange(2 * BAND)[None, None, :] >= BAND)
    valid = jnp.asarray(in_band[None] & has_prev)
    return bias, valid


def shared_kv(h, w_kv):
    bsz, seq, _ = h.shape
    lp = _padded_len(seq)
    kv = (h @ w_kv).astype(jnp.float32)
    kv = jnp.pad(kv, ((0, 0), (0, lp - seq), (0, 0)))
    kv = kv.reshape(bsz, lp, 2, N_DIL, HEADS_PER_GROUP, HEAD_DIM)
    blocks = []
    for g, (_, dil) in enumerate(DILATED_GROUPS):
        blocks.append(_with_previous_block(_to_residue_blocks(kv[:, :, 0, g], dil)))
        blocks.append(_with_previous_block(_to_residue_blocks(kv[:, :, 1, g], dil)))
    return blocks


def dilated_attention(h, w_q, w_out, rel_bias, kv_blocks):
    bsz, seq, _ = h.shape
    lp = _padded_len(seq)
    q = (h @ w_q).astype(jnp.float32) * (HEAD_DIM ** -0.5)
    q = jnp.pad(q, ((0, 0), (0, lp - seq), (0, 0))).reshape(bsz, lp, N_DIL, HEADS_PER_GROUP, HEAD_DIM)
    outs, lses = [], []
    for g, (_, dil) in enumerate(DILATED_GROUPS):
        kb, vb = kv_blocks[2 * g], kv_blocks[2 * g + 1]
        qb = _to_residue_blocks(q[:, :, g], dil)
        bias, valid = _group_bias_mask(rel_bias, g, dil, qb.shape[2])
        s = jnp.einsum('brnqhe,brnkhe->brnhqk', qb, kb) + bias
        s = jnp.where(valid[:, None], s, NEG_BIG)
        lse = jax.nn.logsumexp(s, axis=-1)
        p = jnp.exp(s - lse[..., None])
        o = jnp.einsum('brnhqk,brnkhe->brnqhe', p, vb)
        outs.append(_from_residue_blocks(o, dil)[:, :seq])
        lses.append(_from_residue_blocks(jnp.swapaxes(lse, -1, -2), dil)[:, :seq])
    wts = jax.nn.softmax(jnp.stack(lses), axis=0)
    o = jnp.einsum('gblh,gblhe->blhe', wts, jnp.stack(outs))
    return o.reshape(bsz, seq, MERGED_WIDTH).astype(h.dtype) @ w_out


def _fwd_setup_inputs(seed: int = 0) -> dict:
    key = jax.random.key(seed)
    ks = jax.random.split(key, 24)
    f32 = jnp.float32
    na, nbl, g, p, gs = N_A_LAYERS, N_B_LAYERS, SSM_GROUPS, SSM_STATE, SSM_GROUP

    def nrm(k, shape, scale):
        return jax.random.normal(k, shape, f32) * scale

    x = nrm(ks[0], (BATCH, SEQ, D_MODEL), 1.0)
    s5_lam_re = -0.5 + nrm(ks[1], (na, g, p), 0.01)
    s5_lam_im = math.pi * jnp.arange(p, dtype=f32) + nrm(ks[2], (na, g, p), 0.01)
    s5_log_dt = jax.random.uniform(ks[3], (na, g), f32, math.log(DT_MIN), math.log(DT_MAX))
    s5_b_re = nrm(ks[4], (na, g, p, gs), (2.0 * gs) ** -0.5)
    s5_b_im = nrm(ks[5], (na, g, p, gs), (2.0 * gs) ** -0.5)
    s5_c_re = nrm(ks[6], (na, g, gs, p), p ** -0.5)
    s5_c_im = nrm(ks[7], (na, g, gs, p), p ** -0.5)
    s5_d = nrm(ks[8], (na, g, gs), 1.0)
    s5_w_glu = nrm(ks[9], (na, D_MODEL, D_MODEL), D_MODEL ** -0.5)
    s5_b_glu = nrm(ks[10], (na, D_MODEL), 0.01)
    s5_w_out = nrm(ks[11], (na, D_MODEL, D_MODEL), D_MODEL ** -0.5 * DN_BETA)
    w_k = nrm(ks[12], (D_MODEL, ATT_WIDTH), D_MODEL ** -0.5)
    w_v = nrm(ks[13], (D_MODEL, ATT_WIDTH), D_MODEL ** -0.5 * DN_BETA)
    attn_w_kv = jnp.concatenate([w_k, w_v], axis=1)
    attn_w_q = nrm(ks[14], (nbl, D_MODEL, ATT_WIDTH), D_MODEL ** -0.5)
    attn_w_out = nrm(ks[15], (nbl, MERGED_WIDTH, D_MODEL), MERGED_WIDTH ** -0.5 * DN_BETA)
    rel_bias = nrm(ks[16], (REL_BUCKETS, N_DIL * HEADS_PER_GROUP), 0.5)
    ffn_w_up = nrm(ks[17], (DEPTH, D_MODEL, 2 * D_FF), D_MODEL ** -0.5 * DN_BETA)
    ffn_conv_w = nrm(ks[18], (DEPTH, CONV_WIDTH, 2 * D_FF), CONV_WIDTH ** -0.5)
    ffn_conv_b = nrm(ks[19], (DEPTH, 2 * D_FF), 0.01)
    ffn_w_down = nrm(ks[20], (DEPTH, D_FF, D_MODEL), D_FF ** -0.5 * DN_BETA)
    ln_gain = 1.0 + nrm(ks[21], (DEPTH, 2, D_MODEL), 0.01)
    ln_bias = nrm(ks[22], (DEPTH, 2, D_MODEL), 0.01)
    return {'x': x, 's5_lam_re': s5_lam_re, 's5_lam_im': s5_lam_im, 's5_log_dt': s5_log_dt,
            's5_b_re': s5_b_re, 's5_b_im': s5_b_im, 's5_c_re': s5_c_re, 's5_c_im': s5_c_im,
            's5_d': s5_d, 's5_w_glu': s5_w_glu, 's5_b_glu': s5_b_glu, 's5_w_out': s5_w_out,
            'attn_w_kv': attn_w_kv, 'attn_w_q': attn_w_q, 'attn_w_out': attn_w_out,
            'rel_bias': rel_bias, 'ffn_w_up': ffn_w_up, 'ffn_conv_w': ffn_conv_w,
            'ffn_conv_b': ffn_conv_b, 'ffn_w_down': ffn_w_down, 'ln_gain': ln_gain, 'ln_bias': ln_bias}


def _fwd_reference(x, s5_lam_re, s5_lam_im, s5_log_dt, s5_b_re, s5_b_im, s5_c_re, s5_c_im,
              s5_d, s5_w_glu, s5_b_glu, s5_w_out, attn_w_kv, attn_w_q, attn_w_out,
              rel_bias, ffn_w_up, ffn_conv_w, ffn_conv_b, ffn_w_down, ln_gain, ln_bias):
    h = x
    kv_blocks = None
    for layer in range(DEPTH):
        if layer < N_A_LAYERS:
            i = layer
            mix = s5_mixer(h, s5_lam_re[i], s5_lam_im[i], s5_log_dt[i], s5_b_re[i], s5_b_im[i],
                           s5_c_re[i], s5_c_im[i], s5_d[i], s5_w_glu[i], s5_b_glu[i], s5_w_out[i])
        else:
            j = layer - N_A_LAYERS
            mix = dilated_attention(h, attn_w_q[j], attn_w_out[j], rel_bias, kv_blocks)
        h = post_norm(h, mix, ln_gain[layer, 0], ln_bias[layer, 0])
        ffn = conv_glu_ffn(h, ffn_w_up[layer], ffn_conv_w[layer], ffn_conv_b[layer], ffn_w_down[layer])
        h = post_norm(h, ffn, ln_gain[layer, 1], ln_bias[layer, 1])
        if layer == N_A_LAYERS - 1:
            kv_blocks = shared_kv(h, attn_w_kv)
    return h


import jax as _jax
import jax.numpy as _jnp

TWIN_FORMAT = 'train_step'
FWD_PARAMS = ['x', 's5_lam_re', 's5_lam_im', 's5_log_dt', 's5_b_re', 's5_b_im', 's5_c_re', 's5_c_im', 's5_d', 's5_w_glu', 's5_b_glu', 's5_w_out', 'attn_w_kv', 'attn_w_q', 'attn_w_out', 'rel_bias', 'ffn_w_up', 'ffn_conv_w', 'ffn_conv_b', 'ffn_w_down', 'ln_gain', 'ln_bias']
TWIN_WEIGHTS = ['s5_lam_re', 's5_lam_im', 's5_log_dt', 's5_b_re', 's5_b_im', 's5_c_re', 's5_c_im', 's5_d', 's5_w_glu', 's5_b_glu', 's5_w_out', 'attn_w_kv', 'attn_w_q', 'attn_w_out', 'rel_bias', 'ffn_w_up', 'ffn_conv_w', 'ffn_conv_b', 'ffn_w_down', 'ln_gain', 'ln_bias']
TWIN_DIFF_INPUT = 'x'
TWIN_INPUTS = ['x', 's5_lam_re', 's5_lam_im', 's5_log_dt', 's5_b_re', 's5_b_im', 's5_c_re', 's5_c_im', 's5_d', 's5_w_glu', 's5_b_glu', 's5_w_out', 'attn_w_kv', 'attn_w_q', 'attn_w_out', 'rel_bias', 'ffn_w_up', 'ffn_conv_w', 'ffn_conv_b', 'ffn_w_down', 'ln_gain', 'ln_bias', 'loss_target', 'm_s5_lam_re', 'm_s5_lam_im', 'm_s5_log_dt', 'm_s5_b_re', 'm_s5_b_im', 'm_s5_c_re', 'm_s5_c_im', 'm_s5_d', 'm_s5_w_glu', 'm_s5_b_glu', 'm_s5_w_out', 'm_attn_w_kv', 'm_attn_w_q', 'm_attn_w_out', 'm_rel_bias', 'm_ffn_w_up', 'm_ffn_conv_w', 'm_ffn_conv_b', 'm_ffn_w_down', 'm_ln_gain', 'm_ln_bias', 'v_s5_lam_re', 'v_s5_lam_im', 'v_s5_log_dt', 'v_s5_b_re', 'v_s5_b_im', 'v_s5_c_re', 'v_s5_c_im', 'v_s5_d', 'v_s5_w_glu', 'v_s5_b_glu', 'v_s5_w_out', 'v_attn_w_kv', 'v_attn_w_q', 'v_attn_w_out', 'v_rel_bias', 'v_ffn_w_up', 'v_ffn_conv_w', 'v_ffn_conv_b', 'v_ffn_w_down', 'v_ln_gain', 'v_ln_bias']
TWIN_OUTPUTS = ['loss', 'grad_x', 'grad_s5_lam_re', 'grad_s5_lam_im', 'grad_s5_log_dt', 'grad_s5_b_re', 'grad_s5_b_im', 'grad_s5_c_re', 'grad_s5_c_im', 'grad_s5_d', 'grad_s5_w_glu', 'grad_s5_b_glu', 'grad_s5_w_out', 'grad_attn_w_kv', 'grad_attn_w_q', 'grad_attn_w_out', 'grad_rel_bias', 'grad_ffn_w_up', 'grad_ffn_conv_w', 'grad_ffn_conv_b', 'grad_ffn_w_down', 'grad_ln_gain', 'grad_ln_bias', 'delta_s5_lam_re', 'delta_s5_lam_im', 'delta_s5_log_dt', 'delta_s5_b_re', 'delta_s5_b_im', 'delta_s5_c_re', 'delta_s5_c_im', 'delta_s5_d', 'delta_s5_w_glu', 'delta_s5_b_glu', 'delta_s5_w_out', 'delta_attn_w_kv', 'delta_attn_w_q', 'delta_attn_w_out', 'delta_rel_bias', 'delta_ffn_w_up', 'delta_ffn_conv_w', 'delta_ffn_conv_b', 'delta_ffn_w_down', 'delta_ln_gain', 'delta_ln_bias', 'new_m_s5_lam_re', 'new_m_s5_lam_im', 'new_m_s5_log_dt', 'new_m_s5_b_re', 'new_m_s5_b_im', 'new_m_s5_c_re', 'new_m_s5_c_im', 'new_m_s5_d', 'new_m_s5_w_glu', 'new_m_s5_b_glu', 'new_m_s5_w_out', 'new_m_attn_w_kv', 'new_m_attn_w_q', 'new_m_attn_w_out', 'new_m_rel_bias', 'new_m_ffn_w_up', 'new_m_ffn_conv_w', 'new_m_ffn_conv_b', 'new_m_ffn_w_down', 'new_m_ln_gain', 'new_m_ln_bias', 'new_v_s5_lam_re', 'new_v_s5_lam_im', 'new_v_s5_log_dt', 'new_v_s5_b_re', 'new_v_s5_b_im', 'new_v_s5_c_re', 'new_v_s5_c_im', 'new_v_s5_d', 'new_v_s5_w_glu', 'new_v_s5_b_glu', 'new_v_s5_w_out', 'new_v_attn_w_kv', 'new_v_attn_w_q', 'new_v_attn_w_out', 'new_v_rel_bias', 'new_v_ffn_w_up', 'new_v_ffn_conv_w', 'new_v_ffn_conv_b', 'new_v_ffn_w_down', 'new_v_ln_gain', 'new_v_ln_bias']
TWIN_LEAF_KINDS = {'loss': 'loss', 'grad_x': 'grad_x', 'grad_s5_lam_re': 'grad_w', 'grad_s5_lam_im': 'grad_w', 'grad_s5_log_dt': 'grad_w', 'grad_s5_b_re': 'grad_w', 'grad_s5_b_im': 'grad_w', 'grad_s5_c_re': 'grad_w', 'grad_s5_c_im': 'grad_w', 'grad_s5_d': 'grad_w', 'grad_s5_w_glu': 'grad_w', 'grad_s5_b_glu': 'grad_w', 'grad_s5_w_out': 'grad_w', 'grad_attn_w_kv': 'grad_w', 'grad_attn_w_q': 'grad_w', 'grad_attn_w_out': 'grad_w', 'grad_rel_bias': 'grad_w', 'grad_ffn_w_up': 'grad_w', 'grad_ffn_conv_w': 'grad_w', 'grad_ffn_conv_b': 'grad_w', 'grad_ffn_w_down': 'grad_w', 'grad_ln_gain': 'grad_w', 'grad_ln_bias': 'grad_w', 'delta_s5_lam_re': 'delta_w', 'delta_s5_lam_im': 'delta_w', 'delta_s5_log_dt': 'delta_w', 'delta_s5_b_re': 'delta_w', 'delta_s5_b_im': 'delta_w', 'delta_s5_c_re': 'delta_w', 'delta_s5_c_im': 'delta_w', 'delta_s5_d': 'delta_w', 'delta_s5_w_glu': 'delta_w', 'delta_s5_b_glu': 'delta_w', 'delta_s5_w_out': 'delta_w', 'delta_attn_w_kv': 'delta_w', 'delta_attn_w_q': 'delta_w', 'delta_attn_w_out': 'delta_w', 'delta_rel_bias': 'delta_w', 'delta_ffn_w_up': 'delta_w', 'delta_ffn_conv_w': 'delta_w', 'delta_ffn_conv_b': 'delta_w', 'delta_ffn_w_down': 'delta_w', 'delta_ln_gain': 'delta_w', 'delta_ln_bias': 'delta_w', 'new_m_s5_lam_re': 'new_m', 'new_m_s5_lam_im': 'new_m', 'new_m_s5_log_dt': 'new_m', 'new_m_s5_b_re': 'new_m', 'new_m_s5_b_im': 'new_m', 'new_m_s5_c_re': 'new_m', 'new_m_s5_c_im': 'new_m', 'new_m_s5_d': 'new_m', 'new_m_s5_w_glu': 'new_m', 'new_m_s5_b_glu': 'new_m', 'new_m_s5_w_out': 'new_m', 'new_m_attn_w_kv': 'new_m', 'new_m_attn_w_q': 'new_m', 'new_m_attn_w_out': 'new_m', 'new_m_rel_bias': 'new_m', 'new_m_ffn_w_up': 'new_m', 'new_m_ffn_conv_w': 'new_m', 'new_m_ffn_conv_b': 'new_m', 'new_m_ffn_w_down': 'new_m', 'new_m_ln_gain': 'new_m', 'new_m_ln_bias': 'new_m', 'new_v_s5_lam_re': 'new_v', 'new_v_s5_lam_im': 'new_v', 'new_v_s5_log_dt': 'new_v', 'new_v_s5_b_re': 'new_v', 'new_v_s5_b_im': 'new_v', 'new_v_s5_c_re': 'new_v', 'new_v_s5_c_im': 'new_v', 'new_v_s5_d': 'new_v', 'new_v_s5_w_glu': 'new_v', 'new_v_s5_b_glu': 'new_v', 'new_v_s5_w_out': 'new_v', 'new_v_attn_w_kv': 'new_v', 'new_v_attn_w_q': 'new_v', 'new_v_attn_w_out': 'new_v', 'new_v_rel_bias': 'new_v', 'new_v_ffn_w_up': 'new_v', 'new_v_ffn_conv_w': 'new_v', 'new_v_ffn_conv_b': 'new_v', 'new_v_ffn_w_down': 'new_v', 'new_v_ln_gain': 'new_v', 'new_v_ln_bias': 'new_v'}


def _forward(args):
    return _fwd_reference(*[args[k] for k in FWD_PARAMS])


def _output_shape():
    out = _jax.eval_shape(lambda: _forward(_fwd_setup_inputs(0)))
    return out.shape, out.dtype

N_MICROBATCH = 1
ADAM_LR = 0.001
ADAM_B1 = 0.9
ADAM_B2 = 0.999
ADAM_EPS = 1e-08
ADAM_WD = 0.01
ADAM_STEP = 10
PER_EXAMPLE_BATCH_AXIS = {'x': 0, 'loss_target': 0}
SHARED_INPUTS = []
_WEIGHT_DTYPES = {'s5_lam_re': _jnp.float32, 's5_lam_im': _jnp.float32, 's5_log_dt': _jnp.float32, 's5_b_re': _jnp.float32, 's5_b_im': _jnp.float32, 's5_c_re': _jnp.float32, 's5_c_im': _jnp.float32, 's5_d': _jnp.float32, 's5_w_glu': _jnp.float32, 's5_b_glu': _jnp.float32, 's5_w_out': _jnp.float32, 'attn_w_kv': _jnp.float32, 'attn_w_q': _jnp.float32, 'attn_w_out': _jnp.float32, 'rel_bias': _jnp.float32, 'ffn_w_up': _jnp.float32, 'ffn_conv_w': _jnp.float32, 'ffn_conv_b': _jnp.float32, 'ffn_w_down': _jnp.float32, 'ln_gain': _jnp.float32, 'ln_bias': _jnp.float32}
MOMENT_SCALE = {'s5_lam_re': 1.765542e-03, 's5_lam_im': 1.923623e-03, 's5_log_dt': 1.635434e+00, 's5_b_re': 1.079067e-03, 's5_b_im': 1.101834e-03, 's5_c_re': 1.548686e-03, 's5_c_im': 1.533539e-03, 's5_d': 2.832937e-02, 's5_w_glu': 6.412817e-03, 's5_b_glu': 1.162913e-02, 's5_w_out': 4.560649e-02, 'attn_w_kv': 5.819256e-03, 'attn_w_q': 3.200761e-03, 'attn_w_out': 1.308951e-02, 'rel_bias': 4.103565e-03, 'ffn_w_up': 1.033675e-02, 'ffn_conv_w': 5.246643e-03, 'ffn_conv_b': 1.027031e-02, 'ffn_w_down': 1.703920e-02, 'ln_gain': 1.599065e+01, 'ln_bias': 7.800077e-01}


def _to_microbatches(a, axis):
    t = _jnp.moveaxis(a, axis, 0)
    t = t.reshape((N_MICROBATCH, t.shape[0] // N_MICROBATCH) + t.shape[1:])
    return _jnp.moveaxis(t, 1, axis + 1)


def setup_inputs(seed: int = 0) -> dict:
    inp = _fwd_setup_inputs(seed)
    key = _jax.random.fold_in(_jax.random.key(seed), 7919)
    shape, _ = _output_shape()
    out = dict(inp)
    out["loss_target"] = _jax.random.normal(_jax.random.fold_in(key, 0), shape, _jnp.float32)
    for i, name in enumerate(TWIN_WEIGHTS):
        w = inp[name].astype(_jnp.float32)
        if MOMENT_SCALE is None:
            s = _jnp.sqrt(_jnp.mean(_jnp.square(w)) + 1e-30)
        else:
            s = MOMENT_SCALE[name]
        km, kv = _jax.random.split(_jax.random.fold_in(key, i + 1))
        out[name] = w
        out["m_" + name] = s * _jax.random.normal(km, w.shape, _jnp.float32)
        out["v_" + name] = (s * s) * _jax.random.uniform(kv, w.shape, _jnp.float32, 0.5, 1.5)
    if N_MICROBATCH > 1:
        for name, axis in PER_EXAMPLE_BATCH_AXIS.items():
            out[name] = _to_microbatches(out[name], axis)
    return {'x': out['x'], 's5_lam_re': out['s5_lam_re'], 's5_lam_im': out['s5_lam_im'], 's5_log_dt': out['s5_log_dt'], 's5_b_re': out['s5_b_re'], 's5_b_im': out['s5_b_im'], 's5_c_re': out['s5_c_re'], 's5_c_im': out['s5_c_im'], 's5_d': out['s5_d'], 's5_w_glu': out['s5_w_glu'], 's5_b_glu': out['s5_b_glu'], 's5_w_out': out['s5_w_out'], 'attn_w_kv': out['attn_w_kv'], 'attn_w_q': out['attn_w_q'], 'attn_w_out': out['attn_w_out'], 'rel_bias': out['rel_bias'], 'ffn_w_up': out['ffn_w_up'], 'ffn_conv_w': out['ffn_conv_w'], 'ffn_conv_b': out['ffn_conv_b'], 'ffn_w_down': out['ffn_w_down'], 'ln_gain': out['ln_gain'], 'ln_bias': out['ln_bias'], 'loss_target': out['loss_target'], 'm_s5_lam_re': out['m_s5_lam_re'], 'm_s5_lam_im': out['m_s5_lam_im'], 'm_s5_log_dt': out['m_s5_log_dt'], 'm_s5_b_re': out['m_s5_b_re'], 'm_s5_b_im': out['m_s5_b_im'], 'm_s5_c_re': out['m_s5_c_re'], 'm_s5_c_im': out['m_s5_c_im'], 'm_s5_d': out['m_s5_d'], 'm_s5_w_glu': out['m_s5_w_glu'], 'm_s5_b_glu': out['m_s5_b_glu'], 'm_s5_w_out': out['m_s5_w_out'], 'm_attn_w_kv': out['m_attn_w_kv'], 'm_attn_w_q': out['m_attn_w_q'], 'm_attn_w_out': out['m_attn_w_out'], 'm_rel_bias': out['m_rel_bias'], 'm_ffn_w_up': out['m_ffn_w_up'], 'm_ffn_conv_w': out['m_ffn_conv_w'], 'm_ffn_conv_b': out['m_ffn_conv_b'], 'm_ffn_w_down': out['m_ffn_w_down'], 'm_ln_gain': out['m_ln_gain'], 'm_ln_bias': out['m_ln_bias'], 'v_s5_lam_re': out['v_s5_lam_re'], 'v_s5_lam_im': out['v_s5_lam_im'], 'v_s5_log_dt': out['v_s5_log_dt'], 'v_s5_b_re': out['v_s5_b_re'], 'v_s5_b_im': out['v_s5_b_im'], 'v_s5_c_re': out['v_s5_c_re'], 'v_s5_c_im': out['v_s5_c_im'], 'v_s5_d': out['v_s5_d'], 'v_s5_w_glu': out['v_s5_w_glu'], 'v_s5_b_glu': out['v_s5_b_glu'], 'v_s5_w_out': out['v_s5_w_out'], 'v_attn_w_kv': out['v_attn_w_kv'], 'v_attn_w_q': out['v_attn_w_q'], 'v_attn_w_out': out['v_attn_w_out'], 'v_rel_bias': out['v_rel_bias'], 'v_ffn_w_up': out['v_ffn_w_up'], 'v_ffn_conv_w': out['v_ffn_conv_w'], 'v_ffn_conv_b': out['v_ffn_conv_b'], 'v_ffn_w_down': out['v_ffn_w_down'], 'v_ln_gain': out['v_ln_gain'], 'v_ln_bias': out['v_ln_bias']}


def _loss(weights, diff, rest, loss_target):
    with _jax.named_scope("forward"):
        args = {**rest, TWIN_DIFF_INPUT: diff, **{k: w.astype(_WEIGHT_DTYPES[k]) for k, w in weights.items()}}
        y = _forward(args)
    with _jax.named_scope("loss_head"):
        err = _jnp.square(y.astype(_jnp.float32) - loss_target)
        return 0.5 * _jnp.sum(_jnp.mean(err, axis=-1)) if err.ndim else 0.5 * err


def _adamw(w, g, m, v):
    m = ADAM_B1 * m + (1.0 - ADAM_B1) * g
    v = ADAM_B2 * v + (1.0 - ADAM_B2) * _jnp.square(g)
    m_hat = m / (1.0 - ADAM_B1 ** ADAM_STEP)
    v_hat = v / (1.0 - ADAM_B2 ** ADAM_STEP)
    delta = -ADAM_LR * (m_hat / (_jnp.sqrt(v_hat) + ADAM_EPS) + ADAM_WD * w)
    return delta, m, v


def reference(x, s5_lam_re, s5_lam_im, s5_log_dt, s5_b_re, s5_b_im, s5_c_re, s5_c_im, s5_d, s5_w_glu, s5_b_glu, s5_w_out, attn_w_kv, attn_w_q, attn_w_out, rel_bias, ffn_w_up, ffn_conv_w, ffn_conv_b, ffn_w_down, ln_gain, ln_bias, loss_target, m_s5_lam_re, m_s5_lam_im, m_s5_log_dt, m_s5_b_re, m_s5_b_im, m_s5_c_re, m_s5_c_im, m_s5_d, m_s5_w_glu, m_s5_b_glu, m_s5_w_out, m_attn_w_kv, m_attn_w_q, m_attn_w_out, m_rel_bias, m_ffn_w_up, m_ffn_conv_w, m_ffn_conv_b, m_ffn_w_down, m_ln_gain, m_ln_bias, v_s5_lam_re, v_s5_lam_im, v_s5_log_dt, v_s5_b_re, v_s5_b_im, v_s5_c_re, v_s5_c_im, v_s5_d, v_s5_w_glu, v_s5_b_glu, v_s5_w_out, v_attn_w_kv, v_attn_w_q, v_attn_w_out, v_rel_bias, v_ffn_w_up, v_ffn_conv_w, v_ffn_conv_b, v_ffn_w_down, v_ln_gain, v_ln_bias):
    given = dict(x=x, s5_lam_re=s5_lam_re, s5_lam_im=s5_lam_im, s5_log_dt=s5_log_dt, s5_b_re=s5_b_re, s5_b_im=s5_b_im, s5_c_re=s5_c_re, s5_c_im=s5_c_im, s5_d=s5_d, s5_w_glu=s5_w_glu, s5_b_glu=s5_b_glu, s5_w_out=s5_w_out, attn_w_kv=attn_w_kv, attn_w_q=attn_w_q, attn_w_out=attn_w_out, rel_bias=rel_bias, ffn_w_up=ffn_w_up, ffn_conv_w=ffn_conv_w, ffn_conv_b=ffn_conv_b, ffn_w_down=ffn_w_down, ln_gain=ln_gain, ln_bias=ln_bias, loss_target=loss_target, m_s5_lam_re=m_s5_lam_re, m_s5_lam_im=m_s5_lam_im, m_s5_log_dt=m_s5_log_dt, m_s5_b_re=m_s5_b_re, m_s5_b_im=m_s5_b_im, m_s5_c_re=m_s5_c_re, m_s5_c_im=m_s5_c_im, m_s5_d=m_s5_d, m_s5_w_glu=m_s5_w_glu, m_s5_b_glu=m_s5_b_glu, m_s5_w_out=m_s5_w_out, m_attn_w_kv=m_attn_w_kv, m_attn_w_q=m_attn_w_q, m_attn_w_out=m_attn_w_out, m_rel_bias=m_rel_bias, m_ffn_w_up=m_ffn_w_up, m_ffn_conv_w=m_ffn_conv_w, m_ffn_conv_b=m_ffn_conv_b, m_ffn_w_down=m_ffn_w_down, m_ln_gain=m_ln_gain, m_ln_bias=m_ln_bias, v_s5_lam_re=v_s5_lam_re, v_s5_lam_im=v_s5_lam_im, v_s5_log_dt=v_s5_log_dt, v_s5_b_re=v_s5_b_re, v_s5_b_im=v_s5_b_im, v_s5_c_re=v_s5_c_re, v_s5_c_im=v_s5_c_im, v_s5_d=v_s5_d, v_s5_w_glu=v_s5_w_glu, v_s5_b_glu=v_s5_b_glu, v_s5_w_out=v_s5_w_out, v_attn_w_kv=v_attn_w_kv, v_attn_w_q=v_attn_w_q, v_attn_w_out=v_attn_w_out, v_rel_bias=v_rel_bias, v_ffn_w_up=v_ffn_w_up, v_ffn_conv_w=v_ffn_conv_w, v_ffn_conv_b=v_ffn_conv_b, v_ffn_w_down=v_ffn_w_down, v_ln_gain=v_ln_gain, v_ln_bias=v_ln_bias)
    weights = {n: given[n] for n in TWIN_WEIGHTS}
    shared = {n: given[n] for n in SHARED_INPUTS}
    per_example = {n: given[n] for n in ['x']}
    grad_fn = _jax.value_and_grad(_loss, argnums=(0, 1))

    def one_microbatch(ex, loss_target):
        ex = dict(ex)
        diff = ex.pop(TWIN_DIFF_INPUT)
        return grad_fn(weights, diff, {**shared, **ex}, loss_target)

    if N_MICROBATCH == 1:
        loss, (grad_w, grad_x) = one_microbatch(per_example, given["loss_target"])
    else:
        def body(carry, xs):
            loss_sum, grad_sum = carry
            l_k, (gw_k, gx_k) = one_microbatch(xs[0], xs[1])
            with _jax.named_scope("update"):
                return (loss_sum + l_k, _jax.tree.map(_jnp.add, grad_sum, gw_k)), gx_k

        init = (_jnp.zeros((), _jnp.float32), _jax.tree.map(_jnp.zeros_like, weights))
        (loss, grad_w), grad_x = _jax.lax.scan(body, init, (per_example, given["loss_target"]))
    with _jax.named_scope("update"):
        delta_w, new_m, new_v = {}, {}, {}
        for n in TWIN_WEIGHTS:
            delta_w[n], new_m[n], new_v[n] = _adamw(weights[n], grad_w[n], given["m_" + n], given["v_" + n])
    return (loss, grad_x, *[grad_w[n] for n in TWIN_WEIGHTS], *[delta_w[n] for n in TWIN_WEIGHTS],
            *[new_m[n] for n in TWIN_WEIGHTS], *[new_v[n] for n in TWIN_WEIGHTS])
```

```python
import functools
import math

import numpy as np
import jax
import jax.numpy as jnp
from jax import lax
from jax.experimental import pallas as pl
from jax.experimental.pallas import tpu as pltpu

F32 = jnp.float32
BF16 = jnp.bfloat16
HIGHEST = lax.Precision.HIGHEST

N_DEV = 8
D_MODEL = 1024
SSM_GROUP = 16
SSM_STATE = 64
SSM_CHUNKS = 8
CHUNK_CH = D_MODEL // SSM_CHUNKS
CHUNK_ST = CHUNK_CH // SSM_GROUP * SSM_STATE
N_STATE = D_MODEL // SSM_GROUP * SSM_STATE
HEAD_DIM = 64
N_HEADS = 16
BAND = 128
DILATIONS = (1, 4, 16)
REL_BUCKETS = 32
REL_MAX_DIST = 2048
NEG_BIG = -1e30
DN_ALPHA = (2.0 * 2) ** 0.25
LN_EPS = 1e-5
ADAM_LR = 0.001
ADAM_B1 = 0.9
ADAM_B2 = 0.999
ADAM_EPS = 1e-08
ADAM_WD = 0.01
ADAM_STEP = 10

TM = 512
T_SCAN = 256
VMEM_LIMIT = 56 * 1024 * 1024

NN = (((1,), (0,)), ((), ()))
NT = (((1,), (1,)), ((), ()))
TN = (((0,), (0,)), ((), ()))


def _call(body, name, grid, in_specs, out_specs, out_shape, scratch=()):
    return pl.pallas_call(
        body, name=name, grid=grid, in_specs=in_specs, out_specs=out_specs, out_shape=out_shape,
        scratch_shapes=list(scratch),
        compiler_params=pltpu.CompilerParams(
            dimension_semantics=("arbitrary",) * len(grid), vmem_limit_bytes=VMEM_LIMIT),
    )


def _sds(shape, dtype):
    return jax.ShapeDtypeStruct(shape, dtype)


def _mm(name, a, b, *, dims, grid, a_spec, b_spec, o_spec, out_shape, nred=1, red_axis=0,
        add=None, add_spec=None, add_scale=1.0):
    has_add = add is not None
    acc_shape = tuple(s for s in o_spec.block_shape if s is not None)

    def body(*refs):
        if has_add:
            a_ref, b_ref, add_ref, o_ref = refs[:4]
            rest = refs[4:]
        else:
            a_ref, b_ref, o_ref = refs[:3]
            rest = refs[3:]
        prod = lax.dot_general(a_ref[...].astype(BF16), b_ref[...].astype(BF16), dims,
                               preferred_element_type=F32)

        def finish(val):
            if has_add:
                val = val + add_scale * add_ref[...].astype(F32)
            o_ref[...] = val.astype(o_ref.dtype)

        if nred == 1:
            finish(prod)
        else:
            acc = rest[0]
            k = pl.program_id(red_axis)

            @pl.when(k == 0)
            def _():
                acc[...] = prod

            @pl.when(k > 0)
            def _():
                acc[...] += prod

            @pl.when(k == nred - 1)
            def _():
                finish(acc[...])

    in_specs = [a_spec, b_spec] + ([add_spec] if has_add else [])
    scratch = [pltpu.VMEM(acc_shape, F32)] if nred > 1 else []
    args = (a, b) + ((add,) if has_add else ())
    return _call(body, name, grid, in_specs, o_spec, out_shape, scratch)(*args)


def _mm_nn(name, a, b, out_dtype, blocked_out=False):
    m, k = a.shape
    nb, _, n = b.shape
    if blocked_out:
        o_spec = pl.BlockSpec((None, TM, n), lambda d, i: (d, i, 0))
        out_shape = _sds((nb, m, n), out_dtype)
    else:
        o_spec = pl.BlockSpec((TM, n), lambda d, i: (i, d))
        out_shape = _sds((m, nb * n), out_dtype)
    return _mm(name, a, b, dims=NN, grid=(nb, m // TM),
               a_spec=pl.BlockSpec((TM, k), lambda d, i: (i, 0)),
               b_spec=pl.BlockSpec((None, k, n), lambda d, i: (d, 0, 0)),
               o_spec=o_spec, out_shape=out_shape)


def _mm_nn_red(name, a, b, out_dtype):
    nb, m, k = a.shape
    n = b.shape[2]
    return _mm(name, a, b, dims=NN, grid=(m // TM, nb), nred=nb, red_axis=1,
               a_spec=pl.BlockSpec((None, TM, k), lambda i, d: (d, i, 0)),
               b_spec=pl.BlockSpec((None, k, n), lambda i, d: (d, 0, 0)),
               o_spec=pl.BlockSpec((TM, n), lambda i, d: (i, 0)),
               out_shape=_sds((m, n), out_dtype))


def _mm_tn(name, a, g, nb, out_dtype, a_blocked=False, g_mode="cols"):
    if a_blocked:
        _, m, ka = a.shape
        a_spec = pl.BlockSpec((None, TM, ka), lambda d, i: (d, i, 0))
    else:
        m, ka = a.shape
        a_spec = pl.BlockSpec((TM, ka), lambda d, i: (i, 0))
    if g_mode == "cols":
        n = g.shape[1] // nb
        g_spec = pl.BlockSpec((TM, n), lambda d, i: (i, d))
    elif g_mode == "blocked":
        n = g.shape[2]
        g_spec = pl.BlockSpec((None, TM, n), lambda d, i: (d, i, 0))
    else:
        n = g.shape[1]
        g_spec = pl.BlockSpec((TM, n), lambda d, i: (i, 0))
    nred = m // TM
    return _mm(name, a, g, dims=TN, grid=(nb, nred), nred=nred, red_axis=1,
               a_spec=a_spec, b_spec=g_spec,
               o_spec=pl.BlockSpec((None, ka, n), lambda d, i: (d, 0, 0)),
               out_shape=_sds((nb, ka, n), out_dtype))


def _mm_nt_red(name, g, w, out_dtype, g_blocked, add=None, add_scale=1.0):
    nb, k, n = w.shape
    if g_blocked:
        m = g.shape[1]
        g_spec = pl.BlockSpec((None, TM, n), lambda i, d: (d, i, 0))
    else:
        m = g.shape[0]
        g_spec = pl.BlockSpec((TM, n), lambda i, d: (i, d))
    o_spec = pl.BlockSpec((TM, k), lambda i, d: (i, 0))
    return _mm(name, g, w, dims=NT, grid=(m // TM, nb), nred=nb, red_axis=1,
               a_spec=g_spec, b_spec=pl.BlockSpec((None, k, n), lambda i, d: (d, 0, 0)),
               o_spec=o_spec, out_shape=_sds((m, k), out_dtype),
               add=add, add_spec=o_spec if add is not None else None, add_scale=add_scale)


def _mm_nt_out(name, g, w, out_dtype):
    m, n = g.shape
    nb, k, _ = w.shape
    return _mm(name, g, w, dims=NT, grid=(nb, m // TM),
               a_spec=pl.BlockSpec((TM, n), lambda d, i: (i, 0)),
               b_spec=pl.BlockSpec((None, k, n), lambda d, i: (d, 0, 0)),
               o_spec=pl.BlockSpec((None, TM, k), lambda d, i: (d, i, 0)),
               out_shape=_sds((nb, m, k), out_dtype))


def _ln_stats(x, f):
    r = DN_ALPHA * x + f
    mu = jnp.mean(r, axis=-1, keepdims=True)
    xc = r - mu
    var = jnp.mean(xc * xc, axis=-1, keepdims=True)
    rstd = lax.rsqrt(var + LN_EPS)
    return xc * rstd, rstd


def _ln_fwd(name, xin, f, gain, bias):
    n, d = xin.shape

    def body(x_ref, f_ref, g_ref, b_ref, o_ref):
        xhat, _ = _ln_stats(x_ref[...], f_ref[...])
        o_ref[...] = xhat * g_ref[...] + b_ref[...]

    row = pl.BlockSpec((TM, d), lambda i: (i, 0))
    vec = pl.BlockSpec((1, d), lambda i: (0, 0))
    return _call(body, name, (n // TM,), [row, row, vec, vec], row, _sds((n, d), F32))(xin, f, gain, bias)


def _ln_bwd(name, xin, f, dy, gain):
    n, d = xin.shape

    def body(x_ref, f_ref, dy_ref, g_ref, dr_ref, dg_ref, db_ref):
        i = pl.program_id(0)
        xhat, rstd = _ln_stats(x_ref[...], f_ref[...])
        dy_v = dy_ref[...]
        dxh = dy_v * g_ref[...]
        m1 = jnp.mean(dxh, axis=-1, keepdims=True)
        m2 = jnp.mean(dxh * xhat, axis=-1, keepdims=True)
        dr_ref[...] = rstd * (dxh - m1 - xhat * m2)
        dg = jnp.sum(dy_v * xhat, axis=0, keepdims=True)
        db = jnp.sum(dy_v, axis=0, keepdims=True)

        @pl.when(i == 0)
        def _():
            dg_ref[...] = dg
            db_ref[...] = db

        @pl.when(i > 0)
        def _():
            dg_ref[...] += dg
            db_ref[...] += db

    row = pl.BlockSpec((TM, d), lambda i: (i, 0))
    vec = pl.BlockSpec((1, d), lambda i: (0, 0))
    return _call(body, name, (n // TM,), [row, row, row, vec], [row, vec, vec],
                 [_sds((n, d), F32), _sds((1, d), F32), _sds((1, d), F32)])(xin, f, dy, gain)


def _loss_head(y, target):
    n, d = y.shape

    def body(y_ref, t_ref, l_ref, dy_ref):
        i = pl.program_id(0)
        e = y_ref[...] - t_ref[...]
        dy_ref[...] = e * (1.0 / d)
        part = jnp.sum(jnp.sum(e * e, axis=1, keepdims=True), axis=0, keepdims=True) * (0.5 / d)
        part = jnp.broadcast_to(part, (8, 128))

        @pl.when(i == 0)
        def _():
            l_ref[...] = part

        @pl.when(i > 0)
        def _():
            l_ref[...] += part

    row = pl.BlockSpec((TM, d), lambda i: (i, 0))
    return _call(body, "loss_head", (n // TM,), [row, row],
                 [pl.BlockSpec((8, 128), lambda i: (0, 0)), row],
                 [_sds((8, 128), F32), _sds((n, d), F32)])(y, target)


def _group_expand_matrix():
    e = np.zeros((D_MODEL // SSM_GROUP, N_STATE), np.float32)
    for g in range(D_MODEL // SSM_GROUP):
        e[g, g * SSM_STATE:(g + 1) * SSM_STATE] = 1.0
    return jnp.asarray(e)


def _discretise(lr, li, ldt, br, bi, expand):
    dt = jnp.dot(jnp.exp(ldt), expand, precision=HIGHEST, preferred_element_type=F32)
    mag = jnp.exp(lr * dt)
    th = li * dt
    ab_r = mag * jnp.cos(th)
    ab_i = mag * jnp.sin(th)
    den = lr * lr + li * li
    nr = ab_r - 1.0
    co_r = (nr * lr + ab_i * li) / den
    co_i = (ab_i * lr - nr * li) / den
    bb_r = co_r * br - co_i * bi
    bb_i = co_r * bi + co_i * br
    return ab_r, ab_i, bb_r, bb_i


def _cmul(ar, ai, br, bi):
    return ar * br - ai * bi, ar * bi + ai * br


def _s5_prep(lr, li, ldt, br, bi):
    s = N_STATE
    expand = _group_expand_matrix()

    def body(lr_ref, li_ref, ldt_ref, br_ref, bi_ref, e_ref, abr_ref, abi_ref, bbr_ref, bbi_ref, tab_ref):
        ab_r, ab_i, bb_r, bb_i = _discretise(lr_ref[...], li_ref[...], ldt_ref[...], br_ref[...],
                                             bi_ref[...], e_ref[...])
        abr_ref[...] = ab_r
        abi_ref[...] = ab_i
        bbr_ref[...] = bb_r
        bbi_ref[...] = bb_i
        row = lax.broadcasted_iota(jnp.int32, (8, s), 0)
        for base, sign in ((0, 1.0), (8, -1.0)):
            p = [None] * 9
            p[1] = (ab_r, sign * ab_i)
            p[2] = _cmul(*p[1], *p[1])
            p[3] = _cmul(*p[2], *p[1])
            p[4] = _cmul(*p[2], *p[2])
            p[5] = _cmul(*p[4], *p[1])
            p[6] = _cmul(*p[4], *p[2])
            p[7] = _cmul(*p[4], *p[3])
            p[8] = _cmul(*p[4], *p[4])
            for j, k in enumerate((1, 2, 4)):
                keep = (row >= k) if base == 0 else (row < 8 - k)
                tab_ref[base + 2 * j] = jnp.where(keep, jnp.broadcast_to(p[k][0], (8, s)), 0.0)
                tab_ref[base + 2 * j + 1] = jnp.where(keep, jnp.broadcast_to(p[k][1], (8, s)), 0.0)
            pw_r = jnp.zeros((8, s), F32)
            pw_i = jnp.zeros((8, s), F32)
            for i in range(8):
                e = i + 1 if base == 0 else 8 - i
                pw_r = jnp.where(row == i, jnp.broadcast_to(p[e][0], (8, s)), pw_r)
                pw_i = jnp.where(row == i, jnp.broadcast_to(p[e][1], (8, s)), pw_i)
            tab_ref[base + 6] = pw_r
            tab_ref[base + 7] = pw_i

    def full(shape):
        return pl.BlockSpec(shape, lambda i: (0,) * len(shape))

    g = D_MODEL // SSM_GROUP
    return _call(body, "s5_prep", (1,),
                 [full((1, s)), full((1, s)), full((1, g)), full((16, s)), full((16, s)), full((g, s))],
                 [full((1, s)), full((1, s)), full((16, s)), full((16, s)), full((16, 8, s))],
                 [_sds((1, s), F32), _sds((1, s), F32), _sds((16, s), F32), _sds((16, s), F32),
                  _sds((16, 8, s), F32)])(lr, li, ldt, br, bi, expand)


def _s5_prep_bwd(lr, li, ldt, br, bi, d_abr, d_abi, d_bbr, d_bbi):
    s = N_STATE
    g = D_MODEL // SSM_GROUP
    expand = _group_expand_matrix()

    def body(lr_ref, li_ref, ldt_ref, br_ref, bi_ref, e_ref, c1, c2, c3, c4, o1, o2, o3, o4, o5):
        e_val = e_ref[...]
        _, vjp = jax.vjp(lambda a, b, c, d, e: _discretise(a, b, c, d, e, e_val),
                         lr_ref[...], li_ref[...], ldt_ref[...], br_ref[...], bi_ref[...])
        grads = vjp((c1[...], c2[...], c3[...], c4[...]))
        for o, gr in zip((o1, o2, o3, o4, o5), grads):
            o[...] = gr

    def full(shape):
        return pl.BlockSpec(shape, lambda i: (0,) * len(shape))

    return _call(body, "s5_prep_bwd", (1,),
                 [full((1, s)), full((1, s)), full((1, g)), full((16, s)), full((16, s)), full((g, s)),
                  full((1, s)), full((1, s)), full((16, s)), full((16, s))],
                 [full((1, s)), full((1, s)), full((1, g)), full((16, s)), full((16, s))],
                 [_sds((1, s), F32), _sds((1, s), F32), _sds((1, g), F32), _sds((16, s), F32),
                  _sds((16, s), F32)])(lr, li, ldt, br, bi, expand, d_abr, d_abi, d_bbr, d_bbi)


def _scan8(vr, vi, tab_ref, base, reverse):
    for j, k in enumerate((1, 2, 4)):
        mr = tab_ref[base + 2 * j]
        mi = tab_ref[base + 2 * j + 1]
        shift = 8 - k if reverse else k
        sr = pltpu.roll(vr, shift, 0)
        si = pltpu.roll(vi, shift, 0)
        vr, vi = vr + mr * sr - mi * si, vi + mr * si + mi * sr
    return vr, vi


def _s5_scan_fwd(x, bbr, bbi, cr, ci, dvec, tab, seq):
    n, d = x.shape
    nt = seq // T_SCAN
    nblk = T_SCAN // 8
    st = CHUNK_ST

    def body(x_ref, bbr_ref, bbi_ref, cr_ref, ci_ref, d_ref, tab_ref, y_ref, hr_ref, hi_ref, car_r, car_i):
        t = pl.program_id(2)

        @pl.when(t == 0)
        def _():
            car_r[...] = jnp.zeros_like(car_r)
            car_i[...] = jnp.zeros_like(car_i)

        u = x_ref[...]
        ub = u.astype(BF16)
        hr_ref[...] = jnp.dot(ub, bbr_ref[...], preferred_element_type=F32)
        hi_ref[...] = jnp.dot(ub, bbi_ref[...], preferred_element_type=F32)

        def step(i, carry):
            sl = pl.ds(pl.multiple_of(i * 8, 8), 8)
            vr, vi = _scan8(hr_ref[sl, :], hi_ref[sl, :], tab_ref, 0, False)
            c_r = jnp.broadcast_to(car_r[...], (8, st))
            c_i = jnp.broadcast_to(car_i[...], (8, st))
            pr = tab_ref[6]
            pi = tab_ref[7]
            vr, vi = vr + pr * c_r - pi * c_i, vi + pr * c_i + pi * c_r
            hr_ref[sl, :] = vr
            hi_ref[sl, :] = vi
            car_r[...] = vr[7:8, :]
            car_i[...] = vi[7:8, :]
            return carry

        lax.fori_loop(0, nblk, step, 0)
        y = lax.dot_general(hr_ref[...].astype(BF16), cr_ref[...], NT, preferred_element_type=F32)
        y = y - lax.dot_general(hi_ref[...].astype(BF16), ci_ref[...], NT, preferred_element_type=F32)
        y_ref[...] = y + d_ref[...] * u

    row = lambda c, b, t: (b * nt + t, c)
    mat = pl.BlockSpec((None, CHUNK_CH, st), lambda c, b, t: (c, 0, 0))
    return _call(
        body, "s5_scan_fwd", (SSM_CHUNKS, n // seq, nt),
        [pl.BlockSpec((T_SCAN, CHUNK_CH), row), mat, mat, mat, mat,
         pl.BlockSpec((1, CHUNK_CH), lambda c, b, t: (0, c)),
         pl.BlockSpec((16, 8, st), lambda c, b, t: (0, 0, c))],
        [pl.BlockSpec((T_SCAN, CHUNK_CH), row), pl.BlockSpec((T_SCAN, st), row), pl.BlockSpec((T_SCAN, st), row)],
        [_sds((n, d), F32), _sds((n, N_STATE), F32), _sds((n, N_STATE), F32)],
        [pltpu.VMEM((1, st), F32), pltpu.VMEM((1, st), F32)],
    )(x, bbr, bbi, cr, ci, dvec, tab)


def _s5_scan_bwd(x, dy, dres, hr, hi, bbr, bbi, cr, ci, dvec, tab, seq):
    n, d = x.shape
    nt = seq // T_SCAN
    nblk = T_SCAN // 8
    st = CHUNK_ST

    def body(x_ref, dy_ref, dres_ref, hr_ref, hi_ref, hpr_ref, hpi_ref, bbr_ref, bbi_ref, cr_ref, ci_ref,
             d_ref, tab_ref, dx_ref, dcr_ref, dci_ref, dbr_ref, dbi_ref, dar_ref, dai_ref, dd_ref,
             g_r, g_i, car_r, car_i):
        b = pl.program_id(1)
        tg = pl.program_id(2)
        first = jnp.logical_and(b == 0, tg == 0)

        @pl.when(tg == 0)
        def _():
            car_r[...] = jnp.zeros_like(car_r)
            car_i[...] = jnp.zeros_like(car_i)

        u = x_ref[...]
        dyv = dy_ref[...]
        ub = u.astype(BF16)
        dyb = dyv.astype(BF16)
        g_r[...] = jnp.dot(dyb, cr_ref[...], preferred_element_type=F32)
        g_i[...] = -jnp.dot(dyb, ci_ref[...], preferred_element_type=F32)

        def step(ii, carry):
            i = nblk - 1 - ii
            sl = pl.ds(pl.multiple_of(i * 8, 8), 8)
            vr, vi = _scan8(g_r[sl, :], g_i[sl, :], tab_ref, 8, True)
            c_r = jnp.broadcast_to(car_r[...], (8, st))
            c_i = jnp.broadcast_to(car_i[...], (8, st))
            pr = tab_ref[14]
            pi = tab_ref[15]
            vr, vi = vr + pr * c_r - pi * c_i, vi + pr * c_i + pi * c_r
            g_r[sl, :] = vr
            g_i[sl, :] = vi
            car_r[...] = vr[0:1, :]
            car_i[...] = vi[0:1, :]
            return carry

        lax.fori_loop(0, nblk, step, 0)
        gr = g_r[...]
        gi = g_i[...]
        grb = gr.astype(BF16)
        gib = gi.astype(BF16)
        du = lax.dot_general(grb, bbr_ref[...], NT, preferred_element_type=F32)
        du = du + lax.dot_general(gib, bbi_ref[...], NT, preferred_element_type=F32)
        dx_ref[...] = DN_ALPHA * dres_ref[...] + du + d_ref[...] * dyv

        h_r = hr_ref[...]
        h_i = hi_ref[...]
        t_orig = nt - 1 - tg
        row = lax.broadcasted_iota(jnp.int32, (T_SCAN, st), 0)
        halo_ok = t_orig > 0
        top_r = jnp.where(halo_ok, hpr_ref[7:8, :], 0.0)
        top_i = jnp.where(halo_ok, hpi_ref[7:8, :], 0.0)
        hp_r = jnp.where(row == 0, jnp.broadcast_to(top_r, (T_SCAN, st)), pltpu.roll(h_r, 1, 0))
        hp_i = jnp.where(row == 0, jnp.broadcast_to(top_i, (T_SCAN, st)), pltpu.roll(h_i, 1, 0))
        dar = jnp.sum(gr * hp_r + gi * hp_i, axis=0, keepdims=True)
        dai = jnp.sum(gi * hp_r - gr * hp_i, axis=0, keepdims=True)
        dcr = lax.dot_general(dyb, h_r.astype(BF16), TN, preferred_element_type=F32)
        dci = lax.dot_general(dyb, h_i.astype(BF16), TN, preferred_element_type=F32)
        dbr = lax.dot_general(ub, grb, TN, preferred_element_type=F32)
        dbi = lax.dot_general(ub, gib, TN, preferred_element_type=F32)
        dd = jnp.sum(dyv * u, axis=0, keepdims=True)

        @pl.when(first)
        def _():
            dcr_ref[...] = dcr
            dci_ref[...] = dci
            dbr_ref[...] = dbr
            dbi_ref[...] = dbi
            dar_ref[...] = dar
            dai_ref[...] = dai
            dd_ref[...] = dd

        @pl.when(jnp.logical_not(first))
        def _():
            dcr_ref[...] += dcr
            dci_ref[...] += dci
            dbr_ref[...] += dbr
            dbi_ref[...] += dbi
            dar_ref[...] += dar
            dai_ref[...] += dai
            dd_ref[...] += dd

    row = lambda c, b, t: (b * nt + (nt - 1 - t), c)
    halo = lambda c, b, t: (jnp.maximum((b * nt + (nt - 1 - t)) * (T_SCAN // 8) - 1, 0), c)
    mat = pl.BlockSpec((None, CHUNK_CH, st), lambda c, b, t: (c, 0, 0))
    chan = pl.BlockSpec((T_SCAN, CHUNK_CH), row)
    state = pl.BlockSpec((T_SCAN, st), row)
    return _call(
        body, "s5_scan_bwd", (SSM_CHUNKS, n // seq, nt),
        [chan, chan, chan, state, state, pl.BlockSpec((8, st), halo), pl.BlockSpec((8, st), halo),
         mat, mat, mat, mat, pl.BlockSpec((1, CHUNK_CH), lambda c, b, t: (0, c)),
         pl.BlockSpec((16, 8, st), lambda c, b, t: (0, 0, c))],
        [chan, mat, mat, mat, mat, pl.BlockSpec((1, st), lambda c, b, t: (0, c)),
         pl.BlockSpec((1, st), lambda c, b, t: (0, c)), pl.BlockSpec((1, CHUNK_CH), lambda c, b, t: (0, c))],
        [_sds((n, d), F32)] + [_sds((SSM_CHUNKS, CHUNK_CH, st), F32)] * 4
        + [_sds((1, N_STATE), F32), _sds((1, N_STATE), F32), _sds((1, d), F32)],
        [pltpu.VMEM((T_SCAN, st), F32), pltpu.VMEM((T_SCAN, st), F32), pltpu.VMEM((1, st), F32),
         pltpu.VMEM((1, st), F32)],
    )(x, dy, dres, hr, hi, hr, hi, bbr, bbi, cr, ci, dvec, tab)


_GELU_C = math.sqrt(2.0 / math.pi)


def _gelu_parts(y):
    t = jnp.tanh(_GELU_C * (y + 0.044715 * (y * y * y)))
    return 0.5 * (1.0 + t), t


def _glu_fwd(y, w_glu, b_glu):
    n, d = y.shape

    def body(y_ref, w_ref, b_ref, y2_ref, z_ref, gg_ref):
        yv = y_ref[...]
        cdf, _ = _gelu_parts(yv)
        y2 = yv * cdf
        z = jnp.dot(y2.astype(BF16), w_ref[...], preferred_element_type=F32) + b_ref[...]
        y2_ref[...] = y2.astype(BF16)
        z_ref[...] = z
        gg_ref[...] = (y2 * jax.nn.sigmoid(z)).astype(BF16)

    row = pl.BlockSpec((TM, d), lambda i: (i, 0))
    return _call(body, "glu_fwd", (n // TM,),
                 [row, pl.BlockSpec((d, d), lambda i: (0, 0)), pl.BlockSpec((1, d), lambda i: (0, 0))],
                 [row, row, row], [_sds((n, d), BF16), _sds((n, d), F32), _sds((n, d), BF16)])(y, w_glu, b_glu)


def _glu_bwd_gate(y, z, dgg):
    n, d = y.shape

    def body(y_ref, z_ref, dgg_ref, dz_ref, db_ref, t_ref):
        i = pl.program_id(0)
        yv = y_ref[...]
        cdf, _ = _gelu_parts(yv)
        y2 = yv * cdf
        s = jax.nn.sigmoid(z_ref[...])
        dg = dgg_ref[...]
        dz = dg * y2 * s * (1.0 - s)
        dz_ref[...] = dz.astype(BF16)
        t_ref[...] = dg * s
        db = jnp.sum(dz, axis=0, keepdims=True)

        @pl.when(i == 0)
        def _():
            db_ref[...] = db

        @pl.when(i > 0)
        def _():
            db_ref[...] += db

    row = pl.BlockSpec((TM, d), lambda i: (i, 0))
    vec = pl.BlockSpec((1, d), lambda i: (0, 0))
    return _call(body, "glu_bwd_gate", (n // TM,), [row, row, row], [row, vec, row],
                 [_sds((n, d), BF16), _sds((1, d), F32), _sds((n, d), F32)])(y, z, dgg)


def _gelu_bwd(y, dy2):
    n, d = y.shape

    def body(y_ref, g_ref, o_ref):
        yv = y_ref[...]
        cdf, t = _gelu_parts(yv)
        dinner = _GELU_C * (1.0 + 3.0 * 0.044715 * yv * yv)
        o_ref[...] = g_ref[...] * (cdf + 0.5 * yv * (1.0 - t * t) * dinner)

    row = pl.BlockSpec((TM, d), lambda i: (i, 0))
    return _call(body, "gelu_bwd", (n // TM,), [row, row], row, _sds((n, d), F32))(y, dy2)


def _shift_down(x, halo, k, rows):
    out = pltpu.roll(x, k, 0)
    row = lax.broadcasted_iota(jnp.int32, x.shape, 0)
    for j in range(k):
        out = jnp.where(row == j, jnp.broadcast_to(halo[8 - k + j:8 - k + j + 1, :], x.shape), out)
    return out


def _conv3(x, halo, w, b):
    return (b + w[0:1, :] * x + w[1:2, :] * _shift_down(x, halo, 1, None)
            + w[2:3, :] * _shift_down(x, halo, 2, None))


def _conv_gate_fwd(hc, conv_w, conv_b, seq):
    _, nb, n, c = hc.shape
    tiles_per_seq = seq // TM

    def body(x_ref, halo_ref, w_ref, b_ref, o_ref):
        i = pl.program_id(1)
        start = (i % tiles_per_seq) == 0
        cv = []
        for h in range(2):
            halo = jnp.where(start, 0.0, halo_ref[h])
            cv.append(_conv3(x_ref[h], halo, w_ref[h], b_ref[h]))
        o_ref[...] = (cv[1] * jax.nn.sigmoid(cv[1]) * cv[0]).astype(BF16)

    return _call(
        body, "conv_gate_fwd", (nb, n // TM),
        [pl.BlockSpec((2, None, TM, c), lambda d, i: (0, d, i, 0)),
         pl.BlockSpec((2, None, 8, c), lambda d, i: (0, d, jnp.maximum(i * (TM // 8) - 1, 0), 0)),
         pl.BlockSpec((2, None, 3, c), lambda d, i: (0, d, 0, 0)),
         pl.BlockSpec((2, None, 1, c), lambda d, i: (0, d, 0, 0))],
        pl.BlockSpec((None, TM, c), lambda d, i: (d, i, 0)),
        _sds((nb, n, c), BF16))(hc, hc, conv_w, conv_b)


def _conv_gate_bwd(hc, dact, conv_w, conv_b, seq):
    _, nb, n, c = hc.shape
    tiles_per_seq = seq // TM
    last_halo = n // 8 - 1
    ext = TM + 8

    def body(x_ref, prev_ref, next_ref, da_ref, dan_ref, w_ref, b_ref, dx_ref, dw_ref):
        i = pl.program_id(1)
        start = (i % tiles_per_seq) == 0
        end = (i % tiles_per_seq) == tiles_per_seq - 1
        row = lax.broadcasted_iota(jnp.int32, (ext, c), 0)
        xe, cv = [], []
        for h in range(2):
            halo = jnp.where(start, 0.0, prev_ref[h])
            x_ext = jnp.concatenate([x_ref[h], next_ref[h]], axis=0)
            xe.append((x_ext, halo))
            cv.append(_conv3(x_ext, halo, w_ref[h], b_ref[h]))
        da = jnp.concatenate([da_ref[...], dan_ref[...]], axis=0)
        da = jnp.where(jnp.logical_and(end, row >= TM), 0.0, da)
        sg = jax.nn.sigmoid(cv[1])
        silu = cv[1] * sg
        dcv = [da * silu, da * cv[0] * (sg + silu * (1.0 - sg))]
        for h in range(2):
            w = w_ref[h]
            g = dcv[h]
            dx = w[0:1, :] * g + w[1:2, :] * pltpu.roll(g, ext - 1, 0) + w[2:3, :] * pltpu.roll(g, ext - 2, 0)
            dx_ref[h] = dx[:TM, :].astype(BF16)
            x_ext, halo = xe[h]
            gt = g[:TM, :]
            xt = x_ext[:TM, :]
            parts = [jnp.sum(gt * xt, axis=0, keepdims=True),
                     jnp.sum(gt * _shift_down(xt, halo, 1, None), axis=0, keepdims=True),
                     jnp.sum(gt * _shift_down(xt, halo, 2, None), axis=0, keepdims=True),
                     jnp.sum(gt, axis=0, keepdims=True)]
            upd = jnp.concatenate(parts + [jnp.zeros((4, c), F32)], axis=0)

            @pl.when(i == 0)
            def _():
                dw_ref[h] = upd

            @pl.when(i > 0)
            def _():
                dw_ref[h] += upd

    nxt = lambda i: jnp.minimum((i + 1) * (TM // 8), last_halo)
    return _call(
        body, "conv_gate_bwd", (nb, n // TM),
        [pl.BlockSpec((2, None, TM, c), lambda d, i: (0, d, i, 0)),
         pl.BlockSpec((2, None, 8, c), lambda d, i: (0, d, jnp.maximum(i * (TM // 8) - 1, 0), 0)),
         pl.BlockSpec((2, None, 8, c), lambda d, i: (0, d, nxt(i), 0)),
         pl.BlockSpec((None, TM, c), lambda d, i: (d, i, 0)),
         pl.BlockSpec((None, 8, c), lambda d, i: (d, nxt(i), 0)),
         pl.BlockSpec((2, None, 3, c), lambda d, i: (0, d, 0, 0)),
         pl.BlockSpec((2, None, 1, c), lambda d, i: (0, d, 0, 0))],
        [pl.BlockSpec((2, None, TM, c), lambda d, i: (0, d, i, 0)),
         pl.BlockSpec((2, None, 8, c), lambda d, i: (0, d, 0, 0))],
        [_sds((2, nb, n, c), BF16), _sds((2, nb, 8, c), F32)])(hc, hc, hc, dact, dact, conv_w, conv_b)


def _t5_bucket_np(dist):
    exact = REL_BUCKETS // 2
    d = np.maximum(dist, 1).astype(np.float32)
    large = exact + (np.log(d / exact) / math.log(REL_MAX_DIST / exact) * (REL_BUCKETS - exact)).astype(np.int64)
    large = np.minimum(large, REL_BUCKETS - 1)
    return np.where(dist < exact, dist, large).astype(np.int32)


def _bucket_table(dil):
    steps = np.arange(BAND)[:, None] + BAND - np.arange(2 * BAND)[None, :]
    return _t5_bucket_np(np.maximum(steps, 0) * dil).reshape(1, BAND * 2 * BAND).astype(np.float32)


def _bias_expand(rel_t, dil):
    nk = BAND * 2 * BAND
    bucket = jnp.asarray(_bucket_table(dil))

    def body(r_ref, bk_ref, o_ref):
        ids = lax.broadcasted_iota(jnp.int32, (REL_BUCKETS, nk), 0).astype(F32)
        onehot = (ids == bk_ref[...]).astype(F32)
        o_ref[...] = jnp.dot(r_ref[...], onehot, precision=HIGHEST, preferred_element_type=F32)

    return _call(body, "bias_expand", (1,),
                 [pl.BlockSpec((N_HEADS, REL_BUCKETS), lambda i: (0, 0)), pl.BlockSpec((1, nk), lambda i: (0, 0))],
                 pl.BlockSpec((N_HEADS, nk), lambda i: (0, 0)), _sds((N_HEADS, nk), F32))(rel_t, bucket)


def _bias_reduce(dbias, dil):
    nk = BAND * 2 * BAND
    bucket = jnp.asarray(_bucket_table(dil))

    def body(g_ref, bk_ref, o_ref):
        ids = lax.broadcasted_iota(jnp.int32, (REL_BUCKETS, nk), 0).astype(F32)
        onehot = (ids == bk_ref[...]).astype(F32)
        o_ref[...] = lax.dot_general(g_ref[...], onehot, NT, precision=HIGHEST, preferred_element_type=F32)

    return _call(body, "bias_reduce", (1,),
                 [pl.BlockSpec((N_HEADS, nk), lambda i: (0, 0)), pl.BlockSpec((1, nk), lambda i: (0, 0))],
                 pl.BlockSpec((N_HEADS, REL_BUCKETS), lambda i: (0, 0)),
                 _sds((N_HEADS, REL_BUCKETS), F32))(dbias, bucket)


def _valid_mask(n):
    qi = lax.broadcasted_iota(jnp.int32, (BAND, 2 * BAND), 0)
    kj = lax.broadcasted_iota(jnp.int32, (BAND, 2 * BAND), 1)
    steps = qi + BAND - kj
    in_band = jnp.logical_and(steps >= 0, steps <= BAND)
    return jnp.logical_and(in_band, jnp.logical_or(n > 0, kj >= BAND))


def _attn_views(arr, bsz, seq, dil):
    return arr.reshape(bsz, seq // dil, dil * arr.shape[1])


def _attn_fwd(q, k, v, bias, g, dil, bsz, seq):
    n = q.shape[0]
    nb = seq // dil // BAND
    w = N_HEADS * HEAD_DIM

    def body(q_ref, kc_ref, kp_ref, vc_ref, vp_ref, b_ref, o_ref, l_ref):
        nblk = pl.program_id(2)
        valid = _valid_mask(nblk)
        lane = lax.broadcasted_iota(jnp.int32, (BAND, 128), 1)
        for j in range(N_HEADS // 2):
            sl = slice(128 * j, 128 * (j + 1))
            q2 = q_ref[:, sl]
            k2 = jnp.concatenate([kp_ref[:, sl], kc_ref[:, sl]], axis=0)
            v2 = jnp.concatenate([vp_ref[:, sl], vc_ref[:, sl]], axis=0)
            outs, lses = [], []
            for half in range(2):
                mine = (lane < HEAD_DIM) if half == 0 else (lane >= HEAD_DIM)
                qh = jnp.where(mine, q2, jnp.zeros_like(q2))
                s = lax.dot_general(qh, k2, NT, preferred_element_type=F32) * (HEAD_DIM ** -0.5)
                s = jnp.where(valid, s + b_ref[2 * j + half], NEG_BIG)
                m = jnp.max(s, axis=1, keepdims=True)
                p = jnp.exp(s - m)
                l = jnp.sum(p, axis=1, keepdims=True)
                pv = jnp.dot(p.astype(BF16), v2, preferred_element_type=F32)
                outs.append(pv / l)
                lses.append(jnp.broadcast_to(m + jnp.log(l), (BAND, 128)))
            o_ref[:, sl] = jnp.where(lane < HEAD_DIM, outs[0], outs[1])
            l_ref[:, sl] = jnp.where(lane < HEAD_DIM, lses[0], lses[1])

    cur = lambda b, r, nn: (b, nn, r * 3 + g)
    prev = lambda b, r, nn: (b, jnp.maximum(nn - 1, 0), r * 3 + g)
    blk = lambda im: pl.BlockSpec((None, BAND, w), im)
    out_blk = pl.BlockSpec((None, BAND, w), lambda b, r, nn: (b, nn, r))
    qv, kv_, vv = (_attn_views(a, bsz, seq, dil) for a in (q, k, v))
    o, lse = _call(
        body, "attn_fwd_d%d" % dil, (bsz, dil, nb),
        [blk(cur), blk(cur), blk(prev), blk(cur), blk(prev),
         pl.BlockSpec((N_HEADS, BAND, 2 * BAND), lambda b, r, nn: (0, 0, 0))],
        [out_blk, out_blk],
        [_sds((bsz, seq // dil, dil * w), F32)] * 2)(qv, kv_, kv_, vv, vv, bias)
    return o.reshape(n, w), lse.reshape(n, w)


def _attn_combine(outs, lses):
    n, w = outs[0].shape

    def body(o0, o1, o2, l0, l1, l2, o_ref, l_ref):
        la, lb, lc = l0[...], l1[...], l2[...]
        m = jnp.maximum(jnp.maximum(la, lb), lc)
        wa, wb, wc = jnp.exp(la - m), jnp.exp(lb - m), jnp.exp(lc - m)
        tot = wa + wb + wc
        o_ref[...] = (wa * o0[...] + wb * o1[...] + wc * o2[...]) / tot
        l_ref[...] = m + jnp.log(tot)

    row = pl.BlockSpec((TM, w), lambda i: (i, 0))
    return _call(body, "attn_combine", (n // TM,), [row] * 6, [row, row],
                 [_sds((n, w), F32)] * 2)(*outs, *lses)


def _attn_bwd(q, k, v, do, o, lse, bias, g, dil, bsz, seq):
    n = q.shape[0]
    nb = seq // dil // BAND
    w = N_HEADS * HEAD_DIM
    scale = HEAD_DIM ** -0.5

    def body(q_ref, kc_ref, kp_ref, vc_ref, vp_ref, do_ref, o_ref, l_ref, b_ref,
             dq_ref, dk_ref, dv_ref, db_ref, car_k, car_v):
        b = pl.program_id(0)
        r = pl.program_id(1)
        nblk = pl.program_id(2)

        @pl.when(jnp.logical_and(jnp.logical_and(b == 0, r == 0), nblk == 0))
        def _():
            db_ref[...] = jnp.zeros_like(db_ref)

        @pl.when(nblk == 0)
        def _():
            car_k[...] = jnp.zeros_like(car_k)
            car_v[...] = jnp.zeros_like(car_v)

        @pl.when(nblk < nb)
        def _():
            valid = _valid_mask(nblk)
            lane = lax.broadcasted_iota(jnp.int32, (BAND, 128), 1)
            for j in range(N_HEADS // 2):
                sl = slice(128 * j, 128 * (j + 1))
                q2 = q_ref[:, sl]
                k2 = jnp.concatenate([kp_ref[:, sl], kc_ref[:, sl]], axis=0)
                v2 = jnp.concatenate([vp_ref[:, sl], vc_ref[:, sl]], axis=0)
                do2 = do_ref[:, sl]
                prod = do2 * o_ref[:, sl]
                lse2 = l_ref[:, sl]
                dqs = []
                dk2 = jnp.zeros((2 * BAND, 128), F32)
                dv2 = jnp.zeros((2 * BAND, 128), F32)
                for half in range(2):
                    mine = (lane < HEAD_DIM) if half == 0 else (lane >= HEAD_DIM)
                    qh = jnp.where(mine, q2, jnp.zeros_like(q2))
                    doh = jnp.where(mine, do2, 0.0).astype(BF16)
                    s = lax.dot_general(qh, k2, NT, preferred_element_type=F32) * scale
                    s = jnp.where(valid, s + b_ref[2 * j + half], NEG_BIG)
                    lse_h = lse2[:, 64 * half:64 * half + 1]
                    p = jnp.exp(s - lse_h)
                    dp = lax.dot_general(doh, v2, NT, preferred_element_type=F32)
                    delta = jnp.sum(jnp.where(mine, prod, 0.0), axis=1, keepdims=True)
                    ds = p * (dp - delta)
                    db_ref[2 * j + half] += ds
                    dsb = ds.astype(BF16)
                    dqs.append(jnp.dot(dsb, k2, preferred_element_type=F32) * scale)
                    dk2 = dk2 + lax.dot_general(dsb, qh, TN, preferred_element_type=F32) * scale
                    dv2 = dv2 + lax.dot_general(p.astype(BF16), doh, TN, preferred_element_type=F32)
                dq_ref[:, sl] = jnp.where(lane < HEAD_DIM, dqs[0], dqs[1]).astype(BF16)
                dk_ref[:, sl] = (car_k[:, sl] + dk2[:BAND, :]).astype(BF16)
                dv_ref[:, sl] = (car_v[:, sl] + dv2[:BAND, :]).astype(BF16)
                car_k[:, sl] = dk2[BAND:, :]
                car_v[:, sl] = dv2[BAND:, :]

        @pl.when(nblk == nb)
        def _():
            dk_ref[...] = car_k[...].astype(BF16)
            dv_ref[...] = car_v[...].astype(BF16)

    last = nb - 1
    cur = lambda b, r, nn: (b, jnp.minimum(nn, last), r * 3 + g)
    prev = lambda b, r, nn: (b, jnp.maximum(jnp.minimum(nn, last) - 1, 0), r * 3 + g)
    cur1 = lambda b, r, nn: (b, jnp.minimum(nn, last), r)
    lag1 = lambda b, r, nn: (b, jnp.maximum(nn - 1, 0), r)
    blk = lambda im: pl.BlockSpec((None, BAND, w), im)
    full_bias = pl.BlockSpec((N_HEADS, BAND, 2 * BAND), lambda b, r, nn: (0, 0, 0))
    qv, kv_, vv = (_attn_views(a, bsz, seq, dil) for a in (q, k, v))
    dov, ov, lv = (_attn_views(a, bsz, seq, dil) for a in (do, o, lse))
    view = _sds((bsz, seq // dil, dil * w), BF16)
    dq, dk, dv, db = _call(
        body, "attn_bwd_d%d" % dil, (bsz, dil, nb + 1),
        [blk(cur), blk(cur), blk(prev), blk(cur), blk(prev), blk(cur1), blk(cur1), blk(cur1), full_bias],
        [blk(cur1), blk(lag1), blk(lag1), full_bias],
        [view, view, view, _sds((N_HEADS, BAND, 2 * BAND), F32)],
        [pltpu.VMEM((BAND, w), F32), pltpu.VMEM((BAND, w), F32)],
    )(qv, kv_, kv_, vv, vv, dov, ov, lv, bias)
    return dq.reshape(n, w), dk.reshape(n, w), dv.reshape(n, w), db


def _adamw(name, parts, w, m, v):
    r, c = w.shape
    nparts = parts.shape[0]
    tr = r
    for cand in (256, 352, 128):
        if r % cand == 0 and r > cand:
            tr = cand
            break
    c1 = 1.0 - ADAM_B1 ** ADAM_STEP
    c2 = 1.0 - ADAM_B2 ** ADAM_STEP

    def body(p_ref, w_ref, m_ref, v_ref, g_ref, d_ref, nm_ref, nv_ref):
        g = p_ref[0].astype(F32)
        for dev in range(1, nparts):
            g = g + p_ref[dev].astype(F32)
        nm = ADAM_B1 * m_ref[...] + (1.0 - ADAM_B1) * g
        nv = ADAM_B2 * v_ref[...] + (1.0 - ADAM_B2) * (g * g)
        m_hat = nm / c1
        v_hat = nv / c2
        g_ref[...] = g
        d_ref[...] = -ADAM_LR * (m_hat / (jnp.sqrt(v_hat) + ADAM_EPS) + ADAM_WD * w_ref[...])
        nm_ref[...] = nm
        nv_ref[...] = nv

    row = pl.BlockSpec((tr, c), lambda i: (i, 0))
    return _call(body, name, (r // tr,), [pl.BlockSpec((nparts, tr, c), lambda i: (0, i, 0)), row, row, row],
                 [row] * 4, [_sds((r, c), F32)] * 4)(parts, w, m, v)


def _exchange(name, srcs, out_shapes, items):
    n_in = len(srcs)
    n_out = len(out_shapes)
    n_items = len(items)

    def body(*refs):
        ins = refs[:n_in]
        outs = refs[n_in:n_in + n_out]
        send_sems, recv_sems, local_sems = refs[n_in + n_out:]
        x, y, c = lax.axis_index("x"), lax.axis_index("y"), lax.axis_index("c")
        me = 4 * x + 2 * y + c

        def pick(ref, sel, peer):
            idx = tuple(peer if s == "peer" else me if s == "me" else s for s in sel)
            return ref.at[idx] if idx else ref

        copies = []
        for it, (si, ssel, oi, osel) in enumerate(items):
            local = pltpu.make_async_copy(pick(ins[si], ssel, me), pick(outs[oi], osel, me), local_sems.at[it])
            local.start()
            copies.append(local)
            for j in range(1, N_DEV):
                px = 1 - x if j & 4 else x
                py = 1 - y if j & 2 else y
                pc = 1 - c if j & 1 else c
                peer = 4 * px + 2 * py + pc
                sem = it * (N_DEV - 1) + j - 1
                cp = pltpu.make_async_remote_copy(
                    src_ref=pick(ins[si], ssel, peer), dst_ref=pick(outs[oi], osel, peer),
                    send_sem=send_sems.at[sem], recv_sem=recv_sems.at[sem],
                    device_id=(px, py, pc), device_id_type=pl.DeviceIdType.MESH)
                cp.start()
                copies.append(cp)
        for cp in copies:
            cp.wait()

    any_spec = pl.BlockSpec(memory_space=pl.ANY)
    return pl.pallas_call(
        body, name=name, in_specs=[any_spec] * n_in, out_specs=[any_spec] * n_out, out_shape=list(out_shapes),
        scratch_shapes=[pltpu.SemaphoreType.DMA((n_items * (N_DEV - 1),)),
                        pltpu.SemaphoreType.DMA((n_items * (N_DEV - 1),)),
                        pltpu.SemaphoreType.DMA((n_items,))],
        compiler_params=pltpu.CompilerParams(has_side_effects=True),
    )(*srcs)


def _sum_slots(parts):
    _, r, c = parts.shape

    def body(p_ref, o_ref):
        acc = p_ref[0]
        for dev in range(1, N_DEV):
            acc = acc + p_ref[dev]
        o_ref[...] = acc

    return _call(body, "sum_slots", (1,), [pl.BlockSpec((N_DEV, r, c), lambda i: (0, 0, 0))],
                 pl.BlockSpec((r, c), lambda i: (0, 0)), _sds((r, c), F32))(parts)


def _pack(arrays):
    rows = []
    for a in arrays:
        flat = a.reshape(-1).astype(F32)
        pad = (-flat.shape[0]) % 1024
        if pad:
            flat = jnp.concatenate([flat, jnp.zeros((pad,), F32)])
        rows.append(flat.reshape(-1, 128))
    return jnp.concatenate(rows, axis=0)


def _unpack(buf, shapes):
    out = []
    r0 = 0
    for shp in shapes:
        size = int(np.prod(shp))
        nrows = -(-size // 1024) * 8
        out.append(buf[r0:r0 + nrows].reshape(-1)[:size].reshape(shp))
        r0 += nrows
    return out


def _block_diag_chunks(val):
    eye = jnp.eye(8, dtype=val.dtype)
    return jnp.einsum("cjhp,jk->cjhkp", val, eye).reshape(SSM_CHUNKS, CHUNK_CH, CHUNK_ST)


def _diag_blocks(dense):
    d5 = dense.reshape(SSM_CHUNKS, 8, SSM_GROUP, 8, SSM_STATE)
    return jnp.stack([d5[:, j, :, j, :] for j in range(8)], axis=1)


def _ffn_fwd(tag, hin, w_up, conv_w, conv_b, w_down4, gain, bias, seq):
    hc = _mm_nn("ffn_up" + tag, hin, w_up, F32, blocked_out=True)
    nb, n, c = hc.shape
    hc = hc.reshape(2, nb // 2, n, c)
    act = _conv_gate_fwd(hc, conv_w, conv_b, seq)
    ffn = _mm_nn_red("ffn_down" + tag, act, w_down4, F32)
    hout = _ln_fwd("ffn_norm" + tag, hin, ffn, gain, bias)
    return hout, (hin, hc, act, ffn)


def _ffn_bwd(tag, saved, dh, w_up, conv_w, conv_b, w_down4, gain, seq):
    hin, hc, act, ffn = saved
    dr, dgain, dbias = _ln_bwd("ffn_norm_bwd" + tag, hin, ffn, dh, gain)
    d_wdown = _mm_tn("ffn_down_dw" + tag, act, dr, act.shape[0], BF16, a_blocked=True, g_mode="shared")
    dact = _mm_nt_out("ffn_down_dx" + tag, dr, w_down4, F32)
    dhc, dconv = _conv_gate_bwd(hc, dact, conv_w, conv_b, seq)
    dhc8 = dhc.reshape(2 * dhc.shape[1], dhc.shape[2], dhc.shape[3])
    d_wup = _mm_tn("ffn_up_dw" + tag, hin, dhc8, dhc8.shape[0], BF16, g_mode="blocked")
    dhin = _mm_nt_red("ffn_up_dx" + tag, dhc8, w_up, F32, g_blocked=True, add=dr, add_scale=DN_ALPHA)
    return dhin, d_wup, d_wdown, dconv, dgain, dbias


def _local_step(x, target, p, wfull, bsz, seq):
    n = bsz * seq
    xf = x.reshape(n, D_MODEL)
    tf = target.reshape(n, D_MODEL)
    ln_g = p["ln_gain"].reshape(4, 1, D_MODEL)
    ln_b = p["ln_bias"].reshape(4, 1, D_MODEL)

    lr = p["s5_lam_re"].reshape(1, N_STATE)
    li = p["s5_lam_im"].reshape(1, N_STATE)
    ldt = p["s5_log_dt"].reshape(1, D_MODEL // SSM_GROUP)
    br_t = p["s5_b_re"].reshape(N_STATE, SSM_GROUP).T
    bi_t = p["s5_b_im"].reshape(N_STATE, SSM_GROUP).T
    ab_r, ab_i, bb_r, bb_i, tab = _s5_prep(lr, li, ldt, br_t, bi_t)

    def bb_chunks(bb):
        return _block_diag_chunks(bb.reshape(SSM_GROUP, SSM_CHUNKS, 8, SSM_STATE).transpose(1, 2, 0, 3))

    def c_chunks(cc):
        return _block_diag_chunks(cc.reshape(SSM_CHUNKS, 8, SSM_GROUP, SSM_STATE))

    bbr_c, bbi_c = bb_chunks(bb_r).astype(BF16), bb_chunks(bb_i).astype(BF16)
    cr_c, ci_c = c_chunks(p["s5_c_re"]).astype(BF16), c_chunks(p["s5_c_im"]).astype(BF16)
    dvec = p["s5_d"].reshape(1, D_MODEL)

    y, h_r, h_i = _s5_scan_fwd(xf, bbr_c, bbi_c, cr_c, ci_c, dvec, tab, seq)
    y2, z, gg = _glu_fwd(y, wfull["s5_w_glu"], p["s5_b_glu"].reshape(1, D_MODEL))
    mix = _mm_nn("s5_out", gg, wfull["s5_w_out"][None], F32)
    h1 = _ln_fwd("s5_norm", xf, mix, ln_g[0], ln_b[0])
    h2, ffn0_saved = _ffn_fwd("0", h1, wfull["ffn_w_up"][0], p["conv_w"][0], p["conv_b"][0],
                              wfull["ffn_w_down"][0], ln_g[1], ln_b[1], seq)

    kmat = _mm_nn("attn_k", h2, wfull["attn_w_kv"][:4], BF16)
    vmat = _mm_nn("attn_v", h2, wfull["attn_w_kv"][4:], BF16)
    qmat = _mm_nn("attn_q", h2, wfull["attn_w_q"], BF16)
    biases, outs, lses = [], [], []
    for g, dil in enumerate(DILATIONS):
        rel_t = p["rel_bias"][:, g * N_HEADS:(g + 1) * N_HEADS].T
        biases.append(_bias_expand(rel_t, dil).reshape(N_HEADS, BAND, 2 * BAND))
        o_g, l_g = _attn_fwd(qmat, kmat, vmat, biases[g], g, dil, bsz, seq)
        outs.append(o_g)
        lses.append(l_g)
    o, lse = _attn_combine(outs, lses)
    attn_out = _mm_nn("attn_out", o, wfull["attn_w_out"][None], F32)
    h3 = _ln_fwd("attn_norm", h2, attn_out, ln_g[2], ln_b[2])
    h4, ffn1_saved = _ffn_fwd("1", h3, wfull["ffn_w_up"][1], p["conv_w"][1], p["conv_b"][1],
                              wfull["ffn_w_down"][1], ln_g[3], ln_b[3], seq)
    loss_tile, dh4 = _loss_head(h4, tf)

    grads = {}
    dh3, grads["ffn_w_up1"], grads["ffn_w_down1"], dconv1, dg3, db3 = _ffn_bwd(
        "1", ffn1_saved, dh4, wfull["ffn_w_up"][1], p["conv_w"][1], p["conv_b"][1], wfull["ffn_w_down"][1],
        ln_g[3], seq)
    dr3, dg2, db2 = _ln_bwd("attn_norm_bwd", h2, attn_out, dh3, ln_g[2])
    grads["attn_w_out"] = _mm_tn("attn_out_dw", o, dr3, 1, BF16, g_mode="shared")
    do = _mm_nt_red("attn_out_dx", dr3, wfull["attn_w_out"][None], F32, g_blocked=False)
    dqs, dks, dvs, drel = [], [], [], []
    for g, dil in enumerate(DILATIONS):
        dq_g, dk_g, dv_g, db_g = _attn_bwd(qmat, kmat, vmat, do, o, lse, biases[g], g, dil, bsz, seq)
        dqs.append(dq_g)
        dks.append(dk_g)
        dvs.append(dv_g)
        drel.append(_bias_reduce(db_g.reshape(N_HEADS, BAND * 2 * BAND), dil).T)
    dq = jnp.concatenate(dqs, axis=1)
    dk = jnp.concatenate(dks, axis=1)
    dv = jnp.concatenate(dvs, axis=1)
    grads["rel_bias"] = jnp.concatenate(drel, axis=1)
    grads["attn_w_q"] = _mm_tn("attn_q_dw", h2, dq, N_DEV, BF16)
    grads["attn_w_kv"] = jnp.concatenate([_mm_tn("attn_k_dw", h2, dk, 4, BF16),
                                          _mm_tn("attn_v_dw", h2, dv, 4, BF16)], axis=0)
    dh2 = _mm_nt_red("attn_q_dx", dq, wfull["attn_w_q"], F32, g_blocked=False, add=dr3, add_scale=DN_ALPHA)
    dh2 = _mm_nt_red("attn_k_dx", dk, wfull["attn_w_kv"][:4], F32, g_blocked=False, add=dh2)
    dh2 = _mm_nt_red("attn_v_dx", dv, wfull["attn_w_kv"][4:], F32, g_blocked=False, add=dh2)

    dh1, grads["ffn_w_up0"], grads["ffn_w_down0"], dconv0, dg1, db1 = _ffn_bwd(
        "0", ffn0_saved, dh2, wfull["ffn_w_up"][0], p["conv_w"][0], p["conv_b"][0], wfull["ffn_w_down"][0],
        ln_g[1], seq)
    dr1, dg0, db0 = _ln_bwd("s5_norm_bwd", xf, mix, dh1, ln_g[0])
    grads["s5_w_out"] = _mm_tn("s5_out_dw", gg, dr1, 1, BF16, g_mode="shared")
    dgg = _mm_nt_red("s5_out_dx", dr1, wfull["s5_w_out"][None], F32, g_blocked=False)
    dz, d_bglu, dy2_direct = _glu_bwd_gate(y, z, dgg)
    grads["s5_w_glu"] = _mm_tn("s5_glu_dw", y2, dz, 1, BF16, g_mode="shared")
    dy2 = _mm_nt_red("s5_glu_dx", dz, wfull["s5_w_glu"][None], F32, g_blocked=False, add=dy2_direct)
    dy = _gelu_bwd(y, dy2)
    dx, dcr, dci, dbr, dbi, dar, dai, dd = _s5_scan_bwd(xf, dy, dr1, h_r, h_i, bbr_c, bbi_c, cr_c, ci_c,
                                                        dvec, tab, seq)

    def bb_from_chunks(dense):
        return _diag_blocks(dense).transpose(2, 0, 1, 3).reshape(SSM_GROUP, N_STATE)

    d_lr, d_li, d_ldt, d_br, d_bi = _s5_prep_bwd(lr, li, ldt, br_t, bi_t, dar, dai,
                                                 bb_from_chunks(dbr), bb_from_chunks(dbi))
    grads["s5_lam_re"] = d_lr.reshape(p["s5_lam_re"].shape)
    grads["s5_lam_im"] = d_li.reshape(p["s5_lam_im"].shape)
    grads["s5_log_dt"] = d_ldt.reshape(p["s5_log_dt"].shape)
    grads["s5_b_re"] = d_br.T.reshape(p["s5_b_re"].shape)
    grads["s5_b_im"] = d_bi.T.reshape(p["s5_b_im"].shape)
    grads["s5_c_re"] = _diag_blocks(dcr).reshape(p["s5_c_re"].shape)
    grads["s5_c_im"] = -_diag_blocks(dci).reshape(p["s5_c_im"].shape)
    grads["s5_d"] = dd.reshape(p["s5_d"].shape)
    grads["s5_b_glu"] = d_bglu.reshape(1, D_MODEL)
    dconv = jnp.stack([dconv0, dconv1]).reshape(2, N_DEV, 8, -1)
    grads["ffn_conv_w"] = dconv[:, :, 0:3, :].transpose(0, 2, 1, 3)
    grads["ffn_conv_b"] = dconv[:, :, 3, :]
    grads["ln_gain"] = jnp.stack([dg0, dg1, dg2, dg3]).reshape(2, 2, D_MODEL)
    grads["ln_bias"] = jnp.stack([db0, db1, db2, db3]).reshape(2, 2, D_MODEL)
    return loss_tile[0, 0], dx.reshape(x.shape), grads


SMALL_REPLICATED = ("s5_lam_re", "s5_lam_im", "s5_log_dt", "s5_b_re", "s5_b_im", "s5_c_re", "s5_c_im", "s5_d",
                    "rel_bias", "ffn_conv_b")
SMALL_SHARDED = ("s5_b_glu", "ffn_conv_w", "ln_gain", "ln_bias")
BIG = ("s5_w_glu", "s5_w_out", "attn_w_kv", "attn_w_q", "attn_w_out", "ffn_w_up", "ffn_w_down")
WEIGHTS = ("s5_lam_re", "s5_lam_im", "s5_log_dt", "s5_b_re", "s5_b_im", "s5_c_re", "s5_c_im", "s5_d", "s5_w_glu",
           "s5_b_glu", "s5_w_out", "attn_w_kv", "attn_w_q", "attn_w_out", "rel_bias", "ffn_w_up", "ffn_conv_w",
           "ffn_conv_b", "ffn_w_down", "ln_gain", "ln_bias")


def _gather_weights(w):
    small = _pack([w[k] for k in SMALL_SHARDED])
    srcs = [w["s5_w_glu"][0].astype(BF16), w["s5_w_out"][0].astype(BF16), w["attn_w_kv"].astype(BF16),
            w["attn_w_q"][0].astype(BF16), w["attn_w_out"][0].astype(BF16), w["ffn_w_up"].astype(BF16),
            w["ffn_w_down"].astype(BF16), small]
    outs = [_sds((N_DEV,) + srcs[0].shape, BF16), _sds((N_DEV,) + srcs[1].shape, BF16),
            _sds((N_DEV,) + srcs[2].shape, BF16), _sds((N_DEV,) + srcs[3].shape, BF16),
            _sds((N_DEV,) + srcs[4].shape, BF16), _sds((2, N_DEV) + srcs[5].shape[1:], BF16),
            _sds((2, N_DEV) + srcs[6].shape[1:], BF16), _sds((N_DEV,) + small.shape, F32)]
    items = [(0, (), 0, ("me",)), (1, (), 1, ("me",)), (2, (), 2, ("me",)), (3, (), 3, ("me",)),
             (4, (), 4, ("me",)), (5, (0,), 5, (0, "me")), (5, (1,), 5, (1, "me")),
             (6, (0,), 6, (0, "me")), (6, (1,), 6, (1, "me")), (7, (), 7, ("me",))]
    g = _exchange("gather_weights", srcs, outs, items)
    d = D_MODEL
    wfull = {
        "s5_w_glu": g[0].reshape(d, d),
        "s5_w_out": g[1].reshape(d, d),
        "attn_w_kv": g[2],
        "attn_w_q": g[3],
        "attn_w_out": g[4].reshape(d, d),
        "ffn_w_up": g[5],
        "ffn_w_down": g[6].reshape(2, 4, -1, d),
    }
    smalls = [_unpack(g[7][dev], [w[k].shape for k in SMALL_SHARDED]) for dev in range(N_DEV)]
    full_small = {
        "s5_b_glu": jnp.concatenate([s[0] for s in smalls], axis=1),
        "ffn_conv_w": jnp.stack([s[1] for s in smalls], axis=2),
        "ln_gain": jnp.concatenate([s[2] for s in smalls], axis=2),
        "ln_bias": jnp.concatenate([s[3] for s in smalls], axis=2),
    }
    return wfull, full_small


def kernel(x, s5_lam_re, s5_lam_im, s5_log_dt, s5_b_re, s5_b_im, s5_c_re, s5_c_im, s5_d, s5_w_glu, s5_b_glu, s5_w_out, attn_w_kv, attn_w_q, attn_w_out, rel_bias, ffn_w_up, ffn_conv_w, ffn_conv_b, ffn_w_down, ln_gain, ln_bias, loss_target, m_s5_lam_re, m_s5_lam_im, m_s5_log_dt, m_s5_b_re, m_s5_b_im, m_s5_c_re, m_s5_c_im, m_s5_d, m_s5_w_glu, m_s5_b_glu, m_s5_w_out, m_attn_w_kv, m_attn_w_q, m_attn_w_out, m_rel_bias, m_ffn_w_up, m_ffn_conv_w, m_ffn_conv_b, m_ffn_w_down, m_ln_gain, m_ln_bias, v_s5_lam_re, v_s5_lam_im, v_s5_log_dt, v_s5_b_re, v_s5_b_im, v_s5_c_re, v_s5_c_im, v_s5_d, v_s5_w_glu, v_s5_b_glu, v_s5_w_out, v_attn_w_kv, v_attn_w_q, v_attn_w_out, v_rel_bias, v_ffn_w_up, v_ffn_conv_w, v_ffn_conv_b, v_ffn_w_down, v_ln_gain, v_ln_bias):
    w = dict(s5_lam_re=s5_lam_re, s5_lam_im=s5_lam_im, s5_log_dt=s5_log_dt, s5_b_re=s5_b_re, s5_b_im=s5_b_im,
             s5_c_re=s5_c_re, s5_c_im=s5_c_im, s5_d=s5_d, s5_w_glu=s5_w_glu, s5_b_glu=s5_b_glu, s5_w_out=s5_w_out,
             attn_w_kv=attn_w_kv, attn_w_q=attn_w_q, attn_w_out=attn_w_out, rel_bias=rel_bias, ffn_w_up=ffn_w_up,
             ffn_conv_w=ffn_conv_w, ffn_conv_b=ffn_conv_b, ffn_w_down=ffn_w_down, ln_gain=ln_gain, ln_bias=ln_bias)
    mom = dict(s5_lam_re=m_s5_lam_re, s5_lam_im=m_s5_lam_im, s5_log_dt=m_s5_log_dt, s5_b_re=m_s5_b_re,
               s5_b_im=m_s5_b_im, s5_c_re=m_s5_c_re, s5_c_im=m_s5_c_im, s5_d=m_s5_d, s5_w_glu=m_s5_w_glu,
               s5_b_glu=m_s5_b_glu, s5_w_out=m_s5_w_out, attn_w_kv=m_attn_w_kv, attn_w_q=m_attn_w_q,
               attn_w_out=m_attn_w_out, rel_bias=m_rel_bias, ffn_w_up=m_ffn_w_up, ffn_conv_w=m_ffn_conv_w,
               ffn_conv_b=m_ffn_conv_b, ffn_w_down=m_ffn_w_down, ln_gain=m_ln_gain, ln_bias=m_ln_bias)
    var = dict(s5_lam_re=v_s5_lam_re, s5_lam_im=v_s5_lam_im, s5_log_dt=v_s5_log_dt, s5_b_re=v_s5_b_re,
               s5_b_im=v_s5_b_im, s5_c_re=v_s5_c_re, s5_c_im=v_s5_c_im, s5_d=v_s5_d, s5_w_glu=v_s5_w_glu,
               s5_b_glu=v_s5_b_glu, s5_w_out=v_s5_w_out, attn_w_kv=v_attn_w_kv, attn_w_q=v_attn_w_q,
               attn_w_out=v_attn_w_out, rel_bias=v_rel_bias, ffn_w_up=v_ffn_w_up, ffn_conv_w=v_ffn_conv_w,
               ffn_conv_b=v_ffn_conv_b, ffn_w_down=v_ffn_w_down, ln_gain=v_ln_gain, ln_bias=v_ln_bias)
    bsz, seq, _ = x.shape
    me = 4 * lax.axis_index("x") + 2 * lax.axis_index("y") + lax.axis_index("c")

    wfull, full_small = _gather_weights(w)
    c_ff = ffn_conv_w.shape[2]
    conv_w_full = full_small["ffn_conv_w"]
    p = dict(w)
    p.update(ln_gain=full_small["ln_gain"], ln_bias=full_small["ln_bias"], s5_b_glu=full_small["s5_b_glu"])
    p["conv_w"] = conv_w_full.transpose(0, 2, 1, 3).reshape(2, 2, 4, 3, c_ff)
    p["conv_b"] = ffn_conv_b.reshape(2, 2, 4, 1, c_ff)

    loss_part, grad_x, g = _local_step(x, loss_target, p, wfull, bsz, seq)
    loss = lax.psum(loss_part, ("x", "y", "c"))

    d = D_MODEL
    srcs = [g["s5_w_glu"].reshape(N_DEV, d // N_DEV, d), g["s5_w_out"].reshape(N_DEV, d // N_DEV, d),
            g["attn_w_kv"], g["attn_w_q"], g["attn_w_out"].reshape(N_DEV, d // N_DEV, d),
            g["ffn_w_up0"], g["ffn_w_up1"],
            g["ffn_w_down0"].reshape(N_DEV, -1, d), g["ffn_w_down1"].reshape(N_DEV, -1, d)]
    outs = [_sds((N_DEV,) + srcs[0].shape[1:], BF16), _sds((N_DEV,) + srcs[1].shape[1:], BF16),
            _sds((N_DEV,) + srcs[2].shape[1:], BF16), _sds((N_DEV,) + srcs[3].shape[1:], BF16),
            _sds((N_DEV,) + srcs[4].shape[1:], BF16), _sds((N_DEV, 2) + srcs[5].shape[1:], BF16),
            _sds((N_DEV, 2) + srcs[7].shape[1:], BF16)]
    items = [(0, ("peer",), 0, ("me",)), (1, ("peer",), 1, ("me",)), (2, ("peer",), 2, ("me",)),
             (3, ("peer",), 3, ("me",)), (4, ("peer",), 4, ("me",)),
             (5, ("peer",), 5, ("me", 0)), (6, ("peer",), 5, ("me", 1)),
             (7, ("peer",), 6, ("me", 0)), (8, ("peer",), 6, ("me", 1))]
    parts = _exchange("scatter_grads", srcs, outs, items)
    big_parts = dict(zip(BIG, parts))

    results = {}
    for name in BIG:
        shard = w[name]
        r2 = (-1, shard.shape[-1])
        pr = big_parts[name].reshape(N_DEV, -1, shard.shape[-1])
        outs4 = _adamw("adamw_" + name, pr, shard.reshape(r2), mom[name].reshape(r2), var[name].reshape(r2))
        results[name] = tuple(o.reshape(shard.shape) for o in outs4)

    small_names = SMALL_REPLICATED + SMALL_SHARDED
    packed = _pack([g[k] for k in small_names])
    slots = _exchange("allreduce_small", [packed], [_sds((N_DEV,) + packed.shape, F32)],
                      [(0, (), 0, ("me",))])[0]
    total = _unpack(_sum_slots(slots), [g[k].shape for k in small_names])
    gsum = dict(zip(small_names, total))
    mine = {k: gsum[k] for k in SMALL_REPLICATED}
    mine["s5_b_glu"] = lax.dynamic_slice_in_dim(gsum["s5_b_glu"], me * (d // N_DEV), d // N_DEV, axis=1)
    mine["ffn_conv_w"] = lax.dynamic_index_in_dim(gsum["ffn_conv_w"], me, axis=2, keepdims=False)
    mine["ln_gain"] = lax.dynamic_slice_in_dim(gsum["ln_gain"], me * (d // N_DEV), d // N_DEV, axis=2)
    mine["ln_bias"] = lax.dynamic_slice_in_dim(gsum["ln_bias"], me * (d // N_DEV), d // N_DEV, axis=2)
    gp = _pack([mine[k] for k in small_names])
    outs4 = _adamw("adamw_small", gp[None], _pack([w[k] for k in small_names]),
                   _pack([mom[k] for k in small_names]), _pack([var[k] for k in small_names]))
    shapes = [w[k].shape for k in small_names]
    unpacked = [_unpack(o, shapes) for o in outs4]
    for i, k in enumerate(small_names):
        results[k] = tuple(unpacked[j][i] for j in range(4))

    out = [loss, grad_x]
    for j in range(4):
        out.extend(results[k][j] for k in WEIGHTS)
    return tuple(out)
```

```python
import functools
import math

import numpy as np
import jax
import jax.numpy as jnp
from jax import lax
from jax.experimental import pallas as pl
from jax.experimental.pallas import tpu as pltpu

F32 = jnp.float32
BF16 = jnp.bfloat16
HIGHEST = lax.Precision.HIGHEST

N_DEV = 8
D_MODEL = 1024
SSM_GROUP = 16
SSM_STATE = 64
SSM_CHUNKS = 8
CHUNK_CH = D_MODEL // SSM_CHUNKS
CHUNK_ST = CHUNK_CH // SSM_GROUP * SSM_STATE
N_STATE = D_MODEL // SSM_GROUP * SSM_STATE
HEAD_DIM = 64
N_HEADS = 16
BAND = 128
DILATIONS = (1, 4, 16)
REL_BUCKETS = 32
REL_MAX_DIST = 2048
NEG_BIG = -1e30
DN_ALPHA = (2.0 * 2) ** 0.25
LN_EPS = 1e-5
ADAM_LR = 0.001
ADAM_B1 = 0.9
ADAM_B2 = 0.999
ADAM_EPS = 1e-08
ADAM_WD = 0.01
ADAM_STEP = 10

TM = 512
T_SCAN = 256
VMEM_LIMIT = 56 * 1024 * 1024

NN = (((1,), (0,)), ((), ()))
NT = (((1,), (1,)), ((), ()))
TN = (((0,), (0,)), ((), ()))


def _call(body, name, grid, in_specs, out_specs, out_shape, scratch=()):
    return pl.pallas_call(
        body, name=name, grid=grid, in_specs=in_specs, out_specs=out_specs, out_shape=out_shape,
        scratch_shapes=list(scratch),
        compiler_params=pltpu.CompilerParams(
            dimension_semantics=("arbitrary",) * len(grid), vmem_limit_bytes=VMEM_LIMIT),
    )


def _sds(shape, dtype):
    return jax.ShapeDtypeStruct(shape, dtype)


def _mm(name, a, b, *, dims, grid, a_spec, b_spec, o_spec, out_shape, nred=1, red_axis=0,
        add=None, add_spec=None, add_scale=1.0):
    has_add = add is not None
    acc_shape = tuple(s for s in o_spec.block_shape if s is not None)

    def body(*refs):
        if has_add:
            a_ref, b_ref, add_ref, o_ref = refs[:4]
            rest = refs[4:]
        else:
            a_ref, b_ref, o_ref = refs[:3]
            rest = refs[3:]
        prod = lax.dot_general(a_ref[...].astype(BF16), b_ref[...].astype(BF16), dims,
                               preferred_element_type=F32)

        def finish(val):
            if has_add:
                val = val + add_scale * add_ref[...].astype(F32)
            o_ref[...] = val.astype(o_ref.dtype)

        if nred == 1:
            finish(prod)
        else:
            acc = rest[0]
            k = pl.program_id(red_axis)

            @pl.when(k == 0)
            def _():
                acc[...] = prod

            @pl.when(k > 0)
            def _():
                acc[...] += prod

            @pl.when(k == nred - 1)
            def _():
                finish(acc[...])

    in_specs = [a_spec, b_spec] + ([add_spec] if has_add else [])
    scratch = [pltpu.VMEM(acc_shape, F32)] if nred > 1 else []
    args = (a, b) + ((add,) if has_add else ())
    return _call(body, name, grid, in_specs, o_spec, out_shape, scratch)(*args)


def _mm_nn(name, a, b, out_dtype, blocked_out=False):
    m, k = a.shape
    nb, _, n = b.shape
    if blocked_out:
        o_spec = pl.BlockSpec((None, TM, n), lambda d, i: (d, i, 0))
        out_shape = _sds((nb, m, n), out_dtype)
    else:
        o_spec = pl.BlockSpec((TM, n), lambda d, i: (i, d))
        out_shape = _sds((m, nb * n), out_dtype)
    return _mm(name, a, b, dims=NN, grid=(nb, m // TM),
               a_spec=pl.BlockSpec((TM, k), lambda d, i: (i, 0)),
               b_spec=pl.BlockSpec((None, k, n), lambda d, i: (d, 0, 0)),
               o_spec=o_spec, out_shape=out_shape)


def _mm_nn_red(name, a, b, out_dtype):
    nb, m, k = a.shape
    n = b.shape[2]
    return _mm(name, a, b, dims=NN, grid=(m // TM, nb), nred=nb, red_axis=1,
               a_spec=pl.BlockSpec((None, TM, k), lambda i, d: (d, i, 0)),
               b_spec=pl.BlockSpec((None, k, n), lambda i, d: (d, 0, 0)),
               o_spec=pl.BlockSpec((TM, n), lambda i, d: (i, 0)),
               out_shape=_sds((m, n), out_dtype))


def _mm_tn(name, a, g, nb, out_dtype, a_blocked=False, g_mode="cols"):
    if a_blocked:
        _, m, ka = a.shape
        a_spec = pl.BlockSpec((None, TM, ka), lambda d, i: (d, i, 0))
    else:
        m, ka = a.shape
        a_spec = pl.BlockSpec((TM, ka), lambda d, i: (i, 0))
    if g_mode == "cols":
        n = g.shape[1] // nb
        g_spec = pl.BlockSpec((TM, n), lambda d, i: (i, d))
    elif g_mode == "blocked":
        n = g.shape[2]
        g_spec = pl.BlockSpec((None, TM, n), lambda d, i: (d, i, 0))
    else:
        n = g.shape[1]
        g_spec = pl.BlockSpec((TM, n), lambda d, i: (i, 0))
    nred = m // TM
    return _mm(name, a, g, dims=TN, grid=(nb, nred), nred=nred, red_axis=1,
               a_spec=a_spec, b_spec=g_spec,
               o_spec=pl.BlockSpec((None, ka, n), lambda d, i: (d, 0, 0)),
               out_shape=_sds((nb, ka, n), out_dtype))


def _mm_nt_red(name, g, w, out_dtype, g_blocked, add=None, add_scale=1.0):
    nb, k, n = w.shape
    if g_blocked:
        m = g.shape[1]
        g_spec = pl.BlockSpec((None, TM, n), lambda i, d: (d, i, 0))
    else:
        m = g.shape[0]
        g_spec = pl.BlockSpec((TM, n), lambda i, d: (i, d))
    o_spec = pl.BlockSpec((TM, k), lambda i, d: (i, 0))
    return _mm(name, g, w, dims=NT, grid=(m // TM, nb), nred=nb, red_axis=1,
               a_spec=g_spec, b_spec=pl.BlockSpec((None, k, n), lambda i, d: (d, 0, 0)),
               o_spec=o_spec, out_shape=_sds((m, k), out_dtype),
               add=add, add_spec=o_spec if add is not None else None, add_scale=add_scale)


def _mm_nt_out(name, g, w, out_dtype):
    m, n = g.shape
    nb, k, _ = w.shape
    return _mm(name, g, w, dims=NT, grid=(nb, m // TM),
               a_spec=pl.BlockSpec((TM, n), lambda d, i: (i, 0)),
               b_spec=pl.BlockSpec((None, k, n), lambda d, i: (d, 0, 0)),
               o_spec=pl.BlockSpec((None, TM, k), lambda d, i: (d, i, 0)),
               out_shape=_sds((nb, m, k), out_dtype))


def _ln_stats(x, f):
    r = DN_ALPHA * x + f
    mu = jnp.mean(r, axis=-1, keepdims=True)
    xc = r - mu
    var = jnp.mean(xc * xc, axis=-1, keepdims=True)
    rstd = lax.rsqrt(var + LN_EPS)
    return xc * rstd, rstd


def _ln_fwd(name, xin, f, gain, bias):
    n, d = xin.shape

    def body(x_ref, f_ref, g_ref, b_ref, o_ref):
        xhat, _ = _ln_stats(x_ref[...], f_ref[...])
        o_ref[...] = xhat * g_ref[...] + b_ref[...]

    row = pl.BlockSpec((TM, d), lambda i: (i, 0))
    vec = pl.BlockSpec((1, d), lambda i: (0, 0))
    return _call(body, name, (n // TM,), [row, row, vec, vec], row, _sds((n, d), F32))(xin, f, gain, bias)


def _ln_bwd(name, xin, f, dy, gain):
    n, d = xin.shape

    def body(x_ref, f_ref, dy_ref, g_ref, dr_ref, dg_ref, db_ref):
        i = pl.program_id(0)
        xhat, rstd = _ln_stats(x_ref[...], f_ref[...])
        dy_v = dy_ref[...]
        dxh = dy_v * g_ref[...]
        m1 = jnp.mean(dxh, axis=-1, keepdims=True)
        m2 = jnp.mean(dxh * xhat, axis=-1, keepdims=True)
        dr_ref[...] = rstd * (dxh - m1 - xhat * m2)
        dg = jnp.sum(dy_v * xhat, axis=0, keepdims=True)
        db = jnp.sum(dy_v, axis=0, keepdims=True)

        @pl.when(i == 0)
        def _():
            dg_ref[...] = dg
            db_ref[...] = db

        @pl.when(i > 0)
        def _():
            dg_ref[...] += dg
            db_ref[...] += db

    row = pl.BlockSpec((TM, d), lambda i: (i, 0))
    vec = pl.BlockSpec((1, d), lambda i: (0, 0))
    return _call(body, name, (n // TM,), [row, row, row, vec], [row, vec, vec],
                 [_sds((n, d), F32), _sds((1, d), F32), _sds((1, d), F32)])(xin, f, dy, gain)


def _loss_head(y, target):
    n, d = y.shape

    def body(y_ref, t_ref, l_ref, dy_ref):
        i = pl.program_id(0)
        e = y_ref[...] - t_ref[...]
        dy_ref[...] = e * (1.0 / d)
        part = jnp.sum(jnp.sum(e * e, axis=1, keepdims=True), axis=0, keepdims=True) * (0.5 / d)
        part = jnp.broadcast_to(part, (8, 128))

        @pl.when(i == 0)
        def _():
            l_ref[...] = part

        @pl.when(i > 0)
        def _():
            l_ref[...] += part

    row = pl.BlockSpec((TM, d), lambda i: (i, 0))
    return _call(body, "loss_head", (n // TM,), [row, row],
                 [pl.BlockSpec((8, 128), lambda i: (0, 0)), row],
                 [_sds((8, 128), F32), _sds((n, d), F32)])(y, target)


def _group_expand_matrix():
    e = np.zeros((D_MODEL // SSM_GROUP, N_STATE), np.float32)
    for g in range(D_MODEL // SSM_GROUP):
        e[g, g * SSM_STATE:(g + 1) * SSM_STATE] = 1.0
    return jnp.asarray(e)


def _discretise(lr, li, ldt, br, bi, expand):
    dt = jnp.dot(jnp.exp(ldt), expand, precision=HIGHEST, preferred_element_type=F32)
    mag = jnp.exp(lr * dt)
    th = li * dt
    ab_r = mag * jnp.cos(th)
    ab_i = mag * jnp.sin(th)
    den = lr * lr + li * li
    nr = ab_r - 1.0
    co_r = (nr * lr + ab_i * li) / den
    co_i = (ab_i * lr - nr * li) / den
    bb_r = co_r * br - co_i * bi
    bb_i = co_r * bi + co_i * br
    return ab_r, ab_i, bb_r, bb_i


def _cmul(ar, ai, br, bi):
    return ar * br - ai * bi, ar * bi + ai * br


def _s5_prep(lr, li, ldt, br, bi):
    s = N_STATE
    expand = _group_expand_matrix()

    def body(lr_ref, li_ref, ldt_ref, br_ref, bi_ref, e_ref, abr_ref, abi_ref, bbr_ref, bbi_ref, tab_ref):
        ab_r, ab_i, bb_r, bb_i = _discretise(lr_ref[...], li_ref[...], ldt_ref[...], br_ref[...],
                                             bi_ref[...], e_ref[...])
        abr_ref[...] = ab_r
        abi_ref[...] = ab_i
        bbr_ref[...] = bb_r
        bbi_ref[...] = bb_i
        row = lax.broadcasted_iota(jnp.int32, (8, s), 0)
        for base, sign in ((0, 1.0), (8, -1.0)):
            p = [None] * 9
            p[1] = (ab_r, sign * ab_i)
            p[2] = _cmul(*p[1], *p[1])
            p[3] = _cmul(*p[2], *p[1])
            p[4] = _cmul(*p[2], *p[2])
            p[5] = _cmul(*p[4], *p[1])
            p[6] = _cmul(*p[4], *p[2])
            p[7] = _cmul(*p[4], *p[3])
            p[8] = _cmul(*p[4], *p[4])
            for j, k in enumerate((1, 2, 4)):
                keep = (row >= k) if base == 0 else (row < 8 - k)
                tab_ref[base + 2 * j] = jnp.where(keep, jnp.broadcast_to(p[k][0], (8, s)), 0.0)
                tab_ref[base + 2 * j + 1] = jnp.where(keep, jnp.broadcast_to(p[k][1], (8, s)), 0.0)
            pw_r = jnp.zeros((8, s), F32)
            pw_i = jnp.zeros((8, s), F32)
            for i in range(8):
                e = i + 1 if base == 0 else 8 - i
                pw_r = jnp.where(row == i, jnp.broadcast_to(p[e][0], (8, s)), pw_r)
                pw_i = jnp.where(row == i, jnp.broadcast_to(p[e][1], (8, s)), pw_i)
            tab_ref[base + 6] = pw_r
            tab_ref[base + 7] = pw_i

    def full(shape):
        return pl.BlockSpec(shape, lambda i: (0,) * len(shape))

    g = D_MODEL // SSM_GROUP
    return _call(body, "s5_prep", (1,),
                 [full((1, s)), full((1, s)), full((1, g)), full((16, s)), full((16, s)), full((g, s))],
                 [full((1, s)), full((1, s)), full((16, s)), full((16, s)), full((16, 8, s))],
                 [_sds((1, s), F32), _sds((1, s), F32), _sds((16, s), F32), _sds((16, s), F32),
                  _sds((16, 8, s), F32)])(lr, li, ldt, br, bi, expand)


def _s5_prep_bwd(lr, li, ldt, br, bi, d_abr, d_abi, d_bbr, d_bbi):
    s = N_STATE
    g = D_MODEL // SSM_GROUP
    expand = _group_expand_matrix()

    def body(lr_ref, li_ref, ldt_ref, br_ref, bi_ref, e_ref, c1, c2, c3, c4, o1, o2, o3, o4, o5):
        e_val = e_ref[...]
        _, vjp = jax.vjp(lambda a, b, c, d, e: _discretise(a, b, c, d, e, e_val),
                         lr_ref[...], li_ref[...], ldt_ref[...], br_ref[...], bi_ref[...])
        grads = vjp((c1[...], c2[...], c3[...], c4[...]))
        for o, gr in zip((o1, o2, o3, o4, o5), grads):
            o[...] = gr

    def full(shape):
        return pl.BlockSpec(shape, lambda i: (0,) * len(shape))

    return _call(body, "s5_prep_bwd", (1,),
                 [full((1, s)), full((1, s)), full((1, g)), full((16, s)), full((16, s)), full((g, s)),
                  full((1, s)), full((1, s)), full((16, s)), full((16, s))],
                 [full((1, s)), full((1, s)), full((1, g)), full((16, s)), full((16, s))],
                 [_sds((1, s), F32), _sds((1, s), F32), _sds((1, g), F32), _sds((16, s), F32),
                  _sds((16, s), F32)])(lr, li, ldt, br, bi, expand, d_abr, d_abi, d_bbr, d_bbi)


def _scan8(vr, vi, tab_ref, base, reverse):
    for j, k in enumerate((1, 2, 4)):
        mr = tab_ref[base + 2 * j]
        mi = tab_ref[base + 2 * j + 1]
        shift = 8 - k if reverse else k
        sr = pltpu.roll(vr, shift, 0)
        si = pltpu.roll(vi, shift, 0)
        vr, vi = vr + mr * sr - mi * si, vi + mr * si + mi * sr
    return vr, vi


def _s5_scan_fwd(x, bbr, bbi, cr, ci, dvec, tab, seq):
    n, d = x.shape
    nt = seq // T_SCAN
    nblk = T_SCAN // 8
    st = CHUNK_ST

    def body(x_ref, bbr_ref, bbi_ref, cr_ref, ci_ref, d_ref, tab_ref, y_ref, hr_ref, hi_ref, car_r, car_i):
        t = pl.program_id(2)

        @pl.when(t == 0)
        def _():
            car_r[...] = jnp.zeros_like(car_r)
            car_i[...] = jnp.zeros_like(car_i)

        u = x_ref[...]
        ub = u.astype(BF16)
        hr_ref[...] = jnp.dot(ub, bbr_ref[...], preferred_element_type=F32)
        hi_ref[...] = jnp.dot(ub, bbi_ref[...], preferred_element_type=F32)

        def step(i, carry):
            sl = pl.ds(pl.multiple_of(i * 8, 8), 8)
            vr, vi = _scan8(hr_ref[sl, :], hi_ref[sl, :], tab_ref, 0, False)
            c_r = jnp.broadcast_to(car_r[...], (8, st))
            c_i = jnp.broadcast_to(car_i[...], (8, st))
            pr = tab_ref[6]
            pi = tab_ref[7]
            vr, vi = vr + pr * c_r - pi * c_i, vi + pr * c_i + pi * c_r
            hr_ref[sl, :] = vr
            hi_ref[sl, :] = vi
            car_r[...] = vr[7:8, :]
            car_i[...] = vi[7:8, :]
            return carry

        lax.fori_loop(0, nblk, step, 0)
        y = lax.dot_general(hr_ref[...].astype(BF16), cr_ref[...], NT, preferred_element_type=F32)
        y = y - lax.dot_general(hi_ref[...].astype(BF16), ci_ref[...], NT, preferred_element_type=F32)
        y_ref[...] = y + d_ref[...] * u

    row = lambda c, b, t: (b * nt + t, c)
    mat = pl.BlockSpec((None, CHUNK_CH, st), lambda c, b, t: (c, 0, 0))
    return _call(
        body, "s5_scan_fwd", (SSM_CHUNKS, n // seq, nt),
        [pl.BlockSpec((T_SCAN, CHUNK_CH), row), mat, mat, mat, mat,
         pl.BlockSpec((1, CHUNK_CH), lambda c, b, t: (0, c)),
         pl.BlockSpec((16, 8, st), lambda c, b, t: (0, 0, c))],
        [pl.BlockSpec((T_SCAN, CHUNK_CH), row), pl.BlockSpec((T_SCAN, st), row), pl.BlockSpec((T_SCAN, st), row)],
        [_sds((n, d), F32), _sds((n, N_STATE), F32), _sds((n, N_STATE), F32)],
        [pltpu.VMEM((1, st), F32), pltpu.VMEM((1, st), F32)],
    )(x, bbr, bbi, cr, ci, dvec, tab)


def _s5_scan_bwd(x, dy, dres, hr, hi, bbr, bbi, cr, ci, dvec, tab, seq):
    n, d = x.shape
    nt = seq // T_SCAN
    nblk = T_SCAN // 8
    st = CHUNK_ST

    def body(x_ref, dy_ref, dres_ref, hr_ref, hi_ref, hpr_ref, hpi_ref, bbr_ref, bbi_ref, cr_ref, ci_ref,
             d_ref, tab_ref, dx_ref, dcr_ref, dci_ref, dbr_ref, dbi_ref, dar_ref, dai_ref, dd_ref,
             g_r, g_i, car_r, car_i):
        b = pl.program_id(1)
        tg = pl.program_id(2)
        first = jnp.logical_and(b == 0, tg == 0)

        @pl.when(tg == 0)
        def _():
            car_r[...] = jnp.zeros_like(car_r)
            car_i[...] = jnp.zeros_like(car_i)

        u = x_ref[...]
        dyv = dy_ref[...]
        ub = u.astype(BF16)
        dyb = dyv.astype(BF16)
        g_r[...] = jnp.dot(dyb, cr_ref[...], preferred_element_type=F32)
        g_i[...] = -jnp.dot(dyb, ci_ref[...], preferred_element_type=F32)

        def step(ii, carry):
            i = nblk - 1 - ii
            sl = pl.ds(pl.multiple_of(i * 8, 8), 8)
            vr, vi = _scan8(g_r[sl, :], g_i[sl, :], tab_ref, 8, True)
            c_r = jnp.broadcast_to(car_r[...], (8, st))
            c_i = jnp.broadcast_to(car_i[...], (8, st))
            pr = tab_ref[14]
            pi = tab_ref[15]
            vr, vi = vr + pr * c_r - pi * c_i, vi + pr * c_i + pi * c_r
            g_r[sl, :] = vr
            g_i[sl, :] = vi
            car_r[...] = vr[0:1, :]
            car_i[...] = vi[0:1, :]
            return carry

        lax.fori_loop(0, nblk, step, 0)
        gr = g_r[...]
        gi = g_i[...]
        grb = gr.astype(BF16)
        gib = gi.astype(BF16)
        du = lax.dot_general(grb, bbr_ref[...], NT, preferred_element_type=F32)
        du = du + lax.dot_general(gib, bbi_ref[...], NT, preferred_element_type=F32)
        dx_ref[...] = DN_ALPHA * dres_ref[...] + du + d_ref[...] * dyv

        h_r = hr_ref[...]
        h_i = hi_ref[...]
        t_orig = nt - 1 - tg
        row = lax.broadcasted_iota(jnp.int32, (T_SCAN, st), 0)
        halo_ok = t_orig > 0
        top_r = jnp.where(halo_ok, hpr_ref[7:8, :], 0.0)
        top_i = jnp.where(halo_ok, hpi_ref[7:8, :], 0.0)
        hp_r = jnp.where(row == 0, jnp.broadcast_to(top_r, (T_SCAN, st)), pltpu.roll(h_r, 1, 0))
        hp_i = jnp.where(row == 0, jnp.broadcast_to(top_i, (T_SCAN, st)), pltpu.roll(h_i, 1, 0))
        dar = jnp.sum(gr * hp_r + gi * hp_i, axis=0, keepdims=True)
        dai = jnp.sum(gi * hp_r - gr * hp_i, axis=0, keepdims=True)
        dcr = lax.dot_general(dyb, h_r.astype(BF16), TN, preferred_element_type=F32)
        dci = lax.dot_general(dyb, h_i.astype(BF16), TN, preferred_element_type=F32)
        dbr = lax.dot_general(ub, grb, TN, preferred_element_type=F32)
        dbi = lax.dot_general(ub, gib, TN, preferred_element_type=F32)
        dd = jnp.sum(dyv * u, axis=0, keepdims=True)

        @pl.when(first)
        def _():
            dcr_ref[...] = dcr
            dci_ref[...] = dci
            dbr_ref[...] = dbr
            dbi_ref[...] = dbi
            dar_ref[...] = dar
            dai_ref[...] = dai
            dd_ref[...] = dd

        @pl.when(jnp.logical_not(first))
        def _():
            dcr_ref[...] += dcr
            dci_ref[...] += dci
            dbr_ref[...] += dbr
            dbi_ref[...] += dbi
            dar_ref[...] += dar
            dai_ref[...] += dai
            dd_ref[...] += dd

    row = lambda c, b, t: (b * nt + (nt - 1 - t), c)
    halo = lambda c, b, t: (jnp.maximum((b * nt + (nt - 1 - t)) * (T_SCAN // 8) - 1, 0), c)
    mat = pl.BlockSpec((None, CHUNK_CH, st), lambda c, b, t: (c, 0, 0))
    chan = pl.BlockSpec((T_SCAN, CHUNK_CH), row)
    state = pl.BlockSpec((T_SCAN, st), row)
    return _call(
        body, "s5_scan_bwd", (SSM_CHUNKS, n // seq, nt),
        [chan, chan, chan, state, state, pl.BlockSpec((8, st), halo), pl.BlockSpec((8, st), halo),
         mat, mat, mat, mat, pl.BlockSpec((1, CHUNK_CH), lambda c, b, t: (0, c)),
         pl.BlockSpec((16, 8, st), lambda c, b, t: (0, 0, c))],
        [chan, mat, mat, mat, mat, pl.BlockSpec((1, st), lambda c, b, t: (0, c)),
         pl.BlockSpec((1, st), lambda c, b, t: (0, c)), pl.BlockSpec((1, CHUNK_CH), lambda c, b, t: (0, c))],
        [_sds((n, d), F32)] + [_sds((SSM_CHUNKS, CHUNK_CH, st), F32)] * 4
        + [_sds((1, N_STATE), F32), _sds((1, N_STATE), F32), _sds((1, d), F32)],
        [pltpu.VMEM((T_SCAN, st), F32), pltpu.VMEM((T_SCAN, st), F32), pltpu.VMEM((1, st), F32),
         pltpu.VMEM((1, st), F32)],
    )(x, dy, dres, hr, hi, hr, hi, bbr, bbi, cr, ci, dvec, tab)


_GELU_C = math.sqrt(2.0 / math.pi)


def _gelu_parts(y):
    t = jnp.tanh(_GELU_C * (y + 0.044715 * (y * y * y)))
    return 0.5 * (1.0 + t), t


def _glu_fwd(y, w_glu, b_glu):
    n, d = y.shape

    def body(y_ref, w_ref, b_ref, y2_ref, z_ref, gg_ref):
        yv = y_ref[...]
        cdf, _ = _gelu_parts(yv)
        y2 = yv * cdf
        z = jnp.dot(y2.astype(BF16), w_ref[...], preferred_element_type=F32) + b_ref[...]
        y2_ref[...] = y2.astype(BF16)
        z_ref[...] = z
        gg_ref[...] = (y2 * jax.nn.sigmoid(z)).astype(BF16)

    row = pl.BlockSpec((TM, d), lambda i: (i, 0))
    return _call(body, "glu_fwd", (n // TM,),
                 [row, pl.BlockSpec((d, d), lambda i: (0, 0)), pl.BlockSpec((1, d), lambda i: (0, 0))],
                 [row, row, row], [_sds((n, d), BF16), _sds((n, d), F32), _sds((n, d), BF16)])(y, w_glu, b_glu)


def _glu_bwd_gate(y, z, dgg):
    n, d = y.shape

    def body(y_ref, z_ref, dgg_ref, dz_ref, db_ref, t_ref):
        i = pl.program_id(0)
        yv = y_ref[...]
        cdf, _ = _gelu_parts(yv)
        y2 = yv * cdf
        s = jax.nn.sigmoid(z_ref[...])
        dg = dgg_ref[...]
        dz = dg * y2 * s * (1.0 - s)
        dz_ref[...] = dz.astype(BF16)
        t_ref[...] = dg * s
        db = jnp.sum(dz, axis=0, keepdims=True)

        @pl.when(i == 0)
        def _():
            db_ref[...] = db

        @pl.when(i > 0)
        def _():
            db_ref[...] += db

    row = pl.BlockSpec((TM, d), lambda i: (i, 0))
    vec = pl.BlockSpec((1, d), lambda i: (0, 0))
    return _call(body, "glu_bwd_gate", (n // TM,), [row, row, row], [row, vec, row],
                 [_sds((n, d), BF16), _sds((1, d), F32), _sds((n, d), F32)])(y, z, dgg)


def _gelu_bwd(y, dy2):
    n, d = y.shape

    def body(y_ref, g_ref, o_ref):
        yv = y_ref[...]
        cdf, t = _gelu_parts(yv)
        dinner = _GELU_C * (1.0 + 3.0 * 0.044715 * yv * yv)
        o_ref[...] = g_ref[...] * (cdf + 0.5 * yv * (1.0 - t * t) * dinner)

    row = pl.BlockSpec((TM, d), lambda i: (i, 0))
    return _call(body, "gelu_bwd", (n // TM,), [row, row], row, _sds((n, d), F32))(y, dy2)


def _shift_down(x, halo, k, rows):
    out = pltpu.roll(x, k, 0)
    row = lax.broadcasted_iota(jnp.int32, x.shape, 0)
    for j in range(k):
        out = jnp.where(row == j, jnp.broadcast_to(halo[8 - k + j:8 - k + j + 1, :], x.shape), out)
    return out


def _conv3(x, halo, w, b):
    return (b + w[0:1, :] * x + w[1:2, :] * _shift_down(x, halo, 1, None)
            + w[2:3, :] * _shift_down(x, halo, 2, None))


def _conv_gate_fwd(hc, conv_w, conv_b, seq):
    _, nb, n, c = hc.shape
    tiles_per_seq = seq // TM

    def body(x_ref, halo_ref, w_ref, b_ref, o_ref):
        i = pl.program_id(1)
        start = (i % tiles_per_seq) == 0
        cv = []
        for h in range(2):
            halo = jnp.where(start, 0.0, halo_ref[h])
            cv.append(_conv3(x_ref[h], halo, w_ref[h], b_ref[h]))
        o_ref[...] = (cv[1] * jax.nn.sigmoid(cv[1]) * cv[0]).astype(BF16)

    return _call(
        body, "conv_gate_fwd", (nb, n // TM),
        [pl.BlockSpec((2, None, TM, c), lambda d, i: (0, d, i, 0)),
         pl.BlockSpec((2, None, 8, c), lambda d, i: (0, d, jnp.maximum(i * (TM // 8) - 1, 0), 0)),
         pl.BlockSpec((2, None, 3, c), lambda d, i: (0, d, 0, 0)),
         pl.BlockSpec((2, None, 1, c), lambda d, i: (0, d, 0, 0))],
        pl.BlockSpec((None, TM, c), lambda d, i: (d, i, 0)),
        _sds((nb, n, c), BF16))(hc, hc, conv_w, conv_b)


def _conv_gate_bwd(hc, dact, conv_w, conv_b, seq):
    _, nb, n, c = hc.shape
    tiles_per_seq = seq // TM
    last_halo = n // 8 - 1
    ext = TM + 8

    def body(x_ref, prev_ref, next_ref, da_ref, dan_ref, w_ref, b_ref, dx_ref, dw_ref):
        i = pl.program_id(1)
        start = (i % tiles_per_seq) == 0
        end = (i % tiles_per_seq) == tiles_per_seq - 1
        row = lax.broadcasted_iota(jnp.int32, (ext, c), 0)
        xe, cv = [], []
        for h in range(2):
            halo = jnp.where(start, 0.0, prev_ref[h])
            x_ext = jnp.concatenate([x_ref[h], next_ref[h]], axis=0)
            xe.append((x_ext, halo))
            cv.append(_conv3(x_ext, halo, w_ref[h], b_ref[h]))
        da = jnp.concatenate([da_ref[...], dan_ref[...]], axis=0)
        da = jnp.where(jnp.logical_and(end, row >= TM), 0.0, da)
        sg = jax.nn.sigmoid(cv[1])
        silu = cv[1] * sg
        dcv = [da * silu, da * cv[0] * (sg + silu * (1.0 - sg))]
        for h in range(2):
            w = w_ref[h]
            g = dcv[h]
            dx = w[0:1, :] * g + w[1:2, :] * pltpu.roll(g, ext - 1, 0) + w[2:3, :] * pltpu.roll(g, ext - 2, 0)
            dx_ref[h] = dx[:TM, :].astype(BF16)
            x_ext, halo = xe[h]
            gt = g[:TM, :]
            xt = x_ext[:TM, :]
            parts = [jnp.sum(gt * xt, axis=0, keepdims=True),
                     jnp.sum(gt * _shift_down(xt, halo, 1, None), axis=0, keepdims=True),
                     jnp.sum(gt * _shift_down(xt, halo, 2, None), axis=0, keepdims=True),
                     jnp.sum(gt, axis=0, keepdims=True)]
            upd = jnp.concatenate(parts + [jnp.zeros((4, c), F32)], axis=0)

            @pl.when(i == 0)
            def _():
                dw_ref[h] = upd

            @pl.when(i > 0)
            def _():
                dw_ref[h] += upd

    nxt = lambda i: jnp.minimum((i + 1) * (TM // 8), last_halo)
    return _call(
        body, "conv_gate_bwd", (nb, n // TM),
        [pl.BlockSpec((2, None, TM, c), lambda d, i: (0, d, i, 0)),
         pl.BlockSpec((2, None, 8, c), lambda d, i: (0, d, jnp.maximum(i * (TM // 8) - 1, 0), 0)),
         pl.BlockSpec((2, None, 8, c), lambda d, i: (0, d, nxt(i), 0)),
         pl.BlockSpec((None, TM, c), lambda d, i: (d, i, 0)),
         pl.BlockSpec((None, 8, c), lambda d, i: (d, nxt(i), 0)),
         pl.BlockSpec((2, None, 3, c), lambda d, i: (0, d, 0, 0)),
         pl.BlockSpec((2, None, 1, c), lambda d, i: (0, d, 0, 0))],
        [pl.BlockSpec((2, None, TM, c), lambda d, i: (0, d, i, 0)),
         pl.BlockSpec((2, None, 8, c), lambda d, i: (0, d, 0, 0))],
        [_sds((2, nb, n, c), BF16), _sds((2, nb, 8, c), F32)])(hc, hc, hc, dact, dact, conv_w, conv_b)


def _t5_bucket_np(dist):
    exact = REL_BUCKETS // 2
    d = np.maximum(dist, 1).astype(np.float32)
    large = exact + (np.log(d / exact) / math.log(REL_MAX_DIST / exact) * (REL_BUCKETS - exact)).astype(np.int64)
    large = np.minimum(large, REL_BUCKETS - 1)
    return np.where(dist < exact, dist, large).astype(np.int32)


def _bucket_table(dil):
    steps = np.arange(BAND)[:, None] + BAND - np.arange(2 * BAND)[None, :]
    return _t5_bucket_np(np.maximum(steps, 0) * dil).reshape(1, BAND * 2 * BAND).astype(np.float32)


def _bias_expand(rel_t, dil):
    nk = BAND * 2 * BAND
    bucket = jnp.asarray(_bucket_table(dil))

    def body(r_ref, bk_ref, o_ref):
        ids = lax.broadcasted_iota(jnp.int32, (REL_BUCKETS, nk), 0).astype(F32)
        onehot = (ids == bk_ref[...]).astype(F32)
        o_ref[...] = jnp.dot(r_ref[...], onehot, precision=HIGHEST, preferred_element_type=F32)

    return _call(body, "bias_expand", (1,),
                 [pl.BlockSpec((N_HEADS, REL_BUCKETS), lambda i: (0, 0)), pl.BlockSpec((1, nk), lambda i: (0, 0))],
                 pl.BlockSpec((N_HEADS, nk), lambda i: (0, 0)), _sds((N_HEADS, nk), F32))(rel_t, bucket)


def _bias_reduce(dbias, dil):
    nk = BAND * 2 * BAND
    bucket = jnp.asarray(_bucket_table(dil))

    def body(g_ref, bk_ref, o_ref):
        ids = lax.broadcasted_iota(jnp.int32, (REL_BUCKETS, nk), 0).astype(F32)
        onehot = (ids == bk_ref[...]).astype(F32)
        o_ref[...] = lax.dot_general(g_ref[...], onehot, NT, precision=HIGHEST, preferred_element_type=F32)

    return _call(body, "bias_reduce", (1,),
                 [pl.BlockSpec((N_HEADS, nk), lambda i: (0, 0)), pl.BlockSpec((1, nk), lambda i: (0, 0))],
                 pl.BlockSpec((N_HEADS, REL_BUCKETS), lambda i: (0, 0)),
                 _sds((N_HEADS, REL_BUCKETS), F32))(dbias, bucket)


def _valid_mask(n):
    qi = lax.broadcasted_iota(jnp.int32, (BAND, 2 * BAND), 0)
    kj = lax.broadcasted_iota(jnp.int32, (BAND, 2 * BAND), 1)
    steps = qi + BAND - kj
    in_band = jnp.logical_and(steps >= 0, steps <= BAND)
    return jnp.logical_and(in_band, jnp.logical_or(n > 0, kj >= BAND))


ATTN_COLS = {1: 1024, 4: 512, 16: 128}


def _residue(dil, r):
    return slice(None) if dil == 1 else pl.ds(r, BAND, stride=dil)


def _attn_fwd(q, k, v, bias, g, dil, bsz, seq):
    n = q.shape[0]
    rows = BAND * dil
    nch = seq // rows
    w = N_HEADS * HEAD_DIM
    wc = ATTN_COLS[dil]
    ncol = w // wc
    hpc = wc // HEAD_DIM
    nj = wc // 128
    slab = lambda k: pltpu.VMEM((k * nj, rows, 128), F32)

    def body(q_ref, kc_ref, kp_ref, vc_ref, vp_ref, b_ref, o_ref, l_ref, qf, kf, vf, of, lf):
        ch = pl.program_id(2)
        valid = _valid_mask(ch)
        lane = lax.broadcasted_iota(jnp.int32, (BAND, 128), 1)
        for j in range(nj):
            sl = slice(128 * j, 128 * (j + 1))
            qf[j] = q_ref[:, sl].astype(F32)
            kf[j] = kp_ref[:, sl].astype(F32)
            kf[nj + j] = kc_ref[:, sl].astype(F32)
            vf[j] = vp_ref[:, sl].astype(F32)
            vf[nj + j] = vc_ref[:, sl].astype(F32)
        for r in range(dil):
            rr = _residue(dil, r)
            for j in range(nj):
                q2 = qf.at[j][rr, :].astype(BF16)
                k2 = jnp.concatenate([kf.at[j][rr, :], kf.at[nj + j][rr, :]], axis=0).astype(BF16)
                v2 = jnp.concatenate([vf.at[j][rr, :], vf.at[nj + j][rr, :]], axis=0).astype(BF16)
                outs, lses = [], []
                for half in range(2):
                    mine = (lane < HEAD_DIM) if half == 0 else (lane >= HEAD_DIM)
                    qh = jnp.where(mine, q2, jnp.zeros_like(q2))
                    s = lax.dot_general(qh, k2, NT, preferred_element_type=F32) * (HEAD_DIM ** -0.5)
                    s = jnp.where(valid, s + b_ref[2 * j + half], NEG_BIG)
                    m = jnp.max(s, axis=1, keepdims=True)
                    p = jnp.exp(s - m)
                    l = jnp.sum(p, axis=1, keepdims=True)
                    pv = jnp.dot(p.astype(BF16), v2, preferred_element_type=F32)
                    outs.append(pv / l)
                    lses.append(jnp.broadcast_to(m + jnp.log(l), (BAND, 128)))
                of.at[j][rr, :] = jnp.where(lane < HEAD_DIM, outs[0], outs[1])
                lf.at[j][rr, :] = jnp.where(lane < HEAD_DIM, lses[0], lses[1])
        for j in range(nj):
            sl = slice(128 * j, 128 * (j + 1))
            o_ref[:, sl] = of[j]
            l_ref[:, sl] = lf[j]

    base = g * ncol
    cur = lambda c, b, ch: (b * nch + ch, base + c)
    prev = lambda c, b, ch: (b * nch + jnp.maximum(ch - 1, 0), base + c)
    blk = lambda im: pl.BlockSpec((rows, wc), im)
    out_blk = pl.BlockSpec((rows, wc), lambda c, b, ch: (b * nch + ch, c))
    return _call(
        body, "attn_fwd_d%d" % dil, (ncol, bsz, nch),
        [blk(cur), blk(cur), blk(prev), blk(cur), blk(prev),
         pl.BlockSpec((hpc, BAND, 2 * BAND), lambda c, b, ch: (c, 0, 0))],
        [out_blk, out_blk],
        [_sds((n, w), F32)] * 2,
        [slab(1), slab(2), slab(2), slab(1), slab(1)],
    )(q, k, k, v, v, bias)


def _attn_combine(outs, lses):
    n, w = outs[0].shape

    def body(o0, o1, o2, l0, l1, l2, o_ref, l_ref):
        la, lb, lc = l0[...], l1[...], l2[...]
        m = jnp.maximum(jnp.maximum(la, lb), lc)
        wa, wb, wc = jnp.exp(la - m), jnp.exp(lb - m), jnp.exp(lc - m)
        tot = wa + wb + wc
        o_ref[...] = (wa * o0[...] + wb * o1[...] + wc * o2[...]) / tot
        l_ref[...] = m + jnp.log(tot)

    row = pl.BlockSpec((TM, w), lambda i: (i, 0))
    return _call(body, "attn_combine", (n // TM,), [row] * 6, [row, row],
                 [_sds((n, w), F32)] * 2)(*outs, *lses)


def _attn_bwd(q, k, v, do, o, lse, bias, g, dil, bsz, seq):
    n = q.shape[0]
    rows = BAND * dil
    nch = seq // rows
    w = N_HEADS * HEAD_DIM
    wc = ATTN_COLS[dil]
    ncol = w // wc
    hpc = wc // HEAD_DIM
    nj = wc // 128
    slab = lambda k: pltpu.VMEM((k * nj, rows, 128), F32)
    scale = HEAD_DIM ** -0.5

    def body(q_ref, kc_ref, kp_ref, vc_ref, vp_ref, do_ref, o_ref, l_ref, b_ref,
             dq_ref, dk_ref, dv_ref, db_ref, qf, kf, vf, dof, of, lf, dqf, dkf, dvf):
        b = pl.program_id(1)
        ch = pl.program_id(2)
        cur = ch % 2
        prv = 1 - cur

        @pl.when(jnp.logical_and(b == 0, ch == 0))
        def _():
            db_ref[...] = jnp.zeros_like(db_ref)

        @pl.when(ch == 0)
        def _():
            for j in range(nj):
                dkf[nj + j] = jnp.zeros((rows, 128), F32)
                dvf[nj + j] = jnp.zeros((rows, 128), F32)

        @pl.when(ch < nch)
        def _():
            valid = _valid_mask(ch)
            lane = lax.broadcasted_iota(jnp.int32, (BAND, 128), 1)
            for j in range(nj):
                sl = slice(128 * j, 128 * (j + 1))
                qf[j] = q_ref[:, sl].astype(F32)
                kf[j] = kp_ref[:, sl].astype(F32)
                kf[nj + j] = kc_ref[:, sl].astype(F32)
                vf[j] = vp_ref[:, sl].astype(F32)
                vf[nj + j] = vc_ref[:, sl].astype(F32)
                dof[j] = do_ref[:, sl]
                of[j] = o_ref[:, sl]
                lf[j] = l_ref[:, sl]
            for r in range(dil):
                rr = _residue(dil, r)
                for j in range(nj):
                    q2 = qf.at[j][rr, :].astype(BF16)
                    k2 = jnp.concatenate([kf.at[j][rr, :], kf.at[nj + j][rr, :]], axis=0).astype(BF16)
                    v2 = jnp.concatenate([vf.at[j][rr, :], vf.at[nj + j][rr, :]], axis=0).astype(BF16)
                    do2 = dof.at[j][rr, :]
                    prod = do2 * of.at[j][rr, :]
                    lse2 = lf.at[j][rr, :]
                    dqs = []
                    dk2 = jnp.zeros((2 * BAND, 128), F32)
                    dv2 = jnp.zeros((2 * BAND, 128), F32)
                    for half in range(2):
                        mine = (lane < HEAD_DIM) if half == 0 else (lane >= HEAD_DIM)
                        qh = jnp.where(mine, q2, jnp.zeros_like(q2))
                        doh = jnp.where(mine, do2, 0.0).astype(BF16)
                        s = lax.dot_general(qh, k2, NT, preferred_element_type=F32) * scale
                        s = jnp.where(valid, s + b_ref[2 * j + half], NEG_BIG)
                        lse_h = lse2[:, 64 * half:64 * half + 1]
                        p = jnp.exp(s - lse_h)
                        dp = lax.dot_general(doh, v2, NT, preferred_element_type=F32)
                        delta = jnp.sum(jnp.where(mine, prod, 0.0), axis=1, keepdims=True)
                        ds = p * (dp - delta)
                        db_ref[2 * j + half] += ds
                        dsb = ds.astype(BF16)
                        dqs.append(jnp.dot(dsb, k2, preferred_element_type=F32) * scale)
                        dk2 = dk2 + lax.dot_general(dsb, qh, TN, preferred_element_type=F32) * scale
                        dv2 = dv2 + lax.dot_general(p.astype(BF16), doh, TN, preferred_element_type=F32)
                    dqf.at[j][rr, :] = jnp.where(lane < HEAD_DIM, dqs[0], dqs[1])
                    dkf.at[prv * nj + j][rr, :] += dk2[:BAND, :]
                    dvf.at[prv * nj + j][rr, :] += dv2[:BAND, :]
                    dkf.at[cur * nj + j][rr, :] = dk2[BAND:, :]
                    dvf.at[cur * nj + j][rr, :] = dv2[BAND:, :]
            for j in range(nj):
                dq_ref[:, 128 * j:128 * (j + 1)] = dqf[j].astype(BF16)

        for j in range(nj):
            sl = slice(128 * j, 128 * (j + 1))
            dk_ref[:, sl] = dkf[prv * nj + j].astype(BF16)
            dv_ref[:, sl] = dvf[prv * nj + j].astype(BF16)

    last = nch - 1
    base = g * ncol
    cur3 = lambda c, b, ch: (b * nch + jnp.minimum(ch, last), base + c)
    prev3 = lambda c, b, ch: (b * nch + jnp.maximum(jnp.minimum(ch, last) - 1, 0), base + c)
    cur1 = lambda c, b, ch: (b * nch + jnp.minimum(ch, last), c)
    lag1 = lambda c, b, ch: (b * nch + jnp.maximum(ch - 1, 0), c)
    blk = lambda im: pl.BlockSpec((rows, wc), im)
    bias_blk = pl.BlockSpec((hpc, BAND, 2 * BAND), lambda c, b, ch: (c, 0, 0))
    out = _sds((n, w), BF16)
    return _call(
        body, "attn_bwd_d%d" % dil, (ncol, bsz, nch + 1),
        [blk(cur3), blk(cur3), blk(prev3), blk(cur3), blk(prev3), blk(cur1), blk(cur1), blk(cur1), bias_blk],
        [blk(cur1), blk(lag1), blk(lag1), bias_blk],
        [out, out, out, _sds((N_HEADS, BAND, 2 * BAND), F32)],
        [slab(1), slab(2), slab(2), slab(1), slab(1), slab(1), slab(1), slab(2), slab(2)],
    )(q, k, k, v, v, do, o, lse, bias)


def _adamw(name, parts, w, m, v):
    r, c = w.shape
    nparts = parts.shape[0]
    tr = r
    for cand in (256, 352, 128):
        if r % cand == 0 and r > cand:
            tr = cand
            break
    c1 = 1.0 - ADAM_B1 ** ADAM_STEP
    c2 = 1.0 - ADAM_B2 ** ADAM_STEP

    def body(p_ref, w_ref, m_ref, v_ref, g_ref, d_ref, nm_ref, nv_ref):
        g = p_ref[0].astype(F32)
        for dev in range(1, nparts):
            g = g + p_ref[dev].astype(F32)
        nm = ADAM_B1 * m_ref[...] + (1.0 - ADAM_B1) * g
        nv = ADAM_B2 * v_ref[...] + (1.0 - ADAM_B2) * (g * g)
        m_hat = nm / c1
        v_hat = nv / c2
        g_ref[...] = g
        d_ref[...] = -ADAM_LR * (m_hat / (jnp.sqrt(v_hat) + ADAM_EPS) + ADAM_WD * w_ref[...])
        nm_ref[...] = nm
        nv_ref[...] = nv

    row = pl.BlockSpec((tr, c), lambda i: (i, 0))
    return _call(body, name, (r // tr,), [pl.BlockSpec((nparts, tr, c), lambda i: (0, i, 0)), row, row, row],
                 [row] * 4, [_sds((r, c), F32)] * 4)(parts, w, m, v)


def _exchange(name, srcs, out_shapes, items):
    n_in = len(srcs)
    n_out = len(out_shapes)
    n_items = len(items)

    def body(*refs):
        ins = refs[:n_in]
        outs = refs[n_in:n_in + n_out]
        send_sems, recv_sems, local_sems = refs[n_in + n_out:]
        x, y, c = lax.axis_index("x"), lax.axis_index("y"), lax.axis_index("c")
        me = 4 * x + 2 * y + c

        def pick(ref, sel, peer):
            idx = tuple(peer if s == "peer" else me if s == "me" else s for s in sel)
            return ref.at[idx] if idx else ref

        copies = []
        for it, (si, ssel, oi, osel) in enumerate(items):
            local = pltpu.make_async_copy(pick(ins[si], ssel, me), pick(outs[oi], osel, me), local_sems.at[it])
            local.start()
            copies.append(local)
            for j in range(1, N_DEV):
                px = 1 - x if j & 4 else x
                py = 1 - y if j & 2 else y
                pc = 1 - c if j & 1 else c
                peer = 4 * px + 2 * py + pc
                sem = it * (N_DEV - 1) + j - 1
                cp = pltpu.make_async_remote_copy(
                    src_ref=pick(ins[si], ssel, peer), dst_ref=pick(outs[oi], osel, peer),
                    send_sem=send_sems.at[sem], recv_sem=recv_sems.at[sem],
                    device_id=(px, py, pc), device_id_type=pl.DeviceIdType.MESH)
                cp.start()
                copies.append(cp)
        for cp in copies:
            cp.wait()

    any_spec = pl.BlockSpec(memory_space=pl.ANY)
    return pl.pallas_call(
        body, name=name, in_specs=[any_spec] * n_in, out_specs=[any_spec] * n_out, out_shape=list(out_shapes),
        scratch_shapes=[pltpu.SemaphoreType.DMA((n_items * (N_DEV - 1),)),
                        pltpu.SemaphoreType.DMA((n_items * (N_DEV - 1),)),
                        pltpu.SemaphoreType.DMA((n_items,))],
        compiler_params=pltpu.CompilerParams(has_side_effects=True),
    )(*srcs)


def _sel(ref, sel, slot=None):
    idx = tuple(slot if s == "slot" else s for s in sel)
    return ref.at[idx] if idx else ref


def _comm_call(body, name, srcs, out_shapes, n_sems, n_local):
    any_spec = pl.BlockSpec(memory_space=pl.ANY)
    return pl.pallas_call(
        body, name=name, in_specs=[any_spec] * len(srcs), out_specs=[any_spec] * len(out_shapes),
        out_shape=list(out_shapes),
        scratch_shapes=[pltpu.SemaphoreType.DMA((n_sems,)), pltpu.SemaphoreType.DMA((n_sems,)),
                        pltpu.SemaphoreType.DMA((max(n_local, 1),))],
        compiler_params=pltpu.CompilerParams(has_side_effects=True),
    )(*srcs)


def _gather_two_level(name, srcs, out_shapes, items):
    n_in, n_out = len(srcs), len(out_shapes)
    per = N_DEV - 1

    def body(*refs):
        ins = refs[:n_in]
        outs = refs[n_in:n_in + n_out]
        send_sems, recv_sems, local_sems = refs[n_in + n_out:]
        x, y, c = lax.axis_index("x"), lax.axis_index("y"), lax.axis_index("c")
        sibling = (x, y, 1 - c)
        chips = [(1 - x, y), (x, 1 - y), (1 - x, 1 - y)]
        slot_of = lambda px, py, pc: 4 * px + 2 * py + pc

        def copy(it, k, block, to, src=None):
            _, _, oi, osel = items[it]
            dst = _sel(outs[oi], osel, slot_of(*block))
            return pltpu.make_async_remote_copy(
                src_ref=dst if src is None else src, dst_ref=dst,
                send_sem=send_sems.at[it * per + k], recv_sem=recv_sems.at[it * per + k],
                device_id=to, device_id_type=pl.DeviceIdType.MESH)

        local, sends = [], []
        for it, (si, ssel, oi, osel) in enumerate(items):
            src = _sel(ins[si], ssel)
            mine = pltpu.make_async_copy(src, _sel(outs[oi], osel, slot_of(x, y, c)), local_sems.at[it])
            mine.start()
            local.append(mine)
            first = [copy(it, 0, (x, y, c), sibling, src=src)]
            first += [copy(it, 1 + j, (x, y, c), (*chip, c), src=src) for j, chip in enumerate(chips)]
            for cp in first:
                cp.start()
            sends += first
        for it in range(len(items)):
            for j, chip in enumerate(chips):
                copy(it, 1 + j, (*chip, c), (x, y, c)).wait_recv()
                fwd = copy(it, 4 + j, (*chip, c), sibling)
                fwd.start()
                sends.append(fwd)
        for it in range(len(items)):
            copy(it, 0, sibling, (x, y, c)).wait_recv()
            for j, chip in enumerate(chips):
                copy(it, 4 + j, (*chip, 1 - c), (x, y, c)).wait_recv()
        for cp in sends:
            cp.wait_send()
        for cp in local:
            cp.wait()

    return _comm_call(body, name, srcs, out_shapes, len(items) * per, len(items))


def _scatter_to_sibling(name, srcs, out_shapes, items):
    n_in, n_out = len(srcs), len(out_shapes)

    def body(*refs):
        ins = refs[:n_in]
        outs = refs[n_in:n_in + n_out]
        send_sems, recv_sems, _ = refs[n_in + n_out:]
        x, y, c = lax.axis_index("x"), lax.axis_index("y"), lax.axis_index("c")
        copies = []
        for it, (si, oi, osel) in enumerate(items):
            for q in range(4):
                owner = 2 * q + (1 - c)
                cp = pltpu.make_async_remote_copy(
                    src_ref=ins[si].at[owner], dst_ref=_sel(outs[oi], osel, q),
                    send_sem=send_sems.at[it * 4 + q], recv_sem=recv_sems.at[it * 4 + q],
                    device_id=(x, y, 1 - c), device_id_type=pl.DeviceIdType.MESH)
                cp.start()
                copies.append(cp)
        for cp in copies:
            cp.wait()

    return _comm_call(body, name, srcs, out_shapes, len(items) * 4, 0)


def _chip_sum(name, mine, theirs):
    _, r, c = mine.shape
    tr = r
    for cand in (256, 352):
        if r % cand == 0 and r > cand:
            tr = cand
            break
    side = lax.axis_index("c").astype(jnp.int32).reshape(1)

    def body(side_ref, a_ref, b_ref, o_ref):
        o_ref[...] = (a_ref[...].astype(F32) + b_ref[...].astype(F32)).astype(o_ref.dtype)

    grid_spec = pltpu.PrefetchScalarGridSpec(
        num_scalar_prefetch=1, grid=(4, r // tr),
        in_specs=[pl.BlockSpec((None, tr, c), lambda q, i, s: (2 * q + s[0], i, 0)),
                  pl.BlockSpec((None, tr, c), lambda q, i, s: (q, i, 0))],
        out_specs=pl.BlockSpec((None, tr, c), lambda q, i, s: (q, i, 0)))
    return pl.pallas_call(
        body, name=name, grid_spec=grid_spec, out_shape=_sds((4, r, c), BF16),
        compiler_params=pltpu.CompilerParams(dimension_semantics=("arbitrary", "arbitrary"),
                                             vmem_limit_bytes=VMEM_LIMIT))(side, mine, theirs)


def _scatter_to_chips(name, srcs, out_shapes, items):
    n_in, n_out = len(srcs), len(out_shapes)

    def body(*refs):
        ins = refs[:n_in]
        outs = refs[n_in:n_in + n_out]
        send_sems, recv_sems, local_sems = refs[n_in + n_out:]
        x, y, c = lax.axis_index("x"), lax.axis_index("y"), lax.axis_index("c")
        my_chip = 2 * x + y
        chips = [(1 - x, y), (x, 1 - y), (1 - x, 1 - y)]
        copies = []
        for it, (si, oi, osel) in enumerate(items):
            local = pltpu.make_async_copy(ins[si].at[my_chip], _sel(outs[oi], osel, my_chip), local_sems.at[it])
            local.start()
            copies.append(local)
            for j, (px, py) in enumerate(chips):
                cp = pltpu.make_async_remote_copy(
                    src_ref=ins[si].at[2 * px + py], dst_ref=_sel(outs[oi], osel, my_chip),
                    send_sem=send_sems.at[it * 3 + j], recv_sem=recv_sems.at[it * 3 + j],
                    device_id=(px, py, c), device_id_type=pl.DeviceIdType.MESH)
                cp.start()
                copies.append(cp)
        for cp in copies:
            cp.wait()

    return _comm_call(body, name, srcs, out_shapes, len(items) * 3, len(items))


def _sum_slots(parts):
    _, r, c = parts.shape

    def body(p_ref, o_ref):
        acc = p_ref[0]
        for dev in range(1, N_DEV):
            acc = acc + p_ref[dev]
        o_ref[...] = acc

    return _call(body, "sum_slots", (1,), [pl.BlockSpec((N_DEV, r, c), lambda i: (0, 0, 0))],
                 pl.BlockSpec((r, c), lambda i: (0, 0)), _sds((r, c), F32))(parts)


def _pack(arrays):
    rows = []
    for a in arrays:
        flat = a.reshape(-1).astype(F32)
        pad = (-flat.shape[0]) % 1024
        if pad:
            flat = jnp.concatenate([flat, jnp.zeros((pad,), F32)])
        rows.append(flat.reshape(-1, 128))
    return jnp.concatenate(rows, axis=0)


def _unpack(buf, shapes):
    out = []
    r0 = 0
    for shp in shapes:
        size = int(np.prod(shp))
        nrows = -(-size // 1024) * 8
        out.append(buf[r0:r0 + nrows].reshape(-1)[:size].reshape(shp))
        r0 += nrows
    return out


def _block_diag_chunks(val):
    eye = jnp.eye(8, dtype=val.dtype)
    return jnp.einsum("cjhp,jk->cjhkp", val, eye).reshape(SSM_CHUNKS, CHUNK_CH, CHUNK_ST)


def _diag_blocks(dense):
    d5 = dense.reshape(SSM_CHUNKS, 8, SSM_GROUP, 8, SSM_STATE)
    return jnp.stack([d5[:, j, :, j, :] for j in range(8)], axis=1)


def _ffn_fwd(tag, hin, w_up, conv_w, conv_b, w_down4, gain, bias, seq):
    hc = _mm_nn("ffn_up" + tag, hin, w_up, F32, blocked_out=True)
    nb, n, c = hc.shape
    hc = hc.reshape(2, nb // 2, n, c)
    act = _conv_gate_fwd(hc, conv_w, conv_b, seq)
    ffn = _mm_nn_red("ffn_down" + tag, act, w_down4, F32)
    hout = _ln_fwd("ffn_norm" + tag, hin, ffn, gain, bias)
    return hout, (hin, hc, act, ffn)


def _ffn_bwd(tag, saved, dh, w_up, conv_w, conv_b, w_down4, gain, seq):
    hin, hc, act, ffn = saved
    dr, dgain, dbias = _ln_bwd("ffn_norm_bwd" + tag, hin, ffn, dh, gain)
    d_wdown = _mm_tn("ffn_down_dw" + tag, act, dr, act.shape[0], BF16, a_blocked=True, g_mode="shared")
    dact = _mm_nt_out("ffn_down_dx" + tag, dr, w_down4, F32)
    dhc, dconv = _conv_gate_bwd(hc, dact, conv_w, conv_b, seq)
    dhc8 = dhc.reshape(2 * dhc.shape[1], dhc.shape[2], dhc.shape[3])
    d_wup = _mm_tn("ffn_up_dw" + tag, hin, dhc8, dhc8.shape[0], BF16, g_mode="blocked")
    dhin = _mm_nt_red("ffn_up_dx" + tag, dhc8, w_up, F32, g_blocked=True, add=dr, add_scale=DN_ALPHA)
    return dhin, d_wup, d_wdown, dconv, dgain, dbias


def _local_step(x, target, p, wfull, bsz, seq):
    n = bsz * seq
    xf = x.reshape(n, D_MODEL)
    tf = target.reshape(n, D_MODEL)
    ln_g = p["ln_gain"].reshape(4, 1, D_MODEL)
    ln_b = p["ln_bias"].reshape(4, 1, D_MODEL)

    lr = p["s5_lam_re"].reshape(1, N_STATE)
    li = p["s5_lam_im"].reshape(1, N_STATE)
    ldt = p["s5_log_dt"].reshape(1, D_MODEL // SSM_GROUP)
    br_t = p["s5_b_re"].reshape(N_STATE, SSM_GROUP).T
    bi_t = p["s5_b_im"].reshape(N_STATE, SSM_GROUP).T
    ab_r, ab_i, bb_r, bb_i, tab = _s5_prep(lr, li, ldt, br_t, bi_t)

    def bb_chunks(bb):
        return _block_diag_chunks(bb.reshape(SSM_GROUP, SSM_CHUNKS, 8, SSM_STATE).transpose(1, 2, 0, 3))

    def c_chunks(cc):
        return _block_diag_chunks(cc.reshape(SSM_CHUNKS, 8, SSM_GROUP, SSM_STATE))

    bbr_c, bbi_c = bb_chunks(bb_r).astype(BF16), bb_chunks(bb_i).astype(BF16)
    cr_c, ci_c = c_chunks(p["s5_c_re"]).astype(BF16), c_chunks(p["s5_c_im"]).astype(BF16)
    dvec = p["s5_d"].reshape(1, D_MODEL)

    y, h_r, h_i = _s5_scan_fwd(xf, bbr_c, bbi_c, cr_c, ci_c, dvec, tab, seq)
    y2, z, gg = _glu_fwd(y, wfull["s5_w_glu"], p["s5_b_glu"].reshape(1, D_MODEL))
    mix = _mm_nn("s5_out", gg, wfull["s5_w_out"][None], F32)
    h1 = _ln_fwd("s5_norm", xf, mix, ln_g[0], ln_b[0])
    h2, ffn0_saved = _ffn_fwd("0", h1, wfull["ffn_w_up"][0], p["conv_w"][0], p["conv_b"][0],
                              wfull["ffn_w_down"][0], ln_g[1], ln_b[1], seq)

    kmat = _mm_nn("attn_k", h2, wfull["attn_w_kv"][:4], BF16)
    vmat = _mm_nn("attn_v", h2, wfull["attn_w_kv"][4:], BF16)
    qmat = _mm_nn("attn_q", h2, wfull["attn_w_q"], BF16)
    biases, outs, lses = [], [], []
    for g, dil in enumerate(DILATIONS):
        rel_t = p["rel_bias"][:, g * N_HEADS:(g + 1) * N_HEADS].T
        biases.append(_bias_expand(rel_t, dil).reshape(N_HEADS, BAND, 2 * BAND))
        o_g, l_g = _attn_fwd(qmat, kmat, vmat, biases[g], g, dil, bsz, seq)
        outs.append(o_g)
        lses.append(l_g)
    o, lse = _attn_combine(outs, lses)
    attn_out = _mm_nn("attn_out", o, wfull["attn_w_out"][None], F32)
    h3 = _ln_fwd("attn_norm", h2, attn_out, ln_g[2], ln_b[2])
    h4, ffn1_saved = _ffn_fwd("1", h3, wfull["ffn_w_up"][1], p["conv_w"][1], p["conv_b"][1],
                              wfull["ffn_w_down"][1], ln_g[3], ln_b[3], seq)
    loss_tile, dh4 = _loss_head(h4, tf)

    grads = {}
    dh3, grads["ffn_w_up1"], grads["ffn_w_down1"], dconv1, dg3, db3 = _ffn_bwd(
        "1", ffn1_saved, dh4, wfull["ffn_w_up"][1], p["conv_w"][1], p["conv_b"][1], wfull["ffn_w_down"][1],
        ln_g[3], seq)
    dr3, dg2, db2 = _ln_bwd("attn_norm_bwd", h2, attn_out, dh3, ln_g[2])
    grads["attn_w_out"] = _mm_tn("attn_out_dw", o, dr3, 1, BF16, g_mode="shared")
    do = _mm_nt_red("attn_out_dx", dr3, wfull["attn_w_out"][None], F32, g_blocked=False)
    dqs, dks, dvs, drel = [], [], [], []
    for g, dil in enumerate(DILATIONS):
        dq_g, dk_g, dv_g, db_g = _attn_bwd(qmat, kmat, vmat, do, o, lse, biases[g], g, dil, bsz, seq)
        dqs.append(dq_g)
        dks.append(dk_g)
        dvs.append(dv_g)
        drel.append(_bias_reduce(db_g.reshape(N_HEADS, BAND * 2 * BAND), dil).T)
    dq = jnp.concatenate(dqs, axis=1)
    dk = jnp.concatenate(dks, axis=1)
    dv = jnp.concatenate(dvs, axis=1)
    grads["rel_bias"] = jnp.concatenate(drel, axis=1)
    grads["attn_w_q"] = _mm_tn("attn_q_dw", h2, dq, N_DEV, BF16)
    grads["attn_w_kv"] = jnp.concatenate([_mm_tn("attn_k_dw", h2, dk, 4, BF16),
                                          _mm_tn("attn_v_dw", h2, dv, 4, BF16)], axis=0)
    dh2 = _mm_nt_red("attn_q_dx", dq, wfull["attn_w_q"], F32, g_blocked=False, add=dr3, add_scale=DN_ALPHA)
    dh2 = _mm_nt_red("attn_k_dx", dk, wfull["attn_w_kv"][:4], F32, g_blocked=False, add=dh2)
    dh2 = _mm_nt_red("attn_v_dx", dv, wfull["attn_w_kv"][4:], F32, g_blocked=False, add=dh2)

    dh1, grads["ffn_w_up0"], grads["ffn_w_down0"], dconv0, dg1, db1 = _ffn_bwd(
        "0", ffn0_saved, dh2, wfull["ffn_w_up"][0], p["conv_w"][0], p["conv_b"][0], wfull["ffn_w_down"][0],
        ln_g[1], seq)
    dr1, dg0, db0 = _ln_bwd("s5_norm_bwd", xf, mix, dh1, ln_g[0])
    grads["s5_w_out"] = _mm_tn("s5_out_dw", gg, dr1, 1, BF16, g_mode="shared")
    dgg = _mm_nt_red("s5_out_dx", dr1, wfull["s5_w_out"][None], F32, g_blocked=False)
    dz, d_bglu, dy2_direct = _glu_bwd_gate(y, z, dgg)
    grads["s5_w_glu"] = _mm_tn("s5_glu_dw", y2, dz, 1, BF16, g_mode="shared")
    dy2 = _mm_nt_red("s5_glu_dx", dz, wfull["s5_w_glu"][None], F32, g_blocked=False, add=dy2_direct)
    dy = _gelu_bwd(y, dy2)
    dx, dcr, dci, dbr, dbi, dar, dai, dd = _s5_scan_bwd(xf, dy, dr1, h_r, h_i, bbr_c, bbi_c, cr_c, ci_c,
                                                        dvec, tab, seq)

    def bb_from_chunks(dense):
        return _diag_blocks(dense).transpose(2, 0, 1, 3).reshape(SSM_GROUP, N_STATE)

    d_lr, d_li, d_ldt, d_br, d_bi = _s5_prep_bwd(lr, li, ldt, br_t, bi_t, dar, dai,
                                                 bb_from_chunks(dbr), bb_from_chunks(dbi))
    grads["s5_lam_re"] = d_lr.reshape(p["s5_lam_re"].shape)
    grads["s5_lam_im"] = d_li.reshape(p["s5_lam_im"].shape)
    grads["s5_log_dt"] = d_ldt.reshape(p["s5_log_dt"].shape)
    grads["s5_b_re"] = d_br.T.reshape(p["s5_b_re"].shape)
    grads["s5_b_im"] = d_bi.T.reshape(p["s5_b_im"].shape)
    grads["s5_c_re"] = _diag_blocks(dcr).reshape(p["s5_c_re"].shape)
    grads["s5_c_im"] = -_diag_blocks(dci).reshape(p["s5_c_im"].shape)
    grads["s5_d"] = dd.reshape(p["s5_d"].shape)
    grads["s5_b_glu"] = d_bglu.reshape(1, D_MODEL)
    dconv = jnp.stack([dconv0, dconv1]).reshape(2, N_DEV, 8, -1)
    grads["ffn_conv_w"] = dconv[:, :, 0:3, :].transpose(0, 2, 1, 3)
    grads["ffn_conv_b"] = dconv[:, :, 3, :]
    grads["ln_gain"] = jnp.stack([dg0, dg1, dg2, dg3]).reshape(2, 2, D_MODEL)
    grads["ln_bias"] = jnp.stack([db0, db1, db2, db3]).reshape(2, 2, D_MODEL)
    return loss_tile[0, 0], dx.reshape(x.shape), grads


SMALL_REPLICATED = ("s5_lam_re", "s5_lam_im", "s5_log_dt", "s5_b_re", "s5_b_im", "s5_c_re", "s5_c_im", "s5_d",
                    "rel_bias", "ffn_conv_b")
SMALL_SHARDED = ("s5_b_glu", "ffn_conv_w", "ln_gain", "ln_bias")
BIG = ("s5_w_glu", "s5_w_out", "attn_w_kv", "attn_w_q", "attn_w_out", "ffn_w_up", "ffn_w_down")
WEIGHTS = ("s5_lam_re", "s5_lam_im", "s5_log_dt", "s5_b_re", "s5_b_im", "s5_c_re", "s5_c_im", "s5_d", "s5_w_glu",
           "s5_b_glu", "s5_w_out", "attn_w_kv", "attn_w_q", "attn_w_out", "rel_bias", "ffn_w_up", "ffn_conv_w",
           "ffn_conv_b", "ffn_w_down", "ln_gain", "ln_bias")


def _gather_weights(w):
    small = _pack([w[k] for k in SMALL_SHARDED])
    srcs = [w["s5_w_glu"][0].astype(BF16), w["s5_w_out"][0].astype(BF16), w["attn_w_kv"].astype(BF16),
            w["attn_w_q"][0].astype(BF16), w["attn_w_out"][0].astype(BF16), w["ffn_w_up"].astype(BF16),
            w["ffn_w_down"].astype(BF16), small]
    outs = [_sds((N_DEV,) + srcs[0].shape, BF16), _sds((N_DEV,) + srcs[1].shape, BF16),
            _sds((N_DEV,) + srcs[2].shape, BF16), _sds((N_DEV,) + srcs[3].shape, BF16),
            _sds((N_DEV,) + srcs[4].shape, BF16), _sds((2, N_DEV) + srcs[5].shape[1:], BF16),
            _sds((2, N_DEV) + srcs[6].shape[1:], BF16), _sds((N_DEV,) + small.shape, F32)]
    items = [(0, (), 0, ("slot",)), (1, (), 1, ("slot",)), (5, (0,), 5, (0, "slot")), (6, (0,), 6, (0, "slot")),
             (7, (), 7, ("slot",)), (2, (), 2, ("slot",)), (3, (), 3, ("slot",)), (4, (), 4, ("slot",)),
             (5, (1,), 5, (1, "slot")), (6, (1,), 6, (1, "slot"))]
    g = _gather_two_level("gather_weights", srcs, outs, items)
    d = D_MODEL
    wfull = {
        "s5_w_glu": g[0].reshape(d, d),
        "s5_w_out": g[1].reshape(d, d),
        "attn_w_kv": g[2],
        "attn_w_q": g[3],
        "attn_w_out": g[4].reshape(d, d),
        "ffn_w_up": g[5],
        "ffn_w_down": g[6].reshape(2, 4, -1, d),
    }
    smalls = [_unpack(g[7][dev], [w[k].shape for k in SMALL_SHARDED]) for dev in range(N_DEV)]
    full_small = {
        "s5_b_glu": jnp.concatenate([s[0] for s in smalls], axis=1),
        "ffn_conv_w": jnp.stack([s[1] for s in smalls], axis=2),
        "ln_gain": jnp.concatenate([s[2] for s in smalls], axis=2),
        "ln_bias": jnp.concatenate([s[3] for s in smalls], axis=2),
    }
    return wfull, full_small


def kernel(x, s5_lam_re, s5_lam_im, s5_log_dt, s5_b_re, s5_b_im, s5_c_re, s5_c_im, s5_d, s5_w_glu, s5_b_glu, s5_w_out, attn_w_kv, attn_w_q, attn_w_out, rel_bias, ffn_w_up, ffn_conv_w, ffn_conv_b, ffn_w_down, ln_gain, ln_bias, loss_target, m_s5_lam_re, m_s5_lam_im, m_s5_log_dt, m_s5_b_re, m_s5_b_im, m_s5_c_re, m_s5_c_im, m_s5_d, m_s5_w_glu, m_s5_b_glu, m_s5_w_out, m_attn_w_kv, m_attn_w_q, m_attn_w_out, m_rel_bias, m_ffn_w_up, m_ffn_conv_w, m_ffn_conv_b, m_ffn_w_down, m_ln_gain, m_ln_bias, v_s5_lam_re, v_s5_lam_im, v_s5_log_dt, v_s5_b_re, v_s5_b_im, v_s5_c_re, v_s5_c_im, v_s5_d, v_s5_w_glu, v_s5_b_glu, v_s5_w_out, v_attn_w_kv, v_attn_w_q, v_attn_w_out, v_rel_bias, v_ffn_w_up, v_ffn_conv_w, v_ffn_conv_b, v_ffn_w_down, v_ln_gain, v_ln_bias):
    w = dict(s5_lam_re=s5_lam_re, s5_lam_im=s5_lam_im, s5_log_dt=s5_log_dt, s5_b_re=s5_b_re, s5_b_im=s5_b_im,
             s5_c_re=s5_c_re, s5_c_im=s5_c_im, s5_d=s5_d, s5_w_glu=s5_w_glu, s5_b_glu=s5_b_glu, s5_w_out=s5_w_out,
             attn_w_kv=attn_w_kv, attn_w_q=attn_w_q, attn_w_out=attn_w_out, rel_bias=rel_bias, ffn_w_up=ffn_w_up,
             ffn_conv_w=ffn_conv_w, ffn_conv_b=ffn_conv_b, ffn_w_down=ffn_w_down, ln_gain=ln_gain, ln_bias=ln_bias)
    mom = dict(s5_lam_re=m_s5_lam_re, s5_lam_im=m_s5_lam_im, s5_log_dt=m_s5_log_dt, s5_b_re=m_s5_b_re,
               s5_b_im=m_s5_b_im, s5_c_re=m_s5_c_re, s5_c_im=m_s5_c_im, s5_d=m_s5_d, s5_w_glu=m_s5_w_glu,
               s5_b_glu=m_s5_b_glu, s5_w_out=m_s5_w_out, attn_w_kv=m_attn_w_kv, attn_w_q=m_attn_w_q,
               attn_w_out=m_attn_w_out, rel_bias=m_rel_bias, ffn_w_up=m_ffn_w_up, ffn_conv_w=m_ffn_conv_w,
               ffn_conv_b=m_ffn_conv_b, ffn_w_down=m_ffn_w_down, ln_gain=m_ln_gain, ln_bias=m_ln_bias)
    var = dict(s5_lam_re=v_s5_lam_re, s5_lam_im=v_s5_lam_im, s5_log_dt=v_s5_log_dt, s5_b_re=v_s5_b_re,
               s5_b_im=v_s5_b_im, s5_c_re=v_s5_c_re, s5_c_im=v_s5_c_im, s5_d=v_s5_d, s5_w_glu=v_s5_w_glu,
               s5_b_glu=v_s5_b_glu, s5_w_out=v_s5_w_out, attn_w_kv=v_attn_w_kv, attn_w_q=v_attn_w_q,
               attn_w_out=v_attn_w_out, rel_bias=v_rel_bias, ffn_w_up=v_ffn_w_up, ffn_conv_w=v_ffn_conv_w,
               ffn_conv_b=v_ffn_conv_b, ffn_w_down=v_ffn_w_down, ln_gain=v_ln_gain, ln_bias=v_ln_bias)
    bsz, seq, _ = x.shape
    me = 4 * lax.axis_index("x") + 2 * lax.axis_index("y") + lax.axis_index("c")

    wfull, full_small = _gather_weights(w)
    c_ff = ffn_conv_w.shape[2]
    conv_w_full = full_small["ffn_conv_w"]
    p = dict(w)
    p.update(ln_gain=full_small["ln_gain"], ln_bias=full_small["ln_bias"], s5_b_glu=full_small["s5_b_glu"])
    p["conv_w"] = conv_w_full.transpose(0, 2, 1, 3).reshape(2, 2, 4, 3, c_ff)
    p["conv_b"] = ffn_conv_b.reshape(2, 2, 4, 1, c_ff)

    loss_part, grad_x, g = _local_step(x, loss_target, p, wfull, bsz, seq)
    loss = lax.psum(loss_part, ("x", "y", "c"))

    d = D_MODEL
    srcs = [g["s5_w_glu"].reshape(N_DEV, d // N_DEV, d), g["s5_w_out"].reshape(N_DEV, d // N_DEV, d),
            g["attn_w_kv"], g["attn_w_q"], g["attn_w_out"].reshape(N_DEV, d // N_DEV, d),
            g["ffn_w_up0"], g["ffn_w_up1"],
            g["ffn_w_down0"].reshape(N_DEV, -1, d), g["ffn_w_down1"].reshape(N_DEV, -1, d)]
    stage = [_sds((4,) + s.shape[1:], BF16) for s in srcs]
    theirs = _scatter_to_sibling("scatter_grads_chip", srcs, stage, [(i, i, ("slot",)) for i in range(len(srcs))])
    chip = [_chip_sum("chip_sum_%d" % i, srcs[i], theirs[i]) for i in range(len(srcs))]
    outs = [_sds((4,) + srcs[0].shape[1:], BF16), _sds((4,) + srcs[1].shape[1:], BF16),
            _sds((4,) + srcs[2].shape[1:], BF16), _sds((4,) + srcs[3].shape[1:], BF16),
            _sds((4,) + srcs[4].shape[1:], BF16), _sds((4, 2) + srcs[5].shape[1:], BF16),
            _sds((4, 2) + srcs[7].shape[1:], BF16)]
    items = [(0, 0, ("slot",)), (1, 1, ("slot",)), (2, 2, ("slot",)), (3, 3, ("slot",)), (4, 4, ("slot",)),
             (5, 5, ("slot", 0)), (6, 5, ("slot", 1)), (7, 6, ("slot", 0)), (8, 6, ("slot", 1))]
    parts = _scatter_to_chips("scatter_grads", chip, outs, items)
    big_parts = dict(zip(BIG, parts))

    results = {}
    for name in BIG:
        shard = w[name]
        r2 = (-1, shard.shape[-1])
        pr = big_parts[name].reshape(4, -1, shard.shape[-1])
        outs4 = _adamw("adamw_" + name, pr, shard.reshape(r2), mom[name].reshape(r2), var[name].reshape(r2))
        results[name] = tuple(o.reshape(shard.shape) for o in outs4)

    small_names = SMALL_REPLICATED + SMALL_SHARDED
    packed = _pack([g[k] for k in small_names])
    slots = _exchange("allreduce_small", [packed], [_sds((N_DEV,) + packed.shape, F32)],
                      [(0, (), 0, ("me",))])[0]
    total = _unpack(_sum_slots(slots), [g[k].shape for k in small_names])
    gsum = dict(zip(small_names, total))
    mine = {k: gsum[k] for k in SMALL_REPLICATED}
    mine["s5_b_glu"] = lax.dynamic_slice_in_dim(gsum["s5_b_glu"], me * (d // N_DEV), d // N_DEV, axis=1)
    mine["ffn_conv_w"] = lax.dynamic_index_in_dim(gsum["ffn_conv_w"], me, axis=2, keepdims=False)
    mine["ln_gain"] = lax.dynamic_slice_in_dim(gsum["ln_gain"], me * (d // N_DEV), d // N_DEV, axis=2)
    mine["ln_bias"] = lax.dynamic_slice_in_dim(gsum["ln_bias"], me * (d // N_DEV), d // N_DEV, axis=2)
    gp = _pack([mine[k] for k in small_names])
    outs4 = _adamw("adamw_small", gp[None], _pack([w[k] for k in small_names]),
                   _pack([mom[k] for k in small_names]), _pack([var[k] for k in small_names]))
    shapes = [w[k].shape for k in small_names]
    unpacked = [_unpack(o, shapes) for o in outs4]
    for i, k in enumerate(small_names):
        results[k] = tuple(unpacked[j][i] for j in range(4))

    out = [loss, grad_x]
    for j in range(4):
        out.extend(results[k][j] for k in WEIGHTS)
    return tuple(out)
```

```python
import functools
import math

import numpy as np
import jax
import jax.numpy as jnp
from jax import lax
from jax.experimental import pallas as pl
from jax.experimental.pallas import tpu as pltpu

F32 = jnp.float32
BF16 = jnp.bfloat16
HIGHEST = lax.Precision.HIGHEST

N_DEV = 8
D_MODEL = 1024
SSM_GROUP = 16
SSM_STATE = 64
SSM_CHUNKS = 8
CHUNK_CH = D_MODEL // SSM_CHUNKS
CHUNK_ST = CHUNK_CH // SSM_GROUP * SSM_STATE
N_STATE = D_MODEL // SSM_GROUP * SSM_STATE
HEAD_DIM = 64
N_HEADS = 16
BAND = 128
DILATIONS = (1, 4, 16)
REL_BUCKETS = 32
REL_MAX_DIST = 2048
NEG_BIG = -1e30
DN_ALPHA = (2.0 * 2) ** 0.25
LN_EPS = 1e-5
ADAM_LR = 0.001
ADAM_B1 = 0.9
ADAM_B2 = 0.999
ADAM_EPS = 1e-08
ADAM_WD = 0.01
ADAM_STEP = 10

TM = 512
T_SCAN = 256
VMEM_LIMIT = 56 * 1024 * 1024

NN = (((1,), (0,)), ((), ()))
NT = (((1,), (1,)), ((), ()))
TN = (((0,), (0,)), ((), ()))


class _Plan:
    def __init__(self, srcs, out_shapes, n_sems, n_local, phases):
        self.srcs, self.out_shapes, self.n_sems, self.n_local, self.phases = srcs, out_shapes, n_sems, n_local, phases


def _call(body, name, grid, in_specs, out_specs, out_shape, scratch=(), comm=None):
    params = dict(dimension_semantics=("arbitrary",) * len(grid), vmem_limit_bytes=VMEM_LIMIT)
    if comm is None:
        return pl.pallas_call(
            body, name=name, grid=grid, in_specs=in_specs, out_specs=out_specs, out_shape=out_shape,
            scratch_shapes=list(scratch), compiler_params=pltpu.CompilerParams(**params))
    single = not isinstance(out_specs, (list, tuple))
    o_specs = [out_specs] if single else list(out_specs)
    o_shape = [out_shape] if single else list(out_shape)
    n_in, n_out, n_scr = len(in_specs), len(o_specs), len(scratch)
    n_cin, n_cout = len(comm.srcs), len(comm.out_shapes)
    total = int(np.prod(grid))

    def wrapped(*refs):
        ins = refs[:n_in]
        cins = refs[n_in:n_in + n_cin]
        outs = refs[n_in + n_cin:n_in + n_cin + n_out]
        couts = refs[n_in + n_cin + n_out:n_in + n_cin + n_out + n_cout]
        rest = refs[n_in + n_cin + n_out + n_cout:]
        scr, sems = rest[:n_scr], rest[n_scr:]
        step = pl.program_id(0)
        for ax in range(1, len(grid)):
            step = step * grid[ax] + pl.program_id(ax)
        phases = comm.phases(cins, couts, *sems, total)
        for at, action in phases:
            if at == 0:
                pl.when(step == 0)(action)
        body(*ins, *outs, *scr)
        for at, action in phases:
            if at > 0:
                pl.when(step == at)(action)

    any_spec = pl.BlockSpec(memory_space=pl.ANY)
    call = pl.pallas_call(
        wrapped, name=name, grid=grid, in_specs=list(in_specs) + [any_spec] * n_cin,
        out_specs=o_specs + [any_spec] * n_cout, out_shape=o_shape + list(comm.out_shapes),
        scratch_shapes=list(scratch) + [pltpu.SemaphoreType.DMA((comm.n_sems,)),
                                        pltpu.SemaphoreType.DMA((comm.n_sems,)),
                                        pltpu.SemaphoreType.DMA((max(comm.n_local, 1),))],
        compiler_params=pltpu.CompilerParams(has_side_effects=True, **params))

    def run(*args):
        res = call(*args, *comm.srcs)
        own = res[0] if single else res[:n_out]
        return own, res[n_out:]

    return run


def _sds(shape, dtype):
    return jax.ShapeDtypeStruct(shape, dtype)


def _mm(name, a, b, *, dims, grid, a_spec, b_spec, o_spec, out_shape, nred=1, red_axis=0,
        add=None, add_spec=None, add_scale=1.0, comm=None):
    has_add = add is not None
    acc_shape = tuple(s for s in o_spec.block_shape if s is not None)

    def body(*refs):
        if has_add:
            a_ref, b_ref, add_ref, o_ref = refs[:4]
            rest = refs[4:]
        else:
            a_ref, b_ref, o_ref = refs[:3]
            rest = refs[3:]
        prod = lax.dot_general(a_ref[...].astype(BF16), b_ref[...].astype(BF16), dims,
                               preferred_element_type=F32)

        def finish(val):
            if has_add:
                val = val + add_scale * add_ref[...].astype(F32)
            o_ref[...] = val.astype(o_ref.dtype)

        if nred == 1:
            finish(prod)
        else:
            acc = rest[0]
            k = pl.program_id(red_axis)

            @pl.when(k == 0)
            def _():
                acc[...] = prod

            @pl.when(k > 0)
            def _():
                acc[...] += prod

            @pl.when(k == nred - 1)
            def _():
                finish(acc[...])

    in_specs = [a_spec, b_spec] + ([add_spec] if has_add else [])
    scratch = [pltpu.VMEM(acc_shape, F32)] if nred > 1 else []
    args = (a, b) + ((add,) if has_add else ())
    return _call(body, name, grid, in_specs, o_spec, out_shape, scratch, comm=comm)(*args)


def _mm_nn(name, a, b, out_dtype, blocked_out=False, comm=None):
    m, k = a.shape
    nb, _, n = b.shape
    if blocked_out:
        o_spec = pl.BlockSpec((None, TM, n), lambda d, i: (d, i, 0))
        out_shape = _sds((nb, m, n), out_dtype)
    else:
        o_spec = pl.BlockSpec((TM, n), lambda d, i: (i, d))
        out_shape = _sds((m, nb * n), out_dtype)
    return _mm(name, a, b, dims=NN, grid=(nb, m // TM),
               a_spec=pl.BlockSpec((TM, k), lambda d, i: (i, 0)),
               b_spec=pl.BlockSpec((None, k, n), lambda d, i: (d, 0, 0)),
               o_spec=o_spec, out_shape=out_shape, comm=comm)


def _mm_nn_red(name, a, b, out_dtype):
    nb, m, k = a.shape
    n = b.shape[2]
    return _mm(name, a, b, dims=NN, grid=(m // TM, nb), nred=nb, red_axis=1,
               a_spec=pl.BlockSpec((None, TM, k), lambda i, d: (d, i, 0)),
               b_spec=pl.BlockSpec((None, k, n), lambda i, d: (d, 0, 0)),
               o_spec=pl.BlockSpec((TM, n), lambda i, d: (i, 0)),
               out_shape=_sds((m, n), out_dtype))


def _mm_tn(name, a, g, nb, out_dtype, a_blocked=False, g_mode="cols"):
    if a_blocked:
        _, m, ka = a.shape
        a_spec = pl.BlockSpec((None, TM, ka), lambda d, i: (d, i, 0))
    else:
        m, ka = a.shape
        a_spec = pl.BlockSpec((TM, ka), lambda d, i: (i, 0))
    if g_mode == "cols":
        n = g.shape[1] // nb
        g_spec = pl.BlockSpec((TM, n), lambda d, i: (i, d))
    elif g_mode == "blocked":
        n = g.shape[2]
        g_spec = pl.BlockSpec((None, TM, n), lambda d, i: (d, i, 0))
    else:
        n = g.shape[1]
        g_spec = pl.BlockSpec((TM, n), lambda d, i: (i, 0))
    nred = m // TM
    return _mm(name, a, g, dims=TN, grid=(nb, nred), nred=nred, red_axis=1,
               a_spec=a_spec, b_spec=g_spec,
               o_spec=pl.BlockSpec((None, ka, n), lambda d, i: (d, 0, 0)),
               out_shape=_sds((nb, ka, n), out_dtype))


def _mm_nt_red(name, g, w, out_dtype, g_blocked, add=None, add_scale=1.0, comm=None):
    nb, k, n = w.shape
    if g_blocked:
        m = g.shape[1]
        g_spec = pl.BlockSpec((None, TM, n), lambda i, d: (d, i, 0))
    else:
        m = g.shape[0]
        g_spec = pl.BlockSpec((TM, n), lambda i, d: (i, d))
    o_spec = pl.BlockSpec((TM, k), lambda i, d: (i, 0))
    return _mm(name, g, w, dims=NT, grid=(m // TM, nb), nred=nb, red_axis=1,
               a_spec=g_spec, b_spec=pl.BlockSpec((None, k, n), lambda i, d: (d, 0, 0)),
               o_spec=o_spec, out_shape=_sds((m, k), out_dtype),
               add=add, add_spec=o_spec if add is not None else None, add_scale=add_scale, comm=comm)


def _mm_nt_out(name, g, w, out_dtype):
    m, n = g.shape
    nb, k, _ = w.shape
    return _mm(name, g, w, dims=NT, grid=(nb, m // TM),
               a_spec=pl.BlockSpec((TM, n), lambda d, i: (i, 0)),
               b_spec=pl.BlockSpec((None, k, n), lambda d, i: (d, 0, 0)),
               o_spec=pl.BlockSpec((None, TM, k), lambda d, i: (d, i, 0)),
               out_shape=_sds((nb, m, k), out_dtype))


def _ln_stats(x, f):
    r = DN_ALPHA * x + f
    mu = jnp.mean(r, axis=-1, keepdims=True)
    xc = r - mu
    var = jnp.mean(xc * xc, axis=-1, keepdims=True)
    rstd = lax.rsqrt(var + LN_EPS)
    return xc * rstd, rstd


def _ln_fwd(name, xin, f, gain, bias):
    n, d = xin.shape

    def body(x_ref, f_ref, g_ref, b_ref, o_ref):
        xhat, _ = _ln_stats(x_ref[...], f_ref[...])
        o_ref[...] = xhat * g_ref[...] + b_ref[...]

    row = pl.BlockSpec((TM, d), lambda i: (i, 0))
    vec = pl.BlockSpec((1, d), lambda i: (0, 0))
    return _call(body, name, (n // TM,), [row, row, vec, vec], row, _sds((n, d), F32))(xin, f, gain, bias)


def _ln_bwd(name, xin, f, dy, gain):
    n, d = xin.shape

    def body(x_ref, f_ref, dy_ref, g_ref, dr_ref, dg_ref, db_ref):
        i = pl.program_id(0)
        xhat, rstd = _ln_stats(x_ref[...], f_ref[...])
        dy_v = dy_ref[...]
        dxh = dy_v * g_ref[...]
        m1 = jnp.mean(dxh, axis=-1, keepdims=True)
        m2 = jnp.mean(dxh * xhat, axis=-1, keepdims=True)
        dr_ref[...] = rstd * (dxh - m1 - xhat * m2)
        dg = jnp.sum(dy_v * xhat, axis=0, keepdims=True)
        db = jnp.sum(dy_v, axis=0, keepdims=True)

        @pl.when(i == 0)
        def _():
            dg_ref[...] = dg
            db_ref[...] = db

        @pl.when(i > 0)
        def _():
            dg_ref[...] += dg
            db_ref[...] += db

    row = pl.BlockSpec((TM, d), lambda i: (i, 0))
    vec = pl.BlockSpec((1, d), lambda i: (0, 0))
    return _call(body, name, (n // TM,), [row, row, row, vec], [row, vec, vec],
                 [_sds((n, d), F32), _sds((1, d), F32), _sds((1, d), F32)])(xin, f, dy, gain)


def _loss_head(y, target):
    n, d = y.shape

    def body(y_ref, t_ref, l_ref, dy_ref):
        i = pl.program_id(0)
        e = y_ref[...] - t_ref[...]
        dy_ref[...] = e * (1.0 / d)
        part = jnp.sum(jnp.sum(e * e, axis=1, keepdims=True), axis=0, keepdims=True) * (0.5 / d)
        part = jnp.broadcast_to(part, (8, 128))

        @pl.when(i == 0)
        def _():
            l_ref[...] = part

        @pl.when(i > 0)
        def _():
            l_ref[...] += part

    row = pl.BlockSpec((TM, d), lambda i: (i, 0))
    return _call(body, "loss_head", (n // TM,), [row, row],
                 [pl.BlockSpec((8, 128), lambda i: (0, 0)), row],
                 [_sds((8, 128), F32), _sds((n, d), F32)])(y, target)


def _group_expand_matrix():
    e = np.zeros((D_MODEL // SSM_GROUP, N_STATE), np.float32)
    for g in range(D_MODEL // SSM_GROUP):
        e[g, g * SSM_STATE:(g + 1) * SSM_STATE] = 1.0
    return jnp.asarray(e)


def _discretise(lr, li, ldt, br, bi, expand):
    dt = jnp.dot(jnp.exp(ldt), expand, precision=HIGHEST, preferred_element_type=F32)
    mag = jnp.exp(lr * dt)
    th = li * dt
    ab_r = mag * jnp.cos(th)
    ab_i = mag * jnp.sin(th)
    den = lr * lr + li * li
    nr = ab_r - 1.0
    co_r = (nr * lr + ab_i * li) / den
    co_i = (ab_i * lr - nr * li) / den
    bb_r = co_r * br - co_i * bi
    bb_i = co_r * bi + co_i * br
    return ab_r, ab_i, bb_r, bb_i


def _cmul(ar, ai, br, bi):
    return ar * br - ai * bi, ar * bi + ai * br


def _s5_prep(lr, li, ldt, br, bi):
    s = N_STATE
    expand = _group_expand_matrix()

    def body(lr_ref, li_ref, ldt_ref, br_ref, bi_ref, e_ref, abr_ref, abi_ref, bbr_ref, bbi_ref, tab_ref):
        ab_r, ab_i, bb_r, bb_i = _discretise(lr_ref[...], li_ref[...], ldt_ref[...], br_ref[...],
                                             bi_ref[...], e_ref[...])
        abr_ref[...] = ab_r
        abi_ref[...] = ab_i
        bbr_ref[...] = bb_r
        bbi_ref[...] = bb_i
        row = lax.broadcasted_iota(jnp.int32, (8, s), 0)
        for base, sign in ((0, 1.0), (8, -1.0)):
            p = [None] * 9
            p[1] = (ab_r, sign * ab_i)
            p[2] = _cmul(*p[1], *p[1])
            p[3] = _cmul(*p[2], *p[1])
            p[4] = _cmul(*p[2], *p[2])
            p[5] = _cmul(*p[4], *p[1])
            p[6] = _cmul(*p[4], *p[2])
            p[7] = _cmul(*p[4], *p[3])
            p[8] = _cmul(*p[4], *p[4])
            for j, k in enumerate((1, 2, 4)):
                keep = (row >= k) if base == 0 else (row < 8 - k)
                tab_ref[base + 2 * j] = jnp.where(keep, jnp.broadcast_to(p[k][0], (8, s)), 0.0)
                tab_ref[base + 2 * j + 1] = jnp.where(keep, jnp.broadcast_to(p[k][1], (8, s)), 0.0)
            pw_r = jnp.zeros((8, s), F32)
            pw_i = jnp.zeros((8, s), F32)
            for i in range(8):
                e = i + 1 if base == 0 else 8 - i
                pw_r = jnp.where(row == i, jnp.broadcast_to(p[e][0], (8, s)), pw_r)
                pw_i = jnp.where(row == i, jnp.broadcast_to(p[e][1], (8, s)), pw_i)
            tab_ref[base + 6] = pw_r
            tab_ref[base + 7] = pw_i

    def full(shape):
        return pl.BlockSpec(shape, lambda i: (0,) * len(shape))

    g = D_MODEL // SSM_GROUP
    return _call(body, "s5_prep", (1,),
                 [full((1, s)), full((1, s)), full((1, g)), full((16, s)), full((16, s)), full((g, s))],
                 [full((1, s)), full((1, s)), full((16, s)), full((16, s)), full((16, 8, s))],
                 [_sds((1, s), F32), _sds((1, s), F32), _sds((16, s), F32), _sds((16, s), F32),
                  _sds((16, 8, s), F32)])(lr, li, ldt, br, bi, expand)


def _s5_prep_bwd(lr, li, ldt, br, bi, d_abr, d_abi, d_bbr, d_bbi):
    s = N_STATE
    g = D_MODEL // SSM_GROUP
    expand = _group_expand_matrix()

    def body(lr_ref, li_ref, ldt_ref, br_ref, bi_ref, e_ref, c1, c2, c3, c4, o1, o2, o3, o4, o5):
        e_val = e_ref[...]
        _, vjp = jax.vjp(lambda a, b, c, d, e: _discretise(a, b, c, d, e, e_val),
                         lr_ref[...], li_ref[...], ldt_ref[...], br_ref[...], bi_ref[...])
        grads = vjp((c1[...], c2[...], c3[...], c4[...]))
        for o, gr in zip((o1, o2, o3, o4, o5), grads):
            o[...] = gr

    def full(shape):
        return pl.BlockSpec(shape, lambda i: (0,) * len(shape))

    return _call(body, "s5_prep_bwd", (1,),
                 [full((1, s)), full((1, s)), full((1, g)), full((16, s)), full((16, s)), full((g, s)),
                  full((1, s)), full((1, s)), full((16, s)), full((16, s))],
                 [full((1, s)), full((1, s)), full((1, g)), full((16, s)), full((16, s))],
                 [_sds((1, s), F32), _sds((1, s), F32), _sds((1, g), F32), _sds((16, s), F32),
                  _sds((16, s), F32)])(lr, li, ldt, br, bi, expand, d_abr, d_abi, d_bbr, d_bbi)


def _scan8(vr, vi, tab_ref, base, reverse):
    for j, k in enumerate((1, 2, 4)):
        mr = tab_ref[base + 2 * j]
        mi = tab_ref[base + 2 * j + 1]
        shift = 8 - k if reverse else k
        sr = pltpu.roll(vr, shift, 0)
        si = pltpu.roll(vi, shift, 0)
        vr, vi = vr + mr * sr - mi * si, vi + mr * si + mi * sr
    return vr, vi


def _s5_scan_fwd(x, bbr, bbi, cr, ci, dvec, tab, seq, comm=None):
    n, d = x.shape
    nt = seq // T_SCAN
    nblk = T_SCAN // 8
    st = CHUNK_ST

    def body(x_ref, bbr_ref, bbi_ref, cr_ref, ci_ref, d_ref, tab_ref, y_ref, hr_ref, hi_ref, car_r, car_i):
        t = pl.program_id(2)

        @pl.when(t == 0)
        def _():
            car_r[...] = jnp.zeros_like(car_r)
            car_i[...] = jnp.zeros_like(car_i)

        u = x_ref[...]
        ub = u.astype(BF16)
        hr_ref[...] = jnp.dot(ub, bbr_ref[...], preferred_element_type=F32)
        hi_ref[...] = jnp.dot(ub, bbi_ref[...], preferred_element_type=F32)

        def step(i, carry):
            sl = pl.ds(pl.multiple_of(i * 8, 8), 8)
            vr, vi = _scan8(hr_ref[sl, :], hi_ref[sl, :], tab_ref, 0, False)
            c_r = jnp.broadcast_to(car_r[...], (8, st))
            c_i = jnp.broadcast_to(car_i[...], (8, st))
            pr = tab_ref[6]
            pi = tab_ref[7]
            vr, vi = vr + pr * c_r - pi * c_i, vi + pr * c_i + pi * c_r
            hr_ref[sl, :] = vr
            hi_ref[sl, :] = vi
            car_r[...] = vr[7:8, :]
            car_i[...] = vi[7:8, :]
            return carry

        lax.fori_loop(0, nblk, step, 0)
        y = lax.dot_general(hr_ref[...].astype(BF16), cr_ref[...], NT, preferred_element_type=F32)
        y = y - lax.dot_general(hi_ref[...].astype(BF16), ci_ref[...], NT, preferred_element_type=F32)
        y_ref[...] = y + d_ref[...] * u

    row = lambda c, b, t: (b * nt + t, c)
    mat = pl.BlockSpec((None, CHUNK_CH, st), lambda c, b, t: (c, 0, 0))
    return _call(
        body, "s5_scan_fwd", (SSM_CHUNKS, n // seq, nt),
        [pl.BlockSpec((T_SCAN, CHUNK_CH), row), mat, mat, mat, mat,
         pl.BlockSpec((1, CHUNK_CH), lambda c, b, t: (0, c)),
         pl.BlockSpec((16, 8, st), lambda c, b, t: (0, 0, c))],
        [pl.BlockSpec((T_SCAN, CHUNK_CH), row), pl.BlockSpec((T_SCAN, st), row), pl.BlockSpec((T_SCAN, st), row)],
        [_sds((n, d), F32), _sds((n, N_STATE), F32), _sds((n, N_STATE), F32)],
        [pltpu.VMEM((1, st), F32), pltpu.VMEM((1, st), F32)], comm=comm,
    )(x, bbr, bbi, cr, ci, dvec, tab)


def _s5_scan_bwd(x, dy, dres, hr, hi, bbr, bbi, cr, ci, dvec, tab, seq, comm=None):
    n, d = x.shape
    nt = seq // T_SCAN
    nblk = T_SCAN // 8
    st = CHUNK_ST

    def body(x_ref, dy_ref, dres_ref, hr_ref, hi_ref, hpr_ref, hpi_ref, bbr_ref, bbi_ref, cr_ref, ci_ref,
             d_ref, tab_ref, dx_ref, dcr_ref, dci_ref, dbr_ref, dbi_ref, dar_ref, dai_ref, dd_ref,
             g_r, g_i, car_r, car_i):
        b = pl.program_id(1)
        tg = pl.program_id(2)
        first = jnp.logical_and(b == 0, tg == 0)

        @pl.when(tg == 0)
        def _():
            car_r[...] = jnp.zeros_like(car_r)
            car_i[...] = jnp.zeros_like(car_i)

        u = x_ref[...]
        dyv = dy_ref[...]
        ub = u.astype(BF16)
        dyb = dyv.astype(BF16)
        g_r[...] = jnp.dot(dyb, cr_ref[...], preferred_element_type=F32)
        g_i[...] = -jnp.dot(dyb, ci_ref[...], preferred_element_type=F32)

        def step(ii, carry):
            i = nblk - 1 - ii
            sl = pl.ds(pl.multiple_of(i * 8, 8), 8)
            vr, vi = _scan8(g_r[sl, :], g_i[sl, :], tab_ref, 8, True)
            c_r = jnp.broadcast_to(car_r[...], (8, st))
            c_i = jnp.broadcast_to(car_i[...], (8, st))
            pr = tab_ref[14]
            pi = tab_ref[15]
            vr, vi = vr + pr * c_r - pi * c_i, vi + pr * c_i + pi * c_r
            g_r[sl, :] = vr
            g_i[sl, :] = vi
            car_r[...] = vr[0:1, :]
            car_i[...] = vi[0:1, :]
            return carry

        lax.fori_loop(0, nblk, step, 0)
        gr = g_r[...]
        gi = g_i[...]
        grb = gr.astype(BF16)
        gib = gi.astype(BF16)
        du = lax.dot_general(grb, bbr_ref[...], NT, preferred_element_type=F32)
        du = du + lax.dot_general(gib, bbi_ref[...], NT, preferred_element_type=F32)
        dx_ref[...] = DN_ALPHA * dres_ref[...] + du + d_ref[...] * dyv

        h_r = hr_ref[...]
        h_i = hi_ref[...]
        t_orig = nt - 1 - tg
        row = lax.broadcasted_iota(jnp.int32, (T_SCAN, st), 0)
        halo_ok = t_orig > 0
        top_r = jnp.where(halo_ok, hpr_ref[7:8, :], 0.0)
        top_i = jnp.where(halo_ok, hpi_ref[7:8, :], 0.0)
        hp_r = jnp.where(row == 0, jnp.broadcast_to(top_r, (T_SCAN, st)), pltpu.roll(h_r, 1, 0))
        hp_i = jnp.where(row == 0, jnp.broadcast_to(top_i, (T_SCAN, st)), pltpu.roll(h_i, 1, 0))
        dar = jnp.sum(gr * hp_r + gi * hp_i, axis=0, keepdims=True)
        dai = jnp.sum(gi * hp_r - gr * hp_i, axis=0, keepdims=True)
        dcr = lax.dot_general(dyb, h_r.astype(BF16), TN, preferred_element_type=F32)
        dci = lax.dot_general(dyb, h_i.astype(BF16), TN, preferred_element_type=F32)
        dbr = lax.dot_general(ub, grb, TN, preferred_element_type=F32)
        dbi = lax.dot_general(ub, gib, TN, preferred_element_type=F32)
        dd = jnp.sum(dyv * u, axis=0, keepdims=True)

        @pl.when(first)
        def _():
            dcr_ref[...] = dcr
            dci_ref[...] = dci
            dbr_ref[...] = dbr
            dbi_ref[...] = dbi
            dar_ref[...] = dar
            dai_ref[...] = dai
            dd_ref[...] = dd

        @pl.when(jnp.logical_not(first))
        def _():
            dcr_ref[...] += dcr
            dci_ref[...] += dci
            dbr_ref[...] += dbr
            dbi_ref[...] += dbi
            dar_ref[...] += dar
            dai_ref[...] += dai
            dd_ref[...] += dd

    row = lambda c, b, t: (b * nt + (nt - 1 - t), c)
    halo = lambda c, b, t: (jnp.maximum((b * nt + (nt - 1 - t)) * (T_SCAN // 8) - 1, 0), c)
    mat = pl.BlockSpec((None, CHUNK_CH, st), lambda c, b, t: (c, 0, 0))
    chan = pl.BlockSpec((T_SCAN, CHUNK_CH), row)
    state = pl.BlockSpec((T_SCAN, st), row)
    return _call(
        body, "s5_scan_bwd", (SSM_CHUNKS, n // seq, nt),
        [chan, chan, chan, state, state, pl.BlockSpec((8, st), halo), pl.BlockSpec((8, st), halo),
         mat, mat, mat, mat, pl.BlockSpec((1, CHUNK_CH), lambda c, b, t: (0, c)),
         pl.BlockSpec((16, 8, st), lambda c, b, t: (0, 0, c))],
        [chan, mat, mat, mat, mat, pl.BlockSpec((1, st), lambda c, b, t: (0, c)),
         pl.BlockSpec((1, st), lambda c, b, t: (0, c)), pl.BlockSpec((1, CHUNK_CH), lambda c, b, t: (0, c))],
        [_sds((n, d), F32)] + [_sds((SSM_CHUNKS, CHUNK_CH, st), F32)] * 4
        + [_sds((1, N_STATE), F32), _sds((1, N_STATE), F32), _sds((1, d), F32)],
        [pltpu.VMEM((T_SCAN, st), F32), pltpu.VMEM((T_SCAN, st), F32), pltpu.VMEM((1, st), F32),
         pltpu.VMEM((1, st), F32)], comm=comm,
    )(x, dy, dres, hr, hi, hr, hi, bbr, bbi, cr, ci, dvec, tab)


_GELU_C = math.sqrt(2.0 / math.pi)


def _gelu_parts(y):
    t = jnp.tanh(_GELU_C * (y + 0.044715 * (y * y * y)))
    return 0.5 * (1.0 + t), t


def _glu_fwd(y, w_glu, b_glu):
    n, d = y.shape

    def body(y_ref, w_ref, b_ref, y2_ref, z_ref, gg_ref):
        yv = y_ref[...]
        cdf, _ = _gelu_parts(yv)
        y2 = yv * cdf
        z = jnp.dot(y2.astype(BF16), w_ref[...], preferred_element_type=F32) + b_ref[...]
        y2_ref[...] = y2.astype(BF16)
        z_ref[...] = z
        gg_ref[...] = (y2 * jax.nn.sigmoid(z)).astype(BF16)

    row = pl.BlockSpec((TM, d), lambda i: (i, 0))
    return _call(body, "glu_fwd", (n // TM,),
                 [row, pl.BlockSpec((d, d), lambda i: (0, 0)), pl.BlockSpec((1, d), lambda i: (0, 0))],
                 [row, row, row], [_sds((n, d), BF16), _sds((n, d), F32), _sds((n, d), BF16)])(y, w_glu, b_glu)


def _glu_bwd_gate(y, z, dgg):
    n, d = y.shape

    def body(y_ref, z_ref, dgg_ref, dz_ref, db_ref, t_ref):
        i = pl.program_id(0)
        yv = y_ref[...]
        cdf, _ = _gelu_parts(yv)
        y2 = yv * cdf
        s = jax.nn.sigmoid(z_ref[...])
        dg = dgg_ref[...]
        dz = dg * y2 * s * (1.0 - s)
        dz_ref[...] = dz.astype(BF16)
        t_ref[...] = dg * s
        db = jnp.sum(dz, axis=0, keepdims=True)

        @pl.when(i == 0)
        def _():
            db_ref[...] = db

        @pl.when(i > 0)
        def _():
            db_ref[...] += db

    row = pl.BlockSpec((TM, d), lambda i: (i, 0))
    vec = pl.BlockSpec((1, d), lambda i: (0, 0))
    return _call(body, "glu_bwd_gate", (n // TM,), [row, row, row], [row, vec, row],
                 [_sds((n, d), BF16), _sds((1, d), F32), _sds((n, d), F32)])(y, z, dgg)


def _gelu_bwd(y, dy2):
    n, d = y.shape

    def body(y_ref, g_ref, o_ref):
        yv = y_ref[...]
        cdf, t = _gelu_parts(yv)
        dinner = _GELU_C * (1.0 + 3.0 * 0.044715 * yv * yv)
        o_ref[...] = g_ref[...] * (cdf + 0.5 * yv * (1.0 - t * t) * dinner)

    row = pl.BlockSpec((TM, d), lambda i: (i, 0))
    return _call(body, "gelu_bwd", (n // TM,), [row, row], row, _sds((n, d), F32))(y, dy2)


def _shift_down(x, halo, k, rows):
    out = pltpu.roll(x, k, 0)
    row = lax.broadcasted_iota(jnp.int32, x.shape, 0)
    for j in range(k):
        out = jnp.where(row == j, jnp.broadcast_to(halo[8 - k + j:8 - k + j + 1, :], x.shape), out)
    return out


def _conv3(x, halo, w, b):
    return (b + w[0:1, :] * x + w[1:2, :] * _shift_down(x, halo, 1, None)
            + w[2:3, :] * _shift_down(x, halo, 2, None))


def _conv_gate_fwd(hc, conv_w, conv_b, seq):
    _, nb, n, c = hc.shape
    tiles_per_seq = seq // TM

    def body(x_ref, halo_ref, w_ref, b_ref, o_ref):
        i = pl.program_id(1)
        start = (i % tiles_per_seq) == 0
        cv = []
        for h in range(2):
            halo = jnp.where(start, 0.0, halo_ref[h])
            cv.append(_conv3(x_ref[h], halo, w_ref[h], b_ref[h]))
        o_ref[...] = (cv[1] * jax.nn.sigmoid(cv[1]) * cv[0]).astype(BF16)

    return _call(
        body, "conv_gate_fwd", (nb, n // TM),
        [pl.BlockSpec((2, None, TM, c), lambda d, i: (0, d, i, 0)),
         pl.BlockSpec((2, None, 8, c), lambda d, i: (0, d, jnp.maximum(i * (TM // 8) - 1, 0), 0)),
         pl.BlockSpec((2, None, 3, c), lambda d, i: (0, d, 0, 0)),
         pl.BlockSpec((2, None, 1, c), lambda d, i: (0, d, 0, 0))],
        pl.BlockSpec((None, TM, c), lambda d, i: (d, i, 0)),
        _sds((nb, n, c), BF16))(hc, hc, conv_w, conv_b)


def _conv_gate_bwd(hc, dact, conv_w, conv_b, seq, comm=None):
    _, nb, n, c = hc.shape
    tiles_per_seq = seq // TM
    last_halo = n // 8 - 1
    ext = TM + 8

    def body(x_ref, prev_ref, next_ref, da_ref, dan_ref, w_ref, b_ref, dx_ref, dw_ref):
        i = pl.program_id(1)
        start = (i % tiles_per_seq) == 0
        end = (i % tiles_per_seq) == tiles_per_seq - 1
        row = lax.broadcasted_iota(jnp.int32, (ext, c), 0)
        xe, cv = [], []
        for h in range(2):
            halo = jnp.where(start, 0.0, prev_ref[h])
            x_ext = jnp.concatenate([x_ref[h], next_ref[h]], axis=0)
            xe.append((x_ext, halo))
            cv.append(_conv3(x_ext, halo, w_ref[h], b_ref[h]))
        da = jnp.concatenate([da_ref[...], dan_ref[...]], axis=0)
        da = jnp.where(jnp.logical_and(end, row >= TM), 0.0, da)
        sg = jax.nn.sigmoid(cv[1])
        silu = cv[1] * sg
        dcv = [da * silu, da * cv[0] * (sg + silu * (1.0 - sg))]
        for h in range(2):
            w = w_ref[h]
            g = dcv[h]
            dx = w[0:1, :] * g + w[1:2, :] * pltpu.roll(g, ext - 1, 0) + w[2:3, :] * pltpu.roll(g, ext - 2, 0)
            dx_ref[h] = dx[:TM, :].astype(BF16)
            x_ext, halo = xe[h]
            gt = g[:TM, :]
            xt = x_ext[:TM, :]
            parts = [jnp.sum(gt * xt, axis=0, keepdims=True),
                     jnp.sum(gt * _shift_down(xt, halo, 1, None), axis=0, keepdims=True),
                     jnp.sum(gt * _shift_down(xt, halo, 2, None), axis=0, keepdims=True),
                     jnp.sum(gt, axis=0, keepdims=True)]
            upd = jnp.concatenate(parts + [jnp.zeros((4, c), F32)], axis=0)

            @pl.when(i == 0)
            def _():
                dw_ref[h] = upd

            @pl.when(i > 0)
            def _():
                dw_ref[h] += upd

    nxt = lambda i: jnp.minimum((i + 1) * (TM // 8), last_halo)
    return _call(
        body, "conv_gate_bwd", (nb, n // TM),
        [pl.BlockSpec((2, None, TM, c), lambda d, i: (0, d, i, 0)),
         pl.BlockSpec((2, None, 8, c), lambda d, i: (0, d, jnp.maximum(i * (TM // 8) - 1, 0), 0)),
         pl.BlockSpec((2, None, 8, c), lambda d, i: (0, d, nxt(i), 0)),
         pl.BlockSpec((None, TM, c), lambda d, i: (d, i, 0)),
         pl.BlockSpec((None, 8, c), lambda d, i: (d, nxt(i), 0)),
         pl.BlockSpec((2, None, 3, c), lambda d, i: (0, d, 0, 0)),
         pl.BlockSpec((2, None, 1, c), lambda d, i: (0, d, 0, 0))],
        [pl.BlockSpec((2, None, TM, c), lambda d, i: (0, d, i, 0)),
         pl.BlockSpec((2, None, 8, c), lambda d, i: (0, d, 0, 0))],
        [_sds((2, nb, n, c), BF16), _sds((2, nb, 8, c), F32)], comm=comm)(hc, hc, hc, dact, dact, conv_w, conv_b)


def _t5_bucket_np(dist):
    exact = REL_BUCKETS // 2
    d = np.maximum(dist, 1).astype(np.float32)
    large = exact + (np.log(d / exact) / math.log(REL_MAX_DIST / exact) * (REL_BUCKETS - exact)).astype(np.int64)
    large = np.minimum(large, REL_BUCKETS - 1)
    return np.where(dist < exact, dist, large).astype(np.int32)


def _bucket_table(dil):
    steps = np.arange(BAND)[:, None] + BAND - np.arange(2 * BAND)[None, :]
    return _t5_bucket_np(np.maximum(steps, 0) * dil).reshape(1, BAND * 2 * BAND).astype(np.float32)


def _bias_expand(rel_t, dil):
    nk = BAND * 2 * BAND
    bucket = jnp.asarray(_bucket_table(dil))

    def body(r_ref, bk_ref, o_ref):
        ids = lax.broadcasted_iota(jnp.int32, (REL_BUCKETS, nk), 0).astype(F32)
        onehot = (ids == bk_ref[...]).astype(F32)
        o_ref[...] = jnp.dot(r_ref[...], onehot, precision=HIGHEST, preferred_element_type=F32)

    return _call(body, "bias_expand", (1,),
                 [pl.BlockSpec((N_HEADS, REL_BUCKETS), lambda i: (0, 0)), pl.BlockSpec((1, nk), lambda i: (0, 0))],
                 pl.BlockSpec((N_HEADS, nk), lambda i: (0, 0)), _sds((N_HEADS, nk), F32))(rel_t, bucket)


def _bias_reduce(dbias, dil):
    nk = BAND * 2 * BAND
    bucket = jnp.asarray(_bucket_table(dil))

    def body(g_ref, bk_ref, o_ref):
        ids = lax.broadcasted_iota(jnp.int32, (REL_BUCKETS, nk), 0).astype(F32)
        onehot = (ids == bk_ref[...]).astype(F32)
        o_ref[...] = lax.dot_general(g_ref[...], onehot, NT, precision=HIGHEST, preferred_element_type=F32)

    return _call(body, "bias_reduce", (1,),
                 [pl.BlockSpec((N_HEADS, nk), lambda i: (0, 0)), pl.BlockSpec((1, nk), lambda i: (0, 0))],
                 pl.BlockSpec((N_HEADS, REL_BUCKETS), lambda i: (0, 0)),
                 _sds((N_HEADS, REL_BUCKETS), F32))(dbias, bucket)


def _valid_mask(n):
    qi = lax.broadcasted_iota(jnp.int32, (BAND, 2 * BAND), 0)
    kj = lax.broadcasted_iota(jnp.int32, (BAND, 2 * BAND), 1)
    steps = qi + BAND - kj
    in_band = jnp.logical_and(steps >= 0, steps <= BAND)
    return jnp.logical_and(in_band, jnp.logical_or(n > 0, kj >= BAND))


ATTN_COLS = {1: 1024, 4: 512, 16: 128}


def _residue(dil, r):
    return slice(None) if dil == 1 else pl.ds(r, BAND, stride=dil)


def _attn_fwd(q, k, v, bias, g, dil, bsz, seq):
    n = q.shape[0]
    rows = BAND * dil
    nch = seq // rows
    w = N_HEADS * HEAD_DIM
    wc = ATTN_COLS[dil]
    ncol = w // wc
    hpc = wc // HEAD_DIM
    nj = wc // 128
    slab = lambda k: pltpu.VMEM((k * nj, rows, 128), F32)

    def body(q_ref, kc_ref, kp_ref, vc_ref, vp_ref, b_ref, o_ref, l_ref, qf, kf, vf, of, lf):
        ch = pl.program_id(2)
        valid = _valid_mask(ch)
        lane = lax.broadcasted_iota(jnp.int32, (BAND, 128), 1)
        for j in range(nj):
            sl = slice(128 * j, 128 * (j + 1))
            qf[j] = q_ref[:, sl].astype(F32)
            kf[j] = kp_ref[:, sl].astype(F32)
            kf[nj + j] = kc_ref[:, sl].astype(F32)
            vf[j] = vp_ref[:, sl].astype(F32)
            vf[nj + j] = vc_ref[:, sl].astype(F32)
        for r in range(dil):
            rr = _residue(dil, r)
            for j in range(nj):
                q2 = qf.at[j][rr, :].astype(BF16)
                k2 = jnp.concatenate([kf.at[j][rr, :], kf.at[nj + j][rr, :]], axis=0).astype(BF16)
                v2 = jnp.concatenate([vf.at[j][rr, :], vf.at[nj + j][rr, :]], axis=0).astype(BF16)
                outs, lses = [], []
                for half in range(2):
                    mine = (lane < HEAD_DIM) if half == 0 else (lane >= HEAD_DIM)
                    qh = jnp.where(mine, q2, jnp.zeros_like(q2))
                    s = lax.dot_general(qh, k2, NT, preferred_element_type=F32) * (HEAD_DIM ** -0.5)
                    s = jnp.where(valid, s + b_ref[2 * j + half], NEG_BIG)
                    m = jnp.max(s, axis=1, keepdims=True)
                    p = jnp.exp(s - m)
                    l = jnp.sum(p, axis=1, keepdims=True)
                    pv = jnp.dot(p.astype(BF16), v2, preferred_element_type=F32)
                    outs.append(pv / l)
                    lses.append(jnp.broadcast_to(m + jnp.log(l), (BAND, 128)))
                of.at[j][rr, :] = jnp.where(lane < HEAD_DIM, outs[0], outs[1])
                lf.at[j][rr, :] = jnp.where(lane < HEAD_DIM, lses[0], lses[1])
        for j in range(nj):
            sl = slice(128 * j, 128 * (j + 1))
            o_ref[:, sl] = of[j]
            l_ref[:, sl] = lf[j]

    base = g * ncol
    cur = lambda c, b, ch: (b * nch + ch, base + c)
    prev = lambda c, b, ch: (b * nch + jnp.maximum(ch - 1, 0), base + c)
    blk = lambda im: pl.BlockSpec((rows, wc), im)
    out_blk = pl.BlockSpec((rows, wc), lambda c, b, ch: (b * nch + ch, c))
    return _call(
        body, "attn_fwd_d%d" % dil, (ncol, bsz, nch),
        [blk(cur), blk(cur), blk(prev), blk(cur), blk(prev),
         pl.BlockSpec((hpc, BAND, 2 * BAND), lambda c, b, ch: (c, 0, 0))],
        [out_blk, out_blk],
        [_sds((n, w), F32)] * 2,
        [slab(1), slab(2), slab(2), slab(1), slab(1)],
    )(q, k, k, v, v, bias)


def _attn_combine(outs, lses):
    n, w = outs[0].shape

    def body(o0, o1, o2, l0, l1, l2, o_ref, l_ref):
        la, lb, lc = l0[...], l1[...], l2[...]
        m = jnp.maximum(jnp.maximum(la, lb), lc)
        wa, wb, wc = jnp.exp(la - m), jnp.exp(lb - m), jnp.exp(lc - m)
        tot = wa + wb + wc
        o_ref[...] = (wa * o0[...] + wb * o1[...] + wc * o2[...]) / tot
        l_ref[...] = m + jnp.log(tot)

    row = pl.BlockSpec((TM, w), lambda i: (i, 0))
    return _call(body, "attn_combine", (n // TM,), [row] * 6, [row, row],
                 [_sds((n, w), F32)] * 2)(*outs, *lses)


def _attn_bwd(q, k, v, do, o, lse, bias, g, dil, bsz, seq, comm=None):
    n = q.shape[0]
    rows = BAND * dil
    nch = seq // rows
    w = N_HEADS * HEAD_DIM
    wc = ATTN_COLS[dil]
    ncol = w // wc
    hpc = wc // HEAD_DIM
    nj = wc // 128
    slab = lambda k: pltpu.VMEM((k * nj, rows, 128), F32)
    scale = HEAD_DIM ** -0.5

    def body(q_ref, kc_ref, kp_ref, vc_ref, vp_ref, do_ref, o_ref, l_ref, b_ref,
             dq_ref, dk_ref, dv_ref, db_ref, qf, kf, vf, dof, of, lf, dqf, dkf, dvf):
        b = pl.program_id(1)
        ch = pl.program_id(2)
        cur = ch % 2
        prv = 1 - cur

        @pl.when(jnp.logical_and(b == 0, ch == 0))
        def _():
            db_ref[...] = jnp.zeros_like(db_ref)

        @pl.when(ch == 0)
        def _():
            for j in range(nj):
                dkf[nj + j] = jnp.zeros((rows, 128), F32)
                dvf[nj + j] = jnp.zeros((rows, 128), F32)

        @pl.when(ch < nch)
        def _():
            valid = _valid_mask(ch)
            lane = lax.broadcasted_iota(jnp.int32, (BAND, 128), 1)
            for j in range(nj):
                sl = slice(128 * j, 128 * (j + 1))
                qf[j] = q_ref[:, sl].astype(F32)
                kf[j] = kp_ref[:, sl].astype(F32)
                kf[nj + j] = kc_ref[:, sl].astype(F32)
                vf[j] = vp_ref[:, sl].astype(F32)
                vf[nj + j] = vc_ref[:, sl].astype(F32)
                dof[j] = do_ref[:, sl]
                of[j] = o_ref[:, sl]
                lf[j] = l_ref[:, sl]
            for r in range(dil):
                rr = _residue(dil, r)
                for j in range(nj):
                    q2 = qf.at[j][rr, :].astype(BF16)
                    k2 = jnp.concatenate([kf.at[j][rr, :], kf.at[nj + j][rr, :]], axis=0).astype(BF16)
                    v2 = jnp.concatenate([vf.at[j][rr, :], vf.at[nj + j][rr, :]], axis=0).astype(BF16)
                    do2 = dof.at[j][rr, :]
                    prod = do2 * of.at[j][rr, :]
                    lse2 = lf.at[j][rr, :]
                    dqs = []
                    dk2 = jnp.zeros((2 * BAND, 128), F32)
                    dv2 = jnp.zeros((2 * BAND, 128), F32)
                    for half in range(2):
                        mine = (lane < HEAD_DIM) if half == 0 else (lane >= HEAD_DIM)
                        qh = jnp.where(mine, q2, jnp.zeros_like(q2))
                        doh = jnp.where(mine, do2, 0.0).astype(BF16)
                        s = lax.dot_general(qh, k2, NT, preferred_element_type=F32) * scale
                        s = jnp.where(valid, s + b_ref[2 * j + half], NEG_BIG)
                        lse_h = lse2[:, 64 * half:64 * half + 1]
                        p = jnp.exp(s - lse_h)
                        dp = lax.dot_general(doh, v2, NT, preferred_element_type=F32)
                        delta = jnp.sum(jnp.where(mine, prod, 0.0), axis=1, keepdims=True)
                        ds = p * (dp - delta)
                        db_ref[2 * j + half] += ds
                        dsb = ds.astype(BF16)
                        dqs.append(jnp.dot(dsb, k2, preferred_element_type=F32) * scale)
                        dk2 = dk2 + lax.dot_general(dsb, qh, TN, preferred_element_type=F32) * scale
                        dv2 = dv2 + lax.dot_general(p.astype(BF16), doh, TN, preferred_element_type=F32)
                    dqf.at[j][rr, :] = jnp.where(lane < HEAD_DIM, dqs[0], dqs[1])
                    dkf.at[prv * nj + j][rr, :] += dk2[:BAND, :]
                    dvf.at[prv * nj + j][rr, :] += dv2[:BAND, :]
                    dkf.at[cur * nj + j][rr, :] = dk2[BAND:, :]
                    dvf.at[cur * nj + j][rr, :] = dv2[BAND:, :]
            for j in range(nj):
                dq_ref[:, 128 * j:128 * (j + 1)] = dqf[j].astype(BF16)

        for j in range(nj):
            sl = slice(128 * j, 128 * (j + 1))
            dk_ref[:, sl] = dkf[prv * nj + j].astype(BF16)
            dv_ref[:, sl] = dvf[prv * nj + j].astype(BF16)

    last = nch - 1
    base = g * ncol
    cur3 = lambda c, b, ch: (b * nch + jnp.minimum(ch, last), base + c)
    prev3 = lambda c, b, ch: (b * nch + jnp.maximum(jnp.minimum(ch, last) - 1, 0), base + c)
    cur1 = lambda c, b, ch: (b * nch + jnp.minimum(ch, last), c)
    lag1 = lambda c, b, ch: (b * nch + jnp.maximum(ch - 1, 0), c)
    blk = lambda im: pl.BlockSpec((rows, wc), im)
    bias_blk = pl.BlockSpec((hpc, BAND, 2 * BAND), lambda c, b, ch: (c, 0, 0))
    out = _sds((n, w), BF16)
    return _call(
        body, "attn_bwd_d%d" % dil, (ncol, bsz, nch + 1),
        [blk(cur3), blk(cur3), blk(prev3), blk(cur3), blk(prev3), blk(cur1), blk(cur1), blk(cur1), bias_blk],
        [blk(cur1), blk(lag1), blk(lag1), bias_blk],
        [out, out, out, _sds((N_HEADS, BAND, 2 * BAND), F32)],
        [slab(1), slab(2), slab(2), slab(1), slab(1), slab(1), slab(1), slab(2), slab(2)], comm=comm,
    )(q, k, k, v, v, do, o, lse, bias)


def _adamw(name, parts, w, m, v):
    r, c = w.shape
    nparts = parts.shape[0]
    tr = r
    for cand in (256, 352, 128):
        if r % cand == 0 and r > cand:
            tr = cand
            break
    c1 = 1.0 - ADAM_B1 ** ADAM_STEP
    c2 = 1.0 - ADAM_B2 ** ADAM_STEP

    def body(p_ref, w_ref, m_ref, v_ref, g_ref, d_ref, nm_ref, nv_ref):
        g = p_ref[0].astype(F32)
        for dev in range(1, nparts):
            g = g + p_ref[dev].astype(F32)
        nm = ADAM_B1 * m_ref[...] + (1.0 - ADAM_B1) * g
        nv = ADAM_B2 * v_ref[...] + (1.0 - ADAM_B2) * (g * g)
        m_hat = nm / c1
        v_hat = nv / c2
        g_ref[...] = g
        d_ref[...] = -ADAM_LR * (m_hat / (jnp.sqrt(v_hat) + ADAM_EPS) + ADAM_WD * w_ref[...])
        nm_ref[...] = nm
        nv_ref[...] = nv

    row = pl.BlockSpec((tr, c), lambda i: (i, 0))
    return _call(body, name, (r // tr,), [pl.BlockSpec((nparts, tr, c), lambda i: (0, i, 0)), row, row, row],
                 [row] * 4, [_sds((r, c), F32)] * 4)(parts, w, m, v)


def _exchange(name, srcs, out_shapes, items):
    n_in = len(srcs)
    n_out = len(out_shapes)
    n_items = len(items)

    def body(*refs):
        ins = refs[:n_in]
        outs = refs[n_in:n_in + n_out]
        send_sems, recv_sems, local_sems = refs[n_in + n_out:]
        x, y, c = lax.axis_index("x"), lax.axis_index("y"), lax.axis_index("c")
        me = 4 * x + 2 * y + c

        def pick(ref, sel, peer):
            idx = tuple(peer if s == "peer" else me if s == "me" else s for s in sel)
            return ref.at[idx] if idx else ref

        copies = []
        for it, (si, ssel, oi, osel) in enumerate(items):
            local = pltpu.make_async_copy(pick(ins[si], ssel, me), pick(outs[oi], osel, me), local_sems.at[it])
            local.start()
            copies.append(local)
            for j in range(1, N_DEV):
                px = 1 - x if j & 4 else x
                py = 1 - y if j & 2 else y
                pc = 1 - c if j & 1 else c
                peer = 4 * px + 2 * py + pc
                sem = it * (N_DEV - 1) + j - 1
                cp = pltpu.make_async_remote_copy(
                    src_ref=pick(ins[si], ssel, peer), dst_ref=pick(outs[oi], osel, peer),
                    send_sem=send_sems.at[sem], recv_sem=recv_sems.at[sem],
                    device_id=(px, py, pc), device_id_type=pl.DeviceIdType.MESH)
                cp.start()
                copies.append(cp)
        for cp in copies:
            cp.wait()

    any_spec = pl.BlockSpec(memory_space=pl.ANY)
    return pl.pallas_call(
        body, name=name, in_specs=[any_spec] * n_in, out_specs=[any_spec] * n_out, out_shape=list(out_shapes),
        scratch_shapes=[pltpu.SemaphoreType.DMA((n_items * (N_DEV - 1),)),
                        pltpu.SemaphoreType.DMA((n_items * (N_DEV - 1),)),
                        pltpu.SemaphoreType.DMA((n_items,))],
        compiler_params=pltpu.CompilerParams(has_side_effects=True),
    )(*srcs)


def _sel(ref, sel, slot=None):
    idx = tuple(slot if s == "slot" else s for s in sel)
    return ref.at[idx] if idx else ref


def _gather_plan(srcs, out_shapes, items):
    per = N_DEV - 1
    n_items = len(items)

    def phases(ins, outs, send_sems, recv_sems, local_sems, total):
        x, y, c = lax.axis_index("x"), lax.axis_index("y"), lax.axis_index("c")
        sibling = (x, y, 1 - c)
        chips = [(1 - x, y), (x, 1 - y), (1 - x, 1 - y)]
        slot_of = lambda px, py, pc: 4 * px + 2 * py + pc

        def copy(it, k, block, to, src=None):
            _, _, oi, osel = items[it]
            dst = _sel(outs[oi], osel, slot_of(*block))
            return pltpu.make_async_remote_copy(
                src_ref=dst if src is None else src, dst_ref=dst,
                send_sem=send_sems.at[it * per + k], recv_sem=recv_sems.at[it * per + k],
                device_id=to, device_id_type=pl.DeviceIdType.MESH)

        def own_copies(it):
            si, ssel, oi, osel = items[it]
            src = _sel(ins[si], ssel)
            mine = pltpu.make_async_copy(src, _sel(outs[oi], osel, slot_of(x, y, c)), local_sems.at[it])
            first = [copy(it, 0, (x, y, c), sibling, src=src)]
            first += [copy(it, 1 + j, (x, y, c), (*chip, c), src=src) for j, chip in enumerate(chips)]
            return mine, first

        def start():
            for it in range(n_items):
                mine, first = own_copies(it)
                mine.start()
                for cp in first:
                    cp.start()

        def forward(it):
            def go():
                for j, chip in enumerate(chips):
                    copy(it, 1 + j, (*chip, c), (x, y, c)).wait_recv()
                    copy(it, 4 + j, (*chip, c), sibling).start()
            return go

        def finish():
            for it in range(n_items):
                copy(it, 0, sibling, (x, y, c)).wait_recv()
                for j, chip in enumerate(chips):
                    copy(it, 4 + j, (*chip, 1 - c), (x, y, c)).wait_recv()
            for it in range(n_items):
                mine, first = own_copies(it)
                for cp in first:
                    cp.wait_send()
                for j, chip in enumerate(chips):
                    copy(it, 4 + j, (*chip, c), sibling).wait_send()
                mine.wait()

        lo = max(total // 3, 1)
        acts = [(0, start)]
        for it in range(n_items):
            acts.append((min(lo + it * (total - 1 - lo) // n_items, total - 2), forward(it)))
        acts.append((total - 1, finish))
        return acts

    return _Plan(srcs, out_shapes, n_items * per, n_items, phases)


def _scatter_plan(srcs, out_shapes, items):
    per = N_DEV - 1
    n_items = len(items)

    def phases(ins, outs, send_sems, recv_sems, local_sems, total):
        x, y, c = lax.axis_index("x"), lax.axis_index("y"), lax.axis_index("c")
        me = 4 * x + 2 * y + c

        def copies():
            out = []
            for it, (si, oi, osel) in enumerate(items):
                dst = _sel(outs[oi], osel, me)
                out.append(pltpu.make_async_copy(ins[si].at[me], dst, local_sems.at[it]))
                for j in range(1, N_DEV):
                    px = 1 - x if j & 4 else x
                    py = 1 - y if j & 2 else y
                    pc = 1 - c if j & 1 else c
                    out.append(pltpu.make_async_remote_copy(
                        src_ref=ins[si].at[4 * px + 2 * py + pc], dst_ref=dst,
                        send_sem=send_sems.at[it * per + j - 1], recv_sem=recv_sems.at[it * per + j - 1],
                        device_id=(px, py, pc), device_id_type=pl.DeviceIdType.MESH))
            return out

        def start():
            for cp in copies():
                cp.start()

        def finish():
            for cp in copies():
                cp.wait()

        return [(0, start), (total - 1, finish)]

    return _Plan(srcs, out_shapes, n_items * per, n_items, phases)


def _sum_slots(parts):
    _, r, c = parts.shape

    def body(p_ref, o_ref):
        acc = p_ref[0]
        for dev in range(1, N_DEV):
            acc = acc + p_ref[dev]
        o_ref[...] = acc

    return _call(body, "sum_slots", (1,), [pl.BlockSpec((N_DEV, r, c), lambda i: (0, 0, 0))],
                 pl.BlockSpec((r, c), lambda i: (0, 0)), _sds((r, c), F32))(parts)


def _pack(arrays):
    rows = []
    for a in arrays:
        flat = a.reshape(-1).astype(F32)
        pad = (-flat.shape[0]) % 1024
        if pad:
            flat = jnp.concatenate([flat, jnp.zeros((pad,), F32)])
        rows.append(flat.reshape(-1, 128))
    return jnp.concatenate(rows, axis=0)


def _unpack(buf, shapes):
    out = []
    r0 = 0
    for shp in shapes:
        size = int(np.prod(shp))
        nrows = -(-size // 1024) * 8
        out.append(buf[r0:r0 + nrows].reshape(-1)[:size].reshape(shp))
        r0 += nrows
    return out


def _block_diag_chunks(val):
    eye = jnp.eye(8, dtype=val.dtype)
    return jnp.einsum("cjhp,jk->cjhkp", val, eye).reshape(SSM_CHUNKS, CHUNK_CH, CHUNK_ST)


def _diag_blocks(dense):
    d5 = dense.reshape(SSM_CHUNKS, 8, SSM_GROUP, 8, SSM_STATE)
    return jnp.stack([d5[:, j, :, j, :] for j in range(8)], axis=1)


def _ffn_fwd(tag, hin, w_up, conv_w, conv_b, w_down4, gain, bias, seq, comm=None):
    hc = _mm_nn("ffn_up" + tag, hin, w_up, F32, blocked_out=True, comm=comm)
    couts = None
    if comm is not None:
        hc, couts = hc
    nb, n, c = hc.shape
    hc = hc.reshape(2, nb // 2, n, c)
    act = _conv_gate_fwd(hc, conv_w, conv_b, seq)
    ffn = _mm_nn_red("ffn_down" + tag, act, w_down4, F32)
    hout = _ln_fwd("ffn_norm" + tag, hin, ffn, gain, bias)
    return hout, (hin, hc, act, ffn), couts


def _ffn_bwd(tag, saved, dh, w_up, conv_w, conv_b, w_down4, gain, seq, comm_conv=None, comm_dx=None):
    hin, hc, act, ffn = saved
    dr, dgain, dbias = _ln_bwd("ffn_norm_bwd" + tag, hin, ffn, dh, gain)
    d_wdown = _mm_tn("ffn_down_dw" + tag, act, dr, act.shape[0], BF16, a_blocked=True, g_mode="shared")
    dact = _mm_nt_out("ffn_down_dx" + tag, dr, w_down4, F32)
    res = _conv_gate_bwd(hc, dact, conv_w, conv_b, seq, comm=comm_conv)
    couts_conv = None
    if comm_conv is not None:
        res, couts_conv = res
    dhc, dconv = res
    dhc8 = dhc.reshape(2 * dhc.shape[1], dhc.shape[2], dhc.shape[3])
    d_wup = _mm_tn("ffn_up_dw" + tag, hin, dhc8, dhc8.shape[0], BF16, g_mode="blocked")
    dhin = _mm_nt_red("ffn_up_dx" + tag, dhc8, w_up, F32, g_blocked=True, add=dr, add_scale=DN_ALPHA,
                      comm=comm_dx)
    couts_dx = None
    if comm_dx is not None:
        dhin, couts_dx = dhin
    return dhin, d_wup, d_wdown, dconv, dgain, dbias, couts_conv, couts_dx


def _local_step(x, target, p, wfull, bsz, seq, ex=None):
    n = bsz * seq
    xf = x.reshape(n, D_MODEL)
    tf = target.reshape(n, D_MODEL)

    lr = p["s5_lam_re"].reshape(1, N_STATE)
    li = p["s5_lam_im"].reshape(1, N_STATE)
    ldt = p["s5_log_dt"].reshape(1, D_MODEL // SSM_GROUP)
    br_t = p["s5_b_re"].reshape(N_STATE, SSM_GROUP).T
    bi_t = p["s5_b_im"].reshape(N_STATE, SSM_GROUP).T
    ab_r, ab_i, bb_r, bb_i, tab = _s5_prep(lr, li, ldt, br_t, bi_t)

    def bb_chunks(bb):
        return _block_diag_chunks(bb.reshape(SSM_GROUP, SSM_CHUNKS, 8, SSM_STATE).transpose(1, 2, 0, 3))

    def c_chunks(cc):
        return _block_diag_chunks(cc.reshape(SSM_CHUNKS, 8, SSM_GROUP, SSM_STATE))

    bbr_c, bbi_c = bb_chunks(bb_r).astype(BF16), bb_chunks(bb_i).astype(BF16)
    cr_c, ci_c = c_chunks(p["s5_c_re"]).astype(BF16), c_chunks(p["s5_c_im"]).astype(BF16)
    dvec = p["s5_d"].reshape(1, D_MODEL)

    if ex is None:
        y, h_r, h_i = _s5_scan_fwd(xf, bbr_c, bbi_c, cr_c, ci_c, dvec, tab, seq)
    else:
        (y, h_r, h_i), gathered = _s5_scan_fwd(xf, bbr_c, bbi_c, cr_c, ci_c, dvec, tab, seq,
                                               comm=ex.gather_first())
        p, wfull = ex.take_first(gathered, p)
    ln_g = p["ln_gain"].reshape(4, 1, D_MODEL)
    ln_b = p["ln_bias"].reshape(4, 1, D_MODEL)
    y2, z, gg = _glu_fwd(y, wfull["s5_w_glu"], p["s5_b_glu"].reshape(1, D_MODEL))
    mix = _mm_nn("s5_out", gg, wfull["s5_w_out"][None], F32)
    h1 = _ln_fwd("s5_norm", xf, mix, ln_g[0], ln_b[0])
    h2, ffn0_saved, gathered = _ffn_fwd("0", h1, wfull["ffn_w_up"][0], p["conv_w"][0], p["conv_b"][0],
                                        wfull["ffn_w_down"][0], ln_g[1], ln_b[1], seq,
                                        comm=ex.gather_second() if ex else None)
    if ex is not None:
        wfull = ex.take_second(gathered, wfull)

    kmat = _mm_nn("attn_k", h2, wfull["attn_w_kv"][:4], BF16)
    vmat = _mm_nn("attn_v", h2, wfull["attn_w_kv"][4:], BF16)
    qmat = _mm_nn("attn_q", h2, wfull["attn_w_q"], BF16)
    biases, outs, lses = [], [], []
    for g, dil in enumerate(DILATIONS):
        rel_t = p["rel_bias"][:, g * N_HEADS:(g + 1) * N_HEADS].T
        biases.append(_bias_expand(rel_t, dil).reshape(N_HEADS, BAND, 2 * BAND))
        o_g, l_g = _attn_fwd(qmat, kmat, vmat, biases[g], g, dil, bsz, seq)
        outs.append(o_g)
        lses.append(l_g)
    o, lse = _attn_combine(outs, lses)
    attn_out = _mm_nn("attn_out", o, wfull["attn_w_out"][None], F32)
    h3 = _ln_fwd("attn_norm", h2, attn_out, ln_g[2], ln_b[2])
    h4, ffn1_saved, _ = _ffn_fwd("1", h3, wfull["ffn_w_up"][1], p["conv_w"][1], p["conv_b"][1],
                                 wfull["ffn_w_down"][1], ln_g[3], ln_b[3], seq)
    loss_tile, dh4 = _loss_head(h4, tf)

    grads = {}
    dh3, grads["ffn_w_up1"], grads["ffn_w_down1"], dconv1, dg3, db3, _, _ = _ffn_bwd(
        "1", ffn1_saved, dh4, wfull["ffn_w_up"][1], p["conv_w"][1], p["conv_b"][1], wfull["ffn_w_down"][1],
        ln_g[3], seq)
    dr3, dg2, db2 = _ln_bwd("attn_norm_bwd", h2, attn_out, dh3, ln_g[2])
    grads["attn_w_out"] = _mm_tn("attn_out_dw", o, dr3, 1, BF16, g_mode="shared")
    do = _mm_nt_red("attn_out_dx", dr3, wfull["attn_w_out"][None], F32, g_blocked=False)
    dqs, dks, dvs, drel = [], [], [], []
    for g, dil in enumerate(DILATIONS):
        plan = ex.scatter(grads, ("ffn_w_up1", "ffn_w_down1")) if ex and dil == DILATIONS[-1] else None
        res = _attn_bwd(qmat, kmat, vmat, do, o, lse, biases[g], g, dil, bsz, seq, comm=plan)
        if plan is not None:
            res, got = res
            ex.keep(plan, got)
        dq_g, dk_g, dv_g, db_g = res
        dqs.append(dq_g)
        dks.append(dk_g)
        dvs.append(dv_g)
        drel.append(_bias_reduce(db_g.reshape(N_HEADS, BAND * 2 * BAND), dil).T)
    dq = jnp.concatenate(dqs, axis=1)
    dk = jnp.concatenate(dks, axis=1)
    dv = jnp.concatenate(dvs, axis=1)
    grads["rel_bias"] = jnp.concatenate(drel, axis=1)
    grads["attn_w_q"] = _mm_tn("attn_q_dw", h2, dq, N_DEV, BF16)
    grads["attn_w_kv"] = jnp.concatenate([_mm_tn("attn_k_dw", h2, dk, 4, BF16),
                                          _mm_tn("attn_v_dw", h2, dv, 4, BF16)], axis=0)
    dh2 = _mm_nt_red("attn_q_dx", dq, wfull["attn_w_q"], F32, g_blocked=False, add=dr3, add_scale=DN_ALPHA)
    dh2 = _mm_nt_red("attn_k_dx", dk, wfull["attn_w_kv"][:4], F32, g_blocked=False, add=dh2)
    dh2 = _mm_nt_red("attn_v_dx", dv, wfull["attn_w_kv"][4:], F32, g_blocked=False, add=dh2)

    plan_conv = ex.scatter(grads, ("attn_w_q", "attn_w_out")) if ex else None
    plan_dx = ex.scatter(grads, ("attn_w_kv",)) if ex else None
    dh1, grads["ffn_w_up0"], grads["ffn_w_down0"], dconv0, dg1, db1, got_conv, got_dx = _ffn_bwd(
        "0", ffn0_saved, dh2, wfull["ffn_w_up"][0], p["conv_w"][0], p["conv_b"][0], wfull["ffn_w_down"][0],
        ln_g[1], seq, comm_conv=plan_conv, comm_dx=plan_dx)
    if ex is not None:
        ex.keep(plan_conv, got_conv)
        ex.keep(plan_dx, got_dx)
    dr1, dg0, db0 = _ln_bwd("s5_norm_bwd", xf, mix, dh1, ln_g[0])
    grads["s5_w_out"] = _mm_tn("s5_out_dw", gg, dr1, 1, BF16, g_mode="shared")
    dgg = _mm_nt_red("s5_out_dx", dr1, wfull["s5_w_out"][None], F32, g_blocked=False)
    dz, d_bglu, dy2_direct = _glu_bwd_gate(y, z, dgg)
    grads["s5_w_glu"] = _mm_tn("s5_glu_dw", y2, dz, 1, BF16, g_mode="shared")
    dy2 = _mm_nt_red("s5_glu_dx", dz, wfull["s5_w_glu"][None], F32, g_blocked=False, add=dy2_direct)
    dy = _gelu_bwd(y, dy2)
    plan_last = ex.scatter(grads, ("ffn_w_up0", "ffn_w_down0", "s5_w_out", "s5_w_glu")) if ex else None
    res = _s5_scan_bwd(xf, dy, dr1, h_r, h_i, bbr_c, bbi_c, cr_c, ci_c, dvec, tab, seq, comm=plan_last)
    if ex is not None:
        res, got = res
        ex.keep(plan_last, got)
    dx, dcr, dci, dbr, dbi, dar, dai, dd = res

    def bb_from_chunks(dense):
        return _diag_blocks(dense).transpose(2, 0, 1, 3).reshape(SSM_GROUP, N_STATE)

    d_lr, d_li, d_ldt, d_br, d_bi = _s5_prep_bwd(lr, li, ldt, br_t, bi_t, dar, dai,
                                                 bb_from_chunks(dbr), bb_from_chunks(dbi))
    grads["s5_lam_re"] = d_lr.reshape(p["s5_lam_re"].shape)
    grads["s5_lam_im"] = d_li.reshape(p["s5_lam_im"].shape)
    grads["s5_log_dt"] = d_ldt.reshape(p["s5_log_dt"].shape)
    grads["s5_b_re"] = d_br.T.reshape(p["s5_b_re"].shape)
    grads["s5_b_im"] = d_bi.T.reshape(p["s5_b_im"].shape)
    grads["s5_c_re"] = _diag_blocks(dcr).reshape(p["s5_c_re"].shape)
    grads["s5_c_im"] = -_diag_blocks(dci).reshape(p["s5_c_im"].shape)
    grads["s5_d"] = dd.reshape(p["s5_d"].shape)
    grads["s5_b_glu"] = d_bglu.reshape(1, D_MODEL)
    dconv = jnp.stack([dconv0, dconv1]).reshape(2, N_DEV, 8, -1)
    grads["ffn_conv_w"] = dconv[:, :, 0:3, :].transpose(0, 2, 1, 3)
    grads["ffn_conv_b"] = dconv[:, :, 3, :]
    grads["ln_gain"] = jnp.stack([dg0, dg1, dg2, dg3]).reshape(2, 2, D_MODEL)
    grads["ln_bias"] = jnp.stack([db0, db1, db2, db3]).reshape(2, 2, D_MODEL)
    return loss_tile[0, 0], dx.reshape(x.shape), grads


SMALL_REPLICATED = ("s5_lam_re", "s5_lam_im", "s5_log_dt", "s5_b_re", "s5_b_im", "s5_c_re", "s5_c_im", "s5_d",
                    "rel_bias", "ffn_conv_b")
SMALL_SHARDED = ("s5_b_glu", "ffn_conv_w", "ln_gain", "ln_bias")
BIG = ("s5_w_glu", "s5_w_out", "attn_w_kv", "attn_w_q", "attn_w_out", "ffn_w_up", "ffn_w_down")
WEIGHTS = ("s5_lam_re", "s5_lam_im", "s5_log_dt", "s5_b_re", "s5_b_im", "s5_c_re", "s5_c_im", "s5_d", "s5_w_glu",
           "s5_b_glu", "s5_w_out", "attn_w_kv", "attn_w_q", "attn_w_out", "rel_bias", "ffn_w_up", "ffn_conv_w",
           "ffn_conv_b", "ffn_w_down", "ln_gain", "ln_bias")


class _Exchanges:
    def __init__(self, w):
        self.w = w
        self.parts = {}
        self.up = w["ffn_w_up"].astype(BF16)
        self.down = w["ffn_w_down"].astype(BF16)

    def gather_first(self):
        w = self.w
        srcs = [w["s5_w_glu"][0].astype(BF16), w["s5_w_out"][0].astype(BF16),
                _pack([w[k] for k in SMALL_SHARDED]), self.up, self.down, w["attn_w_kv"].astype(BF16),
                w["attn_w_q"][0].astype(BF16), w["attn_w_out"][0].astype(BF16)]
        outs = [_sds((N_DEV,) + (s.shape[1:] if i in (3, 4) else s.shape), s.dtype) for i, s in enumerate(srcs)]
        items = [(i, (0,) if i in (3, 4) else (), i, ("slot",)) for i in range(len(srcs))]
        return _gather_plan(srcs, outs, items)

    def take_first(self, g, p):
        d = D_MODEL
        w = self.w
        wfull = {
            "s5_w_glu": g[0].reshape(d, d),
            "s5_w_out": g[1].reshape(d, d),
            "ffn_w_up": [g[3], None],
            "ffn_w_down": [g[4].reshape(4, -1, d), None],
            "attn_w_kv": g[5],
            "attn_w_q": g[6],
            "attn_w_out": g[7].reshape(d, d),
        }
        smalls = [_unpack(g[2][dev], [w[k].shape for k in SMALL_SHARDED]) for dev in range(N_DEV)]
        c_ff = w["ffn_conv_w"].shape[2]
        conv_w = jnp.stack([s[1] for s in smalls], axis=2)
        p = dict(p)
        p.update(s5_b_glu=jnp.concatenate([s[0] for s in smalls], axis=1),
                 ln_gain=jnp.concatenate([s[2] for s in smalls], axis=2),
                 ln_bias=jnp.concatenate([s[3] for s in smalls], axis=2),
                 conv_w=conv_w.transpose(0, 2, 1, 3).reshape(2, 2, 4, 3, c_ff),
                 conv_b=w["ffn_conv_b"].reshape(2, 2, 4, 1, c_ff))
        return p, wfull

    def gather_second(self):
        srcs = [self.up, self.down]
        outs = [_sds((N_DEV,) + s.shape[1:], BF16) for s in srcs]
        return _gather_plan(srcs, outs, [(0, (1,), 0, ("slot",)), (1, (1,), 1, ("slot",))])

    def take_second(self, g, wfull):
        wfull = dict(wfull)
        wfull["ffn_w_up"] = [wfull["ffn_w_up"][0], g[0]]
        wfull["ffn_w_down"] = [wfull["ffn_w_down"][0], g[1].reshape(4, -1, D_MODEL)]
        return wfull

    def scatter(self, grads, names):
        srcs = [grads[k].reshape(N_DEV, -1, grads[k].shape[-1]) for k in names]
        plan = _scatter_plan(srcs, [_sds(s.shape, BF16) for s in srcs],
                             [(i, i, ("slot",)) for i in range(len(names))])
        plan.names = names
        return plan

    def keep(self, plan, outs):
        for k, o in zip(plan.names, outs):
            self.parts[k] = o


def kernel(x, s5_lam_re, s5_lam_im, s5_log_dt, s5_b_re, s5_b_im, s5_c_re, s5_c_im, s5_d, s5_w_glu, s5_b_glu, s5_w_out, attn_w_kv, attn_w_q, attn_w_out, rel_bias, ffn_w_up, ffn_conv_w, ffn_conv_b, ffn_w_down, ln_gain, ln_bias, loss_target, m_s5_lam_re, m_s5_lam_im, m_s5_log_dt, m_s5_b_re, m_s5_b_im, m_s5_c_re, m_s5_c_im, m_s5_d, m_s5_w_glu, m_s5_b_glu, m_s5_w_out, m_attn_w_kv, m_attn_w_q, m_attn_w_out, m_rel_bias, m_ffn_w_up, m_ffn_conv_w, m_ffn_conv_b, m_ffn_w_down, m_ln_gain, m_ln_bias, v_s5_lam_re, v_s5_lam_im, v_s5_log_dt, v_s5_b_re, v_s5_b_im, v_s5_c_re, v_s5_c_im, v_s5_d, v_s5_w_glu, v_s5_b_glu, v_s5_w_out, v_attn_w_kv, v_attn_w_q, v_attn_w_out, v_rel_bias, v_ffn_w_up, v_ffn_conv_w, v_ffn_conv_b, v_ffn_w_down, v_ln_gain, v_ln_bias):
    w = dict(s5_lam_re=s5_lam_re, s5_lam_im=s5_lam_im, s5_log_dt=s5_log_dt, s5_b_re=s5_b_re, s5_b_im=s5_b_im,
             s5_c_re=s5_c_re, s5_c_im=s5_c_im, s5_d=s5_d, s5_w_glu=s5_w_glu, s5_b_glu=s5_b_glu, s5_w_out=s5_w_out,
             attn_w_kv=attn_w_kv, attn_w_q=attn_w_q, attn_w_out=attn_w_out, rel_bias=rel_bias, ffn_w_up=ffn_w_up,
             ffn_conv_w=ffn_conv_w, ffn_conv_b=ffn_conv_b, ffn_w_down=ffn_w_down, ln_gain=ln_gain, ln_bias=ln_bias)
    mom = dict(s5_lam_re=m_s5_lam_re, s5_lam_im=m_s5_lam_im, s5_log_dt=m_s5_log_dt, s5_b_re=m_s5_b_re,
               s5_b_im=m_s5_b_im, s5_c_re=m_s5_c_re, s5_c_im=m_s5_c_im, s5_d=m_s5_d, s5_w_glu=m_s5_w_glu,
               s5_b_glu=m_s5_b_glu, s5_w_out=m_s5_w_out, attn_w_kv=m_attn_w_kv, attn_w_q=m_attn_w_q,
               attn_w_out=m_attn_w_out, rel_bias=m_rel_bias, ffn_w_up=m_ffn_w_up, ffn_conv_w=m_ffn_conv_w,
               ffn_conv_b=m_ffn_conv_b, ffn_w_down=m_ffn_w_down, ln_gain=m_ln_gain, ln_bias=m_ln_bias)
    var = dict(s5_lam_re=v_s5_lam_re, s5_lam_im=v_s5_lam_im, s5_log_dt=v_s5_log_dt, s5_b_re=v_s5_b_re,
               s5_b_im=v_s5_b_im, s5_c_re=v_s5_c_re, s5_c_im=v_s5_c_im, s5_d=v_s5_d, s5_w_glu=v_s5_w_glu,
               s5_b_glu=v_s5_b_glu, s5_w_out=v_s5_w_out, attn_w_kv=v_attn_w_kv, attn_w_q=v_attn_w_q,
               attn_w_out=v_attn_w_out, rel_bias=v_rel_bias, ffn_w_up=v_ffn_w_up, ffn_conv_w=v_ffn_conv_w,
               ffn_conv_b=v_ffn_conv_b, ffn_w_down=v_ffn_w_down, ln_gain=v_ln_gain, ln_bias=v_ln_bias)
    bsz, seq, _ = x.shape
    me = 4 * lax.axis_index("x") + 2 * lax.axis_index("y") + lax.axis_index("c")

    ex = _Exchanges(w)
    loss_part, grad_x, g = _local_step(x, loss_target, dict(w), None, bsz, seq, ex=ex)
    loss = lax.psum(loss_part, ("x", "y", "c"))

    d = D_MODEL
    results = {}
    for name in BIG:
        shard = w[name]
        if name in ("ffn_w_up", "ffn_w_down"):
            per_layer = [_adamw("adamw_%s%d" % (name, l), ex.parts[name + str(l)], shard[l], mom[name][l],
                                var[name][l]) for l in range(2)]
            results[name] = tuple(jnp.stack([per_layer[0][j], per_layer[1][j]]) for j in range(4))
        else:
            r2 = (-1, shard.shape[-1])
            outs4 = _adamw("adamw_" + name, ex.parts[name], shard.reshape(r2), mom[name].reshape(r2),
                           var[name].reshape(r2))
            results[name] = tuple(o.reshape(shard.shape) for o in outs4)

    small_names = SMALL_REPLICATED + SMALL_SHARDED
    packed = _pack([g[k] for k in small_names])
    slots = _exchange("allreduce_small", [packed], [_sds((N_DEV,) + packed.shape, F32)],
                      [(0, (), 0, ("me",))])[0]
    total = _unpack(_sum_slots(slots), [g[k].shape for k in small_names])
    gsum = dict(zip(small_names, total))
    mine = {k: gsum[k] for k in SMALL_REPLICATED}
    mine["s5_b_glu"] = lax.dynamic_slice_in_dim(gsum["s5_b_glu"], me * (d // N_DEV), d // N_DEV, axis=1)
    mine["ffn_conv_w"] = lax.dynamic_index_in_dim(gsum["ffn_conv_w"], me, axis=2, keepdims=False)
    mine["ln_gain"] = lax.dynamic_slice_in_dim(gsum["ln_gain"], me * (d // N_DEV), d // N_DEV, axis=2)
    mine["ln_bias"] = lax.dynamic_slice_in_dim(gsum["ln_bias"], me * (d // N_DEV), d // N_DEV, axis=2)
    gp = _pack([mine[k] for k in small_names])
    outs4 = _adamw("adamw_small", gp[None], _pack([w[k] for k in small_names]),
                   _pack([mom[k] for k in small_names]), _pack([var[k] for k in small_names]))
    shapes = [w[k].shape for k in small_names]
    unpacked = [_unpack(o, shapes) for o in outs4]
    for i, k in enumerate(small_names):
        results[k] = tuple(unpacked[j][i] for j in range(4))

    out = [loss, grad_x]
    for j in range(4):
        out.extend(results[k][j] for k in WEIGHTS)
    return tuple(out)
```

```python
import functools
import math

import numpy as np
import jax
import jax.numpy as jnp
from jax import lax
from jax.experimental import pallas as pl
from jax.experimental.pallas import tpu as pltpu

F32 = jnp.float32
BF16 = jnp.bfloat16
HIGHEST = lax.Precision.HIGHEST

N_DEV = 8
D_MODEL = 1024
SSM_GROUP = 16
SSM_STATE = 64
SSM_CHUNKS = 8
CHUNK_CH = D_MODEL // SSM_CHUNKS
CHUNK_ST = CHUNK_CH // SSM_GROUP * SSM_STATE
N_STATE = D_MODEL // SSM_GROUP * SSM_STATE
HEAD_DIM = 64
N_HEADS = 16
BAND = 128
DILATIONS = (1, 4, 16)
REL_BUCKETS = 32
REL_MAX_DIST = 2048
NEG_BIG = -1e30
DN_ALPHA = (2.0 * 2) ** 0.25
LN_EPS = 1e-5
ADAM_LR = 0.001
ADAM_B1 = 0.9
ADAM_B2 = 0.999
ADAM_EPS = 1e-08
ADAM_WD = 0.01
ADAM_STEP = 10

TM = 512
T_SCAN = 256
VMEM_LIMIT = 56 * 1024 * 1024
FFN_HIDDEN = BF16

NN = (((1,), (0,)), ((), ()))
NT = (((1,), (1,)), ((), ()))
TN = (((0,), (0,)), ((), ()))


class _Plan:
    def __init__(self, srcs, out_shapes, n_sems, n_local, phases):
        self.srcs, self.out_shapes, self.n_sems, self.n_local, self.phases = srcs, out_shapes, n_sems, n_local, phases


def _call(body, name, grid, in_specs, out_specs, out_shape, scratch=(), comm=None):
    params = dict(dimension_semantics=("arbitrary",) * len(grid), vmem_limit_bytes=VMEM_LIMIT)
    if comm is None:
        return pl.pallas_call(
            body, name=name, grid=grid, in_specs=in_specs, out_specs=out_specs, out_shape=out_shape,
            scratch_shapes=list(scratch), compiler_params=pltpu.CompilerParams(**params))
    single = not isinstance(out_specs, (list, tuple))
    o_specs = [out_specs] if single else list(out_specs)
    o_shape = [out_shape] if single else list(out_shape)
    n_in, n_out, n_scr = len(in_specs), len(o_specs), len(scratch)
    n_cin, n_cout = len(comm.srcs), len(comm.out_shapes)
    total = int(np.prod(grid))

    def wrapped(*refs):
        ins = refs[:n_in]
        cins = refs[n_in:n_in + n_cin]
        outs = refs[n_in + n_cin:n_in + n_cin + n_out]
        couts = refs[n_in + n_cin + n_out:n_in + n_cin + n_out + n_cout]
        rest = refs[n_in + n_cin + n_out + n_cout:]
        scr, sems = rest[:n_scr], rest[n_scr:]
        step = pl.program_id(0)
        for ax in range(1, len(grid)):
            step = step * grid[ax] + pl.program_id(ax)
        phases = comm.phases(cins, couts, *sems, total)
        for at, action in phases:
            if at == 0:
                pl.when(step == 0)(action)
        body(*ins, *outs, *scr)
        for at, action in phases:
            if at > 0:
                pl.when(step == at)(action)

    any_spec = pl.BlockSpec(memory_space=pl.ANY)
    call = pl.pallas_call(
        wrapped, name=name, grid=grid, in_specs=list(in_specs) + [any_spec] * n_cin,
        out_specs=o_specs + [any_spec] * n_cout, out_shape=o_shape + list(comm.out_shapes),
        scratch_shapes=list(scratch) + [pltpu.SemaphoreType.DMA((comm.n_sems,)),
                                        pltpu.SemaphoreType.DMA((comm.n_sems,)),
                                        pltpu.SemaphoreType.DMA((max(comm.n_local, 1),))],
        compiler_params=pltpu.CompilerParams(has_side_effects=True, **params))

    def run(*args):
        res = call(*args, *comm.srcs)
        own = res[0] if single else res[:n_out]
        return own, res[n_out:]

    return run


def _sds(shape, dtype):
    return jax.ShapeDtypeStruct(shape, dtype)


def _mm(name, a, b, *, dims, grid, a_spec, b_spec, o_spec, out_shape, nred=1, red_axis=0,
        add=None, add_spec=None, add_scale=1.0, comm=None):
    has_add = add is not None
    acc_shape = tuple(s for s in o_spec.block_shape if s is not None)

    def body(*refs):
        if has_add:
            a_ref, b_ref, add_ref, o_ref = refs[:4]
            rest = refs[4:]
        else:
            a_ref, b_ref, o_ref = refs[:3]
            rest = refs[3:]
        prod = lax.dot_general(a_ref[...].astype(BF16), b_ref[...].astype(BF16), dims,
                               preferred_element_type=F32)

        def finish(val):
            if has_add:
                val = val + add_scale * add_ref[...].astype(F32)
            o_ref[...] = val.astype(o_ref.dtype)

        if nred == 1:
            finish(prod)
        else:
            acc = rest[0]
            k = pl.program_id(red_axis)

            @pl.when(k == 0)
            def _():
                acc[...] = prod

            @pl.when(k > 0)
            def _():
                acc[...] += prod

            @pl.when(k == nred - 1)
            def _():
                finish(acc[...])

    in_specs = [a_spec, b_spec] + ([add_spec] if has_add else [])
    scratch = [pltpu.VMEM(acc_shape, F32)] if nred > 1 else []
    args = (a, b) + ((add,) if has_add else ())
    return _call(body, name, grid, in_specs, o_spec, out_shape, scratch, comm=comm)(*args)


def _mm_nn(name, a, b, out_dtype, blocked_out=False, comm=None):
    m, k = a.shape
    nb, _, n = b.shape
    if blocked_out:
        o_spec = pl.BlockSpec((None, TM, n), lambda d, i: (d, i, 0))
        out_shape = _sds((nb, m, n), out_dtype)
    else:
        o_spec = pl.BlockSpec((TM, n), lambda d, i: (i, d))
        out_shape = _sds((m, nb * n), out_dtype)
    return _mm(name, a, b, dims=NN, grid=(nb, m // TM),
               a_spec=pl.BlockSpec((TM, k), lambda d, i: (i, 0)),
               b_spec=pl.BlockSpec((None, k, n), lambda d, i: (d, 0, 0)),
               o_spec=o_spec, out_shape=out_shape, comm=comm)


def _mm_nn_red(name, a, b, out_dtype):
    nb, m, k = a.shape
    n = b.shape[2]
    return _mm(name, a, b, dims=NN, grid=(m // TM, nb), nred=nb, red_axis=1,
               a_spec=pl.BlockSpec((None, TM, k), lambda i, d: (d, i, 0)),
               b_spec=pl.BlockSpec((None, k, n), lambda i, d: (d, 0, 0)),
               o_spec=pl.BlockSpec((TM, n), lambda i, d: (i, 0)),
               out_shape=_sds((m, n), out_dtype))


def _mm_tn(name, a, g, nb, out_dtype, a_blocked=False, g_mode="cols"):
    if a_blocked:
        _, m, ka = a.shape
        a_spec = pl.BlockSpec((None, TM, ka), lambda d, i: (d, i, 0))
    else:
        m, ka = a.shape
        a_spec = pl.BlockSpec((TM, ka), lambda d, i: (i, 0))
    if g_mode == "cols":
        n = g.shape[1] // nb
        g_spec = pl.BlockSpec((TM, n), lambda d, i: (i, d))
    elif g_mode == "blocked":
        n = g.shape[2]
        g_spec = pl.BlockSpec((None, TM, n), lambda d, i: (d, i, 0))
    else:
        n = g.shape[1]
        g_spec = pl.BlockSpec((TM, n), lambda d, i: (i, 0))
    nred = m // TM
    return _mm(name, a, g, dims=TN, grid=(nb, nred), nred=nred, red_axis=1,
               a_spec=a_spec, b_spec=g_spec,
               o_spec=pl.BlockSpec((None, ka, n), lambda d, i: (d, 0, 0)),
               out_shape=_sds((nb, ka, n), out_dtype))


def _mm_nt_red(name, g, w, out_dtype, g_blocked, add=None, add_scale=1.0, comm=None):
    nb, k, n = w.shape
    if g_blocked:
        m = g.shape[1]
        g_spec = pl.BlockSpec((None, TM, n), lambda i, d: (d, i, 0))
    else:
        m = g.shape[0]
        g_spec = pl.BlockSpec((TM, n), lambda i, d: (i, d))
    o_spec = pl.BlockSpec((TM, k), lambda i, d: (i, 0))
    return _mm(name, g, w, dims=NT, grid=(m // TM, nb), nred=nb, red_axis=1,
               a_spec=g_spec, b_spec=pl.BlockSpec((None, k, n), lambda i, d: (d, 0, 0)),
               o_spec=o_spec, out_shape=_sds((m, k), out_dtype),
               add=add, add_spec=o_spec if add is not None else None, add_scale=add_scale, comm=comm)


def _mm_nt_out(name, g, w, out_dtype):
    m, n = g.shape
    nb, k, _ = w.shape
    return _mm(name, g, w, dims=NT, grid=(nb, m // TM),
               a_spec=pl.BlockSpec((TM, n), lambda d, i: (i, 0)),
               b_spec=pl.BlockSpec((None, k, n), lambda d, i: (d, 0, 0)),
               o_spec=pl.BlockSpec((None, TM, k), lambda d, i: (d, i, 0)),
               out_shape=_sds((nb, m, k), out_dtype))


def _ln_stats(x, f):
    r = DN_ALPHA * x + f
    mu = jnp.mean(r, axis=-1, keepdims=True)
    xc = r - mu
    var = jnp.mean(xc * xc, axis=-1, keepdims=True)
    rstd = lax.rsqrt(var + LN_EPS)
    return xc * rstd, rstd


def _ln_fwd(name, xin, f, gain, bias):
    n, d = xin.shape

    def body(x_ref, f_ref, g_ref, b_ref, o_ref, ob_ref):
        xhat, _ = _ln_stats(x_ref[...], f_ref[...])
        out = xhat * g_ref[...] + b_ref[...]
        o_ref[...] = out
        ob_ref[...] = out.astype(BF16)

    row = pl.BlockSpec((TM, d), lambda i: (i, 0))
    vec = pl.BlockSpec((1, d), lambda i: (0, 0))
    return _call(body, name, (n // TM,), [row, row, vec, vec], [row, row],
                 [_sds((n, d), F32), _sds((n, d), BF16)])(xin, f, gain, bias)


def _ln_bwd(name, xin, f, dy, gain):
    n, d = xin.shape

    def body(x_ref, f_ref, dy_ref, g_ref, dr_ref, drb_ref, dg_ref, db_ref):
        i = pl.program_id(0)
        xhat, rstd = _ln_stats(x_ref[...], f_ref[...])
        dy_v = dy_ref[...]
        dxh = dy_v * g_ref[...]
        m1 = jnp.mean(dxh, axis=-1, keepdims=True)
        m2 = jnp.mean(dxh * xhat, axis=-1, keepdims=True)
        dr = rstd * (dxh - m1 - xhat * m2)
        dr_ref[...] = dr
        drb_ref[...] = dr.astype(BF16)
        dg = jnp.sum(dy_v * xhat, axis=0, keepdims=True)
        db = jnp.sum(dy_v, axis=0, keepdims=True)

        @pl.when(i == 0)
        def _():
            dg_ref[...] = dg
            db_ref[...] = db

        @pl.when(i > 0)
        def _():
            dg_ref[...] += dg
            db_ref[...] += db

    row = pl.BlockSpec((TM, d), lambda i: (i, 0))
    vec = pl.BlockSpec((1, d), lambda i: (0, 0))
    return _call(body, name, (n // TM,), [row, row, row, vec], [row, row, vec, vec],
                 [_sds((n, d), F32), _sds((n, d), BF16), _sds((1, d), F32), _sds((1, d), F32)])(xin, f, dy, gain)


def _loss_head(y, target):
    n, d = y.shape

    def body(y_ref, t_ref, l_ref, dy_ref):
        i = pl.program_id(0)
        e = y_ref[...] - t_ref[...]
        dy_ref[...] = e * (1.0 / d)
        part = jnp.sum(jnp.sum(e * e, axis=1, keepdims=True), axis=0, keepdims=True) * (0.5 / d)
        part = jnp.broadcast_to(part, (8, 128))

        @pl.when(i == 0)
        def _():
            l_ref[...] = part

        @pl.when(i > 0)
        def _():
            l_ref[...] += part

    row = pl.BlockSpec((TM, d), lambda i: (i, 0))
    return _call(body, "loss_head", (n // TM,), [row, row],
                 [pl.BlockSpec((8, 128), lambda i: (0, 0)), row],
                 [_sds((8, 128), F32), _sds((n, d), F32)])(y, target)


def _group_expand_matrix():
    e = np.zeros((D_MODEL // SSM_GROUP, N_STATE), np.float32)
    for g in range(D_MODEL // SSM_GROUP):
        e[g, g * SSM_STATE:(g + 1) * SSM_STATE] = 1.0
    return jnp.asarray(e)


def _discretise(lr, li, ldt, br, bi, expand):
    dt = jnp.dot(jnp.exp(ldt), expand, precision=HIGHEST, preferred_element_type=F32)
    mag = jnp.exp(lr * dt)
    th = li * dt
    ab_r = mag * jnp.cos(th)
    ab_i = mag * jnp.sin(th)
    den = lr * lr + li * li
    nr = ab_r - 1.0
    co_r = (nr * lr + ab_i * li) / den
    co_i = (ab_i * lr - nr * li) / den
    bb_r = co_r * br - co_i * bi
    bb_i = co_r * bi + co_i * br
    return ab_r, ab_i, bb_r, bb_i


def _cmul(ar, ai, br, bi):
    return ar * br - ai * bi, ar * bi + ai * br


def _s5_prep(lr, li, ldt, br, bi):
    s = N_STATE
    expand = _group_expand_matrix()

    def body(lr_ref, li_ref, ldt_ref, br_ref, bi_ref, e_ref, abr_ref, abi_ref, bbr_ref, bbi_ref, tab_ref):
        ab_r, ab_i, bb_r, bb_i = _discretise(lr_ref[...], li_ref[...], ldt_ref[...], br_ref[...],
                                             bi_ref[...], e_ref[...])
        abr_ref[...] = ab_r
        abi_ref[...] = ab_i
        bbr_ref[...] = bb_r
        bbi_ref[...] = bb_i
        row = lax.broadcasted_iota(jnp.int32, (8, s), 0)
        for base, sign in ((0, 1.0), (8, -1.0)):
            p = [None] * 9
            p[1] = (ab_r, sign * ab_i)
            p[2] = _cmul(*p[1], *p[1])
            p[3] = _cmul(*p[2], *p[1])
            p[4] = _cmul(*p[2], *p[2])
            p[5] = _cmul(*p[4], *p[1])
            p[6] = _cmul(*p[4], *p[2])
            p[7] = _cmul(*p[4], *p[3])
            p[8] = _cmul(*p[4], *p[4])
            for j, k in enumerate((1, 2, 4)):
                keep = (row >= k) if base == 0 else (row < 8 - k)
                tab_ref[base + 2 * j] = jnp.where(keep, jnp.broadcast_to(p[k][0], (8, s)), 0.0)
                tab_ref[base + 2 * j + 1] = jnp.where(keep, jnp.broadcast_to(p[k][1], (8, s)), 0.0)
            pw_r = jnp.zeros((8, s), F32)
            pw_i = jnp.zeros((8, s), F32)
            for i in range(8):
                e = i + 1 if base == 0 else 8 - i
                pw_r = jnp.where(row == i, jnp.broadcast_to(p[e][0], (8, s)), pw_r)
                pw_i = jnp.where(row == i, jnp.broadcast_to(p[e][1], (8, s)), pw_i)
            tab_ref[base + 6] = pw_r
            tab_ref[base + 7] = pw_i

    def full(shape):
        return pl.BlockSpec(shape, lambda i: (0,) * len(shape))

    g = D_MODEL // SSM_GROUP
    return _call(body, "s5_prep", (1,),
                 [full((1, s)), full((1, s)), full((1, g)), full((16, s)), full((16, s)), full((g, s))],
                 [full((1, s)), full((1, s)), full((16, s)), full((16, s)), full((16, 8, s))],
                 [_sds((1, s), F32), _sds((1, s), F32), _sds((16, s), F32), _sds((16, s), F32),
                  _sds((16, 8, s), F32)])(lr, li, ldt, br, bi, expand)


def _s5_prep_bwd(lr, li, ldt, br, bi, d_abr, d_abi, d_bbr, d_bbi):
    s = N_STATE
    g = D_MODEL // SSM_GROUP
    expand = _group_expand_matrix()

    def body(lr_ref, li_ref, ldt_ref, br_ref, bi_ref, e_ref, c1, c2, c3, c4, o1, o2, o3, o4, o5):
        e_val = e_ref[...]
        _, vjp = jax.vjp(lambda a, b, c, d, e: _discretise(a, b, c, d, e, e_val),
                         lr_ref[...], li_ref[...], ldt_ref[...], br_ref[...], bi_ref[...])
        grads = vjp((c1[...], c2[...], c3[...], c4[...]))
        for o, gr in zip((o1, o2, o3, o4, o5), grads):
            o[...] = gr

    def full(shape):
        return pl.BlockSpec(shape, lambda i: (0,) * len(shape))

    return _call(body, "s5_prep_bwd", (1,),
                 [full((1, s)), full((1, s)), full((1, g)), full((16, s)), full((16, s)), full((g, s)),
                  full((1, s)), full((1, s)), full((16, s)), full((16, s))],
                 [full((1, s)), full((1, s)), full((1, g)), full((16, s)), full((16, s))],
                 [_sds((1, s), F32), _sds((1, s), F32), _sds((1, g), F32), _sds((16, s), F32),
                  _sds((16, s), F32)])(lr, li, ldt, br, bi, expand, d_abr, d_abi, d_bbr, d_bbi)


def _scan8(vr, vi, tab_ref, base, reverse):
    for j, k in enumerate((1, 2, 4)):
        mr = tab_ref[base + 2 * j]
        mi = tab_ref[base + 2 * j + 1]
        shift = 8 - k if reverse else k
        sr = pltpu.roll(vr, shift, 0)
        si = pltpu.roll(vi, shift, 0)
        vr, vi = vr + mr * sr - mi * si, vi + mr * si + mi * sr
    return vr, vi


def _s5_scan_fwd(x, bbr, bbi, cr, ci, dvec, tab, seq, comm=None):
    n, d = x.shape
    nt = seq // T_SCAN
    nblk = T_SCAN // 8
    st = CHUNK_ST

    def body(x_ref, bbr_ref, bbi_ref, cr_ref, ci_ref, d_ref, tab_ref, y_ref, hr_ref, hi_ref, car_r, car_i):
        t = pl.program_id(2)

        @pl.when(t == 0)
        def _():
            car_r[...] = jnp.zeros_like(car_r)
            car_i[...] = jnp.zeros_like(car_i)

        u = x_ref[...]
        ub = u.astype(BF16)
        hr_ref[...] = jnp.dot(ub, bbr_ref[...], preferred_element_type=F32)
        hi_ref[...] = jnp.dot(ub, bbi_ref[...], preferred_element_type=F32)

        def step(i, carry):
            sl = pl.ds(pl.multiple_of(i * 8, 8), 8)
            vr, vi = _scan8(hr_ref[sl, :], hi_ref[sl, :], tab_ref, 0, False)
            c_r = jnp.broadcast_to(car_r[...], (8, st))
            c_i = jnp.broadcast_to(car_i[...], (8, st))
            pr = tab_ref[6]
            pi = tab_ref[7]
            vr, vi = vr + pr * c_r - pi * c_i, vi + pr * c_i + pi * c_r
            hr_ref[sl, :] = vr
            hi_ref[sl, :] = vi
            car_r[...] = vr[7:8, :]
            car_i[...] = vi[7:8, :]
            return carry

        lax.fori_loop(0, nblk, step, 0, unroll=4)
        y = lax.dot_general(hr_ref[...].astype(BF16), cr_ref[...], NT, preferred_element_type=F32)
        y = y - lax.dot_general(hi_ref[...].astype(BF16), ci_ref[...], NT, preferred_element_type=F32)
        y_ref[...] = y + d_ref[...] * u

    row = lambda c, b, t: (b * nt + t, c)
    mat = pl.BlockSpec((None, CHUNK_CH, st), lambda c, b, t: (c, 0, 0))
    return _call(
        body, "s5_scan_fwd", (SSM_CHUNKS, n // seq, nt),
        [pl.BlockSpec((T_SCAN, CHUNK_CH), row), mat, mat, mat, mat,
         pl.BlockSpec((1, CHUNK_CH), lambda c, b, t: (0, c)),
         pl.BlockSpec((16, 8, st), lambda c, b, t: (0, 0, c))],
        [pl.BlockSpec((T_SCAN, CHUNK_CH), row), pl.BlockSpec((T_SCAN, st), row), pl.BlockSpec((T_SCAN, st), row)],
        [_sds((n, d), F32), _sds((n, N_STATE), F32), _sds((n, N_STATE), F32)],
        [pltpu.VMEM((1, st), F32), pltpu.VMEM((1, st), F32)], comm=comm,
    )(x, bbr, bbi, cr, ci, dvec, tab)


def _s5_scan_bwd(x, dy, dres, hr, hi, bbr, bbi, cr, ci, dvec, tab, seq, comm=None):
    n, d = x.shape
    nt = seq // T_SCAN
    nblk = T_SCAN // 8
    st = CHUNK_ST

    def body(x_ref, dy_ref, dres_ref, hr_ref, hi_ref, hpr_ref, hpi_ref, bbr_ref, bbi_ref, cr_ref, ci_ref,
             d_ref, tab_ref, dx_ref, dcr_ref, dci_ref, dbr_ref, dbi_ref, dar_ref, dai_ref, dd_ref,
             g_r, g_i, car_r, car_i):
        b = pl.program_id(1)
        tg = pl.program_id(2)
        first = jnp.logical_and(b == 0, tg == 0)

        @pl.when(tg == 0)
        def _():
            car_r[...] = jnp.zeros_like(car_r)
            car_i[...] = jnp.zeros_like(car_i)

        u = x_ref[...]
        dyv = dy_ref[...]
        ub = u.astype(BF16)
        dyb = dyv.astype(BF16)
        g_r[...] = jnp.dot(dyb, cr_ref[...], preferred_element_type=F32)
        g_i[...] = -jnp.dot(dyb, ci_ref[...], preferred_element_type=F32)

        def step(ii, carry):
            i = nblk - 1 - ii
            sl = pl.ds(pl.multiple_of(i * 8, 8), 8)
            vr, vi = _scan8(g_r[sl, :], g_i[sl, :], tab_ref, 8, True)
            c_r = jnp.broadcast_to(car_r[...], (8, st))
            c_i = jnp.broadcast_to(car_i[...], (8, st))
            pr = tab_ref[14]
            pi = tab_ref[15]
            vr, vi = vr + pr * c_r - pi * c_i, vi + pr * c_i + pi * c_r
            g_r[sl, :] = vr
            g_i[sl, :] = vi
            car_r[...] = vr[0:1, :]
            car_i[...] = vi[0:1, :]
            return carry

        lax.fori_loop(0, nblk, step, 0, unroll=4)
        gr = g_r[...]
        gi = g_i[...]
        grb = gr.astype(BF16)
        gib = gi.astype(BF16)
        du = lax.dot_general(grb, bbr_ref[...], NT, preferred_element_type=F32)
        du = du + lax.dot_general(gib, bbi_ref[...], NT, preferred_element_type=F32)
        dx_ref[...] = DN_ALPHA * dres_ref[...] + du + d_ref[...] * dyv

        h_r = hr_ref[...]
        h_i = hi_ref[...]
        t_orig = nt - 1 - tg
        row = lax.broadcasted_iota(jnp.int32, (T_SCAN, st), 0)
        halo_ok = t_orig > 0
        top_r = jnp.where(halo_ok, hpr_ref[7:8, :], 0.0)
        top_i = jnp.where(halo_ok, hpi_ref[7:8, :], 0.0)
        hp_r = jnp.where(row == 0, jnp.broadcast_to(top_r, (T_SCAN, st)), pltpu.roll(h_r, 1, 0))
        hp_i = jnp.where(row == 0, jnp.broadcast_to(top_i, (T_SCAN, st)), pltpu.roll(h_i, 1, 0))
        dar = jnp.sum(gr * hp_r + gi * hp_i, axis=0, keepdims=True)
        dai = jnp.sum(gi * hp_r - gr * hp_i, axis=0, keepdims=True)
        dcr = lax.dot_general(dyb, h_r.astype(BF16), TN, preferred_element_type=F32)
        dci = lax.dot_general(dyb, h_i.astype(BF16), TN, preferred_element_type=F32)
        dbr = lax.dot_general(ub, grb, TN, preferred_element_type=F32)
        dbi = lax.dot_general(ub, gib, TN, preferred_element_type=F32)
        dd = jnp.sum(dyv * u, axis=0, keepdims=True)

        @pl.when(first)
        def _():
            dcr_ref[...] = dcr
            dci_ref[...] = dci
            dbr_ref[...] = dbr
            dbi_ref[...] = dbi
            dar_ref[...] = dar
            dai_ref[...] = dai
            dd_ref[...] = dd

        @pl.when(jnp.logical_not(first))
        def _():
            dcr_ref[...] += dcr
            dci_ref[...] += dci
            dbr_ref[...] += dbr
            dbi_ref[...] += dbi
            dar_ref[...] += dar
            dai_ref[...] += dai
            dd_ref[...] += dd

    row = lambda c, b, t: (b * nt + (nt - 1 - t), c)
    halo = lambda c, b, t: (jnp.maximum((b * nt + (nt - 1 - t)) * (T_SCAN // 8) - 1, 0), c)
    mat = pl.BlockSpec((None, CHUNK_CH, st), lambda c, b, t: (c, 0, 0))
    chan = pl.BlockSpec((T_SCAN, CHUNK_CH), row)
    state = pl.BlockSpec((T_SCAN, st), row)
    return _call(
        body, "s5_scan_bwd", (SSM_CHUNKS, n // seq, nt),
        [chan, chan, chan, state, state, pl.BlockSpec((8, st), halo), pl.BlockSpec((8, st), halo),
         mat, mat, mat, mat, pl.BlockSpec((1, CHUNK_CH), lambda c, b, t: (0, c)),
         pl.BlockSpec((16, 8, st), lambda c, b, t: (0, 0, c))],
        [chan, mat, mat, mat, mat, pl.BlockSpec((1, st), lambda c, b, t: (0, c)),
         pl.BlockSpec((1, st), lambda c, b, t: (0, c)), pl.BlockSpec((1, CHUNK_CH), lambda c, b, t: (0, c))],
        [_sds((n, d), F32)] + [_sds((SSM_CHUNKS, CHUNK_CH, st), F32)] * 4
        + [_sds((1, N_STATE), F32), _sds((1, N_STATE), F32), _sds((1, d), F32)],
        [pltpu.VMEM((T_SCAN, st), F32), pltpu.VMEM((T_SCAN, st), F32), pltpu.VMEM((1, st), F32),
         pltpu.VMEM((1, st), F32)], comm=comm,
    )(x, dy, dres, hr, hi, hr, hi, bbr, bbi, cr, ci, dvec, tab)


_GELU_C = math.sqrt(2.0 / math.pi)


def _gelu_parts(y):
    t = jnp.tanh(_GELU_C * (y + 0.044715 * (y * y * y)))
    return 0.5 * (1.0 + t), t


def _glu_fwd(y, w_glu, b_glu):
    n, d = y.shape

    def body(y_ref, w_ref, b_ref, y2_ref, z_ref, gg_ref):
        yv = y_ref[...]
        cdf, _ = _gelu_parts(yv)
        y2 = yv * cdf
        z = jnp.dot(y2.astype(BF16), w_ref[...], preferred_element_type=F32) + b_ref[...]
        y2_ref[...] = y2.astype(BF16)
        z_ref[...] = z
        gg_ref[...] = (y2 * jax.nn.sigmoid(z)).astype(BF16)

    row = pl.BlockSpec((TM, d), lambda i: (i, 0))
    return _call(body, "glu_fwd", (n // TM,),
                 [row, pl.BlockSpec((d, d), lambda i: (0, 0)), pl.BlockSpec((1, d), lambda i: (0, 0))],
                 [row, row, row], [_sds((n, d), BF16), _sds((n, d), F32), _sds((n, d), BF16)])(y, w_glu, b_glu)


def _glu_bwd_gate(y, z, dgg):
    n, d = y.shape

    def body(y_ref, z_ref, dgg_ref, dz_ref, db_ref, t_ref):
        i = pl.program_id(0)
        yv = y_ref[...]
        cdf, _ = _gelu_parts(yv)
        y2 = yv * cdf
        s = jax.nn.sigmoid(z_ref[...])
        dg = dgg_ref[...]
        dz = dg * y2 * s * (1.0 - s)
        dz_ref[...] = dz.astype(BF16)
        t_ref[...] = dg * s
        db = jnp.sum(dz, axis=0, keepdims=True)

        @pl.when(i == 0)
        def _():
            db_ref[...] = db

        @pl.when(i > 0)
        def _():
            db_ref[...] += db

    row = pl.BlockSpec((TM, d), lambda i: (i, 0))
    vec = pl.BlockSpec((1, d), lambda i: (0, 0))
    return _call(body, "glu_bwd_gate", (n // TM,), [row, row, row], [row, vec, row],
                 [_sds((n, d), BF16), _sds((1, d), F32), _sds((n, d), F32)])(y, z, dgg)


def _gelu_bwd(y, dy2):
    n, d = y.shape

    def body(y_ref, g_ref, o_ref):
        yv = y_ref[...]
        cdf, t = _gelu_parts(yv)
        dinner = _GELU_C * (1.0 + 3.0 * 0.044715 * yv * yv)
        o_ref[...] = g_ref[...] * (cdf + 0.5 * yv * (1.0 - t * t) * dinner)

    row = pl.BlockSpec((TM, d), lambda i: (i, 0))
    return _call(body, "gelu_bwd", (n // TM,), [row, row], row, _sds((n, d), F32))(y, dy2)


HALO = 16


def _shift_down(x, halo, k):
    out = pltpu.roll(x, k, 0)
    row = lax.broadcasted_iota(jnp.int32, x.shape, 0)
    for j in range(k):
        out = jnp.where(row == j, jnp.broadcast_to(halo[HALO - k + j:HALO - k + j + 1, :], x.shape), out)
    return out


def _conv3(x, halo, w, b):
    return b + w[0:1, :] * x + w[1:2, :] * _shift_down(x, halo, 1) + w[2:3, :] * _shift_down(x, halo, 2)


def _conv_gate_fwd(hc, conv_w, conv_b, seq):
    _, nb, n, c = hc.shape
    tiles_per_seq = seq // TM

    def body(x_ref, halo_ref, w_ref, b_ref, o_ref):
        i = pl.program_id(1)
        start = (i % tiles_per_seq) == 0
        cv = []
        for h in range(2):
            halo = jnp.where(start, 0.0, halo_ref[h].astype(F32))
            cv.append(_conv3(x_ref[h].astype(F32), halo, w_ref[h], b_ref[h]))
        o_ref[...] = (cv[1] * jax.nn.sigmoid(cv[1]) * cv[0]).astype(BF16)

    return _call(
        body, "conv_gate_fwd", (nb, n // TM),
        [pl.BlockSpec((2, None, TM, c), lambda d, i: (0, d, i, 0)),
         pl.BlockSpec((2, None, HALO, c), lambda d, i: (0, d, jnp.maximum(i * (TM // HALO) - 1, 0), 0)),
         pl.BlockSpec((2, None, 3, c), lambda d, i: (0, d, 0, 0)),
         pl.BlockSpec((2, None, 1, c), lambda d, i: (0, d, 0, 0))],
        pl.BlockSpec((None, TM, c), lambda d, i: (d, i, 0)),
        _sds((nb, n, c), BF16))(hc, hc, conv_w, conv_b)


def _conv_gate_bwd(hc, dact, conv_w, conv_b, seq, comm=None):
    _, nb, n, c = hc.shape
    tiles_per_seq = seq // TM
    last_halo = n // HALO - 1
    ext = TM + HALO

    def body(x_ref, prev_ref, next_ref, da_ref, dan_ref, w_ref, b_ref, dx_ref, dw_ref):
        i = pl.program_id(1)
        start = (i % tiles_per_seq) == 0
        end = (i % tiles_per_seq) == tiles_per_seq - 1
        row = lax.broadcasted_iota(jnp.int32, (ext, c), 0)
        xe, cv = [], []
        for h in range(2):
            halo = jnp.where(start, 0.0, prev_ref[h].astype(F32))
            x_ext = jnp.concatenate([x_ref[h], next_ref[h]], axis=0).astype(F32)
            xe.append((x_ext, halo))
            cv.append(_conv3(x_ext, halo, w_ref[h], b_ref[h]))
        da = jnp.concatenate([da_ref[...], dan_ref[...]], axis=0).astype(F32)
        da = jnp.where(jnp.logical_and(end, row >= TM), 0.0, da)
        sg = jax.nn.sigmoid(cv[1])
        silu = cv[1] * sg
        dcv = [da * silu, da * cv[0] * (sg + silu * (1.0 - sg))]
        for h in range(2):
            w = w_ref[h]
            g = dcv[h]
            dx = w[0:1, :] * g + w[1:2, :] * pltpu.roll(g, ext - 1, 0) + w[2:3, :] * pltpu.roll(g, ext - 2, 0)
            dx_ref[h] = dx[:TM, :].astype(BF16)
            x_ext, halo = xe[h]
            gt = g[:TM, :]
            xt = x_ext[:TM, :]
            parts = [jnp.sum(gt * xt, axis=0, keepdims=True),
                     jnp.sum(gt * _shift_down(xt, halo, 1), axis=0, keepdims=True),
                     jnp.sum(gt * _shift_down(xt, halo, 2), axis=0, keepdims=True),
                     jnp.sum(gt, axis=0, keepdims=True)]
            upd = jnp.concatenate(parts + [jnp.zeros((4, c), F32)], axis=0)

            @pl.when(i == 0)
            def _():
                dw_ref[h] = upd

            @pl.when(i > 0)
            def _():
                dw_ref[h] += upd

    nxt = lambda i: jnp.minimum((i + 1) * (TM // HALO), last_halo)
    return _call(
        body, "conv_gate_bwd", (nb, n // TM),
        [pl.BlockSpec((2, None, TM, c), lambda d, i: (0, d, i, 0)),
         pl.BlockSpec((2, None, HALO, c), lambda d, i: (0, d, jnp.maximum(i * (TM // HALO) - 1, 0), 0)),
         pl.BlockSpec((2, None, HALO, c), lambda d, i: (0, d, nxt(i), 0)),
         pl.BlockSpec((None, TM, c), lambda d, i: (d, i, 0)),
         pl.BlockSpec((None, HALO, c), lambda d, i: (d, nxt(i), 0)),
         pl.BlockSpec((2, None, 3, c), lambda d, i: (0, d, 0, 0)),
         pl.BlockSpec((2, None, 1, c), lambda d, i: (0, d, 0, 0))],
        [pl.BlockSpec((2, None, TM, c), lambda d, i: (0, d, i, 0)),
         pl.BlockSpec((2, None, 8, c), lambda d, i: (0, d, 0, 0))],
        [_sds((2, nb, n, c), BF16), _sds((2, nb, 8, c), F32)], comm=comm)(hc, hc, hc, dact, dact, conv_w, conv_b)


def _t5_bucket_np(dist):
    exact = REL_BUCKETS // 2
    d = np.maximum(dist, 1).astype(np.float32)
    large = exact + (np.log(d / exact) / math.log(REL_MAX_DIST / exact) * (REL_BUCKETS - exact)).astype(np.int64)
    large = np.minimum(large, REL_BUCKETS - 1)
    return np.where(dist < exact, dist, large).astype(np.int32)


def _bucket_table(dil):
    steps = np.arange(BAND)[:, None] + BAND - np.arange(2 * BAND)[None, :]
    return _t5_bucket_np(np.maximum(steps, 0) * dil).reshape(1, BAND * 2 * BAND).astype(np.float32)


def _bias_expand(rel_t, dil):
    nk = BAND * 2 * BAND
    bucket = jnp.asarray(_bucket_table(dil))

    def body(r_ref, bk_ref, o_ref):
        ids = lax.broadcasted_iota(jnp.int32, (REL_BUCKETS, nk), 0).astype(F32)
        onehot = (ids == bk_ref[...]).astype(F32)
        o_ref[...] = jnp.dot(r_ref[...], onehot, precision=HIGHEST, preferred_element_type=F32)

    return _call(body, "bias_expand", (1,),
                 [pl.BlockSpec((N_HEADS, REL_BUCKETS), lambda i: (0, 0)), pl.BlockSpec((1, nk), lambda i: (0, 0))],
                 pl.BlockSpec((N_HEADS, nk), lambda i: (0, 0)), _sds((N_HEADS, nk), F32))(rel_t, bucket)


def _bias_reduce(dbias, dil):
    nk = BAND * 2 * BAND
    bucket = jnp.asarray(_bucket_table(dil))

    def body(g_ref, bk_ref, o_ref):
        ids = lax.broadcasted_iota(jnp.int32, (REL_BUCKETS, nk), 0).astype(F32)
        onehot = (ids == bk_ref[...]).astype(F32)
        o_ref[...] = lax.dot_general(g_ref[...], onehot, NT, precision=HIGHEST, preferred_element_type=F32)

    return _call(body, "bias_reduce", (1,),
                 [pl.BlockSpec((N_HEADS, nk), lambda i: (0, 0)), pl.BlockSpec((1, nk), lambda i: (0, 0))],
                 pl.BlockSpec((N_HEADS, REL_BUCKETS), lambda i: (0, 0)),
                 _sds((N_HEADS, REL_BUCKETS), F32))(dbias, bucket)


def _valid_mask(n):
    qi = lax.broadcasted_iota(jnp.int32, (BAND, 2 * BAND), 0)
    kj = lax.broadcasted_iota(jnp.int32, (BAND, 2 * BAND), 1)
    steps = qi + BAND - kj
    in_band = jnp.logical_and(steps >= 0, steps <= BAND)
    return jnp.logical_and(in_band, jnp.logical_or(n > 0, kj >= BAND))


ATTN_COLS = {1: 1024, 4: 512, 16: 128}


def _residue(dil, r):
    return slice(None) if dil == 1 else pl.ds(r, BAND, stride=dil)


def _attn_fwd(q, k, v, bias, g, dil, bsz, seq):
    n = q.shape[0]
    rows = BAND * dil
    nch = seq // rows
    w = N_HEADS * HEAD_DIM
    wc = ATTN_COLS[dil]
    ncol = w // wc
    hpc = wc // HEAD_DIM
    nj = wc // 128
    slab = lambda k: pltpu.VMEM((k * nj, rows, 128), F32)

    def body(q_ref, kc_ref, kp_ref, vc_ref, vp_ref, b_ref, o_ref, l_ref, qf, kf, vf, of, lf):
        ch = pl.program_id(2)
        valid = _valid_mask(ch)
        lane = lax.broadcasted_iota(jnp.int32, (BAND, 128), 1)
        for j in range(nj):
            sl = slice(128 * j, 128 * (j + 1))
            qf[j] = q_ref[:, sl].astype(F32)
            kf[j] = kp_ref[:, sl].astype(F32)
            kf[nj + j] = kc_ref[:, sl].astype(F32)
            vf[j] = vp_ref[:, sl].astype(F32)
            vf[nj + j] = vc_ref[:, sl].astype(F32)
        for r in range(dil):
            rr = _residue(dil, r)
            for j in range(nj):
                q2 = qf.at[j][rr, :].astype(BF16)
                k2 = jnp.concatenate([kf.at[j][rr, :], kf.at[nj + j][rr, :]], axis=0).astype(BF16)
                v2 = jnp.concatenate([vf.at[j][rr, :], vf.at[nj + j][rr, :]], axis=0).astype(BF16)
                outs, lses = [], []
                for half in range(2):
                    mine = (lane < HEAD_DIM) if half == 0 else (lane >= HEAD_DIM)
                    qh = jnp.where(mine, q2, jnp.zeros_like(q2))
                    s = lax.dot_general(qh, k2, NT, preferred_element_type=F32) * (HEAD_DIM ** -0.5)
                    s = jnp.where(valid, s + b_ref[2 * j + half], NEG_BIG)
                    m = jnp.max(s, axis=1, keepdims=True)
                    p = jnp.exp(s - m)
                    l = jnp.sum(p, axis=1, keepdims=True)
                    pv = jnp.dot(p.astype(BF16), v2, preferred_element_type=F32)
                    outs.append(pv / l)
                    lses.append(jnp.broadcast_to(m + jnp.log(l), (BAND, 128)))
                of.at[j][rr, :] = jnp.where(lane < HEAD_DIM, outs[0], outs[1])
                lf.at[j][rr, :] = jnp.where(lane < HEAD_DIM, lses[0], lses[1])
        for j in range(nj):
            sl = slice(128 * j, 128 * (j + 1))
            o_ref[:, sl] = of[j]
            l_ref[:, sl] = lf[j]

    base = g * ncol
    cur = lambda c, b, ch: (b * nch + ch, base + c)
    prev = lambda c, b, ch: (b * nch + jnp.maximum(ch - 1, 0), base + c)
    blk = lambda im: pl.BlockSpec((rows, wc), im)
    out_blk = pl.BlockSpec((rows, wc), lambda c, b, ch: (b * nch + ch, c))
    return _call(
        body, "attn_fwd_d%d" % dil, (ncol, bsz, nch),
        [blk(cur), blk(cur), blk(prev), blk(cur), blk(prev),
         pl.BlockSpec((hpc, BAND, 2 * BAND), lambda c, b, ch: (c, 0, 0))],
        [out_blk, out_blk],
        [_sds((n, w), F32)] * 2,
        [slab(1), slab(2), slab(2), slab(1), slab(1)],
    )(q, k, k, v, v, bias)


def _attn_combine(outs, lses):
    n, w = outs[0].shape

    def body(o0, o1, o2, l0, l1, l2, o_ref, l_ref):
        la, lb, lc = l0[...], l1[...], l2[...]
        m = jnp.maximum(jnp.maximum(la, lb), lc)
        wa, wb, wc = jnp.exp(la - m), jnp.exp(lb - m), jnp.exp(lc - m)
        tot = wa + wb + wc
        o_ref[...] = (wa * o0[...] + wb * o1[...] + wc * o2[...]) / tot
        l_ref[...] = m + jnp.log(tot)

    row = pl.BlockSpec((TM, w), lambda i: (i, 0))
    return _call(body, "attn_combine", (n // TM,), [row] * 6, [row, row],
                 [_sds((n, w), F32)] * 2)(*outs, *lses)


def _attn_bwd(q, k, v, do, o, lse, bias, g, dil, bsz, seq, comm=None):
    n = q.shape[0]
    rows = BAND * dil
    nch = seq // rows
    w = N_HEADS * HEAD_DIM
    wc = ATTN_COLS[dil]
    ncol = w // wc
    hpc = wc // HEAD_DIM
    nj = wc // 128
    slab = lambda k: pltpu.VMEM((k * nj, rows, 128), F32)
    scale = HEAD_DIM ** -0.5

    def body(q_ref, kc_ref, kp_ref, vc_ref, vp_ref, do_ref, o_ref, l_ref, b_ref,
             dq_ref, dk_ref, dv_ref, db_ref, qf, kf, vf, dof, of, lf, dqf, dkf, dvf):
        b = pl.program_id(1)
        ch = pl.program_id(2)
        cur = ch % 2
        prv = 1 - cur

        @pl.when(jnp.logical_and(b == 0, ch == 0))
        def _():
            db_ref[...] = jnp.zeros_like(db_ref)

        @pl.when(ch == 0)
        def _():
            for j in range(nj):
                dkf[nj + j] = jnp.zeros((rows, 128), F32)
                dvf[nj + j] = jnp.zeros((rows, 128), F32)

        @pl.when(ch < nch)
        def _():
            valid = _valid_mask(ch)
            lane = lax.broadcasted_iota(jnp.int32, (BAND, 128), 1)
            for j in range(nj):
                sl = slice(128 * j, 128 * (j + 1))
                qf[j] = q_ref[:, sl].astype(F32)
                kf[j] = kp_ref[:, sl].astype(F32)
                kf[nj + j] = kc_ref[:, sl].astype(F32)
                vf[j] = vp_ref[:, sl].astype(F32)
                vf[nj + j] = vc_ref[:, sl].astype(F32)
                dof[j] = do_ref[:, sl]
                of[j] = o_ref[:, sl]
                lf[j] = l_ref[:, sl]
            for r in range(dil):
                rr = _residue(dil, r)
                for j in range(nj):
                    q2 = qf.at[j][rr, :].astype(BF16)
                    k2 = jnp.concatenate([kf.at[j][rr, :], kf.at[nj + j][rr, :]], axis=0).astype(BF16)
                    v2 = jnp.concatenate([vf.at[j][rr, :], vf.at[nj + j][rr, :]], axis=0).astype(BF16)
                    do2 = dof.at[j][rr, :]
                    prod = do2 * of.at[j][rr, :]
                    lse2 = lf.at[j][rr, :]
                    dqs = []
                    dk2 = jnp.zeros((2 * BAND, 128), F32)
                    dv2 = jnp.zeros((2 * BAND, 128), F32)
                    for half in range(2):
                        mine = (lane < HEAD_DIM) if half == 0 else (lane >= HEAD_DIM)
                        qh = jnp.where(mine, q2, jnp.zeros_like(q2))
                        doh = jnp.where(mine, do2, 0.0).astype(BF16)
                        s = lax.dot_general(qh, k2, NT, preferred_element_type=F32) * scale
                        s = jnp.where(valid, s + b_ref[2 * j + half], NEG_BIG)
                        lse_h = lse2[:, 64 * half:64 * half + 1]
                        p = jnp.exp(s - lse_h)
                        dp = lax.dot_general(doh, v2, NT, preferred_element_type=F32)
                        delta = jnp.sum(jnp.where(mine, prod, 0.0), axis=1, keepdims=True)
                        ds = p * (dp - delta)
                        db_ref[2 * j + half] += ds
                        dsb = ds.astype(BF16)
                        dqs.append(jnp.dot(dsb, k2, preferred_element_type=F32) * scale)
                        dk2 = dk2 + lax.dot_general(dsb, qh, TN, preferred_element_type=F32) * scale
                        dv2 = dv2 + lax.dot_general(p.astype(BF16), doh, TN, preferred_element_type=F32)
                    dqf.at[j][rr, :] = jnp.where(lane < HEAD_DIM, dqs[0], dqs[1])
                    dkf.at[prv * nj + j][rr, :] += dk2[:BAND, :]
                    dvf.at[prv * nj + j][rr, :] += dv2[:BAND, :]
                    dkf.at[cur * nj + j][rr, :] = dk2[BAND:, :]
                    dvf.at[cur * nj + j][rr, :] = dv2[BAND:, :]
            for j in range(nj):
                dq_ref[:, 128 * j:128 * (j + 1)] = dqf[j].astype(BF16)

        for j in range(nj):
            sl = slice(128 * j, 128 * (j + 1))
            dk_ref[:, sl] = dkf[prv * nj + j].astype(BF16)
            dv_ref[:, sl] = dvf[prv * nj + j].astype(BF16)

    last = nch - 1
    base = g * ncol
    cur3 = lambda c, b, ch: (b * nch + jnp.minimum(ch, last), base + c)
    prev3 = lambda c, b, ch: (b * nch + jnp.maximum(jnp.minimum(ch, last) - 1, 0), base + c)
    cur1 = lambda c, b, ch: (b * nch + jnp.minimum(ch, last), c)
    lag1 = lambda c, b, ch: (b * nch + jnp.maximum(ch - 1, 0), c)
    blk = lambda im: pl.BlockSpec((rows, wc), im)
    bias_blk = pl.BlockSpec((hpc, BAND, 2 * BAND), lambda c, b, ch: (c, 0, 0))
    out = _sds((n, w), BF16)
    return _call(
        body, "attn_bwd_d%d" % dil, (ncol, bsz, nch + 1),
        [blk(cur3), blk(cur3), blk(prev3), blk(cur3), blk(prev3), blk(cur1), blk(cur1), blk(cur1), bias_blk],
        [blk(cur1), blk(lag1), blk(lag1), bias_blk],
        [out, out, out, _sds((N_HEADS, BAND, 2 * BAND), F32)],
        [slab(1), slab(2), slab(2), slab(1), slab(1), slab(1), slab(1), slab(2), slab(2)], comm=comm,
    )(q, k, k, v, v, do, o, lse, bias)


def _adamw(name, parts, w, m, v):
    r, c = w.shape
    nparts = parts.shape[0]
    tr = r
    for cand in (256, 352, 128):
        if r % cand == 0 and r > cand:
            tr = cand
            break
    c1 = 1.0 - ADAM_B1 ** ADAM_STEP
    c2 = 1.0 - ADAM_B2 ** ADAM_STEP

    def body(p_ref, w_ref, m_ref, v_ref, g_ref, d_ref, nm_ref, nv_ref):
        g = p_ref[0].astype(F32)
        for dev in range(1, nparts):
            g = g + p_ref[dev].astype(F32)
        nm = ADAM_B1 * m_ref[...] + (1.0 - ADAM_B1) * g
        nv = ADAM_B2 * v_ref[...] + (1.0 - ADAM_B2) * (g * g)
        m_hat = nm / c1
        v_hat = nv / c2
        g_ref[...] = g
        d_ref[...] = -ADAM_LR * (m_hat / (jnp.sqrt(v_hat) + ADAM_EPS) + ADAM_WD * w_ref[...])
        nm_ref[...] = nm
        nv_ref[...] = nv

    row = pl.BlockSpec((tr, c), lambda i: (i, 0))
    return _call(body, name, (r // tr,), [pl.BlockSpec((nparts, tr, c), lambda i: (0, i, 0)), row, row, row],
                 [row] * 4, [_sds((r, c), F32)] * 4)(parts, w, m, v)


def _exchange(name, srcs, out_shapes, items):
    n_in = len(srcs)
    n_out = len(out_shapes)
    n_items = len(items)

    def body(*refs):
        ins = refs[:n_in]
        outs = refs[n_in:n_in + n_out]
        send_sems, recv_sems, local_sems = refs[n_in + n_out:]
        x, y, c = lax.axis_index("x"), lax.axis_index("y"), lax.axis_index("c")
        me = 4 * x + 2 * y + c

        def pick(ref, sel, peer):
            idx = tuple(peer if s == "peer" else me if s == "me" else s for s in sel)
            return ref.at[idx] if idx else ref

        copies = []
        for it, (si, ssel, oi, osel) in enumerate(items):
            local = pltpu.make_async_copy(pick(ins[si], ssel, me), pick(outs[oi], osel, me), local_sems.at[it])
            local.start()
            copies.append(local)
            for j in range(1, N_DEV):
                px = 1 - x if j & 4 else x
                py = 1 - y if j & 2 else y
                pc = 1 - c if j & 1 else c
                peer = 4 * px + 2 * py + pc
                sem = it * (N_DEV - 1) + j - 1
                cp = pltpu.make_async_remote_copy(
                    src_ref=pick(ins[si], ssel, peer), dst_ref=pick(outs[oi], osel, peer),
                    send_sem=send_sems.at[sem], recv_sem=recv_sems.at[sem],
                    device_id=(px, py, pc), device_id_type=pl.DeviceIdType.MESH)
                cp.start()
                copies.append(cp)
        for cp in copies:
            cp.wait()

    any_spec = pl.BlockSpec(memory_space=pl.ANY)
    return pl.pallas_call(
        body, name=name, in_specs=[any_spec] * n_in, out_specs=[any_spec] * n_out, out_shape=list(out_shapes),
        scratch_shapes=[pltpu.SemaphoreType.DMA((n_items * (N_DEV - 1),)),
                        pltpu.SemaphoreType.DMA((n_items * (N_DEV - 1),)),
                        pltpu.SemaphoreType.DMA((n_items,))],
        compiler_params=pltpu.CompilerParams(has_side_effects=True),
    )(*srcs)


def _sel(ref, sel, slot=None):
    idx = tuple(slot if s == "slot" else s for s in sel)
    return ref.at[idx] if idx else ref


def _gather_plan(srcs, out_shapes, items):
    per = N_DEV - 1
    n_items = len(items)

    def phases(ins, outs, send_sems, recv_sems, local_sems, total):
        x, y, c = lax.axis_index("x"), lax.axis_index("y"), lax.axis_index("c")
        sibling = (x, y, 1 - c)
        chips = [(1 - x, y), (x, 1 - y), (1 - x, 1 - y)]
        slot_of = lambda px, py, pc: 4 * px + 2 * py + pc

        def copy(it, k, block, to, src=None):
            _, _, oi, osel = items[it]
            dst = _sel(outs[oi], osel, slot_of(*block))
            return pltpu.make_async_remote_copy(
                src_ref=dst if src is None else src, dst_ref=dst,
                send_sem=send_sems.at[it * per + k], recv_sem=recv_sems.at[it * per + k],
                device_id=to, device_id_type=pl.DeviceIdType.MESH)

        def own_copies(it):
            si, ssel, oi, osel = items[it]
            src = _sel(ins[si], ssel)
            mine = pltpu.make_async_copy(src, _sel(outs[oi], osel, slot_of(x, y, c)), local_sems.at[it])
            first = [copy(it, 0, (x, y, c), sibling, src=src)]
            first += [copy(it, 1 + j, (x, y, c), (*chip, c), src=src) for j, chip in enumerate(chips)]
            return mine, first

        def start():
            for it in range(n_items):
                mine, first = own_copies(it)
                mine.start()
                for cp in first:
                    cp.start()

        def forward(it):
            def go():
                for j, chip in enumerate(chips):
                    copy(it, 1 + j, (*chip, c), (x, y, c)).wait_recv()
                    copy(it, 4 + j, (*chip, c), sibling).start()
            return go

        def finish():
            for it in range(n_items):
                copy(it, 0, sibling, (x, y, c)).wait_recv()
                for j, chip in enumerate(chips):
                    copy(it, 4 + j, (*chip, 1 - c), (x, y, c)).wait_recv()
            for it in range(n_items):
                mine, first = own_copies(it)
                for cp in first:
                    cp.wait_send()
                for j, chip in enumerate(chips):
                    copy(it, 4 + j, (*chip, c), sibling).wait_send()
                mine.wait()

        lo = max(total // 3, 1)
        acts = [(0, start)]
        for it in range(n_items):
            acts.append((min(lo + it * (total - 1 - lo) // n_items, total - 2), forward(it)))
        acts.append((total - 1, finish))
        return acts

    return _Plan(srcs, out_shapes, n_items * per, n_items, phases)


def _scatter_plan(srcs, out_shapes, items):
    per = N_DEV - 1
    n_items = len(items)

    def phases(ins, outs, send_sems, recv_sems, local_sems, total):
        x, y, c = lax.axis_index("x"), lax.axis_index("y"), lax.axis_index("c")
        me = 4 * x + 2 * y + c

        def copies():
            out = []
            for it, (si, oi, osel) in enumerate(items):
                dst = _sel(outs[oi], osel, me)
                out.append(pltpu.make_async_copy(ins[si].at[me], dst, local_sems.at[it]))
                for j in range(1, N_DEV):
                    px = 1 - x if j & 4 else x
                    py = 1 - y if j & 2 else y
                    pc = 1 - c if j & 1 else c
                    out.append(pltpu.make_async_remote_copy(
                        src_ref=ins[si].at[4 * px + 2 * py + pc], dst_ref=dst,
                        send_sem=send_sems.at[it * per + j - 1], recv_sem=recv_sems.at[it * per + j - 1],
                        device_id=(px, py, pc), device_id_type=pl.DeviceIdType.MESH))
            return out

        def start():
            for cp in copies():
                cp.start()

        def finish():
            for cp in copies():
                cp.wait()

        return [(0, start), (total - 1, finish)]

    return _Plan(srcs, out_shapes, n_items * per, n_items, phases)


def _sum_slots(parts):
    _, r, c = parts.shape

    def body(p_ref, o_ref):
        acc = p_ref[0]
        for dev in range(1, N_DEV):
            acc = acc + p_ref[dev]
        o_ref[...] = acc

    return _call(body, "sum_slots", (1,), [pl.BlockSpec((N_DEV, r, c), lambda i: (0, 0, 0))],
                 pl.BlockSpec((r, c), lambda i: (0, 0)), _sds((r, c), F32))(parts)


def _pack(arrays):
    rows = []
    for a in arrays:
        flat = a.reshape(-1).astype(F32)
        pad = (-flat.shape[0]) % 1024
        if pad:
            flat = jnp.concatenate([flat, jnp.zeros((pad,), F32)])
        rows.append(flat.reshape(-1, 128))
    return jnp.concatenate(rows, axis=0)


def _unpack(buf, shapes):
    out = []
    r0 = 0
    for shp in shapes:
        size = int(np.prod(shp))
        nrows = -(-size // 1024) * 8
        out.append(buf[r0:r0 + nrows].reshape(-1)[:size].reshape(shp))
        r0 += nrows
    return out


def _block_diag_chunks(val):
    eye = jnp.eye(8, dtype=val.dtype)
    return jnp.einsum("cjhp,jk->cjhkp", val, eye).reshape(SSM_CHUNKS, CHUNK_CH, CHUNK_ST)


def _diag_blocks(dense):
    d5 = dense.reshape(SSM_CHUNKS, 8, SSM_GROUP, 8, SSM_STATE)
    return jnp.stack([d5[:, j, :, j, :] for j in range(8)], axis=1)


def _ffn_fwd(tag, hin, hin_b, w_up, conv_w, conv_b, w_down4, gain, bias, seq, comm=None):
    hc = _mm_nn("ffn_up" + tag, hin_b, w_up, FFN_HIDDEN, blocked_out=True, comm=comm)
    couts = None
    if comm is not None:
        hc, couts = hc
    nb, n, c = hc.shape
    hc = hc.reshape(2, nb // 2, n, c)
    act = _conv_gate_fwd(hc, conv_w, conv_b, seq)
    ffn = _mm_nn_red("ffn_down" + tag, act, w_down4, F32)
    hout, hout_b = _ln_fwd("ffn_norm" + tag, hin, ffn, gain, bias)
    return hout, hout_b, (hin, hin_b, hc, act, ffn), couts


def _ffn_bwd(tag, saved, dh, w_up, conv_w, conv_b, w_down4, gain, seq, comm_conv=None, comm_dx=None):
    hin, hin_b, hc, act, ffn = saved
    dr, dr_b, dgain, dbias = _ln_bwd("ffn_norm_bwd" + tag, hin, ffn, dh, gain)
    d_wdown = _mm_tn("ffn_down_dw" + tag, act, dr_b, act.shape[0], BF16, a_blocked=True, g_mode="shared")
    dact = _mm_nt_out("ffn_down_dx" + tag, dr_b, w_down4, FFN_HIDDEN)
    res = _conv_gate_bwd(hc, dact, conv_w, conv_b, seq, comm=comm_conv)
    couts_conv = None
    if comm_conv is not None:
        res, couts_conv = res
    dhc, dconv = res
    dhc8 = dhc.reshape(2 * dhc.shape[1], dhc.shape[2], dhc.shape[3])
    d_wup = _mm_tn("ffn_up_dw" + tag, hin_b, dhc8, dhc8.shape[0], BF16, g_mode="blocked")
    dhin = _mm_nt_red("ffn_up_dx" + tag, dhc8, w_up, F32, g_blocked=True, add=dr, add_scale=DN_ALPHA,
                      comm=comm_dx)
    couts_dx = None
    if comm_dx is not None:
        dhin, couts_dx = dhin
    return dhin, d_wup, d_wdown, dconv, dgain, dbias, couts_conv, couts_dx


def _local_step(x, target, p, wfull, bsz, seq, ex=None):
    n = bsz * seq
    xf = x.reshape(n, D_MODEL)
    tf = target.reshape(n, D_MODEL)

    lr = p["s5_lam_re"].reshape(1, N_STATE)
    li = p["s5_lam_im"].reshape(1, N_STATE)
    ldt = p["s5_log_dt"].reshape(1, D_MODEL // SSM_GROUP)
    br_t = p["s5_b_re"].reshape(N_STATE, SSM_GROUP).T
    bi_t = p["s5_b_im"].reshape(N_STATE, SSM_GROUP).T
    ab_r, ab_i, bb_r, bb_i, tab = _s5_prep(lr, li, ldt, br_t, bi_t)

    def bb_chunks(bb):
        return _block_diag_chunks(bb.reshape(SSM_GROUP, SSM_CHUNKS, 8, SSM_STATE).transpose(1, 2, 0, 3))

    def c_chunks(cc):
        return _block_diag_chunks(cc.reshape(SSM_CHUNKS, 8, SSM_GROUP, SSM_STATE))

    bbr_c, bbi_c = bb_chunks(bb_r).astype(BF16), bb_chunks(bb_i).astype(BF16)
    cr_c, ci_c = c_chunks(p["s5_c_re"]).astype(BF16), c_chunks(p["s5_c_im"]).astype(BF16)
    dvec = p["s5_d"].reshape(1, D_MODEL)

    if ex is None:
        y, h_r, h_i = _s5_scan_fwd(xf, bbr_c, bbi_c, cr_c, ci_c, dvec, tab, seq)
    else:
        (y, h_r, h_i), gathered = _s5_scan_fwd(xf, bbr_c, bbi_c, cr_c, ci_c, dvec, tab, seq,
                                               comm=ex.gather_first())
        p, wfull = ex.take_first(gathered, p)
    ln_g = p["ln_gain"].reshape(4, 1, D_MODEL)
    ln_b = p["ln_bias"].reshape(4, 1, D_MODEL)
    y2, z, gg = _glu_fwd(y, wfull["s5_w_glu"], p["s5_b_glu"].reshape(1, D_MODEL))
    mix = _mm_nn("s5_out", gg, wfull["s5_w_out"][None], F32)
    h1, h1b = _ln_fwd("s5_norm", xf, mix, ln_g[0], ln_b[0])
    h2, h2b, ffn0_saved, gathered = _ffn_fwd("0", h1, h1b, wfull["ffn_w_up"][0], p["conv_w"][0], p["conv_b"][0],
                                             wfull["ffn_w_down"][0], ln_g[1], ln_b[1], seq,
                                             comm=ex.gather_second() if ex else None)
    if ex is not None:
        wfull = ex.take_second(gathered, wfull)

    kmat = _mm_nn("attn_k", h2b, wfull["attn_w_kv"][:4], BF16)
    vmat = _mm_nn("attn_v", h2b, wfull["attn_w_kv"][4:], BF16)
    qmat = _mm_nn("attn_q", h2b, wfull["attn_w_q"], BF16)
    biases, outs, lses = [], [], []
    for g, dil in enumerate(DILATIONS):
        rel_t = p["rel_bias"][:, g * N_HEADS:(g + 1) * N_HEADS].T
        biases.append(_bias_expand(rel_t, dil).reshape(N_HEADS, BAND, 2 * BAND))
        o_g, l_g = _attn_fwd(qmat, kmat, vmat, biases[g], g, dil, bsz, seq)
        outs.append(o_g)
        lses.append(l_g)
    o, lse = _attn_combine(outs, lses)
    attn_out = _mm_nn("attn_out", o, wfull["attn_w_out"][None], F32)
    h3, h3b = _ln_fwd("attn_norm", h2, attn_out, ln_g[2], ln_b[2])
    h4, _, ffn1_saved, _ = _ffn_fwd("1", h3, h3b, wfull["ffn_w_up"][1], p["conv_w"][1], p["conv_b"][1],
                                    wfull["ffn_w_down"][1], ln_g[3], ln_b[3], seq)
    loss_tile, dh4 = _loss_head(h4, tf)

    grads = {}
    dh3, grads["ffn_w_up1"], grads["ffn_w_down1"], dconv1, dg3, db3, _, _ = _ffn_bwd(
        "1", ffn1_saved, dh4, wfull["ffn_w_up"][1], p["conv_w"][1], p["conv_b"][1], wfull["ffn_w_down"][1],
        ln_g[3], seq)
    dr3, dr3b, dg2, db2 = _ln_bwd("attn_norm_bwd", h2, attn_out, dh3, ln_g[2])
    grads["attn_w_out"] = _mm_tn("attn_out_dw", o, dr3b, 1, BF16, g_mode="shared")
    do = _mm_nt_red("attn_out_dx", dr3b, wfull["attn_w_out"][None], F32, g_blocked=False)
    dqs, dks, dvs, drel = [], [], [], []
    for g, dil in enumerate(DILATIONS):
        plan = ex.scatter(grads, ("ffn_w_up1", "ffn_w_down1")) if ex and dil == DILATIONS[-1] else None
        res = _attn_bwd(qmat, kmat, vmat, do, o, lse, biases[g], g, dil, bsz, seq, comm=plan)
        if plan is not None:
            res, got = res
            ex.keep(plan, got)
        dq_g, dk_g, dv_g, db_g = res
        dqs.append(dq_g)
        dks.append(dk_g)
        dvs.append(dv_g)
        drel.append(_bias_reduce(db_g.reshape(N_HEADS, BAND * 2 * BAND), dil).T)
    dq = jnp.concatenate(dqs, axis=1)
    dk = jnp.concatenate(dks, axis=1)
    dv = jnp.concatenate(dvs, axis=1)
    grads["rel_bias"] = jnp.concatenate(drel, axis=1)
    grads["attn_w_q"] = _mm_tn("attn_q_dw", h2b, dq, N_DEV, BF16)
    grads["attn_w_kv"] = jnp.concatenate([_mm_tn("attn_k_dw", h2b, dk, 4, BF16),
                                          _mm_tn("attn_v_dw", h2b, dv, 4, BF16)], axis=0)
    dh2 = _mm_nt_red("attn_q_dx", dq, wfull["attn_w_q"], F32, g_blocked=False, add=dr3, add_scale=DN_ALPHA)
    dh2 = _mm_nt_red("attn_k_dx", dk, wfull["attn_w_kv"][:4], F32, g_blocked=False, add=dh2)
    dh2 = _mm_nt_red("attn_v_dx", dv, wfull["attn_w_kv"][4:], F32, g_blocked=False, add=dh2)

    plan_conv = ex.scatter(grads, ("attn_w_q", "attn_w_out")) if ex else None
    plan_dx = ex.scatter(grads, ("attn_w_kv",)) if ex else None
    dh1, grads["ffn_w_up0"], grads["ffn_w_down0"], dconv0, dg1, db1, got_conv, got_dx = _ffn_bwd(
        "0", ffn0_saved, dh2, wfull["ffn_w_up"][0], p["conv_w"][0], p["conv_b"][0], wfull["ffn_w_down"][0],
        ln_g[1], seq, comm_conv=plan_conv, comm_dx=plan_dx)
    if ex is not None:
        ex.keep(plan_conv, got_conv)
        ex.keep(plan_dx, got_dx)
    dr1, dr1b, dg0, db0 = _ln_bwd("s5_norm_bwd", xf, mix, dh1, ln_g[0])
    grads["s5_w_out"] = _mm_tn("s5_out_dw", gg, dr1b, 1, BF16, g_mode="shared")
    dgg = _mm_nt_red("s5_out_dx", dr1b, wfull["s5_w_out"][None], F32, g_blocked=False)
    dz, d_bglu, dy2_direct = _glu_bwd_gate(y, z, dgg)
    grads["s5_w_glu"] = _mm_tn("s5_glu_dw", y2, dz, 1, BF16, g_mode="shared")
    dy2 = _mm_nt_red("s5_glu_dx", dz, wfull["s5_w_glu"][None], F32, g_blocked=False, add=dy2_direct)
    dy = _gelu_bwd(y, dy2)
    plan_last = ex.scatter(grads, ("ffn_w_up0", "ffn_w_down0", "s5_w_out", "s5_w_glu")) if ex else None
    res = _s5_scan_bwd(xf, dy, dr1, h_r, h_i, bbr_c, bbi_c, cr_c, ci_c, dvec, tab, seq, comm=plan_last)
    if ex is not None:
        res, got = res
        ex.keep(plan_last, got)
    dx, dcr, dci, dbr, dbi, dar, dai, dd = res

    def bb_from_chunks(dense):
        return _diag_blocks(dense).transpose(2, 0, 1, 3).reshape(SSM_GROUP, N_STATE)

    d_lr, d_li, d_ldt, d_br, d_bi = _s5_prep_bwd(lr, li, ldt, br_t, bi_t, dar, dai,
                                                 bb_from_chunks(dbr), bb_from_chunks(dbi))
    grads["s5_lam_re"] = d_lr.reshape(p["s5_lam_re"].shape)
    grads["s5_lam_im"] = d_li.reshape(p["s5_lam_im"].shape)
    grads["s5_log_dt"] = d_ldt.reshape(p["s5_log_dt"].shape)
    grads["s5_b_re"] = d_br.T.reshape(p["s5_b_re"].shape)
    grads["s5_b_im"] = d_bi.T.reshape(p["s5_b_im"].shape)
    grads["s5_c_re"] = _diag_blocks(dcr).reshape(p["s5_c_re"].shape)
    grads["s5_c_im"] = -_diag_blocks(dci).reshape(p["s5_c_im"].shape)
    grads["s5_d"] = dd.reshape(p["s5_d"].shape)
    grads["s5_b_glu"] = d_bglu.reshape(1, D_MODEL)
    dconv = jnp.stack([dconv0, dconv1]).reshape(2, N_DEV, 8, -1)
    grads["ffn_conv_w"] = dconv[:, :, 0:3, :].transpose(0, 2, 1, 3)
    grads["ffn_conv_b"] = dconv[:, :, 3, :]
    grads["ln_gain"] = jnp.stack([dg0, dg1, dg2, dg3]).reshape(2, 2, D_MODEL)
    grads["ln_bias"] = jnp.stack([db0, db1, db2, db3]).reshape(2, 2, D_MODEL)
    return loss_tile[0, 0], dx.reshape(x.shape), grads


SMALL_REPLICATED = ("s5_lam_re", "s5_lam_im", "s5_log_dt", "s5_b_re", "s5_b_im", "s5_c_re", "s5_c_im", "s5_d",
                    "rel_bias", "ffn_conv_b")
SMALL_SHARDED = ("s5_b_glu", "ffn_conv_w", "ln_gain", "ln_bias")
BIG = ("s5_w_glu", "s5_w_out", "attn_w_kv", "attn_w_q", "attn_w_out", "ffn_w_up", "ffn_w_down")
WEIGHTS = ("s5_lam_re", "s5_lam_im", "s5_log_dt", "s5_b_re", "s5_b_im", "s5_c_re", "s5_c_im", "s5_d", "s5_w_glu",
           "s5_b_glu", "s5_w_out", "attn_w_kv", "attn_w_q", "attn_w_out", "rel_bias", "ffn_w_up", "ffn_conv_w",
           "ffn_conv_b", "ffn_w_down", "ln_gain", "ln_bias")


class _Exchanges:
    def __init__(self, w):
        self.w = w
        self.parts = {}
        self.up = w["ffn_w_up"].astype(BF16)
        self.down = w["ffn_w_down"].astype(BF16)

    def gather_first(self):
        w = self.w
        srcs = [w["s5_w_glu"][0].astype(BF16), w["s5_w_out"][0].astype(BF16),
                _pack([w[k] for k in SMALL_SHARDED]), self.up, self.down, w["attn_w_kv"].astype(BF16),
                w["attn_w_q"][0].astype(BF16), w["attn_w_out"][0].astype(BF16)]
        outs = [_sds((N_DEV,) + (s.shape[1:] if i in (3, 4) else s.shape), s.dtype) for i, s in enumerate(srcs)]
        items = [(i, (0,) if i in (3, 4) else (), i, ("slot",)) for i in range(len(srcs))]
        return _gather_plan(srcs, outs, items)

    def take_first(self, g, p):
        d = D_MODEL
        w = self.w
        wfull = {
            "s5_w_glu": g[0].reshape(d, d),
            "s5_w_out": g[1].reshape(d, d),
            "ffn_w_up": [g[3], None],
            "ffn_w_down": [g[4].reshape(4, -1, d), None],
            "attn_w_kv": g[5],
            "attn_w_q": g[6],
            "attn_w_out": g[7].reshape(d, d),
        }
        smalls = [_unpack(g[2][dev], [w[k].shape for k in SMALL_SHARDED]) for dev in range(N_DEV)]
        c_ff = w["ffn_conv_w"].shape[2]
        conv_w = jnp.stack([s[1] for s in smalls], axis=2)
        p = dict(p)
        p.update(s5_b_glu=jnp.concatenate([s[0] for s in smalls], axis=1),
                 ln_gain=jnp.concatenate([s[2] for s in smalls], axis=2),
                 ln_bias=jnp.concatenate([s[3] for s in smalls], axis=2),
                 conv_w=conv_w.transpose(0, 2, 1, 3).reshape(2, 2, 4, 3, c_ff),
                 conv_b=w["ffn_conv_b"].reshape(2, 2, 4, 1, c_ff))
        return p, wfull

    def gather_second(self):
        srcs = [self.up, self.down]
        outs = [_sds((N_DEV,) + s.shape[1:], BF16) for s in srcs]
        return _gather_plan(srcs, outs, [(0, (1,), 0, ("slot",)), (1, (1,), 1, ("slot",))])

    def take_second(self, g, wfull):
        wfull = dict(wfull)
        wfull["ffn_w_up"] = [wfull["ffn_w_up"][0], g[0]]
        wfull["ffn_w_down"] = [wfull["ffn_w_down"][0], g[1].reshape(4, -1, D_MODEL)]
        return wfull

    def scatter(self, grads, names):
        srcs = [grads[k].reshape(N_DEV, -1, grads[k].shape[-1]) for k in names]
        plan = _scatter_plan(srcs, [_sds(s.shape, BF16) for s in srcs],
                             [(i, i, ("slot",)) for i in range(len(names))])
        plan.names = names
        return plan

    def keep(self, plan, outs):
        for k, o in zip(plan.names, outs):
            self.parts[k] = o


def kernel(x, s5_lam_re, s5_lam_im, s5_log_dt, s5_b_re, s5_b_im, s5_c_re, s5_c_im, s5_d, s5_w_glu, s5_b_glu, s5_w_out, attn_w_kv, attn_w_q, attn_w_out, rel_bias, ffn_w_up, ffn_conv_w, ffn_conv_b, ffn_w_down, ln_gain, ln_bias, loss_target, m_s5_lam_re, m_s5_lam_im, m_s5_log_dt, m_s5_b_re, m_s5_b_im, m_s5_c_re, m_s5_c_im, m_s5_d, m_s5_w_glu, m_s5_b_glu, m_s5_w_out, m_attn_w_kv, m_attn_w_q, m_attn_w_out, m_rel_bias, m_ffn_w_up, m_ffn_conv_w, m_ffn_conv_b, m_ffn_w_down, m_ln_gain, m_ln_bias, v_s5_lam_re, v_s5_lam_im, v_s5_log_dt, v_s5_b_re, v_s5_b_im, v_s5_c_re, v_s5_c_im, v_s5_d, v_s5_w_glu, v_s5_b_glu, v_s5_w_out, v_attn_w_kv, v_attn_w_q, v_attn_w_out, v_rel_bias, v_ffn_w_up, v_ffn_conv_w, v_ffn_conv_b, v_ffn_w_down, v_ln_gain, v_ln_bias):
    w = dict(s5_lam_re=s5_lam_re, s5_lam_im=s5_lam_im, s5_log_dt=s5_log_dt, s5_b_re=s5_b_re, s5_b_im=s5_b_im,
             s5_c_re=s5_c_re, s5_c_im=s5_c_im, s5_d=s5_d, s5_w_glu=s5_w_glu, s5_b_glu=s5_b_glu, s5_w_out=s5_w_out,
             attn_w_kv=attn_w_kv, attn_w_q=attn_w_q, attn_w_out=attn_w_out, rel_bias=rel_bias, ffn_w_up=ffn_w_up,
             ffn_conv_w=ffn_conv_w, ffn_conv_b=ffn_conv_b, ffn_w_down=ffn_w_down, ln_gain=ln_gain, ln_bias=ln_bias)
    mom = dict(s5_lam_re=m_s5_lam_re, s5_lam_im=m_s5_lam_im, s5_log_dt=m_s5_log_dt, s5_b_re=m_s5_b_re,
               s5_b_im=m_s5_b_im, s5_c_re=m_s5_c_re, s5_c_im=m_s5_c_im, s5_d=m_s5_d, s5_w_glu=m_s5_w_glu,
               s5_b_glu=m_s5_b_glu, s5_w_out=m_s5_w_out, attn_w_kv=m_attn_w_kv, attn_w_q=m_attn_w_q,
               attn_w_out=m_attn_w_out, rel_bias=m_rel_bias, ffn_w_up=m_ffn_w_up, ffn_conv_w=m_ffn_conv_w,
               ffn_conv_b=m_ffn_conv_b, ffn_w_down=m_ffn_w_down, ln_gain=m_ln_gain, ln_bias=m_ln_bias)
    var = dict(s5_lam_re=v_s5_lam_re, s5_lam_im=v_s5_lam_im, s5_log_dt=v_s5_log_dt, s5_b_re=v_s5_b_re,
               s5_b_im=v_s5_b_im, s5_c_re=v_s5_c_re, s5_c_im=v_s5_c_im, s5_d=v_s5_d, s5_w_glu=v_s5_w_glu,
               s5_b_glu=v_s5_b_glu, s5_w_out=v_s5_w_out, attn_w_kv=v_attn_w_kv, attn_w_q=v_attn_w_q,
               attn_w_out=v_attn_w_out, rel_bias=v_rel_bias, ffn_w_up=v_ffn_w_up, ffn_conv_w=v_ffn_conv_w,
               ffn_conv_b=v_ffn_conv_b, ffn_w_down=v_ffn_w_down, ln_gain=v_ln_gain, ln_bias=v_ln_bias)
    bsz, seq, _ = x.shape
    me = 4 * lax.axis_index("x") + 2 * lax.axis_index("y") + lax.axis_index("c")

    ex = _Exchanges(w)
    loss_part, grad_x, g = _local_step(x, loss_target, dict(w), None, bsz, seq, ex=ex)
    loss = lax.psum(loss_part, ("x", "y", "c"))

    d = D_MODEL
    results = {}
    for name in BIG:
        shard = w[name]
        if name in ("ffn_w_up", "ffn_w_down"):
            per_layer = [_adamw("adamw_%s%d" % (name, l), ex.parts[name + str(l)], shard[l], mom[name][l],
                                var[name][l]) for l in range(2)]
            results[name] = tuple(jnp.stack([per_layer[0][j], per_layer[1][j]]) for j in range(4))
        else:
            r2 = (-1, shard.shape[-1])
            outs4 = _adamw("adamw_" + name, ex.parts[name], shard.reshape(r2), mom[name].reshape(r2),
                           var[name].reshape(r2))
            results[name] = tuple(o.reshape(shard.shape) for o in outs4)

    small_names = SMALL_REPLICATED + SMALL_SHARDED
    packed = _pack([g[k] for k in small_names])
    slots = _exchange("allreduce_small", [packed], [_sds((N_DEV,) + packed.shape, F32)],
                      [(0, (), 0, ("me",))])[0]
    total = _unpack(_sum_slots(slots), [g[k].shape for k in small_names])
    gsum = dict(zip(small_names, total))
    mine = {k: gsum[k] for k in SMALL_REPLICATED}
    mine["s5_b_glu"] = lax.dynamic_slice_in_dim(gsum["s5_b_glu"], me * (d // N_DEV), d // N_DEV, axis=1)
    mine["ffn_conv_w"] = lax.dynamic_index_in_dim(gsum["ffn_conv_w"], me, axis=2, keepdims=False)
    mine["ln_gain"] = lax.dynamic_slice_in_dim(gsum["ln_gain"], me * (d // N_DEV), d // N_DEV, axis=2)
    mine["ln_bias"] = lax.dynamic_slice_in_dim(gsum["ln_bias"], me * (d // N_DEV), d // N_DEV, axis=2)
    gp = _pack([mine[k] for k in small_names])
    outs4 = _adamw("adamw_small", gp[None], _pack([w[k] for k in small_names]),
                   _pack([mom[k] for k in small_names]), _pack([var[k] for k in small_names]))
    shapes = [w[k].shape for k in small_names]
    unpacked = [_unpack(o, shapes) for o in outs4]
    for i, k in enumerate(small_names):
        results[k] = tuple(unpacked[j][i] for j in range(4))

    out = [loss, grad_x]
    for j in range(4):
        out.extend(results[k][j] for k in WEIGHTS)
    return tuple(out)
```

```python
import functools
import math

import numpy as np
import jax
import jax.numpy as jnp
from jax import lax
from jax.experimental import pallas as pl
from jax.experimental.pallas import tpu as pltpu

F32 = jnp.float32
BF16 = jnp.bfloat16
HIGHEST = lax.Precision.HIGHEST

N_DEV = 8
D_MODEL = 1024
SSM_GROUP = 16
SSM_STATE = 64
SSM_CHUNKS = 8
CHUNK_CH = D_MODEL // SSM_CHUNKS
CHUNK_ST = CHUNK_CH // SSM_GROUP * SSM_STATE
N_STATE = D_MODEL // SSM_GROUP * SSM_STATE
HEAD_DIM = 64
N_HEADS = 16
BAND = 128
DILATIONS = (1, 4, 16)
REL_BUCKETS = 32
REL_MAX_DIST = 2048
NEG_BIG = -1e30
DN_ALPHA = (2.0 * 2) ** 0.25
LN_EPS = 1e-5
ADAM_LR = 0.001
ADAM_B1 = 0.9
ADAM_B2 = 0.999
ADAM_EPS = 1e-08
ADAM_WD = 0.01
ADAM_STEP = 10

TM = 512
T_SCAN = 256
VMEM_LIMIT = 56 * 1024 * 1024
FFN_HIDDEN = BF16

NN = (((1,), (0,)), ((), ()))
NT = (((1,), (1,)), ((), ()))
TN = (((0,), (0,)), ((), ()))


class _Plan:
    def __init__(self, srcs, out_shapes, n_sems, n_local, phases):
        self.srcs, self.out_shapes, self.n_sems, self.n_local, self.phases = srcs, out_shapes, n_sems, n_local, phases


def _call(body, name, grid, in_specs, out_specs, out_shape, scratch=(), comm=None):
    params = dict(dimension_semantics=("arbitrary",) * len(grid), vmem_limit_bytes=VMEM_LIMIT)
    if comm is None:
        return pl.pallas_call(
            body, name=name, grid=grid, in_specs=in_specs, out_specs=out_specs, out_shape=out_shape,
            scratch_shapes=list(scratch), compiler_params=pltpu.CompilerParams(**params))
    single = not isinstance(out_specs, (list, tuple))
    o_specs = [out_specs] if single else list(out_specs)
    o_shape = [out_shape] if single else list(out_shape)
    n_in, n_out, n_scr = len(in_specs), len(o_specs), len(scratch)
    n_cin, n_cout = len(comm.srcs), len(comm.out_shapes)
    total = int(np.prod(grid))

    def wrapped(*refs):
        ins = refs[:n_in]
        cins = refs[n_in:n_in + n_cin]
        outs = refs[n_in + n_cin:n_in + n_cin + n_out]
        couts = refs[n_in + n_cin + n_out:n_in + n_cin + n_out + n_cout]
        rest = refs[n_in + n_cin + n_out + n_cout:]
        scr, sems = rest[:n_scr], rest[n_scr:]
        step = pl.program_id(0)
        for ax in range(1, len(grid)):
            step = step * grid[ax] + pl.program_id(ax)
        phases = comm.phases(cins, couts, *sems, total)
        for at, action in phases:
            if at == 0:
                pl.when(step == 0)(action)
        body(*ins, *outs, *scr)
        for at, action in phases:
            if at > 0:
                pl.when(step == at)(action)

    any_spec = pl.BlockSpec(memory_space=pl.ANY)
    call = pl.pallas_call(
        wrapped, name=name, grid=grid, in_specs=list(in_specs) + [any_spec] * n_cin,
        out_specs=o_specs + [any_spec] * n_cout, out_shape=o_shape + list(comm.out_shapes),
        scratch_shapes=list(scratch) + [pltpu.SemaphoreType.DMA((comm.n_sems,)),
                                        pltpu.SemaphoreType.DMA((comm.n_sems,)),
                                        pltpu.SemaphoreType.DMA((max(comm.n_local, 1),))],
        compiler_params=pltpu.CompilerParams(has_side_effects=True, **params))

    def run(*args):
        res = call(*args, *comm.srcs)
        own = res[0] if single else res[:n_out]
        return own, res[n_out:]

    return run


def _sds(shape, dtype):
    return jax.ShapeDtypeStruct(shape, dtype)


def _mm(name, a, b, *, dims, grid, a_spec, b_spec, o_spec, out_shape, nred=1, red_axis=0,
        add=None, add_spec=None, add_scale=1.0, comm=None):
    has_add = add is not None
    acc_shape = tuple(s for s in o_spec.block_shape if s is not None)

    def body(*refs):
        if has_add:
            a_ref, b_ref, add_ref, o_ref = refs[:4]
            rest = refs[4:]
        else:
            a_ref, b_ref, o_ref = refs[:3]
            rest = refs[3:]
        prod = lax.dot_general(a_ref[...].astype(BF16), b_ref[...].astype(BF16), dims,
                               preferred_element_type=F32)

        def finish(val):
            if has_add:
                val = val + add_scale * add_ref[...].astype(F32)
            o_ref[...] = val.astype(o_ref.dtype)

        if nred == 1:
            finish(prod)
        else:
            acc = rest[0]
            k = pl.program_id(red_axis)

            @pl.when(k == 0)
            def _():
                acc[...] = prod

            @pl.when(k > 0)
            def _():
                acc[...] += prod

            @pl.when(k == nred - 1)
            def _():
                finish(acc[...])

    in_specs = [a_spec, b_spec] + ([add_spec] if has_add else [])
    scratch = [pltpu.VMEM(acc_shape, F32)] if nred > 1 else []
    args = (a, b) + ((add,) if has_add else ())
    return _call(body, name, grid, in_specs, o_spec, out_shape, scratch, comm=comm)(*args)


def _mm_nn(name, a, b, out_dtype, blocked_out=False, comm=None, blocks=None):
    m, k = a.shape
    _, _, n = b.shape
    nb = b.shape[0] if blocks is None else blocks[1]
    first = 0 if blocks is None else blocks[0]
    if blocked_out:
        o_spec = pl.BlockSpec((None, TM, n), lambda d, i: (d, i, 0))
        out_shape = _sds((nb, m, n), out_dtype)
    else:
        o_spec = pl.BlockSpec((TM, n), lambda d, i: (i, d))
        out_shape = _sds((m, nb * n), out_dtype)
    return _mm(name, a, b, dims=NN, grid=(nb, m // TM),
               a_spec=pl.BlockSpec((TM, k), lambda d, i: (i, 0)),
               b_spec=pl.BlockSpec((None, k, n), lambda d, i: (d + first, 0, 0)),
               o_spec=o_spec, out_shape=out_shape, comm=comm)


def _mm_nn_red(name, a, b, out_dtype):
    nb, m, k = a.shape
    n = b.shape[2]
    return _mm(name, a, b, dims=NN, grid=(m // TM, nb), nred=nb, red_axis=1,
               a_spec=pl.BlockSpec((None, TM, k), lambda i, d: (d, i, 0)),
               b_spec=pl.BlockSpec((None, k, n), lambda i, d: (d, 0, 0)),
               o_spec=pl.BlockSpec((TM, n), lambda i, d: (i, 0)),
               out_shape=_sds((m, n), out_dtype))


def _mm_tn(name, a, g, nb, out_dtype, a_blocked=False, g_mode="cols"):
    if a_blocked:
        _, m, ka = a.shape
        a_spec = pl.BlockSpec((None, TM, ka), lambda d, i: (d, i, 0))
    else:
        m, ka = a.shape
        a_spec = pl.BlockSpec((TM, ka), lambda d, i: (i, 0))
    if g_mode == "cols":
        n = g.shape[1] // nb
        g_spec = pl.BlockSpec((TM, n), lambda d, i: (i, d))
    elif g_mode == "blocked":
        n = g.shape[2]
        g_spec = pl.BlockSpec((None, TM, n), lambda d, i: (d, i, 0))
    else:
        n = g.shape[1]
        g_spec = pl.BlockSpec((TM, n), lambda d, i: (i, 0))
    nred = m // TM
    return _mm(name, a, g, dims=TN, grid=(nb, nred), nred=nred, red_axis=1,
               a_spec=a_spec, b_spec=g_spec,
               o_spec=pl.BlockSpec((None, ka, n), lambda d, i: (d, 0, 0)),
               out_shape=_sds((nb, ka, n), out_dtype))


def _mm_nt_red(name, g, w, out_dtype, g_blocked, add=None, add_scale=1.0, comm=None, blocks=None):
    _, k, n = w.shape
    nb = w.shape[0] if blocks is None else blocks[1]
    first = 0 if blocks is None else blocks[0]
    if g_blocked:
        m = g.shape[1]
        g_spec = pl.BlockSpec((None, TM, n), lambda i, d: (d, i, 0))
    else:
        m = g.shape[0]
        g_spec = pl.BlockSpec((TM, n), lambda i, d: (i, d))
    o_spec = pl.BlockSpec((TM, k), lambda i, d: (i, 0))
    return _mm(name, g, w, dims=NT, grid=(m // TM, nb), nred=nb, red_axis=1,
               a_spec=g_spec, b_spec=pl.BlockSpec((None, k, n), lambda i, d: (d + first, 0, 0)),
               o_spec=o_spec, out_shape=_sds((m, k), out_dtype),
               add=add, add_spec=o_spec if add is not None else None, add_scale=add_scale, comm=comm)


def _mm_nt_out(name, g, w, out_dtype):
    m, n = g.shape
    nb, k, _ = w.shape
    return _mm(name, g, w, dims=NT, grid=(nb, m // TM),
               a_spec=pl.BlockSpec((TM, n), lambda d, i: (i, 0)),
               b_spec=pl.BlockSpec((None, k, n), lambda d, i: (d, 0, 0)),
               o_spec=pl.BlockSpec((None, TM, k), lambda d, i: (d, i, 0)),
               out_shape=_sds((nb, m, k), out_dtype))


def _ln_stats(x, f):
    r = DN_ALPHA * x + f
    mu = jnp.mean(r, axis=-1, keepdims=True)
    xc = r - mu
    var = jnp.mean(xc * xc, axis=-1, keepdims=True)
    rstd = lax.rsqrt(var + LN_EPS)
    return xc * rstd, rstd


def _ln_fwd(name, xin, f, gain, bias):
    n, d = xin.shape

    def body(x_ref, f_ref, g_ref, b_ref, o_ref, ob_ref):
        xhat, _ = _ln_stats(x_ref[...], f_ref[...])
        out = xhat * g_ref[...] + b_ref[...]
        o_ref[...] = out
        ob_ref[...] = out.astype(BF16)

    row = pl.BlockSpec((TM, d), lambda i: (i, 0))
    vec = pl.BlockSpec((1, d), lambda i: (0, 0))
    return _call(body, name, (n // TM,), [row, row, vec, vec], [row, row],
                 [_sds((n, d), F32), _sds((n, d), BF16)])(xin, f, gain, bias)


def _ln_bwd(name, xin, f, dy, gain):
    n, d = xin.shape

    def body(x_ref, f_ref, dy_ref, g_ref, dr_ref, drb_ref, dg_ref, db_ref):
        i = pl.program_id(0)
        xhat, rstd = _ln_stats(x_ref[...], f_ref[...])
        dy_v = dy_ref[...]
        dxh = dy_v * g_ref[...]
        m1 = jnp.mean(dxh, axis=-1, keepdims=True)
        m2 = jnp.mean(dxh * xhat, axis=-1, keepdims=True)
        dr = rstd * (dxh - m1 - xhat * m2)
        dr_ref[...] = dr
        drb_ref[...] = dr.astype(BF16)
        dg = jnp.sum(dy_v * xhat, axis=0, keepdims=True)
        db = jnp.sum(dy_v, axis=0, keepdims=True)

        @pl.when(i == 0)
        def _():
            dg_ref[...] = dg
            db_ref[...] = db

        @pl.when(i > 0)
        def _():
            dg_ref[...] += dg
            db_ref[...] += db

    row = pl.BlockSpec((TM, d), lambda i: (i, 0))
    vec = pl.BlockSpec((1, d), lambda i: (0, 0))
    return _call(body, name, (n // TM,), [row, row, row, vec], [row, row, vec, vec],
                 [_sds((n, d), F32), _sds((n, d), BF16), _sds((1, d), F32), _sds((1, d), F32)])(xin, f, dy, gain)


def _loss_head(y, target):
    n, d = y.shape

    def body(y_ref, t_ref, l_ref, dy_ref):
        i = pl.program_id(0)
        e = y_ref[...] - t_ref[...]
        dy_ref[...] = e * (1.0 / d)
        part = jnp.sum(jnp.sum(e * e, axis=1, keepdims=True), axis=0, keepdims=True) * (0.5 / d)
        part = jnp.broadcast_to(part, (8, 128))

        @pl.when(i == 0)
        def _():
            l_ref[...] = part

        @pl.when(i > 0)
        def _():
            l_ref[...] += part

    row = pl.BlockSpec((TM, d), lambda i: (i, 0))
    return _call(body, "loss_head", (n // TM,), [row, row],
                 [pl.BlockSpec((8, 128), lambda i: (0, 0)), row],
                 [_sds((8, 128), F32), _sds((n, d), F32)])(y, target)


def _group_expand_matrix():
    e = np.zeros((D_MODEL // SSM_GROUP, N_STATE), np.float32)
    for g in range(D_MODEL // SSM_GROUP):
        e[g, g * SSM_STATE:(g + 1) * SSM_STATE] = 1.0
    return jnp.asarray(e)


def _discretise(lr, li, ldt, br, bi, expand):
    dt = jnp.dot(jnp.exp(ldt), expand, precision=HIGHEST, preferred_element_type=F32)
    mag = jnp.exp(lr * dt)
    th = li * dt
    ab_r = mag * jnp.cos(th)
    ab_i = mag * jnp.sin(th)
    den = lr * lr + li * li
    nr = ab_r - 1.0
    co_r = (nr * lr + ab_i * li) / den
    co_i = (ab_i * lr - nr * li) / den
    bb_r = co_r * br - co_i * bi
    bb_i = co_r * bi + co_i * br
    return ab_r, ab_i, bb_r, bb_i


def _cmul(ar, ai, br, bi):
    return ar * br - ai * bi, ar * bi + ai * br


def _s5_prep(lr, li, ldt, br, bi):
    s = N_STATE
    expand = _group_expand_matrix()

    def body(lr_ref, li_ref, ldt_ref, br_ref, bi_ref, e_ref, abr_ref, abi_ref, bbr_ref, bbi_ref, tab_ref):
        ab_r, ab_i, bb_r, bb_i = _discretise(lr_ref[...], li_ref[...], ldt_ref[...], br_ref[...],
                                             bi_ref[...], e_ref[...])
        abr_ref[...] = ab_r
        abi_ref[...] = ab_i
        bbr_ref[...] = bb_r
        bbi_ref[...] = bb_i
        row = lax.broadcasted_iota(jnp.int32, (8, s), 0)
        for base, sign in ((0, 1.0), (8, -1.0)):
            p = [None] * 9
            p[1] = (ab_r, sign * ab_i)
            p[2] = _cmul(*p[1], *p[1])
            p[3] = _cmul(*p[2], *p[1])
            p[4] = _cmul(*p[2], *p[2])
            p[5] = _cmul(*p[4], *p[1])
            p[6] = _cmul(*p[4], *p[2])
            p[7] = _cmul(*p[4], *p[3])
            p[8] = _cmul(*p[4], *p[4])
            for j, k in enumerate((1, 2, 4)):
                keep = (row >= k) if base == 0 else (row < 8 - k)
                tab_ref[base + 2 * j] = jnp.where(keep, jnp.broadcast_to(p[k][0], (8, s)), 0.0)
                tab_ref[base + 2 * j + 1] = jnp.where(keep, jnp.broadcast_to(p[k][1], (8, s)), 0.0)
            pw_r = jnp.zeros((8, s), F32)
            pw_i = jnp.zeros((8, s), F32)
            for i in range(8):
                e = i + 1 if base == 0 else 8 - i
                pw_r = jnp.where(row == i, jnp.broadcast_to(p[e][0], (8, s)), pw_r)
                pw_i = jnp.where(row == i, jnp.broadcast_to(p[e][1], (8, s)), pw_i)
            tab_ref[base + 6] = pw_r
            tab_ref[base + 7] = pw_i

    def full(shape):
        return pl.BlockSpec(shape, lambda i: (0,) * len(shape))

    g = D_MODEL // SSM_GROUP
    return _call(body, "s5_prep", (1,),
                 [full((1, s)), full((1, s)), full((1, g)), full((16, s)), full((16, s)), full((g, s))],
                 [full((1, s)), full((1, s)), full((16, s)), full((16, s)), full((16, 8, s))],
                 [_sds((1, s), F32), _sds((1, s), F32), _sds((16, s), F32), _sds((16, s), F32),
                  _sds((16, 8, s), F32)])(lr, li, ldt, br, bi, expand)


def _s5_prep_bwd(lr, li, ldt, br, bi, d_abr, d_abi, d_bbr, d_bbi):
    s = N_STATE
    g = D_MODEL // SSM_GROUP
    expand = _group_expand_matrix()

    def body(lr_ref, li_ref, ldt_ref, br_ref, bi_ref, e_ref, c1, c2, c3, c4, o1, o2, o3, o4, o5):
        e_val = e_ref[...]
        _, vjp = jax.vjp(lambda a, b, c, d, e: _discretise(a, b, c, d, e, e_val),
                         lr_ref[...], li_ref[...], ldt_ref[...], br_ref[...], bi_ref[...])
        grads = vjp((c1[...], c2[...], c3[...], c4[...]))
        for o, gr in zip((o1, o2, o3, o4, o5), grads):
            o[...] = gr

    def full(shape):
        return pl.BlockSpec(shape, lambda i: (0,) * len(shape))

    return _call(body, "s5_prep_bwd", (1,),
                 [full((1, s)), full((1, s)), full((1, g)), full((16, s)), full((16, s)), full((g, s)),
                  full((1, s)), full((1, s)), full((16, s)), full((16, s))],
                 [full((1, s)), full((1, s)), full((1, g)), full((16, s)), full((16, s))],
                 [_sds((1, s), F32), _sds((1, s), F32), _sds((1, g), F32), _sds((16, s), F32),
                  _sds((16, s), F32)])(lr, li, ldt, br, bi, expand, d_abr, d_abi, d_bbr, d_bbi)


def _scan8(vr, vi, tab_ref, base, reverse):
    for j, k in enumerate((1, 2, 4)):
        mr = tab_ref[base + 2 * j]
        mi = tab_ref[base + 2 * j + 1]
        shift = 8 - k if reverse else k
        sr = pltpu.roll(vr, shift, 0)
        si = pltpu.roll(vi, shift, 0)
        vr, vi = vr + mr * sr - mi * si, vi + mr * si + mi * sr
    return vr, vi


def _s5_scan_fwd(x, bbr, bbi, cr, ci, dvec, tab, seq, comm=None):
    n, d = x.shape
    nt = seq // T_SCAN
    nblk = T_SCAN // 8
    st = CHUNK_ST

    def body(x_ref, bbr_ref, bbi_ref, cr_ref, ci_ref, d_ref, tab_ref, y_ref, hr_ref, hi_ref, car_r, car_i):
        t = pl.program_id(2)

        @pl.when(t == 0)
        def _():
            car_r[...] = jnp.zeros_like(car_r)
            car_i[...] = jnp.zeros_like(car_i)

        u = x_ref[...]
        ub = u.astype(BF16)
        hr_ref[...] = jnp.dot(ub, bbr_ref[...], preferred_element_type=F32)
        hi_ref[...] = jnp.dot(ub, bbi_ref[...], preferred_element_type=F32)

        def step(i, carry):
            sl = pl.ds(pl.multiple_of(i * 8, 8), 8)
            vr, vi = _scan8(hr_ref[sl, :], hi_ref[sl, :], tab_ref, 0, False)
            c_r = jnp.broadcast_to(car_r[...], (8, st))
            c_i = jnp.broadcast_to(car_i[...], (8, st))
            pr = tab_ref[6]
            pi = tab_ref[7]
            vr, vi = vr + pr * c_r - pi * c_i, vi + pr * c_i + pi * c_r
            hr_ref[sl, :] = vr
            hi_ref[sl, :] = vi
            car_r[...] = vr[7:8, :]
            car_i[...] = vi[7:8, :]
            return carry

        lax.fori_loop(0, nblk, step, 0, unroll=4)
        y = lax.dot_general(hr_ref[...].astype(BF16), cr_ref[...], NT, preferred_element_type=F32)
        y = y - lax.dot_general(hi_ref[...].astype(BF16), ci_ref[...], NT, preferred_element_type=F32)
        y_ref[...] = y + d_ref[...] * u

    row = lambda c, b, t: (b * nt + t, c)
    mat = pl.BlockSpec((None, CHUNK_CH, st), lambda c, b, t: (c, 0, 0))
    return _call(
        body, "s5_scan_fwd", (SSM_CHUNKS, n // seq, nt),
        [pl.BlockSpec((T_SCAN, CHUNK_CH), row), mat, mat, mat, mat,
         pl.BlockSpec((1, CHUNK_CH), lambda c, b, t: (0, c)),
         pl.BlockSpec((16, 8, st), lambda c, b, t: (0, 0, c))],
        [pl.BlockSpec((T_SCAN, CHUNK_CH), row), pl.BlockSpec((T_SCAN, st), row), pl.BlockSpec((T_SCAN, st), row)],
        [_sds((n, d), F32), _sds((n, N_STATE), F32), _sds((n, N_STATE), F32)],
        [pltpu.VMEM((1, st), F32), pltpu.VMEM((1, st), F32)], comm=comm,
    )(x, bbr, bbi, cr, ci, dvec, tab)


def _s5_scan_bwd(x, dy, dres, hr, hi, bbr, bbi, cr, ci, dvec, tab, seq, comm=None):
    n, d = x.shape
    nt = seq // T_SCAN
    nblk = T_SCAN // 8
    st = CHUNK_ST

    def body(x_ref, dy_ref, dres_ref, hr_ref, hi_ref, hpr_ref, hpi_ref, bbr_ref, bbi_ref, cr_ref, ci_ref,
             d_ref, tab_ref, dx_ref, dcr_ref, dci_ref, dbr_ref, dbi_ref, dar_ref, dai_ref, dd_ref,
             g_r, g_i, car_r, car_i):
        b = pl.program_id(1)
        tg = pl.program_id(2)
        first = jnp.logical_and(b == 0, tg == 0)

        @pl.when(tg == 0)
        def _():
            car_r[...] = jnp.zeros_like(car_r)
            car_i[...] = jnp.zeros_like(car_i)

        u = x_ref[...]
        dyv = dy_ref[...]
        ub = u.astype(BF16)
        dyb = dyv.astype(BF16)
        g_r[...] = jnp.dot(dyb, cr_ref[...], preferred_element_type=F32)
        g_i[...] = -jnp.dot(dyb, ci_ref[...], preferred_element_type=F32)

        def step(ii, carry):
            i = nblk - 1 - ii
            sl = pl.ds(pl.multiple_of(i * 8, 8), 8)
            vr, vi = _scan8(g_r[sl, :], g_i[sl, :], tab_ref, 8, True)
            c_r = jnp.broadcast_to(car_r[...], (8, st))
            c_i = jnp.broadcast_to(car_i[...], (8, st))
            pr = tab_ref[14]
            pi = tab_ref[15]
            vr, vi = vr + pr * c_r - pi * c_i, vi + pr * c_i + pi * c_r
            g_r[sl, :] = vr
            g_i[sl, :] = vi
            car_r[...] = vr[0:1, :]
            car_i[...] = vi[0:1, :]
            return carry

        lax.fori_loop(0, nblk, step, 0, unroll=4)
        gr = g_r[...]
        gi = g_i[...]
        grb = gr.astype(BF16)
        gib = gi.astype(BF16)
        du = lax.dot_general(grb, bbr_ref[...], NT, preferred_element_type=F32)
        du = du + lax.dot_general(gib, bbi_ref[...], NT, preferred_element_type=F32)
        dx_ref[...] = DN_ALPHA * dres_ref[...] + du + d_ref[...] * dyv

        h_r = hr_ref[...]
        h_i = hi_ref[...]
        t_orig = nt - 1 - tg
        row = lax.broadcasted_iota(jnp.int32, (T_SCAN, st), 0)
        halo_ok = t_orig > 0
        top_r = jnp.where(halo_ok, hpr_ref[7:8, :], 0.0)
        top_i = jnp.where(halo_ok, hpi_ref[7:8, :], 0.0)
        hp_r = jnp.where(row == 0, jnp.broadcast_to(top_r, (T_SCAN, st)), pltpu.roll(h_r, 1, 0))
        hp_i = jnp.where(row == 0, jnp.broadcast_to(top_i, (T_SCAN, st)), pltpu.roll(h_i, 1, 0))
        dar = jnp.sum(gr * hp_r + gi * hp_i, axis=0, keepdims=True)
        dai = jnp.sum(gi * hp_r - gr * hp_i, axis=0, keepdims=True)
        dcr = lax.dot_general(dyb, h_r.astype(BF16), TN, preferred_element_type=F32)
        dci = lax.dot_general(dyb, h_i.astype(BF16), TN, preferred_element_type=F32)
        dbr = lax.dot_general(ub, grb, TN, preferred_element_type=F32)
        dbi = lax.dot_general(ub, gib, TN, preferred_element_type=F32)
        dd = jnp.sum(dyv * u, axis=0, keepdims=True)

        @pl.when(first)
        def _():
            dcr_ref[...] = dcr
            dci_ref[...] = dci
            dbr_ref[...] = dbr
            dbi_ref[...] = dbi
            dar_ref[...] = dar
            dai_ref[...] = dai
            dd_ref[...] = dd

        @pl.when(jnp.logical_not(first))
        def _():
            dcr_ref[...] += dcr
            dci_ref[...] += dci
            dbr_ref[...] += dbr
            dbi_ref[...] += dbi
            dar_ref[...] += dar
            dai_ref[...] += dai
            dd_ref[...] += dd

    row = lambda c, b, t: (b * nt + (nt - 1 - t), c)
    halo = lambda c, b, t: (jnp.maximum((b * nt + (nt - 1 - t)) * (T_SCAN // 8) - 1, 0), c)
    mat = pl.BlockSpec((None, CHUNK_CH, st), lambda c, b, t: (c, 0, 0))
    chan = pl.BlockSpec((T_SCAN, CHUNK_CH), row)
    state = pl.BlockSpec((T_SCAN, st), row)
    return _call(
        body, "s5_scan_bwd", (SSM_CHUNKS, n // seq, nt),
        [chan, chan, chan, state, state, pl.BlockSpec((8, st), halo), pl.BlockSpec((8, st), halo),
         mat, mat, mat, mat, pl.BlockSpec((1, CHUNK_CH), lambda c, b, t: (0, c)),
         pl.BlockSpec((16, 8, st), lambda c, b, t: (0, 0, c))],
        [chan, mat, mat, mat, mat, pl.BlockSpec((1, st), lambda c, b, t: (0, c)),
         pl.BlockSpec((1, st), lambda c, b, t: (0, c)), pl.BlockSpec((1, CHUNK_CH), lambda c, b, t: (0, c))],
        [_sds((n, d), F32)] + [_sds((SSM_CHUNKS, CHUNK_CH, st), F32)] * 4
        + [_sds((1, N_STATE), F32), _sds((1, N_STATE), F32), _sds((1, d), F32)],
        [pltpu.VMEM((T_SCAN, st), F32), pltpu.VMEM((T_SCAN, st), F32), pltpu.VMEM((1, st), F32),
         pltpu.VMEM((1, st), F32)], comm=comm,
    )(x, dy, dres, hr, hi, hr, hi, bbr, bbi, cr, ci, dvec, tab)


_GELU_C = math.sqrt(2.0 / math.pi)


def _gelu_parts(y):
    t = jnp.tanh(_GELU_C * (y + 0.044715 * (y * y * y)))
    return 0.5 * (1.0 + t), t


def _glu_fwd(y, w_glu, b_glu):
    n, d = y.shape

    def body(y_ref, w_ref, b_ref, y2_ref, z_ref, gg_ref):
        yv = y_ref[...]
        cdf, _ = _gelu_parts(yv)
        y2 = yv * cdf
        z = jnp.dot(y2.astype(BF16), w_ref[...], preferred_element_type=F32) + b_ref[...]
        y2_ref[...] = y2.astype(BF16)
        z_ref[...] = z
        gg_ref[...] = (y2 * jax.nn.sigmoid(z)).astype(BF16)

    row = pl.BlockSpec((TM, d), lambda i: (i, 0))
    return _call(body, "glu_fwd", (n // TM,),
                 [row, pl.BlockSpec((d, d), lambda i: (0, 0)), pl.BlockSpec((1, d), lambda i: (0, 0))],
                 [row, row, row], [_sds((n, d), BF16), _sds((n, d), F32), _sds((n, d), BF16)])(y, w_glu, b_glu)


def _glu_bwd_gate(y, z, dgg):
    n, d = y.shape

    def body(y_ref, z_ref, dgg_ref, dz_ref, db_ref, t_ref):
        i = pl.program_id(0)
        yv = y_ref[...]
        cdf, _ = _gelu_parts(yv)
        y2 = yv * cdf
        s = jax.nn.sigmoid(z_ref[...])
        dg = dgg_ref[...]
        dz = dg * y2 * s * (1.0 - s)
        dz_ref[...] = dz.astype(BF16)
        t_ref[...] = dg * s
        db = jnp.sum(dz, axis=0, keepdims=True)

        @pl.when(i == 0)
        def _():
            db_ref[...] = db

        @pl.when(i > 0)
        def _():
            db_ref[...] += db

    row = pl.BlockSpec((TM, d), lambda i: (i, 0))
    vec = pl.BlockSpec((1, d), lambda i: (0, 0))
    return _call(body, "glu_bwd_gate", (n // TM,), [row, row, row], [row, vec, row],
                 [_sds((n, d), BF16), _sds((1, d), F32), _sds((n, d), F32)])(y, z, dgg)


def _gelu_bwd(y, dy2):
    n, d = y.shape

    def body(y_ref, g_ref, o_ref):
        yv = y_ref[...]
        cdf, t = _gelu_parts(yv)
        dinner = _GELU_C * (1.0 + 3.0 * 0.044715 * yv * yv)
        o_ref[...] = g_ref[...] * (cdf + 0.5 * yv * (1.0 - t * t) * dinner)

    row = pl.BlockSpec((TM, d), lambda i: (i, 0))
    return _call(body, "gelu_bwd", (n // TM,), [row, row], row, _sds((n, d), F32))(y, dy2)


HALO = 16


def _shift_down(x, halo, k):
    out = pltpu.roll(x, k, 0)
    row = lax.broadcasted_iota(jnp.int32, x.shape, 0)
    for j in range(k):
        out = jnp.where(row == j, jnp.broadcast_to(halo[HALO - k + j:HALO - k + j + 1, :], x.shape), out)
    return out


def _conv3(x, halo, w, b):
    return b + w[0:1, :] * x + w[1:2, :] * _shift_down(x, halo, 1) + w[2:3, :] * _shift_down(x, halo, 2)


def _conv_gate_fwd(hc, conv_w, conv_b, seq):
    _, nb, n, c = hc.shape
    tiles_per_seq = seq // TM

    def body(x_ref, halo_ref, w_ref, b_ref, o_ref):
        i = pl.program_id(1)
        start = (i % tiles_per_seq) == 0
        cv = []
        for h in range(2):
            halo = jnp.where(start, 0.0, halo_ref[h].astype(F32))
            cv.append(_conv3(x_ref[h].astype(F32), halo, w_ref[h], b_ref[h]))
        o_ref[...] = (cv[1] * jax.nn.sigmoid(cv[1]) * cv[0]).astype(BF16)

    return _call(
        body, "conv_gate_fwd", (nb, n // TM),
        [pl.BlockSpec((2, None, TM, c), lambda d, i: (0, d, i, 0)),
         pl.BlockSpec((2, None, HALO, c), lambda d, i: (0, d, jnp.maximum(i * (TM // HALO) - 1, 0), 0)),
         pl.BlockSpec((2, None, 3, c), lambda d, i: (0, d, 0, 0)),
         pl.BlockSpec((2, None, 1, c), lambda d, i: (0, d, 0, 0))],
        pl.BlockSpec((None, TM, c), lambda d, i: (d, i, 0)),
        _sds((nb, n, c), BF16))(hc, hc, conv_w, conv_b)


def _conv_gate_bwd(hc, dact, conv_w, conv_b, seq, comm=None):
    _, nb, n, c = hc.shape
    tiles_per_seq = seq // TM
    last_halo = n // HALO - 1
    ext = TM + HALO

    def body(x_ref, prev_ref, next_ref, da_ref, dan_ref, w_ref, b_ref, dx_ref, dw_ref):
        i = pl.program_id(1)
        start = (i % tiles_per_seq) == 0
        end = (i % tiles_per_seq) == tiles_per_seq - 1
        row = lax.broadcasted_iota(jnp.int32, (ext, c), 0)
        xe, cv = [], []
        for h in range(2):
            halo = jnp.where(start, 0.0, prev_ref[h].astype(F32))
            x_ext = jnp.concatenate([x_ref[h], next_ref[h]], axis=0).astype(F32)
            xe.append((x_ext, halo))
            cv.append(_conv3(x_ext, halo, w_ref[h], b_ref[h]))
        da = jnp.concatenate([da_ref[...], dan_ref[...]], axis=0).astype(F32)
        da = jnp.where(jnp.logical_and(end, row >= TM), 0.0, da)
        sg = jax.nn.sigmoid(cv[1])
        silu = cv[1] * sg
        dcv = [da * silu, da * cv[0] * (sg + silu * (1.0 - sg))]
        for h in range(2):
            w = w_ref[h]
            g = dcv[h]
            dx = w[0:1, :] * g + w[1:2, :] * pltpu.roll(g, ext - 1, 0) + w[2:3, :] * pltpu.roll(g, ext - 2, 0)
            dx_ref[h] = dx[:TM, :].astype(BF16)
            x_ext, halo = xe[h]
            gt = g[:TM, :]
            xt = x_ext[:TM, :]
            parts = [jnp.sum(gt * xt, axis=0, keepdims=True),
                     jnp.sum(gt * _shift_down(xt, halo, 1), axis=0, keepdims=True),
                     jnp.sum(gt * _shift_down(xt, halo, 2), axis=0, keepdims=True),
                     jnp.sum(gt, axis=0, keepdims=True)]
            upd = jnp.concatenate(parts + [jnp.zeros((4, c), F32)], axis=0)

            @pl.when(i == 0)
            def _():
                dw_ref[h] = upd

            @pl.when(i > 0)
            def _():
                dw_ref[h] += upd

    nxt = lambda i: jnp.minimum((i + 1) * (TM // HALO), last_halo)
    return _call(
        body, "conv_gate_bwd", (nb, n // TM),
        [pl.BlockSpec((2, None, TM, c), lambda d, i: (0, d, i, 0)),
         pl.BlockSpec((2, None, HALO, c), lambda d, i: (0, d, jnp.maximum(i * (TM // HALO) - 1, 0), 0)),
         pl.BlockSpec((2, None, HALO, c), lambda d, i: (0, d, nxt(i), 0)),
         pl.BlockSpec((None, TM, c), lambda d, i: (d, i, 0)),
         pl.BlockSpec((None, HALO, c), lambda d, i: (d, nxt(i), 0)),
         pl.BlockSpec((2, None, 3, c), lambda d, i: (0, d, 0, 0)),
         pl.BlockSpec((2, None, 1, c), lambda d, i: (0, d, 0, 0))],
        [pl.BlockSpec((2, None, TM, c), lambda d, i: (0, d, i, 0)),
         pl.BlockSpec((2, None, 8, c), lambda d, i: (0, d, 0, 0))],
        [_sds((2, nb, n, c), BF16), _sds((2, nb, 8, c), F32)], comm=comm)(hc, hc, hc, dact, dact, conv_w, conv_b)


def _t5_bucket_np(dist):
    exact = REL_BUCKETS // 2
    d = np.maximum(dist, 1).astype(np.float32)
    large = exact + (np.log(d / exact) / math.log(REL_MAX_DIST / exact) * (REL_BUCKETS - exact)).astype(np.int64)
    large = np.minimum(large, REL_BUCKETS - 1)
    return np.where(dist < exact, dist, large).astype(np.int32)


def _bucket_table(dil):
    steps = np.arange(BAND)[:, None] + BAND - np.arange(2 * BAND)[None, :]
    return _t5_bucket_np(np.maximum(steps, 0) * dil).reshape(1, BAND * 2 * BAND).astype(np.float32)


def _bias_expand(rel_t, dil):
    nk = BAND * 2 * BAND
    bucket = jnp.asarray(_bucket_table(dil))

    def body(r_ref, bk_ref, o_ref):
        ids = lax.broadcasted_iota(jnp.int32, (REL_BUCKETS, nk), 0).astype(F32)
        onehot = (ids == bk_ref[...]).astype(F32)
        o_ref[...] = jnp.dot(r_ref[...], onehot, precision=HIGHEST, preferred_element_type=F32)

    return _call(body, "bias_expand", (1,),
                 [pl.BlockSpec((N_HEADS, REL_BUCKETS), lambda i: (0, 0)), pl.BlockSpec((1, nk), lambda i: (0, 0))],
                 pl.BlockSpec((N_HEADS, nk), lambda i: (0, 0)), _sds((N_HEADS, nk), F32))(rel_t, bucket)


def _bias_reduce(dbias, dil):
    nk = BAND * 2 * BAND
    bucket = jnp.asarray(_bucket_table(dil))

    def body(g_ref, bk_ref, o_ref):
        ids = lax.broadcasted_iota(jnp.int32, (REL_BUCKETS, nk), 0).astype(F32)
        onehot = (ids == bk_ref[...]).astype(F32)
        o_ref[...] = lax.dot_general(g_ref[...], onehot, NT, precision=HIGHEST, preferred_element_type=F32)

    return _call(body, "bias_reduce", (1,),
                 [pl.BlockSpec((N_HEADS, nk), lambda i: (0, 0)), pl.BlockSpec((1, nk), lambda i: (0, 0))],
                 pl.BlockSpec((N_HEADS, REL_BUCKETS), lambda i: (0, 0)),
                 _sds((N_HEADS, REL_BUCKETS), F32))(dbias, bucket)


def _valid_mask(n):
    qi = lax.broadcasted_iota(jnp.int32, (BAND, 2 * BAND), 0)
    kj = lax.broadcasted_iota(jnp.int32, (BAND, 2 * BAND), 1)
    steps = qi + BAND - kj
    in_band = jnp.logical_and(steps >= 0, steps <= BAND)
    return jnp.logical_and(in_band, jnp.logical_or(n > 0, kj >= BAND))


ATTN_COLS = {1: 1024, 4: 512, 16: 128}


def _residue(dil, r):
    return slice(None) if dil == 1 else pl.ds(r, BAND, stride=dil)


def _stack_heads(x, first):
    return jnp.concatenate([jnp.where(first, x, 0.0), jnp.where(first, 0.0, x)], axis=0)


def _attn_fwd(q, k, v, bias, g, dil, bsz, seq):
    n = q.shape[0]
    rows = BAND * dil
    nch = seq // rows
    w = N_HEADS * HEAD_DIM
    wc = ATTN_COLS[dil]
    ncol = w // wc
    hpc = wc // HEAD_DIM
    nj = wc // 128
    slab = lambda k: pltpu.VMEM((k * nj, rows, 128), F32)

    def body(q_ref, kc_ref, kp_ref, vc_ref, vp_ref, b_ref, o_ref, l_ref, qf, kf, vf, of, lf):
        ch = pl.program_id(2)
        valid = _valid_mask(ch)
        valid = jnp.concatenate([valid, valid], axis=0)
        lane = lax.broadcasted_iota(jnp.int32, (BAND, 128), 1)
        first = lane < HEAD_DIM
        for j in range(nj):
            sl = slice(128 * j, 128 * (j + 1))
            qf[j] = q_ref[:, sl].astype(F32)
            kf[j] = kp_ref[:, sl].astype(F32)
            kf[nj + j] = kc_ref[:, sl].astype(F32)
            vf[j] = vp_ref[:, sl].astype(F32)
            vf[nj + j] = vc_ref[:, sl].astype(F32)
        for r in range(dil):
            rr = _residue(dil, r)
            for j in range(nj):
                q2 = qf.at[j][rr, :]
                k2 = jnp.concatenate([kf.at[j][rr, :], kf.at[nj + j][rr, :]], axis=0).astype(BF16)
                v2 = jnp.concatenate([vf.at[j][rr, :], vf.at[nj + j][rr, :]], axis=0).astype(BF16)
                qs = _stack_heads(q2, first).astype(BF16)
                s = lax.dot_general(qs, k2, NT, preferred_element_type=F32) * (HEAD_DIM ** -0.5)
                s = jnp.where(valid, s + b_ref[2 * j:2 * j + 2].reshape(2 * BAND, 2 * BAND), NEG_BIG)
                m = jnp.max(s, axis=1, keepdims=True)
                p = jnp.exp(s - m)
                l = jnp.sum(p, axis=1, keepdims=True)
                pv = jnp.dot(p.astype(BF16), v2, preferred_element_type=F32) / l
                lse = jnp.broadcast_to(m + jnp.log(l), (2 * BAND, 128))
                of.at[j][rr, :] = jnp.where(first, pv[:BAND], pv[BAND:])
                lf.at[j][rr, :] = jnp.where(first, lse[:BAND], lse[BAND:])
        for j in range(nj):
            sl = slice(128 * j, 128 * (j + 1))
            o_ref[:, sl] = of[j]
            l_ref[:, sl] = lf[j]

    base = g * ncol
    cur = lambda c, b, ch: (b * nch + ch, base + c)
    prev = lambda c, b, ch: (b * nch + jnp.maximum(ch - 1, 0), base + c)
    blk = lambda im: pl.BlockSpec((rows, wc), im)
    out_blk = pl.BlockSpec((rows, wc), lambda c, b, ch: (b * nch + ch, c))
    return _call(
        body, "attn_fwd_d%d" % dil, (ncol, bsz, nch),
        [blk(cur), blk(cur), blk(prev), blk(cur), blk(prev),
         pl.BlockSpec((hpc, BAND, 2 * BAND), lambda c, b, ch: (c, 0, 0))],
        [out_blk, out_blk],
        [_sds((n, w), F32)] * 2,
        [slab(1), slab(2), slab(2), slab(1), slab(1)],
    )(q, k, k, v, v, bias)


def _attn_combine(outs, lses):
    n, w = outs[0].shape

    def body(o0, o1, o2, l0, l1, l2, o_ref, l_ref):
        la, lb, lc = l0[...], l1[...], l2[...]
        m = jnp.maximum(jnp.maximum(la, lb), lc)
        wa, wb, wc = jnp.exp(la - m), jnp.exp(lb - m), jnp.exp(lc - m)
        tot = wa + wb + wc
        o_ref[...] = (wa * o0[...] + wb * o1[...] + wc * o2[...]) / tot
        l_ref[...] = m + jnp.log(tot)

    row = pl.BlockSpec((TM, w), lambda i: (i, 0))
    return _call(body, "attn_combine", (n // TM,), [row] * 6, [row, row],
                 [_sds((n, w), F32)] * 2)(*outs, *lses)


def _attn_bwd(q, k, v, do, o, lse, bias, g, dil, bsz, seq, comm=None):
    n = q.shape[0]
    rows = BAND * dil
    nch = seq // rows
    w = N_HEADS * HEAD_DIM
    wc = ATTN_COLS[dil]
    ncol = w // wc
    hpc = wc // HEAD_DIM
    nj = wc // 128
    slab = lambda k: pltpu.VMEM((k * nj, rows, 128), F32)
    scale = HEAD_DIM ** -0.5

    def body(q_ref, kc_ref, kp_ref, vc_ref, vp_ref, do_ref, o_ref, l_ref, b_ref,
             dq_ref, dk_ref, dv_ref, db_ref, qf, kf, vf, dof, of, lf, dqf, dkf, dvf):
        b = pl.program_id(1)
        ch = pl.program_id(2)
        cur = ch % 2
        prv = 1 - cur

        @pl.when(jnp.logical_and(b == 0, ch == 0))
        def _():
            db_ref[...] = jnp.zeros_like(db_ref)

        @pl.when(ch == 0)
        def _():
            for j in range(nj):
                dkf[nj + j] = jnp.zeros((rows, 128), F32)
                dvf[nj + j] = jnp.zeros((rows, 128), F32)

        @pl.when(ch < nch)
        def _():
            valid = _valid_mask(ch)
            valid = jnp.concatenate([valid, valid], axis=0)
            first = lax.broadcasted_iota(jnp.int32, (BAND, 128), 1) < HEAD_DIM
            for j in range(nj):
                sl = slice(128 * j, 128 * (j + 1))
                qf[j] = q_ref[:, sl].astype(F32)
                kf[j] = kp_ref[:, sl].astype(F32)
                kf[nj + j] = kc_ref[:, sl].astype(F32)
                vf[j] = vp_ref[:, sl].astype(F32)
                vf[nj + j] = vc_ref[:, sl].astype(F32)
                dof[j] = do_ref[:, sl]
                of[j] = o_ref[:, sl]
                lf[j] = l_ref[:, sl]
            for r in range(dil):
                rr = _residue(dil, r)
                for j in range(nj):
                    k2 = jnp.concatenate([kf.at[j][rr, :], kf.at[nj + j][rr, :]], axis=0).astype(BF16)
                    v2 = jnp.concatenate([vf.at[j][rr, :], vf.at[nj + j][rr, :]], axis=0).astype(BF16)
                    do2 = dof.at[j][rr, :]
                    lse2 = lf.at[j][rr, :]
                    qs = _stack_heads(qf.at[j][rr, :], first).astype(BF16)
                    dos = _stack_heads(do2, first)
                    delta = jnp.sum(dos * jnp.concatenate([of.at[j][rr, :]] * 2, axis=0), axis=1, keepdims=True)
                    dos = dos.astype(BF16)
                    lse_col = jnp.concatenate([lse2[:, 0:1], lse2[:, HEAD_DIM:HEAD_DIM + 1]], axis=0)
                    s = lax.dot_general(qs, k2, NT, preferred_element_type=F32) * scale
                    s = jnp.where(valid, s + b_ref[2 * j:2 * j + 2].reshape(2 * BAND, 2 * BAND), NEG_BIG)
                    p = jnp.exp(s - lse_col)
                    dp = lax.dot_general(dos, v2, NT, preferred_element_type=F32)
                    ds = p * (dp - delta)
                    db_ref[2 * j:2 * j + 2] += ds.reshape(2, BAND, 2 * BAND)
                    dsb = ds.astype(BF16)
                    dq2 = jnp.dot(dsb, k2, preferred_element_type=F32) * scale
                    dk2 = lax.dot_general(dsb, qs, TN, preferred_element_type=F32) * scale
                    dv2 = lax.dot_general(p.astype(BF16), dos, TN, preferred_element_type=F32)
                    dqf.at[j][rr, :] = jnp.where(first, dq2[:BAND], dq2[BAND:])
                    dkf.at[prv * nj + j][rr, :] += dk2[:BAND, :]
                    dvf.at[prv * nj + j][rr, :] += dv2[:BAND, :]
                    dkf.at[cur * nj + j][rr, :] = dk2[BAND:, :]
                    dvf.at[cur * nj + j][rr, :] = dv2[BAND:, :]
            for j in range(nj):
                dq_ref[:, 128 * j:128 * (j + 1)] = dqf[j].astype(BF16)

        for j in range(nj):
            sl = slice(128 * j, 128 * (j + 1))
            dk_ref[:, sl] = dkf[prv * nj + j].astype(BF16)
            dv_ref[:, sl] = dvf[prv * nj + j].astype(BF16)

    last = nch - 1
    base = g * ncol
    cur3 = lambda c, b, ch: (b * nch + jnp.minimum(ch, last), base + c)
    prev3 = lambda c, b, ch: (b * nch + jnp.maximum(jnp.minimum(ch, last) - 1, 0), base + c)
    cur1 = lambda c, b, ch: (b * nch + jnp.minimum(ch, last), c)
    lag1 = lambda c, b, ch: (b * nch + jnp.maximum(ch - 1, 0), c)
    blk = lambda im: pl.BlockSpec((rows, wc), im)
    bias_blk = pl.BlockSpec((hpc, BAND, 2 * BAND), lambda c, b, ch: (c, 0, 0))
    out = _sds((n, w), BF16)
    return _call(
        body, "attn_bwd_d%d" % dil, (ncol, bsz, nch + 1),
        [blk(cur3), blk(cur3), blk(prev3), blk(cur3), blk(prev3), blk(cur1), blk(cur1), blk(cur1), bias_blk],
        [blk(cur1), blk(lag1), blk(lag1), bias_blk],
        [out, out, out, _sds((N_HEADS, BAND, 2 * BAND), F32)],
        [slab(1), slab(2), slab(2), slab(1), slab(1), slab(1), slab(1), slab(2), slab(2)], comm=comm,
    )(q, k, k, v, v, do, o, lse, bias)


def _adamw(name, parts, w, m, v):
    nl, r, c = w.shape
    nparts = parts[0].shape[0]
    tr = r
    for cand in (256, 352, 128):
        if r % cand == 0 and r > cand:
            tr = cand
            break
    c1 = 1.0 - ADAM_B1 ** ADAM_STEP
    c2 = 1.0 - ADAM_B2 ** ADAM_STEP

    def body(*refs):
        p_refs = refs[:nl]
        w_ref, m_ref, v_ref, g_ref, d_ref, nm_ref, nv_ref = refs[nl:]
        layer = pl.program_id(0)
        for l in range(nl):
            @pl.when(layer == l)
            def _():
                g = p_refs[l][0].astype(F32)
                for dev in range(1, nparts):
                    g = g + p_refs[l][dev].astype(F32)
                nm = ADAM_B1 * m_ref[...] + (1.0 - ADAM_B1) * g
                nv = ADAM_B2 * v_ref[...] + (1.0 - ADAM_B2) * (g * g)
                m_hat = nm / c1
                v_hat = nv / c2
                g_ref[...] = g
                d_ref[...] = -ADAM_LR * (m_hat / (jnp.sqrt(v_hat) + ADAM_EPS) + ADAM_WD * w_ref[...])
                nm_ref[...] = nm
                nv_ref[...] = nv

    def part_spec(l):
        return pl.BlockSpec((nparts, tr, c), lambda layer, i: (0, jnp.where(layer == l, i, 0), 0))

    row = pl.BlockSpec((None, tr, c), lambda layer, i: (layer, i, 0))
    return _call(body, name, (nl, r // tr), [part_spec(l) for l in range(nl)] + [row, row, row],
                 [row] * 4, [_sds((nl, r, c), F32)] * 4)(*parts, w, m, v)


def _exchange(name, srcs, out_shapes, items):
    n_in = len(srcs)
    n_out = len(out_shapes)
    n_items = len(items)

    def body(*refs):
        ins = refs[:n_in]
        outs = refs[n_in:n_in + n_out]
        send_sems, recv_sems, local_sems = refs[n_in + n_out:]
        x, y, c = lax.axis_index("x"), lax.axis_index("y"), lax.axis_index("c")
        me = 4 * x + 2 * y + c

        def pick(ref, sel, peer):
            idx = tuple(peer if s == "peer" else me if s == "me" else s for s in sel)
            return ref.at[idx] if idx else ref

        copies = []
        for it, (si, ssel, oi, osel) in enumerate(items):
            local = pltpu.make_async_copy(pick(ins[si], ssel, me), pick(outs[oi], osel, me), local_sems.at[it])
            local.start()
            copies.append(local)
            for j in range(1, N_DEV):
                px = 1 - x if j & 4 else x
                py = 1 - y if j & 2 else y
                pc = 1 - c if j & 1 else c
                peer = 4 * px + 2 * py + pc
                sem = it * (N_DEV - 1) + j - 1
                cp = pltpu.make_async_remote_copy(
                    src_ref=pick(ins[si], ssel, peer), dst_ref=pick(outs[oi], osel, peer),
                    send_sem=send_sems.at[sem], recv_sem=recv_sems.at[sem],
                    device_id=(px, py, pc), device_id_type=pl.DeviceIdType.MESH)
                cp.start()
                copies.append(cp)
        for cp in copies:
            cp.wait()

    any_spec = pl.BlockSpec(memory_space=pl.ANY)
    return pl.pallas_call(
        body, name=name, in_specs=[any_spec] * n_in, out_specs=[any_spec] * n_out, out_shape=list(out_shapes),
        scratch_shapes=[pltpu.SemaphoreType.DMA((n_items * (N_DEV - 1),)),
                        pltpu.SemaphoreType.DMA((n_items * (N_DEV - 1),)),
                        pltpu.SemaphoreType.DMA((n_items,))],
        compiler_params=pltpu.CompilerParams(has_side_effects=True),
    )(*srcs)


def _sel(ref, sel, slot=None):
    idx = tuple(slot if s == "slot" else s for s in sel)
    return ref.at[idx] if idx else ref


def _gather_plan(srcs, out_shapes, items):
    per = N_DEV - 1
    n_items = len(items)

    def phases(ins, outs, send_sems, recv_sems, local_sems, total):
        x, y, c = lax.axis_index("x"), lax.axis_index("y"), lax.axis_index("c")
        sibling = (x, y, 1 - c)
        chips = [(1 - x, y), (x, 1 - y), (1 - x, 1 - y)]
        slot_of = lambda px, py, pc: 4 * px + 2 * py + pc

        def copy(it, k, block, to, src=None):
            _, _, oi, osel = items[it]
            dst = _sel(outs[oi], osel, slot_of(*block))
            return pltpu.make_async_remote_copy(
                src_ref=dst if src is None else src, dst_ref=dst,
                send_sem=send_sems.at[it * per + k], recv_sem=recv_sems.at[it * per + k],
                device_id=to, device_id_type=pl.DeviceIdType.MESH)

        def own_copies(it):
            si, ssel, oi, osel = items[it]
            src = _sel(ins[si], ssel)
            mine = pltpu.make_async_copy(src, _sel(outs[oi], osel, slot_of(x, y, c)), local_sems.at[it])
            first = [copy(it, 0, (x, y, c), sibling, src=src)]
            first += [copy(it, 1 + j, (x, y, c), (*chip, c), src=src) for j, chip in enumerate(chips)]
            return mine, first

        def start():
            for it in range(n_items):
                mine, first = own_copies(it)
                mine.start()
                for cp in first:
                    cp.start()

        def forward(it):
            def go():
                for j, chip in enumerate(chips):
                    copy(it, 1 + j, (*chip, c), (x, y, c)).wait_recv()
                    copy(it, 4 + j, (*chip, c), sibling).start()
            return go

        def finish():
            for it in range(n_items):
                copy(it, 0, sibling, (x, y, c)).wait_recv()
                for j, chip in enumerate(chips):
                    copy(it, 4 + j, (*chip, 1 - c), (x, y, c)).wait_recv()
            for it in range(n_items):
                mine, first = own_copies(it)
                for cp in first:
                    cp.wait_send()
                for j, chip in enumerate(chips):
                    copy(it, 4 + j, (*chip, c), sibling).wait_send()
                mine.wait()

        lo = max(total // 3, 1)
        acts = [(0, start)]
        for it in range(n_items):
            acts.append((min(lo + it * (total - 1 - lo) // n_items, total - 2), forward(it)))
        acts.append((total - 1, finish))
        return acts

    return _Plan(srcs, out_shapes, n_items * per, n_items, phases)


def _scatter_plan(srcs, out_shapes, items):
    per = N_DEV - 1
    n_items = len(items)

    def phases(ins, outs, send_sems, recv_sems, local_sems, total):
        x, y, c = lax.axis_index("x"), lax.axis_index("y"), lax.axis_index("c")
        me = 4 * x + 2 * y + c

        def copies():
            out = []
            for it, (si, oi, osel) in enumerate(items):
                dst = _sel(outs[oi], osel, me)
                out.append(pltpu.make_async_copy(ins[si].at[me], dst, local_sems.at[it]))
                for j in range(1, N_DEV):
                    px = 1 - x if j & 4 else x
                    py = 1 - y if j & 2 else y
                    pc = 1 - c if j & 1 else c
                    out.append(pltpu.make_async_remote_copy(
                        src_ref=ins[si].at[4 * px + 2 * py + pc], dst_ref=dst,
                        send_sem=send_sems.at[it * per + j - 1], recv_sem=recv_sems.at[it * per + j - 1],
                        device_id=(px, py, pc), device_id_type=pl.DeviceIdType.MESH))
            return out

        def start():
            for cp in copies():
                cp.start()

        def finish():
            for cp in copies():
                cp.wait()

        return [(0, start), (total - 1, finish)]

    return _Plan(srcs, out_shapes, n_items * per, n_items, phases)


def _sum_slots(parts):
    _, r, c = parts.shape

    def body(p_ref, o_ref):
        acc = p_ref[0]
        for dev in range(1, N_DEV):
            acc = acc + p_ref[dev]
        o_ref[...] = acc

    return _call(body, "sum_slots", (1,), [pl.BlockSpec((N_DEV, r, c), lambda i: (0, 0, 0))],
                 pl.BlockSpec((r, c), lambda i: (0, 0)), _sds((r, c), F32))(parts)


def _pack(arrays):
    rows = []
    for a in arrays:
        flat = a.reshape(-1).astype(F32)
        pad = (-flat.shape[0]) % 1024
        if pad:
            flat = jnp.concatenate([flat, jnp.zeros((pad,), F32)])
        rows.append(flat.reshape(-1, 128))
    return jnp.concatenate(rows, axis=0)


def _unpack(buf, shapes):
    out = []
    r0 = 0
    for shp in shapes:
        size = int(np.prod(shp))
        nrows = -(-size // 1024) * 8
        out.append(buf[r0:r0 + nrows].reshape(-1)[:size].reshape(shp))
        r0 += nrows
    return out


def _block_diag_chunks(val):
    eye = jnp.eye(8, dtype=val.dtype)
    return jnp.einsum("cjhp,jk->cjhkp", val, eye).reshape(SSM_CHUNKS, CHUNK_CH, CHUNK_ST)


def _diag_blocks(dense):
    d5 = dense.reshape(SSM_CHUNKS, 8, SSM_GROUP, 8, SSM_STATE)
    return jnp.stack([d5[:, j, :, j, :] for j in range(8)], axis=1)


def _ffn_fwd(tag, hin, hin_b, w_up, conv_w, conv_b, w_down4, gain, bias, seq, comm=None):
    hc = _mm_nn("ffn_up" + tag, hin_b, w_up, FFN_HIDDEN, blocked_out=True, comm=comm)
    couts = None
    if comm is not None:
        hc, couts = hc
    nb, n, c = hc.shape
    hc = hc.reshape(2, nb // 2, n, c)
    act = _conv_gate_fwd(hc, conv_w, conv_b, seq)
    ffn = _mm_nn_red("ffn_down" + tag, act, w_down4, F32)
    hout, hout_b = _ln_fwd("ffn_norm" + tag, hin, ffn, gain, bias)
    return hout, hout_b, (hin, hin_b, hc, act, ffn), couts


def _ffn_bwd(tag, saved, dh, w_up, conv_w, conv_b, w_down4, gain, seq, comm_conv=None, comm_dx=None):
    hin, hin_b, hc, act, ffn = saved
    dr, dr_b, dgain, dbias = _ln_bwd("ffn_norm_bwd" + tag, hin, ffn, dh, gain)
    d_wdown = _mm_tn("ffn_down_dw" + tag, act, dr_b, act.shape[0], BF16, a_blocked=True, g_mode="shared")
    dact = _mm_nt_out("ffn_down_dx" + tag, dr_b, w_down4, FFN_HIDDEN)
    res = _conv_gate_bwd(hc, dact, conv_w, conv_b, seq, comm=comm_conv)
    couts_conv = None
    if comm_conv is not None:
        res, couts_conv = res
    dhc, dconv = res
    dhc8 = dhc.reshape(2 * dhc.shape[1], dhc.shape[2], dhc.shape[3])
    d_wup = _mm_tn("ffn_up_dw" + tag, hin_b, dhc8, dhc8.shape[0], BF16, g_mode="blocked")
    dhin = _mm_nt_red("ffn_up_dx" + tag, dhc8, w_up, F32, g_blocked=True, add=dr, add_scale=DN_ALPHA,
                      comm=comm_dx)
    couts_dx = None
    if comm_dx is not None:
        dhin, couts_dx = dhin
    return dhin, d_wup, d_wdown, dconv, dgain, dbias, couts_conv, couts_dx


def _local_step(x, target, p, wfull, bsz, seq, ex=None):
    n = bsz * seq
    xf = x.reshape(n, D_MODEL)
    tf = target.reshape(n, D_MODEL)

    lr = p["s5_lam_re"].reshape(1, N_STATE)
    li = p["s5_lam_im"].reshape(1, N_STATE)
    ldt = p["s5_log_dt"].reshape(1, D_MODEL // SSM_GROUP)
    br_t = p["s5_b_re"].reshape(N_STATE, SSM_GROUP).T
    bi_t = p["s5_b_im"].reshape(N_STATE, SSM_GROUP).T
    ab_r, ab_i, bb_r, bb_i, tab = _s5_prep(lr, li, ldt, br_t, bi_t)

    def bb_chunks(bb):
        return _block_diag_chunks(bb.reshape(SSM_GROUP, SSM_CHUNKS, 8, SSM_STATE).transpose(1, 2, 0, 3))

    def c_chunks(cc):
        return _block_diag_chunks(cc.reshape(SSM_CHUNKS, 8, SSM_GROUP, SSM_STATE))

    bbr_c, bbi_c = bb_chunks(bb_r).astype(BF16), bb_chunks(bb_i).astype(BF16)
    cr_c, ci_c = c_chunks(p["s5_c_re"]).astype(BF16), c_chunks(p["s5_c_im"]).astype(BF16)
    dvec = p["s5_d"].reshape(1, D_MODEL)

    if ex is None:
        y, h_r, h_i = _s5_scan_fwd(xf, bbr_c, bbi_c, cr_c, ci_c, dvec, tab, seq)
    else:
        (y, h_r, h_i), gathered = _s5_scan_fwd(xf, bbr_c, bbi_c, cr_c, ci_c, dvec, tab, seq,
                                               comm=ex.gather_first())
        p, wfull = ex.take_first(gathered, p)
    ln_g = p["ln_gain"].reshape(4, 1, D_MODEL)
    ln_b = p["ln_bias"].reshape(4, 1, D_MODEL)
    y2, z, gg = _glu_fwd(y, wfull["s5_w_glu"], p["s5_b_glu"].reshape(1, D_MODEL))
    mix = _mm_nn("s5_out", gg, wfull["s5_w_out"][None], F32)
    h1, h1b = _ln_fwd("s5_norm", xf, mix, ln_g[0], ln_b[0])
    h2, h2b, ffn0_saved, gathered = _ffn_fwd("0", h1, h1b, wfull["ffn_w_up"][0], p["conv_w"][0], p["conv_b"][0],
                                             wfull["ffn_w_down"][0], ln_g[1], ln_b[1], seq,
                                             comm=ex.gather_second() if ex else None)
    if ex is not None:
        wfull = ex.take_second(gathered, wfull)

    kmat = _mm_nn("attn_k", h2b, wfull["attn_w_kv"], BF16, blocks=(0, 4))
    vmat = _mm_nn("attn_v", h2b, wfull["attn_w_kv"], BF16, blocks=(4, 4))
    qmat = _mm_nn("attn_q", h2b, wfull["attn_w_q"], BF16)
    biases, outs, lses = [], [], []
    for g, dil in enumerate(DILATIONS):
        rel_t = p["rel_bias"][:, g * N_HEADS:(g + 1) * N_HEADS].T
        biases.append(_bias_expand(rel_t, dil).reshape(N_HEADS, BAND, 2 * BAND))
        o_g, l_g = _attn_fwd(qmat, kmat, vmat, biases[g], g, dil, bsz, seq)
        outs.append(o_g)
        lses.append(l_g)
    o, lse = _attn_combine(outs, lses)
    attn_out = _mm_nn("attn_out", o, wfull["attn_w_out"][None], F32)
    h3, h3b = _ln_fwd("attn_norm", h2, attn_out, ln_g[2], ln_b[2])
    h4, _, ffn1_saved, _ = _ffn_fwd("1", h3, h3b, wfull["ffn_w_up"][1], p["conv_w"][1], p["conv_b"][1],
                                    wfull["ffn_w_down"][1], ln_g[3], ln_b[3], seq)
    loss_tile, dh4 = _loss_head(h4, tf)

    grads = {}
    dh3, grads["ffn_w_up1"], grads["ffn_w_down1"], dconv1, dg3, db3, _, _ = _ffn_bwd(
        "1", ffn1_saved, dh4, wfull["ffn_w_up"][1], p["conv_w"][1], p["conv_b"][1], wfull["ffn_w_down"][1],
        ln_g[3], seq)
    dr3, dr3b, dg2, db2 = _ln_bwd("attn_norm_bwd", h2, attn_out, dh3, ln_g[2])
    grads["attn_w_out"] = _mm_tn("attn_out_dw", o, dr3b, 1, BF16, g_mode="shared")
    do = _mm_nt_red("attn_out_dx", dr3b, wfull["attn_w_out"][None], F32, g_blocked=False)
    dqs, dks, dvs, drel = [], [], [], []
    for g, dil in enumerate(DILATIONS):
        plan = ex.scatter(grads, ("ffn_w_up1", "ffn_w_down1")) if ex and dil == DILATIONS[-1] else None
        res = _attn_bwd(qmat, kmat, vmat, do, o, lse, biases[g], g, dil, bsz, seq, comm=plan)
        if plan is not None:
            res, got = res
            ex.keep(plan, got)
        dq_g, dk_g, dv_g, db_g = res
        dqs.append(dq_g)
        dks.append(dk_g)
        dvs.append(dv_g)
        drel.append(_bias_reduce(db_g.reshape(N_HEADS, BAND * 2 * BAND), dil).T)
    dq = jnp.concatenate(dqs, axis=1)
    dk = jnp.concatenate(dks, axis=1)
    dv = jnp.concatenate(dvs, axis=1)
    grads["rel_bias"] = jnp.concatenate(drel, axis=1)
    grads["attn_w_q"] = _mm_tn("attn_q_dw", h2b, dq, N_DEV, BF16)
    grads["attn_w_kv"] = jnp.concatenate([_mm_tn("attn_k_dw", h2b, dk, 4, BF16),
                                          _mm_tn("attn_v_dw", h2b, dv, 4, BF16)], axis=0)
    dh2 = _mm_nt_red("attn_q_dx", dq, wfull["attn_w_q"], F32, g_blocked=False, add=dr3, add_scale=DN_ALPHA)
    dh2 = _mm_nt_red("attn_k_dx", dk, wfull["attn_w_kv"], F32, g_blocked=False, add=dh2, blocks=(0, 4))
    dh2 = _mm_nt_red("attn_v_dx", dv, wfull["attn_w_kv"], F32, g_blocked=False, add=dh2, blocks=(4, 4))

    plan_conv = ex.scatter(grads, ("attn_w_q", "attn_w_out")) if ex else None
    plan_dx = ex.scatter(grads, ("attn_w_kv",)) if ex else None
    dh1, grads["ffn_w_up0"], grads["ffn_w_down0"], dconv0, dg1, db1, got_conv, got_dx = _ffn_bwd(
        "0", ffn0_saved, dh2, wfull["ffn_w_up"][0], p["conv_w"][0], p["conv_b"][0], wfull["ffn_w_down"][0],
        ln_g[1], seq, comm_conv=plan_conv, comm_dx=plan_dx)
    if ex is not None:
        ex.keep(plan_conv, got_conv)
        ex.keep(plan_dx, got_dx)
    dr1, dr1b, dg0, db0 = _ln_bwd("s5_norm_bwd", xf, mix, dh1, ln_g[0])
    grads["s5_w_out"] = _mm_tn("s5_out_dw", gg, dr1b, 1, BF16, g_mode="shared")
    dgg = _mm_nt_red("s5_out_dx", dr1b, wfull["s5_w_out"][None], F32, g_blocked=False)
    dz, d_bglu, dy2_direct = _glu_bwd_gate(y, z, dgg)
    grads["s5_w_glu"] = _mm_tn("s5_glu_dw", y2, dz, 1, BF16, g_mode="shared")
    dy2 = _mm_nt_red("s5_glu_dx", dz, wfull["s5_w_glu"][None], F32, g_blocked=False, add=dy2_direct)
    dy = _gelu_bwd(y, dy2)
    plan_last = ex.scatter(grads, ("ffn_w_up0", "ffn_w_down0", "s5_w_out", "s5_w_glu")) if ex else None
    res = _s5_scan_bwd(xf, dy, dr1, h_r, h_i, bbr_c, bbi_c, cr_c, ci_c, dvec, tab, seq, comm=plan_last)
    if ex is not None:
        res, got = res
        ex.keep(plan_last, got)
    dx, dcr, dci, dbr, dbi, dar, dai, dd = res

    def bb_from_chunks(dense):
        return _diag_blocks(dense).transpose(2, 0, 1, 3).reshape(SSM_GROUP, N_STATE)

    d_lr, d_li, d_ldt, d_br, d_bi = _s5_prep_bwd(lr, li, ldt, br_t, bi_t, dar, dai,
                                                 bb_from_chunks(dbr), bb_from_chunks(dbi))
    grads["s5_lam_re"] = d_lr.reshape(p["s5_lam_re"].shape)
    grads["s5_lam_im"] = d_li.reshape(p["s5_lam_im"].shape)
    grads["s5_log_dt"] = d_ldt.reshape(p["s5_log_dt"].shape)
    grads["s5_b_re"] = d_br.T.reshape(p["s5_b_re"].shape)
    grads["s5_b_im"] = d_bi.T.reshape(p["s5_b_im"].shape)
    grads["s5_c_re"] = _diag_blocks(dcr).reshape(p["s5_c_re"].shape)
    grads["s5_c_im"] = -_diag_blocks(dci).reshape(p["s5_c_im"].shape)
    grads["s5_d"] = dd.reshape(p["s5_d"].shape)
    grads["s5_b_glu"] = d_bglu.reshape(1, D_MODEL)
    dconv = jnp.stack([dconv0, dconv1]).reshape(2, N_DEV, 8, -1)
    grads["ffn_conv_w"] = dconv[:, :, 0:3, :].transpose(0, 2, 1, 3)
    grads["ffn_conv_b"] = dconv[:, :, 3, :]
    grads["ln_gain"] = jnp.stack([dg0, dg1, dg2, dg3]).reshape(2, 2, D_MODEL)
    grads["ln_bias"] = jnp.stack([db0, db1, db2, db3]).reshape(2, 2, D_MODEL)
    return loss_tile[0, 0], dx.reshape(x.shape), grads


SMALL_REPLICATED = ("s5_lam_re", "s5_lam_im", "s5_log_dt", "s5_b_re", "s5_b_im", "s5_c_re", "s5_c_im", "s5_d",
                    "rel_bias", "ffn_conv_b")
SMALL_SHARDED = ("s5_b_glu", "ffn_conv_w", "ln_gain", "ln_bias")
BIG = ("s5_w_glu", "s5_w_out", "attn_w_kv", "attn_w_q", "attn_w_out", "ffn_w_up", "ffn_w_down")
WEIGHTS = ("s5_lam_re", "s5_lam_im", "s5_log_dt", "s5_b_re", "s5_b_im", "s5_c_re", "s5_c_im", "s5_d", "s5_w_glu",
           "s5_b_glu", "s5_w_out", "attn_w_kv", "attn_w_q", "attn_w_out", "rel_bias", "ffn_w_up", "ffn_conv_w",
           "ffn_conv_b", "ffn_w_down", "ln_gain", "ln_bias")


class _Exchanges:
    def __init__(self, w):
        self.w = w
        self.parts = {}
        self.up = w["ffn_w_up"].astype(BF16)
        self.down = w["ffn_w_down"].astype(BF16)

    def gather_first(self):
        w = self.w
        srcs = [w["s5_w_glu"][0].astype(BF16), w["s5_w_out"][0].astype(BF16),
                _pack([w[k] for k in SMALL_SHARDED]), self.up, self.down, w["attn_w_kv"].astype(BF16),
                w["attn_w_q"][0].astype(BF16), w["attn_w_out"][0].astype(BF16)]
        outs = [_sds((N_DEV,) + (s.shape[1:] if i in (3, 4) else s.shape), s.dtype) for i, s in enumerate(srcs)]
        items = [(i, (0,) if i in (3, 4) else (), i, ("slot",)) for i in range(len(srcs))]
        return _gather_plan(srcs, outs, items)

    def take_first(self, g, p):
        d = D_MODEL
        w = self.w
        wfull = {
            "s5_w_glu": g[0].reshape(d, d),
            "s5_w_out": g[1].reshape(d, d),
            "ffn_w_up": [g[3], None],
            "ffn_w_down": [g[4].reshape(4, -1, d), None],
            "attn_w_kv": g[5],
            "attn_w_q": g[6],
            "attn_w_out": g[7].reshape(d, d),
        }
        smalls = [_unpack(g[2][dev], [w[k].shape for k in SMALL_SHARDED]) for dev in range(N_DEV)]
        c_ff = w["ffn_conv_w"].shape[2]
        conv_w = jnp.stack([s[1] for s in smalls], axis=2)
        p = dict(p)
        p.update(s5_b_glu=jnp.concatenate([s[0] for s in smalls], axis=1),
                 ln_gain=jnp.concatenate([s[2] for s in smalls], axis=2),
                 ln_bias=jnp.concatenate([s[3] for s in smalls], axis=2),
                 conv_w=conv_w.transpose(0, 2, 1, 3).reshape(2, 2, 4, 3, c_ff),
                 conv_b=w["ffn_conv_b"].reshape(2, 2, 4, 1, c_ff))
        return p, wfull

    def gather_second(self):
        srcs = [self.up, self.down]
        outs = [_sds((N_DEV,) + s.shape[1:], BF16) for s in srcs]
        return _gather_plan(srcs, outs, [(0, (1,), 0, ("slot",)), (1, (1,), 1, ("slot",))])

    def take_second(self, g, wfull):
        wfull = dict(wfull)
        wfull["ffn_w_up"] = [wfull["ffn_w_up"][0], g[0]]
        wfull["ffn_w_down"] = [wfull["ffn_w_down"][0], g[1].reshape(4, -1, D_MODEL)]
        return wfull

    def scatter(self, grads, names):
        srcs = [grads[k].reshape(N_DEV, -1, grads[k].shape[-1]) for k in names]
        plan = _scatter_plan(srcs, [_sds(s.shape, BF16) for s in srcs],
                             [(i, i, ("slot",)) for i in range(len(names))])
        plan.names = names
        return plan

    def keep(self, plan, outs):
        for k, o in zip(plan.names, outs):
            self.parts[k] = o


def kernel(x, s5_lam_re, s5_lam_im, s5_log_dt, s5_b_re, s5_b_im, s5_c_re, s5_c_im, s5_d, s5_w_glu, s5_b_glu, s5_w_out, attn_w_kv, attn_w_q, attn_w_out, rel_bias, ffn_w_up, ffn_conv_w, ffn_conv_b, ffn_w_down, ln_gain, ln_bias, loss_target, m_s5_lam_re, m_s5_lam_im, m_s5_log_dt, m_s5_b_re, m_s5_b_im, m_s5_c_re, m_s5_c_im, m_s5_d, m_s5_w_glu, m_s5_b_glu, m_s5_w_out, m_attn_w_kv, m_attn_w_q, m_attn_w_out, m_rel_bias, m_ffn_w_up, m_ffn_conv_w, m_ffn_conv_b, m_ffn_w_down, m_ln_gain, m_ln_bias, v_s5_lam_re, v_s5_lam_im, v_s5_log_dt, v_s5_b_re, v_s5_b_im, v_s5_c_re, v_s5_c_im, v_s5_d, v_s5_w_glu, v_s5_b_glu, v_s5_w_out, v_attn_w_kv, v_attn_w_q, v_attn_w_out, v_rel_bias, v_ffn_w_up, v_ffn_conv_w, v_ffn_conv_b, v_ffn_w_down, v_ln_gain, v_ln_bias):
    w = dict(s5_lam_re=s5_lam_re, s5_lam_im=s5_lam_im, s5_log_dt=s5_log_dt, s5_b_re=s5_b_re, s5_b_im=s5_b_im,
             s5_c_re=s5_c_re, s5_c_im=s5_c_im, s5_d=s5_d, s5_w_glu=s5_w_glu, s5_b_glu=s5_b_glu, s5_w_out=s5_w_out,
             attn_w_kv=attn_w_kv, attn_w_q=attn_w_q, attn_w_out=attn_w_out, rel_bias=rel_bias, ffn_w_up=ffn_w_up,
             ffn_conv_w=ffn_conv_w, ffn_conv_b=ffn_conv_b, ffn_w_down=ffn_w_down, ln_gain=ln_gain, ln_bias=ln_bias)
    mom = dict(s5_lam_re=m_s5_lam_re, s5_lam_im=m_s5_lam_im, s5_log_dt=m_s5_log_dt, s5_b_re=m_s5_b_re,
               s5_b_im=m_s5_b_im, s5_c_re=m_s5_c_re, s5_c_im=m_s5_c_im, s5_d=m_s5_d, s5_w_glu=m_s5_w_glu,
               s5_b_glu=m_s5_b_glu, s5_w_out=m_s5_w_out, attn_w_kv=m_attn_w_kv, attn_w_q=m_attn_w_q,
               attn_w_out=m_attn_w_out, rel_bias=m_rel_bias, ffn_w_up=m_ffn_w_up, ffn_conv_w=m_ffn_conv_w,
               ffn_conv_b=m_ffn_conv_b, ffn_w_down=m_ffn_w_down, ln_gain=m_ln_gain, ln_bias=m_ln_bias)
    var = dict(s5_lam_re=v_s5_lam_re, s5_lam_im=v_s5_lam_im, s5_log_dt=v_s5_log_dt, s5_b_re=v_s5_b_re,
               s5_b_im=v_s5_b_im, s5_c_re=v_s5_c_re, s5_c_im=v_s5_c_im, s5_d=v_s5_d, s5_w_glu=v_s5_w_glu,
               s5_b_glu=v_s5_b_glu, s5_w_out=v_s5_w_out, attn_w_kv=v_attn_w_kv, attn_w_q=v_attn_w_q,
               attn_w_out=v_attn_w_out, rel_bias=v_rel_bias, ffn_w_up=v_ffn_w_up, ffn_conv_w=v_ffn_conv_w,
               ffn_conv_b=v_ffn_conv_b, ffn_w_down=v_ffn_w_down, ln_gain=v_ln_gain, ln_bias=v_ln_bias)
    bsz, seq, _ = x.shape
    me = 4 * lax.axis_index("x") + 2 * lax.axis_index("y") + lax.axis_index("c")

    ex = _Exchanges(w)
    loss_part, grad_x, g = _local_step(x, loss_target, dict(w), None, bsz, seq, ex=ex)
    loss = lax.psum(loss_part, ("x", "y", "c"))

    d = D_MODEL
    results = {}
    for name in BIG:
        shard = w[name]
        if name in ("ffn_w_up", "ffn_w_down"):
            parts, r3 = [ex.parts[name + "0"], ex.parts[name + "1"]], shard.shape
        else:
            parts, r3 = [ex.parts[name]], (1, -1, shard.shape[-1])
        outs4 = _adamw("adamw_" + name, parts, shard.reshape(r3), mom[name].reshape(r3), var[name].reshape(r3))
        results[name] = tuple(o.reshape(shard.shape) for o in outs4)

    small_names = SMALL_REPLICATED + SMALL_SHARDED
    packed = _pack([g[k] for k in small_names])
    pad = (-packed.shape[0]) % (8 * N_DEV)
    if pad:
        packed = jnp.concatenate([packed, jnp.zeros((pad, 128), F32)], axis=0)
    piece = packed.shape[0] // N_DEV
    slots = _exchange("reduce_small", [packed.reshape(N_DEV, piece, 128)], [_sds((N_DEV, piece, 128), F32)],
                      [(0, ("peer",), 0, ("me",))])[0]
    summed = _exchange("gather_small", [_sum_slots(slots)], [_sds((N_DEV, piece, 128), F32)],
                       [(0, (), 0, ("me",))])[0]
    total = _unpack(summed.reshape(N_DEV * piece, 128), [g[k].shape for k in small_names])
    gsum = dict(zip(small_names, total))
    mine = {k: gsum[k] for k in SMALL_REPLICATED}
    mine["s5_b_glu"] = lax.dynamic_slice_in_dim(gsum["s5_b_glu"], me * (d // N_DEV), d // N_DEV, axis=1)
    mine["ffn_conv_w"] = lax.dynamic_index_in_dim(gsum["ffn_conv_w"], me, axis=2, keepdims=False)
    mine["ln_gain"] = lax.dynamic_slice_in_dim(gsum["ln_gain"], me * (d // N_DEV), d // N_DEV, axis=2)
    mine["ln_bias"] = lax.dynamic_slice_in_dim(gsum["ln_bias"], me * (d // N_DEV), d // N_DEV, axis=2)
    gp = _pack([mine[k] for k in small_names])
    outs4 = _adamw("adamw_small", [gp[None]], _pack([w[k] for k in small_names])[None],
                   _pack([mom[k] for k in small_names])[None], _pack([var[k] for k in small_names])[None])
    outs4 = [o[0] for o in outs4]
    shapes = [w[k].shape for k in small_names]
    unpacked = [_unpack(o, shapes) for o in outs4]
    for i, k in enumerate(small_names):
        results[k] = tuple(unpacked[j][i] for j in range(4))

    out = [loss, grad_x]
    for j in range(4):
        out.extend(results[k][j] for k in WEIGHTS)
    return tuple(out)
```

```python
import functools
import math

import numpy as np
import jax
import jax.numpy as jnp
from jax import lax
from jax.experimental import pallas as pl
from jax.experimental.pallas import tpu as pltpu

F32 = jnp.float32
BF16 = jnp.bfloat16
HIGHEST = lax.Precision.HIGHEST

N_DEV = 8
D_MODEL = 1024
SSM_GROUP = 16
SSM_STATE = 64
SSM_CHUNKS = 8
CHUNK_CH = D_MODEL // SSM_CHUNKS
CHUNK_ST = CHUNK_CH // SSM_GROUP * SSM_STATE
N_STATE = D_MODEL // SSM_GROUP * SSM_STATE
HEAD_DIM = 64
N_HEADS = 16
BAND = 128
DILATIONS = (1, 4, 16)
REL_BUCKETS = 32
REL_MAX_DIST = 2048
NEG_BIG = -1e30
DN_ALPHA = (2.0 * 2) ** 0.25
LN_EPS = 1e-5
ADAM_LR = 0.001
ADAM_B1 = 0.9
ADAM_B2 = 0.999
ADAM_EPS = 1e-08
ADAM_WD = 0.01
ADAM_STEP = 10

TM = 512
TM_MM = 1024
T_SCAN = 256
VMEM_LIMIT = 56 * 1024 * 1024
FFN_HIDDEN = BF16

NN = (((1,), (0,)), ((), ()))
NT = (((1,), (1,)), ((), ()))
TN = (((0,), (0,)), ((), ()))


class _Plan:
    def __init__(self, srcs, out_shapes, n_sems, n_local, phases):
        self.srcs, self.out_shapes, self.n_sems, self.n_local, self.phases = srcs, out_shapes, n_sems, n_local, phases


def _call(body, name, grid, in_specs, out_specs, out_shape, scratch=(), comm=None):
    params = dict(dimension_semantics=("arbitrary",) * len(grid), vmem_limit_bytes=VMEM_LIMIT)
    if comm is None:
        return pl.pallas_call(
            body, name=name, grid=grid, in_specs=in_specs, out_specs=out_specs, out_shape=out_shape,
            scratch_shapes=list(scratch), compiler_params=pltpu.CompilerParams(**params))
    single = not isinstance(out_specs, (list, tuple))
    o_specs = [out_specs] if single else list(out_specs)
    o_shape = [out_shape] if single else list(out_shape)
    n_in, n_out, n_scr = len(in_specs), len(o_specs), len(scratch)
    n_cin, n_cout = len(comm.srcs), len(comm.out_shapes)
    total = int(np.prod(grid))

    def wrapped(*refs):
        ins = refs[:n_in]
        cins = refs[n_in:n_in + n_cin]
        outs = refs[n_in + n_cin:n_in + n_cin + n_out]
        couts = refs[n_in + n_cin + n_out:n_in + n_cin + n_out + n_cout]
        rest = refs[n_in + n_cin + n_out + n_cout:]
        scr, sems = rest[:n_scr], rest[n_scr:]
        step = pl.program_id(0)
        for ax in range(1, len(grid)):
            step = step * grid[ax] + pl.program_id(ax)
        phases = comm.phases(cins, couts, *sems, total)
        for at, action in phases:
            if at == 0:
                pl.when(step == 0)(action)
        body(*ins, *outs, *scr)
        for at, action in phases:
            if at > 0:
                pl.when(step == at)(action)

    any_spec = pl.BlockSpec(memory_space=pl.ANY)
    call = pl.pallas_call(
        wrapped, name=name, grid=grid, in_specs=list(in_specs) + [any_spec] * n_cin,
        out_specs=o_specs + [any_spec] * n_cout, out_shape=o_shape + list(comm.out_shapes),
        scratch_shapes=list(scratch) + [pltpu.SemaphoreType.DMA((comm.n_sems,)),
                                        pltpu.SemaphoreType.DMA((comm.n_sems,)),
                                        pltpu.SemaphoreType.DMA((max(comm.n_local, 1),))],
        compiler_params=pltpu.CompilerParams(has_side_effects=True, **params))

    def run(*args):
        res = call(*args, *comm.srcs)
        own = res[0] if single else res[:n_out]
        return own, res[n_out:]

    return run


def _sds(shape, dtype):
    return jax.ShapeDtypeStruct(shape, dtype)


def _mm(name, a, b, *, dims, grid, a_spec, b_spec, o_spec, out_shape, nred=1, red_axis=0,
        add=None, add_spec=None, add_scale=1.0, comm=None):
    has_add = add is not None
    acc_shape = tuple(s for s in o_spec.block_shape if s is not None)

    def body(*refs):
        if has_add:
            a_ref, b_ref, add_ref, o_ref = refs[:4]
            rest = refs[4:]
        else:
            a_ref, b_ref, o_ref = refs[:3]
            rest = refs[3:]
        prod = lax.dot_general(a_ref[...].astype(BF16), b_ref[...].astype(BF16), dims,
                               preferred_element_type=F32)

        def finish(val):
            if has_add:
                val = val + add_scale * add_ref[...].astype(F32)
            o_ref[...] = val.astype(o_ref.dtype)

        if nred == 1:
            finish(prod)
        else:
            acc = rest[0]
            k = pl.program_id(red_axis)

            @pl.when(k == 0)
            def _():
                acc[...] = prod

            @pl.when(k > 0)
            def _():
                acc[...] += prod

            @pl.when(k == nred - 1)
            def _():
                finish(acc[...])

    in_specs = [a_spec, b_spec] + ([add_spec] if has_add else [])
    scratch = [pltpu.VMEM(acc_shape, F32)] if nred > 1 else []
    args = (a, b) + ((add,) if has_add else ())
    return _call(body, name, grid, in_specs, o_spec, out_shape, scratch, comm=comm)(*args)


def _mm_nn(name, a, b, out_dtype, blocked_out=False, comm=None, blocks=None):
    TM = TM_MM
    m, k = a.shape
    _, _, n = b.shape
    nb = b.shape[0] if blocks is None else blocks[1]
    first = 0 if blocks is None else blocks[0]
    if blocked_out:
        o_spec = pl.BlockSpec((None, TM, n), lambda d, i: (d, i, 0))
        out_shape = _sds((nb, m, n), out_dtype)
    else:
        o_spec = pl.BlockSpec((TM, n), lambda d, i: (i, d))
        out_shape = _sds((m, nb * n), out_dtype)
    return _mm(name, a, b, dims=NN, grid=(nb, m // TM),
               a_spec=pl.BlockSpec((TM, k), lambda d, i: (i, 0)),
               b_spec=pl.BlockSpec((None, k, n), lambda d, i: (d + first, 0, 0)),
               o_spec=o_spec, out_shape=out_shape, comm=comm)


def _mm_nn_red(name, a, b, out_dtype):
    TM = TM_MM
    nb, m, k = a.shape
    n = b.shape[2]
    return _mm(name, a, b, dims=NN, grid=(m // TM, nb), nred=nb, red_axis=1,
               a_spec=pl.BlockSpec((None, TM, k), lambda i, d: (d, i, 0)),
               b_spec=pl.BlockSpec((None, k, n), lambda i, d: (d, 0, 0)),
               o_spec=pl.BlockSpec((TM, n), lambda i, d: (i, 0)),
               out_shape=_sds((m, n), out_dtype))


def _mm_tn(name, a, g, nb, out_dtype, a_blocked=False, g_mode="cols"):
    TM = TM_MM
    if a_blocked:
        _, m, ka = a.shape
        a_spec = pl.BlockSpec((None, TM, ka), lambda d, i: (d, i, 0))
    else:
        m, ka = a.shape
        a_spec = pl.BlockSpec((TM, ka), lambda d, i: (i, 0))
    if g_mode == "cols":
        n = g.shape[1] // nb
        g_spec = pl.BlockSpec((TM, n), lambda d, i: (i, d))
    elif g_mode == "blocked":
        n = g.shape[2]
        g_spec = pl.BlockSpec((None, TM, n), lambda d, i: (d, i, 0))
    else:
        n = g.shape[1]
        g_spec = pl.BlockSpec((TM, n), lambda d, i: (i, 0))
    nred = m // TM
    return _mm(name, a, g, dims=TN, grid=(nb, nred), nred=nred, red_axis=1,
               a_spec=a_spec, b_spec=g_spec,
               o_spec=pl.BlockSpec((None, ka, n), lambda d, i: (d, 0, 0)),
               out_shape=_sds((nb, ka, n), out_dtype))


def _mm_nt_red(name, g, w, out_dtype, g_blocked, add=None, add_scale=1.0, comm=None, blocks=None):
    TM = TM_MM
    _, k, n = w.shape
    nb = w.shape[0] if blocks is None else blocks[1]
    first = 0 if blocks is None else blocks[0]
    if g_blocked:
        m = g.shape[1]
        g_spec = pl.BlockSpec((None, TM, n), lambda i, d: (d, i, 0))
    else:
        m = g.shape[0]
        g_spec = pl.BlockSpec((TM, n), lambda i, d: (i, d))
    o_spec = pl.BlockSpec((TM, k), lambda i, d: (i, 0))
    return _mm(name, g, w, dims=NT, grid=(m // TM, nb), nred=nb, red_axis=1,
               a_spec=g_spec, b_spec=pl.BlockSpec((None, k, n), lambda i, d: (d + first, 0, 0)),
               o_spec=o_spec, out_shape=_sds((m, k), out_dtype),
               add=add, add_spec=o_spec if add is not None else None, add_scale=add_scale, comm=comm)


def _mm_nt_out(name, g, w, out_dtype):
    TM = TM_MM
    m, n = g.shape
    nb, k, _ = w.shape
    return _mm(name, g, w, dims=NT, grid=(nb, m // TM),
               a_spec=pl.BlockSpec((TM, n), lambda d, i: (i, 0)),
               b_spec=pl.BlockSpec((None, k, n), lambda d, i: (d, 0, 0)),
               o_spec=pl.BlockSpec((None, TM, k), lambda d, i: (d, i, 0)),
               out_shape=_sds((nb, m, k), out_dtype))


def _ln_stats(x, f):
    r = DN_ALPHA * x + f
    mu = jnp.mean(r, axis=-1, keepdims=True)
    xc = r - mu
    var = jnp.mean(xc * xc, axis=-1, keepdims=True)
    rstd = lax.rsqrt(var + LN_EPS)
    return xc * rstd, rstd


def _ln_fwd(name, xin, f, gain, bias):
    n, d = xin.shape

    def body(x_ref, f_ref, g_ref, b_ref, o_ref, ob_ref):
        xhat, _ = _ln_stats(x_ref[...], f_ref[...])
        out = xhat * g_ref[...] + b_ref[...]
        o_ref[...] = out
        ob_ref[...] = out.astype(BF16)

    row = pl.BlockSpec((TM, d), lambda i: (i, 0))
    vec = pl.BlockSpec((1, d), lambda i: (0, 0))
    return _call(body, name, (n // TM,), [row, row, vec, vec], [row, row],
                 [_sds((n, d), F32), _sds((n, d), BF16)])(xin, f, gain, bias)


def _ln_bwd(name, xin, f, dy, gain):
    n, d = xin.shape

    def body(x_ref, f_ref, dy_ref, g_ref, dr_ref, drb_ref, dg_ref, db_ref):
        i = pl.program_id(0)
        xhat, rstd = _ln_stats(x_ref[...], f_ref[...])
        dy_v = dy_ref[...]
        dxh = dy_v * g_ref[...]
        m1 = jnp.mean(dxh, axis=-1, keepdims=True)
        m2 = jnp.mean(dxh * xhat, axis=-1, keepdims=True)
        dr = rstd * (dxh - m1 - xhat * m2)
        dr_ref[...] = dr
        drb_ref[...] = dr.astype(BF16)
        dg = jnp.sum(dy_v * xhat, axis=0, keepdims=True)
        db = jnp.sum(dy_v, axis=0, keepdims=True)

        @pl.when(i == 0)
        def _():
            dg_ref[...] = dg
            db_ref[...] = db

        @pl.when(i > 0)
        def _():
            dg_ref[...] += dg
            db_ref[...] += db

    row = pl.BlockSpec((TM, d), lambda i: (i, 0))
    vec = pl.BlockSpec((1, d), lambda i: (0, 0))
    return _call(body, name, (n // TM,), [row, row, row, vec], [row, row, vec, vec],
                 [_sds((n, d), F32), _sds((n, d), BF16), _sds((1, d), F32), _sds((1, d), F32)])(xin, f, dy, gain)


def _loss_head(y, target):
    n, d = y.shape

    def body(y_ref, t_ref, l_ref, dy_ref):
        i = pl.program_id(0)
        e = y_ref[...] - t_ref[...]
        dy_ref[...] = e * (1.0 / d)
        part = jnp.sum(jnp.sum(e * e, axis=1, keepdims=True), axis=0, keepdims=True) * (0.5 / d)
        part = jnp.broadcast_to(part, (8, 128))

        @pl.when(i == 0)
        def _():
            l_ref[...] = part

        @pl.when(i > 0)
        def _():
            l_ref[...] += part

    row = pl.BlockSpec((TM, d), lambda i: (i, 0))
    return _call(body, "loss_head", (n // TM,), [row, row],
                 [pl.BlockSpec((8, 128), lambda i: (0, 0)), row],
                 [_sds((8, 128), F32), _sds((n, d), F32)])(y, target)


def _group_expand_matrix():
    e = np.zeros((D_MODEL // SSM_GROUP, N_STATE), np.float32)
    for g in range(D_MODEL // SSM_GROUP):
        e[g, g * SSM_STATE:(g + 1) * SSM_STATE] = 1.0
    return jnp.asarray(e)


def _discretise(lr, li, ldt, br, bi, expand):
    dt = jnp.dot(jnp.exp(ldt), expand, precision=HIGHEST, preferred_element_type=F32)
    mag = jnp.exp(lr * dt)
    th = li * dt
    ab_r = mag * jnp.cos(th)
    ab_i = mag * jnp.sin(th)
    den = lr * lr + li * li
    nr = ab_r - 1.0
    co_r = (nr * lr + ab_i * li) / den
    co_i = (ab_i * lr - nr * li) / den
    bb_r = co_r * br - co_i * bi
    bb_i = co_r * bi + co_i * br
    return ab_r, ab_i, bb_r, bb_i


def _cmul(ar, ai, br, bi):
    return ar * br - ai * bi, ar * bi + ai * br


def _s5_prep(lr, li, ldt, br, bi):
    s = N_STATE
    expand = _group_expand_matrix()

    def body(lr_ref, li_ref, ldt_ref, br_ref, bi_ref, e_ref, abr_ref, abi_ref, bbr_ref, bbi_ref, tab_ref):
        ab_r, ab_i, bb_r, bb_i = _discretise(lr_ref[...], li_ref[...], ldt_ref[...], br_ref[...],
                                             bi_ref[...], e_ref[...])
        abr_ref[...] = ab_r
        abi_ref[...] = ab_i
        bbr_ref[...] = bb_r
        bbi_ref[...] = bb_i
        row = lax.broadcasted_iota(jnp.int32, (8, s), 0)
        for base, sign in ((0, 1.0), (8, -1.0)):
            p = [None] * 9
            p[1] = (ab_r, sign * ab_i)
            p[2] = _cmul(*p[1], *p[1])
            p[3] = _cmul(*p[2], *p[1])
            p[4] = _cmul(*p[2], *p[2])
            p[5] = _cmul(*p[4], *p[1])
            p[6] = _cmul(*p[4], *p[2])
            p[7] = _cmul(*p[4], *p[3])
            p[8] = _cmul(*p[4], *p[4])
            for j, k in enumerate((1, 2, 4)):
                keep = (row >= k) if base == 0 else (row < 8 - k)
                tab_ref[base + 2 * j] = jnp.where(keep, jnp.broadcast_to(p[k][0], (8, s)), 0.0)
                tab_ref[base + 2 * j + 1] = jnp.where(keep, jnp.broadcast_to(p[k][1], (8, s)), 0.0)
            pw_r = jnp.zeros((8, s), F32)
            pw_i = jnp.zeros((8, s), F32)
            for i in range(8):
                e = i + 1 if base == 0 else 8 - i
                pw_r = jnp.where(row == i, jnp.broadcast_to(p[e][0], (8, s)), pw_r)
                pw_i = jnp.where(row == i, jnp.broadcast_to(p[e][1], (8, s)), pw_i)
            tab_ref[base + 6] = pw_r
            tab_ref[base + 7] = pw_i

    def full(shape):
        return pl.BlockSpec(shape, lambda i: (0,) * len(shape))

    g = D_MODEL // SSM_GROUP
    return _call(body, "s5_prep", (1,),
                 [full((1, s)), full((1, s)), full((1, g)), full((16, s)), full((16, s)), full((g, s))],
                 [full((1, s)), full((1, s)), full((16, s)), full((16, s)), full((16, 8, s))],
                 [_sds((1, s), F32), _sds((1, s), F32), _sds((16, s), F32), _sds((16, s), F32),
                  _sds((16, 8, s), F32)])(lr, li, ldt, br, bi, expand)


def _s5_prep_bwd(lr, li, ldt, br, bi, d_abr, d_abi, d_bbr, d_bbi):
    s = N_STATE
    g = D_MODEL // SSM_GROUP
    expand = _group_expand_matrix()

    def body(lr_ref, li_ref, ldt_ref, br_ref, bi_ref, e_ref, c1, c2, c3, c4, o1, o2, o3, o4, o5):
        e_val = e_ref[...]
        _, vjp = jax.vjp(lambda a, b, c, d, e: _discretise(a, b, c, d, e, e_val),
                         lr_ref[...], li_ref[...], ldt_ref[...], br_ref[...], bi_ref[...])
        grads = vjp((c1[...], c2[...], c3[...], c4[...]))
        for o, gr in zip((o1, o2, o3, o4, o5), grads):
            o[...] = gr

    def full(shape):
        return pl.BlockSpec(shape, lambda i: (0,) * len(shape))

    return _call(body, "s5_prep_bwd", (1,),
                 [full((1, s)), full((1, s)), full((1, g)), full((16, s)), full((16, s)), full((g, s)),
                  full((1, s)), full((1, s)), full((16, s)), full((16, s))],
                 [full((1, s)), full((1, s)), full((1, g)), full((16, s)), full((16, s))],
                 [_sds((1, s), F32), _sds((1, s), F32), _sds((1, g), F32), _sds((16, s), F32),
                  _sds((16, s), F32)])(lr, li, ldt, br, bi, expand, d_abr, d_abi, d_bbr, d_bbi)


def _scan8(vr, vi, tab_ref, base, reverse):
    for j, k in enumerate((1, 2, 4)):
        mr = tab_ref[base + 2 * j]
        mi = tab_ref[base + 2 * j + 1]
        shift = 8 - k if reverse else k
        sr = pltpu.roll(vr, shift, 0)
        si = pltpu.roll(vi, shift, 0)
        vr, vi = vr + mr * sr - mi * si, vi + mr * si + mi * sr
    return vr, vi


def _s5_scan_fwd(x, bbr, bbi, cr, ci, dvec, tab, seq, comm=None):
    n, d = x.shape
    nt = seq // T_SCAN
    nblk = T_SCAN // 8
    st = CHUNK_ST

    def body(x_ref, bbr_ref, bbi_ref, cr_ref, ci_ref, d_ref, tab_ref, y_ref, hr_ref, hi_ref, car_r, car_i):
        t = pl.program_id(2)

        @pl.when(t == 0)
        def _():
            car_r[...] = jnp.zeros_like(car_r)
            car_i[...] = jnp.zeros_like(car_i)

        u = x_ref[...]
        ub = u.astype(BF16)
        hr_ref[...] = jnp.dot(ub, bbr_ref[...], preferred_element_type=F32)
        hi_ref[...] = jnp.dot(ub, bbi_ref[...], preferred_element_type=F32)

        def step(i, carry):
            sl = pl.ds(pl.multiple_of(i * 8, 8), 8)
            vr, vi = _scan8(hr_ref[sl, :], hi_ref[sl, :], tab_ref, 0, False)
            c_r = jnp.broadcast_to(car_r[...], (8, st))
            c_i = jnp.broadcast_to(car_i[...], (8, st))
            pr = tab_ref[6]
            pi = tab_ref[7]
            vr, vi = vr + pr * c_r - pi * c_i, vi + pr * c_i + pi * c_r
            hr_ref[sl, :] = vr
            hi_ref[sl, :] = vi
            car_r[...] = vr[7:8, :]
            car_i[...] = vi[7:8, :]
            return carry

        lax.fori_loop(0, nblk, step, 0, unroll=4)
        y = lax.dot_general(hr_ref[...].astype(BF16), cr_ref[...], NT, preferred_element_type=F32)
        y = y - lax.dot_general(hi_ref[...].astype(BF16), ci_ref[...], NT, preferred_element_type=F32)
        y_ref[...] = y + d_ref[...] * u

    row = lambda c, b, t: (b * nt + t, c)
    mat = pl.BlockSpec((None, CHUNK_CH, st), lambda c, b, t: (c, 0, 0))
    return _call(
        body, "s5_scan_fwd", (SSM_CHUNKS, n // seq, nt),
        [pl.BlockSpec((T_SCAN, CHUNK_CH), row), mat, mat, mat, mat,
         pl.BlockSpec((1, CHUNK_CH), lambda c, b, t: (0, c)),
         pl.BlockSpec((16, 8, st), lambda c, b, t: (0, 0, c))],
        [pl.BlockSpec((T_SCAN, CHUNK_CH), row), pl.BlockSpec((T_SCAN, st), row), pl.BlockSpec((T_SCAN, st), row)],
        [_sds((n, d), F32), _sds((n, N_STATE), F32), _sds((n, N_STATE), F32)],
        [pltpu.VMEM((1, st), F32), pltpu.VMEM((1, st), F32)], comm=comm,
    )(x, bbr, bbi, cr, ci, dvec, tab)


def _s5_scan_bwd(x, dy, dres, hr, hi, bbr, bbi, cr, ci, dvec, tab, seq, comm=None):
    n, d = x.shape
    nt = seq // T_SCAN
    nblk = T_SCAN // 8
    st = CHUNK_ST

    def body(x_ref, dy_ref, dres_ref, hr_ref, hi_ref, hpr_ref, hpi_ref, bbr_ref, bbi_ref, cr_ref, ci_ref,
             d_ref, tab_ref, dx_ref, dcr_ref, dci_ref, dbr_ref, dbi_ref, dar_ref, dai_ref, dd_ref,
             g_r, g_i, car_r, car_i):
        b = pl.program_id(1)
        tg = pl.program_id(2)
        first = jnp.logical_and(b == 0, tg == 0)

        @pl.when(tg == 0)
        def _():
            car_r[...] = jnp.zeros_like(car_r)
            car_i[...] = jnp.zeros_like(car_i)

        u = x_ref[...]
        dyv = dy_ref[...]
        ub = u.astype(BF16)
        dyb = dyv.astype(BF16)
        g_r[...] = jnp.dot(dyb, cr_ref[...], preferred_element_type=F32)
        g_i[...] = -jnp.dot(dyb, ci_ref[...], preferred_element_type=F32)

        def step(ii, carry):
            i = nblk - 1 - ii
            sl = pl.ds(pl.multiple_of(i * 8, 8), 8)
            vr, vi = _scan8(g_r[sl, :], g_i[sl, :], tab_ref, 8, True)
            c_r = jnp.broadcast_to(car_r[...], (8, st))
            c_i = jnp.broadcast_to(car_i[...], (8, st))
            pr = tab_ref[14]
            pi = tab_ref[15]
            vr, vi = vr + pr * c_r - pi * c_i, vi + pr * c_i + pi * c_r
            g_r[sl, :] = vr
            g_i[sl, :] = vi
            car_r[...] = vr[0:1, :]
            car_i[...] = vi[0:1, :]
            return carry

        lax.fori_loop(0, nblk, step, 0, unroll=4)
        gr = g_r[...]
        gi = g_i[...]
        grb = gr.astype(BF16)
        gib = gi.astype(BF16)
        du = lax.dot_general(grb, bbr_ref[...], NT, preferred_element_type=F32)
        du = du + lax.dot_general(gib, bbi_ref[...], NT, preferred_element_type=F32)
        dx_ref[...] = DN_ALPHA * dres_ref[...] + du + d_ref[...] * dyv

        h_r = hr_ref[...]
        h_i = hi_ref[...]
        t_orig = nt - 1 - tg
        row0 = lax.broadcasted_iota(jnp.int32, (8, st), 0) == 0
        halo_ok = t_orig > 0

        def previous(h, halo_ref):
            top = jnp.broadcast_to(jnp.where(halo_ok, halo_ref[7:8, :], 0.0), (8, st))
            rolled = pltpu.roll(h, 1, 0)
            return jnp.concatenate([jnp.where(row0, top, rolled[:8, :]), rolled[8:, :]], axis=0)

        hp_r = previous(h_r, hpr_ref)
        hp_i = previous(h_i, hpi_ref)
        dar = jnp.sum(gr * hp_r + gi * hp_i, axis=0, keepdims=True)
        dai = jnp.sum(gi * hp_r - gr * hp_i, axis=0, keepdims=True)
        dcr = lax.dot_general(dyb, h_r.astype(BF16), TN, preferred_element_type=F32)
        dci = lax.dot_general(dyb, h_i.astype(BF16), TN, preferred_element_type=F32)
        dbr = lax.dot_general(ub, grb, TN, preferred_element_type=F32)
        dbi = lax.dot_general(ub, gib, TN, preferred_element_type=F32)
        dd = jnp.sum(dyv * u, axis=0, keepdims=True)

        @pl.when(first)
        def _():
            dcr_ref[...] = dcr
            dci_ref[...] = dci
            dbr_ref[...] = dbr
            dbi_ref[...] = dbi
            dar_ref[...] = dar
            dai_ref[...] = dai
            dd_ref[...] = dd

        @pl.when(jnp.logical_not(first))
        def _():
            dcr_ref[...] += dcr
            dci_ref[...] += dci
            dbr_ref[...] += dbr
            dbi_ref[...] += dbi
            dar_ref[...] += dar
            dai_ref[...] += dai
            dd_ref[...] += dd

    row = lambda c, b, t: (b * nt + (nt - 1 - t), c)
    halo = lambda c, b, t: (jnp.maximum((b * nt + (nt - 1 - t)) * (T_SCAN // 8) - 1, 0), c)
    mat = pl.BlockSpec((None, CHUNK_CH, st), lambda c, b, t: (c, 0, 0))
    chan = pl.BlockSpec((T_SCAN, CHUNK_CH), row)
    state = pl.BlockSpec((T_SCAN, st), row)
    return _call(
        body, "s5_scan_bwd", (SSM_CHUNKS, n // seq, nt),
        [chan, chan, chan, state, state, pl.BlockSpec((8, st), halo), pl.BlockSpec((8, st), halo),
         mat, mat, mat, mat, pl.BlockSpec((1, CHUNK_CH), lambda c, b, t: (0, c)),
         pl.BlockSpec((16, 8, st), lambda c, b, t: (0, 0, c))],
        [chan, mat, mat, mat, mat, pl.BlockSpec((1, st), lambda c, b, t: (0, c)),
         pl.BlockSpec((1, st), lambda c, b, t: (0, c)), pl.BlockSpec((1, CHUNK_CH), lambda c, b, t: (0, c))],
        [_sds((n, d), F32)] + [_sds((SSM_CHUNKS, CHUNK_CH, st), F32)] * 4
        + [_sds((1, N_STATE), F32), _sds((1, N_STATE), F32), _sds((1, d), F32)],
        [pltpu.VMEM((T_SCAN, st), F32), pltpu.VMEM((T_SCAN, st), F32), pltpu.VMEM((1, st), F32),
         pltpu.VMEM((1, st), F32)], comm=comm,
    )(x, dy, dres, hr, hi, hr, hi, bbr, bbi, cr, ci, dvec, tab)


_GELU_C = math.sqrt(2.0 / math.pi)


def _gelu_parts(y):
    t = jnp.tanh(_GELU_C * (y + 0.044715 * (y * y * y)))
    return 0.5 * (1.0 + t), t


def _glu_fwd(y, w_glu, b_glu):
    n, d = y.shape

    def body(y_ref, w_ref, b_ref, y2_ref, z_ref, gg_ref):
        yv = y_ref[...]
        cdf, _ = _gelu_parts(yv)
        y2 = yv * cdf
        z = jnp.dot(y2.astype(BF16), w_ref[...], preferred_element_type=F32) + b_ref[...]
        y2_ref[...] = y2.astype(BF16)
        z_ref[...] = z
        gg_ref[...] = (y2 * jax.nn.sigmoid(z)).astype(BF16)

    row = pl.BlockSpec((TM, d), lambda i: (i, 0))
    return _call(body, "glu_fwd", (n // TM,),
                 [row, pl.BlockSpec((d, d), lambda i: (0, 0)), pl.BlockSpec((1, d), lambda i: (0, 0))],
                 [row, row, row], [_sds((n, d), BF16), _sds((n, d), F32), _sds((n, d), BF16)])(y, w_glu, b_glu)


def _glu_bwd_gate(y, z, dgg):
    n, d = y.shape

    def body(y_ref, z_ref, dgg_ref, dz_ref, db_ref, t_ref):
        i = pl.program_id(0)
        yv = y_ref[...]
        cdf, _ = _gelu_parts(yv)
        y2 = yv * cdf
        s = jax.nn.sigmoid(z_ref[...])
        dg = dgg_ref[...]
        dz = dg * y2 * s * (1.0 - s)
        dz_ref[...] = dz.astype(BF16)
        t_ref[...] = dg * s
        db = jnp.sum(dz, axis=0, keepdims=True)

        @pl.when(i == 0)
        def _():
            db_ref[...] = db

        @pl.when(i > 0)
        def _():
            db_ref[...] += db

    row = pl.BlockSpec((TM, d), lambda i: (i, 0))
    vec = pl.BlockSpec((1, d), lambda i: (0, 0))
    return _call(body, "glu_bwd_gate", (n // TM,), [row, row, row], [row, vec, row],
                 [_sds((n, d), BF16), _sds((1, d), F32), _sds((n, d), F32)])(y, z, dgg)


def _gelu_bwd(y, dy2):
    n, d = y.shape

    def body(y_ref, g_ref, o_ref):
        yv = y_ref[...]
        cdf, t = _gelu_parts(yv)
        dinner = _GELU_C * (1.0 + 3.0 * 0.044715 * yv * yv)
        o_ref[...] = g_ref[...] * (cdf + 0.5 * yv * (1.0 - t * t) * dinner)

    row = pl.BlockSpec((TM, d), lambda i: (i, 0))
    return _call(body, "gelu_bwd", (n // TM,), [row, row], row, _sds((n, d), F32))(y, dy2)


HALO = 16


def _shift_down(x, halo, k):
    out = pltpu.roll(x, k, 0)
    row = lax.broadcasted_iota(jnp.int32, (8, x.shape[1]), 0)
    top = jnp.where(row < k, pltpu.roll(halo[HALO - 8:HALO, :], k, 0), out[:8, :])
    return jnp.concatenate([top, out[8:, :]], axis=0)


def _conv3(x, halo, w, b):
    x1 = _shift_down(x, halo, 1)
    x2 = _shift_down(x, halo, 2)
    return b + w[0:1, :] * x + w[1:2, :] * x1 + w[2:3, :] * x2, x1, x2


def _conv_gate_fwd(hc, conv_w, conv_b, seq):
    _, nb, n, c = hc.shape
    tiles_per_seq = seq // TM

    def body(x_ref, halo_ref, w_ref, b_ref, o_ref):
        i = pl.program_id(1)
        start = (i % tiles_per_seq) == 0
        cv = []
        for h in range(2):
            halo = jnp.where(start, 0.0, halo_ref[h].astype(F32))
            cv.append(_conv3(x_ref[h].astype(F32), halo, w_ref[h], b_ref[h])[0])
        o_ref[...] = (cv[1] * jax.nn.sigmoid(cv[1]) * cv[0]).astype(BF16)

    return _call(
        body, "conv_gate_fwd", (nb, n // TM),
        [pl.BlockSpec((2, None, TM, c), lambda d, i: (0, d, i, 0)),
         pl.BlockSpec((2, None, HALO, c), lambda d, i: (0, d, jnp.maximum(i * (TM // HALO) - 1, 0), 0)),
         pl.BlockSpec((2, None, 3, c), lambda d, i: (0, d, 0, 0)),
         pl.BlockSpec((2, None, 1, c), lambda d, i: (0, d, 0, 0))],
        pl.BlockSpec((None, TM, c), lambda d, i: (d, i, 0)),
        _sds((nb, n, c), BF16))(hc, hc, conv_w, conv_b)


def _conv_gate_bwd(hc, dact, conv_w, conv_b, seq, comm=None):
    _, nb, n, c = hc.shape
    tiles_per_seq = seq // TM
    last_halo = n // HALO - 1
    ext = TM + HALO

    def body(x_ref, prev_ref, next_ref, da_ref, dan_ref, w_ref, b_ref, dx_ref, dw_ref):
        i = pl.program_id(1)
        start = (i % tiles_per_seq) == 0
        end = (i % tiles_per_seq) == tiles_per_seq - 1
        row = lax.broadcasted_iota(jnp.int32, (ext, c), 0)
        xe, cv = [], []
        for h in range(2):
            halo = jnp.where(start, 0.0, prev_ref[h].astype(F32))
            x_ext = jnp.concatenate([x_ref[h], next_ref[h]], axis=0).astype(F32)
            conv, x1, x2 = _conv3(x_ext, halo, w_ref[h], b_ref[h])
            xe.append((x_ext, x1, x2))
            cv.append(conv)
        da = jnp.concatenate([da_ref[...], dan_ref[...]], axis=0).astype(F32)
        da = jnp.where(jnp.logical_and(end, row >= TM), 0.0, da)
        sg = jax.nn.sigmoid(cv[1])
        silu = cv[1] * sg
        dcv = [da * silu, da * cv[0] * (sg + silu * (1.0 - sg))]
        for h in range(2):
            w = w_ref[h]
            g = dcv[h]
            dx = w[0:1, :] * g + w[1:2, :] * pltpu.roll(g, ext - 1, 0) + w[2:3, :] * pltpu.roll(g, ext - 2, 0)
            dx_ref[h] = dx[:TM, :].astype(BF16)
            x_ext, x1, x2 = xe[h]
            gt = g[:TM, :]
            parts = [jnp.sum(gt * x_ext[:TM, :], axis=0, keepdims=True),
                     jnp.sum(gt * x1[:TM, :], axis=0, keepdims=True),
                     jnp.sum(gt * x2[:TM, :], axis=0, keepdims=True),
                     jnp.sum(gt, axis=0, keepdims=True)]
            upd = jnp.concatenate(parts + [jnp.zeros((4, c), F32)], axis=0)

            @pl.when(i == 0)
            def _():
                dw_ref[h] = upd

            @pl.when(i > 0)
            def _():
                dw_ref[h] += upd

    nxt = lambda i: jnp.minimum((i + 1) * (TM // HALO), last_halo)
    return _call(
        body, "conv_gate_bwd", (nb, n // TM),
        [pl.BlockSpec((2, None, TM, c), lambda d, i: (0, d, i, 0)),
         pl.BlockSpec((2, None, HALO, c), lambda d, i: (0, d, jnp.maximum(i * (TM // HALO) - 1, 0), 0)),
         pl.BlockSpec((2, None, HALO, c), lambda d, i: (0, d, nxt(i), 0)),
         pl.BlockSpec((None, TM, c), lambda d, i: (d, i, 0)),
         pl.BlockSpec((None, HALO, c), lambda d, i: (d, nxt(i), 0)),
         pl.BlockSpec((2, None, 3, c), lambda d, i: (0, d, 0, 0)),
         pl.BlockSpec((2, None, 1, c), lambda d, i: (0, d, 0, 0))],
        [pl.BlockSpec((2, None, TM, c), lambda d, i: (0, d, i, 0)),
         pl.BlockSpec((2, None, 8, c), lambda d, i: (0, d, 0, 0))],
        [_sds((2, nb, n, c), BF16), _sds((2, nb, 8, c), F32)], comm=comm)(hc, hc, hc, dact, dact, conv_w, conv_b)


def _t5_bucket_np(dist):
    exact = REL_BUCKETS // 2
    d = np.maximum(dist, 1).astype(np.float32)
    large = exact + (np.log(d / exact) / math.log(REL_MAX_DIST / exact) * (REL_BUCKETS - exact)).astype(np.int64)
    large = np.minimum(large, REL_BUCKETS - 1)
    return np.where(dist < exact, dist, large).astype(np.int32)


def _bucket_table(dil):
    steps = np.arange(BAND)[:, None] + BAND - np.arange(2 * BAND)[None, :]
    return _t5_bucket_np(np.maximum(steps, 0) * dil).reshape(1, BAND * 2 * BAND).astype(np.float32)


def _bias_expand(rel_t, dil):
    nk = BAND * 2 * BAND
    bucket = jnp.asarray(_bucket_table(dil))

    def body(r_ref, bk_ref, o_ref):
        ids = lax.broadcasted_iota(jnp.int32, (REL_BUCKETS, nk), 0).astype(F32)
        onehot = (ids == bk_ref[...]).astype(F32)
        o_ref[...] = jnp.dot(r_ref[...], onehot, precision=HIGHEST, preferred_element_type=F32)

    return _call(body, "bias_expand", (1,),
                 [pl.BlockSpec((N_HEADS, REL_BUCKETS), lambda i: (0, 0)), pl.BlockSpec((1, nk), lambda i: (0, 0))],
                 pl.BlockSpec((N_HEADS, nk), lambda i: (0, 0)), _sds((N_HEADS, nk), F32))(rel_t, bucket)


def _bias_reduce(dbias, dil):
    nk = BAND * 2 * BAND
    bucket = jnp.asarray(_bucket_table(dil))

    def body(g_ref, bk_ref, o_ref):
        ids = lax.broadcasted_iota(jnp.int32, (REL_BUCKETS, nk), 0).astype(F32)
        onehot = (ids == bk_ref[...]).astype(F32)
        o_ref[...] = lax.dot_general(g_ref[...], onehot, NT, precision=HIGHEST, preferred_element_type=F32)

    return _call(body, "bias_reduce", (1,),
                 [pl.BlockSpec((N_HEADS, nk), lambda i: (0, 0)), pl.BlockSpec((1, nk), lambda i: (0, 0))],
                 pl.BlockSpec((N_HEADS, REL_BUCKETS), lambda i: (0, 0)),
                 _sds((N_HEADS, REL_BUCKETS), F32))(dbias, bucket)


def _valid_mask(n):
    qi = lax.broadcasted_iota(jnp.int32, (BAND, 2 * BAND), 0)
    kj = lax.broadcasted_iota(jnp.int32, (BAND, 2 * BAND), 1)
    steps = qi + BAND - kj
    in_band = jnp.logical_and(steps >= 0, steps <= BAND)
    return jnp.logical_and(in_band, jnp.logical_or(n > 0, kj >= BAND))


ATTN_COLS = {1: 1024, 4: 512, 16: 128}


def _residue(dil, r):
    return slice(None) if dil == 1 else pl.ds(r, BAND, stride=dil)


def _stack_heads(x, first):
    return jnp.concatenate([jnp.where(first, x, 0.0), jnp.where(first, 0.0, x)], axis=0)


def _attn_fwd(q, k, v, bias, g, dil, bsz, seq):
    n = q.shape[0]
    rows = BAND * dil
    nch = seq // rows
    w = N_HEADS * HEAD_DIM
    wc = ATTN_COLS[dil]
    ncol = w // wc
    hpc = wc // HEAD_DIM
    nj = wc // 128
    slab = lambda k: pltpu.VMEM((k * nj, rows, 128), F32)

    def body(q_ref, kc_ref, kp_ref, vc_ref, vp_ref, b_ref, o_ref, l_ref, qf, kf, vf, of, lf):
        ch = pl.program_id(2)
        valid = _valid_mask(ch)
        valid = jnp.concatenate([valid, valid], axis=0)
        lane = lax.broadcasted_iota(jnp.int32, (BAND, 128), 1)
        first = lane < HEAD_DIM
        for j in range(nj):
            sl = slice(128 * j, 128 * (j + 1))
            qf[j] = q_ref[:, sl].astype(F32)
            kf[j] = kp_ref[:, sl].astype(F32)
            kf[nj + j] = kc_ref[:, sl].astype(F32)
            vf[j] = vp_ref[:, sl].astype(F32)
            vf[nj + j] = vc_ref[:, sl].astype(F32)
        for r in range(dil):
            rr = _residue(dil, r)
            for j in range(nj):
                q2 = qf.at[j][rr, :]
                k2 = jnp.concatenate([kf.at[j][rr, :], kf.at[nj + j][rr, :]], axis=0).astype(BF16)
                v2 = jnp.concatenate([vf.at[j][rr, :], vf.at[nj + j][rr, :]], axis=0).astype(BF16)
                qs = _stack_heads(q2, first).astype(BF16)
                s = lax.dot_general(qs, k2, NT, preferred_element_type=F32) * (HEAD_DIM ** -0.5)
                s = jnp.where(valid, s + b_ref[2 * j:2 * j + 2].reshape(2 * BAND, 2 * BAND), NEG_BIG)
                m = jnp.max(s, axis=1, keepdims=True)
                p = jnp.exp(s - m)
                l = jnp.sum(p, axis=1, keepdims=True)
                pv = jnp.dot(p.astype(BF16), v2, preferred_element_type=F32) / l
                lse = jnp.broadcast_to(m + jnp.log(l), (2 * BAND, 128))
                of.at[j][rr, :] = jnp.where(first, pv[:BAND], pv[BAND:])
                lf.at[j][rr, :] = jnp.where(first, lse[:BAND], lse[BAND:])
        for j in range(nj):
            sl = slice(128 * j, 128 * (j + 1))
            o_ref[:, sl] = of[j]
            l_ref[:, sl] = lf[j]

    base = g * ncol
    cur = lambda c, b, ch: (b * nch + ch, base + c)
    prev = lambda c, b, ch: (b * nch + jnp.maximum(ch - 1, 0), base + c)
    blk = lambda im: pl.BlockSpec((rows, wc), im)
    out_blk = pl.BlockSpec((rows, wc), lambda c, b, ch: (b * nch + ch, c))
    return _call(
        body, "attn_fwd_d%d" % dil, (ncol, bsz, nch),
        [blk(cur), blk(cur), blk(prev), blk(cur), blk(prev),
         pl.BlockSpec((hpc, BAND, 2 * BAND), lambda c, b, ch: (c, 0, 0))],
        [out_blk, out_blk],
        [_sds((n, w), F32)] * 2,
        [slab(1), slab(2), slab(2), slab(1), slab(1)],
    )(q, k, k, v, v, bias)


def _attn_combine(outs, lses):
    n, w = outs[0].shape

    def body(o0, o1, o2, l0, l1, l2, o_ref, l_ref):
        la, lb, lc = l0[...], l1[...], l2[...]
        m = jnp.maximum(jnp.maximum(la, lb), lc)
        wa, wb, wc = jnp.exp(la - m), jnp.exp(lb - m), jnp.exp(lc - m)
        tot = wa + wb + wc
        o_ref[...] = (wa * o0[...] + wb * o1[...] + wc * o2[...]) / tot
        l_ref[...] = m + jnp.log(tot)

    row = pl.BlockSpec((TM, w), lambda i: (i, 0))
    return _call(body, "attn_combine", (n // TM,), [row] * 6, [row, row],
                 [_sds((n, w), F32)] * 2)(*outs, *lses)


def _attn_bwd(q, k, v, do, o, lse, bias, g, dil, bsz, seq, comm=None):
    n = q.shape[0]
    rows = BAND * dil
    nch = seq // rows
    w = N_HEADS * HEAD_DIM
    wc = ATTN_COLS[dil]
    ncol = w // wc
    hpc = wc // HEAD_DIM
    nj = wc // 128
    slab = lambda k: pltpu.VMEM((k * nj, rows, 128), F32)
    scale = HEAD_DIM ** -0.5

    def body(q_ref, kc_ref, kp_ref, vc_ref, vp_ref, do_ref, o_ref, l_ref, b_ref,
             dq_ref, dk_ref, dv_ref, db_ref, qf, kf, vf, dof, of, lf, dqf, dkf, dvf):
        b = pl.program_id(1)
        ch = pl.program_id(2)
        cur = ch % 2
        prv = 1 - cur

        @pl.when(jnp.logical_and(b == 0, ch == 0))
        def _():
            db_ref[...] = jnp.zeros_like(db_ref)

        @pl.when(ch == 0)
        def _():
            for j in range(nj):
                dkf[nj + j] = jnp.zeros((rows, 128), F32)
                dvf[nj + j] = jnp.zeros((rows, 128), F32)

        @pl.when(ch < nch)
        def _():
            valid = _valid_mask(ch)
            valid = jnp.concatenate([valid, valid], axis=0)
            first = lax.broadcasted_iota(jnp.int32, (BAND, 128), 1) < HEAD_DIM
            for j in range(nj):
                sl = slice(128 * j, 128 * (j + 1))
                qf[j] = q_ref[:, sl].astype(F32)
                kf[j] = kp_ref[:, sl].astype(F32)
                kf[nj + j] = kc_ref[:, sl].astype(F32)
                vf[j] = vp_ref[:, sl].astype(F32)
                vf[nj + j] = vc_ref[:, sl].astype(F32)
                dof[j] = do_ref[:, sl]
                of[j] = o_ref[:, sl]
                lf[j] = l_ref[:, sl]
            for r in range(dil):
                rr = _residue(dil, r)
                for j in range(nj):
                    k2 = jnp.concatenate([kf.at[j][rr, :], kf.at[nj + j][rr, :]], axis=0).astype(BF16)
                    v2 = jnp.concatenate([vf.at[j][rr, :], vf.at[nj + j][rr, :]], axis=0).astype(BF16)
                    do2 = dof.at[j][rr, :]
                    lse2 = lf.at[j][rr, :]
                    qs = _stack_heads(qf.at[j][rr, :], first).astype(BF16)
                    dos = _stack_heads(do2, first)
                    delta = jnp.sum(dos * jnp.concatenate([of.at[j][rr, :]] * 2, axis=0), axis=1, keepdims=True)
                    dos = dos.astype(BF16)
                    lse_col = jnp.concatenate([lse2[:, 0:1], lse2[:, HEAD_DIM:HEAD_DIM + 1]], axis=0)
                    s = lax.dot_general(qs, k2, NT, preferred_element_type=F32) * scale
                    s = jnp.where(valid, s + b_ref[2 * j:2 * j + 2].reshape(2 * BAND, 2 * BAND), NEG_BIG)
                    p = jnp.exp(s - lse_col)
                    dp = lax.dot_general(dos, v2, NT, preferred_element_type=F32)
                    ds = p * (dp - delta)
                    db_ref[2 * j:2 * j + 2] += ds.reshape(2, BAND, 2 * BAND)
                    dsb = ds.astype(BF16)
                    dq2 = jnp.dot(dsb, k2, preferred_element_type=F32) * scale
                    dk2 = lax.dot_general(dsb, qs, TN, preferred_element_type=F32) * scale
                    dv2 = lax.dot_general(p.astype(BF16), dos, TN, preferred_element_type=F32)
                    dqf.at[j][rr, :] = jnp.where(first, dq2[:BAND], dq2[BAND:])
                    dkf.at[prv * nj + j][rr, :] += dk2[:BAND, :]
                    dvf.at[prv * nj + j][rr, :] += dv2[:BAND, :]
                    dkf.at[cur * nj + j][rr, :] = dk2[BAND:, :]
                    dvf.at[cur * nj + j][rr, :] = dv2[BAND:, :]
            for j in range(nj):
                dq_ref[:, 128 * j:128 * (j + 1)] = dqf[j].astype(BF16)

        for j in range(nj):
            sl = slice(128 * j, 128 * (j + 1))
            dk_ref[:, sl] = dkf[prv * nj + j].astype(BF16)
            dv_ref[:, sl] = dvf[prv * nj + j].astype(BF16)

    last = nch - 1
    base = g * ncol
    cur3 = lambda c, b, ch: (b * nch + jnp.minimum(ch, last), base + c)
    prev3 = lambda c, b, ch: (b * nch + jnp.maximum(jnp.minimum(ch, last) - 1, 0), base + c)
    cur1 = lambda c, b, ch: (b * nch + jnp.minimum(ch, last), c)
    lag1 = lambda c, b, ch: (b * nch + jnp.maximum(ch - 1, 0), c)
    blk = lambda im: pl.BlockSpec((rows, wc), im)
    bias_blk = pl.BlockSpec((hpc, BAND, 2 * BAND), lambda c, b, ch: (c, 0, 0))
    out = _sds((n, w), BF16)
    return _call(
        body, "attn_bwd_d%d" % dil, (ncol, bsz, nch + 1),
        [blk(cur3), blk(cur3), blk(prev3), blk(cur3), blk(prev3), blk(cur1), blk(cur1), blk(cur1), bias_blk],
        [blk(cur1), blk(lag1), blk(lag1), bias_blk],
        [out, out, out, _sds((N_HEADS, BAND, 2 * BAND), F32)],
        [slab(1), slab(2), slab(2), slab(1), slab(1), slab(1), slab(1), slab(2), slab(2)], comm=comm,
    )(q, k, k, v, v, do, o, lse, bias)


def _adamw(name, parts, w, m, v):
    nl, r, c = w.shape
    nparts = parts[0].shape[0]
    tr = r
    for cand in (256, 352, 128):
        if r % cand == 0 and r > cand:
            tr = cand
            break
    c1 = 1.0 - ADAM_B1 ** ADAM_STEP
    c2 = 1.0 - ADAM_B2 ** ADAM_STEP

    def body(*refs):
        p_refs = refs[:nl]
        w_ref, m_ref, v_ref, g_ref, d_ref, nm_ref, nv_ref = refs[nl:]
        layer = pl.program_id(0)
        for l in range(nl):
            @pl.when(layer == l)
            def _():
                g = p_refs[l][0].astype(F32)
                for dev in range(1, nparts):
                    g = g + p_refs[l][dev].astype(F32)
                nm = ADAM_B1 * m_ref[...] + (1.0 - ADAM_B1) * g
                nv = ADAM_B2 * v_ref[...] + (1.0 - ADAM_B2) * (g * g)
                m_hat = nm / c1
                v_hat = nv / c2
                g_ref[...] = g
                d_ref[...] = -ADAM_LR * (m_hat / (jnp.sqrt(v_hat) + ADAM_EPS) + ADAM_WD * w_ref[...])
                nm_ref[...] = nm
                nv_ref[...] = nv

    def part_spec(l):
        return pl.BlockSpec((nparts, tr, c), lambda layer, i: (0, jnp.where(layer == l, i, 0), 0))

    row = pl.BlockSpec((None, tr, c), lambda layer, i: (layer, i, 0))
    return _call(body, name, (nl, r // tr), [part_spec(l) for l in range(nl)] + [row, row, row],
                 [row] * 4, [_sds((nl, r, c), F32)] * 4)(*parts, w, m, v)


def _exchange(name, srcs, out_shapes, items):
    n_in = len(srcs)
    n_out = len(out_shapes)
    n_items = len(items)

    def body(*refs):
        ins = refs[:n_in]
        outs = refs[n_in:n_in + n_out]
        send_sems, recv_sems, local_sems = refs[n_in + n_out:]
        x, y, c = lax.axis_index("x"), lax.axis_index("y"), lax.axis_index("c")
        me = 4 * x + 2 * y + c

        def pick(ref, sel, peer):
            idx = tuple(peer if s == "peer" else me if s == "me" else s for s in sel)
            return ref.at[idx] if idx else ref

        copies = []
        for it, (si, ssel, oi, osel) in enumerate(items):
            local = pltpu.make_async_copy(pick(ins[si], ssel, me), pick(outs[oi], osel, me), local_sems.at[it])
            local.start()
            copies.append(local)
            for j in range(1, N_DEV):
                px = 1 - x if j & 4 else x
                py = 1 - y if j & 2 else y
                pc = 1 - c if j & 1 else c
                peer = 4 * px + 2 * py + pc
                sem = it * (N_DEV - 1) + j - 1
                cp = pltpu.make_async_remote_copy(
                    src_ref=pick(ins[si], ssel, peer), dst_ref=pick(outs[oi], osel, peer),
                    send_sem=send_sems.at[sem], recv_sem=recv_sems.at[sem],
                    device_id=(px, py, pc), device_id_type=pl.DeviceIdType.MESH)
                cp.start()
                copies.append(cp)
        for cp in copies:
            cp.wait()

    any_spec = pl.BlockSpec(memory_space=pl.ANY)
    return pl.pallas_call(
        body, name=name, in_specs=[any_spec] * n_in, out_specs=[any_spec] * n_out, out_shape=list(out_shapes),
        scratch_shapes=[pltpu.SemaphoreType.DMA((n_items * (N_DEV - 1),)),
                        pltpu.SemaphoreType.DMA((n_items * (N_DEV - 1),)),
                        pltpu.SemaphoreType.DMA((n_items,))],
        compiler_params=pltpu.CompilerParams(has_side_effects=True),
    )(*srcs)


def _sel(ref, sel, slot=None):
    idx = tuple(slot if s == "slot" else s for s in sel)
    return ref.at[idx] if idx else ref


def _gather_plan(srcs, out_shapes, items):
    per = N_DEV - 1
    n_items = len(items)

    def phases(ins, outs, send_sems, recv_sems, local_sems, total):
        x, y, c = lax.axis_index("x"), lax.axis_index("y"), lax.axis_index("c")
        sibling = (x, y, 1 - c)
        chips = [(1 - x, y), (x, 1 - y), (1 - x, 1 - y)]
        slot_of = lambda px, py, pc: 4 * px + 2 * py + pc

        def copy(it, k, block, to, src=None):
            _, _, oi, osel = items[it]
            dst = _sel(outs[oi], osel, slot_of(*block))
            return pltpu.make_async_remote_copy(
                src_ref=dst if src is None else src, dst_ref=dst,
                send_sem=send_sems.at[it * per + k], recv_sem=recv_sems.at[it * per + k],
                device_id=to, device_id_type=pl.DeviceIdType.MESH)

        def own_copies(it):
            si, ssel, oi, osel = items[it]
            src = _sel(ins[si], ssel)
            mine = pltpu.make_async_copy(src, _sel(outs[oi], osel, slot_of(x, y, c)), local_sems.at[it])
            first = [copy(it, 0, (x, y, c), sibling, src=src)]
            first += [copy(it, 1 + j, (x, y, c), (*chip, c), src=src) for j, chip in enumerate(chips)]
            return mine, first

        def start():
            for it in range(n_items):
                mine, first = own_copies(it)
                mine.start()
                for cp in first:
                    cp.start()

        def forward(it):
            def go():
                for j, chip in enumerate(chips):
                    copy(it, 1 + j, (*chip, c), (x, y, c)).wait_recv()
                    copy(it, 4 + j, (*chip, c), sibling).start()
            return go

        def finish():
            for it in range(n_items):
                copy(it, 0, sibling, (x, y, c)).wait_recv()
                for j, chip in enumerate(chips):
                    copy(it, 4 + j, (*chip, 1 - c), (x, y, c)).wait_recv()
            for it in range(n_items):
                mine, first = own_copies(it)
                for cp in first:
                    cp.wait_send()
                for j, chip in enumerate(chips):
                    copy(it, 4 + j, (*chip, c), sibling).wait_send()
                mine.wait()

        lo = max(total // 3, 1)
        acts = [(0, start)]
        for it in range(n_items):
            acts.append((min(lo + it * (total - 1 - lo) // n_items, total - 2), forward(it)))
        acts.append((total - 1, finish))
        return acts

    return _Plan(srcs, out_shapes, n_items * per, n_items, phases)


def _scatter_plan(srcs, out_shapes, items):
    per = N_DEV - 1
    n_items = len(items)

    def phases(ins, outs, send_sems, recv_sems, local_sems, total):
        x, y, c = lax.axis_index("x"), lax.axis_index("y"), lax.axis_index("c")
        me = 4 * x + 2 * y + c

        def copies():
            out = []
            for it, (si, oi, osel) in enumerate(items):
                dst = _sel(outs[oi], osel, me)
                out.append(pltpu.make_async_copy(ins[si].at[me], dst, local_sems.at[it]))
                for j in range(1, N_DEV):
                    px = 1 - x if j & 4 else x
                    py = 1 - y if j & 2 else y
                    pc = 1 - c if j & 1 else c
                    out.append(pltpu.make_async_remote_copy(
                        src_ref=ins[si].at[4 * px + 2 * py + pc], dst_ref=dst,
                        send_sem=send_sems.at[it * per + j - 1], recv_sem=recv_sems.at[it * per + j - 1],
                        device_id=(px, py, pc), device_id_type=pl.DeviceIdType.MESH))
            return out

        def start():
            for cp in copies():
                cp.start()

        def finish():
            for cp in copies():
                cp.wait()

        return [(0, start), (total - 1, finish)]

    return _Plan(srcs, out_shapes, n_items * per, n_items, phases)


def _sum_slots(parts):
    _, r, c = parts.shape

    def body(p_ref, o_ref):
        acc = p_ref[0]
        for dev in range(1, N_DEV):
            acc = acc + p_ref[dev]
        o_ref[...] = acc

    return _call(body, "sum_slots", (1,), [pl.BlockSpec((N_DEV, r, c), lambda i: (0, 0, 0))],
                 pl.BlockSpec((r, c), lambda i: (0, 0)), _sds((r, c), F32))(parts)


def _pack(arrays):
    rows = []
    for a in arrays:
        flat = a.reshape(-1).astype(F32)
        pad = (-flat.shape[0]) % 1024
        if pad:
            flat = jnp.concatenate([flat, jnp.zeros((pad,), F32)])
        rows.append(flat.reshape(-1, 128))
    return jnp.concatenate(rows, axis=0)


def _unpack(buf, shapes):
    out = []
    r0 = 0
    for shp in shapes:
        size = int(np.prod(shp))
        nrows = -(-size // 1024) * 8
        out.append(buf[r0:r0 + nrows].reshape(-1)[:size].reshape(shp))
        r0 += nrows
    return out


def _block_diag_chunks(val):
    eye = jnp.eye(8, dtype=val.dtype)
    return jnp.einsum("cjhp,jk->cjhkp", val, eye).reshape(SSM_CHUNKS, CHUNK_CH, CHUNK_ST)


def _diag_blocks(dense):
    d5 = dense.reshape(SSM_CHUNKS, 8, SSM_GROUP, 8, SSM_STATE)
    return jnp.stack([d5[:, j, :, j, :] for j in range(8)], axis=1)


def _ffn_fwd(tag, hin, hin_b, w_up, conv_w, conv_b, w_down4, gain, bias, seq, comm=None):
    hc = _mm_nn("ffn_up" + tag, hin_b, w_up, FFN_HIDDEN, blocked_out=True, comm=comm)
    couts = None
    if comm is not None:
        hc, couts = hc
    nb, n, c = hc.shape
    hc = hc.reshape(2, nb // 2, n, c)
    act = _conv_gate_fwd(hc, conv_w, conv_b, seq)
    ffn = _mm_nn_red("ffn_down" + tag, act, w_down4, F32)
    hout, hout_b = _ln_fwd("ffn_norm" + tag, hin, ffn, gain, bias)
    return hout, hout_b, (hin, hin_b, hc, act, ffn), couts


def _ffn_bwd(tag, saved, dh, w_up, conv_w, conv_b, w_down4, gain, seq, comm_conv=None, comm_dx=None):
    hin, hin_b, hc, act, ffn = saved
    dr, dr_b, dgain, dbias = _ln_bwd("ffn_norm_bwd" + tag, hin, ffn, dh, gain)
    d_wdown = _mm_tn("ffn_down_dw" + tag, act, dr_b, act.shape[0], BF16, a_blocked=True, g_mode="shared")
    dact = _mm_nt_out("ffn_down_dx" + tag, dr_b, w_down4, FFN_HIDDEN)
    res = _conv_gate_bwd(hc, dact, conv_w, conv_b, seq, comm=comm_conv)
    couts_conv = None
    if comm_conv is not None:
        res, couts_conv = res
    dhc, dconv = res
    dhc8 = dhc.reshape(2 * dhc.shape[1], dhc.shape[2], dhc.shape[3])
    d_wup = _mm_tn("ffn_up_dw" + tag, hin_b, dhc8, dhc8.shape[0], BF16, g_mode="blocked")
    dhin = _mm_nt_red("ffn_up_dx" + tag, dhc8, w_up, F32, g_blocked=True, add=dr, add_scale=DN_ALPHA,
                      comm=comm_dx)
    couts_dx = None
    if comm_dx is not None:
        dhin, couts_dx = dhin
    return dhin, d_wup, d_wdown, dconv, dgain, dbias, couts_conv, couts_dx


def _local_step(x, target, p, wfull, bsz, seq, ex=None):
    n = bsz * seq
    xf = x.reshape(n, D_MODEL)
    tf = target.reshape(n, D_MODEL)

    lr = p["s5_lam_re"].reshape(1, N_STATE)
    li = p["s5_lam_im"].reshape(1, N_STATE)
    ldt = p["s5_log_dt"].reshape(1, D_MODEL // SSM_GROUP)
    br_t = p["s5_b_re"].reshape(N_STATE, SSM_GROUP).T
    bi_t = p["s5_b_im"].reshape(N_STATE, SSM_GROUP).T
    ab_r, ab_i, bb_r, bb_i, tab = _s5_prep(lr, li, ldt, br_t, bi_t)

    def bb_chunks(bb):
        return _block_diag_chunks(bb.reshape(SSM_GROUP, SSM_CHUNKS, 8, SSM_STATE).transpose(1, 2, 0, 3))

    def c_chunks(cc):
        return _block_diag_chunks(cc.reshape(SSM_CHUNKS, 8, SSM_GROUP, SSM_STATE))

    bbr_c, bbi_c = bb_chunks(bb_r).astype(BF16), bb_chunks(bb_i).astype(BF16)
    cr_c, ci_c = c_chunks(p["s5_c_re"]).astype(BF16), c_chunks(p["s5_c_im"]).astype(BF16)
    dvec = p["s5_d"].reshape(1, D_MODEL)

    if ex is None:
        y, h_r, h_i = _s5_scan_fwd(xf, bbr_c, bbi_c, cr_c, ci_c, dvec, tab, seq)
    else:
        (y, h_r, h_i), gathered = _s5_scan_fwd(xf, bbr_c, bbi_c, cr_c, ci_c, dvec, tab, seq,
                                               comm=ex.gather_first())
        p, wfull = ex.take_first(gathered, p)
    ln_g = p["ln_gain"].reshape(4, 1, D_MODEL)
    ln_b = p["ln_bias"].reshape(4, 1, D_MODEL)
    y2, z, gg = _glu_fwd(y, wfull["s5_w_glu"], p["s5_b_glu"].reshape(1, D_MODEL))
    mix = _mm_nn("s5_out", gg, wfull["s5_w_out"][None], F32)
    h1, h1b = _ln_fwd("s5_norm", xf, mix, ln_g[0], ln_b[0])
    h2, h2b, ffn0_saved, gathered = _ffn_fwd("0", h1, h1b, wfull["ffn_w_up"][0], p["conv_w"][0], p["conv_b"][0],
                                             wfull["ffn_w_down"][0], ln_g[1], ln_b[1], seq,
                                             comm=ex.gather_second() if ex else None)
    if ex is not None:
        wfull = ex.take_second(gathered, wfull)

    kmat = _mm_nn("attn_k", h2b, wfull["attn_w_kv"], BF16, blocks=(0, 4))
    vmat = _mm_nn("attn_v", h2b, wfull["attn_w_kv"], BF16, blocks=(4, 4))
    qmat = _mm_nn("attn_q", h2b, wfull["attn_w_q"], BF16)
    biases, outs, lses = [], [], []
    for g, dil in enumerate(DILATIONS):
        rel_t = p["rel_bias"][:, g * N_HEADS:(g + 1) * N_HEADS].T
        biases.append(_bias_expand(rel_t, dil).reshape(N_HEADS, BAND, 2 * BAND))
        o_g, l_g = _attn_fwd(qmat, kmat, vmat, biases[g], g, dil, bsz, seq)
        outs.append(o_g)
        lses.append(l_g)
    o, lse = _attn_combine(outs, lses)
    attn_out = _mm_nn("attn_out", o, wfull["attn_w_out"][None], F32)
    h3, h3b = _ln_fwd("attn_norm", h2, attn_out, ln_g[2], ln_b[2])
    h4, _, ffn1_saved, _ = _ffn_fwd("1", h3, h3b, wfull["ffn_w_up"][1], p["conv_w"][1], p["conv_b"][1],
                                    wfull["ffn_w_down"][1], ln_g[3], ln_b[3], seq)
    loss_tile, dh4 = _loss_head(h4, tf)

    grads = {}
    dh3, grads["ffn_w_up1"], grads["ffn_w_down1"], dconv1, dg3, db3, _, _ = _ffn_bwd(
        "1", ffn1_saved, dh4, wfull["ffn_w_up"][1], p["conv_w"][1], p["conv_b"][1], wfull["ffn_w_down"][1],
        ln_g[3], seq)
    dr3, dr3b, dg2, db2 = _ln_bwd("attn_norm_bwd", h2, attn_out, dh3, ln_g[2])
    grads["attn_w_out"] = _mm_tn("attn_out_dw", o, dr3b, 1, BF16, g_mode="shared")
    do = _mm_nt_red("attn_out_dx", dr3b, wfull["attn_w_out"][None], F32, g_blocked=False)
    dqs, dks, dvs, drel = [], [], [], []
    for g, dil in enumerate(DILATIONS):
        plan = ex.scatter(grads, ("ffn_w_up1", "ffn_w_down1")) if ex and dil == DILATIONS[-1] else None
        res = _attn_bwd(qmat, kmat, vmat, do, o, lse, biases[g], g, dil, bsz, seq, comm=plan)
        if plan is not None:
            res, got = res
            ex.keep(plan, got)
        dq_g, dk_g, dv_g, db_g = res
        dqs.append(dq_g)
        dks.append(dk_g)
        dvs.append(dv_g)
        drel.append(_bias_reduce(db_g.reshape(N_HEADS, BAND * 2 * BAND), dil).T)
    dq = jnp.concatenate(dqs, axis=1)
    dk = jnp.concatenate(dks, axis=1)
    dv = jnp.concatenate(dvs, axis=1)
    grads["rel_bias"] = jnp.concatenate(drel, axis=1)
    grads["attn_w_q"] = _mm_tn("attn_q_dw", h2b, dq, N_DEV, BF16)
    grads["attn_w_kv"] = jnp.concatenate([_mm_tn("attn_k_dw", h2b, dk, 4, BF16),
                                          _mm_tn("attn_v_dw", h2b, dv, 4, BF16)], axis=0)
    dh2 = _mm_nt_red("attn_q_dx", dq, wfull["attn_w_q"], F32, g_blocked=False, add=dr3, add_scale=DN_ALPHA)
    dh2 = _mm_nt_red("attn_k_dx", dk, wfull["attn_w_kv"], F32, g_blocked=False, add=dh2, blocks=(0, 4))
    dh2 = _mm_nt_red("attn_v_dx", dv, wfull["attn_w_kv"], F32, g_blocked=False, add=dh2, blocks=(4, 4))

    plan_conv = ex.scatter(grads, ("attn_w_q", "attn_w_out")) if ex else None
    plan_dx = ex.scatter(grads, ("attn_w_kv",)) if ex else None
    dh1, grads["ffn_w_up0"], grads["ffn_w_down0"], dconv0, dg1, db1, got_conv, got_dx = _ffn_bwd(
        "0", ffn0_saved, dh2, wfull["ffn_w_up"][0], p["conv_w"][0], p["conv_b"][0], wfull["ffn_w_down"][0],
        ln_g[1], seq, comm_conv=plan_conv, comm_dx=plan_dx)
    if ex is not None:
        ex.keep(plan_conv, got_conv)
        ex.keep(plan_dx, got_dx)
    dr1, dr1b, dg0, db0 = _ln_bwd("s5_norm_bwd", xf, mix, dh1, ln_g[0])
    grads["s5_w_out"] = _mm_tn("s5_out_dw", gg, dr1b, 1, BF16, g_mode="shared")
    dgg = _mm_nt_red("s5_out_dx", dr1b, wfull["s5_w_out"][None], F32, g_blocked=False)
    dz, d_bglu, dy2_direct = _glu_bwd_gate(y, z, dgg)
    grads["s5_w_glu"] = _mm_tn("s5_glu_dw", y2, dz, 1, BF16, g_mode="shared")
    dy2 = _mm_nt_red("s5_glu_dx", dz, wfull["s5_w_glu"][None], F32, g_blocked=False, add=dy2_direct)
    dy = _gelu_bwd(y, dy2)
    plan_last = ex.scatter(grads, ("ffn_w_up0", "ffn_w_down0", "s5_w_out", "s5_w_glu")) if ex else None
    res = _s5_scan_bwd(xf, dy, dr1, h_r, h_i, bbr_c, bbi_c, cr_c, ci_c, dvec, tab, seq, comm=plan_last)
    if ex is not None:
        res, got = res
        ex.keep(plan_last, got)
    dx, dcr, dci, dbr, dbi, dar, dai, dd = res

    def bb_from_chunks(dense):
        return _diag_blocks(dense).transpose(2, 0, 1, 3).reshape(SSM_GROUP, N_STATE)

    d_lr, d_li, d_ldt, d_br, d_bi = _s5_prep_bwd(lr, li, ldt, br_t, bi_t, dar, dai,
                                                 bb_from_chunks(dbr), bb_from_chunks(dbi))
    grads["s5_lam_re"] = d_lr.reshape(p["s5_lam_re"].shape)
    grads["s5_lam_im"] = d_li.reshape(p["s5_lam_im"].shape)
    grads["s5_log_dt"] = d_ldt.reshape(p["s5_log_dt"].shape)
    grads["s5_b_re"] = d_br.T.reshape(p["s5_b_re"].shape)
    grads["s5_b_im"] = d_bi.T.reshape(p["s5_b_im"].shape)
    grads["s5_c_re"] = _diag_blocks(dcr).reshape(p["s5_c_re"].shape)
    grads["s5_c_im"] = -_diag_blocks(dci).reshape(p["s5_c_im"].shape)
    grads["s5_d"] = dd.reshape(p["s5_d"].shape)
    grads["s5_b_glu"] = d_bglu.reshape(1, D_MODEL)
    dconv = jnp.stack([dconv0, dconv1]).reshape(2, N_DEV, 8, -1)
    grads["ffn_conv_w"] = dconv[:, :, 0:3, :].transpose(0, 2, 1, 3)
    grads["ffn_conv_b"] = dconv[:, :, 3, :]
    grads["ln_gain"] = jnp.stack([dg0, dg1, dg2, dg3]).reshape(2, 2, D_MODEL)
    grads["ln_bias"] = jnp.stack([db0, db1, db2, db3]).reshape(2, 2, D_MODEL)
    return loss_tile[0, 0], dx.reshape(x.shape), grads


SMALL_REPLICATED = ("s5_lam_re", "s5_lam_im", "s5_log_dt", "s5_b_re", "s5_b_im", "s5_c_re", "s5_c_im", "s5_d",
                    "rel_bias", "ffn_conv_b")
SMALL_SHARDED = ("s5_b_glu", "ffn_conv_w", "ln_gain", "ln_bias")
BIG = ("s5_w_glu", "s5_w_out", "attn_w_kv", "attn_w_q", "attn_w_out", "ffn_w_up", "ffn_w_down")
WEIGHTS = ("s5_lam_re", "s5_lam_im", "s5_log_dt", "s5_b_re", "s5_b_im", "s5_c_re", "s5_c_im", "s5_d", "s5_w_glu",
           "s5_b_glu", "s5_w_out", "attn_w_kv", "attn_w_q", "attn_w_out", "rel_bias", "ffn_w_up", "ffn_conv_w",
           "ffn_conv_b", "ffn_w_down", "ln_gain", "ln_bias")


class _Exchanges:
    def __init__(self, w):
        self.w = w
        self.parts = {}
        self.up = w["ffn_w_up"].astype(BF16)
        self.down = w["ffn_w_down"].astype(BF16)

    def gather_first(self):
        w = self.w
        srcs = [w["s5_w_glu"][0].astype(BF16), w["s5_w_out"][0].astype(BF16),
                _pack([w[k] for k in SMALL_SHARDED]), self.up, self.down, w["attn_w_kv"].astype(BF16),
                w["attn_w_q"][0].astype(BF16), w["attn_w_out"][0].astype(BF16)]
        outs = [_sds((N_DEV,) + (s.shape[1:] if i in (3, 4) else s.shape), s.dtype) for i, s in enumerate(srcs)]
        items = [(i, (0,) if i in (3, 4) else (), i, ("slot",)) for i in range(len(srcs))]
        return _gather_plan(srcs, outs, items)

    def take_first(self, g, p):
        d = D_MODEL
        w = self.w
        wfull = {
            "s5_w_glu": g[0].reshape(d, d),
            "s5_w_out": g[1].reshape(d, d),
            "ffn_w_up": [g[3], None],
            "ffn_w_down": [g[4].reshape(4, -1, d), None],
            "attn_w_kv": g[5],
            "attn_w_q": g[6],
            "attn_w_out": g[7].reshape(d, d),
        }
        smalls = [_unpack(g[2][dev], [w[k].shape for k in SMALL_SHARDED]) for dev in range(N_DEV)]
        c_ff = w["ffn_conv_w"].shape[2]
        conv_w = jnp.stack([s[1] for s in smalls], axis=2)
        p = dict(p)
        p.update(s5_b_glu=jnp.concatenate([s[0] for s in smalls], axis=1),
                 ln_gain=jnp.concatenate([s[2] for s in smalls], axis=2),
                 ln_bias=jnp.concatenate([s[3] for s in smalls], axis=2),
                 conv_w=conv_w.transpose(0, 2, 1, 3).reshape(2, 2, 4, 3, c_ff),
                 conv_b=w["ffn_conv_b"].reshape(2, 2, 4, 1, c_ff))
        return p, wfull

    def gather_second(self):
        srcs = [self.up, self.down]
        outs = [_sds((N_DEV,) + s.shape[1:], BF16) for s in srcs]
        return _gather_plan(srcs, outs, [(0, (1,), 0, ("slot",)), (1, (1,), 1, ("slot",))])

    def take_second(self, g, wfull):
        wfull = dict(wfull)
        wfull["ffn_w_up"] = [wfull["ffn_w_up"][0], g[0]]
        wfull["ffn_w_down"] = [wfull["ffn_w_down"][0], g[1].reshape(4, -1, D_MODEL)]
        return wfull

    def scatter(self, grads, names):
        srcs = [grads[k].reshape(N_DEV, -1, grads[k].shape[-1]) for k in names]
        plan = _scatter_plan(srcs, [_sds(s.shape, BF16) for s in srcs],
                             [(i, i, ("slot",)) for i in range(len(names))])
        plan.names = names
        return plan

    def keep(self, plan, outs):
        for k, o in zip(plan.names, outs):
            self.parts[k] = o


def kernel(x, s5_lam_re, s5_lam_im, s5_log_dt, s5_b_re, s5_b_im, s5_c_re, s5_c_im, s5_d, s5_w_glu, s5_b_glu, s5_w_out, attn_w_kv, attn_w_q, attn_w_out, rel_bias, ffn_w_up, ffn_conv_w, ffn_conv_b, ffn_w_down, ln_gain, ln_bias, loss_target, m_s5_lam_re, m_s5_lam_im, m_s5_log_dt, m_s5_b_re, m_s5_b_im, m_s5_c_re, m_s5_c_im, m_s5_d, m_s5_w_glu, m_s5_b_glu, m_s5_w_out, m_attn_w_kv, m_attn_w_q, m_attn_w_out, m_rel_bias, m_ffn_w_up, m_ffn_conv_w, m_ffn_conv_b, m_ffn_w_down, m_ln_gain, m_ln_bias, v_s5_lam_re, v_s5_lam_im, v_s5_log_dt, v_s5_b_re, v_s5_b_im, v_s5_c_re, v_s5_c_im, v_s5_d, v_s5_w_glu, v_s5_b_glu, v_s5_w_out, v_attn_w_kv, v_attn_w_q, v_attn_w_out, v_rel_bias, v_ffn_w_up, v_ffn_conv_w, v_ffn_conv_b, v_ffn_w_down, v_ln_gain, v_ln_bias):
    w = dict(s5_lam_re=s5_lam_re, s5_lam_im=s5_lam_im, s5_log_dt=s5_log_dt, s5_b_re=s5_b_re, s5_b_im=s5_b_im,
             s5_c_re=s5_c_re, s5_c_im=s5_c_im, s5_d=s5_d, s5_w_glu=s5_w_glu, s5_b_glu=s5_b_glu, s5_w_out=s5_w_out,
             attn_w_kv=attn_w_kv, attn_w_q=attn_w_q, attn_w_out=attn_w_out, rel_bias=rel_bias, ffn_w_up=ffn_w_up,
             ffn_conv_w=ffn_conv_w, ffn_conv_b=ffn_conv_b, ffn_w_down=ffn_w_down, ln_gain=ln_gain, ln_bias=ln_bias)
    mom = dict(s5_lam_re=m_s5_lam_re, s5_lam_im=m_s5_lam_im, s5_log_dt=m_s5_log_dt, s5_b_re=m_s5_b_re,
               s5_b_im=m_s5_b_im, s5_c_re=m_s5_c_re, s5_c_im=m_s5_c_im, s5_d=m_s5_d, s5_w_glu=m_s5_w_glu,
               s5_b_glu=m_s5_b_glu, s5_w_out=m_s5_w_out, attn_w_kv=m_attn_w_kv, attn_w_q=m_attn_w_q,
               attn_w_out=m_attn_w_out, rel_bias=m_rel_bias, ffn_w_up=m_ffn_w_up, ffn_conv_w=m_ffn_conv_w,
               ffn_conv_b=m_ffn_conv_b, ffn_w_down=m_ffn_w_down, ln_gain=m_ln_gain, ln_bias=m_ln_bias)
    var = dict(s5_lam_re=v_s5_lam_re, s5_lam_im=v_s5_lam_im, s5_log_dt=v_s5_log_dt, s5_b_re=v_s5_b_re,
               s5_b_im=v_s5_b_im, s5_c_re=v_s5_c_re, s5_c_im=v_s5_c_im, s5_d=v_s5_d, s5_w_glu=v_s5_w_glu,
               s5_b_glu=v_s5_b_glu, s5_w_out=v_s5_w_out, attn_w_kv=v_attn_w_kv, attn_w_q=v_attn_w_q,
               attn_w_out=v_attn_w_out, rel_bias=v_rel_bias, ffn_w_up=v_ffn_w_up, ffn_conv_w=v_ffn_conv_w,
               ffn_conv_b=v_ffn_conv_b, ffn_w_down=v_ffn_w_down, ln_gain=v_ln_gain, ln_bias=v_ln_bias)
    bsz, seq, _ = x.shape
    me = 4 * lax.axis_index("x") + 2 * lax.axis_index("y") + lax.axis_index("c")

    ex = _Exchanges(w)
    loss_part, grad_x, g = _local_step(x, loss_target, dict(w), None, bsz, seq, ex=ex)
    loss = lax.psum(loss_part, ("x", "y", "c"))

    d = D_MODEL
    results = {}
    for name in BIG:
        shard = w[name]
        if name in ("ffn_w_up", "ffn_w_down"):
            parts, r3 = [ex.parts[name + "0"], ex.parts[name + "1"]], shard.shape
        else:
            parts, r3 = [ex.parts[name]], (1, -1, shard.shape[-1])
        outs4 = _adamw("adamw_" + name, parts, shard.reshape(r3), mom[name].reshape(r3), var[name].reshape(r3))
        results[name] = tuple(o.reshape(shard.shape) for o in outs4)

    small_names = SMALL_REPLICATED + SMALL_SHARDED
    packed = _pack([g[k] for k in small_names])
    pad = (-packed.shape[0]) % (8 * N_DEV)
    if pad:
        packed = jnp.concatenate([packed, jnp.zeros((pad, 128), F32)], axis=0)
    piece = packed.shape[0] // N_DEV
    slots = _exchange("reduce_small", [packed.reshape(N_DEV, piece, 128)], [_sds((N_DEV, piece, 128), F32)],
                      [(0, ("peer",), 0, ("me",))])[0]
    summed = _exchange("gather_small", [_sum_slots(slots)], [_sds((N_DEV, piece, 128), F32)],
                       [(0, (), 0, ("me",))])[0]
    total = _unpack(summed.reshape(N_DEV * piece, 128), [g[k].shape for k in small_names])
    gsum = dict(zip(small_names, total))
    mine = {k: gsum[k] for k in SMALL_REPLICATED}
    mine["s5_b_glu"] = lax.dynamic_slice_in_dim(gsum["s5_b_glu"], me * (d // N_DEV), d // N_DEV, axis=1)
    mine["ffn_conv_w"] = lax.dynamic_index_in_dim(gsum["ffn_conv_w"], me, axis=2, keepdims=False)
    mine["ln_gain"] = lax.dynamic_slice_in_dim(gsum["ln_gain"], me * (d // N_DEV), d // N_DEV, axis=2)
    mine["ln_bias"] = lax.dynamic_slice_in_dim(gsum["ln_bias"], me * (d // N_DEV), d // N_DEV, axis=2)
    gp = _pack([mine[k] for k in small_names])
    outs4 = _adamw("adamw_small", [gp[None]], _pack([w[k] for k in small_names])[None],
                   _pack([mom[k] for k in small_names])[None], _pack([var[k] for k in small_names])[None])
    outs4 = [o[0] for o in outs4]
    shapes = [w[k].shape for k in small_names]
    unpacked = [_unpack(o, shapes) for o in outs4]
    for i, k in enumerate(small_names):
        results[k] = tuple(unpacked[j][i] for j in range(4))

    out = [loss, grad_x]
    for j in range(4):
        out.extend(results[k][j] for k in WEIGHTS)
    return tuple(out)
```

```python
import functools
import math

import numpy as np
import jax
import jax.numpy as jnp
from jax import lax
from jax.experimental import pallas as pl
from jax.experimental.pallas import tpu as pltpu

F32 = jnp.float32
BF16 = jnp.bfloat16
HIGHEST = lax.Precision.HIGHEST

N_DEV = 8
D_MODEL = 1024
SSM_GROUP = 16
SSM_STATE = 64
SSM_CHUNKS = 8
CHUNK_CH = D_MODEL // SSM_CHUNKS
CHUNK_ST = CHUNK_CH // SSM_GROUP * SSM_STATE
N_STATE = D_MODEL // SSM_GROUP * SSM_STATE
HEAD_DIM = 64
N_HEADS = 16
BAND = 128
DILATIONS = (1, 4, 16)
REL_BUCKETS = 32
REL_MAX_DIST = 2048
NEG_BIG = -1e30
DN_ALPHA = (2.0 * 2) ** 0.25
LN_EPS = 1e-5
ADAM_LR = 0.001
ADAM_B1 = 0.9
ADAM_B2 = 0.999
ADAM_EPS = 1e-08
ADAM_WD = 0.01
ADAM_STEP = 10

TM = 512
TM_MM = 2048
TM_MM_F32 = 1024
T_SCAN = 256
VMEM_LIMIT = 56 * 1024 * 1024
FFN_HIDDEN = BF16

NN = (((1,), (0,)), ((), ()))
NT = (((1,), (1,)), ((), ()))
TN = (((0,), (0,)), ((), ()))


class _Plan:
    def __init__(self, srcs, out_shapes, n_sems, n_local, phases):
        self.srcs, self.out_shapes, self.n_sems, self.n_local, self.phases = srcs, out_shapes, n_sems, n_local, phases


def _call(body, name, grid, in_specs, out_specs, out_shape, scratch=(), comm=None):
    params = dict(dimension_semantics=("arbitrary",) * len(grid), vmem_limit_bytes=VMEM_LIMIT)
    if comm is None:
        return pl.pallas_call(
            body, name=name, grid=grid, in_specs=in_specs, out_specs=out_specs, out_shape=out_shape,
            scratch_shapes=list(scratch), compiler_params=pltpu.CompilerParams(**params))
    single = not isinstance(out_specs, (list, tuple))
    o_specs = [out_specs] if single else list(out_specs)
    o_shape = [out_shape] if single else list(out_shape)
    n_in, n_out, n_scr = len(in_specs), len(o_specs), len(scratch)
    n_cin, n_cout = len(comm.srcs), len(comm.out_shapes)
    total = int(np.prod(grid))

    def wrapped(*refs):
        ins = refs[:n_in]
        cins = refs[n_in:n_in + n_cin]
        outs = refs[n_in + n_cin:n_in + n_cin + n_out]
        couts = refs[n_in + n_cin + n_out:n_in + n_cin + n_out + n_cout]
        rest = refs[n_in + n_cin + n_out + n_cout:]
        scr, sems = rest[:n_scr], rest[n_scr:]
        step = pl.program_id(0)
        for ax in range(1, len(grid)):
            step = step * grid[ax] + pl.program_id(ax)
        phases = comm.phases(cins, couts, *sems, total)
        for at, action in phases:
            if at == 0:
                pl.when(step == 0)(action)
        body(*ins, *outs, *scr)
        for at, action in phases:
            if at > 0:
                pl.when(step == at)(action)

    any_spec = pl.BlockSpec(memory_space=pl.ANY)
    call = pl.pallas_call(
        wrapped, name=name, grid=grid, in_specs=list(in_specs) + [any_spec] * n_cin,
        out_specs=o_specs + [any_spec] * n_cout, out_shape=o_shape + list(comm.out_shapes),
        scratch_shapes=list(scratch) + [pltpu.SemaphoreType.DMA((comm.n_sems,)),
                                        pltpu.SemaphoreType.DMA((comm.n_sems,)),
                                        pltpu.SemaphoreType.DMA((max(comm.n_local, 1),))],
        compiler_params=pltpu.CompilerParams(has_side_effects=True, **params))

    def run(*args):
        res = call(*args, *comm.srcs)
        own = res[0] if single else res[:n_out]
        return own, res[n_out:]

    return run


def _sds(shape, dtype):
    return jax.ShapeDtypeStruct(shape, dtype)


def _mm(name, a, b, *, dims, grid, a_spec, b_spec, o_spec, out_shape, nred=1, red_axis=0,
        add=None, add_spec=None, add_scale=1.0, comm=None):
    has_add = add is not None
    acc_shape = tuple(s for s in o_spec.block_shape if s is not None)

    def body(*refs):
        if has_add:
            a_ref, b_ref, add_ref, o_ref = refs[:4]
            rest = refs[4:]
        else:
            a_ref, b_ref, o_ref = refs[:3]
            rest = refs[3:]
        prod = lax.dot_general(a_ref[...].astype(BF16), b_ref[...].astype(BF16), dims,
                               preferred_element_type=F32)

        def finish(val):
            if has_add:
                val = val + add_scale * add_ref[...].astype(F32)
            o_ref[...] = val.astype(o_ref.dtype)

        if nred == 1:
            finish(prod)
        else:
            acc = rest[0]
            k = pl.program_id(red_axis)

            @pl.when(k == 0)
            def _():
                acc[...] = prod

            @pl.when(k > 0)
            def _():
                acc[...] += prod

            @pl.when(k == nred - 1)
            def _():
                finish(acc[...])

    in_specs = [a_spec, b_spec] + ([add_spec] if has_add else [])
    scratch = [pltpu.VMEM(acc_shape, F32)] if nred > 1 else []
    args = (a, b) + ((add,) if has_add else ())
    return _call(body, name, grid, in_specs, o_spec, out_shape, scratch, comm=comm)(*args)


def _mm_nn(name, a, b, out_dtype, blocked_out=False, comm=None, blocks=None):
    TM = TM_MM if out_dtype == BF16 else TM_MM_F32
    m, k = a.shape
    _, _, n = b.shape
    nb = b.shape[0] if blocks is None else blocks[1]
    first = 0 if blocks is None else blocks[0]
    if blocked_out:
        o_spec = pl.BlockSpec((None, TM, n), lambda d, i: (d, i, 0))
        out_shape = _sds((nb, m, n), out_dtype)
    else:
        o_spec = pl.BlockSpec((TM, n), lambda d, i: (i, d))
        out_shape = _sds((m, nb * n), out_dtype)
    return _mm(name, a, b, dims=NN, grid=(nb, m // TM),
               a_spec=pl.BlockSpec((TM, k), lambda d, i: (i, 0)),
               b_spec=pl.BlockSpec((None, k, n), lambda d, i: (d + first, 0, 0)),
               o_spec=o_spec, out_shape=out_shape, comm=comm)


def _mm_nn_red(name, a, b, out_dtype, comm=None):
    TM = TM_MM_F32
    nb, m, k = a.shape
    n = b.shape[2]
    return _mm(name, a, b, dims=NN, grid=(m // TM, nb), nred=nb, red_axis=1,
               a_spec=pl.BlockSpec((None, TM, k), lambda i, d: (d, i, 0)),
               b_spec=pl.BlockSpec((None, k, n), lambda i, d: (d, 0, 0)),
               o_spec=pl.BlockSpec((TM, n), lambda i, d: (i, 0)),
               out_shape=_sds((m, n), out_dtype), comm=comm)


def _mm_tn(name, a, g, nb, out_dtype, a_blocked=False, g_mode="cols", comm=None):
    TM = TM_MM
    if a_blocked:
        _, m, ka = a.shape
        a_spec = pl.BlockSpec((None, TM, ka), lambda d, i: (d, i, 0))
    else:
        m, ka = a.shape
        a_spec = pl.BlockSpec((TM, ka), lambda d, i: (i, 0))
    if g_mode == "cols":
        n = g.shape[1] // nb
        g_spec = pl.BlockSpec((TM, n), lambda d, i: (i, d))
    elif g_mode == "blocked":
        n = g.shape[2]
        g_spec = pl.BlockSpec((None, TM, n), lambda d, i: (d, i, 0))
    else:
        n = g.shape[1]
        g_spec = pl.BlockSpec((TM, n), lambda d, i: (i, 0))
    nred = m // TM
    return _mm(name, a, g, dims=TN, grid=(nb, nred), nred=nred, red_axis=1,
               a_spec=a_spec, b_spec=g_spec,
               o_spec=pl.BlockSpec((None, ka, n), lambda d, i: (d, 0, 0)),
               out_shape=_sds((nb, ka, n), out_dtype), comm=comm)


def _mm_nt_red(name, g, w, out_dtype, g_blocked, add=None, add_scale=1.0, comm=None, blocks=None):
    TM = TM_MM_F32
    _, k, n = w.shape
    nb = w.shape[0] if blocks is None else blocks[1]
    first = 0 if blocks is None else blocks[0]
    if g_blocked:
        m = g.shape[1]
        g_spec = pl.BlockSpec((None, TM, n), lambda i, d: (d, i, 0))
    else:
        m = g.shape[0]
        g_spec = pl.BlockSpec((TM, n), lambda i, d: (i, d))
    o_spec = pl.BlockSpec((TM, k), lambda i, d: (i, 0))
    return _mm(name, g, w, dims=NT, grid=(m // TM, nb), nred=nb, red_axis=1,
               a_spec=g_spec, b_spec=pl.BlockSpec((None, k, n), lambda i, d: (d + first, 0, 0)),
               o_spec=o_spec, out_shape=_sds((m, k), out_dtype),
               add=add, add_spec=o_spec if add is not None else None, add_scale=add_scale, comm=comm)


def _mm_nt_out(name, g, w, out_dtype):
    TM = TM_MM
    m, n = g.shape
    nb, k, _ = w.shape
    return _mm(name, g, w, dims=NT, grid=(nb, m // TM),
               a_spec=pl.BlockSpec((TM, n), lambda d, i: (i, 0)),
               b_spec=pl.BlockSpec((None, k, n), lambda d, i: (d, 0, 0)),
               o_spec=pl.BlockSpec((None, TM, k), lambda d, i: (d, i, 0)),
               out_shape=_sds((nb, m, k), out_dtype))


def _ln_stats(x, f):
    r = DN_ALPHA * x + f
    mu = jnp.mean(r, axis=-1, keepdims=True)
    xc = r - mu
    var = jnp.mean(xc * xc, axis=-1, keepdims=True)
    rstd = lax.rsqrt(var + LN_EPS)
    return xc * rstd, rstd


def _ln_fwd(name, xin, f, gain, bias):
    n, d = xin.shape

    def body(x_ref, f_ref, g_ref, b_ref, o_ref, ob_ref):
        xhat, _ = _ln_stats(x_ref[...], f_ref[...])
        out = xhat * g_ref[...] + b_ref[...]
        o_ref[...] = out
        ob_ref[...] = out.astype(BF16)

    row = pl.BlockSpec((TM, d), lambda i: (i, 0))
    vec = pl.BlockSpec((1, d), lambda i: (0, 0))
    return _call(body, name, (n // TM,), [row, row, vec, vec], [row, row],
                 [_sds((n, d), F32), _sds((n, d), BF16)])(xin, f, gain, bias)


def _ln_bwd(name, xin, f, dy, gain):
    n, d = xin.shape

    def body(x_ref, f_ref, dy_ref, g_ref, dr_ref, drb_ref, dg_ref, db_ref):
        i = pl.program_id(0)
        xhat, rstd = _ln_stats(x_ref[...], f_ref[...])
        dy_v = dy_ref[...]
        dxh = dy_v * g_ref[...]
        m1 = jnp.mean(dxh, axis=-1, keepdims=True)
        m2 = jnp.mean(dxh * xhat, axis=-1, keepdims=True)
        dr = rstd * (dxh - m1 - xhat * m2)
        dr_ref[...] = dr
        drb_ref[...] = dr.astype(BF16)
        dg = jnp.sum(dy_v * xhat, axis=0, keepdims=True)
        db = jnp.sum(dy_v, axis=0, keepdims=True)

        @pl.when(i == 0)
        def _():
            dg_ref[...] = dg
            db_ref[...] = db

        @pl.when(i > 0)
        def _():
            dg_ref[...] += dg
            db_ref[...] += db

    row = pl.BlockSpec((TM, d), lambda i: (i, 0))
    vec = pl.BlockSpec((1, d), lambda i: (0, 0))
    return _call(body, name, (n // TM,), [row, row, row, vec], [row, row, vec, vec],
                 [_sds((n, d), F32), _sds((n, d), BF16), _sds((1, d), F32), _sds((1, d), F32)])(xin, f, dy, gain)


def _loss_head(y, target):
    n, d = y.shape

    def body(y_ref, t_ref, l_ref, dy_ref):
        i = pl.program_id(0)
        e = y_ref[...] - t_ref[...]
        dy_ref[...] = e * (1.0 / d)
        part = jnp.sum(jnp.sum(e * e, axis=1, keepdims=True), axis=0, keepdims=True) * (0.5 / d)
        part = jnp.broadcast_to(part, (8, 128))

        @pl.when(i == 0)
        def _():
            l_ref[...] = part

        @pl.when(i > 0)
        def _():
            l_ref[...] += part

    row = pl.BlockSpec((TM, d), lambda i: (i, 0))
    return _call(body, "loss_head", (n // TM,), [row, row],
                 [pl.BlockSpec((8, 128), lambda i: (0, 0)), row],
                 [_sds((8, 128), F32), _sds((n, d), F32)])(y, target)


def _group_expand_matrix():
    e = np.zeros((D_MODEL // SSM_GROUP, N_STATE), np.float32)
    for g in range(D_MODEL // SSM_GROUP):
        e[g, g * SSM_STATE:(g + 1) * SSM_STATE] = 1.0
    return jnp.asarray(e)


def _discretise(lr, li, ldt, br, bi, expand):
    dt = jnp.dot(jnp.exp(ldt), expand, precision=HIGHEST, preferred_element_type=F32)
    mag = jnp.exp(lr * dt)
    th = li * dt
    ab_r = mag * jnp.cos(th)
    ab_i = mag * jnp.sin(th)
    den = lr * lr + li * li
    nr = ab_r - 1.0
    co_r = (nr * lr + ab_i * li) / den
    co_i = (ab_i * lr - nr * li) / den
    bb_r = co_r * br - co_i * bi
    bb_i = co_r * bi + co_i * br
    return ab_r, ab_i, bb_r, bb_i


def _cmul(ar, ai, br, bi):
    return ar * br - ai * bi, ar * bi + ai * br


def _s5_prep(lr, li, ldt, br, bi):
    s = N_STATE
    expand = _group_expand_matrix()

    def body(lr_ref, li_ref, ldt_ref, br_ref, bi_ref, e_ref, abr_ref, abi_ref, bbr_ref, bbi_ref, tab_ref):
        ab_r, ab_i, bb_r, bb_i = _discretise(lr_ref[...], li_ref[...], ldt_ref[...], br_ref[...],
                                             bi_ref[...], e_ref[...])
        abr_ref[...] = ab_r
        abi_ref[...] = ab_i
        bbr_ref[...] = bb_r
        bbi_ref[...] = bb_i
        row = lax.broadcasted_iota(jnp.int32, (8, s), 0)
        for base, sign in ((0, 1.0), (8, -1.0)):
            p = [None] * 9
            p[1] = (ab_r, sign * ab_i)
            p[2] = _cmul(*p[1], *p[1])
            p[3] = _cmul(*p[2], *p[1])
            p[4] = _cmul(*p[2], *p[2])
            p[5] = _cmul(*p[4], *p[1])
            p[6] = _cmul(*p[4], *p[2])
            p[7] = _cmul(*p[4], *p[3])
            p[8] = _cmul(*p[4], *p[4])
            for j, k in enumerate((1, 2, 4)):
                keep = (row >= k) if base == 0 else (row < 8 - k)
                tab_ref[base + 2 * j] = jnp.where(keep, jnp.broadcast_to(p[k][0], (8, s)), 0.0)
                tab_ref[base + 2 * j + 1] = jnp.where(keep, jnp.broadcast_to(p[k][1], (8, s)), 0.0)
            pw_r = jnp.zeros((8, s), F32)
            pw_i = jnp.zeros((8, s), F32)
            for i in range(8):
                e = i + 1 if base == 0 else 8 - i
                pw_r = jnp.where(row == i, jnp.broadcast_to(p[e][0], (8, s)), pw_r)
                pw_i = jnp.where(row == i, jnp.broadcast_to(p[e][1], (8, s)), pw_i)
            tab_ref[base + 6] = pw_r
            tab_ref[base + 7] = pw_i

    def full(shape):
        return pl.BlockSpec(shape, lambda i: (0,) * len(shape))

    g = D_MODEL // SSM_GROUP
    return _call(body, "s5_prep", (1,),
                 [full((1, s)), full((1, s)), full((1, g)), full((16, s)), full((16, s)), full((g, s))],
                 [full((1, s)), full((1, s)), full((16, s)), full((16, s)), full((16, 8, s))],
                 [_sds((1, s), F32), _sds((1, s), F32), _sds((16, s), F32), _sds((16, s), F32),
                  _sds((16, 8, s), F32)])(lr, li, ldt, br, bi, expand)


def _s5_prep_bwd(lr, li, ldt, br, bi, d_abr, d_abi, d_bbr, d_bbi):
    s = N_STATE
    g = D_MODEL // SSM_GROUP
    expand = _group_expand_matrix()

    def body(lr_ref, li_ref, ldt_ref, br_ref, bi_ref, e_ref, c1, c2, c3, c4, o1, o2, o3, o4, o5):
        e_val = e_ref[...]
        _, vjp = jax.vjp(lambda a, b, c, d, e: _discretise(a, b, c, d, e, e_val),
                         lr_ref[...], li_ref[...], ldt_ref[...], br_ref[...], bi_ref[...])
        grads = vjp((c1[...], c2[...], c3[...], c4[...]))
        for o, gr in zip((o1, o2, o3, o4, o5), grads):
            o[...] = gr

    def full(shape):
        return pl.BlockSpec(shape, lambda i: (0,) * len(shape))

    return _call(body, "s5_prep_bwd", (1,),
                 [full((1, s)), full((1, s)), full((1, g)), full((16, s)), full((16, s)), full((g, s)),
                  full((1, s)), full((1, s)), full((16, s)), full((16, s))],
                 [full((1, s)), full((1, s)), full((1, g)), full((16, s)), full((16, s))],
                 [_sds((1, s), F32), _sds((1, s), F32), _sds((1, g), F32), _sds((16, s), F32),
                  _sds((16, s), F32)])(lr, li, ldt, br, bi, expand, d_abr, d_abi, d_bbr, d_bbi)


def _scan8(vr, vi, tab_ref, base, reverse):
    for j, k in enumerate((1, 2, 4)):
        mr = tab_ref[base + 2 * j]
        mi = tab_ref[base + 2 * j + 1]
        shift = 8 - k if reverse else k
        sr = pltpu.roll(vr, shift, 0)
        si = pltpu.roll(vi, shift, 0)
        vr, vi = vr + mr * sr - mi * si, vi + mr * si + mi * sr
    return vr, vi


def _s5_scan_fwd(x, bbr, bbi, cr, ci, dvec, tab, seq, comm=None):
    n, d = x.shape
    nt = seq // T_SCAN
    nblk = T_SCAN // 8
    st = CHUNK_ST

    def body(x_ref, bbr_ref, bbi_ref, cr_ref, ci_ref, d_ref, tab_ref, y_ref, hr_ref, hi_ref, car_r, car_i):
        t = pl.program_id(2)

        @pl.when(t == 0)
        def _():
            car_r[...] = jnp.zeros_like(car_r)
            car_i[...] = jnp.zeros_like(car_i)

        u = x_ref[...]
        ub = u.astype(BF16)
        hr_ref[...] = jnp.dot(ub, bbr_ref[...], preferred_element_type=F32)
        hi_ref[...] = jnp.dot(ub, bbi_ref[...], preferred_element_type=F32)

        def step(i, carry):
            sl = pl.ds(pl.multiple_of(i * 8, 8), 8)
            vr, vi = _scan8(hr_ref[sl, :], hi_ref[sl, :], tab_ref, 0, False)
            c_r = jnp.broadcast_to(car_r[...], (8, st))
            c_i = jnp.broadcast_to(car_i[...], (8, st))
            pr = tab_ref[6]
            pi = tab_ref[7]
            vr, vi = vr + pr * c_r - pi * c_i, vi + pr * c_i + pi * c_r
            hr_ref[sl, :] = vr
            hi_ref[sl, :] = vi
            car_r[...] = vr[7:8, :]
            car_i[...] = vi[7:8, :]
            return carry

        lax.fori_loop(0, nblk, step, 0, unroll=4)
        y = lax.dot_general(hr_ref[...].astype(BF16), cr_ref[...], NT, preferred_element_type=F32)
        y = y - lax.dot_general(hi_ref[...].astype(BF16), ci_ref[...], NT, preferred_element_type=F32)
        y_ref[...] = y + d_ref[...] * u

    row = lambda c, b, t: (b * nt + t, c)
    mat = pl.BlockSpec((None, CHUNK_CH, st), lambda c, b, t: (c, 0, 0))
    return _call(
        body, "s5_scan_fwd", (SSM_CHUNKS, n // seq, nt),
        [pl.BlockSpec((T_SCAN, CHUNK_CH), row), mat, mat, mat, mat,
         pl.BlockSpec((1, CHUNK_CH), lambda c, b, t: (0, c)),
         pl.BlockSpec((16, 8, st), lambda c, b, t: (0, 0, c))],
        [pl.BlockSpec((T_SCAN, CHUNK_CH), row), pl.BlockSpec((T_SCAN, st), row), pl.BlockSpec((T_SCAN, st), row)],
        [_sds((n, d), F32), _sds((n, N_STATE), F32), _sds((n, N_STATE), F32)],
        [pltpu.VMEM((1, st), F32), pltpu.VMEM((1, st), F32)], comm=comm,
    )(x, bbr, bbi, cr, ci, dvec, tab)


def _s5_scan_bwd(x, dy, dres, hr, hi, bbr, bbi, cr, ci, dvec, tab, seq, comm=None):
    n, d = x.shape
    nt = seq // T_SCAN
    nblk = T_SCAN // 8
    st = CHUNK_ST

    def body(x_ref, dy_ref, dres_ref, hr_ref, hi_ref, hpr_ref, hpi_ref, bbr_ref, bbi_ref, cr_ref, ci_ref,
             d_ref, tab_ref, dx_ref, dcr_ref, dci_ref, dbr_ref, dbi_ref, dar_ref, dai_ref, dd_ref,
             g_r, g_i, car_r, car_i):
        b = pl.program_id(1)
        tg = pl.program_id(2)
        first = jnp.logical_and(b == 0, tg == 0)

        @pl.when(tg == 0)
        def _():
            car_r[...] = jnp.zeros_like(car_r)
            car_i[...] = jnp.zeros_like(car_i)

        u = x_ref[...]
        dyv = dy_ref[...]
        ub = u.astype(BF16)
        dyb = dyv.astype(BF16)
        g_r[...] = jnp.dot(dyb, cr_ref[...], preferred_element_type=F32)
        g_i[...] = -jnp.dot(dyb, ci_ref[...], preferred_element_type=F32)

        def step(ii, carry):
            i = nblk - 1 - ii
            sl = pl.ds(pl.multiple_of(i * 8, 8), 8)
            vr, vi = _scan8(g_r[sl, :], g_i[sl, :], tab_ref, 8, True)
            c_r = jnp.broadcast_to(car_r[...], (8, st))
            c_i = jnp.broadcast_to(car_i[...], (8, st))
            pr = tab_ref[14]
            pi = tab_ref[15]
            vr, vi = vr + pr * c_r - pi * c_i, vi + pr * c_i + pi * c_r
            g_r[sl, :] = vr
            g_i[sl, :] = vi
            car_r[...] = vr[0:1, :]
            car_i[...] = vi[0:1, :]
            return carry

        lax.fori_loop(0, nblk, step, 0, unroll=4)
        gr = g_r[...]
        gi = g_i[...]
        grb = gr.astype(BF16)
        gib = gi.astype(BF16)
        du = lax.dot_general(grb, bbr_ref[...], NT, preferred_element_type=F32)
        du = du + lax.dot_general(gib, bbi_ref[...], NT, preferred_element_type=F32)
        dx_ref[...] = DN_ALPHA * dres_ref[...] + du + d_ref[...] * dyv

        h_r = hr_ref[...]
        h_i = hi_ref[...]
        t_orig = nt - 1 - tg
        row0 = lax.broadcasted_iota(jnp.int32, (8, st), 0) == 0
        halo_ok = t_orig > 0

        def previous(h, halo_ref):
            top = jnp.broadcast_to(jnp.where(halo_ok, halo_ref[7:8, :], 0.0), (8, st))
            rolled = pltpu.roll(h, 1, 0)
            return jnp.concatenate([jnp.where(row0, top, rolled[:8, :]), rolled[8:, :]], axis=0)

        hp_r = previous(h_r, hpr_ref)
        hp_i = previous(h_i, hpi_ref)
        dar = jnp.sum(gr * hp_r + gi * hp_i, axis=0, keepdims=True)
        dai = jnp.sum(gi * hp_r - gr * hp_i, axis=0, keepdims=True)
        dcr = lax.dot_general(dyb, h_r.astype(BF16), TN, preferred_element_type=F32)
        dci = lax.dot_general(dyb, h_i.astype(BF16), TN, preferred_element_type=F32)
        dbr = lax.dot_general(ub, grb, TN, preferred_element_type=F32)
        dbi = lax.dot_general(ub, gib, TN, preferred_element_type=F32)
        dd = jnp.sum(dyv * u, axis=0, keepdims=True)

        @pl.when(first)
        def _():
            dcr_ref[...] = dcr
            dci_ref[...] = dci
            dbr_ref[...] = dbr
            dbi_ref[...] = dbi
            dar_ref[...] = dar
            dai_ref[...] = dai
            dd_ref[...] = dd

        @pl.when(jnp.logical_not(first))
        def _():
            dcr_ref[...] += dcr
            dci_ref[...] += dci
            dbr_ref[...] += dbr
            dbi_ref[...] += dbi
            dar_ref[...] += dar
            dai_ref[...] += dai
            dd_ref[...] += dd

    row = lambda c, b, t: (b * nt + (nt - 1 - t), c)
    halo = lambda c, b, t: (jnp.maximum((b * nt + (nt - 1 - t)) * (T_SCAN // 8) - 1, 0), c)
    mat = pl.BlockSpec((None, CHUNK_CH, st), lambda c, b, t: (c, 0, 0))
    chan = pl.BlockSpec((T_SCAN, CHUNK_CH), row)
    state = pl.BlockSpec((T_SCAN, st), row)
    return _call(
        body, "s5_scan_bwd", (SSM_CHUNKS, n // seq, nt),
        [chan, chan, chan, state, state, pl.BlockSpec((8, st), halo), pl.BlockSpec((8, st), halo),
         mat, mat, mat, mat, pl.BlockSpec((1, CHUNK_CH), lambda c, b, t: (0, c)),
         pl.BlockSpec((16, 8, st), lambda c, b, t: (0, 0, c))],
        [chan, mat, mat, mat, mat, pl.BlockSpec((1, st), lambda c, b, t: (0, c)),
         pl.BlockSpec((1, st), lambda c, b, t: (0, c)), pl.BlockSpec((1, CHUNK_CH), lambda c, b, t: (0, c))],
        [_sds((n, d), F32)] + [_sds((SSM_CHUNKS, CHUNK_CH, st), F32)] * 4
        + [_sds((1, N_STATE), F32), _sds((1, N_STATE), F32), _sds((1, d), F32)],
        [pltpu.VMEM((T_SCAN, st), F32), pltpu.VMEM((T_SCAN, st), F32), pltpu.VMEM((1, st), F32),
         pltpu.VMEM((1, st), F32)], comm=comm,
    )(x, dy, dres, hr, hi, hr, hi, bbr, bbi, cr, ci, dvec, tab)


_GELU_C = math.sqrt(2.0 / math.pi)


def _gelu_parts(y):
    t = jnp.tanh(_GELU_C * (y + 0.044715 * (y * y * y)))
    return 0.5 * (1.0 + t), t


def _glu_fwd(y, w_glu, b_glu):
    n, d = y.shape

    def body(y_ref, w_ref, b_ref, y2_ref, z_ref, gg_ref):
        yv = y_ref[...]
        cdf, _ = _gelu_parts(yv)
        y2 = yv * cdf
        z = jnp.dot(y2.astype(BF16), w_ref[...], preferred_element_type=F32) + b_ref[...]
        y2_ref[...] = y2.astype(BF16)
        z_ref[...] = z
        gg_ref[...] = (y2 * jax.nn.sigmoid(z)).astype(BF16)

    row = pl.BlockSpec((TM, d), lambda i: (i, 0))
    return _call(body, "glu_fwd", (n // TM,),
                 [row, pl.BlockSpec((d, d), lambda i: (0, 0)), pl.BlockSpec((1, d), lambda i: (0, 0))],
                 [row, row, row], [_sds((n, d), BF16), _sds((n, d), F32), _sds((n, d), BF16)])(y, w_glu, b_glu)


def _glu_bwd_gate(y, z, dgg):
    n, d = y.shape

    def body(y_ref, z_ref, dgg_ref, dz_ref, db_ref, t_ref):
        i = pl.program_id(0)
        yv = y_ref[...]
        cdf, _ = _gelu_parts(yv)
        y2 = yv * cdf
        s = jax.nn.sigmoid(z_ref[...])
        dg = dgg_ref[...]
        dz = dg * y2 * s * (1.0 - s)
        dz_ref[...] = dz.astype(BF16)
        t_ref[...] = dg * s
        db = jnp.sum(dz, axis=0, keepdims=True)

        @pl.when(i == 0)
        def _():
            db_ref[...] = db

        @pl.when(i > 0)
        def _():
            db_ref[...] += db

    row = pl.BlockSpec((TM, d), lambda i: (i, 0))
    vec = pl.BlockSpec((1, d), lambda i: (0, 0))
    return _call(body, "glu_bwd_gate", (n // TM,), [row, row, row], [row, vec, row],
                 [_sds((n, d), BF16), _sds((1, d), F32), _sds((n, d), F32)])(y, z, dgg)


def _gelu_bwd(y, dy2):
    n, d = y.shape

    def body(y_ref, g_ref, o_ref):
        yv = y_ref[...]
        cdf, t = _gelu_parts(yv)
        dinner = _GELU_C * (1.0 + 3.0 * 0.044715 * yv * yv)
        o_ref[...] = g_ref[...] * (cdf + 0.5 * yv * (1.0 - t * t) * dinner)

    row = pl.BlockSpec((TM, d), lambda i: (i, 0))
    return _call(body, "gelu_bwd", (n // TM,), [row, row], row, _sds((n, d), F32))(y, dy2)


HALO = 16


def _shift_down(x, halo, k):
    out = pltpu.roll(x, k, 0)
    row = lax.broadcasted_iota(jnp.int32, (8, x.shape[1]), 0)
    top = jnp.where(row < k, pltpu.roll(halo[HALO - 8:HALO, :], k, 0), out[:8, :])
    return jnp.concatenate([top, out[8:, :]], axis=0)


def _conv3(x, halo, w, b):
    x1 = _shift_down(x, halo, 1)
    x2 = _shift_down(x, halo, 2)
    return b + w[0:1, :] * x + w[1:2, :] * x1 + w[2:3, :] * x2, x1, x2


def _conv_gate_fwd(hc, conv_w, conv_b, seq):
    _, nb, n, c = hc.shape
    tiles_per_seq = seq // TM

    def body(x_ref, halo_ref, w_ref, b_ref, o_ref):
        i = pl.program_id(1)
        start = (i % tiles_per_seq) == 0
        cv = []
        for h in range(2):
            halo = jnp.where(start, 0.0, halo_ref[h].astype(F32))
            cv.append(_conv3(x_ref[h].astype(F32), halo, w_ref[h], b_ref[h])[0])
        o_ref[...] = (cv[1] * jax.nn.sigmoid(cv[1]) * cv[0]).astype(BF16)

    return _call(
        body, "conv_gate_fwd", (nb, n // TM),
        [pl.BlockSpec((2, None, TM, c), lambda d, i: (0, d, i, 0)),
         pl.BlockSpec((2, None, HALO, c), lambda d, i: (0, d, jnp.maximum(i * (TM // HALO) - 1, 0), 0)),
         pl.BlockSpec((2, None, 3, c), lambda d, i: (0, d, 0, 0)),
         pl.BlockSpec((2, None, 1, c), lambda d, i: (0, d, 0, 0))],
        pl.BlockSpec((None, TM, c), lambda d, i: (d, i, 0)),
        _sds((nb, n, c), BF16))(hc, hc, conv_w, conv_b)


def _conv_gate_bwd(hc, dact, conv_w, conv_b, seq, comm=None):
    _, nb, n, c = hc.shape
    tiles_per_seq = seq // TM
    last_halo = n // HALO - 1
    ext = TM + HALO

    def body(x_ref, prev_ref, next_ref, da_ref, dan_ref, w_ref, b_ref, dx_ref, dw_ref):
        i = pl.program_id(1)
        start = (i % tiles_per_seq) == 0
        end = (i % tiles_per_seq) == tiles_per_seq - 1
        row = lax.broadcasted_iota(jnp.int32, (ext, c), 0)
        xe, cv = [], []
        for h in range(2):
            halo = jnp.where(start, 0.0, prev_ref[h].astype(F32))
            x_ext = jnp.concatenate([x_ref[h], next_ref[h]], axis=0).astype(F32)
            conv, x1, x2 = _conv3(x_ext, halo, w_ref[h], b_ref[h])
            xe.append((x_ext, x1, x2))
            cv.append(conv)
        da = jnp.concatenate([da_ref[...], dan_ref[...]], axis=0).astype(F32)
        da = jnp.where(jnp.logical_and(end, row >= TM), 0.0, da)
        sg = jax.nn.sigmoid(cv[1])
        silu = cv[1] * sg
        dcv = [da * silu, da * cv[0] * (sg + silu * (1.0 - sg))]
        for h in range(2):
            w = w_ref[h]
            g = dcv[h]
            dx = w[0:1, :] * g + w[1:2, :] * pltpu.roll(g, ext - 1, 0) + w[2:3, :] * pltpu.roll(g, ext - 2, 0)
            dx_ref[h] = dx[:TM, :].astype(BF16)
            x_ext, x1, x2 = xe[h]
            gt = g[:TM, :]
            parts = [jnp.sum(gt * x_ext[:TM, :], axis=0, keepdims=True),
                     jnp.sum(gt * x1[:TM, :], axis=0, keepdims=True),
                     jnp.sum(gt * x2[:TM, :], axis=0, keepdims=True),
                     jnp.sum(gt, axis=0, keepdims=True)]
            upd = jnp.concatenate(parts + [jnp.zeros((4, c), F32)], axis=0)

            @pl.when(i == 0)
            def _():
                dw_ref[h] = upd

            @pl.when(i > 0)
            def _():
                dw_ref[h] += upd

    nxt = lambda i: jnp.minimum((i + 1) * (TM // HALO), last_halo)
    return _call(
        body, "conv_gate_bwd", (nb, n // TM),
        [pl.BlockSpec((2, None, TM, c), lambda d, i: (0, d, i, 0)),
         pl.BlockSpec((2, None, HALO, c), lambda d, i: (0, d, jnp.maximum(i * (TM // HALO) - 1, 0), 0)),
         pl.BlockSpec((2, None, HALO, c), lambda d, i: (0, d, nxt(i), 0)),
         pl.BlockSpec((None, TM, c), lambda d, i: (d, i, 0)),
         pl.BlockSpec((None, HALO, c), lambda d, i: (d, nxt(i), 0)),
         pl.BlockSpec((2, None, 3, c), lambda d, i: (0, d, 0, 0)),
         pl.BlockSpec((2, None, 1, c), lambda d, i: (0, d, 0, 0))],
        [pl.BlockSpec((2, None, TM, c), lambda d, i: (0, d, i, 0)),
         pl.BlockSpec((2, None, 8, c), lambda d, i: (0, d, 0, 0))],
        [_sds((2, nb, n, c), BF16), _sds((2, nb, 8, c), F32)], comm=comm)(hc, hc, hc, dact, dact, conv_w, conv_b)


def _t5_bucket_np(dist):
    exact = REL_BUCKETS // 2
    d = np.maximum(dist, 1).astype(np.float32)
    large = exact + (np.log(d / exact) / math.log(REL_MAX_DIST / exact) * (REL_BUCKETS - exact)).astype(np.int64)
    large = np.minimum(large, REL_BUCKETS - 1)
    return np.where(dist < exact, dist, large).astype(np.int32)


def _bucket_table(dil):
    steps = np.arange(BAND)[:, None] + BAND - np.arange(2 * BAND)[None, :]
    return _t5_bucket_np(np.maximum(steps, 0) * dil).reshape(1, BAND * 2 * BAND).astype(np.float32)


def _bias_expand(rel_t, dil):
    nk = BAND * 2 * BAND
    bucket = jnp.asarray(_bucket_table(dil))

    def body(r_ref, bk_ref, o_ref):
        ids = lax.broadcasted_iota(jnp.int32, (REL_BUCKETS, nk), 0).astype(F32)
        onehot = (ids == bk_ref[...]).astype(F32)
        o_ref[...] = jnp.dot(r_ref[...], onehot, precision=HIGHEST, preferred_element_type=F32)

    return _call(body, "bias_expand", (1,),
                 [pl.BlockSpec((N_HEADS, REL_BUCKETS), lambda i: (0, 0)), pl.BlockSpec((1, nk), lambda i: (0, 0))],
                 pl.BlockSpec((N_HEADS, nk), lambda i: (0, 0)), _sds((N_HEADS, nk), F32))(rel_t, bucket)


def _bias_reduce(dbias, dil):
    nk = BAND * 2 * BAND
    bucket = jnp.asarray(_bucket_table(dil))

    def body(g_ref, bk_ref, o_ref):
        ids = lax.broadcasted_iota(jnp.int32, (REL_BUCKETS, nk), 0).astype(F32)
        onehot = (ids == bk_ref[...]).astype(F32)
        o_ref[...] = lax.dot_general(g_ref[...], onehot, NT, precision=HIGHEST, preferred_element_type=F32)

    return _call(body, "bias_reduce", (1,),
                 [pl.BlockSpec((N_HEADS, nk), lambda i: (0, 0)), pl.BlockSpec((1, nk), lambda i: (0, 0))],
                 pl.BlockSpec((N_HEADS, REL_BUCKETS), lambda i: (0, 0)),
                 _sds((N_HEADS, REL_BUCKETS), F32))(dbias, bucket)


def _valid_mask(n):
    qi = lax.broadcasted_iota(jnp.int32, (BAND, 2 * BAND), 0)
    kj = lax.broadcasted_iota(jnp.int32, (BAND, 2 * BAND), 1)
    steps = qi + BAND - kj
    in_band = jnp.logical_and(steps >= 0, steps <= BAND)
    return jnp.logical_and(in_band, jnp.logical_or(n > 0, kj >= BAND))


ATTN_COLS = {1: 1024, 4: 512, 16: 128}


def _residue(dil, r):
    return slice(None) if dil == 1 else pl.ds(r, BAND, stride=dil)


def _stack_heads(x, first):
    return jnp.concatenate([jnp.where(first, x, 0.0), jnp.where(first, 0.0, x)], axis=0)


def _attn_fwd(q, k, v, bias, g, dil, bsz, seq):
    n = q.shape[0]
    rows = BAND * dil
    nch = seq // rows
    w = N_HEADS * HEAD_DIM
    wc = ATTN_COLS[dil]
    ncol = w // wc
    hpc = wc // HEAD_DIM
    nj = wc // 128
    slab = lambda k: pltpu.VMEM((k * nj, rows, 128), F32)

    def body(q_ref, kc_ref, kp_ref, vc_ref, vp_ref, b_ref, o_ref, l_ref, qf, kf, vf, of, lf):
        ch = pl.program_id(2)
        valid = _valid_mask(ch)
        valid = jnp.concatenate([valid, valid], axis=0)
        lane = lax.broadcasted_iota(jnp.int32, (BAND, 128), 1)
        first = lane < HEAD_DIM
        for j in range(nj):
            sl = slice(128 * j, 128 * (j + 1))
            qf[j] = q_ref[:, sl].astype(F32)
            kf[j] = kp_ref[:, sl].astype(F32)
            kf[nj + j] = kc_ref[:, sl].astype(F32)
            vf[j] = vp_ref[:, sl].astype(F32)
            vf[nj + j] = vc_ref[:, sl].astype(F32)
        for r in range(dil):
            rr = _residue(dil, r)
            for j in range(nj):
                q2 = qf.at[j][rr, :]
                k2 = jnp.concatenate([kf.at[j][rr, :], kf.at[nj + j][rr, :]], axis=0).astype(BF16)
                v2 = jnp.concatenate([vf.at[j][rr, :], vf.at[nj + j][rr, :]], axis=0).astype(BF16)
                qs = _stack_heads(q2, first).astype(BF16)
                s = lax.dot_general(qs, k2, NT, preferred_element_type=F32) * (HEAD_DIM ** -0.5)
                s = jnp.where(valid, s + b_ref[2 * j:2 * j + 2].reshape(2 * BAND, 2 * BAND), NEG_BIG)
                m = jnp.max(s, axis=1, keepdims=True)
                p = jnp.exp(s - m)
                l = jnp.sum(p, axis=1, keepdims=True)
                pv = jnp.dot(p.astype(BF16), v2, preferred_element_type=F32) / l
                lse = jnp.broadcast_to(m + jnp.log(l), (2 * BAND, 128))
                of.at[j][rr, :] = jnp.where(first, pv[:BAND], pv[BAND:])
                lf.at[j][rr, :] = jnp.where(first, lse[:BAND], lse[BAND:])
        for j in range(nj):
            sl = slice(128 * j, 128 * (j + 1))
            o_ref[:, sl] = of[j]
            l_ref[:, sl] = lf[j]

    base = g * ncol
    cur = lambda c, b, ch: (b * nch + ch, base + c)
    prev = lambda c, b, ch: (b * nch + jnp.maximum(ch - 1, 0), base + c)
    blk = lambda im: pl.BlockSpec((rows, wc), im)
    out_blk = pl.BlockSpec((rows, wc), lambda c, b, ch: (b * nch + ch, c))
    return _call(
        body, "attn_fwd_d%d" % dil, (ncol, bsz, nch),
        [blk(cur), blk(cur), blk(prev), blk(cur), blk(prev),
         pl.BlockSpec((hpc, BAND, 2 * BAND), lambda c, b, ch: (c, 0, 0))],
        [out_blk, out_blk],
        [_sds((n, w), F32)] * 2,
        [slab(1), slab(2), slab(2), slab(1), slab(1)],
    )(q, k, k, v, v, bias)


def _attn_combine(outs, lses):
    n, w = outs[0].shape

    def body(o0, o1, o2, l0, l1, l2, o_ref, l_ref):
        la, lb, lc = l0[...], l1[...], l2[...]
        m = jnp.maximum(jnp.maximum(la, lb), lc)
        wa, wb, wc = jnp.exp(la - m), jnp.exp(lb - m), jnp.exp(lc - m)
        tot = wa + wb + wc
        o_ref[...] = (wa * o0[...] + wb * o1[...] + wc * o2[...]) / tot
        l_ref[...] = m + jnp.log(tot)

    row = pl.BlockSpec((TM, w), lambda i: (i, 0))
    return _call(body, "attn_combine", (n // TM,), [row] * 6, [row, row],
                 [_sds((n, w), F32)] * 2)(*outs, *lses)


def _attn_bwd(q, k, v, do, o, lse, bias, g, dil, bsz, seq, comm=None):
    n = q.shape[0]
    rows = BAND * dil
    nch = seq // rows
    w = N_HEADS * HEAD_DIM
    wc = ATTN_COLS[dil]
    ncol = w // wc
    hpc = wc // HEAD_DIM
    nj = wc // 128
    slab = lambda k: pltpu.VMEM((k * nj, rows, 128), F32)
    scale = HEAD_DIM ** -0.5

    def body(q_ref, kc_ref, kp_ref, vc_ref, vp_ref, do_ref, o_ref, l_ref, b_ref,
             dq_ref, dk_ref, dv_ref, db_ref, qf, kf, vf, dof, of, lf, dqf, dkf, dvf):
        b = pl.program_id(1)
        ch = pl.program_id(2)
        cur = ch % 2
        prv = 1 - cur

        @pl.when(jnp.logical_and(b == 0, ch == 0))
        def _():
            db_ref[...] = jnp.zeros_like(db_ref)

        @pl.when(ch == 0)
        def _():
            for j in range(nj):
                dkf[nj + j] = jnp.zeros((rows, 128), F32)
                dvf[nj + j] = jnp.zeros((rows, 128), F32)

        @pl.when(ch < nch)
        def _():
            valid = _valid_mask(ch)
            valid = jnp.concatenate([valid, valid], axis=0)
            first = lax.broadcasted_iota(jnp.int32, (BAND, 128), 1) < HEAD_DIM
            for j in range(nj):
                sl = slice(128 * j, 128 * (j + 1))
                qf[j] = q_ref[:, sl].astype(F32)
                kf[j] = kp_ref[:, sl].astype(F32)
                kf[nj + j] = kc_ref[:, sl].astype(F32)
                vf[j] = vp_ref[:, sl].astype(F32)
                vf[nj + j] = vc_ref[:, sl].astype(F32)
                dof[j] = do_ref[:, sl]
                of[j] = o_ref[:, sl]
                lf[j] = l_ref[:, sl]
            for r in range(dil):
                rr = _residue(dil, r)
                for j in range(nj):
                    k2 = jnp.concatenate([kf.at[j][rr, :], kf.at[nj + j][rr, :]], axis=0).astype(BF16)
                    v2 = jnp.concatenate([vf.at[j][rr, :], vf.at[nj + j][rr, :]], axis=0).astype(BF16)
                    do2 = dof.at[j][rr, :]
                    lse2 = lf.at[j][rr, :]
                    qs = _stack_heads(qf.at[j][rr, :], first).astype(BF16)
                    dos = _stack_heads(do2, first)
                    delta = jnp.sum(dos * jnp.concatenate([of.at[j][rr, :]] * 2, axis=0), axis=1, keepdims=True)
                    dos = dos.astype(BF16)
                    lse_col = jnp.concatenate([lse2[:, 0:1], lse2[:, HEAD_DIM:HEAD_DIM + 1]], axis=0)
                    s = lax.dot_general(qs, k2, NT, preferred_element_type=F32) * scale
                    s = jnp.where(valid, s + b_ref[2 * j:2 * j + 2].reshape(2 * BAND, 2 * BAND), NEG_BIG)
                    p = jnp.exp(s - lse_col)
                    dp = lax.dot_general(dos, v2, NT, preferred_element_type=F32)
                    ds = p * (dp - delta)
                    db_ref[2 * j:2 * j + 2] += ds.reshape(2, BAND, 2 * BAND)
                    dsb = ds.astype(BF16)
                    dq2 = jnp.dot(dsb, k2, preferred_element_type=F32) * scale
                    dk2 = lax.dot_general(dsb, qs, TN, preferred_element_type=F32) * scale
                    dv2 = lax.dot_general(p.astype(BF16), dos, TN, preferred_element_type=F32)
                    dqf.at[j][rr, :] = jnp.where(first, dq2[:BAND], dq2[BAND:])
                    dkf.at[prv * nj + j][rr, :] += dk2[:BAND, :]
                    dvf.at[prv * nj + j][rr, :] += dv2[:BAND, :]
                    dkf.at[cur * nj + j][rr, :] = dk2[BAND:, :]
                    dvf.at[cur * nj + j][rr, :] = dv2[BAND:, :]
            for j in range(nj):
                dq_ref[:, 128 * j:128 * (j + 1)] = dqf[j].astype(BF16)

        for j in range(nj):
            sl = slice(128 * j, 128 * (j + 1))
            dk_ref[:, sl] = dkf[prv * nj + j].astype(BF16)
            dv_ref[:, sl] = dvf[prv * nj + j].astype(BF16)

    last = nch - 1
    base = g * ncol
    cur3 = lambda c, b, ch: (b * nch + jnp.minimum(ch, last), base + c)
    prev3 = lambda c, b, ch: (b * nch + jnp.maximum(jnp.minimum(ch, last) - 1, 0), base + c)
    cur1 = lambda c, b, ch: (b * nch + jnp.minimum(ch, last), c)
    lag1 = lambda c, b, ch: (b * nch + jnp.maximum(ch - 1, 0), c)
    blk = lambda im: pl.BlockSpec((rows, wc), im)
    bias_blk = pl.BlockSpec((hpc, BAND, 2 * BAND), lambda c, b, ch: (c, 0, 0))
    out = _sds((n, w), BF16)
    return _call(
        body, "attn_bwd_d%d" % dil, (ncol, bsz, nch + 1),
        [blk(cur3), blk(cur3), blk(prev3), blk(cur3), blk(prev3), blk(cur1), blk(cur1), blk(cur1), bias_blk],
        [blk(cur1), blk(lag1), blk(lag1), bias_blk],
        [out, out, out, _sds((N_HEADS, BAND, 2 * BAND), F32)],
        [slab(1), slab(2), slab(2), slab(1), slab(1), slab(1), slab(1), slab(2), slab(2)], comm=comm,
    )(q, k, k, v, v, do, o, lse, bias)


def _adamw(name, parts, w, m, v):
    nl, r, c = w.shape
    nparts = parts[0].shape[0]
    tr = r
    for cand in (256, 352, 128):
        if r % cand == 0 and r > cand:
            tr = cand
            break
    c1 = 1.0 - ADAM_B1 ** ADAM_STEP
    c2 = 1.0 - ADAM_B2 ** ADAM_STEP

    def body(*refs):
        p_refs = refs[:nl]
        w_ref, m_ref, v_ref, g_ref, d_ref, nm_ref, nv_ref = refs[nl:]
        layer = pl.program_id(0)
        for l in range(nl):
            @pl.when(layer == l)
            def _():
                g = p_refs[l][0].astype(F32)
                for dev in range(1, nparts):
                    g = g + p_refs[l][dev].astype(F32)
                nm = ADAM_B1 * m_ref[...] + (1.0 - ADAM_B1) * g
                nv = ADAM_B2 * v_ref[...] + (1.0 - ADAM_B2) * (g * g)
                m_hat = nm / c1
                v_hat = nv / c2
                g_ref[...] = g
                d_ref[...] = -ADAM_LR * (m_hat / (jnp.sqrt(v_hat) + ADAM_EPS) + ADAM_WD * w_ref[...])
                nm_ref[...] = nm
                nv_ref[...] = nv

    def part_spec(l):
        return pl.BlockSpec((nparts, tr, c), lambda layer, i: (0, jnp.where(layer == l, i, 0), 0))

    row = pl.BlockSpec((None, tr, c), lambda layer, i: (layer, i, 0))
    return _call(body, name, (nl, r // tr), [part_spec(l) for l in range(nl)] + [row, row, row],
                 [row] * 4, [_sds((nl, r, c), F32)] * 4)(*parts, w, m, v)


def _exchange(name, srcs, out_shapes, items):
    n_in = len(srcs)
    n_out = len(out_shapes)
    n_items = len(items)

    def body(*refs):
        ins = refs[:n_in]
        outs = refs[n_in:n_in + n_out]
        send_sems, recv_sems, local_sems = refs[n_in + n_out:]
        x, y, c = lax.axis_index("x"), lax.axis_index("y"), lax.axis_index("c")
        me = 4 * x + 2 * y + c

        def pick(ref, sel, peer):
            idx = tuple(peer if s == "peer" else me if s == "me" else s for s in sel)
            return ref.at[idx] if idx else ref

        copies = []
        for it, (si, ssel, oi, osel) in enumerate(items):
            local = pltpu.make_async_copy(pick(ins[si], ssel, me), pick(outs[oi], osel, me), local_sems.at[it])
            local.start()
            copies.append(local)
            for j in range(1, N_DEV):
                px = 1 - x if j & 4 else x
                py = 1 - y if j & 2 else y
                pc = 1 - c if j & 1 else c
                peer = 4 * px + 2 * py + pc
                sem = it * (N_DEV - 1) + j - 1
                cp = pltpu.make_async_remote_copy(
                    src_ref=pick(ins[si], ssel, peer), dst_ref=pick(outs[oi], osel, peer),
                    send_sem=send_sems.at[sem], recv_sem=recv_sems.at[sem],
                    device_id=(px, py, pc), device_id_type=pl.DeviceIdType.MESH)
                cp.start()
                copies.append(cp)
        for cp in copies:
            cp.wait()

    any_spec = pl.BlockSpec(memory_space=pl.ANY)
    return pl.pallas_call(
        body, name=name, in_specs=[any_spec] * n_in, out_specs=[any_spec] * n_out, out_shape=list(out_shapes),
        scratch_shapes=[pltpu.SemaphoreType.DMA((n_items * (N_DEV - 1),)),
                        pltpu.SemaphoreType.DMA((n_items * (N_DEV - 1),)),
                        pltpu.SemaphoreType.DMA((n_items,))],
        compiler_params=pltpu.CompilerParams(has_side_effects=True),
    )(*srcs)


def _sel(ref, sel, slot=None):
    idx = tuple(slot if s == "slot" else s for s in sel)
    return ref.at[idx] if idx else ref


def _gather_plan(srcs, out_shapes, items):
    per = N_DEV - 1
    n_items = len(items)

    def phases(ins, outs, send_sems, recv_sems, local_sems, total):
        x, y, c = lax.axis_index("x"), lax.axis_index("y"), lax.axis_index("c")
        sibling = (x, y, 1 - c)
        chips = [(1 - x, y), (x, 1 - y), (1 - x, 1 - y)]
        slot_of = lambda px, py, pc: 4 * px + 2 * py + pc

        def copy(it, k, block, to, src=None):
            _, _, oi, osel = items[it]
            dst = _sel(outs[oi], osel, slot_of(*block))
            return pltpu.make_async_remote_copy(
                src_ref=dst if src is None else src, dst_ref=dst,
                send_sem=send_sems.at[it * per + k], recv_sem=recv_sems.at[it * per + k],
                device_id=to, device_id_type=pl.DeviceIdType.MESH)

        def own_copies(it):
            si, ssel, oi, osel = items[it]
            src = _sel(ins[si], ssel)
            mine = pltpu.make_async_copy(src, _sel(outs[oi], osel, slot_of(x, y, c)), local_sems.at[it])
            first = [copy(it, 0, (x, y, c), sibling, src=src)]
            first += [copy(it, 1 + j, (x, y, c), (*chip, c), src=src) for j, chip in enumerate(chips)]
            return mine, first

        def start():
            for it in range(n_items):
                mine, first = own_copies(it)
                mine.start()
                for cp in first:
                    cp.start()

        def forward(it):
            def go():
                for j, chip in enumerate(chips):
                    copy(it, 1 + j, (*chip, c), (x, y, c)).wait_recv()
                    copy(it, 4 + j, (*chip, c), sibling).start()
            return go

        def finish():
            for it in range(n_items):
                copy(it, 0, sibling, (x, y, c)).wait_recv()
                for j, chip in enumerate(chips):
                    copy(it, 4 + j, (*chip, 1 - c), (x, y, c)).wait_recv()
            for it in range(n_items):
                mine, first = own_copies(it)
                for cp in first:
                    cp.wait_send()
                for j, chip in enumerate(chips):
                    copy(it, 4 + j, (*chip, c), sibling).wait_send()
                mine.wait()

        lo = max(total // 3, 1)
        acts = [(0, start)]
        for it in range(n_items):
            acts.append((min(lo + it * (total - 1 - lo) // n_items, total - 2), forward(it)))
        acts.append((total - 1, finish))
        return acts

    return _Plan(srcs, out_shapes, n_items * per, n_items, phases)


def _scatter_plan(srcs, out_shapes, items):
    per = N_DEV - 1
    n_items = len(items)

    def phases(ins, outs, send_sems, recv_sems, local_sems, total):
        x, y, c = lax.axis_index("x"), lax.axis_index("y"), lax.axis_index("c")
        me = 4 * x + 2 * y + c

        def copies():
            out = []
            for it, (si, oi, osel) in enumerate(items):
                dst = _sel(outs[oi], osel, me)
                out.append(pltpu.make_async_copy(ins[si].at[me], dst, local_sems.at[it]))
                for j in range(1, N_DEV):
                    px = 1 - x if j & 4 else x
                    py = 1 - y if j & 2 else y
                    pc = 1 - c if j & 1 else c
                    out.append(pltpu.make_async_remote_copy(
                        src_ref=ins[si].at[4 * px + 2 * py + pc], dst_ref=dst,
                        send_sem=send_sems.at[it * per + j - 1], recv_sem=recv_sems.at[it * per + j - 1],
                        device_id=(px, py, pc), device_id_type=pl.DeviceIdType.MESH))
            return out

        def start():
            for cp in copies():
                cp.start()

        def finish():
            for cp in copies():
                cp.wait()

        return [(0, start), (total - 1, finish)]

    return _Plan(srcs, out_shapes, n_items * per, n_items, phases)


def _sum_slots(parts):
    _, r, c = parts.shape

    def body(p_ref, o_ref):
        acc = p_ref[0]
        for dev in range(1, N_DEV):
            acc = acc + p_ref[dev]
        o_ref[...] = acc

    return _call(body, "sum_slots", (1,), [pl.BlockSpec((N_DEV, r, c), lambda i: (0, 0, 0))],
                 pl.BlockSpec((r, c), lambda i: (0, 0)), _sds((r, c), F32))(parts)


def _pack(arrays):
    rows = []
    for a in arrays:
        flat = a.reshape(-1).astype(F32)
        pad = (-flat.shape[0]) % 1024
        if pad:
            flat = jnp.concatenate([flat, jnp.zeros((pad,), F32)])
        rows.append(flat.reshape(-1, 128))
    return jnp.concatenate(rows, axis=0)


def _unpack(buf, shapes):
    out = []
    r0 = 0
    for shp in shapes:
        size = int(np.prod(shp))
        nrows = -(-size // 1024) * 8
        out.append(buf[r0:r0 + nrows].reshape(-1)[:size].reshape(shp))
        r0 += nrows
    return out


def _block_diag_chunks(val):
    eye = jnp.eye(8, dtype=val.dtype)
    return jnp.einsum("cjhp,jk->cjhkp", val, eye).reshape(SSM_CHUNKS, CHUNK_CH, CHUNK_ST)


def _diag_blocks(dense):
    d5 = dense.reshape(SSM_CHUNKS, 8, SSM_GROUP, 8, SSM_STATE)
    return jnp.stack([d5[:, j, :, j, :] for j in range(8)], axis=1)


def _carried(res, comm):
    return res if comm is not None else (res, None)


def _ffn_fwd(tag, hin, hin_b, w_up, conv_w, conv_b, w_down4, gain, bias, seq, comm_up=None, comm_down=None):
    hc, got_up = _carried(_mm_nn("ffn_up" + tag, hin_b, w_up, FFN_HIDDEN, blocked_out=True, comm=comm_up), comm_up)
    nb, n, c = hc.shape
    hc = hc.reshape(2, nb // 2, n, c)
    act = _conv_gate_fwd(hc, conv_w, conv_b, seq)
    ffn, got_down = _carried(_mm_nn_red("ffn_down" + tag, act, w_down4, F32, comm=comm_down), comm_down)
    hout, hout_b = _ln_fwd("ffn_norm" + tag, hin, ffn, gain, bias)
    return hout, hout_b, (hin, hin_b, hc, act, ffn), (got_up, got_down)


def _ffn_bwd(tag, saved, dh, w_up, conv_w, conv_b, w_down4, gain, seq, comm_conv=None, comm_dw=None,
             comm_dx=None):
    hin, hin_b, hc, act, ffn = saved
    dr, dr_b, dgain, dbias = _ln_bwd("ffn_norm_bwd" + tag, hin, ffn, dh, gain)
    d_wdown = _mm_tn("ffn_down_dw" + tag, act, dr_b, act.shape[0], BF16, a_blocked=True, g_mode="shared")
    dact = _mm_nt_out("ffn_down_dx" + tag, dr_b, w_down4, FFN_HIDDEN)
    (dhc, dconv), got_conv = _carried(_conv_gate_bwd(hc, dact, conv_w, conv_b, seq, comm=comm_conv), comm_conv)
    dhc8 = dhc.reshape(2 * dhc.shape[1], dhc.shape[2], dhc.shape[3])
    d_wup, got_dw = _carried(_mm_tn("ffn_up_dw" + tag, hin_b, dhc8, dhc8.shape[0], BF16, g_mode="blocked",
                                    comm=comm_dw), comm_dw)
    dhin, got_dx = _carried(_mm_nt_red("ffn_up_dx" + tag, dhc8, w_up, F32, g_blocked=True, add=dr,
                                       add_scale=DN_ALPHA, comm=comm_dx), comm_dx)
    return dhin, d_wup, d_wdown, dconv, dgain, dbias, (got_conv, got_dw, got_dx)


def _local_step(x, target, p, wfull, bsz, seq, ex=None):
    n = bsz * seq
    xf = x.reshape(n, D_MODEL)
    tf = target.reshape(n, D_MODEL)

    lr = p["s5_lam_re"].reshape(1, N_STATE)
    li = p["s5_lam_im"].reshape(1, N_STATE)
    ldt = p["s5_log_dt"].reshape(1, D_MODEL // SSM_GROUP)
    br_t = p["s5_b_re"].reshape(N_STATE, SSM_GROUP).T
    bi_t = p["s5_b_im"].reshape(N_STATE, SSM_GROUP).T
    ab_r, ab_i, bb_r, bb_i, tab = _s5_prep(lr, li, ldt, br_t, bi_t)

    def bb_chunks(bb):
        return _block_diag_chunks(bb.reshape(SSM_GROUP, SSM_CHUNKS, 8, SSM_STATE).transpose(1, 2, 0, 3))

    def c_chunks(cc):
        return _block_diag_chunks(cc.reshape(SSM_CHUNKS, 8, SSM_GROUP, SSM_STATE))

    bbr_c, bbi_c = bb_chunks(bb_r).astype(BF16), bb_chunks(bb_i).astype(BF16)
    cr_c, ci_c = c_chunks(p["s5_c_re"]).astype(BF16), c_chunks(p["s5_c_im"]).astype(BF16)
    dvec = p["s5_d"].reshape(1, D_MODEL)

    if ex is None:
        y, h_r, h_i = _s5_scan_fwd(xf, bbr_c, bbi_c, cr_c, ci_c, dvec, tab, seq)
    else:
        (y, h_r, h_i), gathered = _s5_scan_fwd(xf, bbr_c, bbi_c, cr_c, ci_c, dvec, tab, seq,
                                               comm=ex.gather_first())
        p, wfull = ex.take_first(gathered, p)
    ln_g = p["ln_gain"].reshape(4, 1, D_MODEL)
    ln_b = p["ln_bias"].reshape(4, 1, D_MODEL)
    y2, z, gg = _glu_fwd(y, wfull["s5_w_glu"], p["s5_b_glu"].reshape(1, D_MODEL))
    mix = _mm_nn("s5_out", gg, wfull["s5_w_out"][None], F32)
    h1, h1b = _ln_fwd("s5_norm", xf, mix, ln_g[0], ln_b[0])
    h2, h2b, ffn0_saved, gathered = _ffn_fwd("0", h1, h1b, wfull["ffn_w_up"][0], p["conv_w"][0], p["conv_b"][0],
                                             wfull["ffn_w_down"][0], ln_g[1], ln_b[1], seq,
                                             comm_up=ex.gather_layer1("up") if ex else None,
                                             comm_down=ex.gather_layer1("down") if ex else None)
    if ex is not None:
        wfull = ex.take_layer1(gathered, wfull)

    kmat = _mm_nn("attn_k", h2b, wfull["attn_w_kv"], BF16, blocks=(0, 4))
    vmat = _mm_nn("attn_v", h2b, wfull["attn_w_kv"], BF16, blocks=(4, 4))
    qmat = _mm_nn("attn_q", h2b, wfull["attn_w_q"], BF16)
    biases, outs, lses = [], [], []
    for g, dil in enumerate(DILATIONS):
        rel_t = p["rel_bias"][:, g * N_HEADS:(g + 1) * N_HEADS].T
        biases.append(_bias_expand(rel_t, dil).reshape(N_HEADS, BAND, 2 * BAND))
        o_g, l_g = _attn_fwd(qmat, kmat, vmat, biases[g], g, dil, bsz, seq)
        outs.append(o_g)
        lses.append(l_g)
    o, lse = _attn_combine(outs, lses)
    attn_out = _mm_nn("attn_out", o, wfull["attn_w_out"][None], F32)
    h3, h3b = _ln_fwd("attn_norm", h2, attn_out, ln_g[2], ln_b[2])
    h4, _, ffn1_saved, _ = _ffn_fwd("1", h3, h3b, wfull["ffn_w_up"][1], p["conv_w"][1], p["conv_b"][1],
                                    wfull["ffn_w_down"][1], ln_g[3], ln_b[3], seq)
    loss_tile, dh4 = _loss_head(h4, tf)

    grads = {}
    dh3, grads["ffn_w_up1"], grads["ffn_w_down1"], dconv1, dg3, db3, _ = _ffn_bwd(
        "1", ffn1_saved, dh4, wfull["ffn_w_up"][1], p["conv_w"][1], p["conv_b"][1], wfull["ffn_w_down"][1],
        ln_g[3], seq)
    dr3, dr3b, dg2, db2 = _ln_bwd("attn_norm_bwd", h2, attn_out, dh3, ln_g[2])
    grads["attn_w_out"] = _mm_tn("attn_out_dw", o, dr3b, 1, BF16, g_mode="shared")
    do = _mm_nt_red("attn_out_dx", dr3b, wfull["attn_w_out"][None], F32, g_blocked=False)
    dqs, dks, dvs, drel = [], [], [], []
    for g, dil in enumerate(DILATIONS):
        plan = ex.scatter(grads, ("ffn_w_up1", "ffn_w_down1")) if ex and dil == DILATIONS[-1] else None
        res = _attn_bwd(qmat, kmat, vmat, do, o, lse, biases[g], g, dil, bsz, seq, comm=plan)
        if plan is not None:
            res, got = res
            ex.keep(plan, got)
        dq_g, dk_g, dv_g, db_g = res
        dqs.append(dq_g)
        dks.append(dk_g)
        dvs.append(dv_g)
        drel.append(_bias_reduce(db_g.reshape(N_HEADS, BAND * 2 * BAND), dil).T)
    dq = jnp.concatenate(dqs, axis=1)
    dk = jnp.concatenate(dks, axis=1)
    dv = jnp.concatenate(dvs, axis=1)
    grads["rel_bias"] = jnp.concatenate(drel, axis=1)
    grads["attn_w_q"] = _mm_tn("attn_q_dw", h2b, dq, N_DEV, BF16)
    grads["attn_w_kv"] = jnp.concatenate([_mm_tn("attn_k_dw", h2b, dk, 4, BF16),
                                          _mm_tn("attn_v_dw", h2b, dv, 4, BF16)], axis=0)
    dh2 = _mm_nt_red("attn_q_dx", dq, wfull["attn_w_q"], F32, g_blocked=False, add=dr3, add_scale=DN_ALPHA)
    dh2 = _mm_nt_red("attn_k_dx", dk, wfull["attn_w_kv"], F32, g_blocked=False, add=dh2, blocks=(0, 4))
    dh2 = _mm_nt_red("attn_v_dx", dv, wfull["attn_w_kv"], F32, g_blocked=False, add=dh2, blocks=(4, 4))

    plans = [ex.scatter(grads, names) if ex else None
             for names in (("attn_w_kv",), ("attn_w_out",), ("attn_w_q",))]
    dh1, grads["ffn_w_up0"], grads["ffn_w_down0"], dconv0, dg1, db1, got = _ffn_bwd(
        "0", ffn0_saved, dh2, wfull["ffn_w_up"][0], p["conv_w"][0], p["conv_b"][0], wfull["ffn_w_down"][0],
        ln_g[1], seq, comm_conv=plans[0], comm_dw=plans[1], comm_dx=plans[2])
    if ex is not None:
        for plan, outs in zip(plans, got):
            ex.keep(plan, outs)
    dr1, dr1b, dg0, db0 = _ln_bwd("s5_norm_bwd", xf, mix, dh1, ln_g[0])
    grads["s5_w_out"] = _mm_tn("s5_out_dw", gg, dr1b, 1, BF16, g_mode="shared")
    dgg = _mm_nt_red("s5_out_dx", dr1b, wfull["s5_w_out"][None], F32, g_blocked=False)
    dz, d_bglu, dy2_direct = _glu_bwd_gate(y, z, dgg)
    grads["s5_w_glu"] = _mm_tn("s5_glu_dw", y2, dz, 1, BF16, g_mode="shared")
    dy2 = _mm_nt_red("s5_glu_dx", dz, wfull["s5_w_glu"][None], F32, g_blocked=False, add=dy2_direct)
    dy = _gelu_bwd(y, dy2)
    plan_last = ex.scatter(grads, ("ffn_w_up0", "ffn_w_down0", "s5_w_out", "s5_w_glu")) if ex else None
    res = _s5_scan_bwd(xf, dy, dr1, h_r, h_i, bbr_c, bbi_c, cr_c, ci_c, dvec, tab, seq, comm=plan_last)
    if ex is not None:
        res, got = res
        ex.keep(plan_last, got)
    dx, dcr, dci, dbr, dbi, dar, dai, dd = res

    def bb_from_chunks(dense):
        return _diag_blocks(dense).transpose(2, 0, 1, 3).reshape(SSM_GROUP, N_STATE)

    d_lr, d_li, d_ldt, d_br, d_bi = _s5_prep_bwd(lr, li, ldt, br_t, bi_t, dar, dai,
                                                 bb_from_chunks(dbr), bb_from_chunks(dbi))
    grads["s5_lam_re"] = d_lr.reshape(p["s5_lam_re"].shape)
    grads["s5_lam_im"] = d_li.reshape(p["s5_lam_im"].shape)
    grads["s5_log_dt"] = d_ldt.reshape(p["s5_log_dt"].shape)
    grads["s5_b_re"] = d_br.T.reshape(p["s5_b_re"].shape)
    grads["s5_b_im"] = d_bi.T.reshape(p["s5_b_im"].shape)
    grads["s5_c_re"] = _diag_blocks(dcr).reshape(p["s5_c_re"].shape)
    grads["s5_c_im"] = -_diag_blocks(dci).reshape(p["s5_c_im"].shape)
    grads["s5_d"] = dd.reshape(p["s5_d"].shape)
    grads["s5_b_glu"] = d_bglu.reshape(1, D_MODEL)
    dconv = jnp.stack([dconv0, dconv1]).reshape(2, N_DEV, 8, -1)
    grads["ffn_conv_w"] = dconv[:, :, 0:3, :].transpose(0, 2, 1, 3)
    grads["ffn_conv_b"] = dconv[:, :, 3, :]
    grads["ln_gain"] = jnp.stack([dg0, dg1, dg2, dg3]).reshape(2, 2, D_MODEL)
    grads["ln_bias"] = jnp.stack([db0, db1, db2, db3]).reshape(2, 2, D_MODEL)
    return loss_tile[0, 0], dx.reshape(x.shape), grads


SMALL_REPLICATED = ("s5_lam_re", "s5_lam_im", "s5_log_dt", "s5_b_re", "s5_b_im", "s5_c_re", "s5_c_im", "s5_d",
                    "rel_bias", "ffn_conv_b")
SMALL_SHARDED = ("s5_b_glu", "ffn_conv_w", "ln_gain", "ln_bias")
BIG = ("s5_w_glu", "s5_w_out", "attn_w_kv", "attn_w_q", "attn_w_out", "ffn_w_up", "ffn_w_down")
WEIGHTS = ("s5_lam_re", "s5_lam_im", "s5_log_dt", "s5_b_re", "s5_b_im", "s5_c_re", "s5_c_im", "s5_d", "s5_w_glu",
           "s5_b_glu", "s5_w_out", "attn_w_kv", "attn_w_q", "attn_w_out", "rel_bias", "ffn_w_up", "ffn_conv_w",
           "ffn_conv_b", "ffn_w_down", "ln_gain", "ln_bias")


class _Exchanges:
    def __init__(self, w):
        self.w = w
        self.parts = {}
        self.up = w["ffn_w_up"].astype(BF16)
        self.down = w["ffn_w_down"].astype(BF16)

    def gather_first(self):
        w = self.w
        srcs = [w["s5_w_glu"][0].astype(BF16), w["s5_w_out"][0].astype(BF16),
                _pack([w[k] for k in SMALL_SHARDED]), self.up, self.down, w["attn_w_kv"].astype(BF16),
                w["attn_w_q"][0].astype(BF16), w["attn_w_out"][0].astype(BF16)]
        outs = [_sds((N_DEV,) + (s.shape[1:] if i in (3, 4) else s.shape), s.dtype) for i, s in enumerate(srcs)]
        items = [(i, (0,) if i in (3, 4) else (), i, ("slot",)) for i in range(len(srcs))]
        return _gather_plan(srcs, outs, items)

    def take_first(self, g, p):
        d = D_MODEL
        w = self.w
        wfull = {
            "s5_w_glu": g[0].reshape(d, d),
            "s5_w_out": g[1].reshape(d, d),
            "ffn_w_up": [g[3], None],
            "ffn_w_down": [g[4].reshape(4, -1, d), None],
            "attn_w_kv": g[5],
            "attn_w_q": g[6],
            "attn_w_out": g[7].reshape(d, d),
        }
        smalls = [_unpack(g[2][dev], [w[k].shape for k in SMALL_SHARDED]) for dev in range(N_DEV)]
        c_ff = w["ffn_conv_w"].shape[2]
        conv_w = jnp.stack([s[1] for s in smalls], axis=2)
        p = dict(p)
        p.update(s5_b_glu=jnp.concatenate([s[0] for s in smalls], axis=1),
                 ln_gain=jnp.concatenate([s[2] for s in smalls], axis=2),
                 ln_bias=jnp.concatenate([s[3] for s in smalls], axis=2),
                 conv_w=conv_w.transpose(0, 2, 1, 3).reshape(2, 2, 4, 3, c_ff),
                 conv_b=w["ffn_conv_b"].reshape(2, 2, 4, 1, c_ff))
        return p, wfull

    def gather_layer1(self, which):
        src = self.up if which == "up" else self.down
        return _gather_plan([src], [_sds((N_DEV,) + src.shape[1:], BF16)], [(0, (1,), 0, ("slot",))])

    def take_layer1(self, got, wfull):
        (up,), (down,) = got
        wfull = dict(wfull)
        wfull["ffn_w_up"] = [wfull["ffn_w_up"][0], up]
        wfull["ffn_w_down"] = [wfull["ffn_w_down"][0], down.reshape(4, -1, D_MODEL)]
        return wfull

    def scatter(self, grads, names):
        srcs = [grads[k].reshape(N_DEV, -1, grads[k].shape[-1]) for k in names]
        plan = _scatter_plan(srcs, [_sds(s.shape, BF16) for s in srcs],
                             [(i, i, ("slot",)) for i in range(len(names))])
        plan.names = names
        return plan

    def keep(self, plan, outs):
        for k, o in zip(plan.names, outs):
            self.parts[k] = o


def kernel(x, s5_lam_re, s5_lam_im, s5_log_dt, s5_b_re, s5_b_im, s5_c_re, s5_c_im, s5_d, s5_w_glu, s5_b_glu, s5_w_out, attn_w_kv, attn_w_q, attn_w_out, rel_bias, ffn_w_up, ffn_conv_w, ffn_conv_b, ffn_w_down, ln_gain, ln_bias, loss_target, m_s5_lam_re, m_s5_lam_im, m_s5_log_dt, m_s5_b_re, m_s5_b_im, m_s5_c_re, m_s5_c_im, m_s5_d, m_s5_w_glu, m_s5_b_glu, m_s5_w_out, m_attn_w_kv, m_attn_w_q, m_attn_w_out, m_rel_bias, m_ffn_w_up, m_ffn_conv_w, m_ffn_conv_b, m_ffn_w_down, m_ln_gain, m_ln_bias, v_s5_lam_re, v_s5_lam_im, v_s5_log_dt, v_s5_b_re, v_s5_b_im, v_s5_c_re, v_s5_c_im, v_s5_d, v_s5_w_glu, v_s5_b_glu, v_s5_w_out, v_attn_w_kv, v_attn_w_q, v_attn_w_out, v_rel_bias, v_ffn_w_up, v_ffn_conv_w, v_ffn_conv_b, v_ffn_w_down, v_ln_gain, v_ln_bias):
    w = dict(s5_lam_re=s5_lam_re, s5_lam_im=s5_lam_im, s5_log_dt=s5_log_dt, s5_b_re=s5_b_re, s5_b_im=s5_b_im,
             s5_c_re=s5_c_re, s5_c_im=s5_c_im, s5_d=s5_d, s5_w_glu=s5_w_glu, s5_b_glu=s5_b_glu, s5_w_out=s5_w_out,
             attn_w_kv=attn_w_kv, attn_w_q=attn_w_q, attn_w_out=attn_w_out, rel_bias=rel_bias, ffn_w_up=ffn_w_up,
             ffn_conv_w=ffn_conv_w, ffn_conv_b=ffn_conv_b, ffn_w_down=ffn_w_down, ln_gain=ln_gain, ln_bias=ln_bias)
    mom = dict(s5_lam_re=m_s5_lam_re, s5_lam_im=m_s5_lam_im, s5_log_dt=m_s5_log_dt, s5_b_re=m_s5_b_re,
               s5_b_im=m_s5_b_im, s5_c_re=m_s5_c_re, s5_c_im=m_s5_c_im, s5_d=m_s5_d, s5_w_glu=m_s5_w_glu,
               s5_b_glu=m_s5_b_glu, s5_w_out=m_s5_w_out, attn_w_kv=m_attn_w_kv, attn_w_q=m_attn_w_q,
               attn_w_out=m_attn_w_out, rel_bias=m_rel_bias, ffn_w_up=m_ffn_w_up, ffn_conv_w=m_ffn_conv_w,
               ffn_conv_b=m_ffn_conv_b, ffn_w_down=m_ffn_w_down, ln_gain=m_ln_gain, ln_bias=m_ln_bias)
    var = dict(s5_lam_re=v_s5_lam_re, s5_lam_im=v_s5_lam_im, s5_log_dt=v_s5_log_dt, s5_b_re=v_s5_b_re,
               s5_b_im=v_s5_b_im, s5_c_re=v_s5_c_re, s5_c_im=v_s5_c_im, s5_d=v_s5_d, s5_w_glu=v_s5_w_glu,
               s5_b_glu=v_s5_b_glu, s5_w_out=v_s5_w_out, attn_w_kv=v_attn_w_kv, attn_w_q=v_attn_w_q,
               attn_w_out=v_attn_w_out, rel_bias=v_rel_bias, ffn_w_up=v_ffn_w_up, ffn_conv_w=v_ffn_conv_w,
               ffn_conv_b=v_ffn_conv_b, ffn_w_down=v_ffn_w_down, ln_gain=v_ln_gain, ln_bias=v_ln_bias)
    bsz, seq, _ = x.shape
    me = 4 * lax.axis_index("x") + 2 * lax.axis_index("y") + lax.axis_index("c")

    ex = _Exchanges(w)
    loss_part, grad_x, g = _local_step(x, loss_target, dict(w), None, bsz, seq, ex=ex)
    loss = lax.psum(loss_part, ("x", "y", "c"))

    d = D_MODEL
    results = {}
    for name in BIG:
        shard = w[name]
        if name in ("ffn_w_up", "ffn_w_down"):
            parts, r3 = [ex.parts[name + "0"], ex.parts[name + "1"]], shard.shape
        else:
            parts, r3 = [ex.parts[name]], (1, -1, shard.shape[-1])
        outs4 = _adamw("adamw_" + name, parts, shard.reshape(r3), mom[name].reshape(r3), var[name].reshape(r3))
        results[name] = tuple(o.reshape(shard.shape) for o in outs4)

    small_names = SMALL_REPLICATED + SMALL_SHARDED
    packed = _pack([g[k] for k in small_names])
    pad = (-packed.shape[0]) % (8 * N_DEV)
    if pad:
        packed = jnp.concatenate([packed, jnp.zeros((pad, 128), F32)], axis=0)
    piece = packed.shape[0] // N_DEV
    slots = _exchange("reduce_small", [packed.reshape(N_DEV, piece, 128)], [_sds((N_DEV, piece, 128), F32)],
                      [(0, ("peer",), 0, ("me",))])[0]
    summed = _exchange("gather_small", [_sum_slots(slots)], [_sds((N_DEV, piece, 128), F32)],
                       [(0, (), 0, ("me",))])[0]
    total = _unpack(summed.reshape(N_DEV * piece, 128), [g[k].shape for k in small_names])
    gsum = dict(zip(small_names, total))
    mine = {k: gsum[k] for k in SMALL_REPLICATED}
    mine["s5_b_glu"] = lax.dynamic_slice_in_dim(gsum["s5_b_glu"], me * (d // N_DEV), d // N_DEV, axis=1)
    mine["ffn_conv_w"] = lax.dynamic_index_in_dim(gsum["ffn_conv_w"], me, axis=2, keepdims=False)
    mine["ln_gain"] = lax.dynamic_slice_in_dim(gsum["ln_gain"], me * (d // N_DEV), d // N_DEV, axis=2)
    mine["ln_bias"] = lax.dynamic_slice_in_dim(gsum["ln_bias"], me * (d // N_DEV), d // N_DEV, axis=2)
    gp = _pack([mine[k] for k in small_names])
    outs4 = _adamw("adamw_small", [gp[None]], _pack([w[k] for k in small_names])[None],
                   _pack([mom[k] for k in small_names])[None], _pack([var[k] for k in small_names])[None])
    outs4 = [o[0] for o in outs4]
    shapes = [w[k].shape for k in small_names]
    unpacked = [_unpack(o, shapes) for o in outs4]
    for i, k in enumerate(small_names):
        results[k] = tuple(unpacked[j][i] for j in range(4))

    out = [loss, grad_x]
    for j in range(4):
        out.extend(results[k][j] for k in WEIGHTS)
    return tuple(out)
```

```python
import functools
import math

import numpy as np
import jax
import jax.numpy as jnp
from jax import lax
from jax.experimental import pallas as pl
from jax.experimental.pallas import tpu as pltpu

F32 = jnp.float32
BF16 = jnp.bfloat16
HIGHEST = lax.Precision.HIGHEST

N_DEV = 8
D_MODEL = 1024
SSM_GROUP = 16
SSM_STATE = 64
SSM_CHUNKS = 8
CHUNK_CH = D_MODEL // SSM_CHUNKS
CHUNK_ST = CHUNK_CH // SSM_GROUP * SSM_STATE
N_STATE = D_MODEL // SSM_GROUP * SSM_STATE
HEAD_DIM = 64
N_HEADS = 16
BAND = 128
DILATIONS = (1, 4, 16)
REL_BUCKETS = 32
REL_MAX_DIST = 2048
NEG_BIG = -1e30
DN_ALPHA = (2.0 * 2) ** 0.25
LN_EPS = 1e-5
ADAM_LR = 0.001
ADAM_B1 = 0.9
ADAM_B2 = 0.999
ADAM_EPS = 1e-08
ADAM_WD = 0.01
ADAM_STEP = 10

TM = 512
TM_MM = 2048
TM_MM_F32 = 1024
T_SCAN = 512
VMEM_LIMIT = 56 * 1024 * 1024
FFN_HIDDEN = BF16

NN = (((1,), (0,)), ((), ()))
NT = (((1,), (1,)), ((), ()))
TN = (((0,), (0,)), ((), ()))


class _Plan:
    def __init__(self, srcs, out_shapes, n_sems, n_local, phases):
        self.srcs, self.out_shapes, self.n_sems, self.n_local, self.phases = srcs, out_shapes, n_sems, n_local, phases


def _call(body, name, grid, in_specs, out_specs, out_shape, scratch=(), comm=None, fill=()):
    params = dict(dimension_semantics=("arbitrary",) * len(grid), vmem_limit_bytes=VMEM_LIMIT)
    if comm is None and not fill:
        return pl.pallas_call(
            body, name=name, grid=grid, in_specs=in_specs, out_specs=out_specs, out_shape=out_shape,
            scratch_shapes=list(scratch), compiler_params=pltpu.CompilerParams(**params))
    single = not isinstance(out_specs, (list, tuple))
    o_specs = [out_specs] if single else list(out_specs)
    o_shape = [out_shape] if single else list(out_shape)
    n_in, n_out, n_scr, n_fill = len(in_specs), len(o_specs), len(scratch), len(fill)
    srcs = list(comm.srcs) if comm else []
    c_shapes = list(comm.out_shapes) if comm else []
    n_cin, n_cout = len(srcs), len(c_shapes)
    total = int(np.prod(grid))

    def wrapped(*refs):
        ins = refs[:n_in]
        refs = refs[n_in + n_fill:]
        cins, outs, couts = refs[:n_cin], refs[n_cin:n_cin + n_out], refs[n_cin + n_out:n_cin + n_out + n_cout]
        rest = refs[n_cin + n_out + n_cout:]
        scr, sems = rest[:n_scr], rest[n_scr:]
        if comm is None:
            body(*ins, *outs, *scr)
            return
        step = pl.program_id(0)
        for ax in range(1, len(grid)):
            step = step * grid[ax] + pl.program_id(ax)
        phases = comm.phases(cins, couts, *sems, total)
        for at, action in phases:
            if at == 0:
                pl.when(step == 0)(action)
        body(*ins, *outs, *scr)
        for at, action in phases:
            if at > 0:
                pl.when(step == at)(action)

    any_spec = pl.BlockSpec(memory_space=pl.ANY)
    sems = [] if comm is None else [pltpu.SemaphoreType.DMA((comm.n_sems,)), pltpu.SemaphoreType.DMA((comm.n_sems,)),
                                    pltpu.SemaphoreType.DMA((max(comm.n_local, 1),))]
    call = pl.pallas_call(
        wrapped, name=name, grid=grid, in_specs=list(in_specs) + [any_spec] * (n_fill + n_cin),
        out_specs=o_specs + [any_spec] * n_cout, out_shape=o_shape + c_shapes,
        scratch_shapes=list(scratch) + sems, input_output_aliases={n_in + j: j for j in range(n_fill)},
        compiler_params=pltpu.CompilerParams(has_side_effects=comm is not None, **params))

    def run(*args):
        res = call(*args, *fill, *srcs)
        own = res[0] if single else res[:n_out]
        return (own, res[n_out:]) if comm is not None else own

    return run


def _sds(shape, dtype):
    return jax.ShapeDtypeStruct(shape, dtype)


def _mm(name, a, b, *, dims, grid, a_spec, b_spec, o_spec, out_shape, nred=1, red_axis=0,
        add=None, add_spec=None, add_scale=1.0, comm=None):
    has_add = add is not None
    acc_shape = tuple(s for s in o_spec.block_shape if s is not None)

    def body(*refs):
        if has_add:
            a_ref, b_ref, add_ref, o_ref = refs[:4]
            rest = refs[4:]
        else:
            a_ref, b_ref, o_ref = refs[:3]
            rest = refs[3:]
        prod = lax.dot_general(a_ref[...].astype(BF16), b_ref[...].astype(BF16), dims,
                               preferred_element_type=F32)

        def finish(val):
            if has_add:
                val = val + add_scale * add_ref[...].astype(F32)
            o_ref[...] = val.astype(o_ref.dtype)

        if nred == 1:
            finish(prod)
        else:
            acc = rest[0]
            k = pl.program_id(red_axis)

            @pl.when(k == 0)
            def _():
                acc[...] = prod

            @pl.when(k > 0)
            def _():
                acc[...] += prod

            @pl.when(k == nred - 1)
            def _():
                finish(acc[...])

    in_specs = [a_spec, b_spec] + ([add_spec] if has_add else [])
    scratch = [pltpu.VMEM(acc_shape, F32)] if nred > 1 else []
    args = (a, b) + ((add,) if has_add else ())
    return _call(body, name, grid, in_specs, o_spec, out_shape, scratch, comm=comm)(*args)


def _mm_nn(name, a, b, out_dtype, blocked_out=False, comm=None, blocks=None):
    TM = TM_MM if out_dtype == BF16 else TM_MM_F32
    m, k = a.shape
    _, _, n = b.shape
    nb = b.shape[0] if blocks is None else blocks[1]
    first = 0 if blocks is None else blocks[0]
    if blocked_out:
        o_spec = pl.BlockSpec((None, TM, n), lambda d, i: (d, i, 0))
        out_shape = _sds((nb, m, n), out_dtype)
    else:
        o_spec = pl.BlockSpec((TM, n), lambda d, i: (i, d))
        out_shape = _sds((m, nb * n), out_dtype)
    return _mm(name, a, b, dims=NN, grid=(nb, m // TM),
               a_spec=pl.BlockSpec((TM, k), lambda d, i: (i, 0)),
               b_spec=pl.BlockSpec((None, k, n), lambda d, i: (d + first, 0, 0)),
               o_spec=o_spec, out_shape=out_shape, comm=comm)


def _mm_nn_red(name, a, b, out_dtype, comm=None):
    TM = TM_MM_F32
    nb, m, k = a.shape
    n = b.shape[2]
    return _mm(name, a, b, dims=NN, grid=(m // TM, nb), nred=nb, red_axis=1,
               a_spec=pl.BlockSpec((None, TM, k), lambda i, d: (d, i, 0)),
               b_spec=pl.BlockSpec((None, k, n), lambda i, d: (d, 0, 0)),
               o_spec=pl.BlockSpec((TM, n), lambda i, d: (i, 0)),
               out_shape=_sds((m, n), out_dtype), comm=comm)


def _mm_tn(name, a, g, nb, out_dtype, a_blocked=False, g_mode="cols", comm=None):
    TM = TM_MM
    if a_blocked:
        _, m, ka = a.shape
        a_spec = pl.BlockSpec((None, TM, ka), lambda d, i: (d, i, 0))
    else:
        m, ka = a.shape
        a_spec = pl.BlockSpec((TM, ka), lambda d, i: (i, 0))
    if g_mode == "cols":
        n = g.shape[1] // nb
        g_spec = pl.BlockSpec((TM, n), lambda d, i: (i, d))
    elif g_mode == "blocked":
        n = g.shape[2]
        g_spec = pl.BlockSpec((None, TM, n), lambda d, i: (d, i, 0))
    else:
        n = g.shape[1]
        g_spec = pl.BlockSpec((TM, n), lambda d, i: (i, 0))
    nred = m // TM
    return _mm(name, a, g, dims=TN, grid=(nb, nred), nred=nred, red_axis=1,
               a_spec=a_spec, b_spec=g_spec,
               o_spec=pl.BlockSpec((None, ka, n), lambda d, i: (d, 0, 0)),
               out_shape=_sds((nb, ka, n), out_dtype), comm=comm)


def _mm_nt_red(name, g, w, out_dtype, g_blocked, add=None, add_scale=1.0, comm=None, blocks=None):
    TM = TM_MM_F32
    _, k, n = w.shape
    nb = w.shape[0] if blocks is None else blocks[1]
    first = 0 if blocks is None else blocks[0]
    if g_blocked:
        m = g.shape[1]
        g_spec = pl.BlockSpec((None, TM, n), lambda i, d: (d, i, 0))
    else:
        m = g.shape[0]
        g_spec = pl.BlockSpec((TM, n), lambda i, d: (i, d))
    o_spec = pl.BlockSpec((TM, k), lambda i, d: (i, 0))
    return _mm(name, g, w, dims=NT, grid=(m // TM, nb), nred=nb, red_axis=1,
               a_spec=g_spec, b_spec=pl.BlockSpec((None, k, n), lambda i, d: (d + first, 0, 0)),
               o_spec=o_spec, out_shape=_sds((m, k), out_dtype),
               add=add, add_spec=o_spec if add is not None else None, add_scale=add_scale, comm=comm)


def _mm_nt_out(name, g, w, out_dtype):
    TM = TM_MM
    m, n = g.shape
    nb, k, _ = w.shape
    return _mm(name, g, w, dims=NT, grid=(nb, m // TM),
               a_spec=pl.BlockSpec((TM, n), lambda d, i: (i, 0)),
               b_spec=pl.BlockSpec((None, k, n), lambda d, i: (d, 0, 0)),
               o_spec=pl.BlockSpec((None, TM, k), lambda d, i: (d, i, 0)),
               out_shape=_sds((nb, m, k), out_dtype))


def _ln_stats(x, f):
    r = DN_ALPHA * x + f
    mu = jnp.mean(r, axis=-1, keepdims=True)
    xc = r - mu
    var = jnp.mean(xc * xc, axis=-1, keepdims=True)
    rstd = lax.rsqrt(var + LN_EPS)
    return xc * rstd, rstd


def _ln_fwd(name, xin, f, gain, bias):
    n, d = xin.shape

    def body(x_ref, f_ref, g_ref, b_ref, o_ref, ob_ref):
        xhat, _ = _ln_stats(x_ref[...], f_ref[...])
        out = xhat * g_ref[...] + b_ref[...]
        o_ref[...] = out
        ob_ref[...] = out.astype(BF16)

    row = pl.BlockSpec((TM, d), lambda i: (i, 0))
    vec = pl.BlockSpec((1, d), lambda i: (0, 0))
    return _call(body, name, (n // TM,), [row, row, vec, vec], [row, row],
                 [_sds((n, d), F32), _sds((n, d), BF16)])(xin, f, gain, bias)


def _ln_bwd(name, xin, f, dy, gain):
    n, d = xin.shape

    def body(x_ref, f_ref, dy_ref, g_ref, dr_ref, drb_ref, dg_ref, db_ref):
        i = pl.program_id(0)
        xhat, rstd = _ln_stats(x_ref[...], f_ref[...])
        dy_v = dy_ref[...]
        dxh = dy_v * g_ref[...]
        m1 = jnp.mean(dxh, axis=-1, keepdims=True)
        m2 = jnp.mean(dxh * xhat, axis=-1, keepdims=True)
        dr = rstd * (dxh - m1 - xhat * m2)
        dr_ref[...] = dr
        drb_ref[...] = dr.astype(BF16)
        dg = jnp.sum(dy_v * xhat, axis=0, keepdims=True)
        db = jnp.sum(dy_v, axis=0, keepdims=True)

        @pl.when(i == 0)
        def _():
            dg_ref[...] = dg
            db_ref[...] = db

        @pl.when(i > 0)
        def _():
            dg_ref[...] += dg
            db_ref[...] += db

    row = pl.BlockSpec((TM, d), lambda i: (i, 0))
    vec = pl.BlockSpec((1, d), lambda i: (0, 0))
    return _call(body, name, (n // TM,), [row, row, row, vec], [row, row, vec, vec],
                 [_sds((n, d), F32), _sds((n, d), BF16), _sds((1, d), F32), _sds((1, d), F32)])(xin, f, dy, gain)


def _loss_head(y, target):
    n, d = y.shape

    def body(y_ref, t_ref, l_ref, dy_ref):
        i = pl.program_id(0)
        e = y_ref[...] - t_ref[...]
        dy_ref[...] = e * (1.0 / d)
        part = jnp.sum(jnp.sum(e * e, axis=1, keepdims=True), axis=0, keepdims=True) * (0.5 / d)
        part = jnp.broadcast_to(part, (8, 128))

        @pl.when(i == 0)
        def _():
            l_ref[...] = part

        @pl.when(i > 0)
        def _():
            l_ref[...] += part

    row = pl.BlockSpec((TM, d), lambda i: (i, 0))
    return _call(body, "loss_head", (n // TM,), [row, row],
                 [pl.BlockSpec((8, 128), lambda i: (0, 0)), row],
                 [_sds((8, 128), F32), _sds((n, d), F32)])(y, target)


def _group_expand_matrix():
    e = np.zeros((D_MODEL // SSM_GROUP, N_STATE), np.float32)
    for g in range(D_MODEL // SSM_GROUP):
        e[g, g * SSM_STATE:(g + 1) * SSM_STATE] = 1.0
    return jnp.asarray(e)


def _discretise(lr, li, ldt, br, bi, expand):
    dt = jnp.dot(jnp.exp(ldt), expand, precision=HIGHEST, preferred_element_type=F32)
    mag = jnp.exp(lr * dt)
    th = li * dt
    ab_r = mag * jnp.cos(th)
    ab_i = mag * jnp.sin(th)
    den = lr * lr + li * li
    nr = ab_r - 1.0
    co_r = (nr * lr + ab_i * li) / den
    co_i = (ab_i * lr - nr * li) / den
    bb_r = co_r * br - co_i * bi
    bb_i = co_r * bi + co_i * br
    return ab_r, ab_i, bb_r, bb_i


def _cmul(ar, ai, br, bi):
    return ar * br - ai * bi, ar * bi + ai * br


def _s5_prep(lr, li, ldt, br, bi):
    s = N_STATE
    expand = _group_expand_matrix()

    def body(lr_ref, li_ref, ldt_ref, br_ref, bi_ref, e_ref, abr_ref, abi_ref, bbr_ref, bbi_ref, tab_ref):
        ab_r, ab_i, bb_r, bb_i = _discretise(lr_ref[...], li_ref[...], ldt_ref[...], br_ref[...],
                                             bi_ref[...], e_ref[...])
        abr_ref[...] = ab_r
        abi_ref[...] = ab_i
        bbr_ref[...] = bb_r
        bbi_ref[...] = bb_i
        row = lax.broadcasted_iota(jnp.int32, (8, s), 0)
        for base, sign in ((0, 1.0), (8, -1.0)):
            p = [None] * 9
            p[1] = (ab_r, sign * ab_i)
            p[2] = _cmul(*p[1], *p[1])
            p[3] = _cmul(*p[2], *p[1])
            p[4] = _cmul(*p[2], *p[2])
            p[5] = _cmul(*p[4], *p[1])
            p[6] = _cmul(*p[4], *p[2])
            p[7] = _cmul(*p[4], *p[3])
            p[8] = _cmul(*p[4], *p[4])
            for j, k in enumerate((1, 2, 4)):
                keep = (row >= k) if base == 0 else (row < 8 - k)
                tab_ref[base + 2 * j] = jnp.where(keep, jnp.broadcast_to(p[k][0], (8, s)), 0.0)
                tab_ref[base + 2 * j + 1] = jnp.where(keep, jnp.broadcast_to(p[k][1], (8, s)), 0.0)
            pw_r = jnp.zeros((8, s), F32)
            pw_i = jnp.zeros((8, s), F32)
            for i in range(8):
                e = i + 1 if base == 0 else 8 - i
                pw_r = jnp.where(row == i, jnp.broadcast_to(p[e][0], (8, s)), pw_r)
                pw_i = jnp.where(row == i, jnp.broadcast_to(p[e][1], (8, s)), pw_i)
            tab_ref[base + 6] = pw_r
            tab_ref[base + 7] = pw_i

    def full(shape):
        return pl.BlockSpec(shape, lambda i: (0,) * len(shape))

    g = D_MODEL // SSM_GROUP
    return _call(body, "s5_prep", (1,),
                 [full((1, s)), full((1, s)), full((1, g)), full((16, s)), full((16, s)), full((g, s))],
                 [full((1, s)), full((1, s)), full((16, s)), full((16, s)), full((16, 8, s))],
                 [_sds((1, s), F32), _sds((1, s), F32), _sds((16, s), F32), _sds((16, s), F32),
                  _sds((16, 8, s), F32)])(lr, li, ldt, br, bi, expand)


def _s5_prep_bwd(lr, li, ldt, br, bi, d_abr, d_abi, d_bbr, d_bbi):
    s = N_STATE
    g = D_MODEL // SSM_GROUP
    expand = _group_expand_matrix()

    def body(lr_ref, li_ref, ldt_ref, br_ref, bi_ref, e_ref, c1, c2, c3, c4, o1, o2, o3, o4, o5):
        e_val = e_ref[...]
        _, vjp = jax.vjp(lambda a, b, c, d, e: _discretise(a, b, c, d, e, e_val),
                         lr_ref[...], li_ref[...], ldt_ref[...], br_ref[...], bi_ref[...])
        grads = vjp((c1[...], c2[...], c3[...], c4[...]))
        for o, gr in zip((o1, o2, o3, o4, o5), grads):
            o[...] = gr

    def full(shape):
        return pl.BlockSpec(shape, lambda i: (0,) * len(shape))

    return _call(body, "s5_prep_bwd", (1,),
                 [full((1, s)), full((1, s)), full((1, g)), full((16, s)), full((16, s)), full((g, s)),
                  full((1, s)), full((1, s)), full((16, s)), full((16, s))],
                 [full((1, s)), full((1, s)), full((1, g)), full((16, s)), full((16, s))],
                 [_sds((1, s), F32), _sds((1, s), F32), _sds((1, g), F32), _sds((16, s), F32),
                  _sds((16, s), F32)])(lr, li, ldt, br, bi, expand, d_abr, d_abi, d_bbr, d_bbi)


def _scan8(vr, vi, tab_ref, base, reverse):
    for j, k in enumerate((1, 2, 4)):
        mr = tab_ref[base + 2 * j]
        mi = tab_ref[base + 2 * j + 1]
        shift = 8 - k if reverse else k
        sr = pltpu.roll(vr, shift, 0)
        si = pltpu.roll(vi, shift, 0)
        vr, vi = vr + mr * sr - mi * si, vi + mr * si + mi * sr
    return vr, vi


def _s5_scan_fwd(x, bbr, bbi, cr, ci, dvec, tab, seq, comm=None):
    n, d = x.shape
    nt = seq // T_SCAN
    nblk = T_SCAN // 8
    st = CHUNK_ST

    def body(x_ref, bbr_ref, bbi_ref, cr_ref, ci_ref, d_ref, tab_ref, y_ref, hr_ref, hi_ref, car_r, car_i):
        t = pl.program_id(2)

        @pl.when(t == 0)
        def _():
            car_r[...] = jnp.zeros_like(car_r)
            car_i[...] = jnp.zeros_like(car_i)

        u = x_ref[...]
        ub = u.astype(BF16)
        hr_ref[...] = jnp.dot(ub, bbr_ref[...], preferred_element_type=F32)
        hi_ref[...] = jnp.dot(ub, bbi_ref[...], preferred_element_type=F32)

        def step(i, carry):
            sl = pl.ds(pl.multiple_of(i * 8, 8), 8)
            vr, vi = _scan8(hr_ref[sl, :], hi_ref[sl, :], tab_ref, 0, False)
            c_r = jnp.broadcast_to(car_r[...], (8, st))
            c_i = jnp.broadcast_to(car_i[...], (8, st))
            pr = tab_ref[6]
            pi = tab_ref[7]
            vr, vi = vr + pr * c_r - pi * c_i, vi + pr * c_i + pi * c_r
            hr_ref[sl, :] = vr
            hi_ref[sl, :] = vi
            car_r[...] = vr[7:8, :]
            car_i[...] = vi[7:8, :]
            return carry

        lax.fori_loop(0, nblk, step, 0, unroll=4)
        y = lax.dot_general(hr_ref[...].astype(BF16), cr_ref[...], NT, preferred_element_type=F32)
        y = y - lax.dot_general(hi_ref[...].astype(BF16), ci_ref[...], NT, preferred_element_type=F32)
        y_ref[...] = y + d_ref[...] * u

    row = lambda c, b, t: (b * nt + t, c)
    mat = pl.BlockSpec((None, CHUNK_CH, st), lambda c, b, t: (c, 0, 0))
    return _call(
        body, "s5_scan_fwd", (SSM_CHUNKS, n // seq, nt),
        [pl.BlockSpec((T_SCAN, CHUNK_CH), row), mat, mat, mat, mat,
         pl.BlockSpec((1, CHUNK_CH), lambda c, b, t: (0, c)),
         pl.BlockSpec((16, 8, st), lambda c, b, t: (0, 0, c))],
        [pl.BlockSpec((T_SCAN, CHUNK_CH), row), pl.BlockSpec((T_SCAN, st), row), pl.BlockSpec((T_SCAN, st), row)],
        [_sds((n, d), F32), _sds((n, N_STATE), F32), _sds((n, N_STATE), F32)],
        [pltpu.VMEM((1, st), F32), pltpu.VMEM((1, st), F32)], comm=comm,
    )(x, bbr, bbi, cr, ci, dvec, tab)


def _s5_scan_bwd(x, dy, dres, hr, hi, bbr, bbi, cr, ci, dvec, tab, seq, comm=None):
    n, d = x.shape
    nt = seq // T_SCAN
    nblk = T_SCAN // 8
    st = CHUNK_ST

    def body(x_ref, dy_ref, dres_ref, hr_ref, hi_ref, hpr_ref, hpi_ref, bbr_ref, bbi_ref, cr_ref, ci_ref,
             d_ref, tab_ref, dx_ref, dcr_ref, dci_ref, dbr_ref, dbi_ref, dar_ref, dai_ref, dd_ref,
             g_r, g_i, car_r, car_i):
        b = pl.program_id(1)
        tg = pl.program_id(2)
        first = jnp.logical_and(b == 0, tg == 0)

        @pl.when(tg == 0)
        def _():
            car_r[...] = jnp.zeros_like(car_r)
            car_i[...] = jnp.zeros_like(car_i)

        u = x_ref[...]
        dyv = dy_ref[...]
        ub = u.astype(BF16)
        dyb = dyv.astype(BF16)
        g_r[...] = jnp.dot(dyb, cr_ref[...], preferred_element_type=F32)
        g_i[...] = -jnp.dot(dyb, ci_ref[...], preferred_element_type=F32)

        def step(ii, carry):
            i = nblk - 1 - ii
            sl = pl.ds(pl.multiple_of(i * 8, 8), 8)
            vr, vi = _scan8(g_r[sl, :], g_i[sl, :], tab_ref, 8, True)
            c_r = jnp.broadcast_to(car_r[...], (8, st))
            c_i = jnp.broadcast_to(car_i[...], (8, st))
            pr = tab_ref[14]
            pi = tab_ref[15]
            vr, vi = vr + pr * c_r - pi * c_i, vi + pr * c_i + pi * c_r
            g_r[sl, :] = vr
            g_i[sl, :] = vi
            car_r[...] = vr[0:1, :]
            car_i[...] = vi[0:1, :]
            return carry

        lax.fori_loop(0, nblk, step, 0, unroll=4)
        gr = g_r[...]
        gi = g_i[...]
        grb = gr.astype(BF16)
        gib = gi.astype(BF16)
        du = lax.dot_general(grb, bbr_ref[...], NT, preferred_element_type=F32)
        du = du + lax.dot_general(gib, bbi_ref[...], NT, preferred_element_type=F32)
        dx_ref[...] = DN_ALPHA * dres_ref[...] + du + d_ref[...] * dyv

        h_r = hr_ref[...]
        h_i = hi_ref[...]
        t_orig = nt - 1 - tg
        row0 = lax.broadcasted_iota(jnp.int32, (8, st), 0) == 0
        halo_ok = t_orig > 0

        def previous(h, halo_ref):
            top = jnp.broadcast_to(jnp.where(halo_ok, halo_ref[7:8, :], 0.0), (8, st))
            rolled = pltpu.roll(h, 1, 0)
            return jnp.concatenate([jnp.where(row0, top, rolled[:8, :]), rolled[8:, :]], axis=0)

        hp_r = previous(h_r, hpr_ref)
        hp_i = previous(h_i, hpi_ref)
        dar = jnp.sum(gr * hp_r + gi * hp_i, axis=0, keepdims=True)
        dai = jnp.sum(gi * hp_r - gr * hp_i, axis=0, keepdims=True)
        dcr = lax.dot_general(dyb, h_r.astype(BF16), TN, preferred_element_type=F32)
        dci = lax.dot_general(dyb, h_i.astype(BF16), TN, preferred_element_type=F32)
        dbr = lax.dot_general(ub, grb, TN, preferred_element_type=F32)
        dbi = lax.dot_general(ub, gib, TN, preferred_element_type=F32)
        dd = jnp.sum(dyv * u, axis=0, keepdims=True)

        @pl.when(first)
        def _():
            dcr_ref[...] = dcr
            dci_ref[...] = dci
            dbr_ref[...] = dbr
            dbi_ref[...] = dbi
            dar_ref[...] = dar
            dai_ref[...] = dai
            dd_ref[...] = dd

        @pl.when(jnp.logical_not(first))
        def _():
            dcr_ref[...] += dcr
            dci_ref[...] += dci
            dbr_ref[...] += dbr
            dbi_ref[...] += dbi
            dar_ref[...] += dar
            dai_ref[...] += dai
            dd_ref[...] += dd

    row = lambda c, b, t: (b * nt + (nt - 1 - t), c)
    halo = lambda c, b, t: (jnp.maximum((b * nt + (nt - 1 - t)) * (T_SCAN // 8) - 1, 0), c)
    mat = pl.BlockSpec((None, CHUNK_CH, st), lambda c, b, t: (c, 0, 0))
    chan = pl.BlockSpec((T_SCAN, CHUNK_CH), row)
    state = pl.BlockSpec((T_SCAN, st), row)
    return _call(
        body, "s5_scan_bwd", (SSM_CHUNKS, n // seq, nt),
        [chan, chan, chan, state, state, pl.BlockSpec((8, st), halo), pl.BlockSpec((8, st), halo),
         mat, mat, mat, mat, pl.BlockSpec((1, CHUNK_CH), lambda c, b, t: (0, c)),
         pl.BlockSpec((16, 8, st), lambda c, b, t: (0, 0, c))],
        [chan, mat, mat, mat, mat, pl.BlockSpec((1, st), lambda c, b, t: (0, c)),
         pl.BlockSpec((1, st), lambda c, b, t: (0, c)), pl.BlockSpec((1, CHUNK_CH), lambda c, b, t: (0, c))],
        [_sds((n, d), F32)] + [_sds((SSM_CHUNKS, CHUNK_CH, st), F32)] * 4
        + [_sds((1, N_STATE), F32), _sds((1, N_STATE), F32), _sds((1, d), F32)],
        [pltpu.VMEM((T_SCAN, st), F32), pltpu.VMEM((T_SCAN, st), F32), pltpu.VMEM((1, st), F32),
         pltpu.VMEM((1, st), F32)], comm=comm,
    )(x, dy, dres, hr, hi, hr, hi, bbr, bbi, cr, ci, dvec, tab)


_GELU_C = math.sqrt(2.0 / math.pi)


def _gelu_parts(y):
    t = jnp.tanh(_GELU_C * (y + 0.044715 * (y * y * y)))
    return 0.5 * (1.0 + t), t


def _glu_fwd(y, w_glu, b_glu):
    n, d = y.shape

    def body(y_ref, w_ref, b_ref, y2_ref, z_ref, gg_ref):
        yv = y_ref[...]
        cdf, _ = _gelu_parts(yv)
        y2 = yv * cdf
        z = jnp.dot(y2.astype(BF16), w_ref[...], preferred_element_type=F32) + b_ref[...]
        y2_ref[...] = y2.astype(BF16)
        z_ref[...] = z
        gg_ref[...] = (y2 * jax.nn.sigmoid(z)).astype(BF16)

    row = pl.BlockSpec((TM, d), lambda i: (i, 0))
    return _call(body, "glu_fwd", (n // TM,),
                 [row, pl.BlockSpec((d, d), lambda i: (0, 0)), pl.BlockSpec((1, d), lambda i: (0, 0))],
                 [row, row, row], [_sds((n, d), BF16), _sds((n, d), F32), _sds((n, d), BF16)])(y, w_glu, b_glu)


def _glu_bwd_gate(y, z, dgg):
    n, d = y.shape

    def body(y_ref, z_ref, dgg_ref, dz_ref, db_ref, t_ref):
        i = pl.program_id(0)
        yv = y_ref[...]
        cdf, _ = _gelu_parts(yv)
        y2 = yv * cdf
        s = jax.nn.sigmoid(z_ref[...])
        dg = dgg_ref[...]
        dz = dg * y2 * s * (1.0 - s)
        dz_ref[...] = dz.astype(BF16)
        t_ref[...] = dg * s
        db = jnp.sum(dz, axis=0, keepdims=True)

        @pl.when(i == 0)
        def _():
            db_ref[...] = db

        @pl.when(i > 0)
        def _():
            db_ref[...] += db

    row = pl.BlockSpec((TM, d), lambda i: (i, 0))
    vec = pl.BlockSpec((1, d), lambda i: (0, 0))
    return _call(body, "glu_bwd_gate", (n // TM,), [row, row, row], [row, vec, row],
                 [_sds((n, d), BF16), _sds((1, d), F32), _sds((n, d), F32)])(y, z, dgg)


def _gelu_bwd(y, dy2):
    n, d = y.shape

    def body(y_ref, g_ref, o_ref):
        yv = y_ref[...]
        cdf, t = _gelu_parts(yv)
        dinner = _GELU_C * (1.0 + 3.0 * 0.044715 * yv * yv)
        o_ref[...] = g_ref[...] * (cdf + 0.5 * yv * (1.0 - t * t) * dinner)

    row = pl.BlockSpec((TM, d), lambda i: (i, 0))
    return _call(body, "gelu_bwd", (n // TM,), [row, row], row, _sds((n, d), F32))(y, dy2)


HALO = 16


def _shift_down(x, halo, k):
    out = pltpu.roll(x, k, 0)
    row = lax.broadcasted_iota(jnp.int32, (8, x.shape[1]), 0)
    top = jnp.where(row < k, pltpu.roll(halo[HALO - 8:HALO, :], k, 0), out[:8, :])
    return jnp.concatenate([top, out[8:, :]], axis=0)


def _conv3(x, halo, w, b):
    x1 = _shift_down(x, halo, 1)
    x2 = _shift_down(x, halo, 2)
    return b + w[0:1, :] * x + w[1:2, :] * x1 + w[2:3, :] * x2, x1, x2


def _conv_gate_fwd(hc, conv_w, conv_b, seq):
    _, nb, n, c = hc.shape
    tiles_per_seq = seq // TM

    def body(x_ref, halo_ref, w_ref, b_ref, o_ref):
        i = pl.program_id(1)
        start = (i % tiles_per_seq) == 0
        cv = []
        for h in range(2):
            halo = jnp.where(start, 0.0, halo_ref[h].astype(F32))
            cv.append(_conv3(x_ref[h].astype(F32), halo, w_ref[h], b_ref[h])[0])
        o_ref[...] = (cv[1] * jax.nn.sigmoid(cv[1]) * cv[0]).astype(BF16)

    return _call(
        body, "conv_gate_fwd", (nb, n // TM),
        [pl.BlockSpec((2, None, TM, c), lambda d, i: (0, d, i, 0)),
         pl.BlockSpec((2, None, HALO, c), lambda d, i: (0, d, jnp.maximum(i * (TM // HALO) - 1, 0), 0)),
         pl.BlockSpec((2, None, 3, c), lambda d, i: (0, d, 0, 0)),
         pl.BlockSpec((2, None, 1, c), lambda d, i: (0, d, 0, 0))],
        pl.BlockSpec((None, TM, c), lambda d, i: (d, i, 0)),
        _sds((nb, n, c), BF16))(hc, hc, conv_w, conv_b)


def _conv_gate_bwd(hc, dact, conv_w, conv_b, seq, comm=None):
    _, nb, n, c = hc.shape
    tiles_per_seq = seq // TM
    last_halo = n // HALO - 1
    ext = TM + HALO

    def body(x_ref, prev_ref, next_ref, da_ref, dan_ref, w_ref, b_ref, dx_ref, dw_ref):
        i = pl.program_id(1)
        start = (i % tiles_per_seq) == 0
        end = (i % tiles_per_seq) == tiles_per_seq - 1
        row = lax.broadcasted_iota(jnp.int32, (ext, c), 0)
        xe, cv = [], []
        for h in range(2):
            halo = jnp.where(start, 0.0, prev_ref[h].astype(F32))
            x_ext = jnp.concatenate([x_ref[h], next_ref[h]], axis=0).astype(F32)
            conv, x1, x2 = _conv3(x_ext, halo, w_ref[h], b_ref[h])
            xe.append((x_ext, x1, x2))
            cv.append(conv)
        da = jnp.concatenate([da_ref[...], dan_ref[...]], axis=0).astype(F32)
        da = jnp.where(jnp.logical_and(end, row >= TM), 0.0, da)
        sg = jax.nn.sigmoid(cv[1])
        silu = cv[1] * sg
        dcv = [da * silu, da * cv[0] * (sg + silu * (1.0 - sg))]
        for h in range(2):
            w = w_ref[h]
            g = dcv[h]
            dx = w[0:1, :] * g + w[1:2, :] * pltpu.roll(g, ext - 1, 0) + w[2:3, :] * pltpu.roll(g, ext - 2, 0)
            dx_ref[h] = dx[:TM, :].astype(BF16)
            x_ext, x1, x2 = xe[h]
            gt = g[:TM, :]
            parts = [jnp.sum(gt * x_ext[:TM, :], axis=0, keepdims=True),
                     jnp.sum(gt * x1[:TM, :], axis=0, keepdims=True),
                     jnp.sum(gt * x2[:TM, :], axis=0, keepdims=True),
                     jnp.sum(gt, axis=0, keepdims=True)]
            upd = jnp.concatenate(parts + [jnp.zeros((4, c), F32)], axis=0)

            @pl.when(i == 0)
            def _():
                dw_ref[h] = upd

            @pl.when(i > 0)
            def _():
                dw_ref[h] += upd

    nxt = lambda i: jnp.minimum((i + 1) * (TM // HALO), last_halo)
    return _call(
        body, "conv_gate_bwd", (nb, n // TM),
        [pl.BlockSpec((2, None, TM, c), lambda d, i: (0, d, i, 0)),
         pl.BlockSpec((2, None, HALO, c), lambda d, i: (0, d, jnp.maximum(i * (TM // HALO) - 1, 0), 0)),
         pl.BlockSpec((2, None, HALO, c), lambda d, i: (0, d, nxt(i), 0)),
         pl.BlockSpec((None, TM, c), lambda d, i: (d, i, 0)),
         pl.BlockSpec((None, HALO, c), lambda d, i: (d, nxt(i), 0)),
         pl.BlockSpec((2, None, 3, c), lambda d, i: (0, d, 0, 0)),
         pl.BlockSpec((2, None, 1, c), lambda d, i: (0, d, 0, 0))],
        [pl.BlockSpec((2, None, TM, c), lambda d, i: (0, d, i, 0)),
         pl.BlockSpec((2, None, 8, c), lambda d, i: (0, d, 0, 0))],
        [_sds((2, nb, n, c), BF16), _sds((2, nb, 8, c), F32)], comm=comm)(hc, hc, hc, dact, dact, conv_w, conv_b)


def _t5_bucket_np(dist):
    exact = REL_BUCKETS // 2
    d = np.maximum(dist, 1).astype(np.float32)
    large = exact + (np.log(d / exact) / math.log(REL_MAX_DIST / exact) * (REL_BUCKETS - exact)).astype(np.int64)
    large = np.minimum(large, REL_BUCKETS - 1)
    return np.where(dist < exact, dist, large).astype(np.int32)


def _bucket_table(dil):
    steps = np.arange(BAND)[:, None] + BAND - np.arange(2 * BAND)[None, :]
    return _t5_bucket_np(np.maximum(steps, 0) * dil).reshape(1, BAND * 2 * BAND).astype(np.float32)


def _bias_expand(rel_t, dil):
    nk = BAND * 2 * BAND
    bucket = jnp.asarray(_bucket_table(dil))

    def body(r_ref, bk_ref, o_ref):
        ids = lax.broadcasted_iota(jnp.int32, (REL_BUCKETS, nk), 0).astype(F32)
        onehot = (ids == bk_ref[...]).astype(F32)
        o_ref[...] = jnp.dot(r_ref[...], onehot, precision=HIGHEST, preferred_element_type=F32)

    return _call(body, "bias_expand", (1,),
                 [pl.BlockSpec((N_HEADS, REL_BUCKETS), lambda i: (0, 0)), pl.BlockSpec((1, nk), lambda i: (0, 0))],
                 pl.BlockSpec((N_HEADS, nk), lambda i: (0, 0)), _sds((N_HEADS, nk), F32))(rel_t, bucket)


def _bias_reduce(dbias, dil):
    nk = BAND * 2 * BAND
    bucket = jnp.asarray(_bucket_table(dil))

    def body(g_ref, bk_ref, o_ref):
        ids = lax.broadcasted_iota(jnp.int32, (REL_BUCKETS, nk), 0).astype(F32)
        onehot = (ids == bk_ref[...]).astype(F32)
        o_ref[...] = lax.dot_general(g_ref[...], onehot, NT, precision=HIGHEST, preferred_element_type=F32)

    return _call(body, "bias_reduce", (1,),
                 [pl.BlockSpec((N_HEADS, nk), lambda i: (0, 0)), pl.BlockSpec((1, nk), lambda i: (0, 0))],
                 pl.BlockSpec((N_HEADS, REL_BUCKETS), lambda i: (0, 0)),
                 _sds((N_HEADS, REL_BUCKETS), F32))(dbias, bucket)


def _valid_mask(n):
    qi = lax.broadcasted_iota(jnp.int32, (BAND, 2 * BAND), 0)
    kj = lax.broadcasted_iota(jnp.int32, (BAND, 2 * BAND), 1)
    steps = qi + BAND - kj
    in_band = jnp.logical_and(steps >= 0, steps <= BAND)
    return jnp.logical_and(in_band, jnp.logical_or(n > 0, kj >= BAND))


ATTN_COLS = {1: 1024, 4: 512, 16: 128}


def _residue(dil, r):
    return slice(None) if dil == 1 else pl.ds(r, BAND, stride=dil)


def _stack_heads(x, first):
    return jnp.concatenate([jnp.where(first, x, 0.0), jnp.where(first, 0.0, x)], axis=0)


def _attn_fwd(q, k, v, bias, g, dil, bsz, seq, comm=None):
    n = q.shape[0]
    rows = BAND * dil
    nch = seq // rows
    w = N_HEADS * HEAD_DIM
    wc = ATTN_COLS[dil]
    ncol = w // wc
    hpc = wc // HEAD_DIM
    nj = wc // 128
    slab = lambda k: pltpu.VMEM((k * nj, rows, 128), F32)

    def body(q_ref, kc_ref, kp_ref, vc_ref, vp_ref, b_ref, o_ref, l_ref, qf, kf, vf, of, lf):
        ch = pl.program_id(2)
        valid = _valid_mask(ch)
        valid = jnp.concatenate([valid, valid], axis=0)
        lane = lax.broadcasted_iota(jnp.int32, (BAND, 128), 1)
        first = lane < HEAD_DIM
        for j in range(nj):
            sl = slice(128 * j, 128 * (j + 1))
            qf[j] = q_ref[:, sl].astype(F32)
            kf[j] = kp_ref[:, sl].astype(F32)
            kf[nj + j] = kc_ref[:, sl].astype(F32)
            vf[j] = vp_ref[:, sl].astype(F32)
            vf[nj + j] = vc_ref[:, sl].astype(F32)
        for r in range(dil):
            rr = _residue(dil, r)
            for j in range(nj):
                q2 = qf.at[j][rr, :]
                k2 = jnp.concatenate([kf.at[j][rr, :], kf.at[nj + j][rr, :]], axis=0).astype(BF16)
                v2 = jnp.concatenate([vf.at[j][rr, :], vf.at[nj + j][rr, :]], axis=0).astype(BF16)
                qs = _stack_heads(q2, first).astype(BF16)
                s = lax.dot_general(qs, k2, NT, preferred_element_type=F32) * (HEAD_DIM ** -0.5)
                s = jnp.where(valid, s + b_ref[2 * j:2 * j + 2].reshape(2 * BAND, 2 * BAND), NEG_BIG)
                m = jnp.max(s, axis=1, keepdims=True)
                p = jnp.exp(s - m)
                l = jnp.sum(p, axis=1, keepdims=True)
                pv = jnp.dot(p.astype(BF16), v2, preferred_element_type=F32) / l
                lse = jnp.broadcast_to(m + jnp.log(l), (2 * BAND, 128))
                of.at[j][rr, :] = jnp.where(first, pv[:BAND], pv[BAND:])
                lf.at[j][rr, :] = jnp.where(first, lse[:BAND], lse[BAND:])
        for j in range(nj):
            sl = slice(128 * j, 128 * (j + 1))
            o_ref[:, sl] = of[j]
            l_ref[:, sl] = lf[j]

    base = g * ncol
    cur = lambda c, b, ch: (b * nch + ch, base + c)
    prev = lambda c, b, ch: (b * nch + jnp.maximum(ch - 1, 0), base + c)
    blk = lambda im: pl.BlockSpec((rows, wc), im)
    out_blk = pl.BlockSpec((rows, wc), lambda c, b, ch: (b * nch + ch, c))
    return _call(
        body, "attn_fwd_d%d" % dil, (ncol, bsz, nch),
        [blk(cur), blk(cur), blk(prev), blk(cur), blk(prev),
         pl.BlockSpec((hpc, BAND, 2 * BAND), lambda c, b, ch: (c, 0, 0))],
        [out_blk, out_blk],
        [_sds((n, w), F32)] * 2,
        [slab(1), slab(2), slab(2), slab(1), slab(1)], comm=comm,
    )(q, k, k, v, v, bias)


def _attn_combine(outs, lses):
    n, w = outs[0].shape

    def body(o0, o1, o2, l0, l1, l2, o_ref, l_ref):
        la, lb, lc = l0[...], l1[...], l2[...]
        m = jnp.maximum(jnp.maximum(la, lb), lc)
        wa, wb, wc = jnp.exp(la - m), jnp.exp(lb - m), jnp.exp(lc - m)
        tot = wa + wb + wc
        o_ref[...] = (wa * o0[...] + wb * o1[...] + wc * o2[...]) / tot
        l_ref[...] = m + jnp.log(tot)

    row = pl.BlockSpec((TM, w), lambda i: (i, 0))
    return _call(body, "attn_combine", (n // TM,), [row] * 6, [row, row],
                 [_sds((n, w), F32)] * 2)(*outs, *lses)


def _attn_bwd(q, k, v, do, o, lse, bias, g, dil, bsz, seq, comm=None, into=()):
    n = q.shape[0]
    rows = BAND * dil
    nch = seq // rows
    w = N_HEADS * HEAD_DIM
    wc = ATTN_COLS[dil]
    ncol = w // wc
    hpc = wc // HEAD_DIM
    nj = wc // 128
    slab = lambda k: pltpu.VMEM((k * nj, rows, 128), F32)
    scale = HEAD_DIM ** -0.5

    def body(q_ref, kc_ref, kp_ref, vc_ref, vp_ref, do_ref, o_ref, l_ref, b_ref,
             dq_ref, dk_ref, dv_ref, db_ref, qf, kf, vf, dof, of, lf, dqf, dkf, dvf):
        b = pl.program_id(1)
        ch = pl.program_id(2)
        cur = ch % 2
        prv = 1 - cur

        @pl.when(jnp.logical_and(b == 0, ch == 0))
        def _():
            db_ref[...] = jnp.zeros_like(db_ref)

        @pl.when(ch == 0)
        def _():
            for j in range(nj):
                dkf[nj + j] = jnp.zeros((rows, 128), F32)
                dvf[nj + j] = jnp.zeros((rows, 128), F32)

        @pl.when(ch < nch)
        def _():
            valid = _valid_mask(ch)
            valid = jnp.concatenate([valid, valid], axis=0)
            first = lax.broadcasted_iota(jnp.int32, (BAND, 128), 1) < HEAD_DIM
            for j in range(nj):
                sl = slice(128 * j, 128 * (j + 1))
                qf[j] = q_ref[:, sl].astype(F32)
                kf[j] = kp_ref[:, sl].astype(F32)
                kf[nj + j] = kc_ref[:, sl].astype(F32)
                vf[j] = vp_ref[:, sl].astype(F32)
                vf[nj + j] = vc_ref[:, sl].astype(F32)
                dof[j] = do_ref[:, sl]
                of[j] = o_ref[:, sl]
                lf[j] = l_ref[:, sl]
            for r in range(dil):
                rr = _residue(dil, r)
                for j in range(nj):
                    k2 = jnp.concatenate([kf.at[j][rr, :], kf.at[nj + j][rr, :]], axis=0).astype(BF16)
                    v2 = jnp.concatenate([vf.at[j][rr, :], vf.at[nj + j][rr, :]], axis=0).astype(BF16)
                    do2 = dof.at[j][rr, :]
                    lse2 = lf.at[j][rr, :]
                    qs = _stack_heads(qf.at[j][rr, :], first).astype(BF16)
                    dos = _stack_heads(do2, first)
                    delta = jnp.sum(dos * jnp.concatenate([of.at[j][rr, :]] * 2, axis=0), axis=1, keepdims=True)
                    dos = dos.astype(BF16)
                    lse_col = jnp.concatenate([lse2[:, 0:1], lse2[:, HEAD_DIM:HEAD_DIM + 1]], axis=0)
                    s = lax.dot_general(qs, k2, NT, preferred_element_type=F32) * scale
                    s = jnp.where(valid, s + b_ref[2 * j:2 * j + 2].reshape(2 * BAND, 2 * BAND), NEG_BIG)
                    p = jnp.exp(s - lse_col)
                    dp = lax.dot_general(dos, v2, NT, preferred_element_type=F32)
                    ds = p * (dp - delta)
                    db_ref[2 * j:2 * j + 2] += ds.reshape(2, BAND, 2 * BAND)
                    dsb = ds.astype(BF16)
                    dq2 = jnp.dot(dsb, k2, preferred_element_type=F32) * scale
                    dk2 = lax.dot_general(dsb, qs, TN, preferred_element_type=F32) * scale
                    dv2 = lax.dot_general(p.astype(BF16), dos, TN, preferred_element_type=F32)
                    dqf.at[j][rr, :] = jnp.where(first, dq2[:BAND], dq2[BAND:])
                    dkf.at[prv * nj + j][rr, :] += dk2[:BAND, :]
                    dvf.at[prv * nj + j][rr, :] += dv2[:BAND, :]
                    dkf.at[cur * nj + j][rr, :] = dk2[BAND:, :]
                    dvf.at[cur * nj + j][rr, :] = dv2[BAND:, :]
            for j in range(nj):
                dq_ref[:, 128 * j:128 * (j + 1)] = dqf[j].astype(BF16)

        for j in range(nj):
            sl = slice(128 * j, 128 * (j + 1))
            dk_ref[:, sl] = dkf[prv * nj + j].astype(BF16)
            dv_ref[:, sl] = dvf[prv * nj + j].astype(BF16)

    last = nch - 1
    base = g * ncol
    cur3 = lambda c, b, ch: (b * nch + jnp.minimum(ch, last), base + c)
    prev3 = lambda c, b, ch: (b * nch + jnp.maximum(jnp.minimum(ch, last) - 1, 0), base + c)
    cur1 = lambda c, b, ch: (b * nch + jnp.minimum(ch, last), c)
    lag3 = lambda c, b, ch: (b * nch + jnp.maximum(ch - 1, 0), base + c)
    blk = lambda im: pl.BlockSpec((rows, wc), im)
    bias_blk = pl.BlockSpec((hpc, BAND, 2 * BAND), lambda c, b, ch: (c, 0, 0))
    out = _sds(q.shape, BF16)
    return _call(
        body, "attn_bwd_d%d" % dil, (ncol, bsz, nch + 1),
        [blk(cur3), blk(cur3), blk(prev3), blk(cur3), blk(prev3), blk(cur1), blk(cur1), blk(cur1), bias_blk],
        [blk(cur3), blk(lag3), blk(lag3), bias_blk],
        [out, out, out, _sds((N_HEADS, BAND, 2 * BAND), F32)],
        [slab(1), slab(2), slab(2), slab(1), slab(1), slab(1), slab(1), slab(2), slab(2)], comm=comm, fill=into,
    )(q, k, k, v, v, do, o, lse, bias)


def _adamw(name, parts, w, m, v):
    nl, r, c = w.shape
    nparts = parts[0].shape[0]
    tr = r
    for cand in (256, 352, 128):
        if r % cand == 0 and r > cand:
            tr = cand
            break
    c1 = 1.0 - ADAM_B1 ** ADAM_STEP
    c2 = 1.0 - ADAM_B2 ** ADAM_STEP

    def body(*refs):
        p_refs = refs[:nl]
        w_ref, m_ref, v_ref, g_ref, d_ref, nm_ref, nv_ref = refs[nl:]
        layer = pl.program_id(0)
        for l in range(nl):
            @pl.when(layer == l)
            def _():
                g = p_refs[l][0].astype(F32)
                for dev in range(1, nparts):
                    g = g + p_refs[l][dev].astype(F32)
                nm = ADAM_B1 * m_ref[...] + (1.0 - ADAM_B1) * g
                nv = ADAM_B2 * v_ref[...] + (1.0 - ADAM_B2) * (g * g)
                m_hat = nm / c1
                v_hat = nv / c2
                g_ref[...] = g
                d_ref[...] = -ADAM_LR * (m_hat / (jnp.sqrt(v_hat) + ADAM_EPS) + ADAM_WD * w_ref[...])
                nm_ref[...] = nm
                nv_ref[...] = nv

    def part_spec(l):
        return pl.BlockSpec((nparts, tr, c), lambda layer, i: (0, jnp.where(layer == l, i, 0), 0))

    row = pl.BlockSpec((None, tr, c), lambda layer, i: (layer, i, 0))
    return _call(body, name, (nl, r // tr), [part_spec(l) for l in range(nl)] + [row, row, row],
                 [row] * 4, [_sds((nl, r, c), F32)] * 4)(*parts, w, m, v)


def _exchange(name, srcs, out_shapes, items):
    n_in = len(srcs)
    n_out = len(out_shapes)
    n_items = len(items)

    def body(*refs):
        ins = refs[:n_in]
        outs = refs[n_in:n_in + n_out]
        send_sems, recv_sems, local_sems = refs[n_in + n_out:]
        x, y, c = lax.axis_index("x"), lax.axis_index("y"), lax.axis_index("c")
        me = 4 * x + 2 * y + c

        def pick(ref, sel, peer):
            idx = tuple(peer if s == "peer" else me if s == "me" else s for s in sel)
            return ref.at[idx] if idx else ref

        copies = []
        for it, (si, ssel, oi, osel) in enumerate(items):
            local = pltpu.make_async_copy(pick(ins[si], ssel, me), pick(outs[oi], osel, me), local_sems.at[it])
            local.start()
            copies.append(local)
            for j in range(1, N_DEV):
                px = 1 - x if j & 4 else x
                py = 1 - y if j & 2 else y
                pc = 1 - c if j & 1 else c
                peer = 4 * px + 2 * py + pc
                sem = it * (N_DEV - 1) + j - 1
                cp = pltpu.make_async_remote_copy(
                    src_ref=pick(ins[si], ssel, peer), dst_ref=pick(outs[oi], osel, peer),
                    send_sem=send_sems.at[sem], recv_sem=recv_sems.at[sem],
                    device_id=(px, py, pc), device_id_type=pl.DeviceIdType.MESH)
                cp.start()
                copies.append(cp)
        for cp in copies:
            cp.wait()

    any_spec = pl.BlockSpec(memory_space=pl.ANY)
    return pl.pallas_call(
        body, name=name, in_specs=[any_spec] * n_in, out_specs=[any_spec] * n_out, out_shape=list(out_shapes),
        scratch_shapes=[pltpu.SemaphoreType.DMA((n_items * (N_DEV - 1),)),
                        pltpu.SemaphoreType.DMA((n_items * (N_DEV - 1),)),
                        pltpu.SemaphoreType.DMA((n_items,))],
        compiler_params=pltpu.CompilerParams(has_side_effects=True),
    )(*srcs)


def _sel(ref, sel, slot=None):
    idx = tuple(slot if s == "slot" else s for s in sel)
    return ref.at[idx] if idx else ref


def _gather_plan(srcs, out_shapes, items):
    per = N_DEV - 1
    n_items = len(items)

    def phases(ins, outs, send_sems, recv_sems, local_sems, total):
        x, y, c = lax.axis_index("x"), lax.axis_index("y"), lax.axis_index("c")
        sibling = (x, y, 1 - c)
        chips = [(1 - x, y), (x, 1 - y), (1 - x, 1 - y)]
        slot_of = lambda px, py, pc: 4 * px + 2 * py + pc

        def copy(it, k, block, to, src=None):
            _, _, oi, osel = items[it]
            dst = _sel(outs[oi], osel, slot_of(*block))
            return pltpu.make_async_remote_copy(
                src_ref=dst if src is None else src, dst_ref=dst,
                send_sem=send_sems.at[it * per + k], recv_sem=recv_sems.at[it * per + k],
                device_id=to, device_id_type=pl.DeviceIdType.MESH)

        def own_copies(it):
            si, ssel, oi, osel = items[it]
            src = _sel(ins[si], ssel)
            mine = pltpu.make_async_copy(src, _sel(outs[oi], osel, slot_of(x, y, c)), local_sems.at[it])
            first = [copy(it, 0, (x, y, c), sibling, src=src)]
            first += [copy(it, 1 + j, (x, y, c), (*chip, c), src=src) for j, chip in enumerate(chips)]
            return mine, first

        def start():
            for it in range(n_items):
                mine, first = own_copies(it)
                mine.start()
                for cp in first:
                    cp.start()

        def forward(it):
            def go():
                for j, chip in enumerate(chips):
                    copy(it, 1 + j, (*chip, c), (x, y, c)).wait_recv()
                    copy(it, 4 + j, (*chip, c), sibling).start()
            return go

        def finish():
            for it in range(n_items):
                copy(it, 0, sibling, (x, y, c)).wait_recv()
                for j, chip in enumerate(chips):
                    copy(it, 4 + j, (*chip, 1 - c), (x, y, c)).wait_recv()
            for it in range(n_items):
                mine, first = own_copies(it)
                for cp in first:
                    cp.wait_send()
                for j, chip in enumerate(chips):
                    copy(it, 4 + j, (*chip, c), sibling).wait_send()
                mine.wait()

        lo = max(total // 3, 1)
        acts = [(0, start)]
        for it in range(n_items):
            acts.append((min(lo + it * (total - 1 - lo) // n_items, total - 2), forward(it)))
        acts.append((total - 1, finish))
        return acts

    return _Plan(srcs, out_shapes, n_items * per, n_items, phases)


def _scatter_plan(srcs, out_shapes, items):
    per = N_DEV - 1
    n_items = len(items)

    def phases(ins, outs, send_sems, recv_sems, local_sems, total):
        x, y, c = lax.axis_index("x"), lax.axis_index("y"), lax.axis_index("c")
        me = 4 * x + 2 * y + c

        def copies():
            out = []
            for it, (si, oi, osel) in enumerate(items):
                dst = _sel(outs[oi], osel, me)
                out.append(pltpu.make_async_copy(ins[si].at[me], dst, local_sems.at[it]))
                for j in range(1, N_DEV):
                    px = 1 - x if j & 4 else x
                    py = 1 - y if j & 2 else y
                    pc = 1 - c if j & 1 else c
                    out.append(pltpu.make_async_remote_copy(
                        src_ref=ins[si].at[4 * px + 2 * py + pc], dst_ref=dst,
                        send_sem=send_sems.at[it * per + j - 1], recv_sem=recv_sems.at[it * per + j - 1],
                        device_id=(px, py, pc), device_id_type=pl.DeviceIdType.MESH))
            return out

        def start():
            for cp in copies():
                cp.start()

        def finish():
            for cp in copies():
                cp.wait()

        return [(0, start), (total - 1, finish)]

    return _Plan(srcs, out_shapes, n_items * per, n_items, phases)


def _sum_slots(parts):
    _, r, c = parts.shape

    def body(p_ref, o_ref):
        acc = p_ref[0]
        for dev in range(1, N_DEV):
            acc = acc + p_ref[dev]
        o_ref[...] = acc

    return _call(body, "sum_slots", (1,), [pl.BlockSpec((N_DEV, r, c), lambda i: (0, 0, 0))],
                 pl.BlockSpec((r, c), lambda i: (0, 0)), _sds((r, c), F32))(parts)


def _pack(arrays):
    rows = []
    for a in arrays:
        flat = a.reshape(-1).astype(F32)
        pad = (-flat.shape[0]) % 1024
        if pad:
            flat = jnp.concatenate([flat, jnp.zeros((pad,), F32)])
        rows.append(flat.reshape(-1, 128))
    return jnp.concatenate(rows, axis=0)


def _unpack(buf, shapes):
    out = []
    r0 = 0
    for shp in shapes:
        size = int(np.prod(shp))
        nrows = -(-size // 1024) * 8
        out.append(buf[r0:r0 + nrows].reshape(-1)[:size].reshape(shp))
        r0 += nrows
    return out


def _block_diag_chunks(val):
    eye = jnp.eye(8, dtype=val.dtype)
    return jnp.einsum("cjhp,jk->cjhkp", val, eye).reshape(SSM_CHUNKS, CHUNK_CH, CHUNK_ST)


def _diag_blocks(dense):
    d5 = dense.reshape(SSM_CHUNKS, 8, SSM_GROUP, 8, SSM_STATE)
    return jnp.stack([d5[:, j, :, j, :] for j in range(8)], axis=1)


def _carried(res, comm):
    return res if comm is not None else (res, None)


def _ffn_fwd(tag, hin, hin_b, w_up, conv_w, conv_b, w_down4, gain, bias, seq, comm_up=None, comm_down=None):
    hc, got_up = _carried(_mm_nn("ffn_up" + tag, hin_b, w_up, FFN_HIDDEN, blocked_out=True, comm=comm_up), comm_up)
    nb, n, c = hc.shape
    hc = hc.reshape(2, nb // 2, n, c)
    act = _conv_gate_fwd(hc, conv_w, conv_b, seq)
    ffn, got_down = _carried(_mm_nn_red("ffn_down" + tag, act, w_down4, F32, comm=comm_down), comm_down)
    hout, hout_b = _ln_fwd("ffn_norm" + tag, hin, ffn, gain, bias)
    return hout, hout_b, (hin, hin_b, hc, act, ffn), (got_up, got_down)


def _ffn_bwd(tag, saved, dh, w_up, conv_w, conv_b, w_down4, gain, seq, comm_conv=None, comm_dw=None,
             comm_dx=None):
    hin, hin_b, hc, act, ffn = saved
    dr, dr_b, dgain, dbias = _ln_bwd("ffn_norm_bwd" + tag, hin, ffn, dh, gain)
    d_wdown = _mm_tn("ffn_down_dw" + tag, act, dr_b, act.shape[0], BF16, a_blocked=True, g_mode="shared")
    dact = _mm_nt_out("ffn_down_dx" + tag, dr_b, w_down4, FFN_HIDDEN)
    (dhc, dconv), got_conv = _carried(_conv_gate_bwd(hc, dact, conv_w, conv_b, seq, comm=comm_conv), comm_conv)
    dhc8 = dhc.reshape(2 * dhc.shape[1], dhc.shape[2], dhc.shape[3])
    d_wup, got_dw = _carried(_mm_tn("ffn_up_dw" + tag, hin_b, dhc8, dhc8.shape[0], BF16, g_mode="blocked",
                                    comm=comm_dw), comm_dw)
    dhin, got_dx = _carried(_mm_nt_red("ffn_up_dx" + tag, dhc8, w_up, F32, g_blocked=True, add=dr,
                                       add_scale=DN_ALPHA, comm=comm_dx), comm_dx)
    return dhin, d_wup, d_wdown, dconv, dgain, dbias, (got_conv, got_dw, got_dx)


def _local_step(x, target, p, wfull, bsz, seq, ex=None):
    n = bsz * seq
    xf = x.reshape(n, D_MODEL)
    tf = target.reshape(n, D_MODEL)

    lr = p["s5_lam_re"].reshape(1, N_STATE)
    li = p["s5_lam_im"].reshape(1, N_STATE)
    ldt = p["s5_log_dt"].reshape(1, D_MODEL // SSM_GROUP)
    br_t = p["s5_b_re"].reshape(N_STATE, SSM_GROUP).T
    bi_t = p["s5_b_im"].reshape(N_STATE, SSM_GROUP).T
    ab_r, ab_i, bb_r, bb_i, tab = _s5_prep(lr, li, ldt, br_t, bi_t)

    def bb_chunks(bb):
        return _block_diag_chunks(bb.reshape(SSM_GROUP, SSM_CHUNKS, 8, SSM_STATE).transpose(1, 2, 0, 3))

    def c_chunks(cc):
        return _block_diag_chunks(cc.reshape(SSM_CHUNKS, 8, SSM_GROUP, SSM_STATE))

    bbr_c, bbi_c = bb_chunks(bb_r).astype(BF16), bb_chunks(bb_i).astype(BF16)
    cr_c, ci_c = c_chunks(p["s5_c_re"]).astype(BF16), c_chunks(p["s5_c_im"]).astype(BF16)
    dvec = p["s5_d"].reshape(1, D_MODEL)

    if ex is None:
        y, h_r, h_i = _s5_scan_fwd(xf, bbr_c, bbi_c, cr_c, ci_c, dvec, tab, seq)
    else:
        (y, h_r, h_i), gathered = _s5_scan_fwd(xf, bbr_c, bbi_c, cr_c, ci_c, dvec, tab, seq,
                                               comm=ex.gather_first())
        p, wfull = ex.take_first(gathered, p)
    ln_g = p["ln_gain"].reshape(4, 1, D_MODEL)
    ln_b = p["ln_bias"].reshape(4, 1, D_MODEL)
    y2, z, gg = _glu_fwd(y, wfull["s5_w_glu"], p["s5_b_glu"].reshape(1, D_MODEL))
    mix = _mm_nn("s5_out", gg, wfull["s5_w_out"][None], F32)
    h1, h1b = _ln_fwd("s5_norm", xf, mix, ln_g[0], ln_b[0])
    h2, h2b, ffn0_saved, _ = _ffn_fwd("0", h1, h1b, wfull["ffn_w_up"][0], p["conv_w"][0], p["conv_b"][0],
                                      wfull["ffn_w_down"][0], ln_g[1], ln_b[1], seq)

    kmat = _mm_nn("attn_k", h2b, wfull["attn_w_kv"], BF16, blocks=(0, 4))
    vmat = _mm_nn("attn_v", h2b, wfull["attn_w_kv"], BF16, blocks=(4, 4))
    qmat = _mm_nn("attn_q", h2b, wfull["attn_w_q"], BF16)
    biases, outs, lses = [], [], []
    for g, dil in enumerate(DILATIONS):
        rel_t = p["rel_bias"][:, g * N_HEADS:(g + 1) * N_HEADS].T
        biases.append(_bias_expand(rel_t, dil).reshape(N_HEADS, BAND, 2 * BAND))
        which = {4: "down", 16: "up"}.get(dil) if ex else None
        plan = ex.gather_layer1(which) if which else None
        (o_g, l_g), got = _carried(_attn_fwd(qmat, kmat, vmat, biases[g], g, dil, bsz, seq, comm=plan), plan)
        if plan is not None:
            wfull = ex.take_layer1(which, got, wfull)
        outs.append(o_g)
        lses.append(l_g)
    o, lse = _attn_combine(outs, lses)
    attn_out = _mm_nn("attn_out", o, wfull["attn_w_out"][None], F32)
    h3, h3b = _ln_fwd("attn_norm", h2, attn_out, ln_g[2], ln_b[2])
    h4, _, ffn1_saved, _ = _ffn_fwd("1", h3, h3b, wfull["ffn_w_up"][1], p["conv_w"][1], p["conv_b"][1],
                                    wfull["ffn_w_down"][1], ln_g[3], ln_b[3], seq)
    loss_tile, dh4 = _loss_head(h4, tf)

    grads = {}
    dh3, grads["ffn_w_up1"], grads["ffn_w_down1"], dconv1, dg3, db3, _ = _ffn_bwd(
        "1", ffn1_saved, dh4, wfull["ffn_w_up"][1], p["conv_w"][1], p["conv_b"][1], wfull["ffn_w_down"][1],
        ln_g[3], seq)
    dr3, dr3b, dg2, db2 = _ln_bwd("attn_norm_bwd", h2, attn_out, dh3, ln_g[2])
    grads["attn_w_out"] = _mm_tn("attn_out_dw", o, dr3b, 1, BF16, g_mode="shared")
    do = _mm_nt_red("attn_out_dx", dr3b, wfull["attn_w_out"][None], F32, g_blocked=False)
    dqkv, drel = (), []
    for g, dil in enumerate(DILATIONS):
        plan = ex.scatter(grads, ("ffn_w_up1", "ffn_w_down1")) if ex and dil == DILATIONS[-1] else None
        res, got = _carried(_attn_bwd(qmat, kmat, vmat, do, o, lse, biases[g], g, dil, bsz, seq, comm=plan,
                                      into=dqkv), plan)
        if plan is not None:
            ex.keep(plan, got)
        dqkv, db_g = tuple(res[:3]), res[3]
        drel.append(_bias_reduce(db_g.reshape(N_HEADS, BAND * 2 * BAND), dil).T)
    dq, dk, dv = dqkv
    grads["rel_bias"] = jnp.concatenate(drel, axis=1)
    grads["attn_w_q"] = _mm_tn("attn_q_dw", h2b, dq, N_DEV, BF16)
    grads["attn_w_kv"] = jnp.concatenate([_mm_tn("attn_k_dw", h2b, dk, 4, BF16),
                                          _mm_tn("attn_v_dw", h2b, dv, 4, BF16)], axis=0)
    dh2 = _mm_nt_red("attn_q_dx", dq, wfull["attn_w_q"], F32, g_blocked=False, add=dr3, add_scale=DN_ALPHA)
    dh2 = _mm_nt_red("attn_k_dx", dk, wfull["attn_w_kv"], F32, g_blocked=False, add=dh2, blocks=(0, 4))
    dh2 = _mm_nt_red("attn_v_dx", dv, wfull["attn_w_kv"], F32, g_blocked=False, add=dh2, blocks=(4, 4))

    plans = [ex.scatter(grads, names) if ex else None
             for names in (("attn_w_kv",), ("attn_w_out",), ("attn_w_q",))]
    dh1, grads["ffn_w_up0"], grads["ffn_w_down0"], dconv0, dg1, db1, got = _ffn_bwd(
        "0", ffn0_saved, dh2, wfull["ffn_w_up"][0], p["conv_w"][0], p["conv_b"][0], wfull["ffn_w_down"][0],
        ln_g[1], seq, comm_conv=plans[0], comm_dw=plans[1], comm_dx=plans[2])
    if ex is not None:
        for plan, outs in zip(plans, got):
            ex.keep(plan, outs)
    dr1, dr1b, dg0, db0 = _ln_bwd("s5_norm_bwd", xf, mix, dh1, ln_g[0])
    grads["s5_w_out"] = _mm_tn("s5_out_dw", gg, dr1b, 1, BF16, g_mode="shared")
    dgg = _mm_nt_red("s5_out_dx", dr1b, wfull["s5_w_out"][None], F32, g_blocked=False)
    dz, d_bglu, dy2_direct = _glu_bwd_gate(y, z, dgg)
    grads["s5_w_glu"] = _mm_tn("s5_glu_dw", y2, dz, 1, BF16, g_mode="shared")
    dy2 = _mm_nt_red("s5_glu_dx", dz, wfull["s5_w_glu"][None], F32, g_blocked=False, add=dy2_direct)
    dy = _gelu_bwd(y, dy2)
    plan_last = ex.scatter(grads, ("ffn_w_up0", "ffn_w_down0", "s5_w_out", "s5_w_glu")) if ex else None
    res = _s5_scan_bwd(xf, dy, dr1, h_r, h_i, bbr_c, bbi_c, cr_c, ci_c, dvec, tab, seq, comm=plan_last)
    if ex is not None:
        res, got = res
        ex.keep(plan_last, got)
    dx, dcr, dci, dbr, dbi, dar, dai, dd = res

    def bb_from_chunks(dense):
        return _diag_blocks(dense).transpose(2, 0, 1, 3).reshape(SSM_GROUP, N_STATE)

    d_lr, d_li, d_ldt, d_br, d_bi = _s5_prep_bwd(lr, li, ldt, br_t, bi_t, dar, dai,
                                                 bb_from_chunks(dbr), bb_from_chunks(dbi))
    grads["s5_lam_re"] = d_lr.reshape(p["s5_lam_re"].shape)
    grads["s5_lam_im"] = d_li.reshape(p["s5_lam_im"].shape)
    grads["s5_log_dt"] = d_ldt.reshape(p["s5_log_dt"].shape)
    grads["s5_b_re"] = d_br.T.reshape(p["s5_b_re"].shape)
    grads["s5_b_im"] = d_bi.T.reshape(p["s5_b_im"].shape)
    grads["s5_c_re"] = _diag_blocks(dcr).reshape(p["s5_c_re"].shape)
    grads["s5_c_im"] = -_diag_blocks(dci).reshape(p["s5_c_im"].shape)
    grads["s5_d"] = dd.reshape(p["s5_d"].shape)
    grads["s5_b_glu"] = d_bglu.reshape(1, D_MODEL)
    dconv = jnp.stack([dconv0, dconv1]).reshape(2, N_DEV, 8, -1)
    grads["ffn_conv_w"] = dconv[:, :, 0:3, :].transpose(0, 2, 1, 3)
    grads["ffn_conv_b"] = dconv[:, :, 3, :]
    grads["ln_gain"] = jnp.stack([dg0, dg1, dg2, dg3]).reshape(2, 2, D_MODEL)
    grads["ln_bias"] = jnp.stack([db0, db1, db2, db3]).reshape(2, 2, D_MODEL)
    return loss_tile[0, 0], dx.reshape(x.shape), grads


SMALL_REPLICATED = ("s5_lam_re", "s5_lam_im", "s5_log_dt", "s5_b_re", "s5_b_im", "s5_c_re", "s5_c_im", "s5_d",
                    "rel_bias", "ffn_conv_b")
SMALL_SHARDED = ("s5_b_glu", "ffn_conv_w", "ln_gain", "ln_bias")
BIG = ("s5_w_glu", "s5_w_out", "attn_w_kv", "attn_w_q", "attn_w_out", "ffn_w_up", "ffn_w_down")
WEIGHTS = ("s5_lam_re", "s5_lam_im", "s5_log_dt", "s5_b_re", "s5_b_im", "s5_c_re", "s5_c_im", "s5_d", "s5_w_glu",
           "s5_b_glu", "s5_w_out", "attn_w_kv", "attn_w_q", "attn_w_out", "rel_bias", "ffn_w_up", "ffn_conv_w",
           "ffn_conv_b", "ffn_w_down", "ln_gain", "ln_bias")


class _Exchanges:
    def __init__(self, w):
        self.w = w
        self.parts = {}
        self.up = w["ffn_w_up"].astype(BF16)
        self.down = w["ffn_w_down"].astype(BF16)

    def gather_first(self):
        w = self.w
        srcs = [w["s5_w_glu"][0].astype(BF16), w["s5_w_out"][0].astype(BF16),
                _pack([w[k] for k in SMALL_SHARDED]), self.up, self.down, w["attn_w_kv"].astype(BF16),
                w["attn_w_q"][0].astype(BF16), w["attn_w_out"][0].astype(BF16)]
        outs = [_sds((N_DEV,) + (s.shape[1:] if i in (3, 4) else s.shape), s.dtype) for i, s in enumerate(srcs)]
        items = [(i, (0,) if i in (3, 4) else (), i, ("slot",)) for i in range(len(srcs))]
        return _gather_plan(srcs, outs, items)

    def take_first(self, g, p):
        d = D_MODEL
        w = self.w
        wfull = {
            "s5_w_glu": g[0].reshape(d, d),
            "s5_w_out": g[1].reshape(d, d),
            "ffn_w_up": [g[3], None],
            "ffn_w_down": [g[4].reshape(4, -1, d), None],
            "attn_w_kv": g[5],
            "attn_w_q": g[6],
            "attn_w_out": g[7].reshape(d, d),
        }
        smalls = [_unpack(g[2][dev], [w[k].shape for k in SMALL_SHARDED]) for dev in range(N_DEV)]
        c_ff = w["ffn_conv_w"].shape[2]
        conv_w = jnp.stack([s[1] for s in smalls], axis=2)
        p = dict(p)
        p.update(s5_b_glu=jnp.concatenate([s[0] for s in smalls], axis=1),
                 ln_gain=jnp.concatenate([s[2] for s in smalls], axis=2),
                 ln_bias=jnp.concatenate([s[3] for s in smalls], axis=2),
                 conv_w=conv_w.transpose(0, 2, 1, 3).reshape(2, 2, 4, 3, c_ff),
                 conv_b=w["ffn_conv_b"].reshape(2, 2, 4, 1, c_ff))
        return p, wfull

    def gather_layer1(self, which):
        src = self.up if which == "up" else self.down
        return _gather_plan([src], [_sds((N_DEV,) + src.shape[1:], BF16)], [(0, (1,), 0, ("slot",))])

    def take_layer1(self, which, got, wfull):
        wfull = dict(wfull)
        if which == "up":
            wfull["ffn_w_up"] = [wfull["ffn_w_up"][0], got[0]]
        else:
            wfull["ffn_w_down"] = [wfull["ffn_w_down"][0], got[0].reshape(4, -1, D_MODEL)]
        return wfull

    def scatter(self, grads, names):
        srcs = [grads[k].reshape(N_DEV, -1, grads[k].shape[-1]) for k in names]
        plan = _scatter_plan(srcs, [_sds(s.shape, BF16) for s in srcs],
                             [(i, i, ("slot",)) for i in range(len(names))])
        plan.names = names
        return plan

    def keep(self, plan, outs):
        for k, o in zip(plan.names, outs):
            self.parts[k] = o


def kernel(x, s5_lam_re, s5_lam_im, s5_log_dt, s5_b_re, s5_b_im, s5_c_re, s5_c_im, s5_d, s5_w_glu, s5_b_glu, s5_w_out, attn_w_kv, attn_w_q, attn_w_out, rel_bias, ffn_w_up, ffn_conv_w, ffn_conv_b, ffn_w_down, ln_gain, ln_bias, loss_target, m_s5_lam_re, m_s5_lam_im, m_s5_log_dt, m_s5_b_re, m_s5_b_im, m_s5_c_re, m_s5_c_im, m_s5_d, m_s5_w_glu, m_s5_b_glu, m_s5_w_out, m_attn_w_kv, m_attn_w_q, m_attn_w_out, m_rel_bias, m_ffn_w_up, m_ffn_conv_w, m_ffn_conv_b, m_ffn_w_down, m_ln_gain, m_ln_bias, v_s5_lam_re, v_s5_lam_im, v_s5_log_dt, v_s5_b_re, v_s5_b_im, v_s5_c_re, v_s5_c_im, v_s5_d, v_s5_w_glu, v_s5_b_glu, v_s5_w_out, v_attn_w_kv, v_attn_w_q, v_attn_w_out, v_rel_bias, v_ffn_w_up, v_ffn_conv_w, v_ffn_conv_b, v_ffn_w_down, v_ln_gain, v_ln_bias):
    w = dict(s5_lam_re=s5_lam_re, s5_lam_im=s5_lam_im, s5_log_dt=s5_log_dt, s5_b_re=s5_b_re, s5_b_im=s5_b_im,
             s5_c_re=s5_c_re, s5_c_im=s5_c_im, s5_d=s5_d, s5_w_glu=s5_w_glu, s5_b_glu=s5_b_glu, s5_w_out=s5_w_out,
             attn_w_kv=attn_w_kv, attn_w_q=attn_w_q, attn_w_out=attn_w_out, rel_bias=rel_bias, ffn_w_up=ffn_w_up,
             ffn_conv_w=ffn_conv_w, ffn_conv_b=ffn_conv_b, ffn_w_down=ffn_w_down, ln_gain=ln_gain, ln_bias=ln_bias)
    mom = dict(s5_lam_re=m_s5_lam_re, s5_lam_im=m_s5_lam_im, s5_log_dt=m_s5_log_dt, s5_b_re=m_s5_b_re,
               s5_b_im=m_s5_b_im, s5_c_re=m_s5_c_re, s5_c_im=m_s5_c_im, s5_d=m_s5_d, s5_w_glu=m_s5_w_glu,
               s5_b_glu=m_s5_b_glu, s5_w_out=m_s5_w_out, attn_w_kv=m_attn_w_kv, attn_w_q=m_attn_w_q,
               attn_w_out=m_attn_w_out, rel_bias=m_rel_bias, ffn_w_up=m_ffn_w_up, ffn_conv_w=m_ffn_conv_w,
               ffn_conv_b=m_ffn_conv_b, ffn_w_down=m_ffn_w_down, ln_gain=m_ln_gain, ln_bias=m_ln_bias)
    var = dict(s5_lam_re=v_s5_lam_re, s5_lam_im=v_s5_lam_im, s5_log_dt=v_s5_log_dt, s5_b_re=v_s5_b_re,
               s5_b_im=v_s5_b_im, s5_c_re=v_s5_c_re, s5_c_im=v_s5_c_im, s5_d=v_s5_d, s5_w_glu=v_s5_w_glu,
               s5_b_glu=v_s5_b_glu, s5_w_out=v_s5_w_out, attn_w_kv=v_attn_w_kv, attn_w_q=v_attn_w_q,
               attn_w_out=v_attn_w_out, rel_bias=v_rel_bias, ffn_w_up=v_ffn_w_up, ffn_conv_w=v_ffn_conv_w,
               ffn_conv_b=v_ffn_conv_b, ffn_w_down=v_ffn_w_down, ln_gain=v_ln_gain, ln_bias=v_ln_bias)
    bsz, seq, _ = x.shape
    me = 4 * lax.axis_index("x") + 2 * lax.axis_index("y") + lax.axis_index("c")

    ex = _Exchanges(w)
    loss_part, grad_x, g = _local_step(x, loss_target, dict(w), None, bsz, seq, ex=ex)
    loss = lax.psum(loss_part, ("x", "y", "c"))

    d = D_MODEL
    results = {}
    for name in BIG:
        shard = w[name]
        if name in ("ffn_w_up", "ffn_w_down"):
            parts, r3 = [ex.parts[name + "0"], ex.parts[name + "1"]], shard.shape
        else:
            parts, r3 = [ex.parts[name]], (1, -1, shard.shape[-1])
        outs4 = _adamw("adamw_" + name, parts, shard.reshape(r3), mom[name].reshape(r3), var[name].reshape(r3))
        results[name] = tuple(o.reshape(shard.shape) for o in outs4)

    small_names = SMALL_REPLICATED + SMALL_SHARDED
    packed = _pack([g[k] for k in small_names])
    pad = (-packed.shape[0]) % (8 * N_DEV)
    if pad:
        packed = jnp.concatenate([packed, jnp.zeros((pad, 128), F32)], axis=0)
    piece = packed.shape[0] // N_DEV
    slots = _exchange("reduce_small", [packed.reshape(N_DEV, piece, 128)], [_sds((N_DEV, piece, 128), F32)],
                      [(0, ("peer",), 0, ("me",))])[0]
    summed = _exchange("gather_small", [_sum_slots(slots)], [_sds((N_DEV, piece, 128), F32)],
                       [(0, (), 0, ("me",))])[0]
    total = _unpack(summed.reshape(N_DEV * piece, 128), [g[k].shape for k in small_names])
    gsum = dict(zip(small_names, total))
    mine = {k: gsum[k] for k in SMALL_REPLICATED}
    mine["s5_b_glu"] = lax.dynamic_slice_in_dim(gsum["s5_b_glu"], me * (d // N_DEV), d // N_DEV, axis=1)
    mine["ffn_conv_w"] = lax.dynamic_index_in_dim(gsum["ffn_conv_w"], me, axis=2, keepdims=False)
    mine["ln_gain"] = lax.dynamic_slice_in_dim(gsum["ln_gain"], me * (d // N_DEV), d // N_DEV, axis=2)
    mine["ln_bias"] = lax.dynamic_slice_in_dim(gsum["ln_bias"], me * (d // N_DEV), d // N_DEV, axis=2)
    gp = _pack([mine[k] for k in small_names])
    outs4 = _adamw("adamw_small", [gp[None]], _pack([w[k] for k in small_names])[None],
                   _pack([mom[k] for k in small_names])[None], _pack([var[k] for k in small_names])[None])
    outs4 = [o[0] for o in outs4]
    shapes = [w[k].shape for k in small_names]
    unpacked = [_unpack(o, shapes) for o in outs4]
    for i, k in enumerate(small_names):
        results[k] = tuple(unpacked[j][i] for j in range(4))

    out = [loss, grad_x]
    for j in range(4):
        out.extend(results[k][j] for k in WEIGHTS)
    return tuple(out)
```

```python
import functools
import math

import numpy as np
import jax
import jax.numpy as jnp
from jax import lax
from jax.experimental import pallas as pl
from jax.experimental.pallas import tpu as pltpu

F32 = jnp.float32
BF16 = jnp.bfloat16
HIGHEST = lax.Precision.HIGHEST

N_DEV = 8
D_MODEL = 1024
SSM_GROUP = 16
SSM_STATE = 64
SSM_CHUNKS = 8
CHUNK_CH = D_MODEL // SSM_CHUNKS
CHUNK_ST = CHUNK_CH // SSM_GROUP * SSM_STATE
N_STATE = D_MODEL // SSM_GROUP * SSM_STATE
HEAD_DIM = 64
N_HEADS = 16
BAND = 128
DILATIONS = (1, 4, 16)
REL_BUCKETS = 32
REL_MAX_DIST = 2048
NEG_BIG = -1e30
DN_ALPHA = (2.0 * 2) ** 0.25
LN_EPS = 1e-5
ADAM_LR = 0.001
ADAM_B1 = 0.9
ADAM_B2 = 0.999
ADAM_EPS = 1e-08
ADAM_WD = 0.01
ADAM_STEP = 10

TM = 512
TM_MM = 2048
TM_MM_F32 = 1024
T_SCAN = 512
VMEM_LIMIT = 56 * 1024 * 1024
FFN_HIDDEN = BF16

NN = (((1,), (0,)), ((), ()))
NT = (((1,), (1,)), ((), ()))
TN = (((0,), (0,)), ((), ()))


class _Plan:
    def __init__(self, srcs, out_shapes, n_sems, n_local, phases):
        self.srcs, self.out_shapes, self.n_sems, self.n_local, self.phases = srcs, out_shapes, n_sems, n_local, phases


def _call(body, name, grid, in_specs, out_specs, out_shape, scratch=(), comm=None, fill=()):
    params = dict(dimension_semantics=("arbitrary",) * len(grid), vmem_limit_bytes=VMEM_LIMIT)
    if comm is None and not fill:
        return pl.pallas_call(
            body, name=name, grid=grid, in_specs=in_specs, out_specs=out_specs, out_shape=out_shape,
            scratch_shapes=list(scratch), compiler_params=pltpu.CompilerParams(**params))
    single = not isinstance(out_specs, (list, tuple))
    o_specs = [out_specs] if single else list(out_specs)
    o_shape = [out_shape] if single else list(out_shape)
    n_in, n_out, n_scr, n_fill = len(in_specs), len(o_specs), len(scratch), len(fill)
    srcs = list(comm.srcs) if comm else []
    c_shapes = list(comm.out_shapes) if comm else []
    n_cin, n_cout = len(srcs), len(c_shapes)
    total = int(np.prod(grid))

    def wrapped(*refs):
        ins = refs[:n_in]
        refs = refs[n_in + n_fill:]
        cins, outs, couts = refs[:n_cin], refs[n_cin:n_cin + n_out], refs[n_cin + n_out:n_cin + n_out + n_cout]
        rest = refs[n_cin + n_out + n_cout:]
        scr, sems = rest[:n_scr], rest[n_scr:]
        if comm is None:
            body(*ins, *outs, *scr)
            return
        step = pl.program_id(0)
        for ax in range(1, len(grid)):
            step = step * grid[ax] + pl.program_id(ax)
        phases = comm.phases(cins, couts, *sems, total)
        for at, action in phases:
            if at == 0:
                pl.when(step == 0)(action)
        body(*ins, *outs, *scr)
        for at, action in phases:
            if at > 0:
                pl.when(step == at)(action)

    any_spec = pl.BlockSpec(memory_space=pl.ANY)
    sems = [] if comm is None else [pltpu.SemaphoreType.DMA((comm.n_sems,)), pltpu.SemaphoreType.DMA((comm.n_sems,)),
                                    pltpu.SemaphoreType.DMA((max(comm.n_local, 1),))]
    call = pl.pallas_call(
        wrapped, name=name, grid=grid, in_specs=list(in_specs) + [any_spec] * (n_fill + n_cin),
        out_specs=o_specs + [any_spec] * n_cout, out_shape=o_shape + c_shapes,
        scratch_shapes=list(scratch) + sems, input_output_aliases={n_in + j: j for j in range(n_fill)},
        compiler_params=pltpu.CompilerParams(has_side_effects=comm is not None, **params))

    def run(*args):
        res = call(*args, *fill, *srcs)
        own = res[0] if single else res[:n_out]
        return (own, res[n_out:]) if comm is not None else own

    return run


def _sds(shape, dtype):
    return jax.ShapeDtypeStruct(shape, dtype)


def _mm(name, a, b, *, dims, grid, a_spec, b_spec, o_spec, out_shape, nred=1, red_axis=0,
        add=None, add_spec=None, add_scale=1.0, comm=None, norm=None):
    has_add = add is not None
    n_extra = (1 if has_add else 0) + (2 if norm else 0)
    n_out = 3 if norm else 1
    acc_shape = tuple(s for s in o_spec.block_shape if s is not None)

    def body(*refs):
        a_ref, b_ref = refs[:2]
        extra = refs[2:2 + n_extra]
        outs = refs[2 + n_extra:2 + n_extra + n_out]
        rest = refs[2 + n_extra + n_out:]
        o_ref = outs[0]
        prod = lax.dot_general(a_ref[...].astype(BF16), b_ref[...].astype(BF16), dims,
                               preferred_element_type=F32)

        def finish(val):
            if has_add:
                val = val + add_scale * extra[0][...].astype(F32)
            o_ref[...] = val.astype(o_ref.dtype)
            if norm:
                xhat, _ = _ln_stats(val)
                y = xhat * extra[-2][...] + extra[-1][...]
                outs[1][...] = y
                outs[2][...] = y.astype(BF16)

        if nred == 1:
            finish(prod)
        else:
            acc = rest[0]
            k = pl.program_id(red_axis)

            @pl.when(k == 0)
            def _():
                acc[...] = prod

            @pl.when(k > 0)
            def _():
                acc[...] += prod

            @pl.when(k == nred - 1)
            def _():
                finish(acc[...])

    in_specs = [a_spec, b_spec] + ([add_spec] if has_add else [])
    scratch = [pltpu.VMEM(acc_shape, F32)] if nred > 1 else []
    args = (a, b) + ((add,) if has_add else ())
    if norm:
        vec = pl.BlockSpec((1, out_shape.shape[1]), lambda *_: (0, 0))
        in_specs += [vec, vec]
        args += tuple(norm)
        o_spec = [o_spec, o_spec, o_spec]
        out_shape = [out_shape, out_shape, _sds(out_shape.shape, BF16)]
    return _call(body, name, grid, in_specs, o_spec, out_shape, scratch, comm=comm)(*args)


def _mm_nn(name, a, b, out_dtype, blocked_out=False, comm=None, blocks=None, residual=None, norm=None):
    TM = TM_MM if out_dtype == BF16 else TM_MM_F32
    m, k = a.shape
    _, _, n = b.shape
    nb = b.shape[0] if blocks is None else blocks[1]
    first = 0 if blocks is None else blocks[0]
    if blocked_out:
        o_spec = pl.BlockSpec((None, TM, n), lambda d, i: (d, i, 0))
        out_shape = _sds((nb, m, n), out_dtype)
    else:
        o_spec = pl.BlockSpec((TM, n), lambda d, i: (i, d))
        out_shape = _sds((m, nb * n), out_dtype)
    return _mm(name, a, b, dims=NN, grid=(nb, m // TM),
               a_spec=pl.BlockSpec((TM, k), lambda d, i: (i, 0)),
               b_spec=pl.BlockSpec((None, k, n), lambda d, i: (d + first, 0, 0)),
               o_spec=o_spec, out_shape=out_shape, comm=comm, norm=norm,
               add=residual, add_spec=o_spec if residual is not None else None, add_scale=DN_ALPHA)


def _mm_nn_red(name, a, b, out_dtype, comm=None, residual=None, norm=None):
    TM = TM_MM_F32
    nb, m, k = a.shape
    n = b.shape[2]
    o_spec = pl.BlockSpec((TM, n), lambda i, d: (i, 0))
    return _mm(name, a, b, dims=NN, grid=(m // TM, nb), nred=nb, red_axis=1,
               a_spec=pl.BlockSpec((None, TM, k), lambda i, d: (d, i, 0)),
               b_spec=pl.BlockSpec((None, k, n), lambda i, d: (d, 0, 0)),
               o_spec=o_spec, out_shape=_sds((m, n), out_dtype), comm=comm, norm=norm,
               add=residual, add_spec=o_spec if residual is not None else None, add_scale=DN_ALPHA)


def _mm_tn(name, a, g, nb, out_dtype, a_blocked=False, g_mode="cols", comm=None):
    TM = TM_MM
    if a_blocked:
        _, m, ka = a.shape
        a_spec = pl.BlockSpec((None, TM, ka), lambda d, i: (d, i, 0))
    else:
        m, ka = a.shape
        a_spec = pl.BlockSpec((TM, ka), lambda d, i: (i, 0))
    if g_mode == "cols":
        n = g.shape[1] // nb
        g_spec = pl.BlockSpec((TM, n), lambda d, i: (i, d))
    elif g_mode == "blocked":
        n = g.shape[2]
        g_spec = pl.BlockSpec((None, TM, n), lambda d, i: (d, i, 0))
    else:
        n = g.shape[1]
        g_spec = pl.BlockSpec((TM, n), lambda d, i: (i, 0))
    nred = m // TM
    return _mm(name, a, g, dims=TN, grid=(nb, nred), nred=nred, red_axis=1,
               a_spec=a_spec, b_spec=g_spec,
               o_spec=pl.BlockSpec((None, ka, n), lambda d, i: (d, 0, 0)),
               out_shape=_sds((nb, ka, n), out_dtype), comm=comm)


def _mm_nt_red(name, g, w, out_dtype, g_blocked, add=None, add_scale=1.0, comm=None, blocks=None):
    TM = TM_MM_F32
    _, k, n = w.shape
    nb = w.shape[0] if blocks is None else blocks[1]
    first = 0 if blocks is None else blocks[0]
    if g_blocked:
        m = g.shape[1]
        g_spec = pl.BlockSpec((None, TM, n), lambda i, d: (d, i, 0))
    else:
        m = g.shape[0]
        g_spec = pl.BlockSpec((TM, n), lambda i, d: (i, d))
    o_spec = pl.BlockSpec((TM, k), lambda i, d: (i, 0))
    return _mm(name, g, w, dims=NT, grid=(m // TM, nb), nred=nb, red_axis=1,
               a_spec=g_spec, b_spec=pl.BlockSpec((None, k, n), lambda i, d: (d + first, 0, 0)),
               o_spec=o_spec, out_shape=_sds((m, k), out_dtype),
               add=add, add_spec=o_spec if add is not None else None, add_scale=add_scale, comm=comm)


def _mm_nt_out(name, g, w, out_dtype):
    TM = TM_MM
    m, n = g.shape
    nb, k, _ = w.shape
    return _mm(name, g, w, dims=NT, grid=(nb, m // TM),
               a_spec=pl.BlockSpec((TM, n), lambda d, i: (i, 0)),
               b_spec=pl.BlockSpec((None, k, n), lambda d, i: (d, 0, 0)),
               o_spec=pl.BlockSpec((None, TM, k), lambda d, i: (d, i, 0)),
               out_shape=_sds((nb, m, k), out_dtype))


def _ln_stats(r):
    mu = jnp.mean(r, axis=-1, keepdims=True)
    xc = r - mu
    var = jnp.mean(xc * xc, axis=-1, keepdims=True)
    rstd = lax.rsqrt(var + LN_EPS)
    return xc * rstd, rstd


def _ln_bwd(name, r, dy, gain):
    n, d = r.shape

    def body(r_ref, dy_ref, g_ref, dr_ref, drb_ref, dg_ref, db_ref):
        i = pl.program_id(0)
        xhat, rstd = _ln_stats(r_ref[...])
        dy_v = dy_ref[...]
        dxh = dy_v * g_ref[...]
        m1 = jnp.mean(dxh, axis=-1, keepdims=True)
        m2 = jnp.mean(dxh * xhat, axis=-1, keepdims=True)
        dr = rstd * (dxh - m1 - xhat * m2)
        dr_ref[...] = dr
        drb_ref[...] = dr.astype(BF16)
        dg = jnp.sum(dy_v * xhat, axis=0, keepdims=True)
        db = jnp.sum(dy_v, axis=0, keepdims=True)

        @pl.when(i == 0)
        def _():
            dg_ref[...] = dg
            db_ref[...] = db

        @pl.when(i > 0)
        def _():
            dg_ref[...] += dg
            db_ref[...] += db

    row = pl.BlockSpec((TM, d), lambda i: (i, 0))
    vec = pl.BlockSpec((1, d), lambda i: (0, 0))
    return _call(body, name, (n // TM,), [row, row, vec], [row, row, vec, vec],
                 [_sds((n, d), F32), _sds((n, d), BF16), _sds((1, d), F32), _sds((1, d), F32)])(r, dy, gain)


def _loss_head(y, target):
    n, d = y.shape

    def body(y_ref, t_ref, l_ref, dy_ref):
        i = pl.program_id(0)
        e = y_ref[...] - t_ref[...]
        dy_ref[...] = e * (1.0 / d)
        part = jnp.sum(jnp.sum(e * e, axis=1, keepdims=True), axis=0, keepdims=True) * (0.5 / d)
        part = jnp.broadcast_to(part, (8, 128))

        @pl.when(i == 0)
        def _():
            l_ref[...] = part

        @pl.when(i > 0)
        def _():
            l_ref[...] += part

    row = pl.BlockSpec((TM, d), lambda i: (i, 0))
    return _call(body, "loss_head", (n // TM,), [row, row],
                 [pl.BlockSpec((8, 128), lambda i: (0, 0)), row],
                 [_sds((8, 128), F32), _sds((n, d), F32)])(y, target)


def _group_expand_matrix():
    e = np.zeros((D_MODEL // SSM_GROUP, N_STATE), np.float32)
    for g in range(D_MODEL // SSM_GROUP):
        e[g, g * SSM_STATE:(g + 1) * SSM_STATE] = 1.0
    return jnp.asarray(e)


def _discretise(lr, li, ldt, br, bi, expand):
    dt = jnp.dot(jnp.exp(ldt), expand, precision=HIGHEST, preferred_element_type=F32)
    mag = jnp.exp(lr * dt)
    th = li * dt
    ab_r = mag * jnp.cos(th)
    ab_i = mag * jnp.sin(th)
    den = lr * lr + li * li
    nr = ab_r - 1.0
    co_r = (nr * lr + ab_i * li) / den
    co_i = (ab_i * lr - nr * li) / den
    bb_r = co_r * br - co_i * bi
    bb_i = co_r * bi + co_i * br
    return ab_r, ab_i, bb_r, bb_i


def _cmul(ar, ai, br, bi):
    return ar * br - ai * bi, ar * bi + ai * br


def _s5_prep(lr, li, ldt, br, bi):
    s = N_STATE
    expand = _group_expand_matrix()

    def body(lr_ref, li_ref, ldt_ref, br_ref, bi_ref, e_ref, abr_ref, abi_ref, bbr_ref, bbi_ref, tab_ref):
        ab_r, ab_i, bb_r, bb_i = _discretise(lr_ref[...], li_ref[...], ldt_ref[...], br_ref[...],
                                             bi_ref[...], e_ref[...])
        abr_ref[...] = ab_r
        abi_ref[...] = ab_i
        bbr_ref[...] = bb_r
        bbi_ref[...] = bb_i
        row = lax.broadcasted_iota(jnp.int32, (8, s), 0)
        for base, sign in ((0, 1.0), (8, -1.0)):
            p = [None] * 9
            p[1] = (ab_r, sign * ab_i)
            p[2] = _cmul(*p[1], *p[1])
            p[3] = _cmul(*p[2], *p[1])
            p[4] = _cmul(*p[2], *p[2])
            p[5] = _cmul(*p[4], *p[1])
            p[6] = _cmul(*p[4], *p[2])
            p[7] = _cmul(*p[4], *p[3])
            p[8] = _cmul(*p[4], *p[4])
            for j, k in enumerate((1, 2, 4)):
                keep = (row >= k) if base == 0 else (row < 8 - k)
                tab_ref[base + 2 * j] = jnp.where(keep, jnp.broadcast_to(p[k][0], (8, s)), 0.0)
                tab_ref[base + 2 * j + 1] = jnp.where(keep, jnp.broadcast_to(p[k][1], (8, s)), 0.0)
            pw_r = jnp.zeros((8, s), F32)
            pw_i = jnp.zeros((8, s), F32)
            for i in range(8):
                e = i + 1 if base == 0 else 8 - i
                pw_r = jnp.where(row == i, jnp.broadcast_to(p[e][0], (8, s)), pw_r)
                pw_i = jnp.where(row == i, jnp.broadcast_to(p[e][1], (8, s)), pw_i)
            tab_ref[base + 6] = pw_r
            tab_ref[base + 7] = pw_i

    def full(shape):
        return pl.BlockSpec(shape, lambda i: (0,) * len(shape))

    g = D_MODEL // SSM_GROUP
    return _call(body, "s5_prep", (1,),
                 [full((1, s)), full((1, s)), full((1, g)), full((16, s)), full((16, s)), full((g, s))],
                 [full((1, s)), full((1, s)), full((16, s)), full((16, s)), full((16, 8, s))],
                 [_sds((1, s), F32), _sds((1, s), F32), _sds((16, s), F32), _sds((16, s), F32),
                  _sds((16, 8, s), F32)])(lr, li, ldt, br, bi, expand)


def _s5_prep_bwd(lr, li, ldt, br, bi, d_abr, d_abi, d_bbr, d_bbi):
    s = N_STATE
    g = D_MODEL // SSM_GROUP
    expand = _group_expand_matrix()

    def body(lr_ref, li_ref, ldt_ref, br_ref, bi_ref, e_ref, c1, c2, c3, c4, o1, o2, o3, o4, o5):
        e_val = e_ref[...]
        _, vjp = jax.vjp(lambda a, b, c, d, e: _discretise(a, b, c, d, e, e_val),
                         lr_ref[...], li_ref[...], ldt_ref[...], br_ref[...], bi_ref[...])
        grads = vjp((c1[...], c2[...], c3[...], c4[...]))
        for o, gr in zip((o1, o2, o3, o4, o5), grads):
            o[...] = gr

    def full(shape):
        return pl.BlockSpec(shape, lambda i: (0,) * len(shape))

    return _call(body, "s5_prep_bwd", (1,),
                 [full((1, s)), full((1, s)), full((1, g)), full((16, s)), full((16, s)), full((g, s)),
                  full((1, s)), full((1, s)), full((16, s)), full((16, s))],
                 [full((1, s)), full((1, s)), full((1, g)), full((16, s)), full((16, s))],
                 [_sds((1, s), F32), _sds((1, s), F32), _sds((1, g), F32), _sds((16, s), F32),
                  _sds((16, s), F32)])(lr, li, ldt, br, bi, expand, d_abr, d_abi, d_bbr, d_bbi)


def _scan8(vr, vi, tab_ref, base, reverse):
    for j, k in enumerate((1, 2, 4)):
        mr = tab_ref[base + 2 * j]
        mi = tab_ref[base + 2 * j + 1]
        shift = 8 - k if reverse else k
        sr = pltpu.roll(vr, shift, 0)
        si = pltpu.roll(vi, shift, 0)
        vr, vi = vr + mr * sr - mi * si, vi + mr * si + mi * sr
    return vr, vi


def _s5_scan_fwd(x, bbr, bbi, cr, ci, dvec, tab, seq, comm=None):
    n, d = x.shape
    nt = seq // T_SCAN
    nblk = T_SCAN // 8
    st = CHUNK_ST

    def body(x_ref, bbr_ref, bbi_ref, cr_ref, ci_ref, d_ref, tab_ref, y_ref, hr_ref, hi_ref, car_r, car_i):
        t = pl.program_id(2)

        @pl.when(t == 0)
        def _():
            car_r[...] = jnp.zeros_like(car_r)
            car_i[...] = jnp.zeros_like(car_i)

        u = x_ref[...]
        ub = u.astype(BF16)
        hr_ref[...] = jnp.dot(ub, bbr_ref[...], preferred_element_type=F32)
        hi_ref[...] = jnp.dot(ub, bbi_ref[...], preferred_element_type=F32)

        def step(i, carry):
            sl = pl.ds(pl.multiple_of(i * 8, 8), 8)
            vr, vi = _scan8(hr_ref[sl, :], hi_ref[sl, :], tab_ref, 0, False)
            c_r = jnp.broadcast_to(car_r[...], (8, st))
            c_i = jnp.broadcast_to(car_i[...], (8, st))
            pr = tab_ref[6]
            pi = tab_ref[7]
            vr, vi = vr + pr * c_r - pi * c_i, vi + pr * c_i + pi * c_r
            hr_ref[sl, :] = vr
            hi_ref[sl, :] = vi
            car_r[...] = vr[7:8, :]
            car_i[...] = vi[7:8, :]
            return carry

        lax.fori_loop(0, nblk, step, 0, unroll=4)
        y = lax.dot_general(hr_ref[...].astype(BF16), cr_ref[...], NT, preferred_element_type=F32)
        y = y - lax.dot_general(hi_ref[...].astype(BF16), ci_ref[...], NT, preferred_element_type=F32)
        y_ref[...] = y + d_ref[...] * u

    row = lambda c, b, t: (b * nt + t, c)
    mat = pl.BlockSpec((None, CHUNK_CH, st), lambda c, b, t: (c, 0, 0))
    return _call(
        body, "s5_scan_fwd", (SSM_CHUNKS, n // seq, nt),
        [pl.BlockSpec((T_SCAN, CHUNK_CH), row), mat, mat, mat, mat,
         pl.BlockSpec((1, CHUNK_CH), lambda c, b, t: (0, c)),
         pl.BlockSpec((16, 8, st), lambda c, b, t: (0, 0, c))],
        [pl.BlockSpec((T_SCAN, CHUNK_CH), row), pl.BlockSpec((T_SCAN, st), row), pl.BlockSpec((T_SCAN, st), row)],
        [_sds((n, d), F32), _sds((n, N_STATE), F32), _sds((n, N_STATE), F32)],
        [pltpu.VMEM((1, st), F32), pltpu.VMEM((1, st), F32)], comm=comm,
    )(x, bbr, bbi, cr, ci, dvec, tab)


def _s5_scan_bwd(x, dy, dres, hr, hi, bbr, bbi, cr, ci, dvec, tab, seq, comm=None):
    n, d = x.shape
    nt = seq // T_SCAN
    nblk = T_SCAN // 8
    st = CHUNK_ST

    def body(x_ref, dy_ref, dres_ref, hr_ref, hi_ref, hpr_ref, hpi_ref, bbr_ref, bbi_ref, cr_ref, ci_ref,
             d_ref, tab_ref, dx_ref, dcr_ref, dci_ref, dbr_ref, dbi_ref, dar_ref, dai_ref, dd_ref,
             g_r, g_i, car_r, car_i):
        b = pl.program_id(1)
        tg = pl.program_id(2)
        first = jnp.logical_and(b == 0, tg == 0)

        @pl.when(tg == 0)
        def _():
            car_r[...] = jnp.zeros_like(car_r)
            car_i[...] = jnp.zeros_like(car_i)

        u = x_ref[...]
        dyv = dy_ref[...]
        ub = u.astype(BF16)
        dyb = dyv.astype(BF16)
        g_r[...] = jnp.dot(dyb, cr_ref[...], preferred_element_type=F32)
        g_i[...] = -jnp.dot(dyb, ci_ref[...], preferred_element_type=F32)

        def step(ii, carry):
            i = nblk - 1 - ii
            sl = pl.ds(pl.multiple_of(i * 8, 8), 8)
            vr, vi = _scan8(g_r[sl, :], g_i[sl, :], tab_ref, 8, True)
            c_r = jnp.broadcast_to(car_r[...], (8, st))
            c_i = jnp.broadcast_to(car_i[...], (8, st))
            pr = tab_ref[14]
            pi = tab_ref[15]
            vr, vi = vr + pr * c_r - pi * c_i, vi + pr * c_i + pi * c_r
            g_r[sl, :] = vr
            g_i[sl, :] = vi
            car_r[...] = vr[0:1, :]
            car_i[...] = vi[0:1, :]
            return carry

        lax.fori_loop(0, nblk, step, 0, unroll=4)
        gr = g_r[...]
        gi = g_i[...]
        grb = gr.astype(BF16)
        gib = gi.astype(BF16)
        du = lax.dot_general(grb, bbr_ref[...], NT, preferred_element_type=F32)
        du = du + lax.dot_general(gib, bbi_ref[...], NT, preferred_element_type=F32)
        dx_ref[...] = DN_ALPHA * dres_ref[...] + du + d_ref[...] * dyv

        h_r = hr_ref[...]
        h_i = hi_ref[...]
        t_orig = nt - 1 - tg
        row0 = lax.broadcasted_iota(jnp.int32, (8, st), 0) == 0
        halo_ok = t_orig > 0

        def previous(h, halo_ref):
            top = jnp.broadcast_to(jnp.where(halo_ok, halo_ref[7:8, :], 0.0), (8, st))
            rolled = pltpu.roll(h, 1, 0)
            return jnp.concatenate([jnp.where(row0, top, rolled[:8, :]), rolled[8:, :]], axis=0)

        hp_r = previous(h_r, hpr_ref)
        hp_i = previous(h_i, hpi_ref)
        dar = jnp.sum(gr * hp_r + gi * hp_i, axis=0, keepdims=True)
        dai = jnp.sum(gi * hp_r - gr * hp_i, axis=0, keepdims=True)
        dcr = lax.dot_general(dyb, h_r.astype(BF16), TN, preferred_element_type=F32)
        dci = lax.dot_general(dyb, h_i.astype(BF16), TN, preferred_element_type=F32)
        dbr = lax.dot_general(ub, grb, TN, preferred_element_type=F32)
        dbi = lax.dot_general(ub, gib, TN, preferred_element_type=F32)
        dd = jnp.sum(dyv * u, axis=0, keepdims=True)

        @pl.when(first)
        def _():
            dcr_ref[...] = dcr
            dci_ref[...] = dci
            dbr_ref[...] = dbr
            dbi_ref[...] = dbi
            dar_ref[...] = dar
            dai_ref[...] = dai
            dd_ref[...] = dd

        @pl.when(jnp.logical_not(first))
        def _():
            dcr_ref[...] += dcr
            dci_ref[...] += dci
            dbr_ref[...] += dbr
            dbi_ref[...] += dbi
            dar_ref[...] += dar
            dai_ref[...] += dai
            dd_ref[...] += dd

    row = lambda c, b, t: (b * nt + (nt - 1 - t), c)
    halo = lambda c, b, t: (jnp.maximum((b * nt + (nt - 1 - t)) * (T_SCAN // 8) - 1, 0), c)
    mat = pl.BlockSpec((None, CHUNK_CH, st), lambda c, b, t: (c, 0, 0))
    chan = pl.BlockSpec((T_SCAN, CHUNK_CH), row)
    state = pl.BlockSpec((T_SCAN, st), row)
    return _call(
        body, "s5_scan_bwd", (SSM_CHUNKS, n // seq, nt),
        [chan, chan, chan, state, state, pl.BlockSpec((8, st), halo), pl.BlockSpec((8, st), halo),
         mat, mat, mat, mat, pl.BlockSpec((1, CHUNK_CH), lambda c, b, t: (0, c)),
         pl.BlockSpec((16, 8, st), lambda c, b, t: (0, 0, c))],
        [chan, mat, mat, mat, mat, pl.BlockSpec((1, st), lambda c, b, t: (0, c)),
         pl.BlockSpec((1, st), lambda c, b, t: (0, c)), pl.BlockSpec((1, CHUNK_CH), lambda c, b, t: (0, c))],
        [_sds((n, d), F32)] + [_sds((SSM_CHUNKS, CHUNK_CH, st), F32)] * 4
        + [_sds((1, N_STATE), F32), _sds((1, N_STATE), F32), _sds((1, d), F32)],
        [pltpu.VMEM((T_SCAN, st), F32), pltpu.VMEM((T_SCAN, st), F32), pltpu.VMEM((1, st), F32),
         pltpu.VMEM((1, st), F32)], comm=comm,
    )(x, dy, dres, hr, hi, hr, hi, bbr, bbi, cr, ci, dvec, tab)


_GELU_C = math.sqrt(2.0 / math.pi)


def _gelu_parts(y):
    t = jnp.tanh(_GELU_C * (y + 0.044715 * (y * y * y)))
    return 0.5 * (1.0 + t), t


def _glu_fwd(y, w_glu, b_glu):
    n, d = y.shape

    def body(y_ref, w_ref, b_ref, y2_ref, z_ref, gg_ref):
        yv = y_ref[...]
        cdf, _ = _gelu_parts(yv)
        y2 = yv * cdf
        z = jnp.dot(y2.astype(BF16), w_ref[...], preferred_element_type=F32) + b_ref[...]
        y2_ref[...] = y2.astype(BF16)
        z_ref[...] = z
        gg_ref[...] = (y2 * jax.nn.sigmoid(z)).astype(BF16)

    row = pl.BlockSpec((TM, d), lambda i: (i, 0))
    return _call(body, "glu_fwd", (n // TM,),
                 [row, pl.BlockSpec((d, d), lambda i: (0, 0)), pl.BlockSpec((1, d), lambda i: (0, 0))],
                 [row, row, row], [_sds((n, d), BF16), _sds((n, d), F32), _sds((n, d), BF16)])(y, w_glu, b_glu)


def _glu_bwd_gate(y, z, dgg):
    n, d = y.shape

    def body(y_ref, z_ref, dgg_ref, dz_ref, db_ref, t_ref):
        i = pl.program_id(0)
        yv = y_ref[...]
        cdf, _ = _gelu_parts(yv)
        y2 = yv * cdf
        s = jax.nn.sigmoid(z_ref[...])
        dg = dgg_ref[...]
        dz = dg * y2 * s * (1.0 - s)
        dz_ref[...] = dz.astype(BF16)
        t_ref[...] = dg * s
        db = jnp.sum(dz, axis=0, keepdims=True)

        @pl.when(i == 0)
        def _():
            db_ref[...] = db

        @pl.when(i > 0)
        def _():
            db_ref[...] += db

    row = pl.BlockSpec((TM, d), lambda i: (i, 0))
    vec = pl.BlockSpec((1, d), lambda i: (0, 0))
    return _call(body, "glu_bwd_gate", (n // TM,), [row, row, row], [row, vec, row],
                 [_sds((n, d), BF16), _sds((1, d), F32), _sds((n, d), F32)])(y, z, dgg)


def _gelu_bwd(y, dy2):
    n, d = y.shape

    def body(y_ref, g_ref, o_ref):
        yv = y_ref[...]
        cdf, t = _gelu_parts(yv)
        dinner = _GELU_C * (1.0 + 3.0 * 0.044715 * yv * yv)
        o_ref[...] = g_ref[...] * (cdf + 0.5 * yv * (1.0 - t * t) * dinner)

    row = pl.BlockSpec((TM, d), lambda i: (i, 0))
    return _call(body, "gelu_bwd", (n // TM,), [row, row], row, _sds((n, d), F32))(y, dy2)


HALO = 16


def _shift_down(x, halo, k):
    out = pltpu.roll(x, k, 0)
    row = lax.broadcasted_iota(jnp.int32, (8, x.shape[1]), 0)
    top = jnp.where(row < k, pltpu.roll(halo[HALO - 8:HALO, :], k, 0), out[:8, :])
    return jnp.concatenate([top, out[8:, :]], axis=0)


def _conv3(x, halo, w, b):
    x1 = _shift_down(x, halo, 1)
    x2 = _shift_down(x, halo, 2)
    return b + w[0:1, :] * x + w[1:2, :] * x1 + w[2:3, :] * x2, x1, x2


def _conv_gate_fwd(hc, conv_w, conv_b, seq):
    _, nb, n, c = hc.shape
    tiles_per_seq = seq // TM

    def body(x_ref, halo_ref, w_ref, b_ref, o_ref):
        i = pl.program_id(1)
        start = (i % tiles_per_seq) == 0
        cv = []
        for h in range(2):
            halo = jnp.where(start, 0.0, halo_ref[h].astype(F32))
            cv.append(_conv3(x_ref[h].astype(F32), halo, w_ref[h], b_ref[h])[0])
        o_ref[...] = (cv[1] * jax.nn.sigmoid(cv[1]) * cv[0]).astype(BF16)

    return _call(
        body, "conv_gate_fwd", (nb, n // TM),
        [pl.BlockSpec((2, None, TM, c), lambda d, i: (0, d, i, 0)),
         pl.BlockSpec((2, None, HALO, c), lambda d, i: (0, d, jnp.maximum(i * (TM // HALO) - 1, 0), 0)),
         pl.BlockSpec((2, None, 3, c), lambda d, i: (0, d, 0, 0)),
         pl.BlockSpec((2, None, 1, c), lambda d, i: (0, d, 0, 0))],
        pl.BlockSpec((None, TM, c), lambda d, i: (d, i, 0)),
        _sds((nb, n, c), BF16))(hc, hc, conv_w, conv_b)


def _conv_gate_bwd(hc, dact, conv_w, conv_b, seq, comm=None):
    _, nb, n, c = hc.shape
    tiles_per_seq = seq // TM
    last_halo = n // HALO - 1
    ext = TM + HALO

    def body(x_ref, prev_ref, next_ref, da_ref, dan_ref, w_ref, b_ref, dx_ref, dw_ref):
        i = pl.program_id(1)
        start = (i % tiles_per_seq) == 0
        end = (i % tiles_per_seq) == tiles_per_seq - 1
        row = lax.broadcasted_iota(jnp.int32, (ext, c), 0)
        xe, cv = [], []
        for h in range(2):
            halo = jnp.where(start, 0.0, prev_ref[h].astype(F32))
            x_ext = jnp.concatenate([x_ref[h], next_ref[h]], axis=0).astype(F32)
            conv, x1, x2 = _conv3(x_ext, halo, w_ref[h], b_ref[h])
            xe.append((x_ext, x1, x2))
            cv.append(conv)
        da = jnp.concatenate([da_ref[...], dan_ref[...]], axis=0).astype(F32)
        da = jnp.where(jnp.logical_and(end, row >= TM), 0.0, da)
        sg = jax.nn.sigmoid(cv[1])
        silu = cv[1] * sg
        dcv = [da * silu, da * cv[0] * (sg + silu * (1.0 - sg))]
        for h in range(2):
            w = w_ref[h]
            g = dcv[h]
            dx = w[0:1, :] * g + w[1:2, :] * pltpu.roll(g, ext - 1, 0) + w[2:3, :] * pltpu.roll(g, ext - 2, 0)
            dx_ref[h] = dx[:TM, :].astype(BF16)
            x_ext, x1, x2 = xe[h]
            gt = g[:TM, :]
            parts = [jnp.sum(gt * x_ext[:TM, :], axis=0, keepdims=True),
                     jnp.sum(gt * x1[:TM, :], axis=0, keepdims=True),
                     jnp.sum(gt * x2[:TM, :], axis=0, keepdims=True),
                     jnp.sum(gt, axis=0, keepdims=True)]
            upd = jnp.concatenate(parts + [jnp.zeros((4, c), F32)], axis=0)

            @pl.when(i == 0)
            def _():
                dw_ref[h] = upd

            @pl.when(i > 0)
            def _():
                dw_ref[h] += upd

    nxt = lambda i: jnp.minimum((i + 1) * (TM // HALO), last_halo)
    return _call(
        body, "conv_gate_bwd", (nb, n // TM),
        [pl.BlockSpec((2, None, TM, c), lambda d, i: (0, d, i, 0)),
         pl.BlockSpec((2, None, HALO, c), lambda d, i: (0, d, jnp.maximum(i * (TM // HALO) - 1, 0), 0)),
         pl.BlockSpec((2, None, HALO, c), lambda d, i: (0, d, nxt(i), 0)),
         pl.BlockSpec((None, TM, c), lambda d, i: (d, i, 0)),
         pl.BlockSpec((None, HALO, c), lambda d, i: (d, nxt(i), 0)),
         pl.BlockSpec((2, None, 3, c), lambda d, i: (0, d, 0, 0)),
         pl.BlockSpec((2, None, 1, c), lambda d, i: (0, d, 0, 0))],
        [pl.BlockSpec((2, None, TM, c), lambda d, i: (0, d, i, 0)),
         pl.BlockSpec((2, None, 8, c), lambda d, i: (0, d, 0, 0))],
        [_sds((2, nb, n, c), BF16), _sds((2, nb, 8, c), F32)], comm=comm)(hc, hc, hc, dact, dact, conv_w, conv_b)


def _t5_bucket_np(dist):
    exact = REL_BUCKETS // 2
    d = np.maximum(dist, 1).astype(np.float32)
    large = exact + (np.log(d / exact) / math.log(REL_MAX_DIST / exact) * (REL_BUCKETS - exact)).astype(np.int64)
    large = np.minimum(large, REL_BUCKETS - 1)
    return np.where(dist < exact, dist, large).astype(np.int32)


def _bucket_table(dil):
    steps = np.arange(BAND)[:, None] + BAND - np.arange(2 * BAND)[None, :]
    return _t5_bucket_np(np.maximum(steps, 0) * dil).reshape(1, BAND * 2 * BAND).astype(np.float32)


def _bias_expand(rel_t, dil):
    nk = BAND * 2 * BAND
    bucket = jnp.asarray(_bucket_table(dil))

    def body(r_ref, bk_ref, o_ref):
        ids = lax.broadcasted_iota(jnp.int32, (REL_BUCKETS, nk), 0).astype(F32)
        onehot = (ids == bk_ref[...]).astype(F32)
        o_ref[...] = jnp.dot(r_ref[...], onehot, precision=HIGHEST, preferred_element_type=F32)

    return _call(body, "bias_expand", (1,),
                 [pl.BlockSpec((N_HEADS, REL_BUCKETS), lambda i: (0, 0)), pl.BlockSpec((1, nk), lambda i: (0, 0))],
                 pl.BlockSpec((N_HEADS, nk), lambda i: (0, 0)), _sds((N_HEADS, nk), F32))(rel_t, bucket)


def _bias_reduce(dbias, dil):
    nk = BAND * 2 * BAND
    bucket = jnp.asarray(_bucket_table(dil))

    def body(g_ref, bk_ref, o_ref):
        ids = lax.broadcasted_iota(jnp.int32, (REL_BUCKETS, nk), 0).astype(F32)
        onehot = (ids == bk_ref[...]).astype(F32)
        o_ref[...] = lax.dot_general(g_ref[...], onehot, NT, precision=HIGHEST, preferred_element_type=F32)

    return _call(body, "bias_reduce", (1,),
                 [pl.BlockSpec((N_HEADS, nk), lambda i: (0, 0)), pl.BlockSpec((1, nk), lambda i: (0, 0))],
                 pl.BlockSpec((N_HEADS, REL_BUCKETS), lambda i: (0, 0)),
                 _sds((N_HEADS, REL_BUCKETS), F32))(dbias, bucket)


def _valid_mask(n):
    qi = lax.broadcasted_iota(jnp.int32, (BAND, 2 * BAND), 0)
    kj = lax.broadcasted_iota(jnp.int32, (BAND, 2 * BAND), 1)
    steps = qi + BAND - kj
    in_band = jnp.logical_and(steps >= 0, steps <= BAND)
    return jnp.logical_and(in_band, jnp.logical_or(n > 0, kj >= BAND))


ATTN_COLS = {1: 1024, 4: 512, 16: 128}


def _residue(dil, r):
    return slice(None) if dil == 1 else pl.ds(r, BAND, stride=dil)


def _stack_heads(x, first):
    return jnp.concatenate([jnp.where(first, x, 0.0), jnp.where(first, 0.0, x)], axis=0)


def _attn_fwd(q, k, v, bias, g, dil, bsz, seq, comm=None):
    n = q.shape[0]
    rows = BAND * dil
    nch = seq // rows
    w = N_HEADS * HEAD_DIM
    wc = ATTN_COLS[dil]
    ncol = w // wc
    hpc = wc // HEAD_DIM
    nj = wc // 128
    slab = lambda k: pltpu.VMEM((k * nj, rows, 128), F32)

    def body(q_ref, kc_ref, kp_ref, vc_ref, vp_ref, b_ref, o_ref, l_ref, qf, kf, vf, of, lf):
        ch = pl.program_id(2)
        valid = _valid_mask(ch)
        valid = jnp.concatenate([valid, valid], axis=0)
        lane = lax.broadcasted_iota(jnp.int32, (BAND, 128), 1)
        first = lane < HEAD_DIM
        for j in range(nj):
            sl = slice(128 * j, 128 * (j + 1))
            qf[j] = q_ref[:, sl].astype(F32)
            kf[j] = kp_ref[:, sl].astype(F32)
            kf[nj + j] = kc_ref[:, sl].astype(F32)
            vf[j] = vp_ref[:, sl].astype(F32)
            vf[nj + j] = vc_ref[:, sl].astype(F32)
        for r in range(dil):
            rr = _residue(dil, r)
            for j in range(nj):
                q2 = qf.at[j][rr, :]
                k2 = jnp.concatenate([kf.at[j][rr, :], kf.at[nj + j][rr, :]], axis=0).astype(BF16)
                v2 = jnp.concatenate([vf.at[j][rr, :], vf.at[nj + j][rr, :]], axis=0).astype(BF16)
                qs = _stack_heads(q2, first).astype(BF16)
                s = lax.dot_general(qs, k2, NT, preferred_element_type=F32) * (HEAD_DIM ** -0.5)
                s = jnp.where(valid, s + b_ref[2 * j:2 * j + 2].reshape(2 * BAND, 2 * BAND), NEG_BIG)
                m = jnp.max(s, axis=1, keepdims=True)
                p = jnp.exp(s - m)
                l = jnp.sum(p, axis=1, keepdims=True)
                pv = jnp.dot(p.astype(BF16), v2, preferred_element_type=F32) / l
                lse = jnp.broadcast_to(m + jnp.log(l), (2 * BAND, 128))
                of.at[j][rr, :] = jnp.where(first, pv[:BAND], pv[BAND:])
                lf.at[j][rr, :] = jnp.where(first, lse[:BAND], lse[BAND:])
        for j in range(nj):
            sl = slice(128 * j, 128 * (j + 1))
            o_ref[:, sl] = of[j]
            l_ref[:, sl] = lf[j]

    base = g * ncol
    cur = lambda c, b, ch: (b * nch + ch, base + c)
    prev = lambda c, b, ch: (b * nch + jnp.maximum(ch - 1, 0), base + c)
    blk = lambda im: pl.BlockSpec((rows, wc), im)
    out_blk = pl.BlockSpec((rows, wc), lambda c, b, ch: (b * nch + ch, c))
    return _call(
        body, "attn_fwd_d%d" % dil, (ncol, bsz, nch),
        [blk(cur), blk(cur), blk(prev), blk(cur), blk(prev),
         pl.BlockSpec((hpc, BAND, 2 * BAND), lambda c, b, ch: (c, 0, 0))],
        [out_blk, out_blk],
        [_sds((n, w), F32)] * 2,
        [slab(1), slab(2), slab(2), slab(1), slab(1)], comm=comm,
    )(q, k, k, v, v, bias)


def _attn_combine(outs, lses):
    n, w = outs[0].shape

    def body(o0, o1, o2, l0, l1, l2, o_ref, l_ref):
        la, lb, lc = l0[...], l1[...], l2[...]
        m = jnp.maximum(jnp.maximum(la, lb), lc)
        wa, wb, wc = jnp.exp(la - m), jnp.exp(lb - m), jnp.exp(lc - m)
        tot = wa + wb + wc
        o_ref[...] = (wa * o0[...] + wb * o1[...] + wc * o2[...]) / tot
        l_ref[...] = m + jnp.log(tot)

    row = pl.BlockSpec((TM, w), lambda i: (i, 0))
    return _call(body, "attn_combine", (n // TM,), [row] * 6, [row, row],
                 [_sds((n, w), F32)] * 2)(*outs, *lses)


def _attn_bwd(q, k, v, do, o, lse, bias, g, dil, bsz, seq, comm=None, into=()):
    n = q.shape[0]
    rows = BAND * dil
    nch = seq // rows
    w = N_HEADS * HEAD_DIM
    wc = ATTN_COLS[dil]
    ncol = w // wc
    hpc = wc // HEAD_DIM
    nj = wc // 128
    slab = lambda k: pltpu.VMEM((k * nj, rows, 128), F32)
    scale = HEAD_DIM ** -0.5

    def body(q_ref, kc_ref, kp_ref, vc_ref, vp_ref, do_ref, o_ref, l_ref, b_ref,
             dq_ref, dk_ref, dv_ref, db_ref, qf, kf, vf, dof, of, lf, dqf, dkf, dvf):
        b = pl.program_id(1)
        ch = pl.program_id(2)
        cur = ch % 2
        prv = 1 - cur

        @pl.when(jnp.logical_and(b == 0, ch == 0))
        def _():
            db_ref[...] = jnp.zeros_like(db_ref)

        @pl.when(ch == 0)
        def _():
            for j in range(nj):
                dkf[nj + j] = jnp.zeros((rows, 128), F32)
                dvf[nj + j] = jnp.zeros((rows, 128), F32)

        @pl.when(ch < nch)
        def _():
            valid = _valid_mask(ch)
            valid = jnp.concatenate([valid, valid], axis=0)
            first = lax.broadcasted_iota(jnp.int32, (BAND, 128), 1) < HEAD_DIM
            for j in range(nj):
                sl = slice(128 * j, 128 * (j + 1))
                qf[j] = q_ref[:, sl].astype(F32)
                kf[j] = kp_ref[:, sl].astype(F32)
                kf[nj + j] = kc_ref[:, sl].astype(F32)
                vf[j] = vp_ref[:, sl].astype(F32)
                vf[nj + j] = vc_ref[:, sl].astype(F32)
                dof[j] = do_ref[:, sl]
                of[j] = o_ref[:, sl]
                lf[j] = l_ref[:, sl]
            for r in range(dil):
                rr = _residue(dil, r)
                for j in range(nj):
                    k2 = jnp.concatenate([kf.at[j][rr, :], kf.at[nj + j][rr, :]], axis=0).astype(BF16)
                    v2 = jnp.concatenate([vf.at[j][rr, :], vf.at[nj + j][rr, :]], axis=0).astype(BF16)
                    do2 = dof.at[j][rr, :]
                    lse2 = lf.at[j][rr, :]
                    qs = _stack_heads(qf.at[j][rr, :], first).astype(BF16)
                    dos = _stack_heads(do2, first)
                    delta = jnp.sum(dos * jnp.concatenate([of.at[j][rr, :]] * 2, axis=0), axis=1, keepdims=True)
                    dos = dos.astype(BF16)
                    lse_col = jnp.concatenate([lse2[:, 0:1], lse2[:, HEAD_DIM:HEAD_DIM + 1]], axis=0)
                    s = lax.dot_general(qs, k2, NT, preferred_element_type=F32) * scale
                    s = jnp.where(valid, s + b_ref[2 * j:2 * j + 2].reshape(2 * BAND, 2 * BAND), NEG_BIG)
                    p = jnp.exp(s - lse_col)
                    dp = lax.dot_general(dos, v2, NT, preferred_element_type=F32)
                    ds = p * (dp - delta)
                    db_ref[2 * j:2 * j + 2] += ds.reshape(2, BAND, 2 * BAND)
                    dsb = ds.astype(BF16)
                    dq2 = jnp.dot(dsb, k2, preferred_element_type=F32) * scale
                    dk2 = lax.dot_general(dsb, qs, TN, preferred_element_type=F32) * scale
                    dv2 = lax.dot_general(p.astype(BF16), dos, TN, preferred_element_type=F32)
                    dqf.at[j][rr, :] = jnp.where(first, dq2[:BAND], dq2[BAND:])
                    dkf.at[prv * nj + j][rr, :] += dk2[:BAND, :]
                    dvf.at[prv * nj + j][rr, :] += dv2[:BAND, :]
                    dkf.at[cur * nj + j][rr, :] = dk2[BAND:, :]
                    dvf.at[cur * nj + j][rr, :] = dv2[BAND:, :]
            for j in range(nj):
                dq_ref[:, 128 * j:128 * (j + 1)] = dqf[j].astype(BF16)

        for j in range(nj):
            sl = slice(128 * j, 128 * (j + 1))
            dk_ref[:, sl] = dkf[prv * nj + j].astype(BF16)
            dv_ref[:, sl] = dvf[prv * nj + j].astype(BF16)

    last = nch - 1
    base = g * ncol
    cur3 = lambda c, b, ch: (b * nch + jnp.minimum(ch, last), base + c)
    prev3 = lambda c, b, ch: (b * nch + jnp.maximum(jnp.minimum(ch, last) - 1, 0), base + c)
    cur1 = lambda c, b, ch: (b * nch + jnp.minimum(ch, last), c)
    lag3 = lambda c, b, ch: (b * nch + jnp.maximum(ch - 1, 0), base + c)
    blk = lambda im: pl.BlockSpec((rows, wc), im)
    bias_blk = pl.BlockSpec((hpc, BAND, 2 * BAND), lambda c, b, ch: (c, 0, 0))
    out = _sds(q.shape, BF16)
    return _call(
        body, "attn_bwd_d%d" % dil, (ncol, bsz, nch + 1),
        [blk(cur3), blk(cur3), blk(prev3), blk(cur3), blk(prev3), blk(cur1), blk(cur1), blk(cur1), bias_blk],
        [blk(cur3), blk(lag3), blk(lag3), bias_blk],
        [out, out, out, _sds((N_HEADS, BAND, 2 * BAND), F32)],
        [slab(1), slab(2), slab(2), slab(1), slab(1), slab(1), slab(1), slab(2), slab(2)], comm=comm, fill=into,
    )(q, k, k, v, v, do, o, lse, bias)


def _adamw(name, parts, w, m, v):
    nl, r, c = w.shape
    nparts = parts[0].shape[0]
    tr = r
    for cand in (256, 352, 128):
        if r % cand == 0 and r > cand:
            tr = cand
            break
    c1 = 1.0 - ADAM_B1 ** ADAM_STEP
    c2 = 1.0 - ADAM_B2 ** ADAM_STEP

    def body(*refs):
        p_refs = refs[:nl]
        w_ref, m_ref, v_ref, g_ref, d_ref, nm_ref, nv_ref = refs[nl:]
        layer = pl.program_id(0)
        for l in range(nl):
            @pl.when(layer == l)
            def _():
                g = p_refs[l][0].astype(F32)
                for dev in range(1, nparts):
                    g = g + p_refs[l][dev].astype(F32)
                nm = ADAM_B1 * m_ref[...] + (1.0 - ADAM_B1) * g
                nv = ADAM_B2 * v_ref[...] + (1.0 - ADAM_B2) * (g * g)
                m_hat = nm / c1
                v_hat = nv / c2
                g_ref[...] = g
                d_ref[...] = -ADAM_LR * (m_hat / (jnp.sqrt(v_hat) + ADAM_EPS) + ADAM_WD * w_ref[...])
                nm_ref[...] = nm
                nv_ref[...] = nv

    def part_spec(l):
        return pl.BlockSpec((nparts, tr, c), lambda layer, i: (0, jnp.where(layer == l, i, 0), 0))

    row = pl.BlockSpec((None, tr, c), lambda layer, i: (layer, i, 0))
    return _call(body, name, (nl, r // tr), [part_spec(l) for l in range(nl)] + [row, row, row],
                 [row] * 4, [_sds((nl, r, c), F32)] * 4)(*parts, w, m, v)


def _exchange(name, srcs, out_shapes, items):
    n_in = len(srcs)
    n_out = len(out_shapes)
    n_items = len(items)

    def body(*refs):
        ins = refs[:n_in]
        outs = refs[n_in:n_in + n_out]
        send_sems, recv_sems, local_sems = refs[n_in + n_out:]
        x, y, c = lax.axis_index("x"), lax.axis_index("y"), lax.axis_index("c")
        me = 4 * x + 2 * y + c

        def pick(ref, sel, peer):
            idx = tuple(peer if s == "peer" else me if s == "me" else s for s in sel)
            return ref.at[idx] if idx else ref

        copies = []
        for it, (si, ssel, oi, osel) in enumerate(items):
            local = pltpu.make_async_copy(pick(ins[si], ssel, me), pick(outs[oi], osel, me), local_sems.at[it])
            local.start()
            copies.append(local)
            for j in range(1, N_DEV):
                px = 1 - x if j & 4 else x
                py = 1 - y if j & 2 else y
                pc = 1 - c if j & 1 else c
                peer = 4 * px + 2 * py + pc
                sem = it * (N_DEV - 1) + j - 1
                cp = pltpu.make_async_remote_copy(
                    src_ref=pick(ins[si], ssel, peer), dst_ref=pick(outs[oi], osel, peer),
                    send_sem=send_sems.at[sem], recv_sem=recv_sems.at[sem],
                    device_id=(px, py, pc), device_id_type=pl.DeviceIdType.MESH)
                cp.start()
                copies.append(cp)
        for cp in copies:
            cp.wait()

    any_spec = pl.BlockSpec(memory_space=pl.ANY)
    return pl.pallas_call(
        body, name=name, in_specs=[any_spec] * n_in, out_specs=[any_spec] * n_out, out_shape=list(out_shapes),
        scratch_shapes=[pltpu.SemaphoreType.DMA((n_items * (N_DEV - 1),)),
                        pltpu.SemaphoreType.DMA((n_items * (N_DEV - 1),)),
                        pltpu.SemaphoreType.DMA((n_items,))],
        compiler_params=pltpu.CompilerParams(has_side_effects=True),
    )(*srcs)


def _sel(ref, sel, slot=None):
    idx = tuple(slot if s == "slot" else s for s in sel)
    return ref.at[idx] if idx else ref


def _gather_plan(srcs, out_shapes, items):
    per = N_DEV - 1
    n_items = len(items)

    def phases(ins, outs, send_sems, recv_sems, local_sems, total):
        x, y, c = lax.axis_index("x"), lax.axis_index("y"), lax.axis_index("c")
        sibling = (x, y, 1 - c)
        chips = [(1 - x, y), (x, 1 - y), (1 - x, 1 - y)]
        slot_of = lambda px, py, pc: 4 * px + 2 * py + pc

        def copy(it, k, block, to, src=None):
            _, _, oi, osel = items[it]
            dst = _sel(outs[oi], osel, slot_of(*block))
            return pltpu.make_async_remote_copy(
                src_ref=dst if src is None else src, dst_ref=dst,
                send_sem=send_sems.at[it * per + k], recv_sem=recv_sems.at[it * per + k],
                device_id=to, device_id_type=pl.DeviceIdType.MESH)

        def own_copies(it):
            si, ssel, oi, osel = items[it]
            src = _sel(ins[si], ssel)
            mine = pltpu.make_async_copy(src, _sel(outs[oi], osel, slot_of(x, y, c)), local_sems.at[it])
            first = [copy(it, 0, (x, y, c), sibling, src=src)]
            first += [copy(it, 1 + j, (x, y, c), (*chip, c), src=src) for j, chip in enumerate(chips)]
            return mine, first

        def start():
            for it in range(n_items):
                mine, first = own_copies(it)
                mine.start()
                for cp in first:
                    cp.start()

        def forward(it):
            def go():
                for j, chip in enumerate(chips):
                    copy(it, 1 + j, (*chip, c), (x, y, c)).wait_recv()
                    copy(it, 4 + j, (*chip, c), sibling).start()
            return go

        def finish():
            for it in range(n_items):
                copy(it, 0, sibling, (x, y, c)).wait_recv()
                for j, chip in enumerate(chips):
                    copy(it, 4 + j, (*chip, 1 - c), (x, y, c)).wait_recv()
            for it in range(n_items):
                mine, first = own_copies(it)
                for cp in first:
                    cp.wait_send()
                for j, chip in enumerate(chips):
                    copy(it, 4 + j, (*chip, c), sibling).wait_send()
                mine.wait()

        lo = max(total // 3, 1)
        acts = [(0, start)]
        for it in range(n_items):
            acts.append((min(lo + it * (total - 1 - lo) // n_items, total - 2), forward(it)))
        acts.append((total - 1, finish))
        return acts

    return _Plan(srcs, out_shapes, n_items * per, n_items, phases)


def _scatter_plan(srcs, out_shapes, items):
    per = N_DEV - 1
    n_items = len(items)

    def phases(ins, outs, send_sems, recv_sems, local_sems, total):
        x, y, c = lax.axis_index("x"), lax.axis_index("y"), lax.axis_index("c")
        me = 4 * x + 2 * y + c

        def copies():
            out = []
            for it, (si, oi, osel) in enumerate(items):
                dst = _sel(outs[oi], osel, me)
                out.append(pltpu.make_async_copy(ins[si].at[me], dst, local_sems.at[it]))
                for j in range(1, N_DEV):
                    px = 1 - x if j & 4 else x
                    py = 1 - y if j & 2 else y
                    pc = 1 - c if j & 1 else c
                    out.append(pltpu.make_async_remote_copy(
                        src_ref=ins[si].at[4 * px + 2 * py + pc], dst_ref=dst,
                        send_sem=send_sems.at[it * per + j - 1], recv_sem=recv_sems.at[it * per + j - 1],
                        device_id=(px, py, pc), device_id_type=pl.DeviceIdType.MESH))
            return out

        def start():
            for cp in copies():
                cp.start()

        def finish():
            for cp in copies():
                cp.wait()

        return [(0, start), (total - 1, finish)]

    return _Plan(srcs, out_shapes, n_items * per, n_items, phases)


def _sum_slots(parts):
    _, r, c = parts.shape

    def body(p_ref, o_ref):
        acc = p_ref[0]
        for dev in range(1, N_DEV):
            acc = acc + p_ref[dev]
        o_ref[...] = acc

    return _call(body, "sum_slots", (1,), [pl.BlockSpec((N_DEV, r, c), lambda i: (0, 0, 0))],
                 pl.BlockSpec((r, c), lambda i: (0, 0)), _sds((r, c), F32))(parts)


def _pack(arrays):
    rows = []
    for a in arrays:
        flat = a.reshape(-1).astype(F32)
        pad = (-flat.shape[0]) % 1024
        if pad:
            flat = jnp.concatenate([flat, jnp.zeros((pad,), F32)])
        rows.append(flat.reshape(-1, 128))
    return jnp.concatenate(rows, axis=0)


def _unpack(buf, shapes):
    out = []
    r0 = 0
    for shp in shapes:
        size = int(np.prod(shp))
        nrows = -(-size // 1024) * 8
        out.append(buf[r0:r0 + nrows].reshape(-1)[:size].reshape(shp))
        r0 += nrows
    return out


def _block_diag_chunks(val):
    eye = jnp.eye(8, dtype=val.dtype)
    return jnp.einsum("cjhp,jk->cjhkp", val, eye).reshape(SSM_CHUNKS, CHUNK_CH, CHUNK_ST)


def _diag_blocks(dense):
    d5 = dense.reshape(SSM_CHUNKS, 8, SSM_GROUP, 8, SSM_STATE)
    return jnp.stack([d5[:, j, :, j, :] for j in range(8)], axis=1)


def _carried(res, comm):
    return res if comm is not None else (res, None)


def _ffn_fwd(tag, hin, hin_b, w_up, conv_w, conv_b, w_down4, gain, bias, seq, comm_up=None, comm_down=None):
    hc, got_up = _carried(_mm_nn("ffn_up" + tag, hin_b, w_up, FFN_HIDDEN, blocked_out=True, comm=comm_up), comm_up)
    nb, n, c = hc.shape
    hc = hc.reshape(2, nb // 2, n, c)
    act = _conv_gate_fwd(hc, conv_w, conv_b, seq)
    (r, hout, hout_b), got_down = _carried(
        _mm_nn_red("ffn_down" + tag, act, w_down4, F32, comm=comm_down, residual=hin, norm=(gain, bias)), comm_down)
    return hout, hout_b, (hin_b, hc, act, r), (got_up, got_down)


def _ffn_bwd(tag, saved, dh, w_up, conv_w, conv_b, w_down4, gain, seq, comm_conv=None, comm_dw=None,
             comm_dx=None):
    hin_b, hc, act, r = saved
    dr, dr_b, dgain, dbias = _ln_bwd("ffn_norm_bwd" + tag, r, dh, gain)
    d_wdown = _mm_tn("ffn_down_dw" + tag, act, dr_b, act.shape[0], BF16, a_blocked=True, g_mode="shared")
    dact = _mm_nt_out("ffn_down_dx" + tag, dr_b, w_down4, FFN_HIDDEN)
    (dhc, dconv), got_conv = _carried(_conv_gate_bwd(hc, dact, conv_w, conv_b, seq, comm=comm_conv), comm_conv)
    dhc8 = dhc.reshape(2 * dhc.shape[1], dhc.shape[2], dhc.shape[3])
    d_wup, got_dw = _carried(_mm_tn("ffn_up_dw" + tag, hin_b, dhc8, dhc8.shape[0], BF16, g_mode="blocked",
                                    comm=comm_dw), comm_dw)
    dhin, got_dx = _carried(_mm_nt_red("ffn_up_dx" + tag, dhc8, w_up, F32, g_blocked=True, add=dr,
                                       add_scale=DN_ALPHA, comm=comm_dx), comm_dx)
    return dhin, d_wup, d_wdown, dconv, dgain, dbias, (got_conv, got_dw, got_dx)


def _local_step(x, target, p, wfull, bsz, seq, ex=None):
    n = bsz * seq
    xf = x.reshape(n, D_MODEL)
    tf = target.reshape(n, D_MODEL)

    lr = p["s5_lam_re"].reshape(1, N_STATE)
    li = p["s5_lam_im"].reshape(1, N_STATE)
    ldt = p["s5_log_dt"].reshape(1, D_MODEL // SSM_GROUP)
    br_t = p["s5_b_re"].reshape(N_STATE, SSM_GROUP).T
    bi_t = p["s5_b_im"].reshape(N_STATE, SSM_GROUP).T
    ab_r, ab_i, bb_r, bb_i, tab = _s5_prep(lr, li, ldt, br_t, bi_t)

    def bb_chunks(bb):
        return _block_diag_chunks(bb.reshape(SSM_GROUP, SSM_CHUNKS, 8, SSM_STATE).transpose(1, 2, 0, 3))

    def c_chunks(cc):
        return _block_diag_chunks(cc.reshape(SSM_CHUNKS, 8, SSM_GROUP, SSM_STATE))

    bbr_c, bbi_c = bb_chunks(bb_r).astype(BF16), bb_chunks(bb_i).astype(BF16)
    cr_c, ci_c = c_chunks(p["s5_c_re"]).astype(BF16), c_chunks(p["s5_c_im"]).astype(BF16)
    dvec = p["s5_d"].reshape(1, D_MODEL)

    if ex is None:
        y, h_r, h_i = _s5_scan_fwd(xf, bbr_c, bbi_c, cr_c, ci_c, dvec, tab, seq)
    else:
        (y, h_r, h_i), gathered = _s5_scan_fwd(xf, bbr_c, bbi_c, cr_c, ci_c, dvec, tab, seq,
                                               comm=ex.gather_first())
        p, wfull = ex.take_first(gathered, p)
    ln_g = p["ln_gain"].reshape(4, 1, D_MODEL)
    ln_b = p["ln_bias"].reshape(4, 1, D_MODEL)
    y2, z, gg = _glu_fwd(y, wfull["s5_w_glu"], p["s5_b_glu"].reshape(1, D_MODEL))
    r1, h1, h1b = _mm_nn("s5_out", gg, wfull["s5_w_out"][None], F32, residual=xf, norm=(ln_g[0], ln_b[0]))
    h2, h2b, ffn0_saved, (got_up, got_down) = _ffn_fwd(
        "0", h1, h1b, wfull["ffn_w_up"][0], p["conv_w"][0], p["conv_b"][0], wfull["ffn_w_down"][0],
        ln_g[1], ln_b[1], seq, comm_up=ex.gather_layer1("up", 0) if ex else None,
        comm_down=ex.gather_layer1("down", 0) if ex else None)
    halves = {"up": [got_up[0]], "down": [got_down[0]]} if ex else None

    kmat = _mm_nn("attn_k", h2b, wfull["attn_w_kv"], BF16, blocks=(0, 4))
    vmat = _mm_nn("attn_v", h2b, wfull["attn_w_kv"], BF16, blocks=(4, 4))
    qmat = _mm_nn("attn_q", h2b, wfull["attn_w_q"], BF16)
    biases, outs, lses = [], [], []
    for g, dil in enumerate(DILATIONS):
        rel_t = p["rel_bias"][:, g * N_HEADS:(g + 1) * N_HEADS].T
        biases.append(_bias_expand(rel_t, dil).reshape(N_HEADS, BAND, 2 * BAND))
        which = {4: "down", 16: "up"}.get(dil) if ex else None
        plan = ex.gather_layer1(which, 1) if which else None
        (o_g, l_g), got = _carried(_attn_fwd(qmat, kmat, vmat, biases[g], g, dil, bsz, seq, comm=plan), plan)
        if plan is not None:
            halves[which].append(got[0])
        outs.append(o_g)
        lses.append(l_g)
    if ex is not None:
        wfull = ex.take_layer1(halves, wfull)
    o, lse = _attn_combine(outs, lses)
    r3, h3, h3b = _mm_nn("attn_out", o, wfull["attn_w_out"][None], F32, residual=h2, norm=(ln_g[2], ln_b[2]))
    h4, _, ffn1_saved, _ = _ffn_fwd("1", h3, h3b, wfull["ffn_w_up"][1], p["conv_w"][1], p["conv_b"][1],
                                    wfull["ffn_w_down"][1], ln_g[3], ln_b[3], seq)
    loss_tile, dh4 = _loss_head(h4, tf)

    grads = {}
    dh3, grads["ffn_w_up1"], grads["ffn_w_down1"], dconv1, dg3, db3, _ = _ffn_bwd(
        "1", ffn1_saved, dh4, wfull["ffn_w_up"][1], p["conv_w"][1], p["conv_b"][1], wfull["ffn_w_down"][1],
        ln_g[3], seq)
    dr3, dr3b, dg2, db2 = _ln_bwd("attn_norm_bwd", r3, dh3, ln_g[2])
    grads["attn_w_out"] = _mm_tn("attn_out_dw", o, dr3b, 1, BF16, g_mode="shared")
    do = _mm_nt_red("attn_out_dx", dr3b, wfull["attn_w_out"][None], F32, g_blocked=False)
    dqkv, drel = (), []
    for g, dil in enumerate(DILATIONS):
        plan = ex.scatter(grads, ("ffn_w_up1", "ffn_w_down1")) if ex and dil == DILATIONS[-1] else None
        res, got = _carried(_attn_bwd(qmat, kmat, vmat, do, o, lse, biases[g], g, dil, bsz, seq, comm=plan,
                                      into=dqkv), plan)
        if plan is not None:
            ex.keep(plan, got)
        dqkv, db_g = tuple(res[:3]), res[3]
        drel.append(_bias_reduce(db_g.reshape(N_HEADS, BAND * 2 * BAND), dil).T)
    dq, dk, dv = dqkv
    grads["rel_bias"] = jnp.concatenate(drel, axis=1)
    grads["attn_w_q"] = _mm_tn("attn_q_dw", h2b, dq, N_DEV, BF16)
    grads["attn_w_kv"] = jnp.concatenate([_mm_tn("attn_k_dw", h2b, dk, 4, BF16),
                                          _mm_tn("attn_v_dw", h2b, dv, 4, BF16)], axis=0)
    dh2 = _mm_nt_red("attn_q_dx", dq, wfull["attn_w_q"], F32, g_blocked=False, add=dr3, add_scale=DN_ALPHA)
    dh2 = _mm_nt_red("attn_k_dx", dk, wfull["attn_w_kv"], F32, g_blocked=False, add=dh2, blocks=(0, 4))
    dh2 = _mm_nt_red("attn_v_dx", dv, wfull["attn_w_kv"], F32, g_blocked=False, add=dh2, blocks=(4, 4))

    plans = [ex.scatter(grads, names) if ex else None
             for names in (("attn_w_kv",), ("attn_w_out",), ("attn_w_q",))]
    dh1, grads["ffn_w_up0"], grads["ffn_w_down0"], dconv0, dg1, db1, got = _ffn_bwd(
        "0", ffn0_saved, dh2, wfull["ffn_w_up"][0], p["conv_w"][0], p["conv_b"][0], wfull["ffn_w_down"][0],
        ln_g[1], seq, comm_conv=plans[0], comm_dw=plans[1], comm_dx=plans[2])
    if ex is not None:
        for plan, outs in zip(plans, got):
            ex.keep(plan, outs)
    dr1, dr1b, dg0, db0 = _ln_bwd("s5_norm_bwd", r1, dh1, ln_g[0])
    grads["s5_w_out"] = _mm_tn("s5_out_dw", gg, dr1b, 1, BF16, g_mode="shared")
    dgg = _mm_nt_red("s5_out_dx", dr1b, wfull["s5_w_out"][None], F32, g_blocked=False)
    dz, d_bglu, dy2_direct = _glu_bwd_gate(y, z, dgg)
    grads["s5_w_glu"] = _mm_tn("s5_glu_dw", y2, dz, 1, BF16, g_mode="shared")
    dy2 = _mm_nt_red("s5_glu_dx", dz, wfull["s5_w_glu"][None], F32, g_blocked=False, add=dy2_direct)
    dy = _gelu_bwd(y, dy2)
    plan_last = ex.scatter(grads, ("ffn_w_up0", "ffn_w_down0", "s5_w_out", "s5_w_glu")) if ex else None
    res = _s5_scan_bwd(xf, dy, dr1, h_r, h_i, bbr_c, bbi_c, cr_c, ci_c, dvec, tab, seq, comm=plan_last)
    if ex is not None:
        res, got = res
        ex.keep(plan_last, got)
    dx, dcr, dci, dbr, dbi, dar, dai, dd = res

    def bb_from_chunks(dense):
        return _diag_blocks(dense).transpose(2, 0, 1, 3).reshape(SSM_GROUP, N_STATE)

    d_lr, d_li, d_ldt, d_br, d_bi = _s5_prep_bwd(lr, li, ldt, br_t, bi_t, dar, dai,
                                                 bb_from_chunks(dbr), bb_from_chunks(dbi))
    grads["s5_lam_re"] = d_lr.reshape(p["s5_lam_re"].shape)
    grads["s5_lam_im"] = d_li.reshape(p["s5_lam_im"].shape)
    grads["s5_log_dt"] = d_ldt.reshape(p["s5_log_dt"].shape)
    grads["s5_b_re"] = d_br.T.reshape(p["s5_b_re"].shape)
    grads["s5_b_im"] = d_bi.T.reshape(p["s5_b_im"].shape)
    grads["s5_c_re"] = _diag_blocks(dcr).reshape(p["s5_c_re"].shape)
    grads["s5_c_im"] = -_diag_blocks(dci).reshape(p["s5_c_im"].shape)
    grads["s5_d"] = dd.reshape(p["s5_d"].shape)
    grads["s5_b_glu"] = d_bglu.reshape(1, D_MODEL)
    dconv = jnp.stack([dconv0, dconv1]).reshape(2, N_DEV, 8, -1)
    grads["ffn_conv_w"] = dconv[:, :, 0:3, :].transpose(0, 2, 1, 3)
    grads["ffn_conv_b"] = dconv[:, :, 3, :]
    grads["ln_gain"] = jnp.stack([dg0, dg1, dg2, dg3]).reshape(2, 2, D_MODEL)
    grads["ln_bias"] = jnp.stack([db0, db1, db2, db3]).reshape(2, 2, D_MODEL)
    return loss_tile[0, 0], dx.reshape(x.shape), grads


SMALL_REPLICATED = ("s5_lam_re", "s5_lam_im", "s5_log_dt", "s5_b_re", "s5_b_im", "s5_c_re", "s5_c_im", "s5_d",
                    "rel_bias", "ffn_conv_b")
SMALL_SHARDED = ("s5_b_glu", "ffn_conv_w", "ln_gain", "ln_bias")
BIG = ("s5_w_glu", "s5_w_out", "attn_w_kv", "attn_w_q", "attn_w_out", "ffn_w_up", "ffn_w_down")
WEIGHTS = ("s5_lam_re", "s5_lam_im", "s5_log_dt", "s5_b_re", "s5_b_im", "s5_c_re", "s5_c_im", "s5_d", "s5_w_glu",
           "s5_b_glu", "s5_w_out", "attn_w_kv", "attn_w_q", "attn_w_out", "rel_bias", "ffn_w_up", "ffn_conv_w",
           "ffn_conv_b", "ffn_w_down", "ln_gain", "ln_bias")


class _Exchanges:
    def __init__(self, w):
        self.w = w
        self.parts = {}
        self.up = w["ffn_w_up"].astype(BF16)
        self.down = w["ffn_w_down"].astype(BF16)

    def gather_first(self):
        w = self.w
        srcs = [w["s5_w_glu"][0].astype(BF16), w["s5_w_out"][0].astype(BF16),
                _pack([w[k] for k in SMALL_SHARDED]), self.up, self.down, w["attn_w_kv"].astype(BF16),
                w["attn_w_q"][0].astype(BF16), w["attn_w_out"][0].astype(BF16)]
        outs = [_sds((N_DEV,) + (s.shape[1:] if i in (3, 4) else s.shape), s.dtype) for i, s in enumerate(srcs)]
        items = [(i, (0,) if i in (3, 4) else (), i, ("slot",)) for i in range(len(srcs))]
        return _gather_plan(srcs, outs, items)

    def take_first(self, g, p):
        d = D_MODEL
        w = self.w
        wfull = {
            "s5_w_glu": g[0].reshape(d, d),
            "s5_w_out": g[1].reshape(d, d),
            "ffn_w_up": [g[3], None],
            "ffn_w_down": [g[4].reshape(4, -1, d), None],
            "attn_w_kv": g[5],
            "attn_w_q": g[6],
            "attn_w_out": g[7].reshape(d, d),
        }
        smalls = [_unpack(g[2][dev], [w[k].shape for k in SMALL_SHARDED]) for dev in range(N_DEV)]
        c_ff = w["ffn_conv_w"].shape[2]
        conv_w = jnp.stack([s[1] for s in smalls], axis=2)
        p = dict(p)
        p.update(s5_b_glu=jnp.concatenate([s[0] for s in smalls], axis=1),
                 ln_gain=jnp.concatenate([s[2] for s in smalls], axis=2),
                 ln_bias=jnp.concatenate([s[3] for s in smalls], axis=2),
                 conv_w=conv_w.transpose(0, 2, 1, 3).reshape(2, 2, 4, 3, c_ff),
                 conv_b=w["ffn_conv_b"].reshape(2, 2, 4, 1, c_ff))
        return p, wfull

    def gather_layer1(self, which, half):
        src = self.up if which == "up" else self.down
        rows = src.shape[1] // 2
        return _gather_plan([src], [_sds((N_DEV, rows) + src.shape[2:], BF16)],
                            [(0, (1, pl.ds(half * rows, rows)), 0, ("slot",))])

    def take_layer1(self, halves, wfull):
        up = jnp.concatenate(halves["up"], axis=1)
        down = jnp.concatenate(halves["down"], axis=1)
        wfull = dict(wfull)
        wfull["ffn_w_up"] = [wfull["ffn_w_up"][0], up]
        wfull["ffn_w_down"] = [wfull["ffn_w_down"][0], down.reshape(4, -1, D_MODEL)]
        return wfull

    def scatter(self, grads, names):
        srcs = [grads[k].reshape(N_DEV, -1, grads[k].shape[-1]) for k in names]
        plan = _scatter_plan(srcs, [_sds(s.shape, BF16) for s in srcs],
                             [(i, i, ("slot",)) for i in range(len(names))])
        plan.names = names
        return plan

    def keep(self, plan, outs):
        for k, o in zip(plan.names, outs):
            self.parts[k] = o


def kernel(x, s5_lam_re, s5_lam_im, s5_log_dt, s5_b_re, s5_b_im, s5_c_re, s5_c_im, s5_d, s5_w_glu, s5_b_glu, s5_w_out, attn_w_kv, attn_w_q, attn_w_out, rel_bias, ffn_w_up, ffn_conv_w, ffn_conv_b, ffn_w_down, ln_gain, ln_bias, loss_target, m_s5_lam_re, m_s5_lam_im, m_s5_log_dt, m_s5_b_re, m_s5_b_im, m_s5_c_re, m_s5_c_im, m_s5_d, m_s5_w_glu, m_s5_b_glu, m_s5_w_out, m_attn_w_kv, m_attn_w_q, m_attn_w_out, m_rel_bias, m_ffn_w_up, m_ffn_conv_w, m_ffn_conv_b, m_ffn_w_down, m_ln_gain, m_ln_bias, v_s5_lam_re, v_s5_lam_im, v_s5_log_dt, v_s5_b_re, v_s5_b_im, v_s5_c_re, v_s5_c_im, v_s5_d, v_s5_w_glu, v_s5_b_glu, v_s5_w_out, v_attn_w_kv, v_attn_w_q, v_attn_w_out, v_rel_bias, v_ffn_w_up, v_ffn_conv_w, v_ffn_conv_b, v_ffn_w_down, v_ln_gain, v_ln_bias):
    w = dict(s5_lam_re=s5_lam_re, s5_lam_im=s5_lam_im, s5_log_dt=s5_log_dt, s5_b_re=s5_b_re, s5_b_im=s5_b_im,
             s5_c_re=s5_c_re, s5_c_im=s5_c_im, s5_d=s5_d, s5_w_glu=s5_w_glu, s5_b_glu=s5_b_glu, s5_w_out=s5_w_out,
             attn_w_kv=attn_w_kv, attn_w_q=attn_w_q, attn_w_out=attn_w_out, rel_bias=rel_bias, ffn_w_up=ffn_w_up,
             ffn_conv_w=ffn_conv_w, ffn_conv_b=ffn_conv_b, ffn_w_down=ffn_w_down, ln_gain=ln_gain, ln_bias=ln_bias)
    mom = dict(s5_lam_re=m_s5_lam_re, s5_lam_im=m_s5_lam_im, s5_log_dt=m_s5_log_dt, s5_b_re=m_s5_b_re,
               s5_b_im=m_s5_b_im, s5_c_re=m_s5_c_re, s5_c_im=m_s5_c_im, s5_d=m_s5_d, s5_w_glu=m_s5_w_glu,
               s5_b_glu=m_s5_b_glu, s5_w_out=m_s5_w_out, attn_w_kv=m_attn_w_kv, attn_w_q=m_attn_w_q,
               attn_w_out=m_attn_w_out, rel_bias=m_rel_bias, ffn_w_up=m_ffn_w_up, ffn_conv_w=m_ffn_conv_w,
               ffn_conv_b=m_ffn_conv_b, ffn_w_down=m_ffn_w_down, ln_gain=m_ln_gain, ln_bias=m_ln_bias)
    var = dict(s5_lam_re=v_s5_lam_re, s5_lam_im=v_s5_lam_im, s5_log_dt=v_s5_log_dt, s5_b_re=v_s5_b_re,
               s5_b_im=v_s5_b_im, s5_c_re=v_s5_c_re, s5_c_im=v_s5_c_im, s5_d=v_s5_d, s5_w_glu=v_s5_w_glu,
               s5_b_glu=v_s5_b_glu, s5_w_out=v_s5_w_out, attn_w_kv=v_attn_w_kv, attn_w_q=v_attn_w_q,
               attn_w_out=v_attn_w_out, rel_bias=v_rel_bias, ffn_w_up=v_ffn_w_up, ffn_conv_w=v_ffn_conv_w,
               ffn_conv_b=v_ffn_conv_b, ffn_w_down=v_ffn_w_down, ln_gain=v_ln_gain, ln_bias=v_ln_bias)
    bsz, seq, _ = x.shape
    me = 4 * lax.axis_index("x") + 2 * lax.axis_index("y") + lax.axis_index("c")

    ex = _Exchanges(w)
    loss_part, grad_x, g = _local_step(x, loss_target, dict(w), None, bsz, seq, ex=ex)
    loss = lax.psum(loss_part, ("x", "y", "c"))

    d = D_MODEL
    results = {}
    for name in BIG:
        shard = w[name]
        if name in ("ffn_w_up", "ffn_w_down"):
            parts, r3 = [ex.parts[name + "0"], ex.parts[name + "1"]], shard.shape
        else:
            parts, r3 = [ex.parts[name]], (1, -1, shard.shape[-1])
        outs4 = _adamw("adamw_" + name, parts, shard.reshape(r3), mom[name].reshape(r3), var[name].reshape(r3))
        results[name] = tuple(o.reshape(shard.shape) for o in outs4)

    small_names = SMALL_REPLICATED + SMALL_SHARDED
    packed = _pack([g[k] for k in small_names])
    pad = (-packed.shape[0]) % (8 * N_DEV)
    if pad:
        packed = jnp.concatenate([packed, jnp.zeros((pad, 128), F32)], axis=0)
    piece = packed.shape[0] // N_DEV
    slots = _exchange("reduce_small", [packed.reshape(N_DEV, piece, 128)], [_sds((N_DEV, piece, 128), F32)],
                      [(0, ("peer",), 0, ("me",))])[0]
    summed = _exchange("gather_small", [_sum_slots(slots)], [_sds((N_DEV, piece, 128), F32)],
                       [(0, (), 0, ("me",))])[0]
    total = _unpack(summed.reshape(N_DEV * piece, 128), [g[k].shape for k in small_names])
    gsum = dict(zip(small_names, total))
    mine = {k: gsum[k] for k in SMALL_REPLICATED}
    mine["s5_b_glu"] = lax.dynamic_slice_in_dim(gsum["s5_b_glu"], me * (d // N_DEV), d // N_DEV, axis=1)
    mine["ffn_conv_w"] = lax.dynamic_index_in_dim(gsum["ffn_conv_w"], me, axis=2, keepdims=False)
    mine["ln_gain"] = lax.dynamic_slice_in_dim(gsum["ln_gain"], me * (d // N_DEV), d // N_DEV, axis=2)
    mine["ln_bias"] = lax.dynamic_slice_in_dim(gsum["ln_bias"], me * (d // N_DEV), d // N_DEV, axis=2)
    gp = _pack([mine[k] for k in small_names])
    outs4 = _adamw("adamw_small", [gp[None]], _pack([w[k] for k in small_names])[None],
                   _pack([mom[k] for k in small_names])[None], _pack([var[k] for k in small_names])[None])
    outs4 = [o[0] for o in outs4]
    shapes = [w[k].shape for k in small_names]
    unpacked = [_unpack(o, shapes) for o in outs4]
    for i, k in enumerate(small_names):
        results[k] = tuple(unpacked[j][i] for j in range(4))

    out = [loss, grad_x]
    for j in range(4):
        out.extend(results[k][j] for k in WEIGHTS)
    return tuple(out)
```

```python
import functools
import math

import numpy as np
import jax
import jax.numpy as jnp
from jax import lax
from jax.experimental import pallas as pl
from jax.experimental.pallas import tpu as pltpu

F32 = jnp.float32
BF16 = jnp.bfloat16
HIGHEST = lax.Precision.HIGHEST

N_DEV = 8
D_MODEL = 1024
SSM_GROUP = 16
SSM_STATE = 64
SSM_CHUNKS = 8
CHUNK_CH = D_MODEL // SSM_CHUNKS
CHUNK_ST = CHUNK_CH // SSM_GROUP * SSM_STATE
N_STATE = D_MODEL // SSM_GROUP * SSM_STATE
HEAD_DIM = 64
N_HEADS = 16
BAND = 128
DILATIONS = (1, 4, 16)
REL_BUCKETS = 32
REL_MAX_DIST = 2048
NEG_BIG = -1e30
DN_ALPHA = (2.0 * 2) ** 0.25
LN_EPS = 1e-5
ADAM_LR = 0.001
ADAM_B1 = 0.9
ADAM_B2 = 0.999
ADAM_EPS = 1e-08
ADAM_WD = 0.01
ADAM_STEP = 10

TM = 512
TM_MM = 2048
TM_MM_F32 = 1024
T_SCAN = 512
VMEM_LIMIT = 56 * 1024 * 1024
FFN_HIDDEN = BF16

NN = (((1,), (0,)), ((), ()))
NT = (((1,), (1,)), ((), ()))
TN = (((0,), (0,)), ((), ()))


class _Plan:
    def __init__(self, srcs, out_shapes, n_sems, n_local, phases):
        self.srcs, self.out_shapes, self.n_sems, self.n_local, self.phases = srcs, out_shapes, n_sems, n_local, phases


def _call(body, name, grid, in_specs, out_specs, out_shape, scratch=(), comm=None, fill=()):
    params = dict(dimension_semantics=("arbitrary",) * len(grid), vmem_limit_bytes=VMEM_LIMIT)
    if comm is None and not fill:
        return pl.pallas_call(
            body, name=name, grid=grid, in_specs=in_specs, out_specs=out_specs, out_shape=out_shape,
            scratch_shapes=list(scratch), compiler_params=pltpu.CompilerParams(**params))
    single = not isinstance(out_specs, (list, tuple))
    o_specs = [out_specs] if single else list(out_specs)
    o_shape = [out_shape] if single else list(out_shape)
    n_in, n_out, n_scr, n_fill = len(in_specs), len(o_specs), len(scratch), len(fill)
    srcs = list(comm.srcs) if comm else []
    c_shapes = list(comm.out_shapes) if comm else []
    n_cin, n_cout = len(srcs), len(c_shapes)
    total = int(np.prod(grid))

    def wrapped(*refs):
        ins = refs[:n_in]
        refs = refs[n_in + n_fill:]
        cins, outs, couts = refs[:n_cin], refs[n_cin:n_cin + n_out], refs[n_cin + n_out:n_cin + n_out + n_cout]
        rest = refs[n_cin + n_out + n_cout:]
        scr, sems = rest[:n_scr], rest[n_scr:]
        if comm is None:
            body(*ins, *outs, *scr)
            return
        step = pl.program_id(0)
        for ax in range(1, len(grid)):
            step = step * grid[ax] + pl.program_id(ax)
        phases = comm.phases(cins, couts, *sems, total)
        for at, action in phases:
            if at == 0:
                pl.when(step == 0)(action)
        body(*ins, *outs, *scr)
        for at, action in phases:
            if at > 0:
                pl.when(step == at)(action)

    any_spec = pl.BlockSpec(memory_space=pl.ANY)
    sems = [] if comm is None else [pltpu.SemaphoreType.DMA((comm.n_sems,)), pltpu.SemaphoreType.DMA((comm.n_sems,)),
                                    pltpu.SemaphoreType.DMA((max(comm.n_local, 1),))]
    call = pl.pallas_call(
        wrapped, name=name, grid=grid, in_specs=list(in_specs) + [any_spec] * (n_fill + n_cin),
        out_specs=o_specs + [any_spec] * n_cout, out_shape=o_shape + c_shapes,
        scratch_shapes=list(scratch) + sems, input_output_aliases={n_in + j: j for j in range(n_fill)},
        compiler_params=pltpu.CompilerParams(has_side_effects=comm is not None, **params))

    def run(*args):
        res = call(*args, *fill, *srcs)
        own = res[0] if single else res[:n_out]
        return (own, res[n_out:]) if comm is not None else own

    return run


def _sds(shape, dtype):
    return jax.ShapeDtypeStruct(shape, dtype)


def _mm(name, a, b, *, dims, grid, a_spec, b_spec, o_spec, out_shape, nred=1, red_axis=0,
        add=None, add_spec=None, add_scale=1.0, comm=None, norm=None, inner=None):
    has_add = add is not None
    n_extra = (1 if has_add else 0) + (2 if norm else 0)
    n_out = 3 if norm else 1
    acc_shape = tuple(s for s in o_spec.block_shape if s is not None)

    def body(*refs):
        a_ref, b_ref = refs[:2]
        extra = refs[2:2 + n_extra]
        outs = refs[2 + n_extra:2 + n_extra + n_out]
        rest = refs[2 + n_extra + n_out:]
        o_ref = outs[0]
        if inner is None:
            prod = lax.dot_general(a_ref[...].astype(BF16), b_ref[...].astype(BF16), dims,
                                   preferred_element_type=F32)
        else:
            nb, a_mode = inner
            width = a_ref.shape[-1] // nb
            prod = None
            for d in range(nb):
                a_d = a_ref[d] if a_mode == "lead" else a_ref[:, d * width:(d + 1) * width]
                part = lax.dot_general(a_d.astype(BF16), b_ref[d].astype(BF16), dims, preferred_element_type=F32)
                prod = part if prod is None else prod + part

        def finish(val):
            if has_add:
                val = val + add_scale * extra[0][...].astype(F32)
            o_ref[...] = val.astype(o_ref.dtype)
            if norm:
                xhat, _ = _ln_stats(val)
                y = xhat * extra[-2][...] + extra[-1][...]
                outs[1][...] = y
                outs[2][...] = y.astype(BF16)

        if nred == 1:
            finish(prod)
        else:
            acc = rest[0]
            k = pl.program_id(red_axis)

            @pl.when(k == 0)
            def _():
                acc[...] = prod

            @pl.when(k > 0)
            def _():
                acc[...] += prod

            @pl.when(k == nred - 1)
            def _():
                finish(acc[...])

    in_specs = [a_spec, b_spec] + ([add_spec] if has_add else [])
    scratch = [pltpu.VMEM(acc_shape, F32)] if nred > 1 else []
    args = (a, b) + ((add,) if has_add else ())
    if norm:
        vec = pl.BlockSpec((1, out_shape.shape[1]), lambda *_: (0, 0))
        in_specs += [vec, vec]
        args += tuple(norm)
        o_spec = [o_spec, o_spec, o_spec]
        out_shape = [out_shape, out_shape, _sds(out_shape.shape, BF16)]
    return _call(body, name, grid, in_specs, o_spec, out_shape, scratch, comm=comm)(*args)


def _mm_nn(name, a, b, out_dtype, blocked_out=False, comm=None, blocks=None, residual=None, norm=None):
    TM = TM_MM if out_dtype == BF16 else TM_MM_F32
    m, k = a.shape
    _, _, n = b.shape
    nb = b.shape[0] if blocks is None else blocks[1]
    first = 0 if blocks is None else blocks[0]
    if blocked_out:
        o_spec = pl.BlockSpec((None, TM, n), lambda d, i: (d, i, 0))
        out_shape = _sds((nb, m, n), out_dtype)
    else:
        o_spec = pl.BlockSpec((TM, n), lambda d, i: (i, d))
        out_shape = _sds((m, nb * n), out_dtype)
    return _mm(name, a, b, dims=NN, grid=(nb, m // TM),
               a_spec=pl.BlockSpec((TM, k), lambda d, i: (i, 0)),
               b_spec=pl.BlockSpec((None, k, n), lambda d, i: (d + first, 0, 0)),
               o_spec=o_spec, out_shape=out_shape, comm=comm, norm=norm,
               add=residual, add_spec=o_spec if residual is not None else None, add_scale=DN_ALPHA)


def _mm_nn_red(name, a, b, out_dtype, comm=None, residual=None, norm=None):
    nb, m, k = a.shape
    n = b.shape[2]
    o_spec = pl.BlockSpec((TM, n), lambda i: (i, 0))
    return _mm(name, a, b, dims=NN, grid=(m // TM,), inner=(nb, "lead"),
               a_spec=pl.BlockSpec((nb, TM, k), lambda i: (0, i, 0)),
               b_spec=pl.BlockSpec((nb, k, n), lambda i: (0, 0, 0)),
               o_spec=o_spec, out_shape=_sds((m, n), out_dtype), comm=comm, norm=norm,
               add=residual, add_spec=o_spec if residual is not None else None, add_scale=DN_ALPHA)


def _mm_tn(name, a, g, nb, out_dtype, a_blocked=False, g_mode="cols", comm=None):
    TM = TM_MM
    if a_blocked:
        _, m, ka = a.shape
        a_spec = pl.BlockSpec((None, TM, ka), lambda d, i: (d, i, 0))
    else:
        m, ka = a.shape
        a_spec = pl.BlockSpec((TM, ka), lambda d, i: (i, 0))
    if g_mode == "cols":
        n = g.shape[1] // nb
        g_spec = pl.BlockSpec((TM, n), lambda d, i: (i, d))
    elif g_mode == "blocked":
        n = g.shape[2]
        g_spec = pl.BlockSpec((None, TM, n), lambda d, i: (d, i, 0))
    else:
        n = g.shape[1]
        g_spec = pl.BlockSpec((TM, n), lambda d, i: (i, 0))
    nred = m // TM
    return _mm(name, a, g, dims=TN, grid=(nb, nred), nred=nred, red_axis=1,
               a_spec=a_spec, b_spec=g_spec,
               o_spec=pl.BlockSpec((None, ka, n), lambda d, i: (d, 0, 0)),
               out_shape=_sds((nb, ka, n), out_dtype), comm=comm)


def _mm_nt_red(name, g, w, out_dtype, g_blocked, add=None, add_scale=1.0, comm=None, blocks=None):
    _, k, n = w.shape
    nb = w.shape[0] if blocks is None else blocks[1]
    first = 0 if blocks is None else blocks[0]
    tm = TM_MM_F32 if nb * k * n <= 4 * 1024 * 768 else TM
    if g_blocked:
        m = g.shape[1]
        g_spec = pl.BlockSpec((nb, tm, n), lambda i: (0, i, 0))
    else:
        m = g.shape[0]
        g_spec = pl.BlockSpec((tm, nb * n), lambda i: (i, 0))
    o_spec = pl.BlockSpec((tm, k), lambda i: (i, 0))
    return _mm(name, g, w, dims=NT, grid=(m // tm,), inner=(nb, "lead" if g_blocked else "cols"),
               a_spec=g_spec, b_spec=pl.BlockSpec((nb, k, n), lambda i: (first // nb, 0, 0)),
               o_spec=o_spec, out_shape=_sds((m, k), out_dtype),
               add=add, add_spec=o_spec if add is not None else None, add_scale=add_scale, comm=comm)


def _mm_nt_out(name, g, w, out_dtype):
    TM = TM_MM
    m, n = g.shape
    nb, k, _ = w.shape
    return _mm(name, g, w, dims=NT, grid=(nb, m // TM),
               a_spec=pl.BlockSpec((TM, n), lambda d, i: (i, 0)),
               b_spec=pl.BlockSpec((None, k, n), lambda d, i: (d, 0, 0)),
               o_spec=pl.BlockSpec((None, TM, k), lambda d, i: (d, i, 0)),
               out_shape=_sds((nb, m, k), out_dtype))


def _ln_stats(r):
    mu = jnp.mean(r, axis=-1, keepdims=True)
    xc = r - mu
    var = jnp.mean(xc * xc, axis=-1, keepdims=True)
    rstd = lax.rsqrt(var + LN_EPS)
    return xc * rstd, rstd


def _ln_bwd(name, r, dy, gain):
    n, d = r.shape

    def body(r_ref, dy_ref, g_ref, dr_ref, drb_ref, dg_ref, db_ref):
        i = pl.program_id(0)
        xhat, rstd = _ln_stats(r_ref[...])
        dy_v = dy_ref[...]
        dxh = dy_v * g_ref[...]
        m1 = jnp.mean(dxh, axis=-1, keepdims=True)
        m2 = jnp.mean(dxh * xhat, axis=-1, keepdims=True)
        dr = rstd * (dxh - m1 - xhat * m2)
        dr_ref[...] = dr
        drb_ref[...] = dr.astype(BF16)
        dg = jnp.sum(dy_v * xhat, axis=0, keepdims=True)
        db = jnp.sum(dy_v, axis=0, keepdims=True)

        @pl.when(i == 0)
        def _():
            dg_ref[...] = dg
            db_ref[...] = db

        @pl.when(i > 0)
        def _():
            dg_ref[...] += dg
            db_ref[...] += db

    row = pl.BlockSpec((TM, d), lambda i: (i, 0))
    vec = pl.BlockSpec((1, d), lambda i: (0, 0))
    return _call(body, name, (n // TM,), [row, row, vec], [row, row, vec, vec],
                 [_sds((n, d), F32), _sds((n, d), BF16), _sds((1, d), F32), _sds((1, d), F32)])(r, dy, gain)


def _loss_head(y, target):
    n, d = y.shape

    def body(y_ref, t_ref, l_ref, dy_ref):
        i = pl.program_id(0)
        e = y_ref[...] - t_ref[...]
        dy_ref[...] = e * (1.0 / d)
        part = jnp.sum(jnp.sum(e * e, axis=1, keepdims=True), axis=0, keepdims=True) * (0.5 / d)
        part = jnp.broadcast_to(part, (8, 128))

        @pl.when(i == 0)
        def _():
            l_ref[...] = part

        @pl.when(i > 0)
        def _():
            l_ref[...] += part

    row = pl.BlockSpec((TM, d), lambda i: (i, 0))
    return _call(body, "loss_head", (n // TM,), [row, row],
                 [pl.BlockSpec((8, 128), lambda i: (0, 0)), row],
                 [_sds((8, 128), F32), _sds((n, d), F32)])(y, target)


def _group_expand_matrix():
    e = np.zeros((D_MODEL // SSM_GROUP, N_STATE), np.float32)
    for g in range(D_MODEL // SSM_GROUP):
        e[g, g * SSM_STATE:(g + 1) * SSM_STATE] = 1.0
    return jnp.asarray(e)


def _discretise(lr, li, ldt, br, bi, expand):
    dt = jnp.dot(jnp.exp(ldt), expand, precision=HIGHEST, preferred_element_type=F32)
    mag = jnp.exp(lr * dt)
    th = li * dt
    ab_r = mag * jnp.cos(th)
    ab_i = mag * jnp.sin(th)
    den = lr * lr + li * li
    nr = ab_r - 1.0
    co_r = (nr * lr + ab_i * li) / den
    co_i = (ab_i * lr - nr * li) / den
    bb_r = co_r * br - co_i * bi
    bb_i = co_r * bi + co_i * br
    return ab_r, ab_i, bb_r, bb_i


def _cmul(ar, ai, br, bi):
    return ar * br - ai * bi, ar * bi + ai * br


def _s5_prep(lr, li, ldt, br, bi):
    s = N_STATE
    expand = _group_expand_matrix()

    def body(lr_ref, li_ref, ldt_ref, br_ref, bi_ref, e_ref, abr_ref, abi_ref, bbr_ref, bbi_ref, tab_ref):
        ab_r, ab_i, bb_r, bb_i = _discretise(lr_ref[...], li_ref[...], ldt_ref[...], br_ref[...],
                                             bi_ref[...], e_ref[...])
        abr_ref[...] = ab_r
        abi_ref[...] = ab_i
        bbr_ref[...] = bb_r
        bbi_ref[...] = bb_i
        row = lax.broadcasted_iota(jnp.int32, (8, s), 0)
        for base, sign in ((0, 1.0), (8, -1.0)):
            p = [None] * 9
            p[1] = (ab_r, sign * ab_i)
            p[2] = _cmul(*p[1], *p[1])
            p[3] = _cmul(*p[2], *p[1])
            p[4] = _cmul(*p[2], *p[2])
            p[5] = _cmul(*p[4], *p[1])
            p[6] = _cmul(*p[4], *p[2])
            p[7] = _cmul(*p[4], *p[3])
            p[8] = _cmul(*p[4], *p[4])
            for j, k in enumerate((1, 2, 4)):
                keep = (row >= k) if base == 0 else (row < 8 - k)
                tab_ref[base + 2 * j] = jnp.where(keep, jnp.broadcast_to(p[k][0], (8, s)), 0.0)
                tab_ref[base + 2 * j + 1] = jnp.where(keep, jnp.broadcast_to(p[k][1], (8, s)), 0.0)
            pw_r = jnp.zeros((8, s), F32)
            pw_i = jnp.zeros((8, s), F32)
            for i in range(8):
                e = i + 1 if base == 0 else 8 - i
                pw_r = jnp.where(row == i, jnp.broadcast_to(p[e][0], (8, s)), pw_r)
                pw_i = jnp.where(row == i, jnp.broadcast_to(p[e][1], (8, s)), pw_i)
            tab_ref[base + 6] = pw_r
            tab_ref[base + 7] = pw_i

    def full(shape):
        return pl.BlockSpec(shape, lambda i: (0,) * len(shape))

    g = D_MODEL // SSM_GROUP
    return _call(body, "s5_prep", (1,),
                 [full((1, s)), full((1, s)), full((1, g)), full((16, s)), full((16, s)), full((g, s))],
                 [full((1, s)), full((1, s)), full((16, s)), full((16, s)), full((16, 8, s))],
                 [_sds((1, s), F32), _sds((1, s), F32), _sds((16, s), F32), _sds((16, s), F32),
                  _sds((16, 8, s), F32)])(lr, li, ldt, br, bi, expand)


def _s5_prep_bwd(lr, li, ldt, br, bi, d_abr, d_abi, d_bbr, d_bbi):
    s = N_STATE
    g = D_MODEL // SSM_GROUP
    expand = _group_expand_matrix()

    def body(lr_ref, li_ref, ldt_ref, br_ref, bi_ref, e_ref, c1, c2, c3, c4, o1, o2, o3, o4, o5):
        e_val = e_ref[...]
        _, vjp = jax.vjp(lambda a, b, c, d, e: _discretise(a, b, c, d, e, e_val),
                         lr_ref[...], li_ref[...], ldt_ref[...], br_ref[...], bi_ref[...])
        grads = vjp((c1[...], c2[...], c3[...], c4[...]))
        for o, gr in zip((o1, o2, o3, o4, o5), grads):
            o[...] = gr

    def full(shape):
        return pl.BlockSpec(shape, lambda i: (0,) * len(shape))

    return _call(body, "s5_prep_bwd", (1,),
                 [full((1, s)), full((1, s)), full((1, g)), full((16, s)), full((16, s)), full((g, s)),
                  full((1, s)), full((1, s)), full((16, s)), full((16, s))],
                 [full((1, s)), full((1, s)), full((1, g)), full((16, s)), full((16, s))],
                 [_sds((1, s), F32), _sds((1, s), F32), _sds((1, g), F32), _sds((16, s), F32),
                  _sds((16, s), F32)])(lr, li, ldt, br, bi, expand, d_abr, d_abi, d_bbr, d_bbi)


def _scan8(vr, vi, tab_ref, base, reverse):
    for j, k in enumerate((1, 2, 4)):
        mr = tab_ref[base + 2 * j]
        mi = tab_ref[base + 2 * j + 1]
        shift = 8 - k if reverse else k
        sr = pltpu.roll(vr, shift, 0)
        si = pltpu.roll(vi, shift, 0)
        vr, vi = vr + mr * sr - mi * si, vi + mr * si + mi * sr
    return vr, vi


def _s5_scan_fwd(x, bbr, bbi, cr, ci, dvec, tab, seq, comm=None):
    n, d = x.shape
    nt = seq // T_SCAN
    nblk = T_SCAN // 8
    st = CHUNK_ST

    def body(x_ref, bbr_ref, bbi_ref, cr_ref, ci_ref, d_ref, tab_ref, y_ref, hr_ref, hi_ref, car_r, car_i):
        t = pl.program_id(2)

        @pl.when(t == 0)
        def _():
            car_r[...] = jnp.zeros_like(car_r)
            car_i[...] = jnp.zeros_like(car_i)

        u = x_ref[...]
        ub = u.astype(BF16)
        hr_ref[...] = jnp.dot(ub, bbr_ref[...], preferred_element_type=F32)
        hi_ref[...] = jnp.dot(ub, bbi_ref[...], preferred_element_type=F32)

        def step(i, carry):
            sl = pl.ds(pl.multiple_of(i * 8, 8), 8)
            vr, vi = _scan8(hr_ref[sl, :], hi_ref[sl, :], tab_ref, 0, False)
            c_r = jnp.broadcast_to(car_r[...], (8, st))
            c_i = jnp.broadcast_to(car_i[...], (8, st))
            pr = tab_ref[6]
            pi = tab_ref[7]
            vr, vi = vr + pr * c_r - pi * c_i, vi + pr * c_i + pi * c_r
            hr_ref[sl, :] = vr
            hi_ref[sl, :] = vi
            car_r[...] = vr[7:8, :]
            car_i[...] = vi[7:8, :]
            return carry

        lax.fori_loop(0, nblk, step, 0, unroll=4)
        y = lax.dot_general(hr_ref[...].astype(BF16), cr_ref[...], NT, preferred_element_type=F32)
        y = y - lax.dot_general(hi_ref[...].astype(BF16), ci_ref[...], NT, preferred_element_type=F32)
        y_ref[...] = y + d_ref[...] * u

    row = lambda c, b, t: (b * nt + t, c)
    mat = pl.BlockSpec((None, CHUNK_CH, st), lambda c, b, t: (c, 0, 0))
    return _call(
        body, "s5_scan_fwd", (SSM_CHUNKS, n // seq, nt),
        [pl.BlockSpec((T_SCAN, CHUNK_CH), row), mat, mat, mat, mat,
         pl.BlockSpec((1, CHUNK_CH), lambda c, b, t: (0, c)),
         pl.BlockSpec((16, 8, st), lambda c, b, t: (0, 0, c))],
        [pl.BlockSpec((T_SCAN, CHUNK_CH), row), pl.BlockSpec((T_SCAN, st), row), pl.BlockSpec((T_SCAN, st), row)],
        [_sds((n, d), F32), _sds((n, N_STATE), F32), _sds((n, N_STATE), F32)],
        [pltpu.VMEM((1, st), F32), pltpu.VMEM((1, st), F32)], comm=comm,
    )(x, bbr, bbi, cr, ci, dvec, tab)


def _s5_scan_bwd(x, dy, dres, hr, hi, bbr, bbi, cr, ci, dvec, tab, seq, comm=None):
    n, d = x.shape
    nt = seq // T_SCAN
    nblk = T_SCAN // 8
    st = CHUNK_ST

    def body(x_ref, dy_ref, dres_ref, hr_ref, hi_ref, hpr_ref, hpi_ref, bbr_ref, bbi_ref, cr_ref, ci_ref,
             d_ref, tab_ref, dx_ref, dcr_ref, dci_ref, dbr_ref, dbi_ref, dar_ref, dai_ref, dd_ref,
             g_r, g_i, car_r, car_i):
        b = pl.program_id(1)
        tg = pl.program_id(2)
        first = jnp.logical_and(b == 0, tg == 0)

        @pl.when(tg == 0)
        def _():
            car_r[...] = jnp.zeros_like(car_r)
            car_i[...] = jnp.zeros_like(car_i)

        u = x_ref[...]
        dyv = dy_ref[...]
        ub = u.astype(BF16)
        dyb = dyv.astype(BF16)
        g_r[...] = jnp.dot(dyb, cr_ref[...], preferred_element_type=F32)
        g_i[...] = -jnp.dot(dyb, ci_ref[...], preferred_element_type=F32)

        def step(ii, carry):
            i = nblk - 1 - ii
            sl = pl.ds(pl.multiple_of(i * 8, 8), 8)
            vr, vi = _scan8(g_r[sl, :], g_i[sl, :], tab_ref, 8, True)
            c_r = jnp.broadcast_to(car_r[...], (8, st))
            c_i = jnp.broadcast_to(car_i[...], (8, st))
            pr = tab_ref[14]
            pi = tab_ref[15]
            vr, vi = vr + pr * c_r - pi * c_i, vi + pr * c_i + pi * c_r
            g_r[sl, :] = vr
            g_i[sl, :] = vi
            car_r[...] = vr[0:1, :]
            car_i[...] = vi[0:1, :]
            return carry

        lax.fori_loop(0, nblk, step, 0, unroll=4)
        gr = g_r[...]
        gi = g_i[...]
        grb = gr.astype(BF16)
        gib = gi.astype(BF16)
        du = lax.dot_general(grb, bbr_ref[...], NT, preferred_element_type=F32)
        du = du + lax.dot_general(gib, bbi_ref[...], NT, preferred_element_type=F32)
        dx_ref[...] = DN_ALPHA * dres_ref[...] + du + d_ref[...] * dyv

        h_r = hr_ref[...]
        h_i = hi_ref[...]
        t_orig = nt - 1 - tg
        row0 = lax.broadcasted_iota(jnp.int32, (8, st), 0) == 0
        halo_ok = t_orig > 0

        def previous(h, halo_ref):
            top = jnp.broadcast_to(jnp.where(halo_ok, halo_ref[7:8, :], 0.0), (8, st))
            rolled = pltpu.roll(h, 1, 0)
            return jnp.concatenate([jnp.where(row0, top, rolled[:8, :]), rolled[8:, :]], axis=0)

        hp_r = previous(h_r, hpr_ref)
        hp_i = previous(h_i, hpi_ref)
        dar = jnp.sum(gr * hp_r + gi * hp_i, axis=0, keepdims=True)
        dai = jnp.sum(gi * hp_r - gr * hp_i, axis=0, keepdims=True)
        dcr = lax.dot_general(dyb, h_r.astype(BF16), TN, preferred_element_type=F32)
        dci = lax.dot_general(dyb, h_i.astype(BF16), TN, preferred_element_type=F32)
        dbr = lax.dot_general(ub, grb, TN, preferred_element_type=F32)
        dbi = lax.dot_general(ub, gib, TN, preferred_element_type=F32)
        dd = jnp.sum(dyv * u, axis=0, keepdims=True)

        @pl.when(first)
        def _():
            dcr_ref[...] = dcr
            dci_ref[...] = dci
            dbr_ref[...] = dbr
            dbi_ref[...] = dbi
            dar_ref[...] = dar
            dai_ref[...] = dai
            dd_ref[...] = dd

        @pl.when(jnp.logical_not(first))
        def _():
            dcr_ref[...] += dcr
            dci_ref[...] += dci
            dbr_ref[...] += dbr
            dbi_ref[...] += dbi
            dar_ref[...] += dar
            dai_ref[...] += dai
            dd_ref[...] += dd

    row = lambda c, b, t: (b * nt + (nt - 1 - t), c)
    halo = lambda c, b, t: (jnp.maximum((b * nt + (nt - 1 - t)) * (T_SCAN // 8) - 1, 0), c)
    mat = pl.BlockSpec((None, CHUNK_CH, st), lambda c, b, t: (c, 0, 0))
    chan = pl.BlockSpec((T_SCAN, CHUNK_CH), row)
    state = pl.BlockSpec((T_SCAN, st), row)
    return _call(
        body, "s5_scan_bwd", (SSM_CHUNKS, n // seq, nt),
        [chan, chan, chan, state, state, pl.BlockSpec((8, st), halo), pl.BlockSpec((8, st), halo),
         mat, mat, mat, mat, pl.BlockSpec((1, CHUNK_CH), lambda c, b, t: (0, c)),
         pl.BlockSpec((16, 8, st), lambda c, b, t: (0, 0, c))],
        [chan, mat, mat, mat, mat, pl.BlockSpec((1, st), lambda c, b, t: (0, c)),
         pl.BlockSpec((1, st), lambda c, b, t: (0, c)), pl.BlockSpec((1, CHUNK_CH), lambda c, b, t: (0, c))],
        [_sds((n, d), F32)] + [_sds((SSM_CHUNKS, CHUNK_CH, st), F32)] * 4
        + [_sds((1, N_STATE), F32), _sds((1, N_STATE), F32), _sds((1, d), F32)],
        [pltpu.VMEM((T_SCAN, st), F32), pltpu.VMEM((T_SCAN, st), F32), pltpu.VMEM((1, st), F32),
         pltpu.VMEM((1, st), F32)], comm=comm,
    )(x, dy, dres, hr, hi, hr, hi, bbr, bbi, cr, ci, dvec, tab)


_GELU_C = math.sqrt(2.0 / math.pi)


def _gelu_parts(y):
    t = jnp.tanh(_GELU_C * (y + 0.044715 * (y * y * y)))
    return 0.5 * (1.0 + t), t


def _glu_fwd(y, w_glu, b_glu):
    n, d = y.shape

    def body(y_ref, w_ref, b_ref, y2_ref, z_ref, gg_ref):
        yv = y_ref[...]
        cdf, _ = _gelu_parts(yv)
        y2 = yv * cdf
        z = jnp.dot(y2.astype(BF16), w_ref[...], preferred_element_type=F32) + b_ref[...]
        y2_ref[...] = y2.astype(BF16)
        z_ref[...] = z
        gg_ref[...] = (y2 * jax.nn.sigmoid(z)).astype(BF16)

    row = pl.BlockSpec((TM, d), lambda i: (i, 0))
    return _call(body, "glu_fwd", (n // TM,),
                 [row, pl.BlockSpec((d, d), lambda i: (0, 0)), pl.BlockSpec((1, d), lambda i: (0, 0))],
                 [row, row, row], [_sds((n, d), BF16), _sds((n, d), F32), _sds((n, d), BF16)])(y, w_glu, b_glu)


def _glu_bwd_gate(y, z, dgg):
    n, d = y.shape

    def body(y_ref, z_ref, dgg_ref, dz_ref, db_ref, t_ref):
        i = pl.program_id(0)
        yv = y_ref[...]
        cdf, _ = _gelu_parts(yv)
        y2 = yv * cdf
        s = jax.nn.sigmoid(z_ref[...])
        dg = dgg_ref[...]
        dz = dg * y2 * s * (1.0 - s)
        dz_ref[...] = dz.astype(BF16)
        t_ref[...] = dg * s
        db = jnp.sum(dz, axis=0, keepdims=True)

        @pl.when(i == 0)
        def _():
            db_ref[...] = db

        @pl.when(i > 0)
        def _():
            db_ref[...] += db

    row = pl.BlockSpec((TM, d), lambda i: (i, 0))
    vec = pl.BlockSpec((1, d), lambda i: (0, 0))
    return _call(body, "glu_bwd_gate", (n // TM,), [row, row, row], [row, vec, row],
                 [_sds((n, d), BF16), _sds((1, d), F32), _sds((n, d), F32)])(y, z, dgg)


def _gelu_bwd(y, dy2):
    n, d = y.shape

    def body(y_ref, g_ref, o_ref):
        yv = y_ref[...]
        cdf, t = _gelu_parts(yv)
        dinner = _GELU_C * (1.0 + 3.0 * 0.044715 * yv * yv)
        o_ref[...] = g_ref[...] * (cdf + 0.5 * yv * (1.0 - t * t) * dinner)

    row = pl.BlockSpec((TM, d), lambda i: (i, 0))
    return _call(body, "gelu_bwd", (n // TM,), [row, row], row, _sds((n, d), F32))(y, dy2)


HALO = 16


def _shift_down(x, halo, k):
    out = pltpu.roll(x, k, 0)
    row = lax.broadcasted_iota(jnp.int32, (8, x.shape[1]), 0)
    top = jnp.where(row < k, pltpu.roll(halo[HALO - 8:HALO, :], k, 0), out[:8, :])
    return jnp.concatenate([top, out[8:, :]], axis=0)


def _conv3(x, halo, w, b):
    x1 = _shift_down(x, halo, 1)
    x2 = _shift_down(x, halo, 2)
    return b + w[0:1, :] * x + w[1:2, :] * x1 + w[2:3, :] * x2, x1, x2


def _conv_gate_fwd(hc, conv_w, conv_b, seq):
    _, nb, n, c = hc.shape
    tiles_per_seq = seq // TM

    def body(x_ref, halo_ref, w_ref, b_ref, o_ref):
        i = pl.program_id(1)
        start = (i % tiles_per_seq) == 0
        cv = []
        for h in range(2):
            halo = jnp.where(start, 0.0, halo_ref[h].astype(F32))
            cv.append(_conv3(x_ref[h].astype(F32), halo, w_ref[h], b_ref[h])[0])
        o_ref[...] = (cv[1] * jax.nn.sigmoid(cv[1]) * cv[0]).astype(BF16)

    return _call(
        body, "conv_gate_fwd", (nb, n // TM),
        [pl.BlockSpec((2, None, TM, c), lambda d, i: (0, d, i, 0)),
         pl.BlockSpec((2, None, HALO, c), lambda d, i: (0, d, jnp.maximum(i * (TM // HALO) - 1, 0), 0)),
         pl.BlockSpec((2, None, 3, c), lambda d, i: (0, d, 0, 0)),
         pl.BlockSpec((2, None, 1, c), lambda d, i: (0, d, 0, 0))],
        pl.BlockSpec((None, TM, c), lambda d, i: (d, i, 0)),
        _sds((nb, n, c), BF16))(hc, hc, conv_w, conv_b)


def _conv_gate_bwd(hc, dact, conv_w, conv_b, seq, comm=None):
    _, nb, n, c = hc.shape
    tiles_per_seq = seq // TM
    last_halo = n // HALO - 1
    ext = TM + HALO

    def body(x_ref, prev_ref, next_ref, da_ref, dan_ref, w_ref, b_ref, dx_ref, dw_ref):
        i = pl.program_id(1)
        start = (i % tiles_per_seq) == 0
        end = (i % tiles_per_seq) == tiles_per_seq - 1
        row = lax.broadcasted_iota(jnp.int32, (ext, c), 0)
        xe, cv = [], []
        for h in range(2):
            halo = jnp.where(start, 0.0, prev_ref[h].astype(F32))
            x_ext = jnp.concatenate([x_ref[h], next_ref[h]], axis=0).astype(F32)
            conv, x1, x2 = _conv3(x_ext, halo, w_ref[h], b_ref[h])
            xe.append((x_ext, x1, x2))
            cv.append(conv)
        da = jnp.concatenate([da_ref[...], dan_ref[...]], axis=0).astype(F32)
        da = jnp.where(jnp.logical_and(end, row >= TM), 0.0, da)
        sg = jax.nn.sigmoid(cv[1])
        silu = cv[1] * sg
        dcv = [da * silu, da * cv[0] * (sg + silu * (1.0 - sg))]
        for h in range(2):
            w = w_ref[h]
            g = dcv[h]
            dx = w[0:1, :] * g + w[1:2, :] * pltpu.roll(g, ext - 1, 0) + w[2:3, :] * pltpu.roll(g, ext - 2, 0)
            dx_ref[h] = dx[:TM, :].astype(BF16)
            x_ext, x1, x2 = xe[h]
            gt = g[:TM, :]
            parts = [jnp.sum(gt * x_ext[:TM, :], axis=0, keepdims=True),
                     jnp.sum(gt * x1[:TM, :], axis=0, keepdims=True),
                     jnp.sum(gt * x2[:TM, :], axis=0, keepdims=True),
                     jnp.sum(gt, axis=0, keepdims=True)]
            upd = jnp.concatenate(parts + [jnp.zeros((4, c), F32)], axis=0)

            @pl.when(i == 0)
            def _():
                dw_ref[h] = upd

            @pl.when(i > 0)
            def _():
                dw_ref[h] += upd

    nxt = lambda i: jnp.minimum((i + 1) * (TM // HALO), last_halo)
    return _call(
        body, "conv_gate_bwd", (nb, n // TM),
        [pl.BlockSpec((2, None, TM, c), lambda d, i: (0, d, i, 0)),
         pl.BlockSpec((2, None, HALO, c), lambda d, i: (0, d, jnp.maximum(i * (TM // HALO) - 1, 0), 0)),
         pl.BlockSpec((2, None, HALO, c), lambda d, i: (0, d, nxt(i), 0)),
         pl.BlockSpec((None, TM, c), lambda d, i: (d, i, 0)),
         pl.BlockSpec((None, HALO, c), lambda d, i: (d, nxt(i), 0)),
         pl.BlockSpec((2, None, 3, c), lambda d, i: (0, d, 0, 0)),
         pl.BlockSpec((2, None, 1, c), lambda d, i: (0, d, 0, 0))],
        [pl.BlockSpec((2, None, TM, c), lambda d, i: (0, d, i, 0)),
         pl.BlockSpec((2, None, 8, c), lambda d, i: (0, d, 0, 0))],
        [_sds((2, nb, n, c), BF16), _sds((2, nb, 8, c), F32)], comm=comm)(hc, hc, hc, dact, dact, conv_w, conv_b)


def _t5_bucket_np(dist):
    exact = REL_BUCKETS // 2
    d = np.maximum(dist, 1).astype(np.float32)
    large = exact + (np.log(d / exact) / math.log(REL_MAX_DIST / exact) * (REL_BUCKETS - exact)).astype(np.int64)
    large = np.minimum(large, REL_BUCKETS - 1)
    return np.where(dist < exact, dist, large).astype(np.int32)


def _bucket_table(dil):
    steps = np.arange(BAND)[:, None] + BAND - np.arange(2 * BAND)[None, :]
    return _t5_bucket_np(np.maximum(steps, 0) * dil).reshape(1, BAND * 2 * BAND).astype(np.float32)


def _bias_expand(rel_t, dil):
    nk = BAND * 2 * BAND
    bucket = jnp.asarray(_bucket_table(dil))

    def body(r_ref, bk_ref, o_ref):
        ids = lax.broadcasted_iota(jnp.int32, (REL_BUCKETS, nk), 0).astype(F32)
        onehot = (ids == bk_ref[...]).astype(F32)
        o_ref[...] = jnp.dot(r_ref[...], onehot, precision=HIGHEST, preferred_element_type=F32)

    return _call(body, "bias_expand", (1,),
                 [pl.BlockSpec((N_HEADS, REL_BUCKETS), lambda i: (0, 0)), pl.BlockSpec((1, nk), lambda i: (0, 0))],
                 pl.BlockSpec((N_HEADS, nk), lambda i: (0, 0)), _sds((N_HEADS, nk), F32))(rel_t, bucket)


def _bias_reduce(dbias, dil):
    nk = BAND * 2 * BAND
    bucket = jnp.asarray(_bucket_table(dil))

    def body(g_ref, bk_ref, o_ref):
        ids = lax.broadcasted_iota(jnp.int32, (REL_BUCKETS, nk), 0).astype(F32)
        onehot = (ids == bk_ref[...]).astype(F32)
        o_ref[...] = lax.dot_general(g_ref[...], onehot, NT, precision=HIGHEST, preferred_element_type=F32)

    return _call(body, "bias_reduce", (1,),
                 [pl.BlockSpec((N_HEADS, nk), lambda i: (0, 0)), pl.BlockSpec((1, nk), lambda i: (0, 0))],
                 pl.BlockSpec((N_HEADS, REL_BUCKETS), lambda i: (0, 0)),
                 _sds((N_HEADS, REL_BUCKETS), F32))(dbias, bucket)


def _valid_mask(n):
    qi = lax.broadcasted_iota(jnp.int32, (BAND, 2 * BAND), 0)
    kj = lax.broadcasted_iota(jnp.int32, (BAND, 2 * BAND), 1)
    steps = qi + BAND - kj
    in_band = jnp.logical_and(steps >= 0, steps <= BAND)
    return jnp.logical_and(in_band, jnp.logical_or(n > 0, kj >= BAND))


ATTN_COLS = {1: 1024, 4: 512, 16: 128}


def _residue(dil, r):
    return slice(None) if dil == 1 else pl.ds(r, BAND, stride=dil)


def _stack_heads(x, first):
    return jnp.concatenate([jnp.where(first, x, 0.0), jnp.where(first, 0.0, x)], axis=0)


def _attn_fwd(q, k, v, bias, g, dil, bsz, seq, comm=None):
    n = q.shape[0]
    rows = BAND * dil
    nch = seq // rows
    w = N_HEADS * HEAD_DIM
    wc = ATTN_COLS[dil]
    ncol = w // wc
    hpc = wc // HEAD_DIM
    nj = wc // 128
    slab = lambda k: pltpu.VMEM((k * nj, rows, 128), F32)

    def body(q_ref, kc_ref, kp_ref, vc_ref, vp_ref, b_ref, o_ref, l_ref, qf, kf, vf, of, lf):
        ch = pl.program_id(2)
        valid = _valid_mask(ch)
        valid = jnp.concatenate([valid, valid], axis=0)
        lane = lax.broadcasted_iota(jnp.int32, (BAND, 128), 1)
        first = lane < HEAD_DIM
        for j in range(nj):
            sl = slice(128 * j, 128 * (j + 1))
            qf[j] = q_ref[:, sl].astype(F32)
            kf[j] = kp_ref[:, sl].astype(F32)
            kf[nj + j] = kc_ref[:, sl].astype(F32)
            vf[j] = vp_ref[:, sl].astype(F32)
            vf[nj + j] = vc_ref[:, sl].astype(F32)
        for r in range(dil):
            rr = _residue(dil, r)
            for j in range(nj):
                q2 = qf.at[j][rr, :]
                k2 = jnp.concatenate([kf.at[j][rr, :], kf.at[nj + j][rr, :]], axis=0).astype(BF16)
                v2 = jnp.concatenate([vf.at[j][rr, :], vf.at[nj + j][rr, :]], axis=0).astype(BF16)
                qs = _stack_heads(q2, first).astype(BF16)
                s = lax.dot_general(qs, k2, NT, preferred_element_type=F32) * (HEAD_DIM ** -0.5)
                s = jnp.where(valid, s + b_ref[2 * j:2 * j + 2].reshape(2 * BAND, 2 * BAND), NEG_BIG)
                m = jnp.max(s, axis=1, keepdims=True)
                p = jnp.exp(s - m)
                l = jnp.sum(p, axis=1, keepdims=True)
                pv = jnp.dot(p.astype(BF16), v2, preferred_element_type=F32) / l
                lse = jnp.broadcast_to(m + jnp.log(l), (2 * BAND, 128))
                of.at[j][rr, :] = jnp.where(first, pv[:BAND], pv[BAND:])
                lf.at[j][rr, :] = jnp.where(first, lse[:BAND], lse[BAND:])
        for j in range(nj):
            sl = slice(128 * j, 128 * (j + 1))
            o_ref[:, sl] = of[j]
            l_ref[:, sl] = lf[j]

    base = g * ncol
    cur = lambda c, b, ch: (b * nch + ch, base + c)
    prev = lambda c, b, ch: (b * nch + jnp.maximum(ch - 1, 0), base + c)
    blk = lambda im: pl.BlockSpec((rows, wc), im)
    out_blk = pl.BlockSpec((rows, wc), lambda c, b, ch: (b * nch + ch, c))
    return _call(
        body, "attn_fwd_d%d" % dil, (ncol, bsz, nch),
        [blk(cur), blk(cur), blk(prev), blk(cur), blk(prev),
         pl.BlockSpec((hpc, BAND, 2 * BAND), lambda c, b, ch: (c, 0, 0))],
        [out_blk, out_blk],
        [_sds((n, w), F32)] * 2,
        [slab(1), slab(2), slab(2), slab(1), slab(1)], comm=comm,
    )(q, k, k, v, v, bias)


def _attn_combine(outs, lses):
    n, w = outs[0].shape

    def body(o0, o1, o2, l0, l1, l2, o_ref, l_ref):
        la, lb, lc = l0[...], l1[...], l2[...]
        m = jnp.maximum(jnp.maximum(la, lb), lc)
        wa, wb, wc = jnp.exp(la - m), jnp.exp(lb - m), jnp.exp(lc - m)
        tot = wa + wb + wc
        o_ref[...] = (wa * o0[...] + wb * o1[...] + wc * o2[...]) / tot
        l_ref[...] = m + jnp.log(tot)

    row = pl.BlockSpec((TM, w), lambda i: (i, 0))
    return _call(body, "attn_combine", (n // TM,), [row] * 6, [row, row],
                 [_sds((n, w), F32)] * 2)(*outs, *lses)


def _attn_bwd(q, k, v, do, o, lse, bias, g, dil, bsz, seq, comm=None, into=()):
    n = q.shape[0]
    rows = BAND * dil
    nch = seq // rows
    w = N_HEADS * HEAD_DIM
    wc = ATTN_COLS[dil]
    ncol = w // wc
    hpc = wc // HEAD_DIM
    nj = wc // 128
    slab = lambda k: pltpu.VMEM((k * nj, rows, 128), F32)
    scale = HEAD_DIM ** -0.5

    def body(q_ref, kc_ref, kp_ref, vc_ref, vp_ref, do_ref, o_ref, l_ref, b_ref,
             dq_ref, dk_ref, dv_ref, db_ref, qf, kf, vf, dof, of, lf, dqf, dkf, dvf):
        b = pl.program_id(1)
        ch = pl.program_id(2)
        cur = ch % 2
        prv = 1 - cur

        @pl.when(jnp.logical_and(b == 0, ch == 0))
        def _():
            db_ref[...] = jnp.zeros_like(db_ref)

        @pl.when(ch == 0)
        def _():
            for j in range(nj):
                dkf[nj + j] = jnp.zeros((rows, 128), F32)
                dvf[nj + j] = jnp.zeros((rows, 128), F32)

        @pl.when(ch < nch)
        def _():
            valid = _valid_mask(ch)
            valid = jnp.concatenate([valid, valid], axis=0)
            first = lax.broadcasted_iota(jnp.int32, (BAND, 128), 1) < HEAD_DIM
            for j in range(nj):
                sl = slice(128 * j, 128 * (j + 1))
                qf[j] = q_ref[:, sl].astype(F32)
                kf[j] = kp_ref[:, sl].astype(F32)
                kf[nj + j] = kc_ref[:, sl].astype(F32)
                vf[j] = vp_ref[:, sl].astype(F32)
                vf[nj + j] = vc_ref[:, sl].astype(F32)
                dof[j] = do_ref[:, sl]
                of[j] = o_ref[:, sl]
                lf[j] = l_ref[:, sl]
            for r in range(dil):
                rr = _residue(dil, r)
                for j in range(nj):
                    k2 = jnp.concatenate([kf.at[j][rr, :], kf.at[nj + j][rr, :]], axis=0).astype(BF16)
                    v2 = jnp.concatenate([vf.at[j][rr, :], vf.at[nj + j][rr, :]], axis=0).astype(BF16)
                    do2 = dof.at[j][rr, :]
                    lse2 = lf.at[j][rr, :]
                    qs = _stack_heads(qf.at[j][rr, :], first).astype(BF16)
                    dos = _stack_heads(do2, first)
                    delta = jnp.sum(dos * jnp.concatenate([of.at[j][rr, :]] * 2, axis=0), axis=1, keepdims=True)
                    dos = dos.astype(BF16)
                    lse_col = jnp.concatenate([lse2[:, 0:1], lse2[:, HEAD_DIM:HEAD_DIM + 1]], axis=0)
                    s = lax.dot_general(qs, k2, NT, preferred_element_type=F32) * scale
                    s = jnp.where(valid, s + b_ref[2 * j:2 * j + 2].reshape(2 * BAND, 2 * BAND), NEG_BIG)
                    p = jnp.exp(s - lse_col)
                    dp = lax.dot_general(dos, v2, NT, preferred_element_type=F32)
                    ds = p * (dp - delta)
                    db_ref[2 * j:2 * j + 2] += ds.reshape(2, BAND, 2 * BAND)
                    dsb = ds.astype(BF16)
                    dq2 = jnp.dot(dsb, k2, preferred_element_type=F32) * scale
                    dk2 = lax.dot_general(dsb, qs, TN, preferred_element_type=F32) * scale
                    dv2 = lax.dot_general(p.astype(BF16), dos, TN, preferred_element_type=F32)
                    dqf.at[j][rr, :] = jnp.where(first, dq2[:BAND], dq2[BAND:])
                    dkf.at[prv * nj + j][rr, :] += dk2[:BAND, :]
                    dvf.at[prv * nj + j][rr, :] += dv2[:BAND, :]
                    dkf.at[cur * nj + j][rr, :] = dk2[BAND:, :]
                    dvf.at[cur * nj + j][rr, :] = dv2[BAND:, :]
            for j in range(nj):
                dq_ref[:, 128 * j:128 * (j + 1)] = dqf[j].astype(BF16)

        for j in range(nj):
            sl = slice(128 * j, 128 * (j + 1))
            dk_ref[:, sl] = dkf[prv * nj + j].astype(BF16)
            dv_ref[:, sl] = dvf[prv * nj + j].astype(BF16)

    last = nch - 1
    base = g * ncol
    cur3 = lambda c, b, ch: (b * nch + jnp.minimum(ch, last), base + c)
    prev3 = lambda c, b, ch: (b * nch + jnp.maximum(jnp.minimum(ch, last) - 1, 0), base + c)
    cur1 = lambda c, b, ch: (b * nch + jnp.minimum(ch, last), c)
    lag3 = lambda c, b, ch: (b * nch + jnp.maximum(ch - 1, 0), base + c)
    blk = lambda im: pl.BlockSpec((rows, wc), im)
    bias_blk = pl.BlockSpec((hpc, BAND, 2 * BAND), lambda c, b, ch: (c, 0, 0))
    out = _sds(q.shape, BF16)
    return _call(
        body, "attn_bwd_d%d" % dil, (ncol, bsz, nch + 1),
        [blk(cur3), blk(cur3), blk(prev3), blk(cur3), blk(prev3), blk(cur1), blk(cur1), blk(cur1), bias_blk],
        [blk(cur3), blk(lag3), blk(lag3), bias_blk],
        [out, out, out, _sds((N_HEADS, BAND, 2 * BAND), F32)],
        [slab(1), slab(2), slab(2), slab(1), slab(1), slab(1), slab(1), slab(2), slab(2)], comm=comm, fill=into,
    )(q, k, k, v, v, do, o, lse, bias)


def _adamw(name, parts, w, m, v):
    nl, r, c = w.shape
    nparts = parts[0].shape[0]
    tr = r
    for cand in (256, 352, 128):
        if r % cand == 0 and r > cand:
            tr = cand
            break
    c1 = 1.0 - ADAM_B1 ** ADAM_STEP
    c2 = 1.0 - ADAM_B2 ** ADAM_STEP

    def body(*refs):
        p_refs = refs[:nl]
        w_ref, m_ref, v_ref, g_ref, d_ref, nm_ref, nv_ref = refs[nl:]
        layer = pl.program_id(0)
        for l in range(nl):
            @pl.when(layer == l)
            def _():
                g = p_refs[l][0].astype(F32)
                for dev in range(1, nparts):
                    g = g + p_refs[l][dev].astype(F32)
                nm = ADAM_B1 * m_ref[...] + (1.0 - ADAM_B1) * g
                nv = ADAM_B2 * v_ref[...] + (1.0 - ADAM_B2) * (g * g)
                m_hat = nm / c1
                v_hat = nv / c2
                g_ref[...] = g
                d_ref[...] = -ADAM_LR * (m_hat / (jnp.sqrt(v_hat) + ADAM_EPS) + ADAM_WD * w_ref[...])
                nm_ref[...] = nm
                nv_ref[...] = nv

    def part_spec(l):
        return pl.BlockSpec((nparts, tr, c), lambda layer, i: (0, jnp.where(layer == l, i, 0), 0))

    row = pl.BlockSpec((None, tr, c), lambda layer, i: (layer, i, 0))
    return _call(body, name, (nl, r // tr), [part_spec(l) for l in range(nl)] + [row, row, row],
                 [row] * 4, [_sds((nl, r, c), F32)] * 4)(*parts, w, m, v)


def _exchange(name, srcs, out_shapes, items):
    n_in = len(srcs)
    n_out = len(out_shapes)
    n_items = len(items)

    def body(*refs):
        ins = refs[:n_in]
        outs = refs[n_in:n_in + n_out]
        send_sems, recv_sems, local_sems = refs[n_in + n_out:]
        x, y, c = lax.axis_index("x"), lax.axis_index("y"), lax.axis_index("c")
        me = 4 * x + 2 * y + c

        def pick(ref, sel, peer):
            idx = tuple(peer if s == "peer" else me if s == "me" else s for s in sel)
            return ref.at[idx] if idx else ref

        copies = []
        for it, (si, ssel, oi, osel) in enumerate(items):
            local = pltpu.make_async_copy(pick(ins[si], ssel, me), pick(outs[oi], osel, me), local_sems.at[it])
            local.start()
            copies.append(local)
            for j in range(1, N_DEV):
                px = 1 - x if j & 4 else x
                py = 1 - y if j & 2 else y
                pc = 1 - c if j & 1 else c
                peer = 4 * px + 2 * py + pc
                sem = it * (N_DEV - 1) + j - 1
                cp = pltpu.make_async_remote_copy(
                    src_ref=pick(ins[si], ssel, peer), dst_ref=pick(outs[oi], osel, peer),
                    send_sem=send_sems.at[sem], recv_sem=recv_sems.at[sem],
                    device_id=(px, py, pc), device_id_type=pl.DeviceIdType.MESH)
                cp.start()
                copies.append(cp)
        for cp in copies:
            cp.wait()

    any_spec = pl.BlockSpec(memory_space=pl.ANY)
    return pl.pallas_call(
        body, name=name, in_specs=[any_spec] * n_in, out_specs=[any_spec] * n_out, out_shape=list(out_shapes),
        scratch_shapes=[pltpu.SemaphoreType.DMA((n_items * (N_DEV - 1),)),
                        pltpu.SemaphoreType.DMA((n_items * (N_DEV - 1),)),
                        pltpu.SemaphoreType.DMA((n_items,))],
        compiler_params=pltpu.CompilerParams(has_side_effects=True),
    )(*srcs)


def _sel(ref, sel, slot=None):
    idx = tuple(slot if s == "slot" else s for s in sel)
    return ref.at[idx] if idx else ref


def _gather_plan(srcs, out_shapes, items):
    per = N_DEV - 1
    n_items = len(items)

    def phases(ins, outs, send_sems, recv_sems, local_sems, total):
        x, y, c = lax.axis_index("x"), lax.axis_index("y"), lax.axis_index("c")
        sibling = (x, y, 1 - c)
        chips = [(1 - x, y), (x, 1 - y), (1 - x, 1 - y)]
        slot_of = lambda px, py, pc: 4 * px + 2 * py + pc

        def copy(it, k, block, to, src=None):
            _, _, oi, osel = items[it]
            dst = _sel(outs[oi], osel, slot_of(*block))
            return pltpu.make_async_remote_copy(
                src_ref=dst if src is None else src, dst_ref=dst,
                send_sem=send_sems.at[it * per + k], recv_sem=recv_sems.at[it * per + k],
                device_id=to, device_id_type=pl.DeviceIdType.MESH)

        def own_copies(it):
            si, ssel, oi, osel = items[it]
            src = _sel(ins[si], ssel)
            mine = pltpu.make_async_copy(src, _sel(outs[oi], osel, slot_of(x, y, c)), local_sems.at[it])
            first = [copy(it, 0, (x, y, c), sibling, src=src)]
            first += [copy(it, 1 + j, (x, y, c), (*chip, c), src=src) for j, chip in enumerate(chips)]
            return mine, first

        def start():
            for it in range(n_items):
                mine, first = own_copies(it)
                mine.start()
                for cp in first:
                    cp.start()

        def forward(it):
            def go():
                for j, chip in enumerate(chips):
                    copy(it, 1 + j, (*chip, c), (x, y, c)).wait_recv()
                    copy(it, 4 + j, (*chip, c), sibling).start()
            return go

        def finish():
            for it in range(n_items):
                copy(it, 0, sibling, (x, y, c)).wait_recv()
                for j, chip in enumerate(chips):
                    copy(it, 4 + j, (*chip, 1 - c), (x, y, c)).wait_recv()
            for it in range(n_items):
                mine, first = own_copies(it)
                for cp in first:
                    cp.wait_send()
                for j, chip in enumerate(chips):
                    copy(it, 4 + j, (*chip, c), sibling).wait_send()
                mine.wait()

        lo = max(total // 3, 1)
        acts = [(0, start)]
        for it in range(n_items):
            acts.append((min(lo + it * (total - 1 - lo) // n_items, total - 2), forward(it)))
        acts.append((total - 1, finish))
        return acts

    return _Plan(srcs, out_shapes, n_items * per, n_items, phases)


def _scatter_plan(srcs, out_shapes, items):
    per = N_DEV - 1
    n_items = len(items)

    def phases(ins, outs, send_sems, recv_sems, local_sems, total):
        x, y, c = lax.axis_index("x"), lax.axis_index("y"), lax.axis_index("c")
        me = 4 * x + 2 * y + c

        def copies():
            out = []
            for it, (si, oi, osel) in enumerate(items):
                dst = _sel(outs[oi], osel, me)
                out.append(pltpu.make_async_copy(ins[si].at[me], dst, local_sems.at[it]))
                for j in range(1, N_DEV):
                    px = 1 - x if j & 4 else x
                    py = 1 - y if j & 2 else y
                    pc = 1 - c if j & 1 else c
                    out.append(pltpu.make_async_remote_copy(
                        src_ref=ins[si].at[4 * px + 2 * py + pc], dst_ref=dst,
                        send_sem=send_sems.at[it * per + j - 1], recv_sem=recv_sems.at[it * per + j - 1],
                        device_id=(px, py, pc), device_id_type=pl.DeviceIdType.MESH))
            return out

        def start():
            for cp in copies():
                cp.start()

        def finish():
            for cp in copies():
                cp.wait()

        return [(0, start), (total - 1, finish)]

    return _Plan(srcs, out_shapes, n_items * per, n_items, phases)


def _sum_slots(parts):
    _, r, c = parts.shape

    def body(p_ref, o_ref):
        acc = p_ref[0]
        for dev in range(1, N_DEV):
            acc = acc + p_ref[dev]
        o_ref[...] = acc

    return _call(body, "sum_slots", (1,), [pl.BlockSpec((N_DEV, r, c), lambda i: (0, 0, 0))],
                 pl.BlockSpec((r, c), lambda i: (0, 0)), _sds((r, c), F32))(parts)


def _pack(arrays):
    rows = []
    for a in arrays:
        flat = a.reshape(-1).astype(F32)
        pad = (-flat.shape[0]) % 1024
        if pad:
            flat = jnp.concatenate([flat, jnp.zeros((pad,), F32)])
        rows.append(flat.reshape(-1, 128))
    return jnp.concatenate(rows, axis=0)


def _unpack(buf, shapes):
    out = []
    r0 = 0
    for shp in shapes:
        size = int(np.prod(shp))
        nrows = -(-size // 1024) * 8
        out.append(buf[r0:r0 + nrows].reshape(-1)[:size].reshape(shp))
        r0 += nrows
    return out


def _block_diag_chunks(val):
    eye = jnp.eye(8, dtype=val.dtype)
    return jnp.einsum("cjhp,jk->cjhkp", val, eye).reshape(SSM_CHUNKS, CHUNK_CH, CHUNK_ST)


def _diag_blocks(dense):
    d5 = dense.reshape(SSM_CHUNKS, 8, SSM_GROUP, 8, SSM_STATE)
    return jnp.stack([d5[:, j, :, j, :] for j in range(8)], axis=1)


def _carried(res, comm):
    return res if comm is not None else (res, None)


def _ffn_fwd(tag, hin, hin_b, w_up, conv_w, conv_b, w_down4, gain, bias, seq, comm_up=None, comm_down=None):
    hc, got_up = _carried(_mm_nn("ffn_up" + tag, hin_b, w_up, FFN_HIDDEN, blocked_out=True, comm=comm_up), comm_up)
    nb, n, c = hc.shape
    hc = hc.reshape(2, nb // 2, n, c)
    act = _conv_gate_fwd(hc, conv_w, conv_b, seq)
    (r, hout, hout_b), got_down = _carried(
        _mm_nn_red("ffn_down" + tag, act, w_down4, F32, comm=comm_down, residual=hin, norm=(gain, bias)), comm_down)
    return hout, hout_b, (hin_b, hc, act, r), (got_up, got_down)


def _ffn_bwd(tag, saved, dh, w_up, conv_w, conv_b, w_down4, gain, seq, comm_conv=None, comm_dw=None,
             comm_dx=None):
    hin_b, hc, act, r = saved
    dr, dr_b, dgain, dbias = _ln_bwd("ffn_norm_bwd" + tag, r, dh, gain)
    d_wdown = _mm_tn("ffn_down_dw" + tag, act, dr_b, act.shape[0], BF16, a_blocked=True, g_mode="shared")
    dact = _mm_nt_out("ffn_down_dx" + tag, dr_b, w_down4, FFN_HIDDEN)
    (dhc, dconv), got_conv = _carried(_conv_gate_bwd(hc, dact, conv_w, conv_b, seq, comm=comm_conv), comm_conv)
    dhc8 = dhc.reshape(2 * dhc.shape[1], dhc.shape[2], dhc.shape[3])
    d_wup, got_dw = _carried(_mm_tn("ffn_up_dw" + tag, hin_b, dhc8, dhc8.shape[0], BF16, g_mode="blocked",
                                    comm=comm_dw), comm_dw)
    dhin, got_dx = _carried(_mm_nt_red("ffn_up_dx" + tag, dhc8, w_up, F32, g_blocked=True, add=dr,
                                       add_scale=DN_ALPHA, comm=comm_dx), comm_dx)
    return dhin, d_wup, d_wdown, dconv, dgain, dbias, (got_conv, got_dw, got_dx)


def _local_step(x, target, p, wfull, bsz, seq, ex=None):
    n = bsz * seq
    xf = x.reshape(n, D_MODEL)
    tf = target.reshape(n, D_MODEL)

    lr = p["s5_lam_re"].reshape(1, N_STATE)
    li = p["s5_lam_im"].reshape(1, N_STATE)
    ldt = p["s5_log_dt"].reshape(1, D_MODEL // SSM_GROUP)
    br_t = p["s5_b_re"].reshape(N_STATE, SSM_GROUP).T
    bi_t = p["s5_b_im"].reshape(N_STATE, SSM_GROUP).T
    ab_r, ab_i, bb_r, bb_i, tab = _s5_prep(lr, li, ldt, br_t, bi_t)

    def bb_chunks(bb):
        return _block_diag_chunks(bb.reshape(SSM_GROUP, SSM_CHUNKS, 8, SSM_STATE).transpose(1, 2, 0, 3))

    def c_chunks(cc):
        return _block_diag_chunks(cc.reshape(SSM_CHUNKS, 8, SSM_GROUP, SSM_STATE))

    bbr_c, bbi_c = bb_chunks(bb_r).astype(BF16), bb_chunks(bb_i).astype(BF16)
    cr_c, ci_c = c_chunks(p["s5_c_re"]).astype(BF16), c_chunks(p["s5_c_im"]).astype(BF16)
    dvec = p["s5_d"].reshape(1, D_MODEL)

    if ex is None:
        y, h_r, h_i = _s5_scan_fwd(xf, bbr_c, bbi_c, cr_c, ci_c, dvec, tab, seq)
    else:
        (y, h_r, h_i), gathered = _s5_scan_fwd(xf, bbr_c, bbi_c, cr_c, ci_c, dvec, tab, seq,
                                               comm=ex.gather_first())
        p, wfull = ex.take_first(gathered, p)
    ln_g = p["ln_gain"].reshape(4, 1, D_MODEL)
    ln_b = p["ln_bias"].reshape(4, 1, D_MODEL)
    y2, z, gg = _glu_fwd(y, wfull["s5_w_glu"], p["s5_b_glu"].reshape(1, D_MODEL))
    r1, h1, h1b = _mm_nn("s5_out", gg, wfull["s5_w_out"][None], F32, residual=xf, norm=(ln_g[0], ln_b[0]))
    h2, h2b, ffn0_saved, (got_up, got_down) = _ffn_fwd(
        "0", h1, h1b, wfull["ffn_w_up"][0], p["conv_w"][0], p["conv_b"][0], wfull["ffn_w_down"][0],
        ln_g[1], ln_b[1], seq, comm_up=ex.gather_layer1("up", 0) if ex else None,
        comm_down=ex.gather_layer1("down", 0) if ex else None)
    halves = {"up": [got_up[0]], "down": [got_down[0]]} if ex else None

    kmat = _mm_nn("attn_k", h2b, wfull["attn_w_kv"], BF16, blocks=(0, 4))
    vmat = _mm_nn("attn_v", h2b, wfull["attn_w_kv"], BF16, blocks=(4, 4))
    qmat = _mm_nn("attn_q", h2b, wfull["attn_w_q"], BF16)
    biases, outs, lses = [], [], []
    for g, dil in enumerate(DILATIONS):
        rel_t = p["rel_bias"][:, g * N_HEADS:(g + 1) * N_HEADS].T
        biases.append(_bias_expand(rel_t, dil).reshape(N_HEADS, BAND, 2 * BAND))
        which = {4: "down", 16: "up"}.get(dil) if ex else None
        plan = ex.gather_layer1(which, 1) if which else None
        (o_g, l_g), got = _carried(_attn_fwd(qmat, kmat, vmat, biases[g], g, dil, bsz, seq, comm=plan), plan)
        if plan is not None:
            halves[which].append(got[0])
        outs.append(o_g)
        lses.append(l_g)
    if ex is not None:
        wfull = ex.take_layer1(halves, wfull)
    o, lse = _attn_combine(outs, lses)
    r3, h3, h3b = _mm_nn("attn_out", o, wfull["attn_w_out"][None], F32, residual=h2, norm=(ln_g[2], ln_b[2]))
    h4, _, ffn1_saved, _ = _ffn_fwd("1", h3, h3b, wfull["ffn_w_up"][1], p["conv_w"][1], p["conv_b"][1],
                                    wfull["ffn_w_down"][1], ln_g[3], ln_b[3], seq)
    loss_tile, dh4 = _loss_head(h4, tf)

    grads = {}
    dh3, grads["ffn_w_up1"], grads["ffn_w_down1"], dconv1, dg3, db3, _ = _ffn_bwd(
        "1", ffn1_saved, dh4, wfull["ffn_w_up"][1], p["conv_w"][1], p["conv_b"][1], wfull["ffn_w_down"][1],
        ln_g[3], seq)
    dr3, dr3b, dg2, db2 = _ln_bwd("attn_norm_bwd", r3, dh3, ln_g[2])
    grads["attn_w_out"] = _mm_tn("attn_out_dw", o, dr3b, 1, BF16, g_mode="shared")
    do = _mm_nt_red("attn_out_dx", dr3b, wfull["attn_w_out"][None], F32, g_blocked=False)
    dqkv, drel = (), []
    for g, dil in enumerate(DILATIONS):
        plan = ex.scatter(grads, ("ffn_w_up1", "ffn_w_down1")) if ex and dil == DILATIONS[-1] else None
        res, got = _carried(_attn_bwd(qmat, kmat, vmat, do, o, lse, biases[g], g, dil, bsz, seq, comm=plan,
                                      into=dqkv), plan)
        if plan is not None:
            ex.keep(plan, got)
        dqkv, db_g = tuple(res[:3]), res[3]
        drel.append(_bias_reduce(db_g.reshape(N_HEADS, BAND * 2 * BAND), dil).T)
    dq, dk, dv = dqkv
    grads["rel_bias"] = jnp.concatenate(drel, axis=1)
    grads["attn_w_q"] = _mm_tn("attn_q_dw", h2b, dq, N_DEV, BF16)
    grads["attn_w_kv"] = jnp.concatenate([_mm_tn("attn_k_dw", h2b, dk, 4, BF16),
                                          _mm_tn("attn_v_dw", h2b, dv, 4, BF16)], axis=0)
    dh2 = _mm_nt_red("attn_q_dx", dq, wfull["attn_w_q"], F32, g_blocked=False, add=dr3, add_scale=DN_ALPHA)
    dh2 = _mm_nt_red("attn_k_dx", dk, wfull["attn_w_kv"], F32, g_blocked=False, add=dh2, blocks=(0, 4))
    dh2 = _mm_nt_red("attn_v_dx", dv, wfull["attn_w_kv"], F32, g_blocked=False, add=dh2, blocks=(4, 4))

    plans = [ex.scatter(grads, names) if ex else None
             for names in (("attn_w_kv",), ("attn_w_out",), ("attn_w_q",))]
    dh1, grads["ffn_w_up0"], grads["ffn_w_down0"], dconv0, dg1, db1, got = _ffn_bwd(
        "0", ffn0_saved, dh2, wfull["ffn_w_up"][0], p["conv_w"][0], p["conv_b"][0], wfull["ffn_w_down"][0],
        ln_g[1], seq, comm_conv=plans[0], comm_dw=plans[1], comm_dx=plans[2])
    if ex is not None:
        for plan, outs in zip(plans, got):
            ex.keep(plan, outs)
    dr1, dr1b, dg0, db0 = _ln_bwd("s5_norm_bwd", r1, dh1, ln_g[0])
    grads["s5_w_out"] = _mm_tn("s5_out_dw", gg, dr1b, 1, BF16, g_mode="shared")
    dgg = _mm_nt_red("s5_out_dx", dr1b, wfull["s5_w_out"][None], F32, g_blocked=False)
    dz, d_bglu, dy2_direct = _glu_bwd_gate(y, z, dgg)
    grads["s5_w_glu"] = _mm_tn("s5_glu_dw", y2, dz, 1, BF16, g_mode="shared")
    dy2 = _mm_nt_red("s5_glu_dx", dz, wfull["s5_w_glu"][None], F32, g_blocked=False, add=dy2_direct)
    dy = _gelu_bwd(y, dy2)
    plan_last = ex.scatter(grads, ("ffn_w_up0", "ffn_w_down0", "s5_w_out", "s5_w_glu")) if ex else None
    res = _s5_scan_bwd(xf, dy, dr1, h_r, h_i, bbr_c, bbi_c, cr_c, ci_c, dvec, tab, seq, comm=plan_last)
    if ex is not None:
        res, got = res
        ex.keep(plan_last, got)
    dx, dcr, dci, dbr, dbi, dar, dai, dd = res

    def bb_from_chunks(dense):
        return _diag_blocks(dense).transpose(2, 0, 1, 3).reshape(SSM_GROUP, N_STATE)

    d_lr, d_li, d_ldt, d_br, d_bi = _s5_prep_bwd(lr, li, ldt, br_t, bi_t, dar, dai,
                                                 bb_from_chunks(dbr), bb_from_chunks(dbi))
    grads["s5_lam_re"] = d_lr.reshape(p["s5_lam_re"].shape)
    grads["s5_lam_im"] = d_li.reshape(p["s5_lam_im"].shape)
    grads["s5_log_dt"] = d_ldt.reshape(p["s5_log_dt"].shape)
    grads["s5_b_re"] = d_br.T.reshape(p["s5_b_re"].shape)
    grads["s5_b_im"] = d_bi.T.reshape(p["s5_b_im"].shape)
    grads["s5_c_re"] = _diag_blocks(dcr).reshape(p["s5_c_re"].shape)
    grads["s5_c_im"] = -_diag_blocks(dci).reshape(p["s5_c_im"].shape)
    grads["s5_d"] = dd.reshape(p["s5_d"].shape)
    grads["s5_b_glu"] = d_bglu.reshape(1, D_MODEL)
    dconv = jnp.stack([dconv0, dconv1]).reshape(2, N_DEV, 8, -1)
    grads["ffn_conv_w"] = dconv[:, :, 0:3, :].transpose(0, 2, 1, 3)
    grads["ffn_conv_b"] = dconv[:, :, 3, :]
    grads["ln_gain"] = jnp.stack([dg0, dg1, dg2, dg3]).reshape(2, 2, D_MODEL)
    grads["ln_bias"] = jnp.stack([db0, db1, db2, db3]).reshape(2, 2, D_MODEL)
    return loss_tile[0, 0], dx.reshape(x.shape), grads


SMALL_REPLICATED = ("s5_lam_re", "s5_lam_im", "s5_log_dt", "s5_b_re", "s5_b_im", "s5_c_re", "s5_c_im", "s5_d",
                    "rel_bias", "ffn_conv_b")
SMALL_SHARDED = ("s5_b_glu", "ffn_conv_w", "ln_gain", "ln_bias")
BIG = ("s5_w_glu", "s5_w_out", "attn_w_kv", "attn_w_q", "attn_w_out", "ffn_w_up", "ffn_w_down")
WEIGHTS = ("s5_lam_re", "s5_lam_im", "s5_log_dt", "s5_b_re", "s5_b_im", "s5_c_re", "s5_c_im", "s5_d", "s5_w_glu",
           "s5_b_glu", "s5_w_out", "attn_w_kv", "attn_w_q", "attn_w_out", "rel_bias", "ffn_w_up", "ffn_conv_w",
           "ffn_conv_b", "ffn_w_down", "ln_gain", "ln_bias")


class _Exchanges:
    def __init__(self, w):
        self.w = w
        self.parts = {}
        self.up = w["ffn_w_up"].astype(BF16)
        self.down = w["ffn_w_down"].astype(BF16)

    def gather_first(self):
        w = self.w
        srcs = [w["s5_w_glu"][0].astype(BF16), w["s5_w_out"][0].astype(BF16),
                _pack([w[k] for k in SMALL_SHARDED]), self.up, self.down, w["attn_w_kv"].astype(BF16),
                w["attn_w_q"][0].astype(BF16), w["attn_w_out"][0].astype(BF16)]
        outs = [_sds((N_DEV,) + (s.shape[1:] if i in (3, 4) else s.shape), s.dtype) for i, s in enumerate(srcs)]
        items = [(i, (0,) if i in (3, 4) else (), i, ("slot",)) for i in range(len(srcs))]
        return _gather_plan(srcs, outs, items)

    def take_first(self, g, p):
        d = D_MODEL
        w = self.w
        wfull = {
            "s5_w_glu": g[0].reshape(d, d),
            "s5_w_out": g[1].reshape(d, d),
            "ffn_w_up": [g[3], None],
            "ffn_w_down": [g[4].reshape(4, -1, d), None],
            "attn_w_kv": g[5],
            "attn_w_q": g[6],
            "attn_w_out": g[7].reshape(d, d),
        }
        smalls = [_unpack(g[2][dev], [w[k].shape for k in SMALL_SHARDED]) for dev in range(N_DEV)]
        c_ff = w["ffn_conv_w"].shape[2]
        conv_w = jnp.stack([s[1] for s in smalls], axis=2)
        p = dict(p)
        p.update(s5_b_glu=jnp.concatenate([s[0] for s in smalls], axis=1),
                 ln_gain=jnp.concatenate([s[2] for s in smalls], axis=2),
                 ln_bias=jnp.concatenate([s[3] for s in smalls], axis=2),
                 conv_w=conv_w.transpose(0, 2, 1, 3).reshape(2, 2, 4, 3, c_ff),
                 conv_b=w["ffn_conv_b"].reshape(2, 2, 4, 1, c_ff))
        return p, wfull

    def gather_layer1(self, which, half):
        src = self.up if which == "up" else self.down
        rows = src.shape[1] // 2
        return _gather_plan([src], [_sds((N_DEV, rows) + src.shape[2:], BF16)],
                            [(0, (1, pl.ds(half * rows, rows)), 0, ("slot",))])

    def take_layer1(self, halves, wfull):
        up = jnp.concatenate(halves["up"], axis=1)
        down = jnp.concatenate(halves["down"], axis=1)
        wfull = dict(wfull)
        wfull["ffn_w_up"] = [wfull["ffn_w_up"][0], up]
        wfull["ffn_w_down"] = [wfull["ffn_w_down"][0], down.reshape(4, -1, D_MODEL)]
        return wfull

    def scatter(self, grads, names):
        srcs = [grads[k].reshape(N_DEV, -1, grads[k].shape[-1]) for k in names]
        plan = _scatter_plan(srcs, [_sds(s.shape, BF16) for s in srcs],
                             [(i, i, ("slot",)) for i in range(len(names))])
        plan.names = names
        return plan

    def keep(self, plan, outs):
        for k, o in zip(plan.names, outs):
            self.parts[k] = o


def kernel(x, s5_lam_re, s5_lam_im, s5_log_dt, s5_b_re, s5_b_im, s5_c_re, s5_c_im, s5_d, s5_w_glu, s5_b_glu, s5_w_out, attn_w_kv, attn_w_q, attn_w_out, rel_bias, ffn_w_up, ffn_conv_w, ffn_conv_b, ffn_w_down, ln_gain, ln_bias, loss_target, m_s5_lam_re, m_s5_lam_im, m_s5_log_dt, m_s5_b_re, m_s5_b_im, m_s5_c_re, m_s5_c_im, m_s5_d, m_s5_w_glu, m_s5_b_glu, m_s5_w_out, m_attn_w_kv, m_attn_w_q, m_attn_w_out, m_rel_bias, m_ffn_w_up, m_ffn_conv_w, m_ffn_conv_b, m_ffn_w_down, m_ln_gain, m_ln_bias, v_s5_lam_re, v_s5_lam_im, v_s5_log_dt, v_s5_b_re, v_s5_b_im, v_s5_c_re, v_s5_c_im, v_s5_d, v_s5_w_glu, v_s5_b_glu, v_s5_w_out, v_attn_w_kv, v_attn_w_q, v_attn_w_out, v_rel_bias, v_ffn_w_up, v_ffn_conv_w, v_ffn_conv_b, v_ffn_w_down, v_ln_gain, v_ln_bias):
    w = dict(s5_lam_re=s5_lam_re, s5_lam_im=s5_lam_im, s5_log_dt=s5_log_dt, s5_b_re=s5_b_re, s5_b_im=s5_b_im,
             s5_c_re=s5_c_re, s5_c_im=s5_c_im, s5_d=s5_d, s5_w_glu=s5_w_glu, s5_b_glu=s5_b_glu, s5_w_out=s5_w_out,
             attn_w_kv=attn_w_kv, attn_w_q=attn_w_q, attn_w_out=attn_w_out, rel_bias=rel_bias, ffn_w_up=ffn_w_up,
             ffn_conv_w=ffn_conv_w, ffn_conv_b=ffn_conv_b, ffn_w_down=ffn_w_down, ln_gain=ln_gain, ln_bias=ln_bias)
    mom = dict(s5_lam_re=m_s5_lam_re, s5_lam_im=m_s5_lam_im, s5_log_dt=m_s5_log_dt, s5_b_re=m_s5_b_re,
               s5_b_im=m_s5_b_im, s5_c_re=m_s5_c_re, s5_c_im=m_s5_c_im, s5_d=m_s5_d, s5_w_glu=m_s5_w_glu,
               s5_b_glu=m_s5_b_glu, s5_w_out=m_s5_w_out, attn_w_kv=m_attn_w_kv, attn_w_q=m_attn_w_q,
               attn_w_out=m_attn_w_out, rel_bias=m_rel_bias, ffn_w_up=m_ffn_w_up, ffn_conv_w=m_ffn_conv_w,
               ffn_conv_b=m_ffn_conv_b, ffn_w_down=m_ffn_w_down, ln_gain=m_ln_gain, ln_bias=m_ln_bias)
    var = dict(s5_lam_re=v_s5_lam_re, s5_lam_im=v_s5_lam_im, s5_log_dt=v_s5_log_dt, s5_b_re=v_s5_b_re,
               s5_b_im=v_s5_b_im, s5_c_re=v_s5_c_re, s5_c_im=v_s5_c_im, s5_d=v_s5_d, s5_w_glu=v_s5_w_glu,
               s5_b_glu=v_s5_b_glu, s5_w_out=v_s5_w_out, attn_w_kv=v_attn_w_kv, attn_w_q=v_attn_w_q,
               attn_w_out=v_attn_w_out, rel_bias=v_rel_bias, ffn_w_up=v_ffn_w_up, ffn_conv_w=v_ffn_conv_w,
               ffn_conv_b=v_ffn_conv_b, ffn_w_down=v_ffn_w_down, ln_gain=v_ln_gain, ln_bias=v_ln_bias)
    bsz, seq, _ = x.shape
    me = 4 * lax.axis_index("x") + 2 * lax.axis_index("y") + lax.axis_index("c")

    ex = _Exchanges(w)
    loss_part, grad_x, g = _local_step(x, loss_target, dict(w), None, bsz, seq, ex=ex)
    loss = lax.psum(loss_part, ("x", "y", "c"))

    d = D_MODEL
    results = {}
    for name in BIG:
        shard = w[name]
        if name in ("ffn_w_up", "ffn_w_down"):
            parts, r3 = [ex.parts[name + "0"], ex.parts[name + "1"]], shard.shape
        else:
            parts, r3 = [ex.parts[name]], (1, -1, shard.shape[-1])
        outs4 = _adamw("adamw_" + name, parts, shard.reshape(r3), mom[name].reshape(r3), var[name].reshape(r3))
        results[name] = tuple(o.reshape(shard.shape) for o in outs4)

    small_names = SMALL_REPLICATED + SMALL_SHARDED
    packed = _pack([g[k] for k in small_names])
    pad = (-packed.shape[0]) % (8 * N_DEV)
    if pad:
        packed = jnp.concatenate([packed, jnp.zeros((pad, 128), F32)], axis=0)
    piece = packed.shape[0] // N_DEV
    slots = _exchange("reduce_small", [packed.reshape(N_DEV, piece, 128)], [_sds((N_DEV, piece, 128), F32)],
                      [(0, ("peer",), 0, ("me",))])[0]
    summed = _exchange("gather_small", [_sum_slots(slots)], [_sds((N_DEV, piece, 128), F32)],
                       [(0, (), 0, ("me",))])[0]
    total = _unpack(summed.reshape(N_DEV * piece, 128), [g[k].shape for k in small_names])
    gsum = dict(zip(small_names, total))
    mine = {k: gsum[k] for k in SMALL_REPLICATED}
    mine["s5_b_glu"] = lax.dynamic_slice_in_dim(gsum["s5_b_glu"], me * (d // N_DEV), d // N_DEV, axis=1)
    mine["ffn_conv_w"] = lax.dynamic_index_in_dim(gsum["ffn_conv_w"], me, axis=2, keepdims=False)
    mine["ln_gain"] = lax.dynamic_slice_in_dim(gsum["ln_gain"], me * (d // N_DEV), d // N_DEV, axis=2)
    mine["ln_bias"] = lax.dynamic_slice_in_dim(gsum["ln_bias"], me * (d // N_DEV), d // N_DEV, axis=2)
    gp = _pack([mine[k] for k in small_names])
    outs4 = _adamw("adamw_small", [gp[None]], _pack([w[k] for k in small_names])[None],
                   _pack([mom[k] for k in small_names])[None], _pack([var[k] for k in small_names])[None])
    outs4 = [o[0] for o in outs4]
    shapes = [w[k].shape for k in small_names]
    unpacked = [_unpack(o, shapes) for o in outs4]
    for i, k in enumerate(small_names):
        results[k] = tuple(unpacked[j][i] for j in range(4))

    out = [loss, grad_x]
    for j in range(4):
        out.extend(results[k][j] for k in WEIGHTS)
    return tuple(out)
```

```python
import functools
import math

import numpy as np
import jax
import jax.numpy as jnp
from jax import lax
from jax.experimental import pallas as pl
from jax.experimental.pallas import tpu as pltpu

F32 = jnp.float32
BF16 = jnp.bfloat16
HIGHEST = lax.Precision.HIGHEST

N_DEV = 8
D_MODEL = 1024
SSM_GROUP = 16
SSM_STATE = 64
SSM_CHUNKS = 8
CHUNK_CH = D_MODEL // SSM_CHUNKS
CHUNK_ST = CHUNK_CH // SSM_GROUP * SSM_STATE
N_STATE = D_MODEL // SSM_GROUP * SSM_STATE
HEAD_DIM = 64
N_HEADS = 16
BAND = 128
DILATIONS = (1, 4, 16)
REL_BUCKETS = 32
REL_MAX_DIST = 2048
NEG_BIG = -1e30
DN_ALPHA = (2.0 * 2) ** 0.25
LN_EPS = 1e-5
ADAM_LR = 0.001
ADAM_B1 = 0.9
ADAM_B2 = 0.999
ADAM_EPS = 1e-08
ADAM_WD = 0.01
ADAM_STEP = 10

TM = 512
TM_MM = 2048
TM_MM_F32 = 1024
T_SCAN = 512
VMEM_LIMIT = 56 * 1024 * 1024
FFN_HIDDEN = BF16

NN = (((1,), (0,)), ((), ()))
NT = (((1,), (1,)), ((), ()))
TN = (((0,), (0,)), ((), ()))


class _Plan:
    def __init__(self, srcs, out_shapes, n_sems, n_local, phases):
        self.srcs, self.out_shapes, self.n_sems, self.n_local, self.phases = srcs, out_shapes, n_sems, n_local, phases


def _call(body, name, grid, in_specs, out_specs, out_shape, scratch=(), comm=None, fill=()):
    params = dict(dimension_semantics=("arbitrary",) * len(grid), vmem_limit_bytes=VMEM_LIMIT)
    if comm is None and not fill:
        return pl.pallas_call(
            body, name=name, grid=grid, in_specs=in_specs, out_specs=out_specs, out_shape=out_shape,
            scratch_shapes=list(scratch), compiler_params=pltpu.CompilerParams(**params))
    single = not isinstance(out_specs, (list, tuple))
    o_specs = [out_specs] if single else list(out_specs)
    o_shape = [out_shape] if single else list(out_shape)
    n_in, n_out, n_scr, n_fill = len(in_specs), len(o_specs), len(scratch), len(fill)
    srcs = list(comm.srcs) if comm else []
    c_shapes = list(comm.out_shapes) if comm else []
    n_cin, n_cout = len(srcs), len(c_shapes)
    total = int(np.prod(grid))

    def wrapped(*refs):
        ins = refs[:n_in]
        refs = refs[n_in + n_fill:]
        cins, outs, couts = refs[:n_cin], refs[n_cin:n_cin + n_out], refs[n_cin + n_out:n_cin + n_out + n_cout]
        rest = refs[n_cin + n_out + n_cout:]
        scr, sems = rest[:n_scr], rest[n_scr:]
        if comm is None:
            body(*ins, *outs, *scr)
            return
        step = pl.program_id(0)
        for ax in range(1, len(grid)):
            step = step * grid[ax] + pl.program_id(ax)
        phases = comm.phases(cins, couts, *sems, total)
        for at, action in phases:
            if at == 0:
                pl.when(step == 0)(action)
        body(*ins, *outs, *scr)
        for at, action in phases:
            if at > 0:
                pl.when(step == at)(action)

    any_spec = pl.BlockSpec(memory_space=pl.ANY)
    sems = [] if comm is None else [pltpu.SemaphoreType.DMA((comm.n_sems,)), pltpu.SemaphoreType.DMA((comm.n_sems,)),
                                    pltpu.SemaphoreType.DMA((max(comm.n_local, 1),))]
    call = pl.pallas_call(
        wrapped, name=name, grid=grid, in_specs=list(in_specs) + [any_spec] * (n_fill + n_cin),
        out_specs=o_specs + [any_spec] * n_cout, out_shape=o_shape + c_shapes,
        scratch_shapes=list(scratch) + sems, input_output_aliases={n_in + j: j for j in range(n_fill)},
        compiler_params=pltpu.CompilerParams(has_side_effects=comm is not None, **params))

    def run(*args):
        res = call(*args, *fill, *srcs)
        own = res[0] if single else res[:n_out]
        return (own, res[n_out:]) if comm is not None else own

    return run


def _sds(shape, dtype):
    return jax.ShapeDtypeStruct(shape, dtype)


def _mm(name, a, b, *, dims, grid, a_spec, b_spec, o_spec, out_shape, nred=1, red_axis=0,
        add=None, add_spec=None, add_scale=1.0, comm=None, norm=None, inner=None, post=None):
    has_add = add is not None
    n_extra = (1 if has_add else 0) + (1 if post else 0) + (2 if norm else 0)
    n_out = 3 if norm else 1
    acc_shape = tuple(s for s in o_spec.block_shape if s is not None)

    def body(*refs):
        a_ref, b_ref = refs[:2]
        extra = refs[2:2 + n_extra]
        outs = refs[2 + n_extra:2 + n_extra + n_out]
        rest = refs[2 + n_extra + n_out:]
        o_ref = outs[0]
        if inner is None:
            prod = lax.dot_general(a_ref[...].astype(BF16), b_ref[...].astype(BF16), dims,
                                   preferred_element_type=F32)
        else:
            nb, a_mode = inner
            width = a_ref.shape[-1] // nb
            prod = None
            for d in range(nb):
                a_d = a_ref[d] if a_mode == "lead" else a_ref[:, d * width:(d + 1) * width]
                part = lax.dot_general(a_d.astype(BF16), b_ref[d].astype(BF16), dims, preferred_element_type=F32)
                prod = part if prod is None else prod + part

        def finish(val):
            if has_add:
                val = val + add_scale * extra[0][...].astype(F32)
            if post:
                val = post[0](val, extra[1 if has_add else 0][...])
            o_ref[...] = val.astype(o_ref.dtype)
            if norm:
                xhat, _ = _ln_stats(val)
                y = xhat * extra[-2][...] + extra[-1][...]
                outs[1][...] = y
                outs[2][...] = y.astype(BF16)

        if nred == 1:
            finish(prod)
        else:
            acc = rest[0]
            k = pl.program_id(red_axis)

            @pl.when(k == 0)
            def _():
                acc[...] = prod

            @pl.when(k > 0)
            def _():
                acc[...] += prod

            @pl.when(k == nred - 1)
            def _():
                finish(acc[...])

    in_specs = [a_spec, b_spec] + ([add_spec] if has_add else [])
    scratch = [pltpu.VMEM(acc_shape, F32)] if nred > 1 else []
    args = (a, b) + ((add,) if has_add else ())
    if post:
        in_specs.append(o_spec)
        args += (post[1],)
    if norm:
        vec = pl.BlockSpec((1, out_shape.shape[1]), lambda *_: (0, 0))
        in_specs += [vec, vec]
        args += tuple(norm)
        o_spec = [o_spec, o_spec, o_spec]
        out_shape = [out_shape, out_shape, _sds(out_shape.shape, BF16)]
    return _call(body, name, grid, in_specs, o_spec, out_shape, scratch, comm=comm)(*args)


def _mm_nn(name, a, b, out_dtype, blocked_out=False, comm=None, blocks=None, residual=None, norm=None):
    TM = TM_MM if out_dtype == BF16 else TM_MM_F32
    m, k = a.shape
    _, _, n = b.shape
    nb = b.shape[0] if blocks is None else blocks[1]
    first = 0 if blocks is None else blocks[0]
    if blocked_out:
        o_spec = pl.BlockSpec((None, TM, n), lambda d, i: (d, i, 0))
        out_shape = _sds((nb, m, n), out_dtype)
    else:
        o_spec = pl.BlockSpec((TM, n), lambda d, i: (i, d))
        out_shape = _sds((m, nb * n), out_dtype)
    return _mm(name, a, b, dims=NN, grid=(nb, m // TM),
               a_spec=pl.BlockSpec((TM, k), lambda d, i: (i, 0)),
               b_spec=pl.BlockSpec((None, k, n), lambda d, i: (d + first, 0, 0)),
               o_spec=o_spec, out_shape=out_shape, comm=comm, norm=norm,
               add=residual, add_spec=o_spec if residual is not None else None, add_scale=DN_ALPHA)


def _mm_nn_red(name, a, b, out_dtype, comm=None, residual=None, norm=None):
    nb, m, k = a.shape
    n = b.shape[2]
    o_spec = pl.BlockSpec((TM, n), lambda i: (i, 0))
    return _mm(name, a, b, dims=NN, grid=(m // TM,), inner=(nb, "lead"),
               a_spec=pl.BlockSpec((nb, TM, k), lambda i: (0, i, 0)),
               b_spec=pl.BlockSpec((nb, k, n), lambda i: (0, 0, 0)),
               o_spec=o_spec, out_shape=_sds((m, n), out_dtype), comm=comm, norm=norm,
               add=residual, add_spec=o_spec if residual is not None else None, add_scale=DN_ALPHA)


def _mm_tn(name, a, g, nb, out_dtype, a_blocked=False, g_mode="cols", comm=None):
    TM = TM_MM
    if a_blocked:
        _, m, ka = a.shape
        a_spec = pl.BlockSpec((None, TM, ka), lambda d, i: (d, i, 0))
    else:
        m, ka = a.shape
        a_spec = pl.BlockSpec((TM, ka), lambda d, i: (i, 0))
    if g_mode == "cols":
        n = g.shape[1] // nb
        g_spec = pl.BlockSpec((TM, n), lambda d, i: (i, d))
    elif g_mode == "blocked":
        n = g.shape[2]
        g_spec = pl.BlockSpec((None, TM, n), lambda d, i: (d, i, 0))
    else:
        n = g.shape[1]
        g_spec = pl.BlockSpec((TM, n), lambda d, i: (i, 0))
    nred = m // TM
    return _mm(name, a, g, dims=TN, grid=(nb, nred), nred=nred, red_axis=1,
               a_spec=a_spec, b_spec=g_spec,
               o_spec=pl.BlockSpec((None, ka, n), lambda d, i: (d, 0, 0)),
               out_shape=_sds((nb, ka, n), out_dtype), comm=comm)


def _mm_nt_red(name, g, w, out_dtype, g_blocked, add=None, add_scale=1.0, comm=None, blocks=None, post=None):
    _, k, n = w.shape
    nb = w.shape[0] if blocks is None else blocks[1]
    first = 0 if blocks is None else blocks[0]
    tm = TM_MM_F32 if nb * k * n <= 4 * 1024 * 768 else TM
    if g_blocked:
        m = g.shape[1]
        g_spec = pl.BlockSpec((nb, tm, n), lambda i: (0, i, 0))
    else:
        m = g.shape[0]
        g_spec = pl.BlockSpec((tm, nb * n), lambda i: (i, 0))
    o_spec = pl.BlockSpec((tm, k), lambda i: (i, 0))
    return _mm(name, g, w, dims=NT, grid=(m // tm,), inner=(nb, "lead" if g_blocked else "cols"),
               a_spec=g_spec, b_spec=pl.BlockSpec((nb, k, n), lambda i: (first // nb, 0, 0)),
               o_spec=o_spec, out_shape=_sds((m, k), out_dtype),
               add=add, add_spec=o_spec if add is not None else None, add_scale=add_scale, comm=comm, post=post)


def _mm_nt_out(name, g, w, out_dtype):
    TM = TM_MM
    m, n = g.shape
    nb, k, _ = w.shape
    return _mm(name, g, w, dims=NT, grid=(nb, m // TM),
               a_spec=pl.BlockSpec((TM, n), lambda d, i: (i, 0)),
               b_spec=pl.BlockSpec((None, k, n), lambda d, i: (d, 0, 0)),
               o_spec=pl.BlockSpec((None, TM, k), lambda d, i: (d, i, 0)),
               out_shape=_sds((nb, m, k), out_dtype))


def _ln_stats(r):
    mu = jnp.mean(r, axis=-1, keepdims=True)
    xc = r - mu
    var = jnp.mean(xc * xc, axis=-1, keepdims=True)
    rstd = lax.rsqrt(var + LN_EPS)
    return xc * rstd, rstd


def _ln_bwd(name, r, dy, gain):
    n, d = r.shape

    def body(r_ref, dy_ref, g_ref, dr_ref, drb_ref, dg_ref, db_ref):
        i = pl.program_id(0)
        xhat, rstd = _ln_stats(r_ref[...])
        dy_v = dy_ref[...]
        dxh = dy_v * g_ref[...]
        m1 = jnp.mean(dxh, axis=-1, keepdims=True)
        m2 = jnp.mean(dxh * xhat, axis=-1, keepdims=True)
        dr = rstd * (dxh - m1 - xhat * m2)
        dr_ref[...] = dr
        drb_ref[...] = dr.astype(BF16)
        dg = jnp.sum(dy_v * xhat, axis=0, keepdims=True)
        db = jnp.sum(dy_v, axis=0, keepdims=True)

        @pl.when(i == 0)
        def _():
            dg_ref[...] = dg
            db_ref[...] = db

        @pl.when(i > 0)
        def _():
            dg_ref[...] += dg
            db_ref[...] += db

    row = pl.BlockSpec((TM, d), lambda i: (i, 0))
    vec = pl.BlockSpec((1, d), lambda i: (0, 0))
    return _call(body, name, (n // TM,), [row, row, vec], [row, row, vec, vec],
                 [_sds((n, d), F32), _sds((n, d), BF16), _sds((1, d), F32), _sds((1, d), F32)])(r, dy, gain)


def _loss_head(y, target):
    n, d = y.shape

    def body(y_ref, t_ref, l_ref, dy_ref):
        i = pl.program_id(0)
        e = y_ref[...] - t_ref[...]
        dy_ref[...] = e * (1.0 / d)
        part = jnp.sum(jnp.sum(e * e, axis=1, keepdims=True), axis=0, keepdims=True) * (0.5 / d)
        part = jnp.broadcast_to(part, (8, 128))

        @pl.when(i == 0)
        def _():
            l_ref[...] = part

        @pl.when(i > 0)
        def _():
            l_ref[...] += part

    row = pl.BlockSpec((TM, d), lambda i: (i, 0))
    return _call(body, "loss_head", (n // TM,), [row, row],
                 [pl.BlockSpec((8, 128), lambda i: (0, 0)), row],
                 [_sds((8, 128), F32), _sds((n, d), F32)])(y, target)


def _group_expand_matrix():
    e = np.zeros((D_MODEL // SSM_GROUP, N_STATE), np.float32)
    for g in range(D_MODEL // SSM_GROUP):
        e[g, g * SSM_STATE:(g + 1) * SSM_STATE] = 1.0
    return jnp.asarray(e)


def _discretise(lr, li, ldt, br, bi, expand):
    dt = jnp.dot(jnp.exp(ldt), expand, precision=HIGHEST, preferred_element_type=F32)
    mag = jnp.exp(lr * dt)
    th = li * dt
    ab_r = mag * jnp.cos(th)
    ab_i = mag * jnp.sin(th)
    den = lr * lr + li * li
    nr = ab_r - 1.0
    co_r = (nr * lr + ab_i * li) / den
    co_i = (ab_i * lr - nr * li) / den
    bb_r = co_r * br - co_i * bi
    bb_i = co_r * bi + co_i * br
    return ab_r, ab_i, bb_r, bb_i


def _cmul(ar, ai, br, bi):
    return ar * br - ai * bi, ar * bi + ai * br


def _s5_prep(lr, li, ldt, br, bi):
    s = N_STATE
    expand = _group_expand_matrix()

    def body(lr_ref, li_ref, ldt_ref, br_ref, bi_ref, e_ref, abr_ref, abi_ref, bbr_ref, bbi_ref, tab_ref):
        ab_r, ab_i, bb_r, bb_i = _discretise(lr_ref[...], li_ref[...], ldt_ref[...], br_ref[...],
                                             bi_ref[...], e_ref[...])
        abr_ref[...] = ab_r
        abi_ref[...] = ab_i
        bbr_ref[...] = bb_r
        bbi_ref[...] = bb_i
        row = lax.broadcasted_iota(jnp.int32, (8, s), 0)
        for base, sign in ((0, 1.0), (8, -1.0)):
            p = [None] * 9
            p[1] = (ab_r, sign * ab_i)
            p[2] = _cmul(*p[1], *p[1])
            p[3] = _cmul(*p[2], *p[1])
            p[4] = _cmul(*p[2], *p[2])
            p[5] = _cmul(*p[4], *p[1])
            p[6] = _cmul(*p[4], *p[2])
            p[7] = _cmul(*p[4], *p[3])
            p[8] = _cmul(*p[4], *p[4])
            for j, k in enumerate((1, 2, 4)):
                keep = (row >= k) if base == 0 else (row < 8 - k)
                tab_ref[base + 2 * j] = jnp.where(keep, jnp.broadcast_to(p[k][0], (8, s)), 0.0)
                tab_ref[base + 2 * j + 1] = jnp.where(keep, jnp.broadcast_to(p[k][1], (8, s)), 0.0)
            pw_r = jnp.zeros((8, s), F32)
            pw_i = jnp.zeros((8, s), F32)
            for i in range(8):
                e = i + 1 if base == 0 else 8 - i
                pw_r = jnp.where(row == i, jnp.broadcast_to(p[e][0], (8, s)), pw_r)
                pw_i = jnp.where(row == i, jnp.broadcast_to(p[e][1], (8, s)), pw_i)
            tab_ref[base + 6] = pw_r
            tab_ref[base + 7] = pw_i

    def full(shape):
        return pl.BlockSpec(shape, lambda i: (0,) * len(shape))

    g = D_MODEL // SSM_GROUP
    return _call(body, "s5_prep", (1,),
                 [full((1, s)), full((1, s)), full((1, g)), full((16, s)), full((16, s)), full((g, s))],
                 [full((1, s)), full((1, s)), full((16, s)), full((16, s)), full((16, 8, s))],
                 [_sds((1, s), F32), _sds((1, s), F32), _sds((16, s), F32), _sds((16, s), F32),
                  _sds((16, 8, s), F32)])(lr, li, ldt, br, bi, expand)


def _s5_prep_bwd(lr, li, ldt, br, bi, d_abr, d_abi, d_bbr, d_bbi):
    s = N_STATE
    g = D_MODEL // SSM_GROUP
    expand = _group_expand_matrix()

    def body(lr_ref, li_ref, ldt_ref, br_ref, bi_ref, e_ref, c1, c2, c3, c4, o1, o2, o3, o4, o5):
        e_val = e_ref[...]
        _, vjp = jax.vjp(lambda a, b, c, d, e: _discretise(a, b, c, d, e, e_val),
                         lr_ref[...], li_ref[...], ldt_ref[...], br_ref[...], bi_ref[...])
        grads = vjp((c1[...], c2[...], c3[...], c4[...]))
        for o, gr in zip((o1, o2, o3, o4, o5), grads):
            o[...] = gr

    def full(shape):
        return pl.BlockSpec(shape, lambda i: (0,) * len(shape))

    return _call(body, "s5_prep_bwd", (1,),
                 [full((1, s)), full((1, s)), full((1, g)), full((16, s)), full((16, s)), full((g, s)),
                  full((1, s)), full((1, s)), full((16, s)), full((16, s))],
                 [full((1, s)), full((1, s)), full((1, g)), full((16, s)), full((16, s))],
                 [_sds((1, s), F32), _sds((1, s), F32), _sds((1, g), F32), _sds((16, s), F32),
                  _sds((16, s), F32)])(lr, li, ldt, br, bi, expand, d_abr, d_abi, d_bbr, d_bbi)


def _scan8(vr, vi, tab_ref, base, reverse):
    for j, k in enumerate((1, 2, 4)):
        mr = tab_ref[base + 2 * j]
        mi = tab_ref[base + 2 * j + 1]
        shift = 8 - k if reverse else k
        sr = pltpu.roll(vr, shift, 0)
        si = pltpu.roll(vi, shift, 0)
        vr, vi = vr + mr * sr - mi * si, vi + mr * si + mi * sr
    return vr, vi


def _s5_scan_fwd(x, bbr, bbi, cr, ci, dvec, tab, seq, comm=None):
    n, d = x.shape
    nt = seq // T_SCAN
    nblk = T_SCAN // 8
    st = CHUNK_ST

    def body(x_ref, bbr_ref, bbi_ref, cr_ref, ci_ref, d_ref, tab_ref, y_ref, hr_ref, hi_ref, car_r, car_i):
        t = pl.program_id(2)

        @pl.when(t == 0)
        def _():
            car_r[...] = jnp.zeros_like(car_r)
            car_i[...] = jnp.zeros_like(car_i)

        u = x_ref[...]
        ub = u.astype(BF16)
        hr_ref[...] = jnp.dot(ub, bbr_ref[...], preferred_element_type=F32)
        hi_ref[...] = jnp.dot(ub, bbi_ref[...], preferred_element_type=F32)

        def step(i, carry):
            sl = pl.ds(pl.multiple_of(i * 8, 8), 8)
            vr, vi = _scan8(hr_ref[sl, :], hi_ref[sl, :], tab_ref, 0, False)
            c_r = jnp.broadcast_to(car_r[...], (8, st))
            c_i = jnp.broadcast_to(car_i[...], (8, st))
            pr = tab_ref[6]
            pi = tab_ref[7]
            vr, vi = vr + pr * c_r - pi * c_i, vi + pr * c_i + pi * c_r
            hr_ref[sl, :] = vr
            hi_ref[sl, :] = vi
            car_r[...] = vr[7:8, :]
            car_i[...] = vi[7:8, :]
            return carry

        lax.fori_loop(0, nblk, step, 0, unroll=4)
        y = lax.dot_general(hr_ref[...].astype(BF16), cr_ref[...], NT, preferred_element_type=F32)
        y = y - lax.dot_general(hi_ref[...].astype(BF16), ci_ref[...], NT, preferred_element_type=F32)
        y_ref[...] = y + d_ref[...] * u

    row = lambda c, b, t: (b * nt + t, c)
    mat = pl.BlockSpec((None, CHUNK_CH, st), lambda c, b, t: (c, 0, 0))
    return _call(
        body, "s5_scan_fwd", (SSM_CHUNKS, n // seq, nt),
        [pl.BlockSpec((T_SCAN, CHUNK_CH), row), mat, mat, mat, mat,
         pl.BlockSpec((1, CHUNK_CH), lambda c, b, t: (0, c)),
         pl.BlockSpec((16, 8, st), lambda c, b, t: (0, 0, c))],
        [pl.BlockSpec((T_SCAN, CHUNK_CH), row), pl.BlockSpec((T_SCAN, st), row), pl.BlockSpec((T_SCAN, st), row)],
        [_sds((n, d), F32), _sds((n, N_STATE), F32), _sds((n, N_STATE), F32)],
        [pltpu.VMEM((1, st), F32), pltpu.VMEM((1, st), F32)], comm=comm,
    )(x, bbr, bbi, cr, ci, dvec, tab)


def _s5_scan_bwd(x, dy, dres, hr, hi, bbr, bbi, cr, ci, dvec, tab, seq, comm=None):
    n, d = x.shape
    nt = seq // T_SCAN
    nblk = T_SCAN // 8
    st = CHUNK_ST

    def body(x_ref, dy_ref, dres_ref, hr_ref, hi_ref, hpr_ref, hpi_ref, bbr_ref, bbi_ref, cr_ref, ci_ref,
             d_ref, tab_ref, dx_ref, dcr_ref, dci_ref, dbr_ref, dbi_ref, dar_ref, dai_ref, dd_ref,
             g_r, g_i, car_r, car_i):
        b = pl.program_id(1)
        tg = pl.program_id(2)
        first = jnp.logical_and(b == 0, tg == 0)

        @pl.when(tg == 0)
        def _():
            car_r[...] = jnp.zeros_like(car_r)
            car_i[...] = jnp.zeros_like(car_i)

        u = x_ref[...]
        dyv = dy_ref[...]
        ub = u.astype(BF16)
        dyb = dyv.astype(BF16)
        g_r[...] = jnp.dot(dyb, cr_ref[...], preferred_element_type=F32)
        g_i[...] = -jnp.dot(dyb, ci_ref[...], preferred_element_type=F32)

        def step(ii, carry):
            i = nblk - 1 - ii
            sl = pl.ds(pl.multiple_of(i * 8, 8), 8)
            vr, vi = _scan8(g_r[sl, :], g_i[sl, :], tab_ref, 8, True)
            c_r = jnp.broadcast_to(car_r[...], (8, st))
            c_i = jnp.broadcast_to(car_i[...], (8, st))
            pr = tab_ref[14]
            pi = tab_ref[15]
            vr, vi = vr + pr * c_r - pi * c_i, vi + pr * c_i + pi * c_r
            g_r[sl, :] = vr
            g_i[sl, :] = vi
            car_r[...] = vr[0:1, :]
            car_i[...] = vi[0:1, :]
            return carry

        lax.fori_loop(0, nblk, step, 0, unroll=4)
        gr = g_r[...]
        gi = g_i[...]
        grb = gr.astype(BF16)
        gib = gi.astype(BF16)
        du = lax.dot_general(grb, bbr_ref[...], NT, preferred_element_type=F32)
        du = du + lax.dot_general(gib, bbi_ref[...], NT, preferred_element_type=F32)
        dx_ref[...] = DN_ALPHA * dres_ref[...] + du + d_ref[...] * dyv

        h_r = hr_ref[...]
        h_i = hi_ref[...]
        t_orig = nt - 1 - tg
        row0 = lax.broadcasted_iota(jnp.int32, (8, st), 0) == 0
        halo_ok = t_orig > 0

        def previous(h, halo_ref):
            top = jnp.broadcast_to(jnp.where(halo_ok, halo_ref[7:8, :], 0.0), (8, st))
            rolled = pltpu.roll(h, 1, 0)
            return jnp.concatenate([jnp.where(row0, top, rolled[:8, :]), rolled[8:, :]], axis=0)

        hp_r = previous(h_r, hpr_ref)
        hp_i = previous(h_i, hpi_ref)
        dar = jnp.sum(gr * hp_r + gi * hp_i, axis=0, keepdims=True)
        dai = jnp.sum(gi * hp_r - gr * hp_i, axis=0, keepdims=True)
        dcr = lax.dot_general(dyb, h_r.astype(BF16), TN, preferred_element_type=F32)
        dci = lax.dot_general(dyb, h_i.astype(BF16), TN, preferred_element_type=F32)
        dbr = lax.dot_general(ub, grb, TN, preferred_element_type=F32)
        dbi = lax.dot_general(ub, gib, TN, preferred_element_type=F32)
        dd = jnp.sum(dyv * u, axis=0, keepdims=True)

        @pl.when(first)
        def _():
            dcr_ref[...] = dcr
            dci_ref[...] = dci
            dbr_ref[...] = dbr
            dbi_ref[...] = dbi
            dar_ref[...] = dar
            dai_ref[...] = dai
            dd_ref[...] = dd

        @pl.when(jnp.logical_not(first))
        def _():
            dcr_ref[...] += dcr
            dci_ref[...] += dci
            dbr_ref[...] += dbr
            dbi_ref[...] += dbi
            dar_ref[...] += dar
            dai_ref[...] += dai
            dd_ref[...] += dd

    row = lambda c, b, t: (b * nt + (nt - 1 - t), c)
    halo = lambda c, b, t: (jnp.maximum((b * nt + (nt - 1 - t)) * (T_SCAN // 8) - 1, 0), c)
    mat = pl.BlockSpec((None, CHUNK_CH, st), lambda c, b, t: (c, 0, 0))
    chan = pl.BlockSpec((T_SCAN, CHUNK_CH), row)
    state = pl.BlockSpec((T_SCAN, st), row)
    return _call(
        body, "s5_scan_bwd", (SSM_CHUNKS, n // seq, nt),
        [chan, chan, chan, state, state, pl.BlockSpec((8, st), halo), pl.BlockSpec((8, st), halo),
         mat, mat, mat, mat, pl.BlockSpec((1, CHUNK_CH), lambda c, b, t: (0, c)),
         pl.BlockSpec((16, 8, st), lambda c, b, t: (0, 0, c))],
        [chan, mat, mat, mat, mat, pl.BlockSpec((1, st), lambda c, b, t: (0, c)),
         pl.BlockSpec((1, st), lambda c, b, t: (0, c)), pl.BlockSpec((1, CHUNK_CH), lambda c, b, t: (0, c))],
        [_sds((n, d), F32)] + [_sds((SSM_CHUNKS, CHUNK_CH, st), F32)] * 4
        + [_sds((1, N_STATE), F32), _sds((1, N_STATE), F32), _sds((1, d), F32)],
        [pltpu.VMEM((T_SCAN, st), F32), pltpu.VMEM((T_SCAN, st), F32), pltpu.VMEM((1, st), F32),
         pltpu.VMEM((1, st), F32)], comm=comm,
    )(x, dy, dres, hr, hi, hr, hi, bbr, bbi, cr, ci, dvec, tab)


_GELU_C = math.sqrt(2.0 / math.pi)


def _gelu_parts(y):
    t = jnp.tanh(_GELU_C * (y + 0.044715 * (y * y * y)))
    return 0.5 * (1.0 + t), t


def _glu_fwd(y, w_glu, b_glu):
    n, d = y.shape

    def body(y_ref, w_ref, b_ref, y2_ref, z_ref, gg_ref):
        yv = y_ref[...]
        cdf, _ = _gelu_parts(yv)
        y2 = yv * cdf
        z = jnp.dot(y2.astype(BF16), w_ref[...], preferred_element_type=F32) + b_ref[...]
        y2_ref[...] = y2.astype(BF16)
        z_ref[...] = z
        gg_ref[...] = (y2 * jax.nn.sigmoid(z)).astype(BF16)

    row = pl.BlockSpec((TM, d), lambda i: (i, 0))
    return _call(body, "glu_fwd", (n // TM,),
                 [row, pl.BlockSpec((d, d), lambda i: (0, 0)), pl.BlockSpec((1, d), lambda i: (0, 0))],
                 [row, row, row], [_sds((n, d), BF16), _sds((n, d), F32), _sds((n, d), BF16)])(y, w_glu, b_glu)


def _glu_bwd_gate(y, z, dgg):
    n, d = y.shape

    def body(y_ref, z_ref, dgg_ref, dz_ref, db_ref, t_ref):
        i = pl.program_id(0)
        yv = y_ref[...]
        cdf, _ = _gelu_parts(yv)
        y2 = yv * cdf
        s = jax.nn.sigmoid(z_ref[...])
        dg = dgg_ref[...]
        dz = dg * y2 * s * (1.0 - s)
        dz_ref[...] = dz.astype(BF16)
        t_ref[...] = dg * s
        db = jnp.sum(dz, axis=0, keepdims=True)

        @pl.when(i == 0)
        def _():
            db_ref[...] = db

        @pl.when(i > 0)
        def _():
            db_ref[...] += db

    row = pl.BlockSpec((TM, d), lambda i: (i, 0))
    vec = pl.BlockSpec((1, d), lambda i: (0, 0))
    return _call(body, "glu_bwd_gate", (n // TM,), [row, row, row], [row, vec, row],
                 [_sds((n, d), BF16), _sds((1, d), F32), _sds((n, d), F32)])(y, z, dgg)


def _gelu_bwd(dy2, y):
    cdf, t = _gelu_parts(y)
    dinner = _GELU_C * (1.0 + 3.0 * 0.044715 * y * y)
    return dy2 * (cdf + 0.5 * y * (1.0 - t * t) * dinner)


HALO = 16


def _shift_down(x, halo, k):
    out = pltpu.roll(x, k, 0)
    row = lax.broadcasted_iota(jnp.int32, (8, x.shape[1]), 0)
    top = jnp.where(row < k, pltpu.roll(halo[HALO - 8:HALO, :], k, 0), out[:8, :])
    return jnp.concatenate([top, out[8:, :]], axis=0)


def _conv3(x, halo, w, b):
    x1 = _shift_down(x, halo, 1)
    x2 = _shift_down(x, halo, 2)
    return b + w[0:1, :] * x + w[1:2, :] * x1 + w[2:3, :] * x2, x1, x2


def _conv_gate_fwd(hc, conv_w, conv_b, seq):
    _, nb, n, c = hc.shape
    tiles_per_seq = seq // TM

    def body(x_ref, halo_ref, w_ref, b_ref, o_ref):
        i = pl.program_id(1)
        start = (i % tiles_per_seq) == 0
        cv = []
        for h in range(2):
            halo = jnp.where(start, 0.0, halo_ref[h].astype(F32))
            cv.append(_conv3(x_ref[h].astype(F32), halo, w_ref[h], b_ref[h])[0])
        o_ref[...] = (cv[1] * jax.nn.sigmoid(cv[1]) * cv[0]).astype(BF16)

    return _call(
        body, "conv_gate_fwd", (nb, n // TM),
        [pl.BlockSpec((2, None, TM, c), lambda d, i: (0, d, i, 0)),
         pl.BlockSpec((2, None, HALO, c), lambda d, i: (0, d, jnp.maximum(i * (TM // HALO) - 1, 0), 0)),
         pl.BlockSpec((2, None, 3, c), lambda d, i: (0, d, 0, 0)),
         pl.BlockSpec((2, None, 1, c), lambda d, i: (0, d, 0, 0))],
        pl.BlockSpec((None, TM, c), lambda d, i: (d, i, 0)),
        _sds((nb, n, c), BF16))(hc, hc, conv_w, conv_b)


def _conv_gate_bwd(hc, dact, conv_w, conv_b, seq, comm=None):
    _, nb, n, c = hc.shape
    tiles_per_seq = seq // TM
    last_halo = n // HALO - 1
    ext = TM + HALO

    def body(x_ref, prev_ref, next_ref, da_ref, dan_ref, w_ref, b_ref, dx_ref, dw_ref):
        i = pl.program_id(1)
        start = (i % tiles_per_seq) == 0
        end = (i % tiles_per_seq) == tiles_per_seq - 1
        row = lax.broadcasted_iota(jnp.int32, (ext, c), 0)
        xe, cv = [], []
        for h in range(2):
            halo = jnp.where(start, 0.0, prev_ref[h].astype(F32))
            x_ext = jnp.concatenate([x_ref[h], next_ref[h]], axis=0).astype(F32)
            conv, x1, x2 = _conv3(x_ext, halo, w_ref[h], b_ref[h])
            xe.append((x_ext, x1, x2))
            cv.append(conv)
        da = jnp.concatenate([da_ref[...], dan_ref[...]], axis=0).astype(F32)
        da = jnp.where(jnp.logical_and(end, row >= TM), 0.0, da)
        sg = jax.nn.sigmoid(cv[1])
        silu = cv[1] * sg
        dcv = [da * silu, da * cv[0] * (sg + silu * (1.0 - sg))]
        for h in range(2):
            w = w_ref[h]
            g = dcv[h]
            dx = w[0:1, :] * g + w[1:2, :] * pltpu.roll(g, ext - 1, 0) + w[2:3, :] * pltpu.roll(g, ext - 2, 0)
            dx_ref[h] = dx[:TM, :].astype(BF16)
            x_ext, x1, x2 = xe[h]
            gt = g[:TM, :]
            parts = [jnp.sum(gt * x_ext[:TM, :], axis=0, keepdims=True),
                     jnp.sum(gt * x1[:TM, :], axis=0, keepdims=True),
                     jnp.sum(gt * x2[:TM, :], axis=0, keepdims=True),
                     jnp.sum(gt, axis=0, keepdims=True)]
            upd = jnp.concatenate(parts + [jnp.zeros((4, c), F32)], axis=0)

            @pl.when(i == 0)
            def _():
                dw_ref[h] = upd

            @pl.when(i > 0)
            def _():
                dw_ref[h] += upd

    nxt = lambda i: jnp.minimum((i + 1) * (TM // HALO), last_halo)
    return _call(
        body, "conv_gate_bwd", (nb, n // TM),
        [pl.BlockSpec((2, None, TM, c), lambda d, i: (0, d, i, 0)),
         pl.BlockSpec((2, None, HALO, c), lambda d, i: (0, d, jnp.maximum(i * (TM // HALO) - 1, 0), 0)),
         pl.BlockSpec((2, None, HALO, c), lambda d, i: (0, d, nxt(i), 0)),
         pl.BlockSpec((None, TM, c), lambda d, i: (d, i, 0)),
         pl.BlockSpec((None, HALO, c), lambda d, i: (d, nxt(i), 0)),
         pl.BlockSpec((2, None, 3, c), lambda d, i: (0, d, 0, 0)),
         pl.BlockSpec((2, None, 1, c), lambda d, i: (0, d, 0, 0))],
        [pl.BlockSpec((2, None, TM, c), lambda d, i: (0, d, i, 0)),
         pl.BlockSpec((2, None, 8, c), lambda d, i: (0, d, 0, 0))],
        [_sds((2, nb, n, c), BF16), _sds((2, nb, 8, c), F32)], comm=comm)(hc, hc, hc, dact, dact, conv_w, conv_b)


def _t5_bucket_np(dist):
    exact = REL_BUCKETS // 2
    d = np.maximum(dist, 1).astype(np.float32)
    large = exact + (np.log(d / exact) / math.log(REL_MAX_DIST / exact) * (REL_BUCKETS - exact)).astype(np.int64)
    large = np.minimum(large, REL_BUCKETS - 1)
    return np.where(dist < exact, dist, large).astype(np.int32)


def _bucket_table(dil):
    steps = np.arange(BAND)[:, None] + BAND - np.arange(2 * BAND)[None, :]
    return _t5_bucket_np(np.maximum(steps, 0) * dil).reshape(1, BAND * 2 * BAND).astype(np.float32)


def _bias_expand(rel_t, dil):
    nk = BAND * 2 * BAND
    bucket = jnp.asarray(_bucket_table(dil))

    def body(r_ref, bk_ref, o_ref):
        ids = lax.broadcasted_iota(jnp.int32, (REL_BUCKETS, nk), 0).astype(F32)
        onehot = (ids == bk_ref[...]).astype(F32)
        o_ref[...] = jnp.dot(r_ref[...], onehot, precision=HIGHEST, preferred_element_type=F32)

    return _call(body, "bias_expand", (1,),
                 [pl.BlockSpec((N_HEADS, REL_BUCKETS), lambda i: (0, 0)), pl.BlockSpec((1, nk), lambda i: (0, 0))],
                 pl.BlockSpec((N_HEADS, nk), lambda i: (0, 0)), _sds((N_HEADS, nk), F32))(rel_t, bucket)


def _bias_reduce(dbias, dil):
    nk = BAND * 2 * BAND
    bucket = jnp.asarray(_bucket_table(dil))

    def body(g_ref, bk_ref, o_ref):
        ids = lax.broadcasted_iota(jnp.int32, (REL_BUCKETS, nk), 0).astype(F32)
        onehot = (ids == bk_ref[...]).astype(F32)
        o_ref[...] = lax.dot_general(g_ref[...], onehot, NT, precision=HIGHEST, preferred_element_type=F32)

    return _call(body, "bias_reduce", (1,),
                 [pl.BlockSpec((N_HEADS, nk), lambda i: (0, 0)), pl.BlockSpec((1, nk), lambda i: (0, 0))],
                 pl.BlockSpec((N_HEADS, REL_BUCKETS), lambda i: (0, 0)),
                 _sds((N_HEADS, REL_BUCKETS), F32))(dbias, bucket)


def _valid_mask(n):
    qi = lax.broadcasted_iota(jnp.int32, (BAND, 2 * BAND), 0)
    kj = lax.broadcasted_iota(jnp.int32, (BAND, 2 * BAND), 1)
    steps = qi + BAND - kj
    in_band = jnp.logical_and(steps >= 0, steps <= BAND)
    return jnp.logical_and(in_band, jnp.logical_or(n > 0, kj >= BAND))


ATTN_COLS = {1: 1024, 4: 512, 16: 128}


def _residue(dil, r):
    return slice(None) if dil == 1 else pl.ds(r, BAND, stride=dil)


def _stack_heads(x, first):
    return jnp.concatenate([jnp.where(first, x, 0.0), jnp.where(first, 0.0, x)], axis=0)


def _attn_fwd(q, k, v, bias, g, dil, bsz, seq, comm=None):
    n = q.shape[0]
    rows = BAND * dil
    nch = seq // rows
    w = N_HEADS * HEAD_DIM
    wc = ATTN_COLS[dil]
    ncol = w // wc
    hpc = wc // HEAD_DIM
    nj = wc // 128
    slab = lambda k: pltpu.VMEM((k * nj, rows, 128), F32)

    def body(q_ref, kc_ref, kp_ref, vc_ref, vp_ref, b_ref, o_ref, l_ref, qf, kf, vf, of, lf):
        ch = pl.program_id(2)
        valid = _valid_mask(ch)
        valid = jnp.concatenate([valid, valid], axis=0)
        lane = lax.broadcasted_iota(jnp.int32, (BAND, 128), 1)
        first = lane < HEAD_DIM
        for j in range(nj):
            sl = slice(128 * j, 128 * (j + 1))
            qf[j] = q_ref[:, sl].astype(F32)
            kf[j] = kp_ref[:, sl].astype(F32)
            kf[nj + j] = kc_ref[:, sl].astype(F32)
            vf[j] = vp_ref[:, sl].astype(F32)
            vf[nj + j] = vc_ref[:, sl].astype(F32)
        for r in range(dil):
            rr = _residue(dil, r)
            for j in range(nj):
                q2 = qf.at[j][rr, :]
                k2 = jnp.concatenate([kf.at[j][rr, :], kf.at[nj + j][rr, :]], axis=0).astype(BF16)
                v2 = jnp.concatenate([vf.at[j][rr, :], vf.at[nj + j][rr, :]], axis=0).astype(BF16)
                qs = _stack_heads(q2, first).astype(BF16)
                s = lax.dot_general(qs, k2, NT, preferred_element_type=F32) * (HEAD_DIM ** -0.5)
                s = jnp.where(valid, s + b_ref[2 * j:2 * j + 2].reshape(2 * BAND, 2 * BAND), NEG_BIG)
                m = jnp.max(s, axis=1, keepdims=True)
                p = jnp.exp(s - m)
                l = jnp.sum(p, axis=1, keepdims=True)
                pv = jnp.dot(p.astype(BF16), v2, preferred_element_type=F32) / l
                lse = jnp.broadcast_to(m + jnp.log(l), (2 * BAND, 128))
                of.at[j][rr, :] = jnp.where(first, pv[:BAND], pv[BAND:])
                lf.at[j][rr, :] = jnp.where(first, lse[:BAND], lse[BAND:])
        for j in range(nj):
            sl = slice(128 * j, 128 * (j + 1))
            o_ref[:, sl] = of[j]
            l_ref[:, sl] = lf[j]

    base = g * ncol
    cur = lambda c, b, ch: (b * nch + ch, base + c)
    prev = lambda c, b, ch: (b * nch + jnp.maximum(ch - 1, 0), base + c)
    blk = lambda im: pl.BlockSpec((rows, wc), im)
    out_blk = pl.BlockSpec((rows, wc), lambda c, b, ch: (b * nch + ch, c))
    return _call(
        body, "attn_fwd_d%d" % dil, (ncol, bsz, nch),
        [blk(cur), blk(cur), blk(prev), blk(cur), blk(prev),
         pl.BlockSpec((hpc, BAND, 2 * BAND), lambda c, b, ch: (c, 0, 0))],
        [out_blk, out_blk],
        [_sds((n, w), F32)] * 2,
        [slab(1), slab(2), slab(2), slab(1), slab(1)], comm=comm,
    )(q, k, k, v, v, bias)


def _attn_combine(outs, lses):
    n, w = outs[0].shape

    def body(o0, o1, o2, l0, l1, l2, o_ref, l_ref):
        la, lb, lc = l0[...], l1[...], l2[...]
        m = jnp.maximum(jnp.maximum(la, lb), lc)
        wa, wb, wc = jnp.exp(la - m), jnp.exp(lb - m), jnp.exp(lc - m)
        tot = wa + wb + wc
        o_ref[...] = (wa * o0[...] + wb * o1[...] + wc * o2[...]) / tot
        l_ref[...] = m + jnp.log(tot)

    row = pl.BlockSpec((TM, w), lambda i: (i, 0))
    return _call(body, "attn_combine", (n // TM,), [row] * 6, [row, row],
                 [_sds((n, w), F32)] * 2)(*outs, *lses)


def _attn_bwd(q, k, v, do, o, lse, bias, g, dil, bsz, seq, comm=None, into=()):
    n = q.shape[0]
    rows = BAND * dil
    nch = seq // rows
    w = N_HEADS * HEAD_DIM
    wc = ATTN_COLS[dil]
    ncol = w // wc
    hpc = wc // HEAD_DIM
    nj = wc // 128
    slab = lambda k: pltpu.VMEM((k * nj, rows, 128), F32)
    scale = HEAD_DIM ** -0.5

    def body(q_ref, kc_ref, kp_ref, vc_ref, vp_ref, do_ref, o_ref, l_ref, b_ref,
             dq_ref, dk_ref, dv_ref, db_ref, qf, kf, vf, dof, of, lf, dqf, dkf, dvf):
        b = pl.program_id(1)
        ch = pl.program_id(2)
        cur = ch % 2
        prv = 1 - cur

        @pl.when(jnp.logical_and(b == 0, ch == 0))
        def _():
            db_ref[...] = jnp.zeros_like(db_ref)

        @pl.when(ch == 0)
        def _():
            for j in range(nj):
                dkf[nj + j] = jnp.zeros((rows, 128), F32)
                dvf[nj + j] = jnp.zeros((rows, 128), F32)

        @pl.when(ch < nch)
        def _():
            valid = _valid_mask(ch)
            valid = jnp.concatenate([valid, valid], axis=0)
            first = lax.broadcasted_iota(jnp.int32, (BAND, 128), 1) < HEAD_DIM
            for j in range(nj):
                sl = slice(128 * j, 128 * (j + 1))
                qf[j] = q_ref[:, sl].astype(F32)
                kf[j] = kp_ref[:, sl].astype(F32)
                kf[nj + j] = kc_ref[:, sl].astype(F32)
                vf[j] = vp_ref[:, sl].astype(F32)
                vf[nj + j] = vc_ref[:, sl].astype(F32)
                dof[j] = do_ref[:, sl]
                of[j] = o_ref[:, sl]
                lf[j] = l_ref[:, sl]
            for r in range(dil):
                rr = _residue(dil, r)
                for j in range(nj):
                    k2 = jnp.concatenate([kf.at[j][rr, :], kf.at[nj + j][rr, :]], axis=0).astype(BF16)
                    v2 = jnp.concatenate([vf.at[j][rr, :], vf.at[nj + j][rr, :]], axis=0).astype(BF16)
                    do2 = dof.at[j][rr, :]
                    lse2 = lf.at[j][rr, :]
                    qs = _stack_heads(qf.at[j][rr, :], first).astype(BF16)
                    dos = _stack_heads(do2, first)
                    delta = jnp.sum(dos * jnp.concatenate([of.at[j][rr, :]] * 2, axis=0), axis=1, keepdims=True)
                    dos = dos.astype(BF16)
                    lse_col = jnp.concatenate([lse2[:, 0:1], lse2[:, HEAD_DIM:HEAD_DIM + 1]], axis=0)
                    s = lax.dot_general(qs, k2, NT, preferred_element_type=F32) * scale
                    s = jnp.where(valid, s + b_ref[2 * j:2 * j + 2].reshape(2 * BAND, 2 * BAND), NEG_BIG)
                    p = jnp.exp(s - lse_col)
                    dp = lax.dot_general(dos, v2, NT, preferred_element_type=F32)
                    ds = p * (dp - delta)
                    db_ref[2 * j:2 * j + 2] += ds.reshape(2, BAND, 2 * BAND)
                    dsb = ds.astype(BF16)
                    dq2 = jnp.dot(dsb, k2, preferred_element_type=F32) * scale
                    dk2 = lax.dot_general(dsb, qs, TN, preferred_element_type=F32) * scale
                    dv2 = lax.dot_general(p.astype(BF16), dos, TN, preferred_element_type=F32)
                    dqf.at[j][rr, :] = jnp.where(first, dq2[:BAND], dq2[BAND:])
                    dkf.at[prv * nj + j][rr, :] += dk2[:BAND, :]
                    dvf.at[prv * nj + j][rr, :] += dv2[:BAND, :]
                    dkf.at[cur * nj + j][rr, :] = dk2[BAND:, :]
                    dvf.at[cur * nj + j][rr, :] = dv2[BAND:, :]
            for j in range(nj):
                dq_ref[:, 128 * j:128 * (j + 1)] = dqf[j].astype(BF16)

        for j in range(nj):
            sl = slice(128 * j, 128 * (j + 1))
            dk_ref[:, sl] = dkf[prv * nj + j].astype(BF16)
            dv_ref[:, sl] = dvf[prv * nj + j].astype(BF16)

    last = nch - 1
    base = g * ncol
    cur3 = lambda c, b, ch: (b * nch + jnp.minimum(ch, last), base + c)
    prev3 = lambda c, b, ch: (b * nch + jnp.maximum(jnp.minimum(ch, last) - 1, 0), base + c)
    cur1 = lambda c, b, ch: (b * nch + jnp.minimum(ch, last), c)
    lag3 = lambda c, b, ch: (b * nch + jnp.maximum(ch - 1, 0), base + c)
    blk = lambda im: pl.BlockSpec((rows, wc), im)
    bias_blk = pl.BlockSpec((hpc, BAND, 2 * BAND), lambda c, b, ch: (c, 0, 0))
    out = _sds(q.shape, BF16)
    return _call(
        body, "attn_bwd_d%d" % dil, (ncol, bsz, nch + 1),
        [blk(cur3), blk(cur3), blk(prev3), blk(cur3), blk(prev3), blk(cur1), blk(cur1), blk(cur1), bias_blk],
        [blk(cur3), blk(lag3), blk(lag3), bias_blk],
        [out, out, out, _sds((N_HEADS, BAND, 2 * BAND), F32)],
        [slab(1), slab(2), slab(2), slab(1), slab(1), slab(1), slab(1), slab(2), slab(2)], comm=comm, fill=into,
    )(q, k, k, v, v, do, o, lse, bias)


def _adamw(name, parts, w, m, v):
    nl, r, c = w.shape
    nparts = parts[0].shape[0]
    tr = r
    for cand in (256, 352, 128):
        if r % cand == 0 and r > cand:
            tr = cand
            break
    c1 = 1.0 - ADAM_B1 ** ADAM_STEP
    c2 = 1.0 - ADAM_B2 ** ADAM_STEP

    def body(*refs):
        p_refs = refs[:nl]
        w_ref, m_ref, v_ref, g_ref, d_ref, nm_ref, nv_ref = refs[nl:]
        layer = pl.program_id(0)
        for l in range(nl):
            @pl.when(layer == l)
            def _():
                g = p_refs[l][0].astype(F32)
                for dev in range(1, nparts):
                    g = g + p_refs[l][dev].astype(F32)
                nm = ADAM_B1 * m_ref[...] + (1.0 - ADAM_B1) * g
                nv = ADAM_B2 * v_ref[...] + (1.0 - ADAM_B2) * (g * g)
                m_hat = nm / c1
                v_hat = nv / c2
                g_ref[...] = g
                d_ref[...] = -ADAM_LR * (m_hat / (jnp.sqrt(v_hat) + ADAM_EPS) + ADAM_WD * w_ref[...])
                nm_ref[...] = nm
                nv_ref[...] = nv

    def part_spec(l):
        return pl.BlockSpec((nparts, tr, c), lambda layer, i: (0, jnp.where(layer == l, i, 0), 0))

    row = pl.BlockSpec((None, tr, c), lambda layer, i: (layer, i, 0))
    return _call(body, name, (nl, r // tr), [part_spec(l) for l in range(nl)] + [row, row, row],
                 [row] * 4, [_sds((nl, r, c), F32)] * 4)(*parts, w, m, v)


def _exchange(name, srcs, out_shapes, items):
    n_in = len(srcs)
    n_out = len(out_shapes)
    n_items = len(items)

    def body(*refs):
        ins = refs[:n_in]
        outs = refs[n_in:n_in + n_out]
        send_sems, recv_sems, local_sems = refs[n_in + n_out:]
        x, y, c = lax.axis_index("x"), lax.axis_index("y"), lax.axis_index("c")
        me = 4 * x + 2 * y + c

        def pick(ref, sel, peer):
            idx = tuple(peer if s == "peer" else me if s == "me" else s for s in sel)
            return ref.at[idx] if idx else ref

        copies = []
        for it, (si, ssel, oi, osel) in enumerate(items):
            local = pltpu.make_async_copy(pick(ins[si], ssel, me), pick(outs[oi], osel, me), local_sems.at[it])
            local.start()
            copies.append(local)
            for j in range(1, N_DEV):
                px = 1 - x if j & 4 else x
                py = 1 - y if j & 2 else y
                pc = 1 - c if j & 1 else c
                peer = 4 * px + 2 * py + pc
                sem = it * (N_DEV - 1) + j - 1
                cp = pltpu.make_async_remote_copy(
                    src_ref=pick(ins[si], ssel, peer), dst_ref=pick(outs[oi], osel, peer),
                    send_sem=send_sems.at[sem], recv_sem=recv_sems.at[sem],
                    device_id=(px, py, pc), device_id_type=pl.DeviceIdType.MESH)
                cp.start()
                copies.append(cp)
        for cp in copies:
            cp.wait()

    any_spec = pl.BlockSpec(memory_space=pl.ANY)
    return pl.pallas_call(
        body, name=name, in_specs=[any_spec] * n_in, out_specs=[any_spec] * n_out, out_shape=list(out_shapes),
        scratch_shapes=[pltpu.SemaphoreType.DMA((n_items * (N_DEV - 1),)),
                        pltpu.SemaphoreType.DMA((n_items * (N_DEV - 1),)),
                        pltpu.SemaphoreType.DMA((n_items,))],
        compiler_params=pltpu.CompilerParams(has_side_effects=True),
    )(*srcs)


def _sel(ref, sel, slot=None):
    idx = tuple(slot if s == "slot" else s for s in sel)
    return ref.at[idx] if idx else ref


def _gather_plan(srcs, out_shapes, items):
    per = N_DEV - 1
    n_items = len(items)

    def phases(ins, outs, send_sems, recv_sems, local_sems, total):
        x, y, c = lax.axis_index("x"), lax.axis_index("y"), lax.axis_index("c")
        sibling = (x, y, 1 - c)
        chips = [(1 - x, y), (x, 1 - y), (1 - x, 1 - y)]
        slot_of = lambda px, py, pc: 4 * px + 2 * py + pc

        def copy(it, k, block, to, src=None):
            _, _, oi, osel = items[it]
            dst = _sel(outs[oi], osel, slot_of(*block))
            return pltpu.make_async_remote_copy(
                src_ref=dst if src is None else src, dst_ref=dst,
                send_sem=send_sems.at[it * per + k], recv_sem=recv_sems.at[it * per + k],
                device_id=to, device_id_type=pl.DeviceIdType.MESH)

        def own_copies(it):
            si, ssel, oi, osel = items[it]
            src = _sel(ins[si], ssel)
            mine = pltpu.make_async_copy(src, _sel(outs[oi], osel, slot_of(x, y, c)), local_sems.at[it])
            first = [copy(it, 0, (x, y, c), sibling, src=src)]
            first += [copy(it, 1 + j, (x, y, c), (*chip, c), src=src) for j, chip in enumerate(chips)]
            return mine, first

        def start():
            for it in range(n_items):
                mine, first = own_copies(it)
                mine.start()
                for cp in first:
                    cp.start()

        def forward(it):
            def go():
                for j, chip in enumerate(chips):
                    copy(it, 1 + j, (*chip, c), (x, y, c)).wait_recv()
                    copy(it, 4 + j, (*chip, c), sibling).start()
            return go

        def finish():
            for it in range(n_items):
                copy(it, 0, sibling, (x, y, c)).wait_recv()
                for j, chip in enumerate(chips):
                    copy(it, 4 + j, (*chip, 1 - c), (x, y, c)).wait_recv()
            for it in range(n_items):
                mine, first = own_copies(it)
                for cp in first:
                    cp.wait_send()
                for j, chip in enumerate(chips):
                    copy(it, 4 + j, (*chip, c), sibling).wait_send()
                mine.wait()

        lo = max(total // 3, 1)
        acts = [(0, start)]
        for it in range(n_items):
            acts.append((min(lo + it * (total - 1 - lo) // n_items, total - 2), forward(it)))
        acts.append((total - 1, finish))
        return acts

    return _Plan(srcs, out_shapes, n_items * per, n_items, phases)


def _scatter_plan(srcs, out_shapes, items):
    per = N_DEV - 1
    n_items = len(items)

    def phases(ins, outs, send_sems, recv_sems, local_sems, total):
        x, y, c = lax.axis_index("x"), lax.axis_index("y"), lax.axis_index("c")
        me = 4 * x + 2 * y + c

        def copies():
            out = []
            for it, (si, oi, osel) in enumerate(items):
                dst = _sel(outs[oi], osel, me)
                out.append(pltpu.make_async_copy(ins[si].at[me], dst, local_sems.at[it]))
                for j in range(1, N_DEV):
                    px = 1 - x if j & 4 else x
                    py = 1 - y if j & 2 else y
                    pc = 1 - c if j & 1 else c
                    out.append(pltpu.make_async_remote_copy(
                        src_ref=ins[si].at[4 * px + 2 * py + pc], dst_ref=dst,
                        send_sem=send_sems.at[it * per + j - 1], recv_sem=recv_sems.at[it * per + j - 1],
                        device_id=(px, py, pc), device_id_type=pl.DeviceIdType.MESH))
            return out

        def start():
            for cp in copies():
                cp.start()

        def finish():
            for cp in copies():
                cp.wait()

        return [(0, start), (total - 1, finish)]

    return _Plan(srcs, out_shapes, n_items * per, n_items, phases)


def _sum_slots(parts):
    _, r, c = parts.shape

    def body(p_ref, o_ref):
        acc = p_ref[0]
        for dev in range(1, N_DEV):
            acc = acc + p_ref[dev]
        o_ref[...] = acc

    return _call(body, "sum_slots", (1,), [pl.BlockSpec((N_DEV, r, c), lambda i: (0, 0, 0))],
                 pl.BlockSpec((r, c), lambda i: (0, 0)), _sds((r, c), F32))(parts)


def _pack(arrays):
    rows = []
    for a in arrays:
        flat = a.reshape(-1).astype(F32)
        pad = (-flat.shape[0]) % 1024
        if pad:
            flat = jnp.concatenate([flat, jnp.zeros((pad,), F32)])
        rows.append(flat.reshape(-1, 128))
    return jnp.concatenate(rows, axis=0)


def _unpack(buf, shapes):
    out = []
    r0 = 0
    for shp in shapes:
        size = int(np.prod(shp))
        nrows = -(-size // 1024) * 8
        out.append(buf[r0:r0 + nrows].reshape(-1)[:size].reshape(shp))
        r0 += nrows
    return out


def _block_diag_chunks(val):
    eye = jnp.eye(8, dtype=val.dtype)
    return jnp.einsum("cjhp,jk->cjhkp", val, eye).reshape(SSM_CHUNKS, CHUNK_CH, CHUNK_ST)


def _diag_blocks(dense):
    d5 = dense.reshape(SSM_CHUNKS, 8, SSM_GROUP, 8, SSM_STATE)
    return jnp.stack([d5[:, j, :, j, :] for j in range(8)], axis=1)


def _carried(res, comm):
    return res if comm is not None else (res, None)


def _ffn_fwd(tag, hin, hin_b, w_up, conv_w, conv_b, w_down4, gain, bias, seq, comm_up=None, comm_down=None):
    hc, got_up = _carried(_mm_nn("ffn_up" + tag, hin_b, w_up, FFN_HIDDEN, blocked_out=True, comm=comm_up), comm_up)
    nb, n, c = hc.shape
    hc = hc.reshape(2, nb // 2, n, c)
    act = _conv_gate_fwd(hc, conv_w, conv_b, seq)
    (r, hout, hout_b), got_down = _carried(
        _mm_nn_red("ffn_down" + tag, act, w_down4, F32, comm=comm_down, residual=hin, norm=(gain, bias)), comm_down)
    return hout, hout_b, (hin_b, hc, act, r), (got_up, got_down)


def _ffn_bwd(tag, saved, dh, w_up, conv_w, conv_b, w_down4, gain, seq, comm_conv=None, comm_dw=None,
             comm_dx=None):
    hin_b, hc, act, r = saved
    dr, dr_b, dgain, dbias = _ln_bwd("ffn_norm_bwd" + tag, r, dh, gain)
    d_wdown = _mm_tn("ffn_down_dw" + tag, act, dr_b, act.shape[0], BF16, a_blocked=True, g_mode="shared")
    dact = _mm_nt_out("ffn_down_dx" + tag, dr_b, w_down4, FFN_HIDDEN)
    (dhc, dconv), got_conv = _carried(_conv_gate_bwd(hc, dact, conv_w, conv_b, seq, comm=comm_conv), comm_conv)
    dhc8 = dhc.reshape(2 * dhc.shape[1], dhc.shape[2], dhc.shape[3])
    d_wup, got_dw = _carried(_mm_tn("ffn_up_dw" + tag, dhc8, hin_b, dhc8.shape[0], BF16, a_blocked=True,
                                    g_mode="shared", comm=comm_dw), comm_dw)
    dhin, got_dx = _carried(_mm_nt_red("ffn_up_dx" + tag, dhc8, w_up, F32, g_blocked=True, add=dr,
                                       add_scale=DN_ALPHA, comm=comm_dx), comm_dx)
    return dhin, d_wup, d_wdown, dconv, dgain, dbias, (got_conv, got_dw, got_dx)


def _local_step(x, target, p, wfull, bsz, seq, ex=None):
    n = bsz * seq
    xf = x.reshape(n, D_MODEL)
    tf = target.reshape(n, D_MODEL)

    lr = p["s5_lam_re"].reshape(1, N_STATE)
    li = p["s5_lam_im"].reshape(1, N_STATE)
    ldt = p["s5_log_dt"].reshape(1, D_MODEL // SSM_GROUP)
    br_t = p["s5_b_re"].reshape(N_STATE, SSM_GROUP).T
    bi_t = p["s5_b_im"].reshape(N_STATE, SSM_GROUP).T
    ab_r, ab_i, bb_r, bb_i, tab = _s5_prep(lr, li, ldt, br_t, bi_t)

    def bb_chunks(bb):
        return _block_diag_chunks(bb.reshape(SSM_GROUP, SSM_CHUNKS, 8, SSM_STATE).transpose(1, 2, 0, 3))

    def c_chunks(cc):
        return _block_diag_chunks(cc.reshape(SSM_CHUNKS, 8, SSM_GROUP, SSM_STATE))

    bbr_c, bbi_c = bb_chunks(bb_r).astype(BF16), bb_chunks(bb_i).astype(BF16)
    cr_c, ci_c = c_chunks(p["s5_c_re"]).astype(BF16), c_chunks(p["s5_c_im"]).astype(BF16)
    dvec = p["s5_d"].reshape(1, D_MODEL)

    if ex is None:
        y, h_r, h_i = _s5_scan_fwd(xf, bbr_c, bbi_c, cr_c, ci_c, dvec, tab, seq)
    else:
        (y, h_r, h_i), gathered = _s5_scan_fwd(xf, bbr_c, bbi_c, cr_c, ci_c, dvec, tab, seq,
                                               comm=ex.gather_first())
        p, wfull = ex.take_first(gathered, p)
    ln_g = p["ln_gain"].reshape(4, 1, D_MODEL)
    ln_b = p["ln_bias"].reshape(4, 1, D_MODEL)
    y2, z, gg = _glu_fwd(y, wfull["s5_w_glu"], p["s5_b_glu"].reshape(1, D_MODEL))
    r1, h1, h1b = _mm_nn("s5_out", gg, wfull["s5_w_out"][None], F32, residual=xf, norm=(ln_g[0], ln_b[0]))
    h2, h2b, ffn0_saved, (got_up, got_down) = _ffn_fwd(
        "0", h1, h1b, wfull["ffn_w_up"][0], p["conv_w"][0], p["conv_b"][0], wfull["ffn_w_down"][0],
        ln_g[1], ln_b[1], seq, comm_up=ex.gather_layer1("up", 0) if ex else None,
        comm_down=ex.gather_layer1("down", 0) if ex else None)
    halves = {"up": [got_up[0]], "down": [got_down[0]]} if ex else None

    kmat = _mm_nn("attn_k", h2b, wfull["attn_w_kv"], BF16, blocks=(0, 4))
    vmat = _mm_nn("attn_v", h2b, wfull["attn_w_kv"], BF16, blocks=(4, 4))
    qmat = _mm_nn("attn_q", h2b, wfull["attn_w_q"], BF16)
    biases, outs, lses = [], [], []
    for g, dil in enumerate(DILATIONS):
        rel_t = p["rel_bias"][:, g * N_HEADS:(g + 1) * N_HEADS].T
        biases.append(_bias_expand(rel_t, dil).reshape(N_HEADS, BAND, 2 * BAND))
        which = {4: "down", 16: "up"}.get(dil) if ex else None
        plan = ex.gather_layer1(which, 1) if which else None
        (o_g, l_g), got = _carried(_attn_fwd(qmat, kmat, vmat, biases[g], g, dil, bsz, seq, comm=plan), plan)
        if plan is not None:
            halves[which].append(got[0])
        outs.append(o_g)
        lses.append(l_g)
    if ex is not None:
        wfull = ex.take_layer1(halves, wfull)
    o, lse = _attn_combine(outs, lses)
    r3, h3, h3b = _mm_nn("attn_out", o, wfull["attn_w_out"][None], F32, residual=h2, norm=(ln_g[2], ln_b[2]))
    h4, _, ffn1_saved, _ = _ffn_fwd("1", h3, h3b, wfull["ffn_w_up"][1], p["conv_w"][1], p["conv_b"][1],
                                    wfull["ffn_w_down"][1], ln_g[3], ln_b[3], seq)
    loss_tile, dh4 = _loss_head(h4, tf)

    grads = {}
    dh3, grads["ffn_w_up1"], grads["ffn_w_down1"], dconv1, dg3, db3, _ = _ffn_bwd(
        "1", ffn1_saved, dh4, wfull["ffn_w_up"][1], p["conv_w"][1], p["conv_b"][1], wfull["ffn_w_down"][1],
        ln_g[3], seq)
    dr3, dr3b, dg2, db2 = _ln_bwd("attn_norm_bwd", r3, dh3, ln_g[2])
    grads["attn_w_out"] = _mm_tn("attn_out_dw", o, dr3b, 1, BF16, g_mode="shared")
    do = _mm_nt_red("attn_out_dx", dr3b, wfull["attn_w_out"][None], F32, g_blocked=False)
    dqkv, drel = (), []
    for g, dil in enumerate(DILATIONS):
        plan = ex.scatter(grads, ("ffn_w_up1", "ffn_w_down1")) if ex and dil == DILATIONS[-1] else None
        res, got = _carried(_attn_bwd(qmat, kmat, vmat, do, o, lse, biases[g], g, dil, bsz, seq, comm=plan,
                                      into=dqkv), plan)
        if plan is not None:
            ex.keep(plan, got)
        dqkv, db_g = tuple(res[:3]), res[3]
        drel.append(_bias_reduce(db_g.reshape(N_HEADS, BAND * 2 * BAND), dil).T)
    dq, dk, dv = dqkv
    grads["rel_bias"] = jnp.concatenate(drel, axis=1)
    grads["attn_w_q"] = _mm_tn("attn_q_dw", h2b, dq, N_DEV, BF16)
    grads["attn_w_kv"] = jnp.concatenate([_mm_tn("attn_k_dw", h2b, dk, 4, BF16),
                                          _mm_tn("attn_v_dw", h2b, dv, 4, BF16)], axis=0)
    dh2 = _mm_nt_red("attn_q_dx", dq, wfull["attn_w_q"], F32, g_blocked=False, add=dr3, add_scale=DN_ALPHA)
    dh2 = _mm_nt_red("attn_k_dx", dk, wfull["attn_w_kv"], F32, g_blocked=False, add=dh2, blocks=(0, 4))
    dh2 = _mm_nt_red("attn_v_dx", dv, wfull["attn_w_kv"], F32, g_blocked=False, add=dh2, blocks=(4, 4))

    plans = [ex.scatter(grads, names) if ex else None
             for names in (("attn_w_kv",), ("attn_w_out",), ("attn_w_q",))]
    dh1, grads["ffn_w_up0"], grads["ffn_w_down0"], dconv0, dg1, db1, got = _ffn_bwd(
        "0", ffn0_saved, dh2, wfull["ffn_w_up"][0], p["conv_w"][0], p["conv_b"][0], wfull["ffn_w_down"][0],
        ln_g[1], seq, comm_conv=plans[0], comm_dw=plans[1], comm_dx=plans[2])
    if ex is not None:
        for plan, outs in zip(plans, got):
            ex.keep(plan, outs)
    dr1, dr1b, dg0, db0 = _ln_bwd("s5_norm_bwd", r1, dh1, ln_g[0])
    grads["s5_w_out"] = _mm_tn("s5_out_dw", gg, dr1b, 1, BF16, g_mode="shared")
    dgg = _mm_nt_red("s5_out_dx", dr1b, wfull["s5_w_out"][None], F32, g_blocked=False)
    dz, d_bglu, dy2_direct = _glu_bwd_gate(y, z, dgg)
    grads["s5_w_glu"] = _mm_tn("s5_glu_dw", y2, dz, 1, BF16, g_mode="shared")
    dy = _mm_nt_red("s5_glu_dx", dz, wfull["s5_w_glu"][None], F32, g_blocked=False, add=dy2_direct,
                    post=(_gelu_bwd, y))
    plan_last = ex.scatter(grads, ("ffn_w_up0", "ffn_w_down0", "s5_w_out", "s5_w_glu")) if ex else None
    res = _s5_scan_bwd(xf, dy, dr1, h_r, h_i, bbr_c, bbi_c, cr_c, ci_c, dvec, tab, seq, comm=plan_last)
    if ex is not None:
        res, got = res
        ex.keep(plan_last, got)
    dx, dcr, dci, dbr, dbi, dar, dai, dd = res

    def bb_from_chunks(dense):
        return _diag_blocks(dense).transpose(2, 0, 1, 3).reshape(SSM_GROUP, N_STATE)

    d_lr, d_li, d_ldt, d_br, d_bi = _s5_prep_bwd(lr, li, ldt, br_t, bi_t, dar, dai,
                                                 bb_from_chunks(dbr), bb_from_chunks(dbi))
    grads["s5_lam_re"] = d_lr.reshape(p["s5_lam_re"].shape)
    grads["s5_lam_im"] = d_li.reshape(p["s5_lam_im"].shape)
    grads["s5_log_dt"] = d_ldt.reshape(p["s5_log_dt"].shape)
    grads["s5_b_re"] = d_br.T.reshape(p["s5_b_re"].shape)
    grads["s5_b_im"] = d_bi.T.reshape(p["s5_b_im"].shape)
    grads["s5_c_re"] = _diag_blocks(dcr).reshape(p["s5_c_re"].shape)
    grads["s5_c_im"] = -_diag_blocks(dci).reshape(p["s5_c_im"].shape)
    grads["s5_d"] = dd.reshape(p["s5_d"].shape)
    grads["s5_b_glu"] = d_bglu.reshape(1, D_MODEL)
    dconv = jnp.stack([dconv0, dconv1]).reshape(2, N_DEV, 8, -1)
    grads["ffn_conv_w"] = dconv[:, :, 0:3, :].transpose(0, 2, 1, 3)
    grads["ffn_conv_b"] = dconv[:, :, 3, :]
    grads["ln_gain"] = jnp.stack([dg0, dg1, dg2, dg3]).reshape(2, 2, D_MODEL)
    grads["ln_bias"] = jnp.stack([db0, db1, db2, db3]).reshape(2, 2, D_MODEL)
    return loss_tile[0, 0], dx.reshape(x.shape), grads


SMALL_REPLICATED = ("s5_lam_re", "s5_lam_im", "s5_log_dt", "s5_b_re", "s5_b_im", "s5_c_re", "s5_c_im", "s5_d",
                    "rel_bias", "ffn_conv_b")
SMALL_SHARDED = ("s5_b_glu", "ffn_conv_w", "ln_gain", "ln_bias")
BIG = ("s5_w_glu", "s5_w_out", "attn_w_kv", "attn_w_q", "attn_w_out", "ffn_w_up", "ffn_w_down")
WEIGHTS = ("s5_lam_re", "s5_lam_im", "s5_log_dt", "s5_b_re", "s5_b_im", "s5_c_re", "s5_c_im", "s5_d", "s5_w_glu",
           "s5_b_glu", "s5_w_out", "attn_w_kv", "attn_w_q", "attn_w_out", "rel_bias", "ffn_w_up", "ffn_conv_w",
           "ffn_conv_b", "ffn_w_down", "ln_gain", "ln_bias")


class _Exchanges:
    def __init__(self, w):
        self.w = w
        self.parts = {}
        self.up = w["ffn_w_up"].astype(BF16)
        self.down = w["ffn_w_down"].astype(BF16)

    def gather_first(self):
        w = self.w
        srcs = [w["s5_w_glu"][0].astype(BF16), w["s5_w_out"][0].astype(BF16),
                _pack([w[k] for k in SMALL_SHARDED]), self.up, self.down, w["attn_w_kv"].astype(BF16),
                w["attn_w_q"][0].astype(BF16), w["attn_w_out"][0].astype(BF16)]
        outs = [_sds((N_DEV,) + (s.shape[1:] if i in (3, 4) else s.shape), s.dtype) for i, s in enumerate(srcs)]
        items = [(i, (0,) if i in (3, 4) else (), i, ("slot",)) for i in range(len(srcs))]
        return _gather_plan(srcs, outs, items)

    def take_first(self, g, p):
        d = D_MODEL
        w = self.w
        wfull = {
            "s5_w_glu": g[0].reshape(d, d),
            "s5_w_out": g[1].reshape(d, d),
            "ffn_w_up": [g[3], None],
            "ffn_w_down": [g[4].reshape(4, -1, d), None],
            "attn_w_kv": g[5],
            "attn_w_q": g[6],
            "attn_w_out": g[7].reshape(d, d),
        }
        smalls = [_unpack(g[2][dev], [w[k].shape for k in SMALL_SHARDED]) for dev in range(N_DEV)]
        c_ff = w["ffn_conv_w"].shape[2]
        conv_w = jnp.stack([s[1] for s in smalls], axis=2)
        p = dict(p)
        p.update(s5_b_glu=jnp.concatenate([s[0] for s in smalls], axis=1),
                 ln_gain=jnp.concatenate([s[2] for s in smalls], axis=2),
                 ln_bias=jnp.concatenate([s[3] for s in smalls], axis=2),
                 conv_w=conv_w.transpose(0, 2, 1, 3).reshape(2, 2, 4, 3, c_ff),
                 conv_b=w["ffn_conv_b"].reshape(2, 2, 4, 1, c_ff))
        return p, wfull

    def gather_layer1(self, which, half):
        src = self.up if which == "up" else self.down
        rows = src.shape[1] // 2
        return _gather_plan([src], [_sds((N_DEV, rows) + src.shape[2:], BF16)],
                            [(0, (1, pl.ds(half * rows, rows)), 0, ("slot",))])

    def take_layer1(self, halves, wfull):
        up = jnp.concatenate(halves["up"], axis=1)
        down = jnp.concatenate(halves["down"], axis=1)
        wfull = dict(wfull)
        wfull["ffn_w_up"] = [wfull["ffn_w_up"][0], up]
        wfull["ffn_w_down"] = [wfull["ffn_w_down"][0], down.reshape(4, -1, D_MODEL)]
        return wfull

    def scatter(self, grads, names):
        srcs = [grads[k].reshape(N_DEV, -1, grads[k].shape[-1]) for k in names]
        plan = _scatter_plan(srcs, [_sds(s.shape, BF16) for s in srcs],
                             [(i, i, ("slot",)) for i in range(len(names))])
        plan.names = names
        return plan

    def keep(self, plan, outs):
        for k, o in zip(plan.names, outs):
            self.parts[k] = o


def kernel(x, s5_lam_re, s5_lam_im, s5_log_dt, s5_b_re, s5_b_im, s5_c_re, s5_c_im, s5_d, s5_w_glu, s5_b_glu, s5_w_out, attn_w_kv, attn_w_q, attn_w_out, rel_bias, ffn_w_up, ffn_conv_w, ffn_conv_b, ffn_w_down, ln_gain, ln_bias, loss_target, m_s5_lam_re, m_s5_lam_im, m_s5_log_dt, m_s5_b_re, m_s5_b_im, m_s5_c_re, m_s5_c_im, m_s5_d, m_s5_w_glu, m_s5_b_glu, m_s5_w_out, m_attn_w_kv, m_attn_w_q, m_attn_w_out, m_rel_bias, m_ffn_w_up, m_ffn_conv_w, m_ffn_conv_b, m_ffn_w_down, m_ln_gain, m_ln_bias, v_s5_lam_re, v_s5_lam_im, v_s5_log_dt, v_s5_b_re, v_s5_b_im, v_s5_c_re, v_s5_c_im, v_s5_d, v_s5_w_glu, v_s5_b_glu, v_s5_w_out, v_attn_w_kv, v_attn_w_q, v_attn_w_out, v_rel_bias, v_ffn_w_up, v_ffn_conv_w, v_ffn_conv_b, v_ffn_w_down, v_ln_gain, v_ln_bias):
    w = dict(s5_lam_re=s5_lam_re, s5_lam_im=s5_lam_im, s5_log_dt=s5_log_dt, s5_b_re=s5_b_re, s5_b_im=s5_b_im,
             s5_c_re=s5_c_re, s5_c_im=s5_c_im, s5_d=s5_d, s5_w_glu=s5_w_glu, s5_b_glu=s5_b_glu, s5_w_out=s5_w_out,
             attn_w_kv=attn_w_kv, attn_w_q=attn_w_q, attn_w_out=attn_w_out, rel_bias=rel_bias, ffn_w_up=ffn_w_up,
             ffn_conv_w=ffn_conv_w, ffn_conv_b=ffn_conv_b, ffn_w_down=ffn_w_down, ln_gain=ln_gain, ln_bias=ln_bias)
    mom = dict(s5_lam_re=m_s5_lam_re, s5_lam_im=m_s5_lam_im, s5_log_dt=m_s5_log_dt, s5_b_re=m_s5_b_re,
               s5_b_im=m_s5_b_im, s5_c_re=m_s5_c_re, s5_c_im=m_s5_c_im, s5_d=m_s5_d, s5_w_glu=m_s5_w_glu,
               s5_b_glu=m_s5_b_glu, s5_w_out=m_s5_w_out, attn_w_kv=m_attn_w_kv, attn_w_q=m_attn_w_q,
               attn_w_out=m_attn_w_out, rel_bias=m_rel_bias, ffn_w_up=m_ffn_w_up, ffn_conv_w=m_ffn_conv_w,
               ffn_conv_b=m_ffn_conv_b, ffn_w_down=m_ffn_w_down, ln_gain=m_ln_gain, ln_bias=m_ln_bias)
    var = dict(s5_lam_re=v_s5_lam_re, s5_lam_im=v_s5_lam_im, s5_log_dt=v_s5_log_dt, s5_b_re=v_s5_b_re,
               s5_b_im=v_s5_b_im, s5_c_re=v_s5_c_re, s5_c_im=v_s5_c_im, s5_d=v_s5_d, s5_w_glu=v_s5_w_glu,
               s5_b_glu=v_s5_b_glu, s5_w_out=v_s5_w_out, attn_w_kv=v_attn_w_kv, attn_w_q=v_attn_w_q,
               attn_w_out=v_attn_w_out, rel_bias=v_rel_bias, ffn_w_up=v_ffn_w_up, ffn_conv_w=v_ffn_conv_w,
               ffn_conv_b=v_ffn_conv_b, ffn_w_down=v_ffn_w_down, ln_gain=v_ln_gain, ln_bias=v_ln_bias)
    bsz, seq, _ = x.shape
    me = 4 * lax.axis_index("x") + 2 * lax.axis_index("y") + lax.axis_index("c")

    ex = _Exchanges(w)
    loss_part, grad_x, g = _local_step(x, loss_target, dict(w), None, bsz, seq, ex=ex)
    loss = lax.psum(loss_part, ("x", "y", "c"))

    d = D_MODEL
    results = {}
    for name in BIG:
        shard = w[name]
        if name == "ffn_w_up":
            parts = [ex.parts[name + "0"], ex.parts[name + "1"]]
            outs4 = _adamw("adamw_" + name, parts, *(jnp.swapaxes(a, 1, 2) for a in (shard, mom[name], var[name])))
            results[name] = tuple(jnp.swapaxes(o, 1, 2) for o in outs4)
            continue
        if name == "ffn_w_down":
            parts, r3 = [ex.parts[name + "0"], ex.parts[name + "1"]], shard.shape
        else:
            parts, r3 = [ex.parts[name]], (1, -1, shard.shape[-1])
        outs4 = _adamw("adamw_" + name, parts, shard.reshape(r3), mom[name].reshape(r3), var[name].reshape(r3))
        results[name] = tuple(o.reshape(shard.shape) for o in outs4)

    small_names = SMALL_REPLICATED + SMALL_SHARDED
    packed = _pack([g[k] for k in small_names])
    pad = (-packed.shape[0]) % (8 * N_DEV)
    if pad:
        packed = jnp.concatenate([packed, jnp.zeros((pad, 128), F32)], axis=0)
    piece = packed.shape[0] // N_DEV
    slots = _exchange("reduce_small", [packed.reshape(N_DEV, piece, 128)], [_sds((N_DEV, piece, 128), F32)],
                      [(0, ("peer",), 0, ("me",))])[0]
    summed = _exchange("gather_small", [_sum_slots(slots)], [_sds((N_DEV, piece, 128), F32)],
                       [(0, (), 0, ("me",))])[0]
    total = _unpack(summed.reshape(N_DEV * piece, 128), [g[k].shape for k in small_names])
    gsum = dict(zip(small_names, total))
    mine = {k: gsum[k] for k in SMALL_REPLICATED}
    mine["s5_b_glu"] = lax.dynamic_slice_in_dim(gsum["s5_b_glu"], me * (d // N_DEV), d // N_DEV, axis=1)
    mine["ffn_conv_w"] = lax.dynamic_index_in_dim(gsum["ffn_conv_w"], me, axis=2, keepdims=False)
    mine["ln_gain"] = lax.dynamic_slice_in_dim(gsum["ln_gain"], me * (d // N_DEV), d // N_DEV, axis=2)
    mine["ln_bias"] = lax.dynamic_slice_in_dim(gsum["ln_bias"], me * (d // N_DEV), d // N_DEV, axis=2)
    gp = _pack([mine[k] for k in small_names])
    outs4 = _adamw("adamw_small", [gp[None]], _pack([w[k] for k in small_names])[None],
                   _pack([mom[k] for k in small_names])[None], _pack([var[k] for k in small_names])[None])
    outs4 = [o[0] for o in outs4]
    shapes = [w[k].shape for k in small_names]
    unpacked = [_unpack(o, shapes) for o in outs4]
    for i, k in enumerate(small_names):
        results[k] = tuple(unpacked[j][i] for j in range(4))

    out = [loss, grad_x]
    for j in range(4):
        out.extend(results[k][j] for k in WEIGHTS)
    return tuple(out)
```

```python
import functools
import math

import numpy as np
import jax
import jax.numpy as jnp
from jax import lax
from jax.experimental import pallas as pl
from jax.experimental.pallas import tpu as pltpu

F32 = jnp.float32
BF16 = jnp.bfloat16
HIGHEST = lax.Precision.HIGHEST

N_DEV = 8
D_MODEL = 1024
SSM_GROUP = 16
SSM_STATE = 64
SSM_CHUNKS = 8
CHUNK_CH = D_MODEL // SSM_CHUNKS
CHUNK_ST = CHUNK_CH // SSM_GROUP * SSM_STATE
N_STATE = D_MODEL // SSM_GROUP * SSM_STATE
HEAD_DIM = 64
N_HEADS = 16
BAND = 128
DILATIONS = (1, 4, 16)
REL_BUCKETS = 32
REL_MAX_DIST = 2048
NEG_BIG = -1e30
DN_ALPHA = (2.0 * 2) ** 0.25
LN_EPS = 1e-5
ADAM_LR = 0.001
ADAM_B1 = 0.9
ADAM_B2 = 0.999
ADAM_EPS = 1e-08
ADAM_WD = 0.01
ADAM_STEP = 10

TM = 512
TM_MM = 2048
TM_MM_F32 = 1024
T_SCAN = 512
VMEM_LIMIT = 56 * 1024 * 1024
FFN_HIDDEN = BF16

NN = (((1,), (0,)), ((), ()))
NT = (((1,), (1,)), ((), ()))
TN = (((0,), (0,)), ((), ()))


class _Plan:
    def __init__(self, srcs, out_shapes, n_sems, n_local, phases):
        self.srcs, self.out_shapes, self.n_sems, self.n_local, self.phases = srcs, out_shapes, n_sems, n_local, phases


def _call(body, name, grid, in_specs, out_specs, out_shape, scratch=(), comm=None, fill=()):
    params = dict(dimension_semantics=("arbitrary",) * len(grid), vmem_limit_bytes=VMEM_LIMIT)
    if comm is None and not fill:
        return pl.pallas_call(
            body, name=name, grid=grid, in_specs=in_specs, out_specs=out_specs, out_shape=out_shape,
            scratch_shapes=list(scratch), compiler_params=pltpu.CompilerParams(**params))
    single = not isinstance(out_specs, (list, tuple))
    o_specs = [out_specs] if single else list(out_specs)
    o_shape = [out_shape] if single else list(out_shape)
    n_in, n_out, n_scr, n_fill = len(in_specs), len(o_specs), len(scratch), len(fill)
    srcs = list(comm.srcs) if comm else []
    c_shapes = list(comm.out_shapes) if comm else []
    n_cin, n_cout = len(srcs), len(c_shapes)
    total = int(np.prod(grid))

    def wrapped(*refs):
        ins = refs[:n_in]
        refs = refs[n_in + n_fill:]
        cins, outs, couts = refs[:n_cin], refs[n_cin:n_cin + n_out], refs[n_cin + n_out:n_cin + n_out + n_cout]
        rest = refs[n_cin + n_out + n_cout:]
        scr, sems = rest[:n_scr], rest[n_scr:]
        if comm is None:
            body(*ins, *outs, *scr)
            return
        step = pl.program_id(0)
        for ax in range(1, len(grid)):
            step = step * grid[ax] + pl.program_id(ax)
        phases = comm.phases(cins, couts, *sems, total)
        for at, action in phases:
            if at == 0:
                pl.when(step == 0)(action)
        body(*ins, *outs, *scr)
        for at, action in phases:
            if at > 0:
                pl.when(step == at)(action)

    any_spec = pl.BlockSpec(memory_space=pl.ANY)
    sems = [] if comm is None else [pltpu.SemaphoreType.DMA((comm.n_sems,)), pltpu.SemaphoreType.DMA((comm.n_sems,)),
                                    pltpu.SemaphoreType.DMA((max(comm.n_local, 1),))]
    call = pl.pallas_call(
        wrapped, name=name, grid=grid, in_specs=list(in_specs) + [any_spec] * (n_fill + n_cin),
        out_specs=o_specs + [any_spec] * n_cout, out_shape=o_shape + c_shapes,
        scratch_shapes=list(scratch) + sems, input_output_aliases={n_in + j: j for j in range(n_fill)},
        compiler_params=pltpu.CompilerParams(has_side_effects=comm is not None, **params))

    def run(*args):
        res = call(*args, *fill, *srcs)
        own = res[0] if single else res[:n_out]
        return (own, res[n_out:]) if comm is not None else own

    return run


def _sds(shape, dtype):
    return jax.ShapeDtypeStruct(shape, dtype)


def _mm(name, a, b, *, dims, grid, a_spec, b_spec, o_spec, out_shape, nred=1, red_axis=0,
        add=None, add_spec=None, add_scale=1.0, comm=None, norm=None, inner=None, post=None):
    has_add = add is not None
    n_extra = (1 if has_add else 0) + (1 if post else 0) + (2 if norm else 0)
    n_out = 3 if norm else 1
    acc_shape = tuple(s for s in o_spec.block_shape if s is not None)

    def body(*refs):
        a_ref, b_ref = refs[:2]
        extra = refs[2:2 + n_extra]
        outs = refs[2 + n_extra:2 + n_extra + n_out]
        rest = refs[2 + n_extra + n_out:]
        o_ref = outs[0]
        if inner is None:
            prod = lax.dot_general(a_ref[...].astype(BF16), b_ref[...].astype(BF16), dims,
                                   preferred_element_type=F32)
        else:
            nb, a_mode = inner
            width = a_ref.shape[-1] // nb
            prod = None
            for d in range(nb):
                a_d = a_ref[d] if a_mode == "lead" else a_ref[:, d * width:(d + 1) * width]
                part = lax.dot_general(a_d.astype(BF16), b_ref[d].astype(BF16), dims, preferred_element_type=F32)
                prod = part if prod is None else prod + part

        def finish(val):
            if has_add:
                val = val + add_scale * extra[0][...].astype(F32)
            if post:
                val = post[0](val, extra[1 if has_add else 0][...])
            o_ref[...] = val.astype(o_ref.dtype)
            if norm:
                xhat, _ = _ln_stats(val)
                y = xhat * extra[-2][...] + extra[-1][...]
                outs[1][...] = y
                outs[2][...] = y.astype(BF16)

        if nred == 1:
            finish(prod)
        else:
            acc = rest[0]
            k = pl.program_id(red_axis)

            @pl.when(k == 0)
            def _():
                acc[...] = prod

            @pl.when(k > 0)
            def _():
                acc[...] += prod

            @pl.when(k == nred - 1)
            def _():
                finish(acc[...])

    in_specs = [a_spec, b_spec] + ([add_spec] if has_add else [])
    scratch = [pltpu.VMEM(acc_shape, F32)] if nred > 1 else []
    args = (a, b) + ((add,) if has_add else ())
    if post:
        in_specs.append(o_spec)
        args += (post[1],)
    if norm:
        vec = pl.BlockSpec((1, out_shape.shape[1]), lambda *_: (0, 0))
        in_specs += [vec, vec]
        args += tuple(norm)
        o_spec = [o_spec, o_spec, o_spec]
        out_shape = [out_shape, out_shape, _sds(out_shape.shape, BF16)]
    return _call(body, name, grid, in_specs, o_spec, out_shape, scratch, comm=comm)(*args)


def _mm_nn(name, a, b, out_dtype, blocked_out=False, comm=None, blocks=None, residual=None, norm=None):
    TM = TM_MM if out_dtype == BF16 else TM_MM_F32
    m, k = a.shape
    _, _, n = b.shape
    nb = b.shape[0] if blocks is None else blocks[1]
    first = 0 if blocks is None else blocks[0]
    if blocked_out:
        o_spec = pl.BlockSpec((None, TM, n), lambda d, i: (d, i, 0))
        out_shape = _sds((nb, m, n), out_dtype)
    else:
        o_spec = pl.BlockSpec((TM, n), lambda d, i: (i, d))
        out_shape = _sds((m, nb * n), out_dtype)
    return _mm(name, a, b, dims=NN, grid=(nb, m // TM),
               a_spec=pl.BlockSpec((TM, k), lambda d, i: (i, 0)),
               b_spec=pl.BlockSpec((None, k, n), lambda d, i: (d + first, 0, 0)),
               o_spec=o_spec, out_shape=out_shape, comm=comm, norm=norm,
               add=residual, add_spec=o_spec if residual is not None else None, add_scale=DN_ALPHA)


def _mm_nn_red(name, a, b, out_dtype, comm=None, residual=None, norm=None):
    nb, m, k = a.shape
    n = b.shape[2]
    o_spec = pl.BlockSpec((TM, n), lambda i: (i, 0))
    return _mm(name, a, b, dims=NN, grid=(m // TM,), inner=(nb, "lead"),
               a_spec=pl.BlockSpec((nb, TM, k), lambda i: (0, i, 0)),
               b_spec=pl.BlockSpec((nb, k, n), lambda i: (0, 0, 0)),
               o_spec=o_spec, out_shape=_sds((m, n), out_dtype), comm=comm, norm=norm,
               add=residual, add_spec=o_spec if residual is not None else None, add_scale=DN_ALPHA)


def _mm_tn(name, a, g, nb, out_dtype, a_blocked=False, g_mode="cols", comm=None):
    TM = TM_MM
    if a_blocked:
        _, m, ka = a.shape
        a_spec = pl.BlockSpec((None, TM, ka), lambda d, i: (d, i, 0))
    else:
        m, ka = a.shape
        a_spec = pl.BlockSpec((TM, ka), lambda d, i: (i, 0))
    if g_mode == "cols":
        n = g.shape[1] // nb
        g_spec = pl.BlockSpec((TM, n), lambda d, i: (i, d))
    elif g_mode == "blocked":
        n = g.shape[2]
        g_spec = pl.BlockSpec((None, TM, n), lambda d, i: (d, i, 0))
    else:
        n = g.shape[1]
        g_spec = pl.BlockSpec((TM, n), lambda d, i: (i, 0))
    nred = m // TM
    return _mm(name, a, g, dims=TN, grid=(nb, nred), nred=nred, red_axis=1,
               a_spec=a_spec, b_spec=g_spec,
               o_spec=pl.BlockSpec((None, ka, n), lambda d, i: (d, 0, 0)),
               out_shape=_sds((nb, ka, n), out_dtype), comm=comm)


def _mm_nt_red(name, g, w, out_dtype, g_blocked, add=None, add_scale=1.0, comm=None, blocks=None, post=None):
    _, k, n = w.shape
    nb = w.shape[0] if blocks is None else blocks[1]
    first = 0 if blocks is None else blocks[0]
    tm = TM_MM_F32 if nb * k * n <= 4 * 1024 * 768 else TM
    if g_blocked:
        m = g.shape[1]
        g_spec = pl.BlockSpec((nb, tm, n), lambda i: (0, i, 0))
    else:
        m = g.shape[0]
        g_spec = pl.BlockSpec((tm, nb * n), lambda i: (i, 0))
    o_spec = pl.BlockSpec((tm, k), lambda i: (i, 0))
    return _mm(name, g, w, dims=NT, grid=(m // tm,), inner=(nb, "lead" if g_blocked else "cols"),
               a_spec=g_spec, b_spec=pl.BlockSpec((nb, k, n), lambda i: (first // nb, 0, 0)),
               o_spec=o_spec, out_shape=_sds((m, k), out_dtype),
               add=add, add_spec=o_spec if add is not None else None, add_scale=add_scale, comm=comm, post=post)


def _mm_nt_out(name, g, w, out_dtype):
    TM = TM_MM
    m, n = g.shape
    nb, k, _ = w.shape
    return _mm(name, g, w, dims=NT, grid=(nb, m // TM),
               a_spec=pl.BlockSpec((TM, n), lambda d, i: (i, 0)),
               b_spec=pl.BlockSpec((None, k, n), lambda d, i: (d, 0, 0)),
               o_spec=pl.BlockSpec((None, TM, k), lambda d, i: (d, i, 0)),
               out_shape=_sds((nb, m, k), out_dtype))


def _ln_stats(r):
    mu = jnp.mean(r, axis=-1, keepdims=True)
    xc = r - mu
    var = jnp.mean(xc * xc, axis=-1, keepdims=True)
    rstd = lax.rsqrt(var + LN_EPS)
    return xc * rstd, rstd


def _ln_bwd(name, r, dy, gain):
    n, d = r.shape

    def body(r_ref, dy_ref, g_ref, dr_ref, drb_ref, dg_ref, db_ref):
        i = pl.program_id(0)
        xhat, rstd = _ln_stats(r_ref[...])
        dy_v = dy_ref[...]
        dxh = dy_v * g_ref[...]
        m1 = jnp.mean(dxh, axis=-1, keepdims=True)
        m2 = jnp.mean(dxh * xhat, axis=-1, keepdims=True)
        dr = rstd * (dxh - m1 - xhat * m2)
        dr_ref[...] = dr
        drb_ref[...] = dr.astype(BF16)
        dg = jnp.sum(dy_v * xhat, axis=0, keepdims=True)
        db = jnp.sum(dy_v, axis=0, keepdims=True)

        @pl.when(i == 0)
        def _():
            dg_ref[...] = dg
            db_ref[...] = db

        @pl.when(i > 0)
        def _():
            dg_ref[...] += dg
            db_ref[...] += db

    row = pl.BlockSpec((TM, d), lambda i: (i, 0))
    vec = pl.BlockSpec((1, d), lambda i: (0, 0))
    return _call(body, name, (n // TM,), [row, row, vec], [row, row, vec, vec],
                 [_sds((n, d), F32), _sds((n, d), BF16), _sds((1, d), F32), _sds((1, d), F32)])(r, dy, gain)


def _loss_head(y, target):
    n, d = y.shape

    def body(y_ref, t_ref, l_ref, dy_ref):
        i = pl.program_id(0)
        e = y_ref[...] - t_ref[...]
        dy_ref[...] = e * (1.0 / d)
        part = jnp.sum(jnp.sum(e * e, axis=1, keepdims=True), axis=0, keepdims=True) * (0.5 / d)
        part = jnp.broadcast_to(part, (8, 128))

        @pl.when(i == 0)
        def _():
            l_ref[...] = part

        @pl.when(i > 0)
        def _():
            l_ref[...] += part

    row = pl.BlockSpec((TM, d), lambda i: (i, 0))
    return _call(body, "loss_head", (n // TM,), [row, row],
                 [pl.BlockSpec((8, 128), lambda i: (0, 0)), row],
                 [_sds((8, 128), F32), _sds((n, d), F32)])(y, target)


def _group_expand_matrix():
    e = np.zeros((D_MODEL // SSM_GROUP, N_STATE), np.float32)
    for g in range(D_MODEL // SSM_GROUP):
        e[g, g * SSM_STATE:(g + 1) * SSM_STATE] = 1.0
    return jnp.asarray(e)


def _discretise(lr, li, ldt, br, bi, expand):
    dt = jnp.dot(jnp.exp(ldt), expand, precision=HIGHEST, preferred_element_type=F32)
    mag = jnp.exp(lr * dt)
    th = li * dt
    ab_r = mag * jnp.cos(th)
    ab_i = mag * jnp.sin(th)
    den = lr * lr + li * li
    nr = ab_r - 1.0
    co_r = (nr * lr + ab_i * li) / den
    co_i = (ab_i * lr - nr * li) / den
    bb_r = co_r * br - co_i * bi
    bb_i = co_r * bi + co_i * br
    return ab_r, ab_i, bb_r, bb_i


def _cmul(ar, ai, br, bi):
    return ar * br - ai * bi, ar * bi + ai * br


def _s5_prep(lr, li, ldt, br, bi):
    s = N_STATE
    expand = _group_expand_matrix()

    def body(lr_ref, li_ref, ldt_ref, br_ref, bi_ref, e_ref, abr_ref, abi_ref, bbr_ref, bbi_ref, tab_ref):
        ab_r, ab_i, bb_r, bb_i = _discretise(lr_ref[...], li_ref[...], ldt_ref[...], br_ref[...],
                                             bi_ref[...], e_ref[...])
        abr_ref[...] = ab_r
        abi_ref[...] = ab_i
        bbr_ref[...] = bb_r
        bbi_ref[...] = bb_i
        row = lax.broadcasted_iota(jnp.int32, (8, s), 0)
        for base, sign in ((0, 1.0), (8, -1.0)):
            p = [None] * 9
            p[1] = (ab_r, sign * ab_i)
            p[2] = _cmul(*p[1], *p[1])
            p[3] = _cmul(*p[2], *p[1])
            p[4] = _cmul(*p[2], *p[2])
            p[5] = _cmul(*p[4], *p[1])
            p[6] = _cmul(*p[4], *p[2])
            p[7] = _cmul(*p[4], *p[3])
            p[8] = _cmul(*p[4], *p[4])
            for j, k in enumerate((1, 2, 4)):
                keep = (row >= k) if base == 0 else (row < 8 - k)
                tab_ref[base + 2 * j] = jnp.where(keep, jnp.broadcast_to(p[k][0], (8, s)), 0.0)
                tab_ref[base + 2 * j + 1] = jnp.where(keep, jnp.broadcast_to(p[k][1], (8, s)), 0.0)
            pw_r = jnp.zeros((8, s), F32)
            pw_i = jnp.zeros((8, s), F32)
            for i in range(8):
                e = i + 1 if base == 0 else 8 - i
                pw_r = jnp.where(row == i, jnp.broadcast_to(p[e][0], (8, s)), pw_r)
                pw_i = jnp.where(row == i, jnp.broadcast_to(p[e][1], (8, s)), pw_i)
            tab_ref[base + 6] = pw_r
            tab_ref[base + 7] = pw_i

    def full(shape):
        return pl.BlockSpec(shape, lambda i: (0,) * len(shape))

    g = D_MODEL // SSM_GROUP
    return _call(body, "s5_prep", (1,),
                 [full((1, s)), full((1, s)), full((1, g)), full((16, s)), full((16, s)), full((g, s))],
                 [full((1, s)), full((1, s)), full((16, s)), full((16, s)), full((16, 8, s))],
                 [_sds((1, s), F32), _sds((1, s), F32), _sds((16, s), F32), _sds((16, s), F32),
                  _sds((16, 8, s), F32)])(lr, li, ldt, br, bi, expand)


def _s5_prep_bwd(lr, li, ldt, br, bi, d_abr, d_abi, d_bbr, d_bbi):
    s = N_STATE
    g = D_MODEL // SSM_GROUP
    expand = _group_expand_matrix()

    def body(lr_ref, li_ref, ldt_ref, br_ref, bi_ref, e_ref, c1, c2, c3, c4, o1, o2, o3, o4, o5):
        e_val = e_ref[...]
        _, vjp = jax.vjp(lambda a, b, c, d, e: _discretise(a, b, c, d, e, e_val),
                         lr_ref[...], li_ref[...], ldt_ref[...], br_ref[...], bi_ref[...])
        grads = vjp((c1[...], c2[...], c3[...], c4[...]))
        for o, gr in zip((o1, o2, o3, o4, o5), grads):
            o[...] = gr

    def full(shape):
        return pl.BlockSpec(shape, lambda i: (0,) * len(shape))

    return _call(body, "s5_prep_bwd", (1,),
                 [full((1, s)), full((1, s)), full((1, g)), full((16, s)), full((16, s)), full((g, s)),
                  full((1, s)), full((1, s)), full((16, s)), full((16, s))],
                 [full((1, s)), full((1, s)), full((1, g)), full((16, s)), full((16, s))],
                 [_sds((1, s), F32), _sds((1, s), F32), _sds((1, g), F32), _sds((16, s), F32),
                  _sds((16, s), F32)])(lr, li, ldt, br, bi, expand, d_abr, d_abi, d_bbr, d_bbi)


def _scan8(vr, vi, tab_ref, base, reverse):
    for j, k in enumerate((1, 2, 4)):
        mr = tab_ref[base + 2 * j]
        mi = tab_ref[base + 2 * j + 1]
        shift = 8 - k if reverse else k
        sr = pltpu.roll(vr, shift, 0)
        si = pltpu.roll(vi, shift, 0)
        vr, vi = vr + mr * sr - mi * si, vi + mr * si + mi * sr
    return vr, vi


def _s5_scan_fwd(x, bbr, bbi, cr, ci, dvec, tab, seq, comm=None):
    n, d = x.shape
    nt = seq // T_SCAN
    nblk = T_SCAN // 8
    st = CHUNK_ST

    def body(x_ref, bbr_ref, bbi_ref, cr_ref, ci_ref, d_ref, tab_ref, y_ref, hr_ref, hi_ref, car_r, car_i):
        t = pl.program_id(2)

        @pl.when(t == 0)
        def _():
            car_r[...] = jnp.zeros_like(car_r)
            car_i[...] = jnp.zeros_like(car_i)

        u = x_ref[...]
        ub = u.astype(BF16)
        hr_ref[...] = jnp.dot(ub, bbr_ref[...], preferred_element_type=F32)
        hi_ref[...] = jnp.dot(ub, bbi_ref[...], preferred_element_type=F32)

        def step(i, carry):
            sl = pl.ds(pl.multiple_of(i * 8, 8), 8)
            vr, vi = _scan8(hr_ref[sl, :], hi_ref[sl, :], tab_ref, 0, False)
            c_r = jnp.broadcast_to(car_r[...], (8, st))
            c_i = jnp.broadcast_to(car_i[...], (8, st))
            pr = tab_ref[6]
            pi = tab_ref[7]
            vr, vi = vr + pr * c_r - pi * c_i, vi + pr * c_i + pi * c_r
            hr_ref[sl, :] = vr
            hi_ref[sl, :] = vi
            car_r[...] = vr[7:8, :]
            car_i[...] = vi[7:8, :]
            return carry

        lax.fori_loop(0, nblk, step, 0, unroll=4)
        y = lax.dot_general(hr_ref[...].astype(BF16), cr_ref[...], NT, preferred_element_type=F32)
        y = y - lax.dot_general(hi_ref[...].astype(BF16), ci_ref[...], NT, preferred_element_type=F32)
        y_ref[...] = y + d_ref[...] * u

    row = lambda c, b, t: (b * nt + t, c)
    mat = pl.BlockSpec((None, CHUNK_CH, st), lambda c, b, t: (c, 0, 0))
    return _call(
        body, "s5_scan_fwd", (SSM_CHUNKS, n // seq, nt),
        [pl.BlockSpec((T_SCAN, CHUNK_CH), row), mat, mat, mat, mat,
         pl.BlockSpec((1, CHUNK_CH), lambda c, b, t: (0, c)),
         pl.BlockSpec((16, 8, st), lambda c, b, t: (0, 0, c))],
        [pl.BlockSpec((T_SCAN, CHUNK_CH), row), pl.BlockSpec((T_SCAN, st), row), pl.BlockSpec((T_SCAN, st), row)],
        [_sds((n, d), F32), _sds((n, N_STATE), F32), _sds((n, N_STATE), F32)],
        [pltpu.VMEM((1, st), F32), pltpu.VMEM((1, st), F32)], comm=comm,
    )(x, bbr, bbi, cr, ci, dvec, tab)


def _s5_scan_bwd(x, dy, dres, hr, hi, bbr, bbi, cr, ci, dvec, tab, seq, comm=None):
    n, d = x.shape
    nt = seq // T_SCAN
    nblk = T_SCAN // 8
    st = CHUNK_ST

    def body(x_ref, dy_ref, dres_ref, hr_ref, hi_ref, hpr_ref, hpi_ref, bbr_ref, bbi_ref, cr_ref, ci_ref,
             d_ref, tab_ref, dx_ref, dcr_ref, dci_ref, dbr_ref, dbi_ref, dar_ref, dai_ref, dd_ref,
             g_r, g_i, car_r, car_i):
        b = pl.program_id(1)
        tg = pl.program_id(2)
        first = jnp.logical_and(b == 0, tg == 0)

        @pl.when(tg == 0)
        def _():
            car_r[...] = jnp.zeros_like(car_r)
            car_i[...] = jnp.zeros_like(car_i)

        u = x_ref[...]
        dyv = dy_ref[...]
        ub = u.astype(BF16)
        dyb = dyv.astype(BF16)
        g_r[...] = jnp.dot(dyb, cr_ref[...], preferred_element_type=F32)
        g_i[...] = -jnp.dot(dyb, ci_ref[...], preferred_element_type=F32)

        def step(ii, carry):
            i = nblk - 1 - ii
            sl = pl.ds(pl.multiple_of(i * 8, 8), 8)
            vr, vi = _scan8(g_r[sl, :], g_i[sl, :], tab_ref, 8, True)
            c_r = jnp.broadcast_to(car_r[...], (8, st))
            c_i = jnp.broadcast_to(car_i[...], (8, st))
            pr = tab_ref[14]
            pi = tab_ref[15]
            vr, vi = vr + pr * c_r - pi * c_i, vi + pr * c_i + pi * c_r
            g_r[sl, :] = vr
            g_i[sl, :] = vi
            car_r[...] = vr[0:1, :]
            car_i[...] = vi[0:1, :]
            return carry

        lax.fori_loop(0, nblk, step, 0, unroll=4)
        gr = g_r[...]
        gi = g_i[...]
        grb = gr.astype(BF16)
        gib = gi.astype(BF16)
        du = lax.dot_general(grb, bbr_ref[...], NT, preferred_element_type=F32)
        du = du + lax.dot_general(gib, bbi_ref[...], NT, preferred_element_type=F32)
        dx_ref[...] = DN_ALPHA * dres_ref[...] + du + d_ref[...] * dyv

        h_r = hr_ref[...]
        h_i = hi_ref[...]
        t_orig = nt - 1 - tg
        row0 = lax.broadcasted_iota(jnp.int32, (8, st), 0) == 0
        halo_ok = t_orig > 0

        def previous(h, halo_ref):
            top = jnp.broadcast_to(jnp.where(halo_ok, halo_ref[7:8, :], 0.0), (8, st))
            rolled = pltpu.roll(h, 1, 0)
            return jnp.concatenate([jnp.where(row0, top, rolled[:8, :]), rolled[8:, :]], axis=0)

        hp_r = previous(h_r, hpr_ref)
        hp_i = previous(h_i, hpi_ref)
        dar = jnp.sum(gr * hp_r + gi * hp_i, axis=0, keepdims=True)
        dai = jnp.sum(gi * hp_r - gr * hp_i, axis=0, keepdims=True)
        dcr = lax.dot_general(dyb, h_r.astype(BF16), TN, preferred_element_type=F32)
        dci = lax.dot_general(dyb, h_i.astype(BF16), TN, preferred_element_type=F32)
        dbr = lax.dot_general(ub, grb, TN, preferred_element_type=F32)
        dbi = lax.dot_general(ub, gib, TN, preferred_element_type=F32)
        dd = jnp.sum(dyv * u, axis=0, keepdims=True)

        @pl.when(first)
        def _():
            dcr_ref[...] = dcr
            dci_ref[...] = dci
            dbr_ref[...] = dbr
            dbi_ref[...] = dbi
            dar_ref[...] = dar
            dai_ref[...] = dai
            dd_ref[...] = dd

        @pl.when(jnp.logical_not(first))
        def _():
            dcr_ref[...] += dcr
            dci_ref[...] += dci
            dbr_ref[...] += dbr
            dbi_ref[...] += dbi
            dar_ref[...] += dar
            dai_ref[...] += dai
            dd_ref[...] += dd

    row = lambda c, b, t: (b * nt + (nt - 1 - t), c)
    halo = lambda c, b, t: (jnp.maximum((b * nt + (nt - 1 - t)) * (T_SCAN // 8) - 1, 0), c)
    mat = pl.BlockSpec((None, CHUNK_CH, st), lambda c, b, t: (c, 0, 0))
    chan = pl.BlockSpec((T_SCAN, CHUNK_CH), row)
    state = pl.BlockSpec((T_SCAN, st), row)
    return _call(
        body, "s5_scan_bwd", (SSM_CHUNKS, n // seq, nt),
        [chan, chan, chan, state, state, pl.BlockSpec((8, st), halo), pl.BlockSpec((8, st), halo),
         mat, mat, mat, mat, pl.BlockSpec((1, CHUNK_CH), lambda c, b, t: (0, c)),
         pl.BlockSpec((16, 8, st), lambda c, b, t: (0, 0, c))],
        [chan, mat, mat, mat, mat, pl.BlockSpec((1, st), lambda c, b, t: (0, c)),
         pl.BlockSpec((1, st), lambda c, b, t: (0, c)), pl.BlockSpec((1, CHUNK_CH), lambda c, b, t: (0, c))],
        [_sds((n, d), F32)] + [_sds((SSM_CHUNKS, CHUNK_CH, st), F32)] * 4
        + [_sds((1, N_STATE), F32), _sds((1, N_STATE), F32), _sds((1, d), F32)],
        [pltpu.VMEM((T_SCAN, st), F32), pltpu.VMEM((T_SCAN, st), F32), pltpu.VMEM((1, st), F32),
         pltpu.VMEM((1, st), F32)], comm=comm,
    )(x, dy, dres, hr, hi, hr, hi, bbr, bbi, cr, ci, dvec, tab)


_GELU_C = math.sqrt(2.0 / math.pi)


def _gelu_parts(y):
    t = jnp.tanh(_GELU_C * (y + 0.044715 * (y * y * y)))
    return 0.5 * (1.0 + t), t


def _glu_fwd(y, w_glu, b_glu):
    n, d = y.shape

    def body(y_ref, w_ref, b_ref, y2_ref, z_ref, gg_ref):
        yv = y_ref[...]
        cdf, _ = _gelu_parts(yv)
        y2 = yv * cdf
        z = jnp.dot(y2.astype(BF16), w_ref[...], preferred_element_type=F32) + b_ref[...]
        y2_ref[...] = y2.astype(BF16)
        z_ref[...] = z
        gg_ref[...] = (y2 * jax.nn.sigmoid(z)).astype(BF16)

    row = pl.BlockSpec((TM, d), lambda i: (i, 0))
    return _call(body, "glu_fwd", (n // TM,),
                 [row, pl.BlockSpec((d, d), lambda i: (0, 0)), pl.BlockSpec((1, d), lambda i: (0, 0))],
                 [row, row, row], [_sds((n, d), BF16), _sds((n, d), F32), _sds((n, d), BF16)])(y, w_glu, b_glu)


def _glu_bwd_gate(y, z, dgg):
    n, d = y.shape

    def body(y_ref, z_ref, dgg_ref, dz_ref, db_ref, t_ref):
        i = pl.program_id(0)
        yv = y_ref[...]
        cdf, _ = _gelu_parts(yv)
        y2 = yv * cdf
        s = jax.nn.sigmoid(z_ref[...])
        dg = dgg_ref[...]
        dz = dg * y2 * s * (1.0 - s)
        dz_ref[...] = dz.astype(BF16)
        t_ref[...] = dg * s
        db = jnp.sum(dz, axis=0, keepdims=True)

        @pl.when(i == 0)
        def _():
            db_ref[...] = db

        @pl.when(i > 0)
        def _():
            db_ref[...] += db

    row = pl.BlockSpec((TM, d), lambda i: (i, 0))
    vec = pl.BlockSpec((1, d), lambda i: (0, 0))
    return _call(body, "glu_bwd_gate", (n // TM,), [row, row, row], [row, vec, row],
                 [_sds((n, d), BF16), _sds((1, d), F32), _sds((n, d), F32)])(y, z, dgg)


def _gelu_bwd(dy2, y):
    cdf, t = _gelu_parts(y)
    dinner = _GELU_C * (1.0 + 3.0 * 0.044715 * y * y)
    return dy2 * (cdf + 0.5 * y * (1.0 - t * t) * dinner)


HALO = 16


def _shift_down(x, halo, k):
    out = pltpu.roll(x, k, 0)
    row = lax.broadcasted_iota(jnp.int32, (8, x.shape[1]), 0)
    top = jnp.where(row < k, pltpu.roll(halo[HALO - 8:HALO, :], k, 0), out[:8, :])
    return jnp.concatenate([top, out[8:, :]], axis=0)


def _conv3(x, halo, w, b):
    x1 = _shift_down(x, halo, 1)
    x2 = _shift_down(x, halo, 2)
    return b + w[0:1, :] * x + w[1:2, :] * x1 + w[2:3, :] * x2, x1, x2


def _conv_gate_fwd(hc, conv_w, conv_b, seq):
    _, nb, n, c = hc.shape
    tiles_per_seq = seq // TM

    def body(x_ref, halo_ref, w_ref, b_ref, o_ref, cv_ref):
        i = pl.program_id(1)
        start = (i % tiles_per_seq) == 0
        cv = []
        for h in range(2):
            halo = jnp.where(start, 0.0, halo_ref[h].astype(F32))
            cv.append(_conv3(x_ref[h].astype(F32), halo, w_ref[h], b_ref[h])[0])
            cv_ref[h] = cv[h].astype(BF16)
        o_ref[...] = (cv[1] * jax.nn.sigmoid(cv[1]) * cv[0]).astype(BF16)

    both = pl.BlockSpec((2, None, TM, c), lambda d, i: (0, d, i, 0))
    return _call(
        body, "conv_gate_fwd", (nb, n // TM),
        [both,
         pl.BlockSpec((2, None, HALO, c), lambda d, i: (0, d, jnp.maximum(i * (TM // HALO) - 1, 0), 0)),
         pl.BlockSpec((2, None, 3, c), lambda d, i: (0, d, 0, 0)),
         pl.BlockSpec((2, None, 1, c), lambda d, i: (0, d, 0, 0))],
        [pl.BlockSpec((None, TM, c), lambda d, i: (d, i, 0)), both],
        [_sds((nb, n, c), BF16), _sds((2, nb, n, c), BF16)])(hc, hc, conv_w, conv_b)


def _conv_gate_bwd(hc, cv, dact, conv_w, seq, comm=None):
    _, nb, n, c = hc.shape
    tiles_per_seq = seq // TM
    last_halo = n // HALO - 1
    ext = TM + HALO

    def body(x_ref, cv_ref, cvn_ref, da_ref, dan_ref, w_ref, dx_ref, dw_ref):
        i = pl.program_id(1)
        end = (i % tiles_per_seq) == tiles_per_seq - 1
        row = lax.broadcasted_iota(jnp.int32, (ext, c), 0)
        cv = [jnp.concatenate([cv_ref[h], cvn_ref[h]], axis=0).astype(F32) for h in range(2)]
        da = jnp.concatenate([da_ref[...], dan_ref[...]], axis=0).astype(F32)
        da = jnp.where(jnp.logical_and(end, row >= TM), 0.0, da)
        sg = jax.nn.sigmoid(cv[1])
        silu = cv[1] * sg
        dcv = [da * silu, da * cv[0] * (sg + silu * (1.0 - sg))]
        for h in range(2):
            w = w_ref[h]
            g = dcv[h]
            g0 = g[:TM, :]
            g1 = pltpu.roll(g, ext - 1, 0)[:TM, :]
            g2 = pltpu.roll(g, ext - 2, 0)[:TM, :]
            dx_ref[h] = (w[0:1, :] * g0 + w[1:2, :] * g1 + w[2:3, :] * g2).astype(BF16)
            x = x_ref[h].astype(F32)
            parts = [jnp.sum(g0 * x, axis=0, keepdims=True),
                     jnp.sum(g1 * x, axis=0, keepdims=True),
                     jnp.sum(g2 * x, axis=0, keepdims=True),
                     jnp.sum(g0, axis=0, keepdims=True)]
            upd = jnp.concatenate(parts + [jnp.zeros((4, c), F32)], axis=0)

            @pl.when(i == 0)
            def _():
                dw_ref[h] = upd

            @pl.when(i > 0)
            def _():
                dw_ref[h] += upd

    nxt = lambda i: jnp.minimum((i + 1) * (TM // HALO), last_halo)
    return _call(
        body, "conv_gate_bwd", (nb, n // TM),
        [pl.BlockSpec((2, None, TM, c), lambda d, i: (0, d, i, 0)),
         pl.BlockSpec((2, None, TM, c), lambda d, i: (0, d, i, 0)),
         pl.BlockSpec((2, None, HALO, c), lambda d, i: (0, d, nxt(i), 0)),
         pl.BlockSpec((None, TM, c), lambda d, i: (d, i, 0)),
         pl.BlockSpec((None, HALO, c), lambda d, i: (d, nxt(i), 0)),
         pl.BlockSpec((2, None, 3, c), lambda d, i: (0, d, 0, 0))],
        [pl.BlockSpec((2, None, TM, c), lambda d, i: (0, d, i, 0)),
         pl.BlockSpec((2, None, 8, c), lambda d, i: (0, d, 0, 0))],
        [_sds((2, nb, n, c), BF16), _sds((2, nb, 8, c), F32)], comm=comm)(hc, cv, cv, dact, dact, conv_w)


def _t5_bucket_np(dist):
    exact = REL_BUCKETS // 2
    d = np.maximum(dist, 1).astype(np.float32)
    large = exact + (np.log(d / exact) / math.log(REL_MAX_DIST / exact) * (REL_BUCKETS - exact)).astype(np.int64)
    large = np.minimum(large, REL_BUCKETS - 1)
    return np.where(dist < exact, dist, large).astype(np.int32)


def _bucket_table(dil):
    steps = np.arange(BAND)[:, None] + BAND - np.arange(2 * BAND)[None, :]
    return _t5_bucket_np(np.maximum(steps, 0) * dil).reshape(1, BAND * 2 * BAND).astype(np.float32)


def _bias_expand(rel_t, dil):
    nk = BAND * 2 * BAND
    bucket = jnp.asarray(_bucket_table(dil))

    def body(r_ref, bk_ref, o_ref):
        ids = lax.broadcasted_iota(jnp.int32, (REL_BUCKETS, nk), 0).astype(F32)
        onehot = (ids == bk_ref[...]).astype(F32)
        o_ref[...] = jnp.dot(r_ref[...], onehot, precision=HIGHEST, preferred_element_type=F32)

    return _call(body, "bias_expand", (1,),
                 [pl.BlockSpec((N_HEADS, REL_BUCKETS), lambda i: (0, 0)), pl.BlockSpec((1, nk), lambda i: (0, 0))],
                 pl.BlockSpec((N_HEADS, nk), lambda i: (0, 0)), _sds((N_HEADS, nk), F32))(rel_t, bucket)


def _bias_reduce(dbias, dil):
    nk = BAND * 2 * BAND
    bucket = jnp.asarray(_bucket_table(dil))

    def body(g_ref, bk_ref, o_ref):
        ids = lax.broadcasted_iota(jnp.int32, (REL_BUCKETS, nk), 0).astype(F32)
        onehot = (ids == bk_ref[...]).astype(F32)
        o_ref[...] = lax.dot_general(g_ref[...], onehot, NT, precision=HIGHEST, preferred_element_type=F32)

    return _call(body, "bias_reduce", (1,),
                 [pl.BlockSpec((N_HEADS, nk), lambda i: (0, 0)), pl.BlockSpec((1, nk), lambda i: (0, 0))],
                 pl.BlockSpec((N_HEADS, REL_BUCKETS), lambda i: (0, 0)),
                 _sds((N_HEADS, REL_BUCKETS), F32))(dbias, bucket)


def _valid_mask(n):
    qi = lax.broadcasted_iota(jnp.int32, (BAND, 2 * BAND), 0)
    kj = lax.broadcasted_iota(jnp.int32, (BAND, 2 * BAND), 1)
    steps = qi + BAND - kj
    in_band = jnp.logical_and(steps >= 0, steps <= BAND)
    return jnp.logical_and(in_band, jnp.logical_or(n > 0, kj >= BAND))


ATTN_COLS = {1: 1024, 4: 512, 16: 128}


def _residue(dil, r):
    return slice(None) if dil == 1 else pl.ds(r, BAND, stride=dil)


def _stack_heads(x, first):
    return jnp.concatenate([jnp.where(first, x, 0.0), jnp.where(first, 0.0, x)], axis=0)


def _attn_fwd(q, k, v, bias, g, dil, bsz, seq, comm=None):
    n = q.shape[0]
    rows = BAND * dil
    nch = seq // rows
    w = N_HEADS * HEAD_DIM
    wc = ATTN_COLS[dil]
    ncol = w // wc
    hpc = wc // HEAD_DIM
    nj = wc // 128
    slab = lambda k: pltpu.VMEM((k * nj, rows, 128), F32)

    def body(q_ref, kc_ref, kp_ref, vc_ref, vp_ref, b_ref, o_ref, l_ref, qf, kf, vf, of, lf):
        ch = pl.program_id(2)
        valid = _valid_mask(ch)
        valid = jnp.concatenate([valid, valid], axis=0)
        lane = lax.broadcasted_iota(jnp.int32, (BAND, 128), 1)
        first = lane < HEAD_DIM
        for j in range(nj):
            sl = slice(128 * j, 128 * (j + 1))
            qf[j] = q_ref[:, sl].astype(F32)
            kf[j] = kp_ref[:, sl].astype(F32)
            kf[nj + j] = kc_ref[:, sl].astype(F32)
            vf[j] = vp_ref[:, sl].astype(F32)
            vf[nj + j] = vc_ref[:, sl].astype(F32)
        for r in range(dil):
            rr = _residue(dil, r)
            for j in range(nj):
                q2 = qf.at[j][rr, :]
                k2 = jnp.concatenate([kf.at[j][rr, :], kf.at[nj + j][rr, :]], axis=0).astype(BF16)
                v2 = jnp.concatenate([vf.at[j][rr, :], vf.at[nj + j][rr, :]], axis=0).astype(BF16)
                qs = _stack_heads(q2, first).astype(BF16)
                s = lax.dot_general(qs, k2, NT, preferred_element_type=F32) * (HEAD_DIM ** -0.5)
                s = jnp.where(valid, s + b_ref[2 * j:2 * j + 2].reshape(2 * BAND, 2 * BAND), NEG_BIG)
                m = jnp.max(s, axis=1, keepdims=True)
                p = jnp.exp(s - m)
                l = jnp.sum(p, axis=1, keepdims=True)
                pv = jnp.dot(p.astype(BF16), v2, preferred_element_type=F32) / l
                lse = jnp.broadcast_to(m + jnp.log(l), (2 * BAND, 128))
                of.at[j][rr, :] = jnp.where(first, pv[:BAND], pv[BAND:])
                lf.at[j][rr, :] = jnp.where(first, lse[:BAND], lse[BAND:])
        for j in range(nj):
            sl = slice(128 * j, 128 * (j + 1))
            o_ref[:, sl] = of[j]
            l_ref[:, sl] = lf[j]

    base = g * ncol
    cur = lambda c, b, ch: (b * nch + ch, base + c)
    prev = lambda c, b, ch: (b * nch + jnp.maximum(ch - 1, 0), base + c)
    blk = lambda im: pl.BlockSpec((rows, wc), im)
    out_blk = pl.BlockSpec((rows, wc), lambda c, b, ch: (b * nch + ch, c))
    return _call(
        body, "attn_fwd_d%d" % dil, (ncol, bsz, nch),
        [blk(cur), blk(cur), blk(prev), blk(cur), blk(prev),
         pl.BlockSpec((hpc, BAND, 2 * BAND), lambda c, b, ch: (c, 0, 0))],
        [out_blk, out_blk],
        [_sds((n, w), F32)] * 2,
        [slab(1), slab(2), slab(2), slab(1), slab(1)], comm=comm,
    )(q, k, k, v, v, bias)


def _attn_combine(outs, lses):
    n, w = outs[0].shape

    def body(o0, o1, o2, l0, l1, l2, o_ref, l_ref):
        la, lb, lc = l0[...], l1[...], l2[...]
        m = jnp.maximum(jnp.maximum(la, lb), lc)
        wa, wb, wc = jnp.exp(la - m), jnp.exp(lb - m), jnp.exp(lc - m)
        tot = wa + wb + wc
        o_ref[...] = (wa * o0[...] + wb * o1[...] + wc * o2[...]) / tot
        l_ref[...] = m + jnp.log(tot)

    row = pl.BlockSpec((TM, w), lambda i: (i, 0))
    return _call(body, "attn_combine", (n // TM,), [row] * 6, [row, row],
                 [_sds((n, w), F32)] * 2)(*outs, *lses)


def _attn_bwd(q, k, v, do, o, lse, bias, g, dil, bsz, seq, comm=None, into=()):
    n = q.shape[0]
    rows = BAND * dil
    nch = seq // rows
    w = N_HEADS * HEAD_DIM
    wc = ATTN_COLS[dil]
    ncol = w // wc
    hpc = wc // HEAD_DIM
    nj = wc // 128
    slab = lambda k: pltpu.VMEM((k * nj, rows, 128), F32)
    scale = HEAD_DIM ** -0.5

    def body(q_ref, kc_ref, kp_ref, vc_ref, vp_ref, do_ref, o_ref, l_ref, b_ref,
             dq_ref, dk_ref, dv_ref, db_ref, qf, kf, vf, dof, of, lf, dqf, dkf, dvf):
        b = pl.program_id(1)
        ch = pl.program_id(2)
        cur = ch % 2
        prv = 1 - cur

        @pl.when(jnp.logical_and(b == 0, ch == 0))
        def _():
            db_ref[...] = jnp.zeros_like(db_ref)

        @pl.when(ch == 0)
        def _():
            for j in range(nj):
                dkf[nj + j] = jnp.zeros((rows, 128), F32)
                dvf[nj + j] = jnp.zeros((rows, 128), F32)

        @pl.when(ch < nch)
        def _():
            valid = _valid_mask(ch)
            valid = jnp.concatenate([valid, valid], axis=0)
            first = lax.broadcasted_iota(jnp.int32, (BAND, 128), 1) < HEAD_DIM
            for j in range(nj):
                sl = slice(128 * j, 128 * (j + 1))
                qf[j] = q_ref[:, sl].astype(F32)
                kf[j] = kp_ref[:, sl].astype(F32)
                kf[nj + j] = kc_ref[:, sl].astype(F32)
                vf[j] = vp_ref[:, sl].astype(F32)
                vf[nj + j] = vc_ref[:, sl].astype(F32)
                dof[j] = do_ref[:, sl]
                of[j] = o_ref[:, sl]
                lf[j] = l_ref[:, sl]
            for r in range(dil):
                rr = _residue(dil, r)
                for j in range(nj):
                    k2 = jnp.concatenate([kf.at[j][rr, :], kf.at[nj + j][rr, :]], axis=0).astype(BF16)
                    v2 = jnp.concatenate([vf.at[j][rr, :], vf.at[nj + j][rr, :]], axis=0).astype(BF16)
                    do2 = dof.at[j][rr, :]
                    lse2 = lf.at[j][rr, :]
                    qs = _stack_heads(qf.at[j][rr, :], first).astype(BF16)
                    dos = _stack_heads(do2, first)
                    delta = jnp.sum(dos * jnp.concatenate([of.at[j][rr, :]] * 2, axis=0), axis=1, keepdims=True)
                    dos = dos.astype(BF16)
                    lse_col = jnp.concatenate([lse2[:, 0:1], lse2[:, HEAD_DIM:HEAD_DIM + 1]], axis=0)
                    s = lax.dot_general(qs, k2, NT, preferred_element_type=F32) * scale
                    s = jnp.where(valid, s + b_ref[2 * j:2 * j + 2].reshape(2 * BAND, 2 * BAND), NEG_BIG)
                    p = jnp.exp(s - lse_col)
                    dp = lax.dot_general(dos, v2, NT, preferred_element_type=F32)
                    ds = p * (dp - delta)
                    db_ref[2 * j:2 * j + 2] += ds.reshape(2, BAND, 2 * BAND)
                    dsb = ds.astype(BF16)
                    dq2 = jnp.dot(dsb, k2, preferred_element_type=F32) * scale
                    dk2 = lax.dot_general(dsb, qs, TN, preferred_element_type=F32) * scale
                    dv2 = lax.dot_general(p.astype(BF16), dos, TN, preferred_element_type=F32)
                    dqf.at[j][rr, :] = jnp.where(first, dq2[:BAND], dq2[BAND:])
                    dkf.at[prv * nj + j][rr, :] += dk2[:BAND, :]
                    dvf.at[prv * nj + j][rr, :] += dv2[:BAND, :]
                    dkf.at[cur * nj + j][rr, :] = dk2[BAND:, :]
                    dvf.at[cur * nj + j][rr, :] = dv2[BAND:, :]
            for j in range(nj):
                dq_ref[:, 128 * j:128 * (j + 1)] = dqf[j].astype(BF16)

        for j in range(nj):
            sl = slice(128 * j, 128 * (j + 1))
            dk_ref[:, sl] = dkf[prv * nj + j].astype(BF16)
            dv_ref[:, sl] = dvf[prv * nj + j].astype(BF16)

    last = nch - 1
    base = g * ncol
    cur3 = lambda c, b, ch: (b * nch + jnp.minimum(ch, last), base + c)
    prev3 = lambda c, b, ch: (b * nch + jnp.maximum(jnp.minimum(ch, last) - 1, 0), base + c)
    cur1 = lambda c, b, ch: (b * nch + jnp.minimum(ch, last), c)
    lag3 = lambda c, b, ch: (b * nch + jnp.maximum(ch - 1, 0), base + c)
    blk = lambda im: pl.BlockSpec((rows, wc), im)
    bias_blk = pl.BlockSpec((hpc, BAND, 2 * BAND), lambda c, b, ch: (c, 0, 0))
    out = _sds(q.shape, BF16)
    return _call(
        body, "attn_bwd_d%d" % dil, (ncol, bsz, nch + 1),
        [blk(cur3), blk(cur3), blk(prev3), blk(cur3), blk(prev3), blk(cur1), blk(cur1), blk(cur1), bias_blk],
        [blk(cur3), blk(lag3), blk(lag3), bias_blk],
        [out, out, out, _sds((N_HEADS, BAND, 2 * BAND), F32)],
        [slab(1), slab(2), slab(2), slab(1), slab(1), slab(1), slab(1), slab(2), slab(2)], comm=comm, fill=into,
    )(q, k, k, v, v, do, o, lse, bias)


def _adamw(name, parts, w, m, v):
    nl, r, c = w.shape
    nparts = parts[0].shape[0]
    tr = r
    for cand in (256, 352, 128):
        if r % cand == 0 and r > cand:
            tr = cand
            break
    c1 = 1.0 - ADAM_B1 ** ADAM_STEP
    c2 = 1.0 - ADAM_B2 ** ADAM_STEP

    def body(*refs):
        p_refs = refs[:nl]
        w_ref, m_ref, v_ref, g_ref, d_ref, nm_ref, nv_ref = refs[nl:]
        layer = pl.program_id(0)
        for l in range(nl):
            @pl.when(layer == l)
            def _():
                g = p_refs[l][0].astype(F32)
                for dev in range(1, nparts):
                    g = g + p_refs[l][dev].astype(F32)
                nm = ADAM_B1 * m_ref[...] + (1.0 - ADAM_B1) * g
                nv = ADAM_B2 * v_ref[...] + (1.0 - ADAM_B2) * (g * g)
                m_hat = nm / c1
                v_hat = nv / c2
                g_ref[...] = g
                d_ref[...] = -ADAM_LR * (m_hat / (jnp.sqrt(v_hat) + ADAM_EPS) + ADAM_WD * w_ref[...])
                nm_ref[...] = nm
                nv_ref[...] = nv

    def part_spec(l):
        return pl.BlockSpec((nparts, tr, c), lambda layer, i: (0, jnp.where(layer == l, i, 0), 0))

    row = pl.BlockSpec((None, tr, c), lambda layer, i: (layer, i, 0))
    return _call(body, name, (nl, r // tr), [part_spec(l) for l in range(nl)] + [row, row, row],
                 [row] * 4, [_sds((nl, r, c), F32)] * 4)(*parts, w, m, v)


def _exchange(name, srcs, out_shapes, items):
    n_in = len(srcs)
    n_out = len(out_shapes)
    n_items = len(items)

    def body(*refs):
        ins = refs[:n_in]
        outs = refs[n_in:n_in + n_out]
        send_sems, recv_sems, local_sems = refs[n_in + n_out:]
        x, y, c = lax.axis_index("x"), lax.axis_index("y"), lax.axis_index("c")
        me = 4 * x + 2 * y + c

        def pick(ref, sel, peer):
            idx = tuple(peer if s == "peer" else me if s == "me" else s for s in sel)
            return ref.at[idx] if idx else ref

        copies = []
        for it, (si, ssel, oi, osel) in enumerate(items):
            local = pltpu.make_async_copy(pick(ins[si], ssel, me), pick(outs[oi], osel, me), local_sems.at[it])
            local.start()
            copies.append(local)
            for j in range(1, N_DEV):
                px = 1 - x if j & 4 else x
                py = 1 - y if j & 2 else y
                pc = 1 - c if j & 1 else c
                peer = 4 * px + 2 * py + pc
                sem = it * (N_DEV - 1) + j - 1
                cp = pltpu.make_async_remote_copy(
                    src_ref=pick(ins[si], ssel, peer), dst_ref=pick(outs[oi], osel, peer),
                    send_sem=send_sems.at[sem], recv_sem=recv_sems.at[sem],
                    device_id=(px, py, pc), device_id_type=pl.DeviceIdType.MESH)
                cp.start()
                copies.append(cp)
        for cp in copies:
            cp.wait()

    any_spec = pl.BlockSpec(memory_space=pl.ANY)
    return pl.pallas_call(
        body, name=name, in_specs=[any_spec] * n_in, out_specs=[any_spec] * n_out, out_shape=list(out_shapes),
        scratch_shapes=[pltpu.SemaphoreType.DMA((n_items * (N_DEV - 1),)),
                        pltpu.SemaphoreType.DMA((n_items * (N_DEV - 1),)),
                        pltpu.SemaphoreType.DMA((n_items,))],
        compiler_params=pltpu.CompilerParams(has_side_effects=True),
    )(*srcs)


def _sel(ref, sel, slot=None):
    idx = tuple(slot if s == "slot" else s for s in sel)
    return ref.at[idx] if idx else ref


def _gather_plan(srcs, out_shapes, items):
    per = N_DEV - 1
    n_items = len(items)

    def phases(ins, outs, send_sems, recv_sems, local_sems, total):
        x, y, c = lax.axis_index("x"), lax.axis_index("y"), lax.axis_index("c")
        sibling = (x, y, 1 - c)
        chips = [(1 - x, y), (x, 1 - y), (1 - x, 1 - y)]
        slot_of = lambda px, py, pc: 4 * px + 2 * py + pc

        def copy(it, k, block, to, src=None):
            _, _, oi, osel = items[it]
            dst = _sel(outs[oi], osel, slot_of(*block))
            return pltpu.make_async_remote_copy(
                src_ref=dst if src is None else src, dst_ref=dst,
                send_sem=send_sems.at[it * per + k], recv_sem=recv_sems.at[it * per + k],
                device_id=to, device_id_type=pl.DeviceIdType.MESH)

        def own_copies(it):
            si, ssel, oi, osel = items[it]
            src = _sel(ins[si], ssel)
            mine = pltpu.make_async_copy(src, _sel(outs[oi], osel, slot_of(x, y, c)), local_sems.at[it])
            first = [copy(it, 0, (x, y, c), sibling, src=src)]
            first += [copy(it, 1 + j, (x, y, c), (*chip, c), src=src) for j, chip in enumerate(chips)]
            return mine, first

        def start():
            for it in range(n_items):
                mine, first = own_copies(it)
                mine.start()
                for cp in first:
                    cp.start()

        def forward(it):
            def go():
                for j, chip in enumerate(chips):
                    copy(it, 1 + j, (*chip, c), (x, y, c)).wait_recv()
                    copy(it, 4 + j, (*chip, c), sibling).start()
            return go

        def finish():
            for it in range(n_items):
                copy(it, 0, sibling, (x, y, c)).wait_recv()
                for j, chip in enumerate(chips):
                    copy(it, 4 + j, (*chip, 1 - c), (x, y, c)).wait_recv()
            for it in range(n_items):
                mine, first = own_copies(it)
                for cp in first:
                    cp.wait_send()
                for j, chip in enumerate(chips):
                    copy(it, 4 + j, (*chip, c), sibling).wait_send()
                mine.wait()

        lo = max(total // 3, 1)
        acts = [(0, start)]
        for it in range(n_items):
            acts.append((min(lo + it * (total - 1 - lo) // n_items, total - 2), forward(it)))
        acts.append((total - 1, finish))
        return acts

    return _Plan(srcs, out_shapes, n_items * per, n_items, phases)


def _scatter_plan(srcs, out_shapes, items):
    per = N_DEV - 1
    n_items = len(items)

    def phases(ins, outs, send_sems, recv_sems, local_sems, total):
        x, y, c = lax.axis_index("x"), lax.axis_index("y"), lax.axis_index("c")
        me = 4 * x + 2 * y + c

        def copies():
            out = []
            for it, (si, oi, osel) in enumerate(items):
                dst = _sel(outs[oi], osel, me)
                out.append(pltpu.make_async_copy(ins[si].at[me], dst, local_sems.at[it]))
                for j in range(1, N_DEV):
                    px = 1 - x if j & 4 else x
                    py = 1 - y if j & 2 else y
                    pc = 1 - c if j & 1 else c
                    out.append(pltpu.make_async_remote_copy(
                        src_ref=ins[si].at[4 * px + 2 * py + pc], dst_ref=dst,
                        send_sem=send_sems.at[it * per + j - 1], recv_sem=recv_sems.at[it * per + j - 1],
                        device_id=(px, py, pc), device_id_type=pl.DeviceIdType.MESH))
            return out

        def start():
            for cp in copies():
                cp.start()

        def finish():
            for cp in copies():
                cp.wait()

        return [(0, start), (total - 1, finish)]

    return _Plan(srcs, out_shapes, n_items * per, n_items, phases)


def _sum_slots(parts):
    _, r, c = parts.shape

    def body(p_ref, o_ref):
        acc = p_ref[0]
        for dev in range(1, N_DEV):
            acc = acc + p_ref[dev]
        o_ref[...] = acc

    return _call(body, "sum_slots", (1,), [pl.BlockSpec((N_DEV, r, c), lambda i: (0, 0, 0))],
                 pl.BlockSpec((r, c), lambda i: (0, 0)), _sds((r, c), F32))(parts)


def _pack(arrays):
    rows = []
    for a in arrays:
        flat = a.reshape(-1).astype(F32)
        pad = (-flat.shape[0]) % 1024
        if pad:
            flat = jnp.concatenate([flat, jnp.zeros((pad,), F32)])
        rows.append(flat.reshape(-1, 128))
    return jnp.concatenate(rows, axis=0)


def _unpack(buf, shapes):
    out = []
    r0 = 0
    for shp in shapes:
        size = int(np.prod(shp))
        nrows = -(-size // 1024) * 8
        out.append(buf[r0:r0 + nrows].reshape(-1)[:size].reshape(shp))
        r0 += nrows
    return out


def _block_diag_chunks(val):
    eye = jnp.eye(8, dtype=val.dtype)
    return jnp.einsum("cjhp,jk->cjhkp", val, eye).reshape(SSM_CHUNKS, CHUNK_CH, CHUNK_ST)


def _diag_blocks(dense):
    d5 = dense.reshape(SSM_CHUNKS, 8, SSM_GROUP, 8, SSM_STATE)
    return jnp.stack([d5[:, j, :, j, :] for j in range(8)], axis=1)


def _carried(res, comm):
    return res if comm is not None else (res, None)


def _ffn_fwd(tag, hin, hin_b, w_up, conv_w, conv_b, w_down4, gain, bias, seq, comm_up=None, comm_down=None):
    hc, got_up = _carried(_mm_nn("ffn_up" + tag, hin_b, w_up, FFN_HIDDEN, blocked_out=True, comm=comm_up), comm_up)
    nb, n, c = hc.shape
    hc = hc.reshape(2, nb // 2, n, c)
    act, cv = _conv_gate_fwd(hc, conv_w, conv_b, seq)
    (r, hout, hout_b), got_down = _carried(
        _mm_nn_red("ffn_down" + tag, act, w_down4, F32, comm=comm_down, residual=hin, norm=(gain, bias)), comm_down)
    return hout, hout_b, (hin_b, hc, cv, act, r), (got_up, got_down)


def _ffn_bwd(tag, saved, dh, w_up, conv_w, conv_b, w_down4, gain, seq, comm_conv=None, comm_dw=None,
             comm_dx=None):
    hin_b, hc, cv, act, r = saved
    dr, dr_b, dgain, dbias = _ln_bwd("ffn_norm_bwd" + tag, r, dh, gain)
    d_wdown = _mm_tn("ffn_down_dw" + tag, act, dr_b, act.shape[0], BF16, a_blocked=True, g_mode="shared")
    dact = _mm_nt_out("ffn_down_dx" + tag, dr_b, w_down4, FFN_HIDDEN)
    (dhc, dconv), got_conv = _carried(_conv_gate_bwd(hc, cv, dact, conv_w, seq, comm=comm_conv), comm_conv)
    dhc8 = dhc.reshape(2 * dhc.shape[1], dhc.shape[2], dhc.shape[3])
    d_wup, got_dw = _carried(_mm_tn("ffn_up_dw" + tag, dhc8, hin_b, dhc8.shape[0], BF16, a_blocked=True,
                                    g_mode="shared", comm=comm_dw), comm_dw)
    dhin, got_dx = _carried(_mm_nt_red("ffn_up_dx" + tag, dhc8, w_up, F32, g_blocked=True, add=dr,
                                       add_scale=DN_ALPHA, comm=comm_dx), comm_dx)
    return dhin, d_wup, d_wdown, dconv, dgain, dbias, (got_conv, got_dw, got_dx)


def _local_step(x, target, p, wfull, bsz, seq, ex=None):
    n = bsz * seq
    xf = x.reshape(n, D_MODEL)
    tf = target.reshape(n, D_MODEL)

    lr = p["s5_lam_re"].reshape(1, N_STATE)
    li = p["s5_lam_im"].reshape(1, N_STATE)
    ldt = p["s5_log_dt"].reshape(1, D_MODEL // SSM_GROUP)
    br_t = p["s5_b_re"].reshape(N_STATE, SSM_GROUP).T
    bi_t = p["s5_b_im"].reshape(N_STATE, SSM_GROUP).T
    ab_r, ab_i, bb_r, bb_i, tab = _s5_prep(lr, li, ldt, br_t, bi_t)

    def bb_chunks(bb):
        return _block_diag_chunks(bb.reshape(SSM_GROUP, SSM_CHUNKS, 8, SSM_STATE).transpose(1, 2, 0, 3))

    def c_chunks(cc):
        return _block_diag_chunks(cc.reshape(SSM_CHUNKS, 8, SSM_GROUP, SSM_STATE))

    bbr_c, bbi_c = bb_chunks(bb_r).astype(BF16), bb_chunks(bb_i).astype(BF16)
    cr_c, ci_c = c_chunks(p["s5_c_re"]).astype(BF16), c_chunks(p["s5_c_im"]).astype(BF16)
    dvec = p["s5_d"].reshape(1, D_MODEL)

    if ex is None:
        y, h_r, h_i = _s5_scan_fwd(xf, bbr_c, bbi_c, cr_c, ci_c, dvec, tab, seq)
    else:
        (y, h_r, h_i), gathered = _s5_scan_fwd(xf, bbr_c, bbi_c, cr_c, ci_c, dvec, tab, seq,
                                               comm=ex.gather_first())
        p, wfull = ex.take_first(gathered, p)
    ln_g = p["ln_gain"].reshape(4, 1, D_MODEL)
    ln_b = p["ln_bias"].reshape(4, 1, D_MODEL)
    y2, z, gg = _glu_fwd(y, wfull["s5_w_glu"], p["s5_b_glu"].reshape(1, D_MODEL))
    r1, h1, h1b = _mm_nn("s5_out", gg, wfull["s5_w_out"][None], F32, residual=xf, norm=(ln_g[0], ln_b[0]))
    h2, h2b, ffn0_saved, (got_up, got_down) = _ffn_fwd(
        "0", h1, h1b, wfull["ffn_w_up"][0], p["conv_w"][0], p["conv_b"][0], wfull["ffn_w_down"][0],
        ln_g[1], ln_b[1], seq, comm_up=ex.gather_layer1("up", 0) if ex else None,
        comm_down=ex.gather_layer1("down", 0) if ex else None)
    halves = {"up": [got_up[0]], "down": [got_down[0]]} if ex else None

    kmat = _mm_nn("attn_k", h2b, wfull["attn_w_kv"], BF16, blocks=(0, 4))
    vmat = _mm_nn("attn_v", h2b, wfull["attn_w_kv"], BF16, blocks=(4, 4))
    qmat = _mm_nn("attn_q", h2b, wfull["attn_w_q"], BF16)
    biases, outs, lses = [], [], []
    for g, dil in enumerate(DILATIONS):
        rel_t = p["rel_bias"][:, g * N_HEADS:(g + 1) * N_HEADS].T
        biases.append(_bias_expand(rel_t, dil).reshape(N_HEADS, BAND, 2 * BAND))
        which = {4: "down", 16: "up"}.get(dil) if ex else None
        plan = ex.gather_layer1(which, 1) if which else None
        (o_g, l_g), got = _carried(_attn_fwd(qmat, kmat, vmat, biases[g], g, dil, bsz, seq, comm=plan), plan)
        if plan is not None:
            halves[which].append(got[0])
        outs.append(o_g)
        lses.append(l_g)
    if ex is not None:
        wfull = ex.take_layer1(halves, wfull)
    o, lse = _attn_combine(outs, lses)
    r3, h3, h3b = _mm_nn("attn_out", o, wfull["attn_w_out"][None], F32, residual=h2, norm=(ln_g[2], ln_b[2]))
    h4, _, ffn1_saved, _ = _ffn_fwd("1", h3, h3b, wfull["ffn_w_up"][1], p["conv_w"][1], p["conv_b"][1],
                                    wfull["ffn_w_down"][1], ln_g[3], ln_b[3], seq)
    loss_tile, dh4 = _loss_head(h4, tf)

    grads = {}
    dh3, grads["ffn_w_up1"], grads["ffn_w_down1"], dconv1, dg3, db3, _ = _ffn_bwd(
        "1", ffn1_saved, dh4, wfull["ffn_w_up"][1], p["conv_w"][1], p["conv_b"][1], wfull["ffn_w_down"][1],
        ln_g[3], seq)
    dr3, dr3b, dg2, db2 = _ln_bwd("attn_norm_bwd", r3, dh3, ln_g[2])
    grads["attn_w_out"] = _mm_tn("attn_out_dw", o, dr3b, 1, BF16, g_mode="shared")
    do = _mm_nt_red("attn_out_dx", dr3b, wfull["attn_w_out"][None], F32, g_blocked=False)
    dqkv, drel = (), []
    for g, dil in enumerate(DILATIONS):
        plan = ex.scatter(grads, ("ffn_w_up1", "ffn_w_down1")) if ex and dil == DILATIONS[-1] else None
        res, got = _carried(_attn_bwd(qmat, kmat, vmat, do, o, lse, biases[g], g, dil, bsz, seq, comm=plan,
                                      into=dqkv), plan)
        if plan is not None:
            ex.keep(plan, got)
        dqkv, db_g = tuple(res[:3]), res[3]
        drel.append(_bias_reduce(db_g.reshape(N_HEADS, BAND * 2 * BAND), dil).T)
    dq, dk, dv = dqkv
    grads["rel_bias"] = jnp.concatenate(drel, axis=1)
    grads["attn_w_q"] = _mm_tn("attn_q_dw", h2b, dq, N_DEV, BF16)
    grads["attn_w_kv"] = jnp.concatenate([_mm_tn("attn_k_dw", h2b, dk, 4, BF16),
                                          _mm_tn("attn_v_dw", h2b, dv, 4, BF16)], axis=0)
    dh2 = _mm_nt_red("attn_q_dx", dq, wfull["attn_w_q"], F32, g_blocked=False, add=dr3, add_scale=DN_ALPHA)
    dh2 = _mm_nt_red("attn_k_dx", dk, wfull["attn_w_kv"], F32, g_blocked=False, add=dh2, blocks=(0, 4))
    dh2 = _mm_nt_red("attn_v_dx", dv, wfull["attn_w_kv"], F32, g_blocked=False, add=dh2, blocks=(4, 4))

    plans = [ex.scatter(grads, names) if ex else None
             for names in (("attn_w_kv",), ("attn_w_out",), ("attn_w_q",))]
    dh1, grads["ffn_w_up0"], grads["ffn_w_down0"], dconv0, dg1, db1, got = _ffn_bwd(
        "0", ffn0_saved, dh2, wfull["ffn_w_up"][0], p["conv_w"][0], p["conv_b"][0], wfull["ffn_w_down"][0],
        ln_g[1], seq, comm_conv=plans[0], comm_dw=plans[1], comm_dx=plans[2])
    if ex is not None:
        for plan, outs in zip(plans, got):
            ex.keep(plan, outs)
    dr1, dr1b, dg0, db0 = _ln_bwd("s5_norm_bwd", r1, dh1, ln_g[0])
    grads["s5_w_out"] = _mm_tn("s5_out_dw", gg, dr1b, 1, BF16, g_mode="shared")
    dgg = _mm_nt_red("s5_out_dx", dr1b, wfull["s5_w_out"][None], F32, g_blocked=False)
    dz, d_bglu, dy2_direct = _glu_bwd_gate(y, z, dgg)
    grads["s5_w_glu"] = _mm_tn("s5_glu_dw", y2, dz, 1, BF16, g_mode="shared")
    dy = _mm_nt_red("s5_glu_dx", dz, wfull["s5_w_glu"][None], F32, g_blocked=False, add=dy2_direct,
                    post=(_gelu_bwd, y))
    plan_last = ex.scatter(grads, ("ffn_w_up0", "ffn_w_down0", "s5_w_out", "s5_w_glu")) if ex else None
    res = _s5_scan_bwd(xf, dy, dr1, h_r, h_i, bbr_c, bbi_c, cr_c, ci_c, dvec, tab, seq, comm=plan_last)
    if ex is not None:
        res, got = res
        ex.keep(plan_last, got)
    dx, dcr, dci, dbr, dbi, dar, dai, dd = res

    def bb_from_chunks(dense):
        return _diag_blocks(dense).transpose(2, 0, 1, 3).reshape(SSM_GROUP, N_STATE)

    d_lr, d_li, d_ldt, d_br, d_bi = _s5_prep_bwd(lr, li, ldt, br_t, bi_t, dar, dai,
                                                 bb_from_chunks(dbr), bb_from_chunks(dbi))
    grads["s5_lam_re"] = d_lr.reshape(p["s5_lam_re"].shape)
    grads["s5_lam_im"] = d_li.reshape(p["s5_lam_im"].shape)
    grads["s5_log_dt"] = d_ldt.reshape(p["s5_log_dt"].shape)
    grads["s5_b_re"] = d_br.T.reshape(p["s5_b_re"].shape)
    grads["s5_b_im"] = d_bi.T.reshape(p["s5_b_im"].shape)
    grads["s5_c_re"] = _diag_blocks(dcr).reshape(p["s5_c_re"].shape)
    grads["s5_c_im"] = -_diag_blocks(dci).reshape(p["s5_c_im"].shape)
    grads["s5_d"] = dd.reshape(p["s5_d"].shape)
    grads["s5_b_glu"] = d_bglu.reshape(1, D_MODEL)
    dconv = jnp.stack([dconv0, dconv1]).reshape(2, N_DEV, 8, -1)
    grads["ffn_conv_w"] = dconv[:, :, 0:3, :].transpose(0, 2, 1, 3)
    grads["ffn_conv_b"] = dconv[:, :, 3, :]
    grads["ln_gain"] = jnp.stack([dg0, dg1, dg2, dg3]).reshape(2, 2, D_MODEL)
    grads["ln_bias"] = jnp.stack([db0, db1, db2, db3]).reshape(2, 2, D_MODEL)
    return loss_tile[0, 0], dx.reshape(x.shape), grads


SMALL_REPLICATED = ("s5_lam_re", "s5_lam_im", "s5_log_dt", "s5_b_re", "s5_b_im", "s5_c_re", "s5_c_im", "s5_d",
                    "rel_bias", "ffn_conv_b")
SMALL_SHARDED = ("s5_b_glu", "ffn_conv_w", "ln_gain", "ln_bias")
BIG = ("s5_w_glu", "s5_w_out", "attn_w_kv", "attn_w_q", "attn_w_out", "ffn_w_up", "ffn_w_down")
WEIGHTS = ("s5_lam_re", "s5_lam_im", "s5_log_dt", "s5_b_re", "s5_b_im", "s5_c_re", "s5_c_im", "s5_d", "s5_w_glu",
           "s5_b_glu", "s5_w_out", "attn_w_kv", "attn_w_q", "attn_w_out", "rel_bias", "ffn_w_up", "ffn_conv_w",
           "ffn_conv_b", "ffn_w_down", "ln_gain", "ln_bias")


class _Exchanges:
    def __init__(self, w):
        self.w = w
        self.parts = {}
        self.up = w["ffn_w_up"].astype(BF16)
        self.down = w["ffn_w_down"].astype(BF16)

    def gather_first(self):
        w = self.w
        srcs = [w["s5_w_glu"][0].astype(BF16), w["s5_w_out"][0].astype(BF16),
                _pack([w[k] for k in SMALL_SHARDED]), self.up, self.down, w["attn_w_kv"].astype(BF16),
                w["attn_w_q"][0].astype(BF16), w["attn_w_out"][0].astype(BF16)]
        outs = [_sds((N_DEV,) + (s.shape[1:] if i in (3, 4) else s.shape), s.dtype) for i, s in enumerate(srcs)]
        items = [(i, (0,) if i in (3, 4) else (), i, ("slot",)) for i in range(len(srcs))]
        return _gather_plan(srcs, outs, items)

    def take_first(self, g, p):
        d = D_MODEL
        w = self.w
        wfull = {
            "s5_w_glu": g[0].reshape(d, d),
            "s5_w_out": g[1].reshape(d, d),
            "ffn_w_up": [g[3], None],
            "ffn_w_down": [g[4].reshape(4, -1, d), None],
            "attn_w_kv": g[5],
            "attn_w_q": g[6],
            "attn_w_out": g[7].reshape(d, d),
        }
        smalls = [_unpack(g[2][dev], [w[k].shape for k in SMALL_SHARDED]) for dev in range(N_DEV)]
        c_ff = w["ffn_conv_w"].shape[2]
        conv_w = jnp.stack([s[1] for s in smalls], axis=2)
        p = dict(p)
        p.update(s5_b_glu=jnp.concatenate([s[0] for s in smalls], axis=1),
                 ln_gain=jnp.concatenate([s[2] for s in smalls], axis=2),
                 ln_bias=jnp.concatenate([s[3] for s in smalls], axis=2),
                 conv_w=conv_w.transpose(0, 2, 1, 3).reshape(2, 2, 4, 3, c_ff),
                 conv_b=w["ffn_conv_b"].reshape(2, 2, 4, 1, c_ff))
        return p, wfull

    def gather_layer1(self, which, half):
        src = self.up if which == "up" else self.down
        rows = src.shape[1] // 2
        return _gather_plan([src], [_sds((N_DEV, rows) + src.shape[2:], BF16)],
                            [(0, (1, pl.ds(half * rows, rows)), 0, ("slot",))])

    def take_layer1(self, halves, wfull):
        up = jnp.concatenate(halves["up"], axis=1)
        down = jnp.concatenate(halves["down"], axis=1)
        wfull = dict(wfull)
        wfull["ffn_w_up"] = [wfull["ffn_w_up"][0], up]
        wfull["ffn_w_down"] = [wfull["ffn_w_down"][0], down.reshape(4, -1, D_MODEL)]
        return wfull

    def scatter(self, grads, names):
        srcs = [grads[k].reshape(N_DEV, -1, grads[k].shape[-1]) for k in names]
        plan = _scatter_plan(srcs, [_sds(s.shape, BF16) for s in srcs],
                             [(i, i, ("slot",)) for i in range(len(names))])
        plan.names = names
        return plan

    def keep(self, plan, outs):
        for k, o in zip(plan.names, outs):
            self.parts[k] = o


def kernel(x, s5_lam_re, s5_lam_im, s5_log_dt, s5_b_re, s5_b_im, s5_c_re, s5_c_im, s5_d, s5_w_glu, s5_b_glu, s5_w_out, attn_w_kv, attn_w_q, attn_w_out, rel_bias, ffn_w_up, ffn_conv_w, ffn_conv_b, ffn_w_down, ln_gain, ln_bias, loss_target, m_s5_lam_re, m_s5_lam_im, m_s5_log_dt, m_s5_b_re, m_s5_b_im, m_s5_c_re, m_s5_c_im, m_s5_d, m_s5_w_glu, m_s5_b_glu, m_s5_w_out, m_attn_w_kv, m_attn_w_q, m_attn_w_out, m_rel_bias, m_ffn_w_up, m_ffn_conv_w, m_ffn_conv_b, m_ffn_w_down, m_ln_gain, m_ln_bias, v_s5_lam_re, v_s5_lam_im, v_s5_log_dt, v_s5_b_re, v_s5_b_im, v_s5_c_re, v_s5_c_im, v_s5_d, v_s5_w_glu, v_s5_b_glu, v_s5_w_out, v_attn_w_kv, v_attn_w_q, v_attn_w_out, v_rel_bias, v_ffn_w_up, v_ffn_conv_w, v_ffn_conv_b, v_ffn_w_down, v_ln_gain, v_ln_bias):
    w = dict(s5_lam_re=s5_lam_re, s5_lam_im=s5_lam_im, s5_log_dt=s5_log_dt, s5_b_re=s5_b_re, s5_b_im=s5_b_im,
             s5_c_re=s5_c_re, s5_c_im=s5_c_im, s5_d=s5_d, s5_w_glu=s5_w_glu, s5_b_glu=s5_b_glu, s5_w_out=s5_w_out,
             attn_w_kv=attn_w_kv, attn_w_q=attn_w_q, attn_w_out=attn_w_out, rel_bias=rel_bias, ffn_w_up=ffn_w_up,
             ffn_conv_w=ffn_conv_w, ffn_conv_b=ffn_conv_b, ffn_w_down=ffn_w_down, ln_gain=ln_gain, ln_bias=ln_bias)
    mom = dict(s5_lam_re=m_s5_lam_re, s5_lam_im=m_s5_lam_im, s5_log_dt=m_s5_log_dt, s5_b_re=m_s5_b_re,
               s5_b_im=m_s5_b_im, s5_c_re=m_s5_c_re, s5_c_im=m_s5_c_im, s5_d=m_s5_d, s5_w_glu=m_s5_w_glu,
               s5_b_glu=m_s5_b_glu, s5_w_out=m_s5_w_out, attn_w_kv=m_attn_w_kv, attn_w_q=m_attn_w_q,
               attn_w_out=m_attn_w_out, rel_bias=m_rel_bias, ffn_w_up=m_ffn_w_up, ffn_conv_w=m_ffn_conv_w,
               ffn_conv_b=m_ffn_conv_b, ffn_w_down=m_ffn_w_down, ln_gain=m_ln_gain, ln_bias=m_ln_bias)
    var = dict(s5_lam_re=v_s5_lam_re, s5_lam_im=v_s5_lam_im, s5_log_dt=v_s5_log_dt, s5_b_re=v_s5_b_re,
               s5_b_im=v_s5_b_im, s5_c_re=v_s5_c_re, s5_c_im=v_s5_c_im, s5_d=v_s5_d, s5_w_glu=v_s5_w_glu,
               s5_b_glu=v_s5_b_glu, s5_w_out=v_s5_w_out, attn_w_kv=v_attn_w_kv, attn_w_q=v_attn_w_q,
               attn_w_out=v_attn_w_out, rel_bias=v_rel_bias, ffn_w_up=v_ffn_w_up, ffn_conv_w=v_ffn_conv_w,
               ffn_conv_b=v_ffn_conv_b, ffn_w_down=v_ffn_w_down, ln_gain=v_ln_gain, ln_bias=v_ln_bias)
    bsz, seq, _ = x.shape
    me = 4 * lax.axis_index("x") + 2 * lax.axis_index("y") + lax.axis_index("c")

    ex = _Exchanges(w)
    loss_part, grad_x, g = _local_step(x, loss_target, dict(w), None, bsz, seq, ex=ex)
    loss = lax.psum(loss_part, ("x", "y", "c"))

    d = D_MODEL
    results = {}
    for name in BIG:
        shard = w[name]
        if name == "ffn_w_up":
            parts = [ex.parts[name + "0"], ex.parts[name + "1"]]
            outs4 = _adamw("adamw_" + name, parts, *(jnp.swapaxes(a, 1, 2) for a in (shard, mom[name], var[name])))
            results[name] = tuple(jnp.swapaxes(o, 1, 2) for o in outs4)
            continue
        if name == "ffn_w_down":
            parts, r3 = [ex.parts[name + "0"], ex.parts[name + "1"]], shard.shape
        else:
            parts, r3 = [ex.parts[name]], (1, -1, shard.shape[-1])
        outs4 = _adamw("adamw_" + name, parts, shard.reshape(r3), mom[name].reshape(r3), var[name].reshape(r3))
        results[name] = tuple(o.reshape(shard.shape) for o in outs4)

    small_names = SMALL_REPLICATED + SMALL_SHARDED
    packed = _pack([g[k] for k in small_names])
    pad = (-packed.shape[0]) % (8 * N_DEV)
    if pad:
        packed = jnp.concatenate([packed, jnp.zeros((pad, 128), F32)], axis=0)
    piece = packed.shape[0] // N_DEV
    slots = _exchange("reduce_small", [packed.reshape(N_DEV, piece, 128)], [_sds((N_DEV, piece, 128), F32)],
                      [(0, ("peer",), 0, ("me",))])[0]
    summed = _exchange("gather_small", [_sum_slots(slots)], [_sds((N_DEV, piece, 128), F32)],
                       [(0, (), 0, ("me",))])[0]
    total = _unpack(summed.reshape(N_DEV * piece, 128), [g[k].shape for k in small_names])
    gsum = dict(zip(small_names, total))
    mine = {k: gsum[k] for k in SMALL_REPLICATED}
    mine["s5_b_glu"] = lax.dynamic_slice_in_dim(gsum["s5_b_glu"], me * (d // N_DEV), d // N_DEV, axis=1)
    mine["ffn_conv_w"] = lax.dynamic_index_in_dim(gsum["ffn_conv_w"], me, axis=2, keepdims=False)
    mine["ln_gain"] = lax.dynamic_slice_in_dim(gsum["ln_gain"], me * (d // N_DEV), d // N_DEV, axis=2)
    mine["ln_bias"] = lax.dynamic_slice_in_dim(gsum["ln_bias"], me * (d // N_DEV), d // N_DEV, axis=2)
    gp = _pack([mine[k] for k in small_names])
    outs4 = _adamw("adamw_small", [gp[None]], _pack([w[k] for k in small_names])[None],
                   _pack([mom[k] for k in small_names])[None], _pack([var[k] for k in small_names])[None])
    outs4 = [o[0] for o in outs4]
    shapes = [w[k].shape for k in small_names]
    unpacked = [_unpack(o, shapes) for o in outs4]
    for i, k in enumerate(small_names):
        results[k] = tuple(unpacked[j][i] for j in range(4))

    out = [loss, grad_x]
    for j in range(4):
        out.extend(results[k][j] for k in WEIGHTS)
    return tuple(out)
```

```python
import functools
import math

import numpy as np
import jax
import jax.numpy as jnp
from jax import lax
from jax.experimental import pallas as pl
from jax.experimental.pallas import tpu as pltpu

F32 = jnp.float32
BF16 = jnp.bfloat16
HIGHEST = lax.Precision.HIGHEST

N_DEV = 8
D_MODEL = 1024
SSM_GROUP = 16
SSM_STATE = 64
SSM_CHUNKS = 8
CHUNK_CH = D_MODEL // SSM_CHUNKS
CHUNK_ST = CHUNK_CH // SSM_GROUP * SSM_STATE
N_STATE = D_MODEL // SSM_GROUP * SSM_STATE
HEAD_DIM = 64
N_HEADS = 16
BAND = 128
DILATIONS = (1, 4, 16)
REL_BUCKETS = 32
REL_MAX_DIST = 2048
NEG_BIG = -1e30
DN_ALPHA = (2.0 * 2) ** 0.25
LN_EPS = 1e-5
ADAM_LR = 0.001
ADAM_B1 = 0.9
ADAM_B2 = 0.999
ADAM_EPS = 1e-08
ADAM_WD = 0.01
ADAM_STEP = 10

TM = 512
TM_MM = 2048
TM_MM_F32 = 1024
T_SCAN = 1024
VMEM_LIMIT = 56 * 1024 * 1024
FFN_HIDDEN = BF16

NN = (((1,), (0,)), ((), ()))
NT = (((1,), (1,)), ((), ()))
TN = (((0,), (0,)), ((), ()))


class _Plan:
    def __init__(self, srcs, out_shapes, n_sems, n_local, phases):
        self.srcs, self.out_shapes, self.n_sems, self.n_local, self.phases = srcs, out_shapes, n_sems, n_local, phases


def _call(body, name, grid, in_specs, out_specs, out_shape, scratch=(), comm=None, fill=()):
    params = dict(dimension_semantics=("arbitrary",) * len(grid), vmem_limit_bytes=VMEM_LIMIT)
    if comm is None and not fill:
        return pl.pallas_call(
            body, name=name, grid=grid, in_specs=in_specs, out_specs=out_specs, out_shape=out_shape,
            scratch_shapes=list(scratch), compiler_params=pltpu.CompilerParams(**params))
    single = not isinstance(out_specs, (list, tuple))
    o_specs = [out_specs] if single else list(out_specs)
    o_shape = [out_shape] if single else list(out_shape)
    n_in, n_out, n_scr, n_fill = len(in_specs), len(o_specs), len(scratch), len(fill)
    srcs = list(comm.srcs) if comm else []
    c_shapes = list(comm.out_shapes) if comm else []
    n_cin, n_cout = len(srcs), len(c_shapes)
    total = int(np.prod(grid))

    def wrapped(*refs):
        ins = refs[:n_in]
        refs = refs[n_in + n_fill:]
        cins, outs, couts = refs[:n_cin], refs[n_cin:n_cin + n_out], refs[n_cin + n_out:n_cin + n_out + n_cout]
        rest = refs[n_cin + n_out + n_cout:]
        scr, sems = rest[:n_scr], rest[n_scr:]
        if comm is None:
            body(*ins, *outs, *scr)
            return
        step = pl.program_id(0)
        for ax in range(1, len(grid)):
            step = step * grid[ax] + pl.program_id(ax)
        phases = comm.phases(cins, couts, *sems, total)
        for at, action in phases:
            if at == 0:
                pl.when(step == 0)(action)
        body(*ins, *outs, *scr)
        for at, action in phases:
            if at > 0:
                pl.when(step == at)(action)

    any_spec = pl.BlockSpec(memory_space=pl.ANY)
    sems = [] if comm is None else [pltpu.SemaphoreType.DMA((comm.n_sems,)), pltpu.SemaphoreType.DMA((comm.n_sems,)),
                                    pltpu.SemaphoreType.DMA((max(comm.n_local, 1),))]
    call = pl.pallas_call(
        wrapped, name=name, grid=grid, in_specs=list(in_specs) + [any_spec] * (n_fill + n_cin),
        out_specs=o_specs + [any_spec] * n_cout, out_shape=o_shape + c_shapes,
        scratch_shapes=list(scratch) + sems, input_output_aliases={n_in + j: j for j in range(n_fill)},
        compiler_params=pltpu.CompilerParams(has_side_effects=comm is not None, **params))

    def run(*args):
        res = call(*args, *fill, *srcs)
        own = res[0] if single else res[:n_out]
        return (own, res[n_out:]) if comm is not None else own

    return run


def _sds(shape, dtype):
    return jax.ShapeDtypeStruct(shape, dtype)


def _mm(name, a, b, *, dims, grid, a_spec, b_spec, o_spec, out_shape, nred=1, red_axis=0,
        add=None, add_spec=None, add_scale=1.0, comm=None, norm=None, inner=None, post=None):
    has_add = add is not None
    n_extra = (1 if has_add else 0) + (1 if post else 0) + (2 if norm else 0)
    n_out = 3 if norm else 1
    acc_shape = tuple(s for s in o_spec.block_shape if s is not None)

    def body(*refs):
        a_ref, b_ref = refs[:2]
        extra = refs[2:2 + n_extra]
        outs = refs[2 + n_extra:2 + n_extra + n_out]
        rest = refs[2 + n_extra + n_out:]
        o_ref = outs[0]
        if inner is None:
            prod = lax.dot_general(a_ref[...].astype(BF16), b_ref[...].astype(BF16), dims,
                                   preferred_element_type=F32)
        else:
            nb, a_mode = inner
            width = a_ref.shape[-1] // nb
            prod = None
            for d in range(nb):
                a_d = a_ref[d] if a_mode == "lead" else a_ref[:, d * width:(d + 1) * width]
                part = lax.dot_general(a_d.astype(BF16), b_ref[d].astype(BF16), dims, preferred_element_type=F32)
                prod = part if prod is None else prod + part

        def finish(val):
            if has_add:
                val = val + add_scale * extra[0][...].astype(F32)
            if post:
                val = post[0](val, extra[1 if has_add else 0][...])
            o_ref[...] = val.astype(o_ref.dtype)
            if norm:
                xhat, _ = _ln_stats(val)
                y = xhat * extra[-2][...] + extra[-1][...]
                outs[1][...] = y
                outs[2][...] = y.astype(BF16)

        if nred == 1:
            finish(prod)
        else:
            acc = rest[0]
            k = pl.program_id(red_axis)

            @pl.when(k == 0)
            def _():
                acc[...] = prod

            @pl.when(k > 0)
            def _():
                acc[...] += prod

            @pl.when(k == nred - 1)
            def _():
                finish(acc[...])

    in_specs = [a_spec, b_spec] + ([add_spec] if has_add else [])
    scratch = [pltpu.VMEM(acc_shape, F32)] if nred > 1 else []
    args = (a, b) + ((add,) if has_add else ())
    if post:
        in_specs.append(o_spec)
        args += (post[1],)
    if norm:
        vec = pl.BlockSpec((1, out_shape.shape[1]), lambda *_: (0, 0))
        in_specs += [vec, vec]
        args += tuple(norm)
        o_spec = [o_spec, o_spec, o_spec]
        out_shape = [out_shape, out_shape, _sds(out_shape.shape, BF16)]
    return _call(body, name, grid, in_specs, o_spec, out_shape, scratch, comm=comm)(*args)


def _mm_nn(name, a, b, out_dtype, blocked_out=False, comm=None, blocks=None, residual=None, norm=None):
    TM = TM_MM if out_dtype == BF16 else TM_MM_F32
    m, k = a.shape
    _, _, n = b.shape
    nb = b.shape[0] if blocks is None else blocks[1]
    first = 0 if blocks is None else blocks[0]
    if blocked_out:
        o_spec = pl.BlockSpec((None, TM, n), lambda d, i: (d, i, 0))
        out_shape = _sds((nb, m, n), out_dtype)
    else:
        o_spec = pl.BlockSpec((TM, n), lambda d, i: (i, d))
        out_shape = _sds((m, nb * n), out_dtype)
    return _mm(name, a, b, dims=NN, grid=(nb, m // TM),
               a_spec=pl.BlockSpec((TM, k), lambda d, i: (i, 0)),
               b_spec=pl.BlockSpec((None, k, n), lambda d, i: (d + first, 0, 0)),
               o_spec=o_spec, out_shape=out_shape, comm=comm, norm=norm,
               add=residual, add_spec=o_spec if residual is not None else None, add_scale=DN_ALPHA)


def _mm_nn_red(name, a, b, out_dtype, comm=None, residual=None, norm=None):
    nb, m, k = a.shape
    n = b.shape[2]
    o_spec = pl.BlockSpec((TM, n), lambda i: (i, 0))
    return _mm(name, a, b, dims=NN, grid=(m // TM,), inner=(nb, "lead"),
               a_spec=pl.BlockSpec((nb, TM, k), lambda i: (0, i, 0)),
               b_spec=pl.BlockSpec((nb, k, n), lambda i: (0, 0, 0)),
               o_spec=o_spec, out_shape=_sds((m, n), out_dtype), comm=comm, norm=norm,
               add=residual, add_spec=o_spec if residual is not None else None, add_scale=DN_ALPHA)


def _mm_tn(name, a, g, nb, out_dtype, a_blocked=False, g_mode="cols", comm=None):
    TM = TM_MM
    if a_blocked:
        _, m, ka = a.shape
        a_spec = pl.BlockSpec((None, TM, ka), lambda d, i: (d, i, 0))
    else:
        m, ka = a.shape
        a_spec = pl.BlockSpec((TM, ka), lambda d, i: (i, 0))
    if g_mode == "cols":
        n = g.shape[1] // nb
        g_spec = pl.BlockSpec((TM, n), lambda d, i: (i, d))
    elif g_mode == "blocked":
        n = g.shape[2]
        g_spec = pl.BlockSpec((None, TM, n), lambda d, i: (d, i, 0))
    else:
        n = g.shape[1]
        g_spec = pl.BlockSpec((TM, n), lambda d, i: (i, 0))
    nred = m // TM
    return _mm(name, a, g, dims=TN, grid=(nb, nred), nred=nred, red_axis=1,
               a_spec=a_spec, b_spec=g_spec,
               o_spec=pl.BlockSpec((None, ka, n), lambda d, i: (d, 0, 0)),
               out_shape=_sds((nb, ka, n), out_dtype), comm=comm)


def _mm_nt_red(name, g, w, out_dtype, g_blocked, add=None, add_scale=1.0, comm=None, blocks=None, post=None):
    _, k, n = w.shape
    nb = w.shape[0] if blocks is None else blocks[1]
    first = 0 if blocks is None else blocks[0]
    tm = TM_MM_F32 if nb * k * n <= 4 * 1024 * 768 else TM
    if g_blocked:
        m = g.shape[1]
        g_spec = pl.BlockSpec((nb, tm, n), lambda i: (0, i, 0))
    else:
        m = g.shape[0]
        g_spec = pl.BlockSpec((tm, nb * n), lambda i: (i, 0))
    o_spec = pl.BlockSpec((tm, k), lambda i: (i, 0))
    return _mm(name, g, w, dims=NT, grid=(m // tm,), inner=(nb, "lead" if g_blocked else "cols"),
               a_spec=g_spec, b_spec=pl.BlockSpec((nb, k, n), lambda i: (first // nb, 0, 0)),
               o_spec=o_spec, out_shape=_sds((m, k), out_dtype),
               add=add, add_spec=o_spec if add is not None else None, add_scale=add_scale, comm=comm, post=post)


def _mm_nt_out(name, g, w, out_dtype):
    TM = TM_MM
    m, n = g.shape
    nb, k, _ = w.shape
    return _mm(name, g, w, dims=NT, grid=(nb, m // TM),
               a_spec=pl.BlockSpec((TM, n), lambda d, i: (i, 0)),
               b_spec=pl.BlockSpec((None, k, n), lambda d, i: (d, 0, 0)),
               o_spec=pl.BlockSpec((None, TM, k), lambda d, i: (d, i, 0)),
               out_shape=_sds((nb, m, k), out_dtype))


def _ln_stats(r):
    mu = jnp.mean(r, axis=-1, keepdims=True)
    xc = r - mu
    var = jnp.mean(xc * xc, axis=-1, keepdims=True)
    rstd = lax.rsqrt(var + LN_EPS)
    return xc * rstd, rstd


def _ln_bwd(name, r, dy, gain):
    n, d = r.shape

    def body(r_ref, dy_ref, g_ref, dr_ref, drb_ref, dg_ref, db_ref):
        i = pl.program_id(0)
        xhat, rstd = _ln_stats(r_ref[...])
        dy_v = dy_ref[...]
        dxh = dy_v * g_ref[...]
        m1 = jnp.mean(dxh, axis=-1, keepdims=True)
        m2 = jnp.mean(dxh * xhat, axis=-1, keepdims=True)
        dr = rstd * (dxh - m1 - xhat * m2)
        dr_ref[...] = dr
        drb_ref[...] = dr.astype(BF16)
        dg = jnp.sum(dy_v * xhat, axis=0, keepdims=True)
        db = jnp.sum(dy_v, axis=0, keepdims=True)

        @pl.when(i == 0)
        def _():
            dg_ref[...] = dg
            db_ref[...] = db

        @pl.when(i > 0)
        def _():
            dg_ref[...] += dg
            db_ref[...] += db

    row = pl.BlockSpec((TM, d), lambda i: (i, 0))
    vec = pl.BlockSpec((1, d), lambda i: (0, 0))
    return _call(body, name, (n // TM,), [row, row, vec], [row, row, vec, vec],
                 [_sds((n, d), F32), _sds((n, d), BF16), _sds((1, d), F32), _sds((1, d), F32)])(r, dy, gain)


def _loss_head(y, target):
    n, d = y.shape

    def body(y_ref, t_ref, l_ref, dy_ref):
        i = pl.program_id(0)
        e = y_ref[...] - t_ref[...]
        dy_ref[...] = e * (1.0 / d)
        part = jnp.sum(jnp.sum(e * e, axis=1, keepdims=True), axis=0, keepdims=True) * (0.5 / d)
        part = jnp.broadcast_to(part, (8, 128))

        @pl.when(i == 0)
        def _():
            l_ref[...] = part

        @pl.when(i > 0)
        def _():
            l_ref[...] += part

    row = pl.BlockSpec((TM, d), lambda i: (i, 0))
    return _call(body, "loss_head", (n // TM,), [row, row],
                 [pl.BlockSpec((8, 128), lambda i: (0, 0)), row],
                 [_sds((8, 128), F32), _sds((n, d), F32)])(y, target)


def _group_expand_matrix():
    e = np.zeros((D_MODEL // SSM_GROUP, N_STATE), np.float32)
    for g in range(D_MODEL // SSM_GROUP):
        e[g, g * SSM_STATE:(g + 1) * SSM_STATE] = 1.0
    return jnp.asarray(e)


def _discretise(lr, li, ldt, br, bi, expand):
    dt = jnp.dot(jnp.exp(ldt), expand, precision=HIGHEST, preferred_element_type=F32)
    mag = jnp.exp(lr * dt)
    th = li * dt
    ab_r = mag * jnp.cos(th)
    ab_i = mag * jnp.sin(th)
    den = lr * lr + li * li
    nr = ab_r - 1.0
    co_r = (nr * lr + ab_i * li) / den
    co_i = (ab_i * lr - nr * li) / den
    bb_r = co_r * br - co_i * bi
    bb_i = co_r * bi + co_i * br
    return ab_r, ab_i, bb_r, bb_i


def _cmul(ar, ai, br, bi):
    return ar * br - ai * bi, ar * bi + ai * br


def _s5_prep(lr, li, ldt, br, bi):
    s = N_STATE
    expand = _group_expand_matrix()

    def body(lr_ref, li_ref, ldt_ref, br_ref, bi_ref, e_ref, abr_ref, abi_ref, bbr_ref, bbi_ref, tab_ref):
        ab_r, ab_i, bb_r, bb_i = _discretise(lr_ref[...], li_ref[...], ldt_ref[...], br_ref[...],
                                             bi_ref[...], e_ref[...])
        abr_ref[...] = ab_r
        abi_ref[...] = ab_i
        bbr_ref[...] = bb_r
        bbi_ref[...] = bb_i
        row = lax.broadcasted_iota(jnp.int32, (8, s), 0)
        for base, sign in ((0, 1.0), (8, -1.0)):
            p = [None] * 9
            p[1] = (ab_r, sign * ab_i)
            p[2] = _cmul(*p[1], *p[1])
            p[3] = _cmul(*p[2], *p[1])
            p[4] = _cmul(*p[2], *p[2])
            p[5] = _cmul(*p[4], *p[1])
            p[6] = _cmul(*p[4], *p[2])
            p[7] = _cmul(*p[4], *p[3])
            p[8] = _cmul(*p[4], *p[4])
            for j, k in enumerate((1, 2, 4)):
                keep = (row >= k) if base == 0 else (row < 8 - k)
                tab_ref[base + 2 * j] = jnp.where(keep, jnp.broadcast_to(p[k][0], (8, s)), 0.0)
                tab_ref[base + 2 * j + 1] = jnp.where(keep, jnp.broadcast_to(p[k][1], (8, s)), 0.0)
            pw_r = jnp.zeros((8, s), F32)
            pw_i = jnp.zeros((8, s), F32)
            for i in range(8):
                e = i + 1 if base == 0 else 8 - i
                pw_r = jnp.where(row == i, jnp.broadcast_to(p[e][0], (8, s)), pw_r)
                pw_i = jnp.where(row == i, jnp.broadcast_to(p[e][1], (8, s)), pw_i)
            tab_ref[base + 6] = pw_r
            tab_ref[base + 7] = pw_i

    def full(shape):
        return pl.BlockSpec(shape, lambda i: (0,) * len(shape))

    g = D_MODEL // SSM_GROUP
    return _call(body, "s5_prep", (1,),
                 [full((1, s)), full((1, s)), full((1, g)), full((16, s)), full((16, s)), full((g, s))],
                 [full((1, s)), full((1, s)), full((16, s)), full((16, s)), full((16, 8, s))],
                 [_sds((1, s), F32), _sds((1, s), F32), _sds((16, s), F32), _sds((16, s), F32),
                  _sds((16, 8, s), F32)])(lr, li, ldt, br, bi, expand)


def _s5_prep_bwd(lr, li, ldt, br, bi, d_abr, d_abi, d_bbr, d_bbi):
    s = N_STATE
    g = D_MODEL // SSM_GROUP
    expand = _group_expand_matrix()

    def body(lr_ref, li_ref, ldt_ref, br_ref, bi_ref, e_ref, c1, c2, c3, c4, o1, o2, o3, o4, o5):
        e_val = e_ref[...]
        _, vjp = jax.vjp(lambda a, b, c, d, e: _discretise(a, b, c, d, e, e_val),
                         lr_ref[...], li_ref[...], ldt_ref[...], br_ref[...], bi_ref[...])
        grads = vjp((c1[...], c2[...], c3[...], c4[...]))
        for o, gr in zip((o1, o2, o3, o4, o5), grads):
            o[...] = gr

    def full(shape):
        return pl.BlockSpec(shape, lambda i: (0,) * len(shape))

    return _call(body, "s5_prep_bwd", (1,),
                 [full((1, s)), full((1, s)), full((1, g)), full((16, s)), full((16, s)), full((g, s)),
                  full((1, s)), full((1, s)), full((16, s)), full((16, s))],
                 [full((1, s)), full((1, s)), full((1, g)), full((16, s)), full((16, s))],
                 [_sds((1, s), F32), _sds((1, s), F32), _sds((1, g), F32), _sds((16, s), F32),
                  _sds((16, s), F32)])(lr, li, ldt, br, bi, expand, d_abr, d_abi, d_bbr, d_bbi)


def _scan8(vr, vi, tab_ref, base, reverse):
    for j, k in enumerate((1, 2, 4)):
        mr = tab_ref[base + 2 * j]
        mi = tab_ref[base + 2 * j + 1]
        shift = 8 - k if reverse else k
        sr = pltpu.roll(vr, shift, 0)
        si = pltpu.roll(vi, shift, 0)
        vr, vi = vr + mr * sr - mi * si, vi + mr * si + mi * sr
    return vr, vi


def _s5_scan_fwd(x, bbr, bbi, cr, ci, dvec, tab, seq, comm=None):
    n, d = x.shape
    nt = seq // T_SCAN
    nblk = T_SCAN // 8
    st = CHUNK_ST

    def body(x_ref, bbr_ref, bbi_ref, cr_ref, ci_ref, d_ref, tab_ref, y_ref, hr_ref, hi_ref, car_r, car_i):
        t = pl.program_id(2)

        @pl.when(t == 0)
        def _():
            car_r[...] = jnp.zeros_like(car_r)
            car_i[...] = jnp.zeros_like(car_i)

        u = x_ref[...]
        ub = u.astype(BF16)
        hr_ref[...] = jnp.dot(ub, bbr_ref[...], preferred_element_type=F32)
        hi_ref[...] = jnp.dot(ub, bbi_ref[...], preferred_element_type=F32)

        def step(i, carry):
            sl = pl.ds(pl.multiple_of(i * 8, 8), 8)
            vr, vi = _scan8(hr_ref[sl, :], hi_ref[sl, :], tab_ref, 0, False)
            c_r = jnp.broadcast_to(car_r[...], (8, st))
            c_i = jnp.broadcast_to(car_i[...], (8, st))
            pr = tab_ref[6]
            pi = tab_ref[7]
            vr, vi = vr + pr * c_r - pi * c_i, vi + pr * c_i + pi * c_r
            hr_ref[sl, :] = vr
            hi_ref[sl, :] = vi
            car_r[...] = vr[7:8, :]
            car_i[...] = vi[7:8, :]
            return carry

        lax.fori_loop(0, nblk, step, 0, unroll=4)
        y = lax.dot_general(hr_ref[...].astype(BF16), cr_ref[...], NT, preferred_element_type=F32)
        y = y - lax.dot_general(hi_ref[...].astype(BF16), ci_ref[...], NT, preferred_element_type=F32)
        y_ref[...] = y + d_ref[...] * u

    row = lambda c, b, t: (b * nt + t, c)
    mat = pl.BlockSpec((None, CHUNK_CH, st), lambda c, b, t: (c, 0, 0))
    return _call(
        body, "s5_scan_fwd", (SSM_CHUNKS, n // seq, nt),
        [pl.BlockSpec((T_SCAN, CHUNK_CH), row), mat, mat, mat, mat,
         pl.BlockSpec((1, CHUNK_CH), lambda c, b, t: (0, c)),
         pl.BlockSpec((16, 8, st), lambda c, b, t: (0, 0, c))],
        [pl.BlockSpec((T_SCAN, CHUNK_CH), row), pl.BlockSpec((T_SCAN, st), row), pl.BlockSpec((T_SCAN, st), row)],
        [_sds((n, d), F32), _sds((n, N_STATE), F32), _sds((n, N_STATE), F32)],
        [pltpu.VMEM((1, st), F32), pltpu.VMEM((1, st), F32)], comm=comm,
    )(x, bbr, bbi, cr, ci, dvec, tab)


def _s5_scan_bwd(x, dy, dres, hr, hi, bbr, bbi, cr, ci, dvec, tab, seq, comm=None):
    n, d = x.shape
    nt = seq // T_SCAN
    nblk = T_SCAN // 8
    st = CHUNK_ST

    def body(x_ref, dy_ref, dres_ref, hr_ref, hi_ref, hpr_ref, hpi_ref, bbr_ref, bbi_ref, cr_ref, ci_ref,
             d_ref, tab_ref, dx_ref, dcr_ref, dci_ref, dbr_ref, dbi_ref, dar_ref, dai_ref, dd_ref,
             g_r, g_i, car_r, car_i):
        b = pl.program_id(1)
        tg = pl.program_id(2)
        first = jnp.logical_and(b == 0, tg == 0)

        @pl.when(tg == 0)
        def _():
            car_r[...] = jnp.zeros_like(car_r)
            car_i[...] = jnp.zeros_like(car_i)

        u = x_ref[...]
        dyv = dy_ref[...]
        ub = u.astype(BF16)
        dyb = dyv.astype(BF16)
        g_r[...] = jnp.dot(dyb, cr_ref[...], preferred_element_type=F32)
        g_i[...] = -jnp.dot(dyb, ci_ref[...], preferred_element_type=F32)

        def step(ii, carry):
            i = nblk - 1 - ii
            sl = pl.ds(pl.multiple_of(i * 8, 8), 8)
            vr, vi = _scan8(g_r[sl, :], g_i[sl, :], tab_ref, 8, True)
            c_r = jnp.broadcast_to(car_r[...], (8, st))
            c_i = jnp.broadcast_to(car_i[...], (8, st))
            pr = tab_ref[14]
            pi = tab_ref[15]
            vr, vi = vr + pr * c_r - pi * c_i, vi + pr * c_i + pi * c_r
            g_r[sl, :] = vr
            g_i[sl, :] = vi
            car_r[...] = vr[0:1, :]
            car_i[...] = vi[0:1, :]
            return carry

        lax.fori_loop(0, nblk, step, 0, unroll=4)
        gr = g_r[...]
        gi = g_i[...]
        grb = gr.astype(BF16)
        gib = gi.astype(BF16)
        du = lax.dot_general(grb, bbr_ref[...], NT, preferred_element_type=F32)
        du = du + lax.dot_general(gib, bbi_ref[...], NT, preferred_element_type=F32)
        dx_ref[...] = DN_ALPHA * dres_ref[...] + du + d_ref[...] * dyv

        h_r = hr_ref[...]
        h_i = hi_ref[...]
        t_orig = nt - 1 - tg
        row0 = lax.broadcasted_iota(jnp.int32, (8, st), 0) == 0
        halo_ok = t_orig > 0

        def previous(h, halo_ref):
            top = jnp.broadcast_to(jnp.where(halo_ok, halo_ref[7:8, :], 0.0), (8, st))
            rolled = pltpu.roll(h, 1, 0)
            return jnp.concatenate([jnp.where(row0, top, rolled[:8, :]), rolled[8:, :]], axis=0)

        hp_r = previous(h_r, hpr_ref)
        hp_i = previous(h_i, hpi_ref)
        dar = jnp.sum(gr * hp_r + gi * hp_i, axis=0, keepdims=True)
        dai = jnp.sum(gi * hp_r - gr * hp_i, axis=0, keepdims=True)
        dcr = lax.dot_general(dyb, h_r.astype(BF16), TN, preferred_element_type=F32)
        dci = lax.dot_general(dyb, h_i.astype(BF16), TN, preferred_element_type=F32)
        dbr = lax.dot_general(ub, grb, TN, preferred_element_type=F32)
        dbi = lax.dot_general(ub, gib, TN, preferred_element_type=F32)
        dd = jnp.sum(dyv * u, axis=0, keepdims=True)

        @pl.when(first)
        def _():
            dcr_ref[...] = dcr
            dci_ref[...] = dci
            dbr_ref[...] = dbr
            dbi_ref[...] = dbi
            dar_ref[...] = dar
            dai_ref[...] = dai
            dd_ref[...] = dd

        @pl.when(jnp.logical_not(first))
        def _():
            dcr_ref[...] += dcr
            dci_ref[...] += dci
            dbr_ref[...] += dbr
            dbi_ref[...] += dbi
            dar_ref[...] += dar
            dai_ref[...] += dai
            dd_ref[...] += dd

    row = lambda c, b, t: (b * nt + (nt - 1 - t), c)
    halo = lambda c, b, t: (jnp.maximum((b * nt + (nt - 1 - t)) * (T_SCAN // 8) - 1, 0), c)
    mat = pl.BlockSpec((None, CHUNK_CH, st), lambda c, b, t: (c, 0, 0))
    chan = pl.BlockSpec((T_SCAN, CHUNK_CH), row)
    state = pl.BlockSpec((T_SCAN, st), row)
    return _call(
        body, "s5_scan_bwd", (SSM_CHUNKS, n // seq, nt),
        [chan, chan, chan, state, state, pl.BlockSpec((8, st), halo), pl.BlockSpec((8, st), halo),
         mat, mat, mat, mat, pl.BlockSpec((1, CHUNK_CH), lambda c, b, t: (0, c)),
         pl.BlockSpec((16, 8, st), lambda c, b, t: (0, 0, c))],
        [chan, mat, mat, mat, mat, pl.BlockSpec((1, st), lambda c, b, t: (0, c)),
         pl.BlockSpec((1, st), lambda c, b, t: (0, c)), pl.BlockSpec((1, CHUNK_CH), lambda c, b, t: (0, c))],
        [_sds((n, d), F32)] + [_sds((SSM_CHUNKS, CHUNK_CH, st), F32)] * 4
        + [_sds((1, N_STATE), F32), _sds((1, N_STATE), F32), _sds((1, d), F32)],
        [pltpu.VMEM((T_SCAN, st), F32), pltpu.VMEM((T_SCAN, st), F32), pltpu.VMEM((1, st), F32),
         pltpu.VMEM((1, st), F32)], comm=comm,
    )(x, dy, dres, hr, hi, hr, hi, bbr, bbi, cr, ci, dvec, tab)


_GELU_C = math.sqrt(2.0 / math.pi)


def _gelu_parts(y):
    t = jnp.tanh(_GELU_C * (y + 0.044715 * (y * y * y)))
    return 0.5 * (1.0 + t), t


def _glu_fwd(y, w_glu, b_glu):
    n, d = y.shape

    def body(y_ref, w_ref, b_ref, y2_ref, z_ref, gg_ref):
        yv = y_ref[...]
        cdf, _ = _gelu_parts(yv)
        y2 = yv * cdf
        z = jnp.dot(y2.astype(BF16), w_ref[...], preferred_element_type=F32) + b_ref[...]
        y2_ref[...] = y2.astype(BF16)
        z_ref[...] = z
        gg_ref[...] = (y2 * jax.nn.sigmoid(z)).astype(BF16)

    row = pl.BlockSpec((TM, d), lambda i: (i, 0))
    return _call(body, "glu_fwd", (n // TM,),
                 [row, pl.BlockSpec((d, d), lambda i: (0, 0)), pl.BlockSpec((1, d), lambda i: (0, 0))],
                 [row, row, row], [_sds((n, d), BF16), _sds((n, d), F32), _sds((n, d), BF16)])(y, w_glu, b_glu)


def _glu_bwd_gate(y, z, dgg):
    n, d = y.shape

    def body(y_ref, z_ref, dgg_ref, dz_ref, db_ref, t_ref):
        i = pl.program_id(0)
        yv = y_ref[...]
        cdf, _ = _gelu_parts(yv)
        y2 = yv * cdf
        s = jax.nn.sigmoid(z_ref[...])
        dg = dgg_ref[...]
        dz = dg * y2 * s * (1.0 - s)
        dz_ref[...] = dz.astype(BF16)
        t_ref[...] = dg * s
        db = jnp.sum(dz, axis=0, keepdims=True)

        @pl.when(i == 0)
        def _():
            db_ref[...] = db

        @pl.when(i > 0)
        def _():
            db_ref[...] += db

    row = pl.BlockSpec((TM, d), lambda i: (i, 0))
    vec = pl.BlockSpec((1, d), lambda i: (0, 0))
    return _call(body, "glu_bwd_gate", (n // TM,), [row, row, row], [row, vec, row],
                 [_sds((n, d), BF16), _sds((1, d), F32), _sds((n, d), F32)])(y, z, dgg)


def _gelu_bwd(dy2, y):
    cdf, t = _gelu_parts(y)
    dinner = _GELU_C * (1.0 + 3.0 * 0.044715 * y * y)
    return dy2 * (cdf + 0.5 * y * (1.0 - t * t) * dinner)


HALO = 16


def _shift_down(x, halo, k):
    out = pltpu.roll(x, k, 0)
    row = lax.broadcasted_iota(jnp.int32, (8, x.shape[1]), 0)
    top = jnp.where(row < k, pltpu.roll(halo[HALO - 8:HALO, :], k, 0), out[:8, :])
    return jnp.concatenate([top, out[8:, :]], axis=0)


def _conv3(x, halo, w, b):
    x1 = _shift_down(x, halo, 1)
    x2 = _shift_down(x, halo, 2)
    return b + w[0:1, :] * x + w[1:2, :] * x1 + w[2:3, :] * x2, x1, x2


def _conv_gate_fwd(hc, conv_w, conv_b, seq, comm=None):
    _, nb, n, c = hc.shape
    tiles_per_seq = seq // TM

    def body(x_ref, halo_ref, w_ref, b_ref, o_ref, cv_ref):
        i = pl.program_id(1)
        start = (i % tiles_per_seq) == 0
        cv = []
        for h in range(2):
            halo = jnp.where(start, 0.0, halo_ref[h].astype(F32))
            cv.append(_conv3(x_ref[h].astype(F32), halo, w_ref[h], b_ref[h])[0])
            cv_ref[h] = cv[h].astype(BF16)
        o_ref[...] = (cv[1] * jax.nn.sigmoid(cv[1]) * cv[0]).astype(BF16)

    both = pl.BlockSpec((2, None, TM, c), lambda d, i: (0, d, i, 0))
    return _call(
        body, "conv_gate_fwd", (nb, n // TM),
        [both,
         pl.BlockSpec((2, None, HALO, c), lambda d, i: (0, d, jnp.maximum(i * (TM // HALO) - 1, 0), 0)),
         pl.BlockSpec((2, None, 3, c), lambda d, i: (0, d, 0, 0)),
         pl.BlockSpec((2, None, 1, c), lambda d, i: (0, d, 0, 0))],
        [pl.BlockSpec((None, TM, c), lambda d, i: (d, i, 0)), both],
        [_sds((nb, n, c), BF16), _sds((2, nb, n, c), BF16)], comm=comm)(hc, hc, conv_w, conv_b)


def _conv_gate_bwd(hc, cv, dact, conv_w, seq, comm=None):
    _, nb, n, c = hc.shape
    tiles_per_seq = seq // TM
    last_halo = n // HALO - 1
    ext = TM + HALO

    def body(x_ref, cv_ref, cvn_ref, da_ref, dan_ref, w_ref, dx_ref, dw_ref):
        i = pl.program_id(1)
        end = (i % tiles_per_seq) == tiles_per_seq - 1
        row = lax.broadcasted_iota(jnp.int32, (ext, c), 0)
        cv = [jnp.concatenate([cv_ref[h], cvn_ref[h]], axis=0).astype(F32) for h in range(2)]
        da = jnp.concatenate([da_ref[...], dan_ref[...]], axis=0).astype(F32)
        da = jnp.where(jnp.logical_and(end, row >= TM), 0.0, da)
        sg = jax.nn.sigmoid(cv[1])
        silu = cv[1] * sg
        dcv = [da * silu, da * cv[0] * (sg + silu * (1.0 - sg))]
        for h in range(2):
            w = w_ref[h]
            g = dcv[h]
            g0 = g[:TM, :]
            g1 = pltpu.roll(g, ext - 1, 0)[:TM, :]
            g2 = pltpu.roll(g, ext - 2, 0)[:TM, :]
            dx_ref[h] = (w[0:1, :] * g0 + w[1:2, :] * g1 + w[2:3, :] * g2).astype(BF16)
            x = x_ref[h].astype(F32)
            parts = [jnp.sum(g0 * x, axis=0, keepdims=True),
                     jnp.sum(g1 * x, axis=0, keepdims=True),
                     jnp.sum(g2 * x, axis=0, keepdims=True),
                     jnp.sum(g0, axis=0, keepdims=True)]
            upd = jnp.concatenate(parts + [jnp.zeros((4, c), F32)], axis=0)

            @pl.when(i == 0)
            def _():
                dw_ref[h] = upd

            @pl.when(i > 0)
            def _():
                dw_ref[h] += upd

    nxt = lambda i: jnp.minimum((i + 1) * (TM // HALO), last_halo)
    return _call(
        body, "conv_gate_bwd", (nb, n // TM),
        [pl.BlockSpec((2, None, TM, c), lambda d, i: (0, d, i, 0)),
         pl.BlockSpec((2, None, TM, c), lambda d, i: (0, d, i, 0)),
         pl.BlockSpec((2, None, HALO, c), lambda d, i: (0, d, nxt(i), 0)),
         pl.BlockSpec((None, TM, c), lambda d, i: (d, i, 0)),
         pl.BlockSpec((None, HALO, c), lambda d, i: (d, nxt(i), 0)),
         pl.BlockSpec((2, None, 3, c), lambda d, i: (0, d, 0, 0))],
        [pl.BlockSpec((2, None, TM, c), lambda d, i: (0, d, i, 0)),
         pl.BlockSpec((2, None, 8, c), lambda d, i: (0, d, 0, 0))],
        [_sds((2, nb, n, c), BF16), _sds((2, nb, 8, c), F32)], comm=comm)(hc, cv, cv, dact, dact, conv_w)


def _t5_bucket_np(dist):
    exact = REL_BUCKETS // 2
    d = np.maximum(dist, 1).astype(np.float32)
    large = exact + (np.log(d / exact) / math.log(REL_MAX_DIST / exact) * (REL_BUCKETS - exact)).astype(np.int64)
    large = np.minimum(large, REL_BUCKETS - 1)
    return np.where(dist < exact, dist, large).astype(np.int32)


def _bucket_table(dil):
    steps = np.arange(BAND)[:, None] + BAND - np.arange(2 * BAND)[None, :]
    return _t5_bucket_np(np.maximum(steps, 0) * dil).reshape(1, BAND * 2 * BAND).astype(np.float32)


def _bias_expand(rel_t, dil):
    nk = BAND * 2 * BAND
    bucket = jnp.asarray(_bucket_table(dil))

    def body(r_ref, bk_ref, o_ref):
        ids = lax.broadcasted_iota(jnp.int32, (REL_BUCKETS, nk), 0).astype(F32)
        onehot = (ids == bk_ref[...]).astype(F32)
        o_ref[...] = jnp.dot(r_ref[...], onehot, precision=HIGHEST, preferred_element_type=F32)

    return _call(body, "bias_expand", (1,),
                 [pl.BlockSpec((N_HEADS, REL_BUCKETS), lambda i: (0, 0)), pl.BlockSpec((1, nk), lambda i: (0, 0))],
                 pl.BlockSpec((N_HEADS, nk), lambda i: (0, 0)), _sds((N_HEADS, nk), F32))(rel_t, bucket)


def _bias_reduce(dbias, dil):
    nk = BAND * 2 * BAND
    bucket = jnp.asarray(_bucket_table(dil))

    def body(g_ref, bk_ref, o_ref):
        ids = lax.broadcasted_iota(jnp.int32, (REL_BUCKETS, nk), 0).astype(F32)
        onehot = (ids == bk_ref[...]).astype(F32)
        o_ref[...] = lax.dot_general(g_ref[...], onehot, NT, precision=HIGHEST, preferred_element_type=F32)

    return _call(body, "bias_reduce", (1,),
                 [pl.BlockSpec((N_HEADS, nk), lambda i: (0, 0)), pl.BlockSpec((1, nk), lambda i: (0, 0))],
                 pl.BlockSpec((N_HEADS, REL_BUCKETS), lambda i: (0, 0)),
                 _sds((N_HEADS, REL_BUCKETS), F32))(dbias, bucket)


def _valid_mask(n):
    qi = lax.broadcasted_iota(jnp.int32, (BAND, 2 * BAND), 0)
    kj = lax.broadcasted_iota(jnp.int32, (BAND, 2 * BAND), 1)
    steps = qi + BAND - kj
    in_band = jnp.logical_and(steps >= 0, steps <= BAND)
    return jnp.logical_and(in_band, jnp.logical_or(n > 0, kj >= BAND))


ATTN_COLS = {1: 1024, 4: 512, 16: 128}


def _residue(dil, r):
    return slice(None) if dil == 1 else pl.ds(r, BAND, stride=dil)


def _stack_heads(x, first):
    return jnp.concatenate([jnp.where(first, x, 0.0), jnp.where(first, 0.0, x)], axis=0)


def _attn_fwd(q, k, v, bias, g, dil, bsz, seq, comm=None):
    n = q.shape[0]
    rows = BAND * dil
    nch = seq // rows
    w = N_HEADS * HEAD_DIM
    wc = ATTN_COLS[dil]
    ncol = w // wc
    hpc = wc // HEAD_DIM
    nj = wc // 128
    slab = lambda k: pltpu.VMEM((k * nj, rows, 128), F32)

    def body(q_ref, kc_ref, kp_ref, vc_ref, vp_ref, b_ref, o_ref, l_ref, qf, kf, vf, of, lf):
        ch = pl.program_id(2)
        valid = _valid_mask(ch)
        valid = jnp.concatenate([valid, valid], axis=0)
        lane = lax.broadcasted_iota(jnp.int32, (BAND, 128), 1)
        first = lane < HEAD_DIM
        for j in range(nj):
            sl = slice(128 * j, 128 * (j + 1))
            qf[j] = q_ref[:, sl].astype(F32)
            kf[j] = kp_ref[:, sl].astype(F32)
            kf[nj + j] = kc_ref[:, sl].astype(F32)
            vf[j] = vp_ref[:, sl].astype(F32)
            vf[nj + j] = vc_ref[:, sl].astype(F32)
        for r in range(dil):
            rr = _residue(dil, r)
            for j in range(nj):
                q2 = qf.at[j][rr, :]
                k2 = jnp.concatenate([kf.at[j][rr, :], kf.at[nj + j][rr, :]], axis=0).astype(BF16)
                v2 = jnp.concatenate([vf.at[j][rr, :], vf.at[nj + j][rr, :]], axis=0).astype(BF16)
                qs = _stack_heads(q2, first).astype(BF16)
                s = lax.dot_general(qs, k2, NT, preferred_element_type=F32) * (HEAD_DIM ** -0.5)
                s = jnp.where(valid, s + b_ref[2 * j:2 * j + 2].reshape(2 * BAND, 2 * BAND), NEG_BIG)
                m = jnp.max(s, axis=1, keepdims=True)
                p = jnp.exp(s - m)
                l = jnp.sum(p, axis=1, keepdims=True)
                pv = jnp.dot(p.astype(BF16), v2, preferred_element_type=F32) / l
                lse = jnp.broadcast_to(m + jnp.log(l), (2 * BAND, 128))
                of.at[j][rr, :] = jnp.where(first, pv[:BAND], pv[BAND:])
                lf.at[j][rr, :] = jnp.where(first, lse[:BAND], lse[BAND:])
        for j in range(nj):
            sl = slice(128 * j, 128 * (j + 1))
            o_ref[:, sl] = of[j]
            l_ref[:, sl] = lf[j]

    base = g * ncol
    cur = lambda c, b, ch: (b * nch + ch, base + c)
    prev = lambda c, b, ch: (b * nch + jnp.maximum(ch - 1, 0), base + c)
    blk = lambda im: pl.BlockSpec((rows, wc), im)
    out_blk = pl.BlockSpec((rows, wc), lambda c, b, ch: (b * nch + ch, c))
    return _call(
        body, "attn_fwd_d%d" % dil, (ncol, bsz, nch),
        [blk(cur), blk(cur), blk(prev), blk(cur), blk(prev),
         pl.BlockSpec((hpc, BAND, 2 * BAND), lambda c, b, ch: (c, 0, 0))],
        [out_blk, out_blk],
        [_sds((n, w), F32)] * 2,
        [slab(1), slab(2), slab(2), slab(1), slab(1)], comm=comm,
    )(q, k, k, v, v, bias)


def _attn_combine(outs, lses):
    n, w = outs[0].shape

    def body(o0, o1, o2, l0, l1, l2, o_ref, l_ref):
        la, lb, lc = l0[...], l1[...], l2[...]
        m = jnp.maximum(jnp.maximum(la, lb), lc)
        wa, wb, wc = jnp.exp(la - m), jnp.exp(lb - m), jnp.exp(lc - m)
        tot = wa + wb + wc
        o_ref[...] = (wa * o0[...] + wb * o1[...] + wc * o2[...]) / tot
        l_ref[...] = m + jnp.log(tot)

    row = pl.BlockSpec((TM, w), lambda i: (i, 0))
    return _call(body, "attn_combine", (n // TM,), [row] * 6, [row, row],
                 [_sds((n, w), F32)] * 2)(*outs, *lses)


def _attn_bwd(q, k, v, do, o, lse, bias, g, dil, bsz, seq, comm=None, into=()):
    n = q.shape[0]
    rows = BAND * dil
    nch = seq // rows
    w = N_HEADS * HEAD_DIM
    wc = ATTN_COLS[dil]
    ncol = w // wc
    hpc = wc // HEAD_DIM
    nj = wc // 128
    slab = lambda k: pltpu.VMEM((k * nj, rows, 128), F32)
    scale = HEAD_DIM ** -0.5

    def body(q_ref, kc_ref, kp_ref, vc_ref, vp_ref, do_ref, o_ref, l_ref, b_ref,
             dq_ref, dk_ref, dv_ref, db_ref, qf, kf, vf, dof, of, lf, dqf, dkf, dvf):
        b = pl.program_id(1)
        ch = pl.program_id(2)
        cur = ch % 2
        prv = 1 - cur

        @pl.when(jnp.logical_and(b == 0, ch == 0))
        def _():
            db_ref[...] = jnp.zeros_like(db_ref)

        @pl.when(ch == 0)
        def _():
            for j in range(nj):
                dkf[nj + j] = jnp.zeros((rows, 128), F32)
                dvf[nj + j] = jnp.zeros((rows, 128), F32)

        @pl.when(ch < nch)
        def _():
            valid = _valid_mask(ch)
            valid = jnp.concatenate([valid, valid], axis=0)
            first = lax.broadcasted_iota(jnp.int32, (BAND, 128), 1) < HEAD_DIM
            for j in range(nj):
                sl = slice(128 * j, 128 * (j + 1))
                qf[j] = q_ref[:, sl].astype(F32)
                kf[j] = kp_ref[:, sl].astype(F32)
                kf[nj + j] = kc_ref[:, sl].astype(F32)
                vf[j] = vp_ref[:, sl].astype(F32)
                vf[nj + j] = vc_ref[:, sl].astype(F32)
                dof[j] = do_ref[:, sl]
                of[j] = o_ref[:, sl]
                lf[j] = l_ref[:, sl]
            for r in range(dil):
                rr = _residue(dil, r)
                for j in range(nj):
                    k2 = jnp.concatenate([kf.at[j][rr, :], kf.at[nj + j][rr, :]], axis=0).astype(BF16)
                    v2 = jnp.concatenate([vf.at[j][rr, :], vf.at[nj + j][rr, :]], axis=0).astype(BF16)
                    do2 = dof.at[j][rr, :]
                    lse2 = lf.at[j][rr, :]
                    qs = _stack_heads(qf.at[j][rr, :], first).astype(BF16)
                    dos = _stack_heads(do2, first)
                    delta = jnp.sum(dos * jnp.concatenate([of.at[j][rr, :]] * 2, axis=0), axis=1, keepdims=True)
                    dos = dos.astype(BF16)
                    lse_col = jnp.concatenate([lse2[:, 0:1], lse2[:, HEAD_DIM:HEAD_DIM + 1]], axis=0)
                    s = lax.dot_general(qs, k2, NT, preferred_element_type=F32) * scale
                    s = jnp.where(valid, s + b_ref[2 * j:2 * j + 2].reshape(2 * BAND, 2 * BAND), NEG_BIG)
                    p = jnp.exp(s - lse_col)
                    dp = lax.dot_general(dos, v2, NT, preferred_element_type=F32)
                    ds = p * (dp - delta)
                    db_ref[2 * j:2 * j + 2] += ds.reshape(2, BAND, 2 * BAND)
                    dsb = ds.astype(BF16)
                    dq2 = jnp.dot(dsb, k2, preferred_element_type=F32) * scale
                    dk2 = lax.dot_general(dsb, qs, TN, preferred_element_type=F32) * scale
                    dv2 = lax.dot_general(p.astype(BF16), dos, TN, preferred_element_type=F32)
                    dqf.at[j][rr, :] = jnp.where(first, dq2[:BAND], dq2[BAND:])
                    dkf.at[prv * nj + j][rr, :] += dk2[:BAND, :]
                    dvf.at[prv * nj + j][rr, :] += dv2[:BAND, :]
                    dkf.at[cur * nj + j][rr, :] = dk2[BAND:, :]
                    dvf.at[cur * nj + j][rr, :] = dv2[BAND:, :]
            for j in range(nj):
                dq_ref[:, 128 * j:128 * (j + 1)] = dqf[j].astype(BF16)

        for j in range(nj):
            sl = slice(128 * j, 128 * (j + 1))
            dk_ref[:, sl] = dkf[prv * nj + j].astype(BF16)
            dv_ref[:, sl] = dvf[prv * nj + j].astype(BF16)

    last = nch - 1
    base = g * ncol
    cur3 = lambda c, b, ch: (b * nch + jnp.minimum(ch, last), base + c)
    prev3 = lambda c, b, ch: (b * nch + jnp.maximum(jnp.minimum(ch, last) - 1, 0), base + c)
    cur1 = lambda c, b, ch: (b * nch + jnp.minimum(ch, last), c)
    lag3 = lambda c, b, ch: (b * nch + jnp.maximum(ch - 1, 0), base + c)
    blk = lambda im: pl.BlockSpec((rows, wc), im)
    bias_blk = pl.BlockSpec((hpc, BAND, 2 * BAND), lambda c, b, ch: (c, 0, 0))
    out = _sds(q.shape, BF16)
    return _call(
        body, "attn_bwd_d%d" % dil, (ncol, bsz, nch + 1),
        [blk(cur3), blk(cur3), blk(prev3), blk(cur3), blk(prev3), blk(cur1), blk(cur1), blk(cur1), bias_blk],
        [blk(cur3), blk(lag3), blk(lag3), bias_blk],
        [out, out, out, _sds((N_HEADS, BAND, 2 * BAND), F32)],
        [slab(1), slab(2), slab(2), slab(1), slab(1), slab(1), slab(1), slab(2), slab(2)], comm=comm, fill=into,
    )(q, k, k, v, v, do, o, lse, bias)


def _adamw(name, parts, w, m, v):
    nl, r, c = w.shape
    nparts = parts[0].shape[0]
    tr = r
    for cand in (256, 352, 128):
        if r % cand == 0 and r > cand:
            tr = cand
            break
    c1 = 1.0 - ADAM_B1 ** ADAM_STEP
    c2 = 1.0 - ADAM_B2 ** ADAM_STEP

    def body(*refs):
        p_refs = refs[:nl]
        w_ref, m_ref, v_ref, g_ref, d_ref, nm_ref, nv_ref = refs[nl:]
        layer = pl.program_id(0)
        for l in range(nl):
            @pl.when(layer == l)
            def _():
                g = p_refs[l][0].astype(F32)
                for dev in range(1, nparts):
                    g = g + p_refs[l][dev].astype(F32)
                nm = ADAM_B1 * m_ref[...] + (1.0 - ADAM_B1) * g
                nv = ADAM_B2 * v_ref[...] + (1.0 - ADAM_B2) * (g * g)
                m_hat = nm / c1
                v_hat = nv / c2
                g_ref[...] = g
                d_ref[...] = -ADAM_LR * (m_hat / (jnp.sqrt(v_hat) + ADAM_EPS) + ADAM_WD * w_ref[...])
                nm_ref[...] = nm
                nv_ref[...] = nv

    def part_spec(l):
        return pl.BlockSpec((nparts, tr, c), lambda layer, i: (0, jnp.where(layer == l, i, 0), 0))

    row = pl.BlockSpec((None, tr, c), lambda layer, i: (layer, i, 0))
    return _call(body, name, (nl, r // tr), [part_spec(l) for l in range(nl)] + [row, row, row],
                 [row] * 4, [_sds((nl, r, c), F32)] * 4)(*parts, w, m, v)


def _exchange(name, srcs, out_shapes, items):
    n_in = len(srcs)
    n_out = len(out_shapes)
    n_items = len(items)

    def body(*refs):
        ins = refs[:n_in]
        outs = refs[n_in:n_in + n_out]
        send_sems, recv_sems, local_sems = refs[n_in + n_out:]
        x, y, c = lax.axis_index("x"), lax.axis_index("y"), lax.axis_index("c")
        me = 4 * x + 2 * y + c

        def pick(ref, sel, peer):
            idx = tuple(peer if s == "peer" else me if s == "me" else s for s in sel)
            return ref.at[idx] if idx else ref

        copies = []
        for it, (si, ssel, oi, osel) in enumerate(items):
            local = pltpu.make_async_copy(pick(ins[si], ssel, me), pick(outs[oi], osel, me), local_sems.at[it])
            local.start()
            copies.append(local)
            for j in range(1, N_DEV):
                px = 1 - x if j & 4 else x
                py = 1 - y if j & 2 else y
                pc = 1 - c if j & 1 else c
                peer = 4 * px + 2 * py + pc
                sem = it * (N_DEV - 1) + j - 1
                cp = pltpu.make_async_remote_copy(
                    src_ref=pick(ins[si], ssel, peer), dst_ref=pick(outs[oi], osel, peer),
                    send_sem=send_sems.at[sem], recv_sem=recv_sems.at[sem],
                    device_id=(px, py, pc), device_id_type=pl.DeviceIdType.MESH)
                cp.start()
                copies.append(cp)
        for cp in copies:
            cp.wait()

    any_spec = pl.BlockSpec(memory_space=pl.ANY)
    return pl.pallas_call(
        body, name=name, in_specs=[any_spec] * n_in, out_specs=[any_spec] * n_out, out_shape=list(out_shapes),
        scratch_shapes=[pltpu.SemaphoreType.DMA((n_items * (N_DEV - 1),)),
                        pltpu.SemaphoreType.DMA((n_items * (N_DEV - 1),)),
                        pltpu.SemaphoreType.DMA((n_items,))],
        compiler_params=pltpu.CompilerParams(has_side_effects=True),
    )(*srcs)


def _sel(ref, sel, slot=None):
    idx = tuple(slot if s == "slot" else s for s in sel)
    return ref.at[idx] if idx else ref


def _gather_plan(srcs, out_shapes, items):
    per = N_DEV - 1
    n_items = len(items)

    def phases(ins, outs, send_sems, recv_sems, local_sems, total):
        x, y, c = lax.axis_index("x"), lax.axis_index("y"), lax.axis_index("c")
        sibling = (x, y, 1 - c)
        chips = [(1 - x, y), (x, 1 - y), (1 - x, 1 - y)]
        slot_of = lambda px, py, pc: 4 * px + 2 * py + pc

        def copy(it, k, block, to, src=None):
            _, _, oi, osel = items[it]
            dst = _sel(outs[oi], osel, slot_of(*block))
            return pltpu.make_async_remote_copy(
                src_ref=dst if src is None else src, dst_ref=dst,
                send_sem=send_sems.at[it * per + k], recv_sem=recv_sems.at[it * per + k],
                device_id=to, device_id_type=pl.DeviceIdType.MESH)

        def own_copies(it):
            si, ssel, oi, osel = items[it]
            src = _sel(ins[si], ssel)
            mine = pltpu.make_async_copy(src, _sel(outs[oi], osel, slot_of(x, y, c)), local_sems.at[it])
            first = [copy(it, 0, (x, y, c), sibling, src=src)]
            first += [copy(it, 1 + j, (x, y, c), (*chip, c), src=src) for j, chip in enumerate(chips)]
            return mine, first

        def start():
            for it in range(n_items):
                mine, first = own_copies(it)
                mine.start()
                for cp in first:
                    cp.start()

        def forward(it):
            def go():
                for j, chip in enumerate(chips):
                    copy(it, 1 + j, (*chip, c), (x, y, c)).wait_recv()
                    copy(it, 4 + j, (*chip, c), sibling).start()
            return go

        def finish():
            for it in range(n_items):
                copy(it, 0, sibling, (x, y, c)).wait_recv()
                for j, chip in enumerate(chips):
                    copy(it, 4 + j, (*chip, 1 - c), (x, y, c)).wait_recv()
            for it in range(n_items):
                mine, first = own_copies(it)
                for cp in first:
                    cp.wait_send()
                for j, chip in enumerate(chips):
                    copy(it, 4 + j, (*chip, c), sibling).wait_send()
                mine.wait()

        lo = max(total // 3, 1)
        acts = [(0, start)]
        for it in range(n_items):
            acts.append((min(lo + it * (total - 1 - lo) // n_items, total - 2), forward(it)))
        acts.append((total - 1, finish))
        return acts

    return _Plan(srcs, out_shapes, n_items * per, n_items, phases)


def _scatter_plan(srcs, out_shapes, items):
    per = N_DEV - 1
    n_items = len(items)

    def phases(ins, outs, send_sems, recv_sems, local_sems, total):
        x, y, c = lax.axis_index("x"), lax.axis_index("y"), lax.axis_index("c")
        me = 4 * x + 2 * y + c

        def copies():
            out = []
            for it, (si, oi, osel) in enumerate(items):
                dst = _sel(outs[oi], osel, me)
                out.append(pltpu.make_async_copy(ins[si].at[me], dst, local_sems.at[it]))
                for j in range(1, N_DEV):
                    px = 1 - x if j & 4 else x
                    py = 1 - y if j & 2 else y
                    pc = 1 - c if j & 1 else c
                    out.append(pltpu.make_async_remote_copy(
                        src_ref=ins[si].at[4 * px + 2 * py + pc], dst_ref=dst,
                        send_sem=send_sems.at[it * per + j - 1], recv_sem=recv_sems.at[it * per + j - 1],
                        device_id=(px, py, pc), device_id_type=pl.DeviceIdType.MESH))
            return out

        def start():
            for cp in copies():
                cp.start()

        def finish():
            for cp in copies():
                cp.wait()

        return [(0, start), (total - 1, finish)]

    return _Plan(srcs, out_shapes, n_items * per, n_items, phases)


def _sum_slots(parts):
    _, r, c = parts.shape

    def body(p_ref, o_ref):
        acc = p_ref[0]
        for dev in range(1, N_DEV):
            acc = acc + p_ref[dev]
        o_ref[...] = acc

    return _call(body, "sum_slots", (1,), [pl.BlockSpec((N_DEV, r, c), lambda i: (0, 0, 0))],
                 pl.BlockSpec((r, c), lambda i: (0, 0)), _sds((r, c), F32))(parts)


def _pack(arrays):
    rows = []
    for a in arrays:
        flat = a.reshape(-1).astype(F32)
        pad = (-flat.shape[0]) % 1024
        if pad:
            flat = jnp.concatenate([flat, jnp.zeros((pad,), F32)])
        rows.append(flat.reshape(-1, 128))
    return jnp.concatenate(rows, axis=0)


def _unpack(buf, shapes):
    out = []
    r0 = 0
    for shp in shapes:
        size = int(np.prod(shp))
        nrows = -(-size // 1024) * 8
        out.append(buf[r0:r0 + nrows].reshape(-1)[:size].reshape(shp))
        r0 += nrows
    return out


def _block_diag_chunks(val):
    eye = jnp.eye(8, dtype=val.dtype)
    return jnp.einsum("cjhp,jk->cjhkp", val, eye).reshape(SSM_CHUNKS, CHUNK_CH, CHUNK_ST)


def _diag_blocks(dense):
    d5 = dense.reshape(SSM_CHUNKS, 8, SSM_GROUP, 8, SSM_STATE)
    return jnp.stack([d5[:, j, :, j, :] for j in range(8)], axis=1)


def _carried(res, comm):
    return res if comm is not None else (res, None)


def _ffn_fwd(tag, hin, hin_b, w_up, conv_w, conv_b, w_down4, gain, bias, seq, comm_up=None, comm_conv=None,
             comm_down=None):
    hc, got_up = _carried(_mm_nn("ffn_up" + tag, hin_b, w_up, FFN_HIDDEN, blocked_out=True, comm=comm_up), comm_up)
    nb, n, c = hc.shape
    hc = hc.reshape(2, nb // 2, n, c)
    (act, cv), got_conv = _carried(_conv_gate_fwd(hc, conv_w, conv_b, seq, comm=comm_conv), comm_conv)
    (r, hout, hout_b), got_down = _carried(
        _mm_nn_red("ffn_down" + tag, act, w_down4, F32, comm=comm_down, residual=hin, norm=(gain, bias)), comm_down)
    return hout, hout_b, (hin_b, hc, cv, act, r), (got_up, got_conv, got_down)


def _ffn_bwd(tag, saved, dh, w_up, conv_w, conv_b, w_down4, gain, seq, comm_conv=None, comm_dw=None,
             comm_dx=None):
    hin_b, hc, cv, act, r = saved
    dr, dr_b, dgain, dbias = _ln_bwd("ffn_norm_bwd" + tag, r, dh, gain)
    d_wdown = _mm_tn("ffn_down_dw" + tag, act, dr_b, act.shape[0], BF16, a_blocked=True, g_mode="shared")
    dact = _mm_nt_out("ffn_down_dx" + tag, dr_b, w_down4, FFN_HIDDEN)
    (dhc, dconv), got_conv = _carried(_conv_gate_bwd(hc, cv, dact, conv_w, seq, comm=comm_conv), comm_conv)
    dhc8 = dhc.reshape(2 * dhc.shape[1], dhc.shape[2], dhc.shape[3])
    d_wup, got_dw = _carried(_mm_tn("ffn_up_dw" + tag, dhc8, hin_b, dhc8.shape[0], BF16, a_blocked=True,
                                    g_mode="shared", comm=comm_dw), comm_dw)
    dhin, got_dx = _carried(_mm_nt_red("ffn_up_dx" + tag, dhc8, w_up, F32, g_blocked=True, add=dr,
                                       add_scale=DN_ALPHA, comm=comm_dx), comm_dx)
    return dhin, d_wup, d_wdown, dconv, dgain, dbias, (got_conv, got_dw, got_dx)


def _local_step(x, target, p, wfull, bsz, seq, ex=None):
    n = bsz * seq
    xf = x.reshape(n, D_MODEL)
    tf = target.reshape(n, D_MODEL)

    lr = p["s5_lam_re"].reshape(1, N_STATE)
    li = p["s5_lam_im"].reshape(1, N_STATE)
    ldt = p["s5_log_dt"].reshape(1, D_MODEL // SSM_GROUP)
    br_t = p["s5_b_re"].reshape(N_STATE, SSM_GROUP).T
    bi_t = p["s5_b_im"].reshape(N_STATE, SSM_GROUP).T
    ab_r, ab_i, bb_r, bb_i, tab = _s5_prep(lr, li, ldt, br_t, bi_t)

    def bb_chunks(bb):
        return _block_diag_chunks(bb.reshape(SSM_GROUP, SSM_CHUNKS, 8, SSM_STATE).transpose(1, 2, 0, 3))

    def c_chunks(cc):
        return _block_diag_chunks(cc.reshape(SSM_CHUNKS, 8, SSM_GROUP, SSM_STATE))

    bbr_c, bbi_c = bb_chunks(bb_r).astype(BF16), bb_chunks(bb_i).astype(BF16)
    cr_c, ci_c = c_chunks(p["s5_c_re"]).astype(BF16), c_chunks(p["s5_c_im"]).astype(BF16)
    dvec = p["s5_d"].reshape(1, D_MODEL)

    if ex is None:
        y, h_r, h_i = _s5_scan_fwd(xf, bbr_c, bbi_c, cr_c, ci_c, dvec, tab, seq)
    else:
        (y, h_r, h_i), gathered = _s5_scan_fwd(xf, bbr_c, bbi_c, cr_c, ci_c, dvec, tab, seq,
                                               comm=ex.gather_first())
        p, wfull = ex.take_first(gathered, p)
    ln_g = p["ln_gain"].reshape(4, 1, D_MODEL)
    ln_b = p["ln_bias"].reshape(4, 1, D_MODEL)
    y2, z, gg = _glu_fwd(y, wfull["s5_w_glu"], p["s5_b_glu"].reshape(1, D_MODEL))
    r1, h1, h1b = _mm_nn("s5_out", gg, wfull["s5_w_out"][None], F32, residual=xf, norm=(ln_g[0], ln_b[0]))
    plans = [ex.gather_attn(k) if ex else None for k in ("kv0", "q", "kv1")]
    h2, h2b, ffn0_saved, got = _ffn_fwd(
        "0", h1, h1b, wfull["ffn_w_up"][0], p["conv_w"][0], p["conv_b"][0], wfull["ffn_w_down"][0],
        ln_g[1], ln_b[1], seq, comm_up=plans[0], comm_conv=plans[1], comm_down=plans[2])
    if ex is not None:
        wfull = ex.take_attn({"kv0": got[0][0], "q": got[1][0], "kv1": got[2][0]}, wfull)

    plan = ex.gather_layer1("up", 0) if ex else None
    kmat, got = _carried(_mm_nn("attn_k", h2b, wfull["attn_w_kv"], BF16, blocks=(0, 4), comm=plan), plan)
    halves = {"up": [got[0]], "down": []} if ex else None
    vmat = _mm_nn("attn_v", h2b, wfull["attn_w_kv"], BF16, blocks=(4, 4))
    plan = ex.gather_attn("out") if ex else None
    qmat, got = _carried(_mm_nn("attn_q", h2b, wfull["attn_w_q"], BF16, comm=plan), plan)
    if ex is not None:
        wfull = ex.take_attn({"out": got[0]}, wfull)
    biases, outs, lses = [], [], []
    for g, dil in enumerate(DILATIONS):
        rel_t = p["rel_bias"][:, g * N_HEADS:(g + 1) * N_HEADS].T
        biases.append(_bias_expand(rel_t, dil).reshape(N_HEADS, BAND, 2 * BAND))
        which, half = {1: ("down", 0), 4: ("down", 1), 16: ("up", 1)}[dil]
        plan = ex.gather_layer1(which, half) if ex else None
        (o_g, l_g), got = _carried(_attn_fwd(qmat, kmat, vmat, biases[g], g, dil, bsz, seq, comm=plan), plan)
        if plan is not None:
            halves[which].append(got[0])
        outs.append(o_g)
        lses.append(l_g)
    if ex is not None:
        wfull = ex.take_layer1(halves, wfull)
    o, lse = _attn_combine(outs, lses)
    r3, h3, h3b = _mm_nn("attn_out", o, wfull["attn_w_out"][None], F32, residual=h2, norm=(ln_g[2], ln_b[2]))
    h4, _, ffn1_saved, _ = _ffn_fwd("1", h3, h3b, wfull["ffn_w_up"][1], p["conv_w"][1], p["conv_b"][1],
                                    wfull["ffn_w_down"][1], ln_g[3], ln_b[3], seq)
    loss_tile, dh4 = _loss_head(h4, tf)

    grads = {}
    dh3, grads["ffn_w_up1"], grads["ffn_w_down1"], dconv1, dg3, db3, _ = _ffn_bwd(
        "1", ffn1_saved, dh4, wfull["ffn_w_up"][1], p["conv_w"][1], p["conv_b"][1], wfull["ffn_w_down"][1],
        ln_g[3], seq)
    dr3, dr3b, dg2, db2 = _ln_bwd("attn_norm_bwd", r3, dh3, ln_g[2])
    grads["attn_w_out"] = _mm_tn("attn_out_dw", o, dr3b, 1, BF16, g_mode="shared")
    do = _mm_nt_red("attn_out_dx", dr3b, wfull["attn_w_out"][None], F32, g_blocked=False)
    dqkv, drel = (), []
    for g, dil in enumerate(DILATIONS):
        plan = ex.scatter(grads, ("ffn_w_up1", "ffn_w_down1")) if ex and dil == DILATIONS[-1] else None
        res, got = _carried(_attn_bwd(qmat, kmat, vmat, do, o, lse, biases[g], g, dil, bsz, seq, comm=plan,
                                      into=dqkv), plan)
        if plan is not None:
            ex.keep(plan, got)
        dqkv, db_g = tuple(res[:3]), res[3]
        drel.append(_bias_reduce(db_g.reshape(N_HEADS, BAND * 2 * BAND), dil).T)
    dq, dk, dv = dqkv
    grads["rel_bias"] = jnp.concatenate(drel, axis=1)
    grads["attn_w_q"] = _mm_tn("attn_q_dw", h2b, dq, N_DEV, BF16)
    grads["attn_w_kv"] = jnp.concatenate([_mm_tn("attn_k_dw", h2b, dk, 4, BF16),
                                          _mm_tn("attn_v_dw", h2b, dv, 4, BF16)], axis=0)
    dh2 = _mm_nt_red("attn_q_dx", dq, wfull["attn_w_q"], F32, g_blocked=False, add=dr3, add_scale=DN_ALPHA)
    dh2 = _mm_nt_red("attn_k_dx", dk, wfull["attn_w_kv"], F32, g_blocked=False, add=dh2, blocks=(0, 4))
    dh2 = _mm_nt_red("attn_v_dx", dv, wfull["attn_w_kv"], F32, g_blocked=False, add=dh2, blocks=(4, 4))

    plans = [ex.scatter(grads, names) if ex else None
             for names in (("attn_w_kv",), ("attn_w_out",), ("attn_w_q",))]
    dh1, grads["ffn_w_up0"], grads["ffn_w_down0"], dconv0, dg1, db1, got = _ffn_bwd(
        "0", ffn0_saved, dh2, wfull["ffn_w_up"][0], p["conv_w"][0], p["conv_b"][0], wfull["ffn_w_down"][0],
        ln_g[1], seq, comm_conv=plans[0], comm_dw=plans[1], comm_dx=plans[2])
    if ex is not None:
        for plan, outs in zip(plans, got):
            ex.keep(plan, outs)
    dr1, dr1b, dg0, db0 = _ln_bwd("s5_norm_bwd", r1, dh1, ln_g[0])
    grads["s5_w_out"] = _mm_tn("s5_out_dw", gg, dr1b, 1, BF16, g_mode="shared")
    dgg = _mm_nt_red("s5_out_dx", dr1b, wfull["s5_w_out"][None], F32, g_blocked=False)
    dz, d_bglu, dy2_direct = _glu_bwd_gate(y, z, dgg)
    grads["s5_w_glu"] = _mm_tn("s5_glu_dw", y2, dz, 1, BF16, g_mode="shared")
    dy = _mm_nt_red("s5_glu_dx", dz, wfull["s5_w_glu"][None], F32, g_blocked=False, add=dy2_direct,
                    post=(_gelu_bwd, y))
    plan_last = ex.scatter(grads, ("ffn_w_up0", "ffn_w_down0", "s5_w_out", "s5_w_glu")) if ex else None
    res = _s5_scan_bwd(xf, dy, dr1, h_r, h_i, bbr_c, bbi_c, cr_c, ci_c, dvec, tab, seq, comm=plan_last)
    if ex is not None:
        res, got = res
        ex.keep(plan_last, got)
    dx, dcr, dci, dbr, dbi, dar, dai, dd = res

    def bb_from_chunks(dense):
        return _diag_blocks(dense).transpose(2, 0, 1, 3).reshape(SSM_GROUP, N_STATE)

    d_lr, d_li, d_ldt, d_br, d_bi = _s5_prep_bwd(lr, li, ldt, br_t, bi_t, dar, dai,
                                                 bb_from_chunks(dbr), bb_from_chunks(dbi))
    grads["s5_lam_re"] = d_lr.reshape(p["s5_lam_re"].shape)
    grads["s5_lam_im"] = d_li.reshape(p["s5_lam_im"].shape)
    grads["s5_log_dt"] = d_ldt.reshape(p["s5_log_dt"].shape)
    grads["s5_b_re"] = d_br.T.reshape(p["s5_b_re"].shape)
    grads["s5_b_im"] = d_bi.T.reshape(p["s5_b_im"].shape)
    grads["s5_c_re"] = _diag_blocks(dcr).reshape(p["s5_c_re"].shape)
    grads["s5_c_im"] = -_diag_blocks(dci).reshape(p["s5_c_im"].shape)
    grads["s5_d"] = dd.reshape(p["s5_d"].shape)
    grads["s5_b_glu"] = d_bglu.reshape(1, D_MODEL)
    dconv = jnp.stack([dconv0, dconv1]).reshape(2, N_DEV, 8, -1)
    grads["ffn_conv_w"] = dconv[:, :, 0:3, :].transpose(0, 2, 1, 3)
    grads["ffn_conv_b"] = dconv[:, :, 3, :]
    grads["ln_gain"] = jnp.stack([dg0, dg1, dg2, dg3]).reshape(2, 2, D_MODEL)
    grads["ln_bias"] = jnp.stack([db0, db1, db2, db3]).reshape(2, 2, D_MODEL)
    return loss_tile[0, 0], dx.reshape(x.shape), grads


SMALL_REPLICATED = ("s5_lam_re", "s5_lam_im", "s5_log_dt", "s5_b_re", "s5_b_im", "s5_c_re", "s5_c_im", "s5_d",
                    "rel_bias", "ffn_conv_b")
SMALL_SHARDED = ("s5_b_glu", "ffn_conv_w", "ln_gain", "ln_bias")
BIG = ("s5_w_glu", "s5_w_out", "attn_w_kv", "attn_w_q", "attn_w_out", "ffn_w_up", "ffn_w_down")
WEIGHTS = ("s5_lam_re", "s5_lam_im", "s5_log_dt", "s5_b_re", "s5_b_im", "s5_c_re", "s5_c_im", "s5_d", "s5_w_glu",
           "s5_b_glu", "s5_w_out", "attn_w_kv", "attn_w_q", "attn_w_out", "rel_bias", "ffn_w_up", "ffn_conv_w",
           "ffn_conv_b", "ffn_w_down", "ln_gain", "ln_bias")


class _Exchanges:
    def __init__(self, w):
        self.w = w
        self.parts = {}
        self.up = w["ffn_w_up"].astype(BF16)
        self.down = w["ffn_w_down"].astype(BF16)

    def gather_first(self):
        w = self.w
        srcs = [w["s5_w_glu"][0].astype(BF16), w["s5_w_out"][0].astype(BF16),
                _pack([w[k] for k in SMALL_SHARDED]), self.up, self.down]
        outs = [_sds((N_DEV,) + (s.shape[1:] if i in (3, 4) else s.shape), s.dtype) for i, s in enumerate(srcs)]
        items = [(i, (0,) if i in (3, 4) else (), i, ("slot",)) for i in range(len(srcs))]
        return _gather_plan(srcs, outs, items)

    def gather_attn(self, piece):
        w = self.w
        if piece in ("kv0", "kv1"):
            src, rows = w["attn_w_kv"].astype(BF16), w["attn_w_kv"].shape[0] // 2
            return _gather_plan([src], [_sds((N_DEV, rows, src.shape[1]), BF16)],
                                [(0, (pl.ds(int(piece[2]) * rows, rows),), 0, ("slot",))])
        src = (w["attn_w_q"] if piece == "q" else w["attn_w_out"])[0].astype(BF16)
        return _gather_plan([src], [_sds((N_DEV,) + src.shape, BF16)], [(0, (), 0, ("slot",))])

    def take_attn(self, pieces, wfull):
        wfull = dict(wfull)
        if "kv0" in pieces:
            wfull["attn_w_kv"] = jnp.concatenate([pieces["kv0"], pieces["kv1"]], axis=1)
            wfull["attn_w_q"] = pieces["q"]
        if "out" in pieces:
            wfull["attn_w_out"] = pieces["out"].reshape(D_MODEL, D_MODEL)
        return wfull

    def take_first(self, g, p):
        d = D_MODEL
        w = self.w
        wfull = {
            "s5_w_glu": g[0].reshape(d, d),
            "s5_w_out": g[1].reshape(d, d),
            "ffn_w_up": [g[3], None],
            "ffn_w_down": [g[4].reshape(4, -1, d), None],
        }
        smalls = [_unpack(g[2][dev], [w[k].shape for k in SMALL_SHARDED]) for dev in range(N_DEV)]
        c_ff = w["ffn_conv_w"].shape[2]
        conv_w = jnp.stack([s[1] for s in smalls], axis=2)
        p = dict(p)
        p.update(s5_b_glu=jnp.concatenate([s[0] for s in smalls], axis=1),
                 ln_gain=jnp.concatenate([s[2] for s in smalls], axis=2),
                 ln_bias=jnp.concatenate([s[3] for s in smalls], axis=2),
                 conv_w=conv_w.transpose(0, 2, 1, 3).reshape(2, 2, 4, 3, c_ff),
                 conv_b=w["ffn_conv_b"].reshape(2, 2, 4, 1, c_ff))
        return p, wfull

    def gather_layer1(self, which, half):
        src = self.up if which == "up" else self.down
        rows = src.shape[1] // 2
        return _gather_plan([src], [_sds((N_DEV, rows) + src.shape[2:], BF16)],
                            [(0, (1, pl.ds(half * rows, rows)), 0, ("slot",))])

    def take_layer1(self, halves, wfull):
        up = jnp.concatenate(halves["up"], axis=1)
        down = jnp.concatenate(halves["down"], axis=1)
        wfull = dict(wfull)
        wfull["ffn_w_up"] = [wfull["ffn_w_up"][0], up]
        wfull["ffn_w_down"] = [wfull["ffn_w_down"][0], down.reshape(4, -1, D_MODEL)]
        return wfull

    def scatter(self, grads, names):
        srcs = [grads[k].reshape(N_DEV, -1, grads[k].shape[-1]) for k in names]
        plan = _scatter_plan(srcs, [_sds(s.shape, BF16) for s in srcs],
                             [(i, i, ("slot",)) for i in range(len(names))])
        plan.names = names
        return plan

    def keep(self, plan, outs):
        for k, o in zip(plan.names, outs):
            self.parts[k] = o


def kernel(x, s5_lam_re, s5_lam_im, s5_log_dt, s5_b_re, s5_b_im, s5_c_re, s5_c_im, s5_d, s5_w_glu, s5_b_glu, s5_w_out, attn_w_kv, attn_w_q, attn_w_out, rel_bias, ffn_w_up, ffn_conv_w, ffn_conv_b, ffn_w_down, ln_gain, ln_bias, loss_target, m_s5_lam_re, m_s5_lam_im, m_s5_log_dt, m_s5_b_re, m_s5_b_im, m_s5_c_re, m_s5_c_im, m_s5_d, m_s5_w_glu, m_s5_b_glu, m_s5_w_out, m_attn_w_kv, m_attn_w_q, m_attn_w_out, m_rel_bias, m_ffn_w_up, m_ffn_conv_w, m_ffn_conv_b, m_ffn_w_down, m_ln_gain, m_ln_bias, v_s5_lam_re, v_s5_lam_im, v_s5_log_dt, v_s5_b_re, v_s5_b_im, v_s5_c_re, v_s5_c_im, v_s5_d, v_s5_w_glu, v_s5_b_glu, v_s5_w_out, v_attn_w_kv, v_attn_w_q, v_attn_w_out, v_rel_bias, v_ffn_w_up, v_ffn_conv_w, v_ffn_conv_b, v_ffn_w_down, v_ln_gain, v_ln_bias):
    w = dict(s5_lam_re=s5_lam_re, s5_lam_im=s5_lam_im, s5_log_dt=s5_log_dt, s5_b_re=s5_b_re, s5_b_im=s5_b_im,
             s5_c_re=s5_c_re, s5_c_im=s5_c_im, s5_d=s5_d, s5_w_glu=s5_w_glu, s5_b_glu=s5_b_glu, s5_w_out=s5_w_out,
             attn_w_kv=attn_w_kv, attn_w_q=attn_w_q, attn_w_out=attn_w_out, rel_bias=rel_bias, ffn_w_up=ffn_w_up,
             ffn_conv_w=ffn_conv_w, ffn_conv_b=ffn_conv_b, ffn_w_down=ffn_w_down, ln_gain=ln_gain, ln_bias=ln_bias)
    mom = dict(s5_lam_re=m_s5_lam_re, s5_lam_im=m_s5_lam_im, s5_log_dt=m_s5_log_dt, s5_b_re=m_s5_b_re,
               s5_b_im=m_s5_b_im, s5_c_re=m_s5_c_re, s5_c_im=m_s5_c_im, s5_d=m_s5_d, s5_w_glu=m_s5_w_glu,
               s5_b_glu=m_s5_b_glu, s5_w_out=m_s5_w_out, attn_w_kv=m_attn_w_kv, attn_w_q=m_attn_w_q,
               attn_w_out=m_attn_w_out, rel_bias=m_rel_bias, ffn_w_up=m_ffn_w_up, ffn_conv_w=m_ffn_conv_w,
               ffn_conv_b=m_ffn_conv_b, ffn_w_down=m_ffn_w_down, ln_gain=m_ln_gain, ln_bias=m_ln_bias)
    var = dict(s5_lam_re=v_s5_lam_re, s5_lam_im=v_s5_lam_im, s5_log_dt=v_s5_log_dt, s5_b_re=v_s5_b_re,
               s5_b_im=v_s5_b_im, s5_c_re=v_s5_c_re, s5_c_im=v_s5_c_im, s5_d=v_s5_d, s5_w_glu=v_s5_w_glu,
               s5_b_glu=v_s5_b_glu, s5_w_out=v_s5_w_out, attn_w_kv=v_attn_w_kv, attn_w_q=v_attn_w_q,
               attn_w_out=v_attn_w_out, rel_bias=v_rel_bias, ffn_w_up=v_ffn_w_up, ffn_conv_w=v_ffn_conv_w,
               ffn_conv_b=v_ffn_conv_b, ffn_w_down=v_ffn_w_down, ln_gain=v_ln_gain, ln_bias=v_ln_bias)
    bsz, seq, _ = x.shape
    me = 4 * lax.axis_index("x") + 2 * lax.axis_index("y") + lax.axis_index("c")

    ex = _Exchanges(w)
    loss_part, grad_x, g = _local_step(x, loss_target, dict(w), None, bsz, seq, ex=ex)
    loss = lax.psum(loss_part, ("x", "y", "c"))

    d = D_MODEL
    results = {}
    for name in BIG:
        shard = w[name]
        if name == "ffn_w_up":
            parts = [ex.parts[name + "0"], ex.parts[name + "1"]]
            outs4 = _adamw("adamw_" + name, parts, *(jnp.swapaxes(a, 1, 2) for a in (shard, mom[name], var[name])))
            results[name] = tuple(jnp.swapaxes(o, 1, 2) for o in outs4)
            continue
        if name == "ffn_w_down":
            parts, r3 = [ex.parts[name + "0"], ex.parts[name + "1"]], shard.shape
        else:
            parts, r3 = [ex.parts[name]], (1, -1, shard.shape[-1])
        outs4 = _adamw("adamw_" + name, parts, shard.reshape(r3), mom[name].reshape(r3), var[name].reshape(r3))
        results[name] = tuple(o.reshape(shard.shape) for o in outs4)

    small_names = SMALL_REPLICATED + SMALL_SHARDED
    packed = _pack([g[k] for k in small_names])
    pad = (-packed.shape[0]) % (8 * N_DEV)
    if pad:
        packed = jnp.concatenate([packed, jnp.zeros((pad, 128), F32)], axis=0)
    piece = packed.shape[0] // N_DEV
    slots = _exchange("reduce_small", [packed.reshape(N_DEV, piece, 128)], [_sds((N_DEV, piece, 128), F32)],
                      [(0, ("peer",), 0, ("me",))])[0]
    summed = _exchange("gather_small", [_sum_slots(slots)], [_sds((N_DEV, piece, 128), F32)],
                       [(0, (), 0, ("me",))])[0]
    total = _unpack(summed.reshape(N_DEV * piece, 128), [g[k].shape for k in small_names])
    gsum = dict(zip(small_names, total))
    mine = {k: gsum[k] for k in SMALL_REPLICATED}
    mine["s5_b_glu"] = lax.dynamic_slice_in_dim(gsum["s5_b_glu"], me * (d // N_DEV), d // N_DEV, axis=1)
    mine["ffn_conv_w"] = lax.dynamic_index_in_dim(gsum["ffn_conv_w"], me, axis=2, keepdims=False)
    mine["ln_gain"] = lax.dynamic_slice_in_dim(gsum["ln_gain"], me * (d // N_DEV), d // N_DEV, axis=2)
    mine["ln_bias"] = lax.dynamic_slice_in_dim(gsum["ln_bias"], me * (d // N_DEV), d // N_DEV, axis=2)
    gp = _pack([mine[k] for k in small_names])
    outs4 = _adamw("adamw_small", [gp[None]], _pack([w[k] for k in small_names])[None],
                   _pack([mom[k] for k in small_names])[None], _pack([var[k] for k in small_names])[None])
    outs4 = [o[0] for o in outs4]
    shapes = [w[k].shape for k in small_names]
    unpacked = [_unpack(o, shapes) for o in outs4]
    for i, k in enumerate(small_names):
        results[k] = tuple(unpacked[j][i] for j in range(4))

    out = [loss, grad_x]
    for j in range(4):
        out.extend(results[k][j] for k in WEIGHTS)
    return tuple(out)
```

```python
import functools
import math

import numpy as np
import jax
import jax.numpy as jnp
from jax import lax
from jax.experimental import pallas as pl
from jax.experimental.pallas import tpu as pltpu

F32 = jnp.float32
BF16 = jnp.bfloat16
HIGHEST = lax.Precision.HIGHEST

N_DEV = 8
D_MODEL = 1024
SSM_GROUP = 16
SSM_STATE = 64
SSM_CHUNKS = 8
CHUNK_CH = D_MODEL // SSM_CHUNKS
CHUNK_ST = CHUNK_CH // SSM_GROUP * SSM_STATE
N_STATE = D_MODEL // SSM_GROUP * SSM_STATE
HEAD_DIM = 64
N_HEADS = 16
BAND = 128
DILATIONS = (1, 4, 16)
REL_BUCKETS = 32
REL_MAX_DIST = 2048
NEG_BIG = -1e30
DN_ALPHA = (2.0 * 2) ** 0.25
LN_EPS = 1e-5
ADAM_LR = 0.001
ADAM_B1 = 0.9
ADAM_B2 = 0.999
ADAM_EPS = 1e-08
ADAM_WD = 0.01
ADAM_STEP = 10

TM = 512
TM_MM = 2048
TM_MM_F32 = 1024
T_SCAN = 1024
VMEM_LIMIT = 56 * 1024 * 1024
FFN_HIDDEN = BF16

NN = (((1,), (0,)), ((), ()))
NT = (((1,), (1,)), ((), ()))
TN = (((0,), (0,)), ((), ()))


class _Plan:
    def __init__(self, srcs, out_shapes, n_sems, n_local, phases):
        self.srcs, self.out_shapes, self.n_sems, self.n_local, self.phases = srcs, out_shapes, n_sems, n_local, phases


def _call(body, name, grid, in_specs, out_specs, out_shape, scratch=(), comm=None, fill=()):
    params = dict(dimension_semantics=("arbitrary",) * len(grid), vmem_limit_bytes=VMEM_LIMIT)
    if comm is None and not fill:
        return pl.pallas_call(
            body, name=name, grid=grid, in_specs=in_specs, out_specs=out_specs, out_shape=out_shape,
            scratch_shapes=list(scratch), compiler_params=pltpu.CompilerParams(**params))
    single = not isinstance(out_specs, (list, tuple))
    o_specs = [out_specs] if single else list(out_specs)
    o_shape = [out_shape] if single else list(out_shape)
    n_in, n_out, n_scr, n_fill = len(in_specs), len(o_specs), len(scratch), len(fill)
    srcs = list(comm.srcs) if comm else []
    c_shapes = list(comm.out_shapes) if comm else []
    n_cin, n_cout = len(srcs), len(c_shapes)
    total = int(np.prod(grid))

    def wrapped(*refs):
        ins = refs[:n_in]
        refs = refs[n_in + n_fill:]
        cins, outs, couts = refs[:n_cin], refs[n_cin:n_cin + n_out], refs[n_cin + n_out:n_cin + n_out + n_cout]
        rest = refs[n_cin + n_out + n_cout:]
        scr, sems = rest[:n_scr], rest[n_scr:]
        if comm is None:
            body(*ins, *outs, *scr)
            return
        step = pl.program_id(0)
        for ax in range(1, len(grid)):
            step = step * grid[ax] + pl.program_id(ax)
        phases = comm.phases(cins, couts, *sems, total)
        for at, action in phases:
            if at == 0:
                pl.when(step == 0)(action)
        body(*ins, *outs, *scr)
        for at, action in phases:
            if at > 0:
                pl.when(step == at)(action)

    any_spec = pl.BlockSpec(memory_space=pl.ANY)
    sems = [] if comm is None else [pltpu.SemaphoreType.DMA((comm.n_sems,)), pltpu.SemaphoreType.DMA((comm.n_sems,)),
                                    pltpu.SemaphoreType.DMA((max(comm.n_local, 1),))]
    call = pl.pallas_call(
        wrapped, name=name, grid=grid, in_specs=list(in_specs) + [any_spec] * (n_fill + n_cin),
        out_specs=o_specs + [any_spec] * n_cout, out_shape=o_shape + c_shapes,
        scratch_shapes=list(scratch) + sems, input_output_aliases={n_in + j: j for j in range(n_fill)},
        compiler_params=pltpu.CompilerParams(has_side_effects=comm is not None, **params))

    def run(*args):
        res = call(*args, *fill, *srcs)
        own = res[0] if single else res[:n_out]
        return (own, res[n_out:]) if comm is not None else own

    return run


def _sds(shape, dtype):
    return jax.ShapeDtypeStruct(shape, dtype)


def _mm(name, a, b, *, dims, grid, a_spec, b_spec, o_spec, out_shape, nred=1, red_axis=0,
        add=None, add_spec=None, add_scale=1.0, comm=None, norm=None, inner=None, post=None):
    has_add = add is not None
    n_extra = (1 if has_add else 0) + (1 if post else 0) + (2 if norm else 0)
    n_out = 3 if norm else 1
    acc_shape = tuple(s for s in o_spec.block_shape if s is not None)

    def body(*refs):
        a_ref, b_ref = refs[:2]
        extra = refs[2:2 + n_extra]
        outs = refs[2 + n_extra:2 + n_extra + n_out]
        rest = refs[2 + n_extra + n_out:]
        o_ref = outs[0]
        if inner is None:
            prod = lax.dot_general(a_ref[...].astype(BF16), b_ref[...].astype(BF16), dims,
                                   preferred_element_type=F32)
        else:
            nb, a_mode = inner
            width = a_ref.shape[-1] // nb
            prod = None
            for d in range(nb):
                a_d = a_ref[d] if a_mode == "lead" else a_ref[:, d * width:(d + 1) * width]
                part = lax.dot_general(a_d.astype(BF16), b_ref[d].astype(BF16), dims, preferred_element_type=F32)
                prod = part if prod is None else prod + part

        def finish(val):
            if has_add:
                val = val + add_scale * extra[0][...].astype(F32)
            if post:
                val = post[0](val, extra[1 if has_add else 0][...])
            o_ref[...] = val.astype(o_ref.dtype)
            if norm:
                xhat, _ = _ln_stats(val)
                y = xhat * extra[-2][...] + extra[-1][...]
                outs[1][...] = y
                outs[2][...] = y.astype(BF16)

        if nred == 1:
            finish(prod)
        else:
            acc = rest[0]
            k = pl.program_id(red_axis)

            @pl.when(k == 0)
            def _():
                acc[...] = prod

            @pl.when(k > 0)
            def _():
                acc[...] += prod

            @pl.when(k == nred - 1)
            def _():
                finish(acc[...])

    in_specs = [a_spec, b_spec] + ([add_spec] if has_add else [])
    scratch = [pltpu.VMEM(acc_shape, F32)] if nred > 1 else []
    args = (a, b) + ((add,) if has_add else ())
    if post:
        in_specs.append(o_spec)
        args += (post[1],)
    if norm:
        vec = pl.BlockSpec((1, out_shape.shape[1]), lambda *_: (0, 0))
        in_specs += [vec, vec]
        args += tuple(norm)
        o_spec = [o_spec, o_spec, o_spec]
        out_shape = [out_shape, out_shape, _sds(out_shape.shape, BF16)]
    return _call(body, name, grid, in_specs, o_spec, out_shape, scratch, comm=comm)(*args)


def _mm_nn(name, a, b, out_dtype, blocked_out=False, comm=None, blocks=None, residual=None, norm=None):
    TM = TM_MM if out_dtype == BF16 else TM_MM_F32
    m, k = a.shape
    _, _, n = b.shape
    nb = b.shape[0] if blocks is None else blocks[1]
    first = 0 if blocks is None else blocks[0]
    if blocked_out:
        o_spec = pl.BlockSpec((None, TM, n), lambda d, i: (d, i, 0))
        out_shape = _sds((nb, m, n), out_dtype)
    else:
        o_spec = pl.BlockSpec((TM, n), lambda d, i: (i, d))
        out_shape = _sds((m, nb * n), out_dtype)
    return _mm(name, a, b, dims=NN, grid=(nb, m // TM),
               a_spec=pl.BlockSpec((TM, k), lambda d, i: (i, 0)),
               b_spec=pl.BlockSpec((None, k, n), lambda d, i: (d + first, 0, 0)),
               o_spec=o_spec, out_shape=out_shape, comm=comm, norm=norm,
               add=residual, add_spec=o_spec if residual is not None else None, add_scale=DN_ALPHA)


def _mm_nn_red(name, a, b, out_dtype, comm=None, residual=None, norm=None):
    nb, m, k = a.shape
    n = b.shape[2]
    o_spec = pl.BlockSpec((TM, n), lambda i: (i, 0))
    return _mm(name, a, b, dims=NN, grid=(m // TM,), inner=(nb, "lead"),
               a_spec=pl.BlockSpec((nb, TM, k), lambda i: (0, i, 0)),
               b_spec=pl.BlockSpec((nb, k, n), lambda i: (0, 0, 0)),
               o_spec=o_spec, out_shape=_sds((m, n), out_dtype), comm=comm, norm=norm,
               add=residual, add_spec=o_spec if residual is not None else None, add_scale=DN_ALPHA)


def _mm_tn(name, a, g, nb, out_dtype, a_blocked=False, g_mode="cols", comm=None):
    TM = TM_MM
    if a_blocked:
        _, m, ka = a.shape
        a_spec = pl.BlockSpec((None, TM, ka), lambda d, i: (d, i, 0))
    else:
        m, ka = a.shape
        a_spec = pl.BlockSpec((TM, ka), lambda d, i: (i, 0))
    if g_mode == "cols":
        n = g.shape[1] // nb
        g_spec = pl.BlockSpec((TM, n), lambda d, i: (i, d))
    elif g_mode == "blocked":
        n = g.shape[2]
        g_spec = pl.BlockSpec((None, TM, n), lambda d, i: (d, i, 0))
    else:
        n = g.shape[1]
        g_spec = pl.BlockSpec((TM, n), lambda d, i: (i, 0))
    nred = m // TM
    return _mm(name, a, g, dims=TN, grid=(nb, nred), nred=nred, red_axis=1,
               a_spec=a_spec, b_spec=g_spec,
               o_spec=pl.BlockSpec((None, ka, n), lambda d, i: (d, 0, 0)),
               out_shape=_sds((nb, ka, n), out_dtype), comm=comm)


def _mm_nt_red(name, g, w, out_dtype, g_blocked, add=None, add_scale=1.0, comm=None, blocks=None, post=None):
    _, k, n = w.shape
    nb = w.shape[0] if blocks is None else blocks[1]
    first = 0 if blocks is None else blocks[0]
    tm = TM_MM_F32 if nb * k * n <= 4 * 1024 * 768 else TM
    if g_blocked:
        m = g.shape[1]
        g_spec = pl.BlockSpec((nb, tm, n), lambda i: (0, i, 0))
    else:
        m = g.shape[0]
        g_spec = pl.BlockSpec((tm, nb * n), lambda i: (i, 0))
    o_spec = pl.BlockSpec((tm, k), lambda i: (i, 0))
    return _mm(name, g, w, dims=NT, grid=(m // tm,), inner=(nb, "lead" if g_blocked else "cols"),
               a_spec=g_spec, b_spec=pl.BlockSpec((nb, k, n), lambda i: (first // nb, 0, 0)),
               o_spec=o_spec, out_shape=_sds((m, k), out_dtype),
               add=add, add_spec=o_spec if add is not None else None, add_scale=add_scale, comm=comm, post=post)


def _mm_nt_out(name, g, w, out_dtype):
    TM = TM_MM
    m, n = g.shape
    nb, k, _ = w.shape
    return _mm(name, g, w, dims=NT, grid=(nb, m // TM),
               a_spec=pl.BlockSpec((TM, n), lambda d, i: (i, 0)),
               b_spec=pl.BlockSpec((None, k, n), lambda d, i: (d, 0, 0)),
               o_spec=pl.BlockSpec((None, TM, k), lambda d, i: (d, i, 0)),
               out_shape=_sds((nb, m, k), out_dtype))


def _ln_stats(r):
    mu = jnp.mean(r, axis=-1, keepdims=True)
    xc = r - mu
    var = jnp.mean(xc * xc, axis=-1, keepdims=True)
    rstd = lax.rsqrt(var + LN_EPS)
    return xc * rstd, rstd


def _ln_bwd(name, r, dy, gain):
    n, d = r.shape

    def body(r_ref, dy_ref, g_ref, dr_ref, drb_ref, dg_ref, db_ref):
        i = pl.program_id(0)
        xhat, rstd = _ln_stats(r_ref[...])
        dy_v = dy_ref[...]
        dxh = dy_v * g_ref[...]
        m1 = jnp.mean(dxh, axis=-1, keepdims=True)
        m2 = jnp.mean(dxh * xhat, axis=-1, keepdims=True)
        dr = rstd * (dxh - m1 - xhat * m2)
        dr_ref[...] = dr
        drb_ref[...] = dr.astype(BF16)
        dg = jnp.sum(dy_v * xhat, axis=0, keepdims=True)
        db = jnp.sum(dy_v, axis=0, keepdims=True)

        @pl.when(i == 0)
        def _():
            dg_ref[...] = dg
            db_ref[...] = db

        @pl.when(i > 0)
        def _():
            dg_ref[...] += dg
            db_ref[...] += db

    row = pl.BlockSpec((TM, d), lambda i: (i, 0))
    vec = pl.BlockSpec((1, d), lambda i: (0, 0))
    return _call(body, name, (n // TM,), [row, row, vec], [row, row, vec, vec],
                 [_sds((n, d), F32), _sds((n, d), BF16), _sds((1, d), F32), _sds((1, d), F32)])(r, dy, gain)


def _loss_head(y, target):
    n, d = y.shape

    def body(y_ref, t_ref, l_ref, dy_ref):
        i = pl.program_id(0)
        e = y_ref[...] - t_ref[...]
        dy_ref[...] = e * (1.0 / d)
        part = jnp.sum(jnp.sum(e * e, axis=1, keepdims=True), axis=0, keepdims=True) * (0.5 / d)
        part = jnp.broadcast_to(part, (8, 128))

        @pl.when(i == 0)
        def _():
            l_ref[...] = part

        @pl.when(i > 0)
        def _():
            l_ref[...] += part

    row = pl.BlockSpec((TM, d), lambda i: (i, 0))
    return _call(body, "loss_head", (n // TM,), [row, row],
                 [pl.BlockSpec((8, 128), lambda i: (0, 0)), row],
                 [_sds((8, 128), F32), _sds((n, d), F32)])(y, target)


def _group_expand_matrix():
    e = np.zeros((D_MODEL // SSM_GROUP, N_STATE), np.float32)
    for g in range(D_MODEL // SSM_GROUP):
        e[g, g * SSM_STATE:(g + 1) * SSM_STATE] = 1.0
    return jnp.asarray(e)


def _discretise(lr, li, ldt, br, bi, expand):
    dt = jnp.dot(jnp.exp(ldt), expand, precision=HIGHEST, preferred_element_type=F32)
    mag = jnp.exp(lr * dt)
    th = li * dt
    ab_r = mag * jnp.cos(th)
    ab_i = mag * jnp.sin(th)
    den = lr * lr + li * li
    nr = ab_r - 1.0
    co_r = (nr * lr + ab_i * li) / den
    co_i = (ab_i * lr - nr * li) / den
    bb_r = co_r * br - co_i * bi
    bb_i = co_r * bi + co_i * br
    return ab_r, ab_i, bb_r, bb_i


def _cmul(ar, ai, br, bi):
    return ar * br - ai * bi, ar * bi + ai * br


def _s5_prep(lr, li, ldt, br, bi):
    s = N_STATE
    expand = _group_expand_matrix()

    def body(lr_ref, li_ref, ldt_ref, br_ref, bi_ref, e_ref, abr_ref, abi_ref, bbr_ref, bbi_ref, tab_ref):
        ab_r, ab_i, bb_r, bb_i = _discretise(lr_ref[...], li_ref[...], ldt_ref[...], br_ref[...],
                                             bi_ref[...], e_ref[...])
        abr_ref[...] = ab_r
        abi_ref[...] = ab_i
        bbr_ref[...] = bb_r
        bbi_ref[...] = bb_i
        row = lax.broadcasted_iota(jnp.int32, (8, s), 0)
        for base, sign in ((0, 1.0), (8, -1.0)):
            p = [None] * 9
            p[1] = (ab_r, sign * ab_i)
            p[2] = _cmul(*p[1], *p[1])
            p[3] = _cmul(*p[2], *p[1])
            p[4] = _cmul(*p[2], *p[2])
            p[5] = _cmul(*p[4], *p[1])
            p[6] = _cmul(*p[4], *p[2])
            p[7] = _cmul(*p[4], *p[3])
            p[8] = _cmul(*p[4], *p[4])
            for j, k in enumerate((1, 2, 4)):
                keep = (row >= k) if base == 0 else (row < 8 - k)
                tab_ref[base + 2 * j] = jnp.where(keep, jnp.broadcast_to(p[k][0], (8, s)), 0.0)
                tab_ref[base + 2 * j + 1] = jnp.where(keep, jnp.broadcast_to(p[k][1], (8, s)), 0.0)
            pw_r = jnp.zeros((8, s), F32)
            pw_i = jnp.zeros((8, s), F32)
            for i in range(8):
                e = i + 1 if base == 0 else 8 - i
                pw_r = jnp.where(row == i, jnp.broadcast_to(p[e][0], (8, s)), pw_r)
                pw_i = jnp.where(row == i, jnp.broadcast_to(p[e][1], (8, s)), pw_i)
            tab_ref[base + 6] = pw_r
            tab_ref[base + 7] = pw_i

    def full(shape):
        return pl.BlockSpec(shape, lambda i: (0,) * len(shape))

    g = D_MODEL // SSM_GROUP
    return _call(body, "s5_prep", (1,),
                 [full((1, s)), full((1, s)), full((1, g)), full((16, s)), full((16, s)), full((g, s))],
                 [full((1, s)), full((1, s)), full((16, s)), full((16, s)), full((16, 8, s))],
                 [_sds((1, s), F32), _sds((1, s), F32), _sds((16, s), F32), _sds((16, s), F32),
                  _sds((16, 8, s), F32)])(lr, li, ldt, br, bi, expand)


def _s5_prep_bwd(lr, li, ldt, br, bi, d_abr, d_abi, d_bbr, d_bbi):
    s = N_STATE
    g = D_MODEL // SSM_GROUP
    expand = _group_expand_matrix()

    def body(lr_ref, li_ref, ldt_ref, br_ref, bi_ref, e_ref, c1, c2, c3, c4, o1, o2, o3, o4, o5):
        e_val = e_ref[...]
        _, vjp = jax.vjp(lambda a, b, c, d, e: _discretise(a, b, c, d, e, e_val),
                         lr_ref[...], li_ref[...], ldt_ref[...], br_ref[...], bi_ref[...])
        grads = vjp((c1[...], c2[...], c3[...], c4[...]))
        for o, gr in zip((o1, o2, o3, o4, o5), grads):
            o[...] = gr

    def full(shape):
        return pl.BlockSpec(shape, lambda i: (0,) * len(shape))

    return _call(body, "s5_prep_bwd", (1,),
                 [full((1, s)), full((1, s)), full((1, g)), full((16, s)), full((16, s)), full((g, s)),
                  full((1, s)), full((1, s)), full((16, s)), full((16, s))],
                 [full((1, s)), full((1, s)), full((1, g)), full((16, s)), full((16, s))],
                 [_sds((1, s), F32), _sds((1, s), F32), _sds((1, g), F32), _sds((16, s), F32),
                  _sds((16, s), F32)])(lr, li, ldt, br, bi, expand, d_abr, d_abi, d_bbr, d_bbi)


def _scan8(vr, vi, tab_ref, base, reverse):
    for j, k in enumerate((1, 2, 4)):
        mr = tab_ref[base + 2 * j]
        mi = tab_ref[base + 2 * j + 1]
        shift = 8 - k if reverse else k
        sr = pltpu.roll(vr, shift, 0)
        si = pltpu.roll(vi, shift, 0)
        vr, vi = vr + mr * sr - mi * si, vi + mr * si + mi * sr
    return vr, vi


def _s5_scan_fwd(x, bbr, bbi, cr, ci, dvec, tab, seq, comm=None):
    n, d = x.shape
    nt = seq // T_SCAN
    nblk = T_SCAN // 8
    st = CHUNK_ST

    def body(x_ref, bbr_ref, bbi_ref, cr_ref, ci_ref, d_ref, tab_ref, y_ref, hr_ref, hi_ref, car_r, car_i):
        t = pl.program_id(2)

        @pl.when(t == 0)
        def _():
            car_r[...] = jnp.zeros_like(car_r)
            car_i[...] = jnp.zeros_like(car_i)

        u = x_ref[...]
        ub = u.astype(BF16)
        hr_ref[...] = jnp.dot(ub, bbr_ref[...], preferred_element_type=F32)
        hi_ref[...] = jnp.dot(ub, bbi_ref[...], preferred_element_type=F32)

        def step(i, carry):
            sl = pl.ds(pl.multiple_of(i * 8, 8), 8)
            vr, vi = _scan8(hr_ref[sl, :], hi_ref[sl, :], tab_ref, 0, False)
            c_r = jnp.broadcast_to(car_r[...], (8, st))
            c_i = jnp.broadcast_to(car_i[...], (8, st))
            pr = tab_ref[6]
            pi = tab_ref[7]
            vr, vi = vr + pr * c_r - pi * c_i, vi + pr * c_i + pi * c_r
            hr_ref[sl, :] = vr
            hi_ref[sl, :] = vi
            car_r[...] = vr[7:8, :]
            car_i[...] = vi[7:8, :]
            return carry

        lax.fori_loop(0, nblk, step, 0, unroll=4)
        y = lax.dot_general(hr_ref[...].astype(BF16), cr_ref[...], NT, preferred_element_type=F32)
        y = y - lax.dot_general(hi_ref[...].astype(BF16), ci_ref[...], NT, preferred_element_type=F32)
        y_ref[...] = y + d_ref[...] * u

    row = lambda c, b, t: (b * nt + t, c)
    mat = pl.BlockSpec((None, CHUNK_CH, st), lambda c, b, t: (c, 0, 0))
    return _call(
        body, "s5_scan_fwd", (SSM_CHUNKS, n // seq, nt),
        [pl.BlockSpec((T_SCAN, CHUNK_CH), row), mat, mat, mat, mat,
         pl.BlockSpec((1, CHUNK_CH), lambda c, b, t: (0, c)),
         pl.BlockSpec((16, 8, st), lambda c, b, t: (0, 0, c))],
        [pl.BlockSpec((T_SCAN, CHUNK_CH), row), pl.BlockSpec((T_SCAN, st), row), pl.BlockSpec((T_SCAN, st), row)],
        [_sds((n, d), F32), _sds((n, N_STATE), F32), _sds((n, N_STATE), F32)],
        [pltpu.VMEM((1, st), F32), pltpu.VMEM((1, st), F32)], comm=comm,
    )(x, bbr, bbi, cr, ci, dvec, tab)


def _s5_scan_bwd(x, dy, dres, hr, hi, bbr, bbi, cr, ci, dvec, tab, seq, comm=None):
    n, d = x.shape
    nt = seq // T_SCAN
    nblk = T_SCAN // 8
    st = CHUNK_ST

    def body(x_ref, dy_ref, dres_ref, hr_ref, hi_ref, hpr_ref, hpi_ref, bbr_ref, bbi_ref, cr_ref, ci_ref,
             d_ref, tab_ref, dx_ref, dcr_ref, dci_ref, dbr_ref, dbi_ref, dar_ref, dai_ref, dd_ref,
             g_r, g_i, car_r, car_i):
        b = pl.program_id(1)
        tg = pl.program_id(2)
        first = jnp.logical_and(b == 0, tg == 0)

        @pl.when(tg == 0)
        def _():
            car_r[...] = jnp.zeros_like(car_r)
            car_i[...] = jnp.zeros_like(car_i)

        u = x_ref[...]
        dyv = dy_ref[...]
        ub = u.astype(BF16)
        dyb = dyv.astype(BF16)
        g_r[...] = jnp.dot(dyb, cr_ref[...], preferred_element_type=F32)
        g_i[...] = -jnp.dot(dyb, ci_ref[...], preferred_element_type=F32)

        def step(ii, carry):
            i = nblk - 1 - ii
            sl = pl.ds(pl.multiple_of(i * 8, 8), 8)
            vr, vi = _scan8(g_r[sl, :], g_i[sl, :], tab_ref, 8, True)
            c_r = jnp.broadcast_to(car_r[...], (8, st))
            c_i = jnp.broadcast_to(car_i[...], (8, st))
            pr = tab_ref[14]
            pi = tab_ref[15]
            vr, vi = vr + pr * c_r - pi * c_i, vi + pr * c_i + pi * c_r
            g_r[sl, :] = vr
            g_i[sl, :] = vi
            car_r[...] = vr[0:1, :]
            car_i[...] = vi[0:1, :]
            return carry

        lax.fori_loop(0, nblk, step, 0, unroll=4)
        gr = g_r[...]
        gi = g_i[...]
        grb = gr.astype(BF16)
        gib = gi.astype(BF16)
        du = lax.dot_general(grb, bbr_ref[...], NT, preferred_element_type=F32)
        du = du + lax.dot_general(gib, bbi_ref[...], NT, preferred_element_type=F32)
        dx_ref[...] = DN_ALPHA * dres_ref[...] + du + d_ref[...] * dyv

        h_r = hr_ref[...]
        h_i = hi_ref[...]
        t_orig = nt - 1 - tg
        row0 = lax.broadcasted_iota(jnp.int32, (8, st), 0) == 0
        halo_ok = t_orig > 0

        def previous(h, halo_ref):
            top = jnp.broadcast_to(jnp.where(halo_ok, halo_ref[7:8, :], 0.0), (8, st))
            rolled = pltpu.roll(h, 1, 0)
            return jnp.concatenate([jnp.where(row0, top, rolled[:8, :]), rolled[8:, :]], axis=0)

        hp_r = previous(h_r, hpr_ref)
        hp_i = previous(h_i, hpi_ref)
        dar = jnp.sum(gr * hp_r + gi * hp_i, axis=0, keepdims=True)
        dai = jnp.sum(gi * hp_r - gr * hp_i, axis=0, keepdims=True)
        dcr = lax.dot_general(dyb, h_r.astype(BF16), TN, preferred_element_type=F32)
        dci = lax.dot_general(dyb, h_i.astype(BF16), TN, preferred_element_type=F32)
        dbr = lax.dot_general(ub, grb, TN, preferred_element_type=F32)
        dbi = lax.dot_general(ub, gib, TN, preferred_element_type=F32)
        dd = jnp.sum(dyv * u, axis=0, keepdims=True)

        @pl.when(first)
        def _():
            dcr_ref[...] = dcr
            dci_ref[...] = dci
            dbr_ref[...] = dbr
            dbi_ref[...] = dbi
            dar_ref[...] = dar
            dai_ref[...] = dai
            dd_ref[...] = dd

        @pl.when(jnp.logical_not(first))
        def _():
            dcr_ref[...] += dcr
            dci_ref[...] += dci
            dbr_ref[...] += dbr
            dbi_ref[...] += dbi
            dar_ref[...] += dar
            dai_ref[...] += dai
            dd_ref[...] += dd

    row = lambda c, b, t: (b * nt + (nt - 1 - t), c)
    halo = lambda c, b, t: (jnp.maximum((b * nt + (nt - 1 - t)) * (T_SCAN // 8) - 1, 0), c)
    mat = pl.BlockSpec((None, CHUNK_CH, st), lambda c, b, t: (c, 0, 0))
    chan = pl.BlockSpec((T_SCAN, CHUNK_CH), row)
    state = pl.BlockSpec((T_SCAN, st), row)
    return _call(
        body, "s5_scan_bwd", (SSM_CHUNKS, n // seq, nt),
        [chan, chan, chan, state, state, pl.BlockSpec((8, st), halo), pl.BlockSpec((8, st), halo),
         mat, mat, mat, mat, pl.BlockSpec((1, CHUNK_CH), lambda c, b, t: (0, c)),
         pl.BlockSpec((16, 8, st), lambda c, b, t: (0, 0, c))],
        [chan, mat, mat, mat, mat, pl.BlockSpec((1, st), lambda c, b, t: (0, c)),
         pl.BlockSpec((1, st), lambda c, b, t: (0, c)), pl.BlockSpec((1, CHUNK_CH), lambda c, b, t: (0, c))],
        [_sds((n, d), F32)] + [_sds((SSM_CHUNKS, CHUNK_CH, st), F32)] * 4
        + [_sds((1, N_STATE), F32), _sds((1, N_STATE), F32), _sds((1, d), F32)],
        [pltpu.VMEM((T_SCAN, st), F32), pltpu.VMEM((T_SCAN, st), F32), pltpu.VMEM((1, st), F32),
         pltpu.VMEM((1, st), F32)], comm=comm,
    )(x, dy, dres, hr, hi, hr, hi, bbr, bbi, cr, ci, dvec, tab)


_GELU_C = math.sqrt(2.0 / math.pi)


def _gelu_parts(y):
    t = jnp.tanh(_GELU_C * (y + 0.044715 * (y * y * y)))
    return 0.5 * (1.0 + t), t


def _glu_fwd(y, w_glu, b_glu):
    n, d = y.shape

    def body(y_ref, w_ref, b_ref, y2_ref, z_ref, gg_ref):
        yv = y_ref[...]
        cdf, _ = _gelu_parts(yv)
        y2 = yv * cdf
        z = jnp.dot(y2.astype(BF16), w_ref[...], preferred_element_type=F32) + b_ref[...]
        y2_ref[...] = y2.astype(BF16)
        z_ref[...] = z
        gg_ref[...] = (y2 * jax.nn.sigmoid(z)).astype(BF16)

    row = pl.BlockSpec((TM, d), lambda i: (i, 0))
    return _call(body, "glu_fwd", (n // TM,),
                 [row, pl.BlockSpec((d, d), lambda i: (0, 0)), pl.BlockSpec((1, d), lambda i: (0, 0))],
                 [row, row, row], [_sds((n, d), BF16), _sds((n, d), F32), _sds((n, d), BF16)])(y, w_glu, b_glu)


def _glu_bwd_gate(y, z, dgg):
    n, d = y.shape

    def body(y_ref, z_ref, dgg_ref, dz_ref, db_ref, t_ref):
        i = pl.program_id(0)
        yv = y_ref[...]
        cdf, _ = _gelu_parts(yv)
        y2 = yv * cdf
        s = jax.nn.sigmoid(z_ref[...])
        dg = dgg_ref[...]
        dz = dg * y2 * s * (1.0 - s)
        dz_ref[...] = dz.astype(BF16)
        t_ref[...] = dg * s
        db = jnp.sum(dz, axis=0, keepdims=True)

        @pl.when(i == 0)
        def _():
            db_ref[...] = db

        @pl.when(i > 0)
        def _():
            db_ref[...] += db

    row = pl.BlockSpec((TM, d), lambda i: (i, 0))
    vec = pl.BlockSpec((1, d), lambda i: (0, 0))
    return _call(body, "glu_bwd_gate", (n // TM,), [row, row, row], [row, vec, row],
                 [_sds((n, d), BF16), _sds((1, d), F32), _sds((n, d), F32)])(y, z, dgg)


def _gelu_bwd(dy2, y):
    cdf, t = _gelu_parts(y)
    dinner = _GELU_C * (1.0 + 3.0 * 0.044715 * y * y)
    return dy2 * (cdf + 0.5 * y * (1.0 - t * t) * dinner)


HALO = 16


def _shift_down(x, halo, k):
    out = pltpu.roll(x, k, 0)
    row = lax.broadcasted_iota(jnp.int32, (8, x.shape[1]), 0)
    top = jnp.where(row < k, pltpu.roll(halo[HALO - 8:HALO, :], k, 0), out[:8, :])
    return jnp.concatenate([top, out[8:, :]], axis=0)


def _conv3(x, halo, w, b):
    x1 = _shift_down(x, halo, 1)
    x2 = _shift_down(x, halo, 2)
    return b + w[0:1, :] * x + w[1:2, :] * x1 + w[2:3, :] * x2, x1, x2


def _conv_gate_fwd(hc, conv_w, conv_b, seq, comm=None):
    _, nb, n, c = hc.shape
    tiles_per_seq = seq // TM

    def body(x_ref, halo_ref, w_ref, b_ref, o_ref, cv_ref):
        i = pl.program_id(1)
        start = (i % tiles_per_seq) == 0
        cv = []
        for h in range(2):
            halo = jnp.where(start, 0.0, halo_ref[h].astype(F32))
            cv.append(_conv3(x_ref[h].astype(F32), halo, w_ref[h], b_ref[h])[0])
            cv_ref[h] = cv[h].astype(BF16)
        o_ref[...] = (cv[1] * jax.nn.sigmoid(cv[1]) * cv[0]).astype(BF16)

    both = pl.BlockSpec((2, None, TM, c), lambda d, i: (0, d, i, 0))
    return _call(
        body, "conv_gate_fwd", (nb, n // TM),
        [both,
         pl.BlockSpec((2, None, HALO, c), lambda d, i: (0, d, jnp.maximum(i * (TM // HALO) - 1, 0), 0)),
         pl.BlockSpec((2, None, 3, c), lambda d, i: (0, d, 0, 0)),
         pl.BlockSpec((2, None, 1, c), lambda d, i: (0, d, 0, 0))],
        [pl.BlockSpec((None, TM, c), lambda d, i: (d, i, 0)), both],
        [_sds((nb, n, c), BF16), _sds((2, nb, n, c), BF16)], comm=comm)(hc, hc, conv_w, conv_b)


def _conv_gate_bwd(hc, cv, dact, conv_w, seq, comm=None):
    _, nb, n, c = hc.shape
    tiles_per_seq = seq // TM
    last_halo = n // HALO - 1
    ext = TM + HALO

    def body(x_ref, cv_ref, cvn_ref, da_ref, dan_ref, w_ref, dx_ref, dw_ref):
        i = pl.program_id(1)
        end = (i % tiles_per_seq) == tiles_per_seq - 1
        row = lax.broadcasted_iota(jnp.int32, (ext, c), 0)
        cv = [jnp.concatenate([cv_ref[h], cvn_ref[h]], axis=0).astype(F32) for h in range(2)]
        da = jnp.concatenate([da_ref[...], dan_ref[...]], axis=0).astype(F32)
        da = jnp.where(jnp.logical_and(end, row >= TM), 0.0, da)
        sg = jax.nn.sigmoid(cv[1])
        silu = cv[1] * sg
        dcv = [da * silu, da * cv[0] * (sg + silu * (1.0 - sg))]
        for h in range(2):
            w = w_ref[h]
            g = dcv[h]
            g0 = g[:TM, :]
            g1 = pltpu.roll(g, ext - 1, 0)[:TM, :]
            g2 = pltpu.roll(g, ext - 2, 0)[:TM, :]
            dx_ref[h] = (w[0:1, :] * g0 + w[1:2, :] * g1 + w[2:3, :] * g2).astype(BF16)
            x = x_ref[h].astype(F32)
            parts = [jnp.sum(g0 * x, axis=0, keepdims=True),
                     jnp.sum(g1 * x, axis=0, keepdims=True),
                     jnp.sum(g2 * x, axis=0, keepdims=True),
                     jnp.sum(g0, axis=0, keepdims=True)]
            upd = jnp.concatenate(parts + [jnp.zeros((4, c), F32)], axis=0)

            @pl.when(i == 0)
            def _():
                dw_ref[h] = upd

            @pl.when(i > 0)
            def _():
                dw_ref[h] += upd

    nxt = lambda i: jnp.minimum((i + 1) * (TM // HALO), last_halo)
    return _call(
        body, "conv_gate_bwd", (nb, n // TM),
        [pl.BlockSpec((2, None, TM, c), lambda d, i: (0, d, i, 0)),
         pl.BlockSpec((2, None, TM, c), lambda d, i: (0, d, i, 0)),
         pl.BlockSpec((2, None, HALO, c), lambda d, i: (0, d, nxt(i), 0)),
         pl.BlockSpec((None, TM, c), lambda d, i: (d, i, 0)),
         pl.BlockSpec((None, HALO, c), lambda d, i: (d, nxt(i), 0)),
         pl.BlockSpec((2, None, 3, c), lambda d, i: (0, d, 0, 0))],
        [pl.BlockSpec((2, None, TM, c), lambda d, i: (0, d, i, 0)),
         pl.BlockSpec((2, None, 8, c), lambda d, i: (0, d, 0, 0))],
        [_sds((2, nb, n, c), BF16), _sds((2, nb, 8, c), F32)], comm=comm)(hc, cv, cv, dact, dact, conv_w)


def _t5_bucket_np(dist):
    exact = REL_BUCKETS // 2
    d = np.maximum(dist, 1).astype(np.float32)
    large = exact + (np.log(d / exact) / math.log(REL_MAX_DIST / exact) * (REL_BUCKETS - exact)).astype(np.int64)
    large = np.minimum(large, REL_BUCKETS - 1)
    return np.where(dist < exact, dist, large).astype(np.int32)


def _bucket_table(dil):
    steps = np.arange(BAND)[:, None] + BAND - np.arange(2 * BAND)[None, :]
    return _t5_bucket_np(np.maximum(steps, 0) * dil).reshape(1, BAND * 2 * BAND).astype(np.float32)


def _bias_expand(rel_t, dil):
    nk = BAND * 2 * BAND
    bucket = jnp.asarray(_bucket_table(dil))

    def body(r_ref, bk_ref, o_ref):
        ids = lax.broadcasted_iota(jnp.int32, (REL_BUCKETS, nk), 0).astype(F32)
        onehot = (ids == bk_ref[...]).astype(F32)
        o_ref[...] = jnp.dot(r_ref[...], onehot, precision=HIGHEST, preferred_element_type=F32)

    return _call(body, "bias_expand", (1,),
                 [pl.BlockSpec((N_HEADS, REL_BUCKETS), lambda i: (0, 0)), pl.BlockSpec((1, nk), lambda i: (0, 0))],
                 pl.BlockSpec((N_HEADS, nk), lambda i: (0, 0)), _sds((N_HEADS, nk), F32))(rel_t, bucket)


def _bias_reduce(dbias, dil):
    nk = BAND * 2 * BAND
    bucket = jnp.asarray(_bucket_table(dil))

    def body(g_ref, bk_ref, o_ref):
        ids = lax.broadcasted_iota(jnp.int32, (REL_BUCKETS, nk), 0).astype(F32)
        onehot = (ids == bk_ref[...]).astype(F32)
        o_ref[...] = lax.dot_general(g_ref[...], onehot, NT, precision=HIGHEST, preferred_element_type=F32)

    return _call(body, "bias_reduce", (1,),
                 [pl.BlockSpec((N_HEADS, nk), lambda i: (0, 0)), pl.BlockSpec((1, nk), lambda i: (0, 0))],
                 pl.BlockSpec((N_HEADS, REL_BUCKETS), lambda i: (0, 0)),
                 _sds((N_HEADS, REL_BUCKETS), F32))(dbias, bucket)


def _valid_mask(n):
    qi = lax.broadcasted_iota(jnp.int32, (BAND, 2 * BAND), 0)
    kj = lax.broadcasted_iota(jnp.int32, (BAND, 2 * BAND), 1)
    steps = qi + BAND - kj
    in_band = jnp.logical_and(steps >= 0, steps <= BAND)
    return jnp.logical_and(in_band, jnp.logical_or(n > 0, kj >= BAND))


ATTN_COLS = {1: 1024, 4: 512, 16: 128}


def _residue(dil, r):
    return slice(None) if dil == 1 else pl.ds(r, BAND, stride=dil)


def _stack_heads(x, first):
    return jnp.concatenate([jnp.where(first, x, 0.0), jnp.where(first, 0.0, x)], axis=0)


def _attn_fwd(q, k, v, bias, g, dil, bsz, seq, comm=None, merge=()):
    n = q.shape[0]
    rows = BAND * dil
    nch = seq // rows
    w = N_HEADS * HEAD_DIM
    wc = ATTN_COLS[dil]
    ncol = w // wc
    hpc = wc // HEAD_DIM
    nj = wc // 128
    slab = lambda k: pltpu.VMEM((k * nj, rows, 128), F32)

    def attend(q_ref, kc_ref, kp_ref, vc_ref, vp_ref, b_ref, others, o_ref, l_ref, qf, kf, vf, of, lf):
        ch = pl.program_id(2)
        valid = _valid_mask(ch)
        valid = jnp.concatenate([valid, valid], axis=0)
        lane = lax.broadcasted_iota(jnp.int32, (BAND, 128), 1)
        first = lane < HEAD_DIM
        for j in range(nj):
            sl = slice(128 * j, 128 * (j + 1))
            qf[j] = q_ref[:, sl].astype(F32)
            kf[j] = kp_ref[:, sl].astype(F32)
            kf[nj + j] = kc_ref[:, sl].astype(F32)
            vf[j] = vp_ref[:, sl].astype(F32)
            vf[nj + j] = vc_ref[:, sl].astype(F32)
        for r in range(dil):
            rr = _residue(dil, r)
            for j in range(nj):
                q2 = qf.at[j][rr, :]
                k2 = jnp.concatenate([kf.at[j][rr, :], kf.at[nj + j][rr, :]], axis=0).astype(BF16)
                v2 = jnp.concatenate([vf.at[j][rr, :], vf.at[nj + j][rr, :]], axis=0).astype(BF16)
                qs = _stack_heads(q2, first).astype(BF16)
                s = lax.dot_general(qs, k2, NT, preferred_element_type=F32) * (HEAD_DIM ** -0.5)
                s = jnp.where(valid, s + b_ref[2 * j:2 * j + 2].reshape(2 * BAND, 2 * BAND), NEG_BIG)
                m = jnp.max(s, axis=1, keepdims=True)
                p = jnp.exp(s - m)
                l = jnp.sum(p, axis=1, keepdims=True)
                pv = jnp.dot(p.astype(BF16), v2, preferred_element_type=F32) / l
                lse = jnp.broadcast_to(m + jnp.log(l), (2 * BAND, 128))
                of.at[j][rr, :] = jnp.where(first, pv[:BAND], pv[BAND:])
                lf.at[j][rr, :] = jnp.where(first, lse[:BAND], lse[BAND:])
        for j in range(nj):
            sl = slice(128 * j, 128 * (j + 1))
            if merge:
                o0, l0, o1, l1 = (r[:, sl] for r in others)
                lc = lf[j]
                m = jnp.maximum(jnp.maximum(l0, l1), lc)
                wa, wb, wc_ = jnp.exp(l0 - m), jnp.exp(l1 - m), jnp.exp(lc - m)
                tot = wa + wb + wc_
                o_ref[:, sl] = (wa * o0 + wb * o1 + wc_ * of[j]) / tot
                l_ref[:, sl] = m + jnp.log(tot)
            else:
                o_ref[:, sl] = of[j]
                l_ref[:, sl] = lf[j]

    def body(q_ref, kc_ref, kp_ref, vc_ref, vp_ref, b_ref, *rest):
        others, rest = rest[:len(merge)], rest[len(merge):]
        attend(q_ref, kc_ref, kp_ref, vc_ref, vp_ref, b_ref, others, *rest)

    base = g * ncol
    cur = lambda c, b, ch: (b * nch + ch, base + c)
    prev = lambda c, b, ch: (b * nch + jnp.maximum(ch - 1, 0), base + c)
    blk = lambda im: pl.BlockSpec((rows, wc), im)
    out_blk = pl.BlockSpec((rows, wc), lambda c, b, ch: (b * nch + ch, c))
    return _call(
        body, "attn_fwd_d%d" % dil, (ncol, bsz, nch),
        [blk(cur), blk(cur), blk(prev), blk(cur), blk(prev),
         pl.BlockSpec((hpc, BAND, 2 * BAND), lambda c, b, ch: (c, 0, 0))] + [out_blk] * len(merge),
        [out_blk, out_blk],
        [_sds((n, w), F32)] * 2,
        [slab(1), slab(2), slab(2), slab(1), slab(1)], comm=comm,
    )(q, k, k, v, v, bias, *merge)


def _attn_bwd(q, k, v, do, o, lse, bias, g, dil, bsz, seq, comm=None, into=()):
    n = q.shape[0]
    rows = BAND * dil
    nch = seq // rows
    w = N_HEADS * HEAD_DIM
    wc = ATTN_COLS[dil]
    ncol = w // wc
    hpc = wc // HEAD_DIM
    nj = wc // 128
    slab = lambda k: pltpu.VMEM((k * nj, rows, 128), F32)
    scale = HEAD_DIM ** -0.5

    def body(q_ref, kc_ref, kp_ref, vc_ref, vp_ref, do_ref, o_ref, l_ref, b_ref,
             dq_ref, dk_ref, dv_ref, db_ref, qf, kf, vf, dof, of, lf, dqf, dkf, dvf):
        b = pl.program_id(1)
        ch = pl.program_id(2)
        cur = ch % 2
        prv = 1 - cur

        @pl.when(jnp.logical_and(b == 0, ch == 0))
        def _():
            db_ref[...] = jnp.zeros_like(db_ref)

        @pl.when(ch == 0)
        def _():
            for j in range(nj):
                dkf[nj + j] = jnp.zeros((rows, 128), F32)
                dvf[nj + j] = jnp.zeros((rows, 128), F32)

        @pl.when(ch < nch)
        def _():
            valid = _valid_mask(ch)
            valid = jnp.concatenate([valid, valid], axis=0)
            first = lax.broadcasted_iota(jnp.int32, (BAND, 128), 1) < HEAD_DIM
            for j in range(nj):
                sl = slice(128 * j, 128 * (j + 1))
                qf[j] = q_ref[:, sl].astype(F32)
                kf[j] = kp_ref[:, sl].astype(F32)
                kf[nj + j] = kc_ref[:, sl].astype(F32)
                vf[j] = vp_ref[:, sl].astype(F32)
                vf[nj + j] = vc_ref[:, sl].astype(F32)
                dof[j] = do_ref[:, sl]
                of[j] = o_ref[:, sl]
                lf[j] = l_ref[:, sl]
            for r in range(dil):
                rr = _residue(dil, r)
                for j in range(nj):
                    k2 = jnp.concatenate([kf.at[j][rr, :], kf.at[nj + j][rr, :]], axis=0).astype(BF16)
                    v2 = jnp.concatenate([vf.at[j][rr, :], vf.at[nj + j][rr, :]], axis=0).astype(BF16)
                    do2 = dof.at[j][rr, :]
                    lse2 = lf.at[j][rr, :]
                    qs = _stack_heads(qf.at[j][rr, :], first).astype(BF16)
                    dos = _stack_heads(do2, first)
                    delta = jnp.sum(dos * jnp.concatenate([of.at[j][rr, :]] * 2, axis=0), axis=1, keepdims=True)
                    dos = dos.astype(BF16)
                    lse_col = jnp.concatenate([lse2[:, 0:1], lse2[:, HEAD_DIM:HEAD_DIM + 1]], axis=0)
                    s = lax.dot_general(qs, k2, NT, preferred_element_type=F32) * scale
                    s = jnp.where(valid, s + b_ref[2 * j:2 * j + 2].reshape(2 * BAND, 2 * BAND), NEG_BIG)
                    p = jnp.exp(s - lse_col)
                    dp = lax.dot_general(dos, v2, NT, preferred_element_type=F32)
                    ds = p * (dp - delta)
                    db_ref[2 * j:2 * j + 2] += ds.reshape(2, BAND, 2 * BAND)
                    dsb = ds.astype(BF16)
                    dq2 = jnp.dot(dsb, k2, preferred_element_type=F32) * scale
                    dk2 = lax.dot_general(dsb, qs, TN, preferred_element_type=F32) * scale
                    dv2 = lax.dot_general(p.astype(BF16), dos, TN, preferred_element_type=F32)
                    dqf.at[j][rr, :] = jnp.where(first, dq2[:BAND], dq2[BAND:])
                    dkf.at[prv * nj + j][rr, :] += dk2[:BAND, :]
                    dvf.at[prv * nj + j][rr, :] += dv2[:BAND, :]
                    dkf.at[cur * nj + j][rr, :] = dk2[BAND:, :]
                    dvf.at[cur * nj + j][rr, :] = dv2[BAND:, :]
            for j in range(nj):
                dq_ref[:, 128 * j:128 * (j + 1)] = dqf[j].astype(BF16)

        for j in range(nj):
            sl = slice(128 * j, 128 * (j + 1))
            dk_ref[:, sl] = dkf[prv * nj + j].astype(BF16)
            dv_ref[:, sl] = dvf[prv * nj + j].astype(BF16)

    last = nch - 1
    base = g * ncol
    cur3 = lambda c, b, ch: (b * nch + jnp.minimum(ch, last), base + c)
    prev3 = lambda c, b, ch: (b * nch + jnp.maximum(jnp.minimum(ch, last) - 1, 0), base + c)
    cur1 = lambda c, b, ch: (b * nch + jnp.minimum(ch, last), c)
    lag3 = lambda c, b, ch: (b * nch + jnp.maximum(ch - 1, 0), base + c)
    blk = lambda im: pl.BlockSpec((rows, wc), im)
    bias_blk = pl.BlockSpec((hpc, BAND, 2 * BAND), lambda c, b, ch: (c, 0, 0))
    out = _sds(q.shape, BF16)
    return _call(
        body, "attn_bwd_d%d" % dil, (ncol, bsz, nch + 1),
        [blk(cur3), blk(cur3), blk(prev3), blk(cur3), blk(prev3), blk(cur1), blk(cur1), blk(cur1), bias_blk],
        [blk(cur3), blk(lag3), blk(lag3), bias_blk],
        [out, out, out, _sds((N_HEADS, BAND, 2 * BAND), F32)],
        [slab(1), slab(2), slab(2), slab(1), slab(1), slab(1), slab(1), slab(2), slab(2)], comm=comm, fill=into,
    )(q, k, k, v, v, do, o, lse, bias)


def _adamw(name, parts, w, m, v):
    nl, r, c = w.shape
    nparts = parts[0].shape[0]
    tr = r
    for cand in (256, 352, 128):
        if r % cand == 0 and r > cand:
            tr = cand
            break
    c1 = 1.0 - ADAM_B1 ** ADAM_STEP
    c2 = 1.0 - ADAM_B2 ** ADAM_STEP

    def body(*refs):
        p_refs = refs[:nl]
        w_ref, m_ref, v_ref, g_ref, d_ref, nm_ref, nv_ref = refs[nl:]
        layer = pl.program_id(0)
        for l in range(nl):
            @pl.when(layer == l)
            def _():
                g = p_refs[l][0].astype(F32)
                for dev in range(1, nparts):
                    g = g + p_refs[l][dev].astype(F32)
                nm = ADAM_B1 * m_ref[...] + (1.0 - ADAM_B1) * g
                nv = ADAM_B2 * v_ref[...] + (1.0 - ADAM_B2) * (g * g)
                m_hat = nm / c1
                v_hat = nv / c2
                g_ref[...] = g
                d_ref[...] = -ADAM_LR * (m_hat / (jnp.sqrt(v_hat) + ADAM_EPS) + ADAM_WD * w_ref[...])
                nm_ref[...] = nm
                nv_ref[...] = nv

    def part_spec(l):
        return pl.BlockSpec((nparts, tr, c), lambda layer, i: (0, jnp.where(layer == l, i, 0), 0))

    row = pl.BlockSpec((None, tr, c), lambda layer, i: (layer, i, 0))
    return _call(body, name, (nl, r // tr), [part_spec(l) for l in range(nl)] + [row, row, row],
                 [row] * 4, [_sds((nl, r, c), F32)] * 4)(*parts, w, m, v)


def _exchange(name, srcs, out_shapes, items):
    n_in = len(srcs)
    n_out = len(out_shapes)
    n_items = len(items)

    def body(*refs):
        ins = refs[:n_in]
        outs = refs[n_in:n_in + n_out]
        send_sems, recv_sems, local_sems = refs[n_in + n_out:]
        x, y, c = lax.axis_index("x"), lax.axis_index("y"), lax.axis_index("c")
        me = 4 * x + 2 * y + c

        def pick(ref, sel, peer):
            idx = tuple(peer if s == "peer" else me if s == "me" else s for s in sel)
            return ref.at[idx] if idx else ref

        copies = []
        for it, (si, ssel, oi, osel) in enumerate(items):
            local = pltpu.make_async_copy(pick(ins[si], ssel, me), pick(outs[oi], osel, me), local_sems.at[it])
            local.start()
            copies.append(local)
            for j in range(1, N_DEV):
                px = 1 - x if j & 4 else x
                py = 1 - y if j & 2 else y
                pc = 1 - c if j & 1 else c
                peer = 4 * px + 2 * py + pc
                sem = it * (N_DEV - 1) + j - 1
                cp = pltpu.make_async_remote_copy(
                    src_ref=pick(ins[si], ssel, peer), dst_ref=pick(outs[oi], osel, peer),
                    send_sem=send_sems.at[sem], recv_sem=recv_sems.at[sem],
                    device_id=(px, py, pc), device_id_type=pl.DeviceIdType.MESH)
                cp.start()
                copies.append(cp)
        for cp in copies:
            cp.wait()

    any_spec = pl.BlockSpec(memory_space=pl.ANY)
    return pl.pallas_call(
        body, name=name, in_specs=[any_spec] * n_in, out_specs=[any_spec] * n_out, out_shape=list(out_shapes),
        scratch_shapes=[pltpu.SemaphoreType.DMA((n_items * (N_DEV - 1),)),
                        pltpu.SemaphoreType.DMA((n_items * (N_DEV - 1),)),
                        pltpu.SemaphoreType.DMA((n_items,))],
        compiler_params=pltpu.CompilerParams(has_side_effects=True),
    )(*srcs)


def _sel(ref, sel, slot=None):
    idx = tuple(slot if s == "slot" else s for s in sel)
    return ref.at[idx] if idx else ref


def _gather_plan(srcs, out_shapes, items):
    per = N_DEV - 1
    n_items = len(items)

    def phases(ins, outs, send_sems, recv_sems, local_sems, total):
        x, y, c = lax.axis_index("x"), lax.axis_index("y"), lax.axis_index("c")
        sibling = (x, y, 1 - c)
        chips = [(1 - x, y), (x, 1 - y), (1 - x, 1 - y)]
        slot_of = lambda px, py, pc: 4 * px + 2 * py + pc

        def copy(it, k, block, to, src=None):
            _, _, oi, osel = items[it]
            dst = _sel(outs[oi], osel, slot_of(*block))
            return pltpu.make_async_remote_copy(
                src_ref=dst if src is None else src, dst_ref=dst,
                send_sem=send_sems.at[it * per + k], recv_sem=recv_sems.at[it * per + k],
                device_id=to, device_id_type=pl.DeviceIdType.MESH)

        def own_copies(it):
            si, ssel, oi, osel = items[it]
            src = _sel(ins[si], ssel)
            mine = pltpu.make_async_copy(src, _sel(outs[oi], osel, slot_of(x, y, c)), local_sems.at[it])
            first = [copy(it, 0, (x, y, c), sibling, src=src)]
            first += [copy(it, 1 + j, (x, y, c), (*chip, c), src=src) for j, chip in enumerate(chips)]
            return mine, first

        def start():
            for it in range(n_items):
                mine, first = own_copies(it)
                mine.start()
                for cp in first:
                    cp.start()

        def forward(it):
            def go():
                for j, chip in enumerate(chips):
                    copy(it, 1 + j, (*chip, c), (x, y, c)).wait_recv()
                    copy(it, 4 + j, (*chip, c), sibling).start()
            return go

        def finish():
            for it in range(n_items):
                copy(it, 0, sibling, (x, y, c)).wait_recv()
                for j, chip in enumerate(chips):
                    copy(it, 4 + j, (*chip, 1 - c), (x, y, c)).wait_recv()
            for it in range(n_items):
                mine, first = own_copies(it)
                for cp in first:
                    cp.wait_send()
                for j, chip in enumerate(chips):
                    copy(it, 4 + j, (*chip, c), sibling).wait_send()
                mine.wait()

        lo = max(total // 3, 1)
        acts = [(0, start)]
        for it in range(n_items):
            acts.append((min(lo + it * (total - 1 - lo) // n_items, total - 2), forward(it)))
        acts.append((total - 1, finish))
        return acts

    return _Plan(srcs, out_shapes, n_items * per, n_items, phases)


def _scatter_plan(srcs, out_shapes, items):
    per = N_DEV - 1
    n_items = len(items)

    def phases(ins, outs, send_sems, recv_sems, local_sems, total):
        x, y, c = lax.axis_index("x"), lax.axis_index("y"), lax.axis_index("c")
        me = 4 * x + 2 * y + c

        def copies():
            out = []
            for it, (si, oi, osel) in enumerate(items):
                dst = _sel(outs[oi], osel, me)
                out.append(pltpu.make_async_copy(ins[si].at[me], dst, local_sems.at[it]))
                for j in range(1, N_DEV):
                    px = 1 - x if j & 4 else x
                    py = 1 - y if j & 2 else y
                    pc = 1 - c if j & 1 else c
                    out.append(pltpu.make_async_remote_copy(
                        src_ref=ins[si].at[4 * px + 2 * py + pc], dst_ref=dst,
                        send_sem=send_sems.at[it * per + j - 1], recv_sem=recv_sems.at[it * per + j - 1],
                        device_id=(px, py, pc), device_id_type=pl.DeviceIdType.MESH))
            return out

        def start():
            for cp in copies():
                cp.start()

        def finish():
            for cp in copies():
                cp.wait()

        return [(0, start), (total - 1, finish)]

    return _Plan(srcs, out_shapes, n_items * per, n_items, phases)


def _sum_slots(parts):
    _, r, c = parts.shape

    def body(p_ref, o_ref):
        acc = p_ref[0]
        for dev in range(1, N_DEV):
            acc = acc + p_ref[dev]
        o_ref[...] = acc

    return _call(body, "sum_slots", (1,), [pl.BlockSpec((N_DEV, r, c), lambda i: (0, 0, 0))],
                 pl.BlockSpec((r, c), lambda i: (0, 0)), _sds((r, c), F32))(parts)


def _pack(arrays):
    rows = []
    for a in arrays:
        flat = a.reshape(-1).astype(F32)
        pad = (-flat.shape[0]) % 1024
        if pad:
            flat = jnp.concatenate([flat, jnp.zeros((pad,), F32)])
        rows.append(flat.reshape(-1, 128))
    return jnp.concatenate(rows, axis=0)


def _unpack(buf, shapes):
    out = []
    r0 = 0
    for shp in shapes:
        size = int(np.prod(shp))
        nrows = -(-size // 1024) * 8
        out.append(buf[r0:r0 + nrows].reshape(-1)[:size].reshape(shp))
        r0 += nrows
    return out


def _block_diag_chunks(val):
    eye = jnp.eye(8, dtype=val.dtype)
    return jnp.einsum("cjhp,jk->cjhkp", val, eye).reshape(SSM_CHUNKS, CHUNK_CH, CHUNK_ST)


def _diag_blocks(dense):
    d5 = dense.reshape(SSM_CHUNKS, 8, SSM_GROUP, 8, SSM_STATE)
    return jnp.stack([d5[:, j, :, j, :] for j in range(8)], axis=1)


def _carried(res, comm):
    return res if comm is not None else (res, None)


def _ffn_fwd(tag, hin, hin_b, w_up, conv_w, conv_b, w_down4, gain, bias, seq, comm_up=None, comm_conv=None,
             comm_down=None):
    hc, got_up = _carried(_mm_nn("ffn_up" + tag, hin_b, w_up, FFN_HIDDEN, blocked_out=True, comm=comm_up), comm_up)
    nb, n, c = hc.shape
    hc = hc.reshape(2, nb // 2, n, c)
    (act, cv), got_conv = _carried(_conv_gate_fwd(hc, conv_w, conv_b, seq, comm=comm_conv), comm_conv)
    (r, hout, hout_b), got_down = _carried(
        _mm_nn_red("ffn_down" + tag, act, w_down4, F32, comm=comm_down, residual=hin, norm=(gain, bias)), comm_down)
    return hout, hout_b, (hin_b, hc, cv, act, r), (got_up, got_conv, got_down)


def _ffn_bwd(tag, saved, dh, w_up, conv_w, conv_b, w_down4, gain, seq, comm_conv=None, comm_dw=None,
             comm_dx=None):
    hin_b, hc, cv, act, r = saved
    dr, dr_b, dgain, dbias = _ln_bwd("ffn_norm_bwd" + tag, r, dh, gain)
    d_wdown = _mm_tn("ffn_down_dw" + tag, act, dr_b, act.shape[0], BF16, a_blocked=True, g_mode="shared")
    dact = _mm_nt_out("ffn_down_dx" + tag, dr_b, w_down4, FFN_HIDDEN)
    (dhc, dconv), got_conv = _carried(_conv_gate_bwd(hc, cv, dact, conv_w, seq, comm=comm_conv), comm_conv)
    dhc8 = dhc.reshape(2 * dhc.shape[1], dhc.shape[2], dhc.shape[3])
    d_wup, got_dw = _carried(_mm_tn("ffn_up_dw" + tag, dhc8, hin_b, dhc8.shape[0], BF16, a_blocked=True,
                                    g_mode="shared", comm=comm_dw), comm_dw)
    dhin, got_dx = _carried(_mm_nt_red("ffn_up_dx" + tag, dhc8, w_up, F32, g_blocked=True, add=dr,
                                       add_scale=DN_ALPHA, comm=comm_dx), comm_dx)
    return dhin, d_wup, d_wdown, dconv, dgain, dbias, (got_conv, got_dw, got_dx)


def _local_step(x, target, p, wfull, bsz, seq, ex=None):
    n = bsz * seq
    xf = x.reshape(n, D_MODEL)
    tf = target.reshape(n, D_MODEL)

    lr = p["s5_lam_re"].reshape(1, N_STATE)
    li = p["s5_lam_im"].reshape(1, N_STATE)
    ldt = p["s5_log_dt"].reshape(1, D_MODEL // SSM_GROUP)
    br_t = p["s5_b_re"].reshape(N_STATE, SSM_GROUP).T
    bi_t = p["s5_b_im"].reshape(N_STATE, SSM_GROUP).T
    ab_r, ab_i, bb_r, bb_i, tab = _s5_prep(lr, li, ldt, br_t, bi_t)

    def bb_chunks(bb):
        return _block_diag_chunks(bb.reshape(SSM_GROUP, SSM_CHUNKS, 8, SSM_STATE).transpose(1, 2, 0, 3))

    def c_chunks(cc):
        return _block_diag_chunks(cc.reshape(SSM_CHUNKS, 8, SSM_GROUP, SSM_STATE))

    bbr_c, bbi_c = bb_chunks(bb_r).astype(BF16), bb_chunks(bb_i).astype(BF16)
    cr_c, ci_c = c_chunks(p["s5_c_re"]).astype(BF16), c_chunks(p["s5_c_im"]).astype(BF16)
    dvec = p["s5_d"].reshape(1, D_MODEL)

    if ex is None:
        y, h_r, h_i = _s5_scan_fwd(xf, bbr_c, bbi_c, cr_c, ci_c, dvec, tab, seq)
    else:
        (y, h_r, h_i), gathered = _s5_scan_fwd(xf, bbr_c, bbi_c, cr_c, ci_c, dvec, tab, seq,
                                               comm=ex.gather_first())
        p, wfull = ex.take_first(gathered, p)
    ln_g = p["ln_gain"].reshape(4, 1, D_MODEL)
    ln_b = p["ln_bias"].reshape(4, 1, D_MODEL)
    y2, z, gg = _glu_fwd(y, wfull["s5_w_glu"], p["s5_b_glu"].reshape(1, D_MODEL))
    r1, h1, h1b = _mm_nn("s5_out", gg, wfull["s5_w_out"][None], F32, residual=xf, norm=(ln_g[0], ln_b[0]))
    plans = [ex.gather_attn(k) if ex else None for k in ("kv0", "q", "kv1")]
    h2, h2b, ffn0_saved, got = _ffn_fwd(
        "0", h1, h1b, wfull["ffn_w_up"][0], p["conv_w"][0], p["conv_b"][0], wfull["ffn_w_down"][0],
        ln_g[1], ln_b[1], seq, comm_up=plans[0], comm_conv=plans[1], comm_down=plans[2])
    if ex is not None:
        wfull = ex.take_attn({"kv0": got[0][0], "q": got[1][0], "kv1": got[2][0]}, wfull)

    plan = ex.gather_layer1("up", 0) if ex else None
    kmat, got = _carried(_mm_nn("attn_k", h2b, wfull["attn_w_kv"], BF16, blocks=(0, 4), comm=plan), plan)
    halves = {"up": [got[0]], "down": []} if ex else None
    vmat = _mm_nn("attn_v", h2b, wfull["attn_w_kv"], BF16, blocks=(4, 4))
    plan = ex.gather_attn("out") if ex else None
    qmat, got = _carried(_mm_nn("attn_q", h2b, wfull["attn_w_q"], BF16, comm=plan), plan)
    if ex is not None:
        wfull = ex.take_attn({"out": got[0]}, wfull)
    biases, outs, lses = [], [], []
    for g, dil in enumerate(DILATIONS):
        rel_t = p["rel_bias"][:, g * N_HEADS:(g + 1) * N_HEADS].T
        biases.append(_bias_expand(rel_t, dil).reshape(N_HEADS, BAND, 2 * BAND))
        which, half = {1: ("down", 0), 4: ("down", 1), 16: ("up", 1)}[dil]
        plan = ex.gather_layer1(which, half) if ex else None
        merge = (outs[0], lses[0], outs[1], lses[1]) if dil == DILATIONS[-1] else ()
        (o_g, l_g), got = _carried(_attn_fwd(qmat, kmat, vmat, biases[g], g, dil, bsz, seq, comm=plan,
                                             merge=merge), plan)
        if plan is not None:
            halves[which].append(got[0])
        outs.append(o_g)
        lses.append(l_g)
    if ex is not None:
        wfull = ex.take_layer1(halves, wfull)
    o, lse = outs[-1], lses[-1]
    r3, h3, h3b = _mm_nn("attn_out", o, wfull["attn_w_out"][None], F32, residual=h2, norm=(ln_g[2], ln_b[2]))
    h4, _, ffn1_saved, _ = _ffn_fwd("1", h3, h3b, wfull["ffn_w_up"][1], p["conv_w"][1], p["conv_b"][1],
                                    wfull["ffn_w_down"][1], ln_g[3], ln_b[3], seq)
    loss_tile, dh4 = _loss_head(h4, tf)

    grads = {}
    dh3, grads["ffn_w_up1"], grads["ffn_w_down1"], dconv1, dg3, db3, _ = _ffn_bwd(
        "1", ffn1_saved, dh4, wfull["ffn_w_up"][1], p["conv_w"][1], p["conv_b"][1], wfull["ffn_w_down"][1],
        ln_g[3], seq)
    dr3, dr3b, dg2, db2 = _ln_bwd("attn_norm_bwd", r3, dh3, ln_g[2])
    grads["attn_w_out"] = _mm_tn("attn_out_dw", o, dr3b, 1, BF16, g_mode="shared")
    do = _mm_nt_red("attn_out_dx", dr3b, wfull["attn_w_out"][None], F32, g_blocked=False)
    dqkv, drel = (), []
    for g, dil in enumerate(DILATIONS):
        plan = ex.scatter(grads, ("ffn_w_up1", "ffn_w_down1")) if ex and dil == DILATIONS[-1] else None
        res, got = _carried(_attn_bwd(qmat, kmat, vmat, do, o, lse, biases[g], g, dil, bsz, seq, comm=plan,
                                      into=dqkv), plan)
        if plan is not None:
            ex.keep(plan, got)
        dqkv, db_g = tuple(res[:3]), res[3]
        drel.append(_bias_reduce(db_g.reshape(N_HEADS, BAND * 2 * BAND), dil).T)
    dq, dk, dv = dqkv
    grads["rel_bias"] = jnp.concatenate(drel, axis=1)
    grads["attn_w_q"] = _mm_tn("attn_q_dw", h2b, dq, N_DEV, BF16)
    grads["attn_w_kv"] = jnp.concatenate([_mm_tn("attn_k_dw", h2b, dk, 4, BF16),
                                          _mm_tn("attn_v_dw", h2b, dv, 4, BF16)], axis=0)
    dh2 = _mm_nt_red("attn_q_dx", dq, wfull["attn_w_q"], F32, g_blocked=False, add=dr3, add_scale=DN_ALPHA)
    dh2 = _mm_nt_red("attn_k_dx", dk, wfull["attn_w_kv"], F32, g_blocked=False, add=dh2, blocks=(0, 4))
    dh2 = _mm_nt_red("attn_v_dx", dv, wfull["attn_w_kv"], F32, g_blocked=False, add=dh2, blocks=(4, 4))

    plans = [ex.scatter(grads, names) if ex else None
             for names in (("attn_w_kv",), ("attn_w_out",), ("attn_w_q",))]
    dh1, grads["ffn_w_up0"], grads["ffn_w_down0"], dconv0, dg1, db1, got = _ffn_bwd(
        "0", ffn0_saved, dh2, wfull["ffn_w_up"][0], p["conv_w"][0], p["conv_b"][0], wfull["ffn_w_down"][0],
        ln_g[1], seq, comm_conv=plans[0], comm_dw=plans[1], comm_dx=plans[2])
    if ex is not None:
        for plan, outs in zip(plans, got):
            ex.keep(plan, outs)
    dr1, dr1b, dg0, db0 = _ln_bwd("s5_norm_bwd", r1, dh1, ln_g[0])
    grads["s5_w_out"] = _mm_tn("s5_out_dw", gg, dr1b, 1, BF16, g_mode="shared")
    dgg = _mm_nt_red("s5_out_dx", dr1b, wfull["s5_w_out"][None], F32, g_blocked=False)
    dz, d_bglu, dy2_direct = _glu_bwd_gate(y, z, dgg)
    grads["s5_w_glu"] = _mm_tn("s5_glu_dw", y2, dz, 1, BF16, g_mode="shared")
    dy = _mm_nt_red("s5_glu_dx", dz, wfull["s5_w_glu"][None], F32, g_blocked=False, add=dy2_direct,
                    post=(_gelu_bwd, y))
    plan_last = ex.scatter(grads, ("ffn_w_up0", "ffn_w_down0", "s5_w_out", "s5_w_glu")) if ex else None
    res = _s5_scan_bwd(xf, dy, dr1, h_r, h_i, bbr_c, bbi_c, cr_c, ci_c, dvec, tab, seq, comm=plan_last)
    if ex is not None:
        res, got = res
        ex.keep(plan_last, got)
    dx, dcr, dci, dbr, dbi, dar, dai, dd = res

    def bb_from_chunks(dense):
        return _diag_blocks(dense).transpose(2, 0, 1, 3).reshape(SSM_GROUP, N_STATE)

    d_lr, d_li, d_ldt, d_br, d_bi = _s5_prep_bwd(lr, li, ldt, br_t, bi_t, dar, dai,
                                                 bb_from_chunks(dbr), bb_from_chunks(dbi))
    grads["s5_lam_re"] = d_lr.reshape(p["s5_lam_re"].shape)
    grads["s5_lam_im"] = d_li.reshape(p["s5_lam_im"].shape)
    grads["s5_log_dt"] = d_ldt.reshape(p["s5_log_dt"].shape)
    grads["s5_b_re"] = d_br.T.reshape(p["s5_b_re"].shape)
    grads["s5_b_im"] = d_bi.T.reshape(p["s5_b_im"].shape)
    grads["s5_c_re"] = _diag_blocks(dcr).reshape(p["s5_c_re"].shape)
    grads["s5_c_im"] = -_diag_blocks(dci).reshape(p["s5_c_im"].shape)
    grads["s5_d"] = dd.reshape(p["s5_d"].shape)
    grads["s5_b_glu"] = d_bglu.reshape(1, D_MODEL)
    dconv = jnp.stack([dconv0, dconv1]).reshape(2, N_DEV, 8, -1)
    grads["ffn_conv_w"] = dconv[:, :, 0:3, :].transpose(0, 2, 1, 3)
    grads["ffn_conv_b"] = dconv[:, :, 3, :]
    grads["ln_gain"] = jnp.stack([dg0, dg1, dg2, dg3]).reshape(2, 2, D_MODEL)
    grads["ln_bias"] = jnp.stack([db0, db1, db2, db3]).reshape(2, 2, D_MODEL)
    return loss_tile[0, 0], dx.reshape(x.shape), grads


SMALL_REPLICATED = ("s5_lam_re", "s5_lam_im", "s5_log_dt", "s5_b_re", "s5_b_im", "s5_c_re", "s5_c_im", "s5_d",
                    "rel_bias", "ffn_conv_b")
SMALL_SHARDED = ("s5_b_glu", "ffn_conv_w", "ln_gain", "ln_bias")
BIG = ("s5_w_glu", "s5_w_out", "attn_w_kv", "attn_w_q", "attn_w_out", "ffn_w_up", "ffn_w_down")
WEIGHTS = ("s5_lam_re", "s5_lam_im", "s5_log_dt", "s5_b_re", "s5_b_im", "s5_c_re", "s5_c_im", "s5_d", "s5_w_glu",
           "s5_b_glu", "s5_w_out", "attn_w_kv", "attn_w_q", "attn_w_out", "rel_bias", "ffn_w_up", "ffn_conv_w",
           "ffn_conv_b", "ffn_w_down", "ln_gain", "ln_bias")


class _Exchanges:
    def __init__(self, w):
        self.w = w
        self.parts = {}
        self.up = w["ffn_w_up"].astype(BF16)
        self.down = w["ffn_w_down"].astype(BF16)

    def gather_first(self):
        w = self.w
        srcs = [w["s5_w_glu"][0].astype(BF16), w["s5_w_out"][0].astype(BF16),
                _pack([w[k] for k in SMALL_SHARDED]), self.up, self.down]
        outs = [_sds((N_DEV,) + (s.shape[1:] if i in (3, 4) else s.shape), s.dtype) for i, s in enumerate(srcs)]
        items = [(i, (0,) if i in (3, 4) else (), i, ("slot",)) for i in range(len(srcs))]
        return _gather_plan(srcs, outs, items)

    def gather_attn(self, piece):
        w = self.w
        if piece in ("kv0", "kv1"):
            src, rows = w["attn_w_kv"].astype(BF16), w["attn_w_kv"].shape[0] // 2
            return _gather_plan([src], [_sds((N_DEV, rows, src.shape[1]), BF16)],
                                [(0, (pl.ds(int(piece[2]) * rows, rows),), 0, ("slot",))])
        src = (w["attn_w_q"] if piece == "q" else w["attn_w_out"])[0].astype(BF16)
        return _gather_plan([src], [_sds((N_DEV,) + src.shape, BF16)], [(0, (), 0, ("slot",))])

    def take_attn(self, pieces, wfull):
        wfull = dict(wfull)
        if "kv0" in pieces:
            wfull["attn_w_kv"] = jnp.concatenate([pieces["kv0"], pieces["kv1"]], axis=1)
            wfull["attn_w_q"] = pieces["q"]
        if "out" in pieces:
            wfull["attn_w_out"] = pieces["out"].reshape(D_MODEL, D_MODEL)
        return wfull

    def take_first(self, g, p):
        d = D_MODEL
        w = self.w
        wfull = {
            "s5_w_glu": g[0].reshape(d, d),
            "s5_w_out": g[1].reshape(d, d),
            "ffn_w_up": [g[3], None],
            "ffn_w_down": [g[4].reshape(4, -1, d), None],
        }
        smalls = [_unpack(g[2][dev], [w[k].shape for k in SMALL_SHARDED]) for dev in range(N_DEV)]
        c_ff = w["ffn_conv_w"].shape[2]
        conv_w = jnp.stack([s[1] for s in smalls], axis=2)
        p = dict(p)
        p.update(s5_b_glu=jnp.concatenate([s[0] for s in smalls], axis=1),
                 ln_gain=jnp.concatenate([s[2] for s in smalls], axis=2),
                 ln_bias=jnp.concatenate([s[3] for s in smalls], axis=2),
                 conv_w=conv_w.transpose(0, 2, 1, 3).reshape(2, 2, 4, 3, c_ff),
                 conv_b=w["ffn_conv_b"].reshape(2, 2, 4, 1, c_ff))
        return p, wfull

    def gather_layer1(self, which, half):
        src = self.up if which == "up" else self.down
        rows = src.shape[1] // 2
        return _gather_plan([src], [_sds((N_DEV, rows) + src.shape[2:], BF16)],
                            [(0, (1, pl.ds(half * rows, rows)), 0, ("slot",))])

    def take_layer1(self, halves, wfull):
        up = jnp.concatenate(halves["up"], axis=1)
        down = jnp.concatenate(halves["down"], axis=1)
        wfull = dict(wfull)
        wfull["ffn_w_up"] = [wfull["ffn_w_up"][0], up]
        wfull["ffn_w_down"] = [wfull["ffn_w_down"][0], down.reshape(4, -1, D_MODEL)]
        return wfull

    def scatter(self, grads, names):
        srcs = [grads[k].reshape(N_DEV, -1, grads[k].shape[-1]) for k in names]
        plan = _scatter_plan(srcs, [_sds(s.shape, BF16) for s in srcs],
                             [(i, i, ("slot",)) for i in range(len(names))])
        plan.names = names
        return plan

    def keep(self, plan, outs):
        for k, o in zip(plan.names, outs):
            self.parts[k] = o


def kernel(x, s5_lam_re, s5_lam_im, s5_log_dt, s5_b_re, s5_b_im, s5_c_re, s5_c_im, s5_d, s5_w_glu, s5_b_glu, s5_w_out, attn_w_kv, attn_w_q, attn_w_out, rel_bias, ffn_w_up, ffn_conv_w, ffn_conv_b, ffn_w_down, ln_gain, ln_bias, loss_target, m_s5_lam_re, m_s5_lam_im, m_s5_log_dt, m_s5_b_re, m_s5_b_im, m_s5_c_re, m_s5_c_im, m_s5_d, m_s5_w_glu, m_s5_b_glu, m_s5_w_out, m_attn_w_kv, m_attn_w_q, m_attn_w_out, m_rel_bias, m_ffn_w_up, m_ffn_conv_w, m_ffn_conv_b, m_ffn_w_down, m_ln_gain, m_ln_bias, v_s5_lam_re, v_s5_lam_im, v_s5_log_dt, v_s5_b_re, v_s5_b_im, v_s5_c_re, v_s5_c_im, v_s5_d, v_s5_w_glu, v_s5_b_glu, v_s5_w_out, v_attn_w_kv, v_attn_w_q, v_attn_w_out, v_rel_bias, v_ffn_w_up, v_ffn_conv_w, v_ffn_conv_b, v_ffn_w_down, v_ln_gain, v_ln_bias):
    w = dict(s5_lam_re=s5_lam_re, s5_lam_im=s5_lam_im, s5_log_dt=s5_log_dt, s5_b_re=s5_b_re, s5_b_im=s5_b_im,
             s5_c_re=s5_c_re, s5_c_im=s5_c_im, s5_d=s5_d, s5_w_glu=s5_w_glu, s5_b_glu=s5_b_glu, s5_w_out=s5_w_out,
             attn_w_kv=attn_w_kv, attn_w_q=attn_w_q, attn_w_out=attn_w_out, rel_bias=rel_bias, ffn_w_up=ffn_w_up,
             ffn_conv_w=ffn_conv_w, ffn_conv_b=ffn_conv_b, ffn_w_down=ffn_w_down, ln_gain=ln_gain, ln_bias=ln_bias)
    mom = dict(s5_lam_re=m_s5_lam_re, s5_lam_im=m_s5_lam_im, s5_log_dt=m_s5_log_dt, s5_b_re=m_s5_b_re,
               s5_b_im=m_s5_b_im, s5_c_re=m_s5_c_re, s5_c_im=m_s5_c_im, s5_d=m_s5_d, s5_w_glu=m_s5_w_glu,
               s5_b_glu=m_s5_b_glu, s5_w_out=m_s5_w_out, attn_w_kv=m_attn_w_kv, attn_w_q=m_attn_w_q,
               attn_w_out=m_attn_w_out, rel_bias=m_rel_bias, ffn_w_up=m_ffn_w_up, ffn_conv_w=m_ffn_conv_w,
               ffn_conv_b=m_ffn_conv_b, ffn_w_down=m_ffn_w_down, ln_gain=m_ln_gain, ln_bias=m_ln_bias)
    var = dict(s5_lam_re=v_s5_lam_re, s5_lam_im=v_s5_lam_im, s5_log_dt=v_s5_log_dt, s5_b_re=v_s5_b_re,
               s5_b_im=v_s5_b_im, s5_c_re=v_s5_c_re, s5_c_im=v_s5_c_im, s5_d=v_s5_d, s5_w_glu=v_s5_w_glu,
               s5_b_glu=v_s5_b_glu, s5_w_out=v_s5_w_out, attn_w_kv=v_attn_w_kv, attn_w_q=v_attn_w_q,
               attn_w_out=v_attn_w_out, rel_bias=v_rel_bias, ffn_w_up=v_ffn_w_up, ffn_conv_w=v_ffn_conv_w,
               ffn_conv_b=v_ffn_conv_b, ffn_w_down=v_ffn_w_down, ln_gain=v_ln_gain, ln_bias=v_ln_bias)
    bsz, seq, _ = x.shape
    me = 4 * lax.axis_index("x") + 2 * lax.axis_index("y") + lax.axis_index("c")

    ex = _Exchanges(w)
    loss_part, grad_x, g = _local_step(x, loss_target, dict(w), None, bsz, seq, ex=ex)
    loss = lax.psum(loss_part, ("x", "y", "c"))

    d = D_MODEL
    results = {}
    for name in BIG:
        shard = w[name]
        if name == "ffn_w_up":
            parts = [ex.parts[name + "0"], ex.parts[name + "1"]]
            outs4 = _adamw("adamw_" + name, parts, *(jnp.swapaxes(a, 1, 2) for a in (shard, mom[name], var[name])))
            results[name] = tuple(jnp.swapaxes(o, 1, 2) for o in outs4)
            continue
        if name == "ffn_w_down":
            parts, r3 = [ex.parts[name + "0"], ex.parts[name + "1"]], shard.shape
        else:
            parts, r3 = [ex.parts[name]], (1, -1, shard.shape[-1])
        outs4 = _adamw("adamw_" + name, parts, shard.reshape(r3), mom[name].reshape(r3), var[name].reshape(r3))
        results[name] = tuple(o.reshape(shard.shape) for o in outs4)

    small_names = SMALL_REPLICATED + SMALL_SHARDED
    packed = _pack([g[k] for k in small_names])
    pad = (-packed.shape[0]) % (8 * N_DEV)
    if pad:
        packed = jnp.concatenate([packed, jnp.zeros((pad, 128), F32)], axis=0)
    piece = packed.shape[0] // N_DEV
    slots = _exchange("reduce_small", [packed.reshape(N_DEV, piece, 128)], [_sds((N_DEV, piece, 128), F32)],
                      [(0, ("peer",), 0, ("me",))])[0]
    summed = _exchange("gather_small", [_sum_slots(slots)], [_sds((N_DEV, piece, 128), F32)],
                       [(0, (), 0, ("me",))])[0]
    total = _unpack(summed.reshape(N_DEV * piece, 128), [g[k].shape for k in small_names])
    gsum = dict(zip(small_names, total))
    mine = {k: gsum[k] for k in SMALL_REPLICATED}
    mine["s5_b_glu"] = lax.dynamic_slice_in_dim(gsum["s5_b_glu"], me * (d // N_DEV), d // N_DEV, axis=1)
    mine["ffn_conv_w"] = lax.dynamic_index_in_dim(gsum["ffn_conv_w"], me, axis=2, keepdims=False)
    mine["ln_gain"] = lax.dynamic_slice_in_dim(gsum["ln_gain"], me * (d // N_DEV), d // N_DEV, axis=2)
    mine["ln_bias"] = lax.dynamic_slice_in_dim(gsum["ln_bias"], me * (d // N_DEV), d // N_DEV, axis=2)
    gp = _pack([mine[k] for k in small_names])
    outs4 = _adamw("adamw_small", [gp[None]], _pack([w[k] for k in small_names])[None],
                   _pack([mom[k] for k in small_names])[None], _pack([var[k] for k in small_names])[None])
    outs4 = [o[0] for o in outs4]
    shapes = [w[k].shape for k in small_names]
    unpacked = [_unpack(o, shapes) for o in outs4]
    for i, k in enumerate(small_names):
        results[k] = tuple(unpacked[j][i] for j in range(4))

    out = [loss, grad_x]
    for j in range(4):
        out.extend(results[k][j] for k in WEIGHTS)
    return tuple(out)
```

```python
import math

import numpy as np
import jax
import jax.numpy as jnp
from jax import lax
from jax.experimental import pallas as pl
from jax.experimental.pallas import tpu as pltpu

F32 = jnp.float32
BF16 = jnp.bfloat16
HIGHEST = lax.Precision.HIGHEST

N_DEV = 8
D_MODEL = 1024
SSM_GROUP = 16
SSM_STATE = 64
SSM_CHUNKS = 8
CHUNK_CH = D_MODEL // SSM_CHUNKS
CHUNK_ST = CHUNK_CH // SSM_GROUP * SSM_STATE
N_STATE = D_MODEL // SSM_GROUP * SSM_STATE
HEAD_DIM = 64
N_HEADS = 16
BAND = 128
DILATIONS = (1, 4, 16)
REL_BUCKETS = 32
REL_MAX_DIST = 2048
NEG_BIG = -1e30
DN_ALPHA = (2.0 * 2) ** 0.25
LN_EPS = 1e-5
ADAM_LR = 0.001
ADAM_B1 = 0.9
ADAM_B2 = 0.999
ADAM_EPS = 1e-08
ADAM_WD = 0.01
ADAM_STEP = 10

TM = 512
TM_MM = 2048
TM_MM_F32 = 1024
T_SCAN = 1024
VMEM_LIMIT = 56 * 1024 * 1024
FFN_HIDDEN = BF16

NN = (((1,), (0,)), ((), ()))
NT = (((1,), (1,)), ((), ()))
TN = (((0,), (0,)), ((), ()))


class _Plan:
    def __init__(self, srcs, out_shapes, n_sems, n_local, phases):
        self.srcs, self.out_shapes, self.n_sems, self.n_local, self.phases = srcs, out_shapes, n_sems, n_local, phases


def _call(body, name, grid, in_specs, out_specs, out_shape, scratch=(), comm=None, fill=()):
    params = dict(dimension_semantics=("arbitrary",) * len(grid), vmem_limit_bytes=VMEM_LIMIT)
    if comm is None and not fill:
        return pl.pallas_call(
            body, name=name, grid=grid, in_specs=in_specs, out_specs=out_specs, out_shape=out_shape,
            scratch_shapes=list(scratch), compiler_params=pltpu.CompilerParams(**params))
    single = not isinstance(out_specs, (list, tuple))
    o_specs = [out_specs] if single else list(out_specs)
    o_shape = [out_shape] if single else list(out_shape)
    n_in, n_out, n_scr, n_fill = len(in_specs), len(o_specs), len(scratch), len(fill)
    srcs = list(comm.srcs) if comm else []
    c_shapes = list(comm.out_shapes) if comm else []
    n_cin, n_cout = len(srcs), len(c_shapes)
    total = int(np.prod(grid))

    def wrapped(*refs):
        ins = refs[:n_in]
        refs = refs[n_in + n_fill:]
        cins, outs, couts = refs[:n_cin], refs[n_cin:n_cin + n_out], refs[n_cin + n_out:n_cin + n_out + n_cout]
        rest = refs[n_cin + n_out + n_cout:]
        scr, sems = rest[:n_scr], rest[n_scr:]
        if comm is None:
            body(*ins, *outs, *scr)
            return
        step = pl.program_id(0)
        for ax in range(1, len(grid)):
            step = step * grid[ax] + pl.program_id(ax)
        phases = comm.phases(cins, couts, *sems, total)
        for at, action in phases:
            if at == 0:
                pl.when(step == 0)(action)
        body(*ins, *outs, *scr)
        for at, action in phases:
            if at > 0:
                pl.when(step == at)(action)

    any_spec = pl.BlockSpec(memory_space=pl.ANY)
    sems = [] if comm is None else [pltpu.SemaphoreType.DMA((comm.n_sems,)), pltpu.SemaphoreType.DMA((comm.n_sems,)),
                                    pltpu.SemaphoreType.DMA((max(comm.n_local, 1),))]
    call = pl.pallas_call(
        wrapped, name=name, grid=grid, in_specs=list(in_specs) + [any_spec] * (n_fill + n_cin),
        out_specs=o_specs + [any_spec] * n_cout, out_shape=o_shape + c_shapes,
        scratch_shapes=list(scratch) + sems, input_output_aliases={n_in + j: j for j in range(n_fill)},
        compiler_params=pltpu.CompilerParams(has_side_effects=comm is not None, **params))

    def run(*args):
        res = call(*args, *fill, *srcs)
        own = res[0] if single else res[:n_out]
        return (own, res[n_out:]) if comm is not None else own

    return run


def _sds(shape, dtype):
    return jax.ShapeDtypeStruct(shape, dtype)


def _mm(name, a, b, *, dims, grid, a_spec, b_spec, o_spec, out_shape, nred=1, red_axis=0,
        add=None, add_spec=None, add_scale=1.0, comm=None, norm=None, inner=None, post=None):
    has_add = add is not None
    n_extra = (1 if has_add else 0) + (1 if post else 0) + (2 if norm else 0)
    n_out = 3 if norm else 1
    acc_shape = tuple(s for s in o_spec.block_shape if s is not None)

    def body(*refs):
        a_ref, b_ref = refs[:2]
        extra = refs[2:2 + n_extra]
        outs = refs[2 + n_extra:2 + n_extra + n_out]
        rest = refs[2 + n_extra + n_out:]
        o_ref = outs[0]
        if inner is None:
            prod = lax.dot_general(a_ref[...].astype(BF16), b_ref[...].astype(BF16), dims,
                                   preferred_element_type=F32)
        else:
            nb, a_mode = inner
            width = a_ref.shape[-1] // nb
            prod = None
            for d in range(nb):
                a_d = a_ref[d] if a_mode == "lead" else a_ref[:, d * width:(d + 1) * width]
                part = lax.dot_general(a_d.astype(BF16), b_ref[d].astype(BF16), dims, preferred_element_type=F32)
                prod = part if prod is None else prod + part

        def finish(val):
            if has_add:
                val = val + add_scale * extra[0][...].astype(F32)
            if post:
                val = post[0](val, extra[1 if has_add else 0][...])
            o_ref[...] = val.astype(o_ref.dtype)
            if norm:
                xhat, _ = _ln_stats(val)
                y = xhat * extra[-2][...] + extra[-1][...]
                outs[1][...] = y
                outs[2][...] = y.astype(BF16)

        if nred == 1:
            finish(prod)
        else:
            acc = rest[0]
            k = pl.program_id(red_axis)

            @pl.when(k == 0)
            def _():
                acc[...] = prod

            @pl.when(k > 0)
            def _():
                acc[...] += prod

            @pl.when(k == nred - 1)
            def _():
                finish(acc[...])

    in_specs = [a_spec, b_spec] + ([add_spec] if has_add else [])
    scratch = [pltpu.VMEM(acc_shape, F32)] if nred > 1 else []
    args = (a, b) + ((add,) if has_add else ())
    if post:
        in_specs.append(o_spec)
        args += (post[1],)
    if norm:
        vec = pl.BlockSpec((1, out_shape.shape[1]), lambda *_: (0, 0))
        in_specs += [vec, vec]
        args += tuple(norm)
        o_spec = [o_spec, o_spec, o_spec]
        out_shape = [out_shape, out_shape, _sds(out_shape.shape, BF16)]
    return _call(body, name, grid, in_specs, o_spec, out_shape, scratch, comm=comm)(*args)


def _mm_nn(name, a, b, out_dtype, blocked_out=False, comm=None, blocks=None, residual=None, norm=None):
    TM = TM_MM if out_dtype == BF16 else TM_MM_F32
    m, k = a.shape
    _, _, n = b.shape
    nb = b.shape[0] if blocks is None else blocks[1]
    first = 0 if blocks is None else blocks[0]
    if blocked_out:
        o_spec = pl.BlockSpec((None, TM, n), lambda d, i: (d, i, 0))
        out_shape = _sds((nb, m, n), out_dtype)
    else:
        o_spec = pl.BlockSpec((TM, n), lambda d, i: (i, d))
        out_shape = _sds((m, nb * n), out_dtype)
    return _mm(name, a, b, dims=NN, grid=(nb, m // TM),
               a_spec=pl.BlockSpec((TM, k), lambda d, i: (i, 0)),
               b_spec=pl.BlockSpec((None, k, n), lambda d, i: (d + first, 0, 0)),
               o_spec=o_spec, out_shape=out_shape, comm=comm, norm=norm,
               add=residual, add_spec=o_spec if residual is not None else None, add_scale=DN_ALPHA)


def _mm_nn_red(name, a, b, out_dtype, comm=None, residual=None, norm=None):
    nb, m, k = a.shape
    n = b.shape[2]
    o_spec = pl.BlockSpec((TM, n), lambda i: (i, 0))
    return _mm(name, a, b, dims=NN, grid=(m // TM,), inner=(nb, "lead"),
               a_spec=pl.BlockSpec((nb, TM, k), lambda i: (0, i, 0)),
               b_spec=pl.BlockSpec((nb, k, n), lambda i: (0, 0, 0)),
               o_spec=o_spec, out_shape=_sds((m, n), out_dtype), comm=comm, norm=norm,
               add=residual, add_spec=o_spec if residual is not None else None, add_scale=DN_ALPHA)


def _mm_tn(name, a, g, nb, out_dtype, a_blocked=False, g_mode="cols", comm=None):
    TM = TM_MM
    if a_blocked:
        _, m, ka = a.shape
        a_spec = pl.BlockSpec((None, TM, ka), lambda d, i: (d, i, 0))
    else:
        m, ka = a.shape
        a_spec = pl.BlockSpec((TM, ka), lambda d, i: (i, 0))
    if g_mode == "cols":
        n = g.shape[1] // nb
        g_spec = pl.BlockSpec((TM, n), lambda d, i: (i, d))
    elif g_mode == "blocked":
        n = g.shape[2]
        g_spec = pl.BlockSpec((None, TM, n), lambda d, i: (d, i, 0))
    else:
        n = g.shape[1]
        g_spec = pl.BlockSpec((TM, n), lambda d, i: (i, 0))
    nred = m // TM
    return _mm(name, a, g, dims=TN, grid=(nb, nred), nred=nred, red_axis=1,
               a_spec=a_spec, b_spec=g_spec,
               o_spec=pl.BlockSpec((None, ka, n), lambda d, i: (d, 0, 0)),
               out_shape=_sds((nb, ka, n), out_dtype), comm=comm)


def _mm_nt_red(name, g, w, out_dtype, g_blocked, add=None, add_scale=1.0, comm=None, blocks=None, post=None):
    _, k, n = w.shape
    nb = w.shape[0] if blocks is None else blocks[1]
    first = 0 if blocks is None else blocks[0]
    tm = TM_MM_F32 if nb * k * n <= 4 * 1024 * 768 else TM
    if g_blocked:
        m = g.shape[1]
        g_spec = pl.BlockSpec((nb, tm, n), lambda i: (0, i, 0))
    else:
        m = g.shape[0]
        g_spec = pl.BlockSpec((tm, nb * n), lambda i: (i, 0))
    o_spec = pl.BlockSpec((tm, k), lambda i: (i, 0))
    return _mm(name, g, w, dims=NT, grid=(m // tm,), inner=(nb, "lead" if g_blocked else "cols"),
               a_spec=g_spec, b_spec=pl.BlockSpec((nb, k, n), lambda i: (first // nb, 0, 0)),
               o_spec=o_spec, out_shape=_sds((m, k), out_dtype),
               add=add, add_spec=o_spec if add is not None else None, add_scale=add_scale, comm=comm, post=post)


def _mm_nt_out(name, g, w, out_dtype):
    TM = TM_MM
    m, n = g.shape
    nb, k, _ = w.shape
    return _mm(name, g, w, dims=NT, grid=(nb, m // TM),
               a_spec=pl.BlockSpec((TM, n), lambda d, i: (i, 0)),
               b_spec=pl.BlockSpec((None, k, n), lambda d, i: (d, 0, 0)),
               o_spec=pl.BlockSpec((None, TM, k), lambda d, i: (d, i, 0)),
               out_shape=_sds((nb, m, k), out_dtype))


def _ln_stats(r):
    mu = jnp.mean(r, axis=-1, keepdims=True)
    xc = r - mu
    var = jnp.mean(xc * xc, axis=-1, keepdims=True)
    rstd = lax.rsqrt(var + LN_EPS)
    return xc * rstd, rstd


def _ln_bwd(name, r, dy, gain):
    n, d = r.shape

    def body(r_ref, dy_ref, g_ref, dr_ref, drb_ref, dg_ref, db_ref):
        i = pl.program_id(0)
        xhat, rstd = _ln_stats(r_ref[...])
        dy_v = dy_ref[...]
        dxh = dy_v * g_ref[...]
        m1 = jnp.mean(dxh, axis=-1, keepdims=True)
        m2 = jnp.mean(dxh * xhat, axis=-1, keepdims=True)
        dr = rstd * (dxh - m1 - xhat * m2)
        dr_ref[...] = dr
        drb_ref[...] = dr.astype(BF16)
        dg = jnp.sum(dy_v * xhat, axis=0, keepdims=True)
        db = jnp.sum(dy_v, axis=0, keepdims=True)

        @pl.when(i == 0)
        def _():
            dg_ref[...] = dg
            db_ref[...] = db

        @pl.when(i > 0)
        def _():
            dg_ref[...] += dg
            db_ref[...] += db

    row = pl.BlockSpec((TM, d), lambda i: (i, 0))
    vec = pl.BlockSpec((1, d), lambda i: (0, 0))
    return _call(body, name, (n // TM,), [row, row, vec], [row, row, vec, vec],
                 [_sds((n, d), F32), _sds((n, d), BF16), _sds((1, d), F32), _sds((1, d), F32)])(r, dy, gain)


def _loss_head(y, target):
    n, d = y.shape

    def body(y_ref, t_ref, l_ref, dy_ref):
        i = pl.program_id(0)
        e = y_ref[...] - t_ref[...]
        dy_ref[...] = e * (1.0 / d)
        part = jnp.sum(jnp.sum(e * e, axis=1, keepdims=True), axis=0, keepdims=True) * (0.5 / d)
        part = jnp.broadcast_to(part, (8, 128))

        @pl.when(i == 0)
        def _():
            l_ref[...] = part

        @pl.when(i > 0)
        def _():
            l_ref[...] += part

    row = pl.BlockSpec((TM, d), lambda i: (i, 0))
    return _call(body, "loss_head", (n // TM,), [row, row],
                 [pl.BlockSpec((8, 128), lambda i: (0, 0)), row],
                 [_sds((8, 128), F32), _sds((n, d), F32)])(y, target)


def _group_expand_matrix():
    e = np.zeros((D_MODEL // SSM_GROUP, N_STATE), np.float32)
    for g in range(D_MODEL // SSM_GROUP):
        e[g, g * SSM_STATE:(g + 1) * SSM_STATE] = 1.0
    return jnp.asarray(e)


def _discretise(lr, li, ldt, br, bi, expand):
    dt = jnp.dot(jnp.exp(ldt), expand, precision=HIGHEST, preferred_element_type=F32)
    mag = jnp.exp(lr * dt)
    th = li * dt
    ab_r = mag * jnp.cos(th)
    ab_i = mag * jnp.sin(th)
    den = lr * lr + li * li
    nr = ab_r - 1.0
    co_r = (nr * lr + ab_i * li) / den
    co_i = (ab_i * lr - nr * li) / den
    bb_r = co_r * br - co_i * bi
    bb_i = co_r * bi + co_i * br
    return ab_r, ab_i, bb_r, bb_i


def _cmul(ar, ai, br, bi):
    return ar * br - ai * bi, ar * bi + ai * br


def _s5_prep(lr, li, ldt, br, bi):
    s = N_STATE
    expand = _group_expand_matrix()

    def body(lr_ref, li_ref, ldt_ref, br_ref, bi_ref, e_ref, abr_ref, abi_ref, bbr_ref, bbi_ref, tab_ref):
        ab_r, ab_i, bb_r, bb_i = _discretise(lr_ref[...], li_ref[...], ldt_ref[...], br_ref[...],
                                             bi_ref[...], e_ref[...])
        abr_ref[...] = ab_r
        abi_ref[...] = ab_i
        bbr_ref[...] = bb_r
        bbi_ref[...] = bb_i
        row = lax.broadcasted_iota(jnp.int32, (8, s), 0)
        for base, sign in ((0, 1.0), (8, -1.0)):
            p = [None] * 9
            p[1] = (ab_r, sign * ab_i)
            p[2] = _cmul(*p[1], *p[1])
            p[3] = _cmul(*p[2], *p[1])
            p[4] = _cmul(*p[2], *p[2])
            p[5] = _cmul(*p[4], *p[1])
            p[6] = _cmul(*p[4], *p[2])
            p[7] = _cmul(*p[4], *p[3])
            p[8] = _cmul(*p[4], *p[4])
            for j, k in enumerate((1, 2, 4)):
                keep = (row >= k) if base == 0 else (row < 8 - k)
                tab_ref[base + 2 * j] = jnp.where(keep, jnp.broadcast_to(p[k][0], (8, s)), 0.0)
                tab_ref[base + 2 * j + 1] = jnp.where(keep, jnp.broadcast_to(p[k][1], (8, s)), 0.0)
            pw_r = jnp.zeros((8, s), F32)
            pw_i = jnp.zeros((8, s), F32)
            for i in range(8):
                e = i + 1 if base == 0 else 8 - i
                pw_r = jnp.where(row == i, jnp.broadcast_to(p[e][0], (8, s)), pw_r)
                pw_i = jnp.where(row == i, jnp.broadcast_to(p[e][1], (8, s)), pw_i)
            tab_ref[base + 6] = pw_r
            tab_ref[base + 7] = pw_i

    def full(shape):
        return pl.BlockSpec(shape, lambda i: (0,) * len(shape))

    g = D_MODEL // SSM_GROUP
    return _call(body, "s5_prep", (1,),
                 [full((1, s)), full((1, s)), full((1, g)), full((16, s)), full((16, s)), full((g, s))],
                 [full((1, s)), full((1, s)), full((16, s)), full((16, s)), full((16, 8, s))],
                 [_sds((1, s), F32), _sds((1, s), F32), _sds((16, s), F32), _sds((16, s), F32),
                  _sds((16, 8, s), F32)])(lr, li, ldt, br, bi, expand)


def _s5_prep_bwd(lr, li, ldt, br, bi, d_abr, d_abi, d_bbr, d_bbi):
    s = N_STATE
    g = D_MODEL // SSM_GROUP
    expand = _group_expand_matrix()

    def body(lr_ref, li_ref, ldt_ref, br_ref, bi_ref, e_ref, c1, c2, c3, c4, o1, o2, o3, o4, o5):
        e_val = e_ref[...]
        _, vjp = jax.vjp(lambda a, b, c, d, e: _discretise(a, b, c, d, e, e_val),
                         lr_ref[...], li_ref[...], ldt_ref[...], br_ref[...], bi_ref[...])
        grads = vjp((c1[...], c2[...], c3[...], c4[...]))
        for o, gr in zip((o1, o2, o3, o4, o5), grads):
            o[...] = gr

    def full(shape):
        return pl.BlockSpec(shape, lambda i: (0,) * len(shape))

    return _call(body, "s5_prep_bwd", (1,),
                 [full((1, s)), full((1, s)), full((1, g)), full((16, s)), full((16, s)), full((g, s)),
                  full((1, s)), full((1, s)), full((16, s)), full((16, s))],
                 [full((1, s)), full((1, s)), full((1, g)), full((16, s)), full((16, s))],
                 [_sds((1, s), F32), _sds((1, s), F32), _sds((1, g), F32), _sds((16, s), F32),
                  _sds((16, s), F32)])(lr, li, ldt, br, bi, expand, d_abr, d_abi, d_bbr, d_bbi)


def _scan8(vr, vi, tab_ref, base, reverse):
    for j, k in enumerate((1, 2, 4)):
        mr = tab_ref[base + 2 * j]
        mi = tab_ref[base + 2 * j + 1]
        shift = 8 - k if reverse else k
        sr = pltpu.roll(vr, shift, 0)
        si = pltpu.roll(vi, shift, 0)
        vr, vi = vr + mr * sr - mi * si, vi + mr * si + mi * sr
    return vr, vi


def _s5_scan_fwd(x, bbr, bbi, cr, ci, dvec, tab, seq, comm=None):
    n, d = x.shape
    nt = seq // T_SCAN
    nblk = T_SCAN // 8
    st = CHUNK_ST

    def body(x_ref, bbr_ref, bbi_ref, cr_ref, ci_ref, d_ref, tab_ref, y_ref, hr_ref, hi_ref, car_r, car_i):
        t = pl.program_id(2)

        @pl.when(t == 0)
        def _():
            car_r[...] = jnp.zeros_like(car_r)
            car_i[...] = jnp.zeros_like(car_i)

        u = x_ref[...]
        ub = u.astype(BF16)
        hr_ref[...] = jnp.dot(ub, bbr_ref[...], preferred_element_type=F32)
        hi_ref[...] = jnp.dot(ub, bbi_ref[...], preferred_element_type=F32)

        def step(i, carry):
            sl = pl.ds(pl.multiple_of(i * 8, 8), 8)
            vr, vi = _scan8(hr_ref[sl, :], hi_ref[sl, :], tab_ref, 0, False)
            c_r = jnp.broadcast_to(car_r[...], (8, st))
            c_i = jnp.broadcast_to(car_i[...], (8, st))
            pr = tab_ref[6]
            pi = tab_ref[7]
            vr, vi = vr + pr * c_r - pi * c_i, vi + pr * c_i + pi * c_r
            hr_ref[sl, :] = vr
            hi_ref[sl, :] = vi
            car_r[...] = vr[7:8, :]
            car_i[...] = vi[7:8, :]
            return carry

        lax.fori_loop(0, nblk, step, 0, unroll=4)
        y = lax.dot_general(hr_ref[...].astype(BF16), cr_ref[...], NT, preferred_element_type=F32)
        y = y - lax.dot_general(hi_ref[...].astype(BF16), ci_ref[...], NT, preferred_element_type=F32)
        y_ref[...] = y + d_ref[...] * u

    row = lambda c, b, t: (b * nt + t, c)
    mat = pl.BlockSpec((None, CHUNK_CH, st), lambda c, b, t: (c, 0, 0))
    return _call(
        body, "s5_scan_fwd", (SSM_CHUNKS, n // seq, nt),
        [pl.BlockSpec((T_SCAN, CHUNK_CH), row), mat, mat, mat, mat,
         pl.BlockSpec((1, CHUNK_CH), lambda c, b, t: (0, c)),
         pl.BlockSpec((16, 8, st), lambda c, b, t: (0, 0, c))],
        [pl.BlockSpec((T_SCAN, CHUNK_CH), row), pl.BlockSpec((T_SCAN, st), row), pl.BlockSpec((T_SCAN, st), row)],
        [_sds((n, d), F32), _sds((n, N_STATE), F32), _sds((n, N_STATE), F32)],
        [pltpu.VMEM((1, st), F32), pltpu.VMEM((1, st), F32)], comm=comm,
    )(x, bbr, bbi, cr, ci, dvec, tab)


def _s5_scan_bwd(x, dy, dres, hr, hi, bbr, bbi, cr, ci, dvec, tab, seq, comm=None):
    n, d = x.shape
    nt = seq // T_SCAN
    nblk = T_SCAN // 8
    st = CHUNK_ST

    def body(x_ref, dy_ref, dres_ref, hr_ref, hi_ref, hpr_ref, hpi_ref, bbr_ref, bbi_ref, cr_ref, ci_ref,
             d_ref, tab_ref, dx_ref, dcr_ref, dci_ref, dbr_ref, dbi_ref, dar_ref, dai_ref, dd_ref,
             g_r, g_i, car_r, car_i):
        b = pl.program_id(1)
        tg = pl.program_id(2)
        first = jnp.logical_and(b == 0, tg == 0)

        @pl.when(tg == 0)
        def _():
            car_r[...] = jnp.zeros_like(car_r)
            car_i[...] = jnp.zeros_like(car_i)

        u = x_ref[...]
        dyv = dy_ref[...]
        ub = u.astype(BF16)
        dyb = dyv.astype(BF16)
        g_r[...] = jnp.dot(dyb, cr_ref[...], preferred_element_type=F32)
        g_i[...] = -jnp.dot(dyb, ci_ref[...], preferred_element_type=F32)

        def step(ii, carry):
            i = nblk - 1 - ii
            sl = pl.ds(pl.multiple_of(i * 8, 8), 8)
            vr, vi = _scan8(g_r[sl, :], g_i[sl, :], tab_ref, 8, True)
            c_r = jnp.broadcast_to(car_r[...], (8, st))
            c_i = jnp.broadcast_to(car_i[...], (8, st))
            pr = tab_ref[14]
            pi = tab_ref[15]
            vr, vi = vr + pr * c_r - pi * c_i, vi + pr * c_i + pi * c_r
            g_r[sl, :] = vr
            g_i[sl, :] = vi
            car_r[...] = vr[0:1, :]
            car_i[...] = vi[0:1, :]
            return carry

        lax.fori_loop(0, nblk, step, 0, unroll=4)
        gr = g_r[...]
        gi = g_i[...]
        grb = gr.astype(BF16)
        gib = gi.astype(BF16)
        du = lax.dot_general(grb, bbr_ref[...], NT, preferred_element_type=F32)
        du = du + lax.dot_general(gib, bbi_ref[...], NT, preferred_element_type=F32)
        dx_ref[...] = DN_ALPHA * dres_ref[...] + du + d_ref[...] * dyv

        h_r = hr_ref[...]
        h_i = hi_ref[...]
        t_orig = nt - 1 - tg
        row0 = lax.broadcasted_iota(jnp.int32, (8, st), 0) == 0
        halo_ok = t_orig > 0

        def previous(h, halo_ref):
            top = jnp.broadcast_to(jnp.where(halo_ok, halo_ref[7:8, :], 0.0), (8, st))
            rolled = pltpu.roll(h, 1, 0)
            return jnp.concatenate([jnp.where(row0, top, rolled[:8, :]), rolled[8:, :]], axis=0)

        hp_r = previous(h_r, hpr_ref)
        hp_i = previous(h_i, hpi_ref)
        dar = jnp.sum(gr * hp_r + gi * hp_i, axis=0, keepdims=True)
        dai = jnp.sum(gi * hp_r - gr * hp_i, axis=0, keepdims=True)
        dcr = lax.dot_general(dyb, h_r.astype(BF16), TN, preferred_element_type=F32)
        dci = lax.dot_general(dyb, h_i.astype(BF16), TN, preferred_element_type=F32)
        dbr = lax.dot_general(ub, grb, TN, preferred_element_type=F32)
        dbi = lax.dot_general(ub, gib, TN, preferred_element_type=F32)
        dd = jnp.sum(dyv * u, axis=0, keepdims=True)

        @pl.when(first)
        def _():
            dcr_ref[...] = dcr
            dci_ref[...] = dci
            dbr_ref[...] = dbr
            dbi_ref[...] = dbi
            dar_ref[...] = dar
            dai_ref[...] = dai
            dd_ref[...] = dd

        @pl.when(jnp.logical_not(first))
        def _():
            dcr_ref[...] += dcr
            dci_ref[...] += dci
            dbr_ref[...] += dbr
            dbi_ref[...] += dbi
            dar_ref[...] += dar
            dai_ref[...] += dai
            dd_ref[...] += dd

    row = lambda c, b, t: (b * nt + (nt - 1 - t), c)
    halo = lambda c, b, t: (jnp.maximum((b * nt + (nt - 1 - t)) * (T_SCAN // 8) - 1, 0), c)
    mat = pl.BlockSpec((None, CHUNK_CH, st), lambda c, b, t: (c, 0, 0))
    chan = pl.BlockSpec((T_SCAN, CHUNK_CH), row)
    state = pl.BlockSpec((T_SCAN, st), row)
    return _call(
        body, "s5_scan_bwd", (SSM_CHUNKS, n // seq, nt),
        [chan, chan, chan, state, state, pl.BlockSpec((8, st), halo), pl.BlockSpec((8, st), halo),
         mat, mat, mat, mat, pl.BlockSpec((1, CHUNK_CH), lambda c, b, t: (0, c)),
         pl.BlockSpec((16, 8, st), lambda c, b, t: (0, 0, c))],
        [chan, mat, mat, mat, mat, pl.BlockSpec((1, st), lambda c, b, t: (0, c)),
         pl.BlockSpec((1, st), lambda c, b, t: (0, c)), pl.BlockSpec((1, CHUNK_CH), lambda c, b, t: (0, c))],
        [_sds((n, d), F32)] + [_sds((SSM_CHUNKS, CHUNK_CH, st), F32)] * 4
        + [_sds((1, N_STATE), F32), _sds((1, N_STATE), F32), _sds((1, d), F32)],
        [pltpu.VMEM((T_SCAN, st), F32), pltpu.VMEM((T_SCAN, st), F32), pltpu.VMEM((1, st), F32),
         pltpu.VMEM((1, st), F32)], comm=comm,
    )(x, dy, dres, hr, hi, hr, hi, bbr, bbi, cr, ci, dvec, tab)


_GELU_C = math.sqrt(2.0 / math.pi)


def _gelu_parts(y):
    t = jnp.tanh(_GELU_C * (y + 0.044715 * (y * y * y)))
    return 0.5 * (1.0 + t), t


def _glu_fwd(y, w_glu, b_glu):
    n, d = y.shape

    def body(y_ref, w_ref, b_ref, y2_ref, z_ref, gg_ref):
        yv = y_ref[...]
        cdf, _ = _gelu_parts(yv)
        y2 = yv * cdf
        z = jnp.dot(y2.astype(BF16), w_ref[...], preferred_element_type=F32) + b_ref[...]
        y2_ref[...] = y2.astype(BF16)
        z_ref[...] = z
        gg_ref[...] = (y2 * jax.nn.sigmoid(z)).astype(BF16)

    row = pl.BlockSpec((TM, d), lambda i: (i, 0))
    return _call(body, "glu_fwd", (n // TM,),
                 [row, pl.BlockSpec((d, d), lambda i: (0, 0)), pl.BlockSpec((1, d), lambda i: (0, 0))],
                 [row, row, row], [_sds((n, d), BF16), _sds((n, d), F32), _sds((n, d), BF16)])(y, w_glu, b_glu)


def _glu_bwd_gate(y, z, dgg):
    n, d = y.shape

    def body(y_ref, z_ref, dgg_ref, dz_ref, db_ref, t_ref):
        i = pl.program_id(0)
        yv = y_ref[...]
        cdf, _ = _gelu_parts(yv)
        y2 = yv * cdf
        s = jax.nn.sigmoid(z_ref[...])
        dg = dgg_ref[...]
        dz = dg * y2 * s * (1.0 - s)
        dz_ref[...] = dz.astype(BF16)
        t_ref[...] = dg * s
        db = jnp.sum(dz, axis=0, keepdims=True)

        @pl.when(i == 0)
        def _():
            db_ref[...] = db

        @pl.when(i > 0)
        def _():
            db_ref[...] += db

    row = pl.BlockSpec((TM, d), lambda i: (i, 0))
    vec = pl.BlockSpec((1, d), lambda i: (0, 0))
    return _call(body, "glu_bwd_gate", (n // TM,), [row, row, row], [row, vec, row],
                 [_sds((n, d), BF16), _sds((1, d), F32), _sds((n, d), F32)])(y, z, dgg)


def _gelu_bwd(dy2, y):
    cdf, t = _gelu_parts(y)
    dinner = _GELU_C * (1.0 + 3.0 * 0.044715 * y * y)
    return dy2 * (cdf + 0.5 * y * (1.0 - t * t) * dinner)


HALO = 16


def _shift_down(x, halo, k):
    out = pltpu.roll(x, k, 0)
    row = lax.broadcasted_iota(jnp.int32, (8, x.shape[1]), 0)
    top = jnp.where(row < k, pltpu.roll(halo[HALO - 8:HALO, :], k, 0), out[:8, :])
    return jnp.concatenate([top, out[8:, :]], axis=0)


def _conv3(x, halo, w, b):
    x1 = _shift_down(x, halo, 1)
    x2 = _shift_down(x, halo, 2)
    return b + w[0:1, :] * x + w[1:2, :] * x1 + w[2:3, :] * x2, x1, x2


def _conv_gate_fwd(hc, conv_w, conv_b, seq, comm=None):
    _, nb, n, c = hc.shape
    tiles_per_seq = seq // TM

    def body(x_ref, halo_ref, w_ref, b_ref, o_ref, cv_ref):
        i = pl.program_id(1)
        start = (i % tiles_per_seq) == 0
        cv = []
        for h in range(2):
            halo = jnp.where(start, 0.0, halo_ref[h].astype(F32))
            cv.append(_conv3(x_ref[h].astype(F32), halo, w_ref[h], b_ref[h])[0])
            cv_ref[h] = cv[h].astype(BF16)
        o_ref[...] = (cv[1] * jax.nn.sigmoid(cv[1]) * cv[0]).astype(BF16)

    both = pl.BlockSpec((2, None, TM, c), lambda d, i: (0, d, i, 0))
    return _call(
        body, "conv_gate_fwd", (nb, n // TM),
        [both,
         pl.BlockSpec((2, None, HALO, c), lambda d, i: (0, d, jnp.maximum(i * (TM // HALO) - 1, 0), 0)),
         pl.BlockSpec((2, None, 3, c), lambda d, i: (0, d, 0, 0)),
         pl.BlockSpec((2, None, 1, c), lambda d, i: (0, d, 0, 0))],
        [pl.BlockSpec((None, TM, c), lambda d, i: (d, i, 0)), both],
        [_sds((nb, n, c), BF16), _sds((2, nb, n, c), BF16)], comm=comm)(hc, hc, conv_w, conv_b)


def _conv_gate_bwd(hc, cv, dact, conv_w, seq, comm=None):
    _, nb, n, c = hc.shape
    tiles_per_seq = seq // TM
    last_halo = n // HALO - 1
    ext = TM + HALO

    def body(x_ref, cv_ref, cvn_ref, da_ref, dan_ref, w_ref, dx_ref, dw_ref):
        i = pl.program_id(1)
        end = (i % tiles_per_seq) == tiles_per_seq - 1
        row = lax.broadcasted_iota(jnp.int32, (ext, c), 0)
        cv = [jnp.concatenate([cv_ref[h], cvn_ref[h]], axis=0).astype(F32) for h in range(2)]
        da = jnp.concatenate([da_ref[...], dan_ref[...]], axis=0).astype(F32)
        da = jnp.where(jnp.logical_and(end, row >= TM), 0.0, da)
        sg = jax.nn.sigmoid(cv[1])
        silu = cv[1] * sg
        dcv = [da * silu, da * cv[0] * (sg + silu * (1.0 - sg))]
        for h in range(2):
            w = w_ref[h]
            g = dcv[h]
            g0 = g[:TM, :]
            g1 = pltpu.roll(g, ext - 1, 0)[:TM, :]
            g2 = pltpu.roll(g, ext - 2, 0)[:TM, :]
            dx_ref[h] = (w[0:1, :] * g0 + w[1:2, :] * g1 + w[2:3, :] * g2).astype(BF16)
            x = x_ref[h].astype(F32)
            parts = [jnp.sum(g0 * x, axis=0, keepdims=True),
                     jnp.sum(g1 * x, axis=0, keepdims=True),
                     jnp.sum(g2 * x, axis=0, keepdims=True),
                     jnp.sum(g0, axis=0, keepdims=True)]
            upd = jnp.concatenate(parts + [jnp.zeros((4, c), F32)], axis=0)

            @pl.when(i == 0)
            def _():
                dw_ref[h] = upd

            @pl.when(i > 0)
            def _():
                dw_ref[h] += upd

    nxt = lambda i: jnp.minimum((i + 1) * (TM // HALO), last_halo)
    return _call(
        body, "conv_gate_bwd", (nb, n // TM),
        [pl.BlockSpec((2, None, TM, c), lambda d, i: (0, d, i, 0)),
         pl.BlockSpec((2, None, TM, c), lambda d, i: (0, d, i, 0)),
         pl.BlockSpec((2, None, HALO, c), lambda d, i: (0, d, nxt(i), 0)),
         pl.BlockSpec((None, TM, c), lambda d, i: (d, i, 0)),
         pl.BlockSpec((None, HALO, c), lambda d, i: (d, nxt(i), 0)),
         pl.BlockSpec((2, None, 3, c), lambda d, i: (0, d, 0, 0))],
        [pl.BlockSpec((2, None, TM, c), lambda d, i: (0, d, i, 0)),
         pl.BlockSpec((2, None, 8, c), lambda d, i: (0, d, 0, 0))],
        [_sds((2, nb, n, c), BF16), _sds((2, nb, 8, c), F32)], comm=comm)(hc, cv, cv, dact, dact, conv_w)


def _t5_bucket_np(dist):
    exact = REL_BUCKETS // 2
    d = np.maximum(dist, 1).astype(np.float32)
    large = exact + (np.log(d / exact) / math.log(REL_MAX_DIST / exact) * (REL_BUCKETS - exact)).astype(np.int64)
    large = np.minimum(large, REL_BUCKETS - 1)
    return np.where(dist < exact, dist, large).astype(np.int32)


def _bucket_table(dil):
    steps = np.arange(BAND)[:, None] + BAND - np.arange(2 * BAND)[None, :]
    return _t5_bucket_np(np.maximum(steps, 0) * dil).reshape(1, BAND * 2 * BAND).astype(np.float32)


def _bias_expand(rel_t, dil):
    nk = BAND * 2 * BAND
    bucket = jnp.asarray(_bucket_table(dil))

    def body(r_ref, bk_ref, o_ref):
        ids = lax.broadcasted_iota(jnp.int32, (REL_BUCKETS, nk), 0).astype(F32)
        onehot = (ids == bk_ref[...]).astype(F32)
        o_ref[...] = jnp.dot(r_ref[...], onehot, precision=HIGHEST, preferred_element_type=F32)

    return _call(body, "bias_expand", (1,),
                 [pl.BlockSpec((N_HEADS, REL_BUCKETS), lambda i: (0, 0)), pl.BlockSpec((1, nk), lambda i: (0, 0))],
                 pl.BlockSpec((N_HEADS, nk), lambda i: (0, 0)), _sds((N_HEADS, nk), F32))(rel_t, bucket)


def _bias_reduce(dbias, dil):
    nk = BAND * 2 * BAND
    bucket = jnp.asarray(_bucket_table(dil))

    def body(g_ref, bk_ref, o_ref):
        ids = lax.broadcasted_iota(jnp.int32, (REL_BUCKETS, nk), 0).astype(F32)
        onehot = (ids == bk_ref[...]).astype(F32)
        o_ref[...] = lax.dot_general(g_ref[...], onehot, NT, precision=HIGHEST, preferred_element_type=F32)

    return _call(body, "bias_reduce", (1,),
                 [pl.BlockSpec((N_HEADS, nk), lambda i: (0, 0)), pl.BlockSpec((1, nk), lambda i: (0, 0))],
                 pl.BlockSpec((N_HEADS, REL_BUCKETS), lambda i: (0, 0)),
                 _sds((N_HEADS, REL_BUCKETS), F32))(dbias, bucket)


def _valid_mask(n):
    qi = lax.broadcasted_iota(jnp.int32, (BAND, 2 * BAND), 0)
    kj = lax.broadcasted_iota(jnp.int32, (BAND, 2 * BAND), 1)
    steps = qi + BAND - kj
    in_band = jnp.logical_and(steps >= 0, steps <= BAND)
    return jnp.logical_and(in_band, jnp.logical_or(n > 0, kj >= BAND))


ATTN_COLS = {1: 1024, 4: 512, 16: 128}


def _residue(dil, r):
    return slice(None) if dil == 1 else pl.ds(r, BAND, stride=dil)


def _stack_heads(x, first):
    return jnp.concatenate([jnp.where(first, x, 0.0), jnp.where(first, 0.0, x)], axis=0)


def _attn_fwd(q, k, v, bias, g, dil, bsz, seq, comm=None, merge=()):
    n = q.shape[0]
    rows = BAND * dil
    nch = seq // rows
    w = N_HEADS * HEAD_DIM
    wc = ATTN_COLS[dil]
    ncol = w // wc
    hpc = wc // HEAD_DIM
    nj = wc // 128
    slab = lambda k: pltpu.VMEM((k * nj, rows, 128), F32)

    def attend(q_ref, kc_ref, kp_ref, vc_ref, vp_ref, b_ref, others, o_ref, l_ref, qf, kf, vf, of, lf):
        ch = pl.program_id(2)
        valid = _valid_mask(ch)
        valid = jnp.concatenate([valid, valid], axis=0)
        lane = lax.broadcasted_iota(jnp.int32, (BAND, 128), 1)
        first = lane < HEAD_DIM
        for j in range(nj):
            sl = slice(128 * j, 128 * (j + 1))
            qf[j] = q_ref[:, sl].astype(F32)
            kf[j] = kp_ref[:, sl].astype(F32)
            kf[nj + j] = kc_ref[:, sl].astype(F32)
            vf[j] = vp_ref[:, sl].astype(F32)
            vf[nj + j] = vc_ref[:, sl].astype(F32)
        for r in range(dil):
            rr = _residue(dil, r)
            for j in range(nj):
                q2 = qf.at[j][rr, :]
                k2 = jnp.concatenate([kf.at[j][rr, :], kf.at[nj + j][rr, :]], axis=0).astype(BF16)
                v2 = jnp.concatenate([vf.at[j][rr, :], vf.at[nj + j][rr, :]], axis=0).astype(BF16)
                qs = _stack_heads(q2, first).astype(BF16)
                s = lax.dot_general(qs, k2, NT, preferred_element_type=F32) * (HEAD_DIM ** -0.5)
                s = jnp.where(valid, s + b_ref[2 * j:2 * j + 2].reshape(2 * BAND, 2 * BAND), NEG_BIG)
                m = jnp.max(s, axis=1, keepdims=True)
                p = jnp.exp(s - m)
                l = jnp.sum(p, axis=1, keepdims=True)
                pv = jnp.dot(p.astype(BF16), v2, preferred_element_type=F32) / l
                lse = jnp.broadcast_to(m + jnp.log(l), (2 * BAND, 128))
                of.at[j][rr, :] = jnp.where(first, pv[:BAND], pv[BAND:])
                lf.at[j][rr, :] = jnp.where(first, lse[:BAND], lse[BAND:])
        for j in range(nj):
            sl = slice(128 * j, 128 * (j + 1))
            if merge:
                o0, l0, o1, l1 = (r[:, sl] for r in others)
                lc = lf[j]
                m = jnp.maximum(jnp.maximum(l0, l1), lc)
                wa, wb, wc_ = jnp.exp(l0 - m), jnp.exp(l1 - m), jnp.exp(lc - m)
                tot = wa + wb + wc_
                o_ref[:, sl] = (wa * o0 + wb * o1 + wc_ * of[j]) / tot
                l_ref[:, sl] = m + jnp.log(tot)
            else:
                o_ref[:, sl] = of[j]
                l_ref[:, sl] = lf[j]

    def body(q_ref, kc_ref, kp_ref, vc_ref, vp_ref, b_ref, *rest):
        others, rest = rest[:len(merge)], rest[len(merge):]
        attend(q_ref, kc_ref, kp_ref, vc_ref, vp_ref, b_ref, others, *rest)

    base = g * ncol
    cur = lambda c, b, ch: (b * nch + ch, base + c)
    prev = lambda c, b, ch: (b * nch + jnp.maximum(ch - 1, 0), base + c)
    blk = lambda im: pl.BlockSpec((rows, wc), im)
    out_blk = pl.BlockSpec((rows, wc), lambda c, b, ch: (b * nch + ch, c))
    return _call(
        body, "attn_fwd_d%d" % dil, (ncol, bsz, nch),
        [blk(cur), blk(cur), blk(prev), blk(cur), blk(prev),
         pl.BlockSpec((hpc, BAND, 2 * BAND), lambda c, b, ch: (c, 0, 0))] + [out_blk] * len(merge),
        [out_blk, out_blk],
        [_sds((n, w), F32)] * 2,
        [slab(1), slab(2), slab(2), slab(1), slab(1)], comm=comm,
    )(q, k, k, v, v, bias, *merge)


def _attn_bwd(q, k, v, do, o, lse, bias, g, dil, bsz, seq, comm=None, into=()):
    n = q.shape[0]
    rows = BAND * dil
    nch = seq // rows
    w = N_HEADS * HEAD_DIM
    wc = ATTN_COLS[dil]
    ncol = w // wc
    hpc = wc // HEAD_DIM
    nj = wc // 128
    slab = lambda k: pltpu.VMEM((k * nj, rows, 128), F32)
    scale = HEAD_DIM ** -0.5

    def body(q_ref, kc_ref, kp_ref, vc_ref, vp_ref, do_ref, o_ref, l_ref, b_ref,
             dq_ref, dk_ref, dv_ref, db_ref, qf, kf, vf, dof, of, lf, dqf, dkf, dvf):
        b = pl.program_id(1)
        ch = pl.program_id(2)
        cur = ch % 2
        prv = 1 - cur

        @pl.when(jnp.logical_and(b == 0, ch == 0))
        def _():
            db_ref[...] = jnp.zeros_like(db_ref)

        @pl.when(ch == 0)
        def _():
            for j in range(nj):
                dkf[nj + j] = jnp.zeros((rows, 128), F32)
                dvf[nj + j] = jnp.zeros((rows, 128), F32)

        @pl.when(ch < nch)
        def _():
            valid = _valid_mask(ch)
            valid = jnp.concatenate([valid, valid], axis=0)
            first = lax.broadcasted_iota(jnp.int32, (BAND, 128), 1) < HEAD_DIM
            for j in range(nj):
                sl = slice(128 * j, 128 * (j + 1))
                qf[j] = q_ref[:, sl].astype(F32)
                kf[j] = kp_ref[:, sl].astype(F32)
                kf[nj + j] = kc_ref[:, sl].astype(F32)
                vf[j] = vp_ref[:, sl].astype(F32)
                vf[nj + j] = vc_ref[:, sl].astype(F32)
                dof[j] = do_ref[:, sl]
                of[j] = o_ref[:, sl]
                lf[j] = l_ref[:, sl]
            for r in range(dil):
                rr = _residue(dil, r)
                for j in range(nj):
                    k2 = jnp.concatenate([kf.at[j][rr, :], kf.at[nj + j][rr, :]], axis=0).astype(BF16)
                    v2 = jnp.concatenate([vf.at[j][rr, :], vf.at[nj + j][rr, :]], axis=0).astype(BF16)
                    do2 = dof.at[j][rr, :]
                    lse2 = lf.at[j][rr, :]
                    qs = _stack_heads(qf.at[j][rr, :], first).astype(BF16)
                    dos = _stack_heads(do2, first)
                    delta = jnp.sum(dos * jnp.concatenate([of.at[j][rr, :]] * 2, axis=0), axis=1, keepdims=True)
                    dos = dos.astype(BF16)
                    lse_col = jnp.concatenate([lse2[:, 0:1], lse2[:, HEAD_DIM:HEAD_DIM + 1]], axis=0)
                    s = lax.dot_general(qs, k2, NT, preferred_element_type=F32) * scale
                    s = jnp.where(valid, s + b_ref[2 * j:2 * j + 2].reshape(2 * BAND, 2 * BAND), NEG_BIG)
                    p = jnp.exp(s - lse_col)
                    dp = lax.dot_general(dos, v2, NT, preferred_element_type=F32)
                    ds = p * (dp - delta)
                    db_ref[2 * j:2 * j + 2] += ds.reshape(2, BAND, 2 * BAND)
                    dsb = ds.astype(BF16)
                    dq2 = jnp.dot(dsb, k2, preferred_element_type=F32) * scale
                    dk2 = lax.dot_general(dsb, qs, TN, preferred_element_type=F32) * scale
                    dv2 = lax.dot_general(p.astype(BF16), dos, TN, preferred_element_type=F32)
                    dqf.at[j][rr, :] = jnp.where(first, dq2[:BAND], dq2[BAND:])
                    dkf.at[prv * nj + j][rr, :] += dk2[:BAND, :]
                    dvf.at[prv * nj + j][rr, :] += dv2[:BAND, :]
                    dkf.at[cur * nj + j][rr, :] = dk2[BAND:, :]
                    dvf.at[cur * nj + j][rr, :] = dv2[BAND:, :]
            for j in range(nj):
                dq_ref[:, 128 * j:128 * (j + 1)] = dqf[j].astype(BF16)

        for j in range(nj):
            sl = slice(128 * j, 128 * (j + 1))
            dk_ref[:, sl] = dkf[prv * nj + j].astype(BF16)
            dv_ref[:, sl] = dvf[prv * nj + j].astype(BF16)

    last = nch - 1
    base = g * ncol
    cur3 = lambda c, b, ch: (b * nch + jnp.minimum(ch, last), base + c)
    prev3 = lambda c, b, ch: (b * nch + jnp.maximum(jnp.minimum(ch, last) - 1, 0), base + c)
    cur1 = lambda c, b, ch: (b * nch + jnp.minimum(ch, last), c)
    lag3 = lambda c, b, ch: (b * nch + jnp.maximum(ch - 1, 0), base + c)
    blk = lambda im: pl.BlockSpec((rows, wc), im)
    bias_blk = pl.BlockSpec((hpc, BAND, 2 * BAND), lambda c, b, ch: (c, 0, 0))
    out = _sds(q.shape, BF16)
    return _call(
        body, "attn_bwd_d%d" % dil, (ncol, bsz, nch + 1),
        [blk(cur3), blk(cur3), blk(prev3), blk(cur3), blk(prev3), blk(cur1), blk(cur1), blk(cur1), bias_blk],
        [blk(cur3), blk(lag3), blk(lag3), bias_blk],
        [out, out, out, _sds((N_HEADS, BAND, 2 * BAND), F32)],
        [slab(1), slab(2), slab(2), slab(1), slab(1), slab(1), slab(1), slab(2), slab(2)], comm=comm, fill=into,
    )(q, k, k, v, v, do, o, lse, bias)


def _adamw(name, parts, w, m, v):
    nl, r, c = w.shape
    nparts = parts[0].shape[0]
    tr = r
    for cand in (256, 352, 128):
        if r % cand == 0 and r > cand:
            tr = cand
            break
    c1 = 1.0 - ADAM_B1 ** ADAM_STEP
    c2 = 1.0 - ADAM_B2 ** ADAM_STEP

    def body(*refs):
        p_refs = refs[:nl]
        w_ref, m_ref, v_ref, g_ref, d_ref, nm_ref, nv_ref = refs[nl:]
        layer = pl.program_id(0)
        for l in range(nl):
            @pl.when(layer == l)
            def _():
                g = p_refs[l][0].astype(F32)
                for dev in range(1, nparts):
                    g = g + p_refs[l][dev].astype(F32)
                nm = ADAM_B1 * m_ref[...] + (1.0 - ADAM_B1) * g
                nv = ADAM_B2 * v_ref[...] + (1.0 - ADAM_B2) * (g * g)
                m_hat = nm / c1
                v_hat = nv / c2
                g_ref[...] = g
                d_ref[...] = -ADAM_LR * (m_hat / (jnp.sqrt(v_hat) + ADAM_EPS) + ADAM_WD * w_ref[...])
                nm_ref[...] = nm
                nv_ref[...] = nv

    def part_spec(l):
        return pl.BlockSpec((nparts, tr, c), lambda layer, i: (0, jnp.where(layer == l, i, 0), 0))

    row = pl.BlockSpec((None, tr, c), lambda layer, i: (layer, i, 0))
    return _call(body, name, (nl, r // tr), [part_spec(l) for l in range(nl)] + [row, row, row],
                 [row] * 4, [_sds((nl, r, c), F32)] * 4)(*parts, w, m, v)


def _exchange(name, srcs, out_shapes, items):
    n_in = len(srcs)
    n_out = len(out_shapes)
    n_items = len(items)

    def body(*refs):
        ins = refs[:n_in]
        outs = refs[n_in:n_in + n_out]
        send_sems, recv_sems, local_sems = refs[n_in + n_out:]
        x, y, c = lax.axis_index("x"), lax.axis_index("y"), lax.axis_index("c")
        me = 4 * x + 2 * y + c

        def pick(ref, sel, peer):
            idx = tuple(peer if s == "peer" else me if s == "me" else s for s in sel)
            return ref.at[idx] if idx else ref

        copies = []
        for it, (si, ssel, oi, osel) in enumerate(items):
            local = pltpu.make_async_copy(pick(ins[si], ssel, me), pick(outs[oi], osel, me), local_sems.at[it])
            local.start()
            copies.append(local)
            for j in range(1, N_DEV):
                px = 1 - x if j & 4 else x
                py = 1 - y if j & 2 else y
                pc = 1 - c if j & 1 else c
                peer = 4 * px + 2 * py + pc
                sem = it * (N_DEV - 1) + j - 1
                cp = pltpu.make_async_remote_copy(
                    src_ref=pick(ins[si], ssel, peer), dst_ref=pick(outs[oi], osel, peer),
                    send_sem=send_sems.at[sem], recv_sem=recv_sems.at[sem],
                    device_id=(px, py, pc), device_id_type=pl.DeviceIdType.MESH)
                cp.start()
                copies.append(cp)
        for cp in copies:
            cp.wait()

    any_spec = pl.BlockSpec(memory_space=pl.ANY)
    return pl.pallas_call(
        body, name=name, in_specs=[any_spec] * n_in, out_specs=[any_spec] * n_out, out_shape=list(out_shapes),
        scratch_shapes=[pltpu.SemaphoreType.DMA((n_items * (N_DEV - 1),)),
                        pltpu.SemaphoreType.DMA((n_items * (N_DEV - 1),)),
                        pltpu.SemaphoreType.DMA((n_items,))],
        compiler_params=pltpu.CompilerParams(has_side_effects=True),
    )(*srcs)


def _sel(ref, sel, slot=None):
    idx = tuple(slot if s == "slot" else s for s in sel)
    return ref.at[idx] if idx else ref


def _gather_plan(srcs, out_shapes, items):
    per = N_DEV - 1
    n_items = len(items)

    def phases(ins, outs, send_sems, recv_sems, local_sems, total):
        x, y, c = lax.axis_index("x"), lax.axis_index("y"), lax.axis_index("c")
        sibling = (x, y, 1 - c)
        chips = [(1 - x, y), (x, 1 - y), (1 - x, 1 - y)]
        slot_of = lambda px, py, pc: 4 * px + 2 * py + pc

        def copy(it, k, block, to, src=None):
            _, _, oi, osel = items[it]
            dst = _sel(outs[oi], osel, slot_of(*block))
            return pltpu.make_async_remote_copy(
                src_ref=dst if src is None else src, dst_ref=dst,
                send_sem=send_sems.at[it * per + k], recv_sem=recv_sems.at[it * per + k],
                device_id=to, device_id_type=pl.DeviceIdType.MESH)

        def own_copies(it):
            si, ssel, oi, osel = items[it]
            src = _sel(ins[si], ssel)
            mine = pltpu.make_async_copy(src, _sel(outs[oi], osel, slot_of(x, y, c)), local_sems.at[it])
            first = [copy(it, 0, (x, y, c), sibling, src=src)]
            first += [copy(it, 1 + j, (x, y, c), (*chip, c), src=src) for j, chip in enumerate(chips)]
            return mine, first

        def start():
            for it in range(n_items):
                mine, first = own_copies(it)
                mine.start()
                for cp in first:
                    cp.start()

        def forward(it):
            def go():
                for j, chip in enumerate(chips):
                    copy(it, 1 + j, (*chip, c), (x, y, c)).wait_recv()
                    copy(it, 4 + j, (*chip, c), sibling).start()
            return go

        def finish():
            for it in range(n_items):
                copy(it, 0, sibling, (x, y, c)).wait_recv()
                for j, chip in enumerate(chips):
                    copy(it, 4 + j, (*chip, 1 - c), (x, y, c)).wait_recv()
            for it in range(n_items):
                mine, first = own_copies(it)
                for cp in first:
                    cp.wait_send()
                for j, chip in enumerate(chips):
                    copy(it, 4 + j, (*chip, c), sibling).wait_send()
                mine.wait()

        lo = max(3 * total // 4, 1)
        acts = [(0, start)]
        for it in range(n_items):
            acts.append((min(lo + it * (total - 1 - lo) // n_items, total - 2), forward(it)))
        acts.append((total - 1, finish))
        return acts

    return _Plan(srcs, out_shapes, n_items * per, n_items, phases)


def _scatter_plan(srcs, out_shapes, items):
    per = N_DEV - 1
    n_items = len(items)

    def phases(ins, outs, send_sems, recv_sems, local_sems, total):
        x, y, c = lax.axis_index("x"), lax.axis_index("y"), lax.axis_index("c")
        me = 4 * x + 2 * y + c

        def copies():
            out = []
            for it, (si, oi, osel) in enumerate(items):
                dst = _sel(outs[oi], osel, me)
                out.append(pltpu.make_async_copy(ins[si].at[me], dst, local_sems.at[it]))
                for j in range(1, N_DEV):
                    px = 1 - x if j & 4 else x
                    py = 1 - y if j & 2 else y
                    pc = 1 - c if j & 1 else c
                    out.append(pltpu.make_async_remote_copy(
                        src_ref=ins[si].at[4 * px + 2 * py + pc], dst_ref=dst,
                        send_sem=send_sems.at[it * per + j - 1], recv_sem=recv_sems.at[it * per + j - 1],
                        device_id=(px, py, pc), device_id_type=pl.DeviceIdType.MESH))
            return out

        def start():
            for cp in copies():
                cp.start()

        def finish():
            for cp in copies():
                cp.wait()

        return [(0, start), (total - 1, finish)]

    return _Plan(srcs, out_shapes, n_items * per, n_items, phases)


def _sum_slots(parts):
    _, r, c = parts.shape

    def body(p_ref, o_ref):
        acc = p_ref[0]
        for dev in range(1, N_DEV):
            acc = acc + p_ref[dev]
        o_ref[...] = acc

    return _call(body, "sum_slots", (1,), [pl.BlockSpec((N_DEV, r, c), lambda i: (0, 0, 0))],
                 pl.BlockSpec((r, c), lambda i: (0, 0)), _sds((r, c), F32))(parts)


def _pack(arrays):
    rows = []
    for a in arrays:
        flat = a.reshape(-1).astype(F32)
        pad = (-flat.shape[0]) % 1024
        if pad:
            flat = jnp.concatenate([flat, jnp.zeros((pad,), F32)])
        rows.append(flat.reshape(-1, 128))
    return jnp.concatenate(rows, axis=0)


def _unpack(buf, shapes):
    out = []
    r0 = 0
    for shp in shapes:
        size = int(np.prod(shp))
        nrows = -(-size // 1024) * 8
        out.append(buf[r0:r0 + nrows].reshape(-1)[:size].reshape(shp))
        r0 += nrows
    return out


def _block_diag_chunks(val):
    eye = jnp.eye(8, dtype=val.dtype)
    return jnp.einsum("cjhp,jk->cjhkp", val, eye).reshape(SSM_CHUNKS, CHUNK_CH, CHUNK_ST)


def _diag_blocks(dense):
    d5 = dense.reshape(SSM_CHUNKS, 8, SSM_GROUP, 8, SSM_STATE)
    return jnp.stack([d5[:, j, :, j, :] for j in range(8)], axis=1)


def _carried(res, comm):
    return res if comm is not None else (res, None)


def _ffn_fwd(tag, hin, hin_b, w_up, conv_w, conv_b, w_down4, gain, bias, seq, comm_up=None, comm_conv=None,
             comm_down=None):
    hc, got_up = _carried(_mm_nn("ffn_up" + tag, hin_b, w_up, FFN_HIDDEN, blocked_out=True, comm=comm_up), comm_up)
    nb, n, c = hc.shape
    hc = hc.reshape(2, nb // 2, n, c)
    (act, cv), got_conv = _carried(_conv_gate_fwd(hc, conv_w, conv_b, seq, comm=comm_conv), comm_conv)
    (r, hout, hout_b), got_down = _carried(
        _mm_nn_red("ffn_down" + tag, act, w_down4, F32, comm=comm_down, residual=hin, norm=(gain, bias)), comm_down)
    return hout, hout_b, (hin_b, hc, cv, act, r), (got_up, got_conv, got_down)


def _ffn_bwd(tag, saved, dh, w_up, conv_w, conv_b, w_down4, gain, seq, comm_conv=None, comm_dw=None,
             comm_dx=None):
    hin_b, hc, cv, act, r = saved
    dr, dr_b, dgain, dbias = _ln_bwd("ffn_norm_bwd" + tag, r, dh, gain)
    d_wdown = _mm_tn("ffn_down_dw" + tag, act, dr_b, act.shape[0], BF16, a_blocked=True, g_mode="shared")
    dact = _mm_nt_out("ffn_down_dx" + tag, dr_b, w_down4, FFN_HIDDEN)
    (dhc, dconv), got_conv = _carried(_conv_gate_bwd(hc, cv, dact, conv_w, seq, comm=comm_conv), comm_conv)
    dhc8 = dhc.reshape(2 * dhc.shape[1], dhc.shape[2], dhc.shape[3])
    d_wup, got_dw = _carried(_mm_tn("ffn_up_dw" + tag, dhc8, hin_b, dhc8.shape[0], BF16, a_blocked=True,
                                    g_mode="shared", comm=comm_dw), comm_dw)
    dhin, got_dx = _carried(_mm_nt_red("ffn_up_dx" + tag, dhc8, w_up, F32, g_blocked=True, add=dr,
                                       add_scale=DN_ALPHA, comm=comm_dx), comm_dx)
    return dhin, d_wup, d_wdown, dconv, dgain, dbias, (got_conv, got_dw, got_dx)


def _local_step(x, target, p, wfull, bsz, seq, ex=None):
    n = bsz * seq
    xf = x.reshape(n, D_MODEL)
    tf = target.reshape(n, D_MODEL)

    lr = p["s5_lam_re"].reshape(1, N_STATE)
    li = p["s5_lam_im"].reshape(1, N_STATE)
    ldt = p["s5_log_dt"].reshape(1, D_MODEL // SSM_GROUP)
    br_t = p["s5_b_re"].reshape(N_STATE, SSM_GROUP).T
    bi_t = p["s5_b_im"].reshape(N_STATE, SSM_GROUP).T
    ab_r, ab_i, bb_r, bb_i, tab = _s5_prep(lr, li, ldt, br_t, bi_t)

    def bb_chunks(bb):
        return _block_diag_chunks(bb.reshape(SSM_GROUP, SSM_CHUNKS, 8, SSM_STATE).transpose(1, 2, 0, 3))

    def c_chunks(cc):
        return _block_diag_chunks(cc.reshape(SSM_CHUNKS, 8, SSM_GROUP, SSM_STATE))

    bbr_c, bbi_c = bb_chunks(bb_r).astype(BF16), bb_chunks(bb_i).astype(BF16)
    cr_c, ci_c = c_chunks(p["s5_c_re"]).astype(BF16), c_chunks(p["s5_c_im"]).astype(BF16)
    dvec = p["s5_d"].reshape(1, D_MODEL)

    if ex is None:
        y, h_r, h_i = _s5_scan_fwd(xf, bbr_c, bbi_c, cr_c, ci_c, dvec, tab, seq)
    else:
        (y, h_r, h_i), gathered = _s5_scan_fwd(xf, bbr_c, bbi_c, cr_c, ci_c, dvec, tab, seq,
                                               comm=ex.gather_first())
        p, wfull = ex.take_first(gathered, p)
    ln_g = p["ln_gain"].reshape(4, 1, D_MODEL)
    ln_b = p["ln_bias"].reshape(4, 1, D_MODEL)
    y2, z, gg = _glu_fwd(y, wfull["s5_w_glu"], p["s5_b_glu"].reshape(1, D_MODEL))
    r1, h1, h1b = _mm_nn("s5_out", gg, wfull["s5_w_out"][None], F32, residual=xf, norm=(ln_g[0], ln_b[0]))
    plans = [ex.gather_attn(k) if ex else None for k in ("kv0", "q", "kv1")]
    h2, h2b, ffn0_saved, got = _ffn_fwd(
        "0", h1, h1b, wfull["ffn_w_up"][0], p["conv_w"][0], p["conv_b"][0], wfull["ffn_w_down"][0],
        ln_g[1], ln_b[1], seq, comm_up=plans[0], comm_conv=plans[1], comm_down=plans[2])
    if ex is not None:
        wfull = ex.take_attn({"kv0": got[0][0], "q": got[1][0], "kv1": got[2][0]}, wfull)

    plan = ex.gather_layer1("up", 0) if ex else None
    kmat, got = _carried(_mm_nn("attn_k", h2b, wfull["attn_w_kv"], BF16, blocks=(0, 4), comm=plan), plan)
    halves = {"up": [got[0]], "down": []} if ex else None
    vmat = _mm_nn("attn_v", h2b, wfull["attn_w_kv"], BF16, blocks=(4, 4))
    plan = ex.gather_attn("out") if ex else None
    qmat, got = _carried(_mm_nn("attn_q", h2b, wfull["attn_w_q"], BF16, comm=plan), plan)
    if ex is not None:
        wfull = ex.take_attn({"out": got[0]}, wfull)
    biases, outs, lses = [], [], []
    for g, dil in enumerate(DILATIONS):
        rel_t = p["rel_bias"][:, g * N_HEADS:(g + 1) * N_HEADS].T
        biases.append(_bias_expand(rel_t, dil).reshape(N_HEADS, BAND, 2 * BAND))
        which, half = {1: ("down", 0), 4: ("down", 1), 16: ("up", 1)}[dil]
        plan = ex.gather_layer1(which, half) if ex else None
        merge = (outs[0], lses[0], outs[1], lses[1]) if dil == DILATIONS[-1] else ()
        (o_g, l_g), got = _carried(_attn_fwd(qmat, kmat, vmat, biases[g], g, dil, bsz, seq, comm=plan,
                                             merge=merge), plan)
        if plan is not None:
            halves[which].append(got[0])
        outs.append(o_g)
        lses.append(l_g)
    if ex is not None:
        wfull = ex.take_layer1(halves, wfull)
    o, lse = outs[-1], lses[-1]
    r3, h3, h3b = _mm_nn("attn_out", o, wfull["attn_w_out"][None], F32, residual=h2, norm=(ln_g[2], ln_b[2]))
    h4, _, ffn1_saved, _ = _ffn_fwd("1", h3, h3b, wfull["ffn_w_up"][1], p["conv_w"][1], p["conv_b"][1],
                                    wfull["ffn_w_down"][1], ln_g[3], ln_b[3], seq)
    loss_tile, dh4 = _loss_head(h4, tf)

    grads = {}
    dh3, grads["ffn_w_up1"], grads["ffn_w_down1"], dconv1, dg3, db3, _ = _ffn_bwd(
        "1", ffn1_saved, dh4, wfull["ffn_w_up"][1], p["conv_w"][1], p["conv_b"][1], wfull["ffn_w_down"][1],
        ln_g[3], seq)
    dr3, dr3b, dg2, db2 = _ln_bwd("attn_norm_bwd", r3, dh3, ln_g[2])
    grads["attn_w_out"] = _mm_tn("attn_out_dw", o, dr3b, 1, BF16, g_mode="shared")
    do = _mm_nt_red("attn_out_dx", dr3b, wfull["attn_w_out"][None], F32, g_blocked=False)
    dqkv, drel = (), []
    for g, dil in enumerate(DILATIONS):
        plan = ex.scatter(grads, ("ffn_w_up1", "ffn_w_down1")) if ex and dil == DILATIONS[-1] else None
        res, got = _carried(_attn_bwd(qmat, kmat, vmat, do, o, lse, biases[g], g, dil, bsz, seq, comm=plan,
                                      into=dqkv), plan)
        if plan is not None:
            ex.keep(plan, got)
        dqkv, db_g = tuple(res[:3]), res[3]
        drel.append(_bias_reduce(db_g.reshape(N_HEADS, BAND * 2 * BAND), dil).T)
    dq, dk, dv = dqkv
    grads["rel_bias"] = jnp.concatenate(drel, axis=1)
    grads["attn_w_q"] = _mm_tn("attn_q_dw", h2b, dq, N_DEV, BF16)
    grads["attn_w_kv"] = jnp.concatenate([_mm_tn("attn_k_dw", h2b, dk, 4, BF16),
                                          _mm_tn("attn_v_dw", h2b, dv, 4, BF16)], axis=0)
    dh2 = _mm_nt_red("attn_q_dx", dq, wfull["attn_w_q"], F32, g_blocked=False, add=dr3, add_scale=DN_ALPHA)
    dh2 = _mm_nt_red("attn_k_dx", dk, wfull["attn_w_kv"], F32, g_blocked=False, add=dh2, blocks=(0, 4))
    dh2 = _mm_nt_red("attn_v_dx", dv, wfull["attn_w_kv"], F32, g_blocked=False, add=dh2, blocks=(4, 4))

    plans = [ex.scatter(grads, names) if ex else None
             for names in (("attn_w_kv",), ("attn_w_out",), ("attn_w_q",))]
    dh1, grads["ffn_w_up0"], grads["ffn_w_down0"], dconv0, dg1, db1, got = _ffn_bwd(
        "0", ffn0_saved, dh2, wfull["ffn_w_up"][0], p["conv_w"][0], p["conv_b"][0], wfull["ffn_w_down"][0],
        ln_g[1], seq, comm_conv=plans[0], comm_dw=plans[1], comm_dx=plans[2])
    if ex is not None:
        for plan, outs in zip(plans, got):
            ex.keep(plan, outs)
    dr1, dr1b, dg0, db0 = _ln_bwd("s5_norm_bwd", r1, dh1, ln_g[0])
    grads["s5_w_out"] = _mm_tn("s5_out_dw", gg, dr1b, 1, BF16, g_mode="shared")
    dgg = _mm_nt_red("s5_out_dx", dr1b, wfull["s5_w_out"][None], F32, g_blocked=False)
    dz, d_bglu, dy2_direct = _glu_bwd_gate(y, z, dgg)
    grads["s5_w_glu"] = _mm_tn("s5_glu_dw", y2, dz, 1, BF16, g_mode="shared")
    dy = _mm_nt_red("s5_glu_dx", dz, wfull["s5_w_glu"][None], F32, g_blocked=False, add=dy2_direct,
                    post=(_gelu_bwd, y))
    plan_last = ex.scatter(grads, ("ffn_w_up0", "ffn_w_down0", "s5_w_out", "s5_w_glu")) if ex else None
    res = _s5_scan_bwd(xf, dy, dr1, h_r, h_i, bbr_c, bbi_c, cr_c, ci_c, dvec, tab, seq, comm=plan_last)
    if ex is not None:
        res, got = res
        ex.keep(plan_last, got)
    dx, dcr, dci, dbr, dbi, dar, dai, dd = res

    def bb_from_chunks(dense):
        return _diag_blocks(dense).transpose(2, 0, 1, 3).reshape(SSM_GROUP, N_STATE)

    d_lr, d_li, d_ldt, d_br, d_bi = _s5_prep_bwd(lr, li, ldt, br_t, bi_t, dar, dai,
                                                 bb_from_chunks(dbr), bb_from_chunks(dbi))
    grads["s5_lam_re"] = d_lr.reshape(p["s5_lam_re"].shape)
    grads["s5_lam_im"] = d_li.reshape(p["s5_lam_im"].shape)
    grads["s5_log_dt"] = d_ldt.reshape(p["s5_log_dt"].shape)
    grads["s5_b_re"] = d_br.T.reshape(p["s5_b_re"].shape)
    grads["s5_b_im"] = d_bi.T.reshape(p["s5_b_im"].shape)
    grads["s5_c_re"] = _diag_blocks(dcr).reshape(p["s5_c_re"].shape)
    grads["s5_c_im"] = -_diag_blocks(dci).reshape(p["s5_c_im"].shape)
    grads["s5_d"] = dd.reshape(p["s5_d"].shape)
    grads["s5_b_glu"] = d_bglu.reshape(1, D_MODEL)
    dconv = jnp.stack([dconv0, dconv1]).reshape(2, N_DEV, 8, -1)
    grads["ffn_conv_w"] = dconv[:, :, 0:3, :].transpose(0, 2, 1, 3)
    grads["ffn_conv_b"] = dconv[:, :, 3, :]
    grads["ln_gain"] = jnp.stack([dg0, dg1, dg2, dg3]).reshape(2, 2, D_MODEL)
    grads["ln_bias"] = jnp.stack([db0, db1, db2, db3]).reshape(2, 2, D_MODEL)
    return loss_tile[0, 0], dx.reshape(x.shape), grads


SMALL_REPLICATED = ("s5_lam_re", "s5_lam_im", "s5_log_dt", "s5_b_re", "s5_b_im", "s5_c_re", "s5_c_im", "s5_d",
                    "rel_bias", "ffn_conv_b")
SMALL_SHARDED = ("s5_b_glu", "ffn_conv_w", "ln_gain", "ln_bias")
BIG = ("s5_w_glu", "s5_w_out", "attn_w_kv", "attn_w_q", "attn_w_out", "ffn_w_up", "ffn_w_down")
WEIGHTS = ("s5_lam_re", "s5_lam_im", "s5_log_dt", "s5_b_re", "s5_b_im", "s5_c_re", "s5_c_im", "s5_d", "s5_w_glu",
           "s5_b_glu", "s5_w_out", "attn_w_kv", "attn_w_q", "attn_w_out", "rel_bias", "ffn_w_up", "ffn_conv_w",
           "ffn_conv_b", "ffn_w_down", "ln_gain", "ln_bias")


class _Exchanges:
    def __init__(self, w):
        self.w = w
        self.parts = {}
        self.up = w["ffn_w_up"].astype(BF16)
        self.down = w["ffn_w_down"].astype(BF16)

    def gather_first(self):
        w = self.w
        srcs = [w["s5_w_glu"][0].astype(BF16), w["s5_w_out"][0].astype(BF16),
                _pack([w[k] for k in SMALL_SHARDED]), self.up, self.down]
        outs = [_sds((N_DEV,) + (s.shape[1:] if i in (3, 4) else s.shape), s.dtype) for i, s in enumerate(srcs)]
        items = [(i, (0,) if i in (3, 4) else (), i, ("slot",)) for i in range(len(srcs))]
        return _gather_plan(srcs, outs, items)

    def gather_attn(self, piece):
        w = self.w
        if piece in ("kv0", "kv1"):
            src, rows = w["attn_w_kv"].astype(BF16), w["attn_w_kv"].shape[0] // 2
            return _gather_plan([src], [_sds((N_DEV, rows, src.shape[1]), BF16)],
                                [(0, (pl.ds(int(piece[2]) * rows, rows),), 0, ("slot",))])
        src = (w["attn_w_q"] if piece == "q" else w["attn_w_out"])[0].astype(BF16)
        return _gather_plan([src], [_sds((N_DEV,) + src.shape, BF16)], [(0, (), 0, ("slot",))])

    def take_attn(self, pieces, wfull):
        wfull = dict(wfull)
        if "kv0" in pieces:
            wfull["attn_w_kv"] = jnp.concatenate([pieces["kv0"], pieces["kv1"]], axis=1)
            wfull["attn_w_q"] = pieces["q"]
        if "out" in pieces:
            wfull["attn_w_out"] = pieces["out"].reshape(D_MODEL, D_MODEL)
        return wfull

    def take_first(self, g, p):
        d = D_MODEL
        w = self.w
        wfull = {
            "s5_w_glu": g[0].reshape(d, d),
            "s5_w_out": g[1].reshape(d, d),
            "ffn_w_up": [g[3], None],
            "ffn_w_down": [g[4].reshape(4, -1, d), None],
        }
        smalls = [_unpack(g[2][dev], [w[k].shape for k in SMALL_SHARDED]) for dev in range(N_DEV)]
        c_ff = w["ffn_conv_w"].shape[2]
        conv_w = jnp.stack([s[1] for s in smalls], axis=2)
        p = dict(p)
        p.update(s5_b_glu=jnp.concatenate([s[0] for s in smalls], axis=1),
                 ln_gain=jnp.concatenate([s[2] for s in smalls], axis=2),
                 ln_bias=jnp.concatenate([s[3] for s in smalls], axis=2),
                 conv_w=conv_w.transpose(0, 2, 1, 3).reshape(2, 2, 4, 3, c_ff),
                 conv_b=w["ffn_conv_b"].reshape(2, 2, 4, 1, c_ff))
        return p, wfull

    def gather_layer1(self, which, half):
        src = self.up if which == "up" else self.down
        rows = src.shape[1] // 2
        return _gather_plan([src], [_sds((N_DEV, rows) + src.shape[2:], BF16)],
                            [(0, (1, pl.ds(half * rows, rows)), 0, ("slot",))])

    def take_layer1(self, halves, wfull):
        up = jnp.concatenate(halves["up"], axis=1)
        down = jnp.concatenate(halves["down"], axis=1)
        wfull = dict(wfull)
        wfull["ffn_w_up"] = [wfull["ffn_w_up"][0], up]
        wfull["ffn_w_down"] = [wfull["ffn_w_down"][0], down.reshape(4, -1, D_MODEL)]
        return wfull

    def scatter(self, grads, names):
        srcs = [grads[k].reshape(N_DEV, -1, grads[k].shape[-1]) for k in names]
        plan = _scatter_plan(srcs, [_sds(s.shape, BF16) for s in srcs],
                             [(i, i, ("slot",)) for i in range(len(names))])
        plan.names = names
        return plan

    def keep(self, plan, outs):
        for k, o in zip(plan.names, outs):
            self.parts[k] = o


def kernel(x, s5_lam_re, s5_lam_im, s5_log_dt, s5_b_re, s5_b_im, s5_c_re, s5_c_im, s5_d, s5_w_glu, s5_b_glu, s5_w_out, attn_w_kv, attn_w_q, attn_w_out, rel_bias, ffn_w_up, ffn_conv_w, ffn_conv_b, ffn_w_down, ln_gain, ln_bias, loss_target, m_s5_lam_re, m_s5_lam_im, m_s5_log_dt, m_s5_b_re, m_s5_b_im, m_s5_c_re, m_s5_c_im, m_s5_d, m_s5_w_glu, m_s5_b_glu, m_s5_w_out, m_attn_w_kv, m_attn_w_q, m_attn_w_out, m_rel_bias, m_ffn_w_up, m_ffn_conv_w, m_ffn_conv_b, m_ffn_w_down, m_ln_gain, m_ln_bias, v_s5_lam_re, v_s5_lam_im, v_s5_log_dt, v_s5_b_re, v_s5_b_im, v_s5_c_re, v_s5_c_im, v_s5_d, v_s5_w_glu, v_s5_b_glu, v_s5_w_out, v_attn_w_kv, v_attn_w_q, v_attn_w_out, v_rel_bias, v_ffn_w_up, v_ffn_conv_w, v_ffn_conv_b, v_ffn_w_down, v_ln_gain, v_ln_bias):
    w = dict(s5_lam_re=s5_lam_re, s5_lam_im=s5_lam_im, s5_log_dt=s5_log_dt, s5_b_re=s5_b_re, s5_b_im=s5_b_im,
             s5_c_re=s5_c_re, s5_c_im=s5_c_im, s5_d=s5_d, s5_w_glu=s5_w_glu, s5_b_glu=s5_b_glu, s5_w_out=s5_w_out,
             attn_w_kv=attn_w_kv, attn_w_q=attn_w_q, attn_w_out=attn_w_out, rel_bias=rel_bias, ffn_w_up=ffn_w_up,
             ffn_conv_w=ffn_conv_w, ffn_conv_b=ffn_conv_b, ffn_w_down=ffn_w_down, ln_gain=ln_gain, ln_bias=ln_bias)
    mom = dict(s5_lam_re=m_s5_lam_re, s5_lam_im=m_s5_lam_im, s5_log_dt=m_s5_log_dt, s5_b_re=m_s5_b_re,
               s5_b_im=m_s5_b_im, s5_c_re=m_s5_c_re, s5_c_im=m_s5_c_im, s5_d=m_s5_d, s5_w_glu=m_s5_w_glu,
               s5_b_glu=m_s5_b_glu, s5_w_out=m_s5_w_out, attn_w_kv=m_attn_w_kv, attn_w_q=m_attn_w_q,
               attn_w_out=m_attn_w_out, rel_bias=m_rel_bias, ffn_w_up=m_ffn_w_up, ffn_conv_w=m_ffn_conv_w,
               ffn_conv_b=m_ffn_conv_b, ffn_w_down=m_ffn_w_down, ln_gain=m_ln_gain, ln_bias=m_ln_bias)
    var = dict(s5_lam_re=v_s5_lam_re, s5_lam_im=v_s5_lam_im, s5_log_dt=v_s5_log_dt, s5_b_re=v_s5_b_re,
               s5_b_im=v_s5_b_im, s5_c_re=v_s5_c_re, s5_c_im=v_s5_c_im, s5_d=v_s5_d, s5_w_glu=v_s5_w_glu,
               s5_b_glu=v_s5_b_glu, s5_w_out=v_s5_w_out, attn_w_kv=v_attn_w_kv, attn_w_q=v_attn_w_q,
               attn_w_out=v_attn_w_out, rel_bias=v_rel_bias, ffn_w_up=v_ffn_w_up, ffn_conv_w=v_ffn_conv_w,
               ffn_conv_b=v_ffn_conv_b, ffn_w_down=v_ffn_w_down, ln_gain=v_ln_gain, ln_bias=v_ln_bias)
    bsz, seq, _ = x.shape
    me = 4 * lax.axis_index("x") + 2 * lax.axis_index("y") + lax.axis_index("c")

    ex = _Exchanges(w)
    loss_part, grad_x, g = _local_step(x, loss_target, dict(w), None, bsz, seq, ex=ex)
    loss = lax.psum(loss_part, ("x", "y", "c"))

    d = D_MODEL
    results = {}
    for name in BIG:
        shard = w[name]
        if name == "ffn_w_up":
            parts = [ex.parts[name + "0"], ex.parts[name + "1"]]
            outs4 = _adamw("adamw_" + name, parts, *(jnp.swapaxes(a, 1, 2) for a in (shard, mom[name], var[name])))
            results[name] = tuple(jnp.swapaxes(o, 1, 2) for o in outs4)
            continue
        if name == "ffn_w_down":
            parts, r3 = [ex.parts[name + "0"], ex.parts[name + "1"]], shard.shape
        else:
            parts, r3 = [ex.parts[name]], (1, -1, shard.shape[-1])
        outs4 = _adamw("adamw_" + name, parts, shard.reshape(r3), mom[name].reshape(r3), var[name].reshape(r3))
        results[name] = tuple(o.reshape(shard.shape) for o in outs4)

    small_names = SMALL_REPLICATED + SMALL_SHARDED
    packed = _pack([g[k] for k in small_names])
    pad = (-packed.shape[0]) % (8 * N_DEV)
    if pad:
        packed = jnp.concatenate([packed, jnp.zeros((pad, 128), F32)], axis=0)
    piece = packed.shape[0] // N_DEV
    slots = _exchange("reduce_small", [packed.reshape(N_DEV, piece, 128)], [_sds((N_DEV, piece, 128), F32)],
                      [(0, ("peer",), 0, ("me",))])[0]
    summed = _exchange("gather_small", [_sum_slots(slots)], [_sds((N_DEV, piece, 128), F32)],
                       [(0, (), 0, ("me",))])[0]
    total = _unpack(summed.reshape(N_DEV * piece, 128), [g[k].shape for k in small_names])
    gsum = dict(zip(small_names, total))
    mine = {k: gsum[k] for k in SMALL_REPLICATED}
    mine["s5_b_glu"] = lax.dynamic_slice_in_dim(gsum["s5_b_glu"], me * (d // N_DEV), d // N_DEV, axis=1)
    mine["ffn_conv_w"] = lax.dynamic_index_in_dim(gsum["ffn_conv_w"], me, axis=2, keepdims=False)
    mine["ln_gain"] = lax.dynamic_slice_in_dim(gsum["ln_gain"], me * (d // N_DEV), d // N_DEV, axis=2)
    mine["ln_bias"] = lax.dynamic_slice_in_dim(gsum["ln_bias"], me * (d // N_DEV), d // N_DEV, axis=2)
    gp = _pack([mine[k] for k in small_names])
    outs4 = _adamw("adamw_small", [gp[None]], _pack([w[k] for k in small_names])[None],
                   _pack([mom[k] for k in small_names])[None], _pack([var[k] for k in small_names])[None])
    outs4 = [o[0] for o in outs4]
    shapes = [w[k].shape for k in small_names]
    unpacked = [_unpack(o, shapes) for o in outs4]
    for i, k in enumerate(small_names):
        results[k] = tuple(unpacked[j][i] for j in range(4))

    out = [loss, grad_x]
    for j in range(4):
        out.extend(results[k][j] for k in WEIGHTS)
    return tuple(out)
```

```python
import math

import numpy as np
import jax
import jax.numpy as jnp
from jax import lax
from jax.experimental import pallas as pl
from jax.experimental.pallas import tpu as pltpu

F32 = jnp.float32
BF16 = jnp.bfloat16
HIGHEST = lax.Precision.HIGHEST

N_DEV = 8
D_MODEL = 1024
SSM_GROUP = 16
SSM_STATE = 64
SSM_CHUNKS = 8
CHUNK_CH = D_MODEL // SSM_CHUNKS
CHUNK_ST = CHUNK_CH // SSM_GROUP * SSM_STATE
N_STATE = D_MODEL // SSM_GROUP * SSM_STATE
HEAD_DIM = 64
N_HEADS = 16
BAND = 128
DILATIONS = (1, 4, 16)
REL_BUCKETS = 32
REL_MAX_DIST = 2048
NEG_BIG = -1e30
DN_ALPHA = (2.0 * 2) ** 0.25
LN_EPS = 1e-5
ADAM_LR = 0.001
ADAM_B1 = 0.9
ADAM_B2 = 0.999
ADAM_EPS = 1e-08
ADAM_WD = 0.01
ADAM_STEP = 10

TM = 512
TM_MM = 2048
TM_MM_F32 = 1024
T_SCAN = 1024
VMEM_LIMIT = 56 * 1024 * 1024
FFN_HIDDEN = BF16

NN = (((1,), (0,)), ((), ()))
NT = (((1,), (1,)), ((), ()))
TN = (((0,), (0,)), ((), ()))


class _Plan:
    def __init__(self, srcs, out_shapes, n_sems, n_local, phases):
        self.srcs, self.out_shapes, self.n_sems, self.n_local, self.phases = srcs, out_shapes, n_sems, n_local, phases


def _call(body, name, grid, in_specs, out_specs, out_shape, scratch=(), comm=None, fill=()):
    params = dict(dimension_semantics=("arbitrary",) * len(grid), vmem_limit_bytes=VMEM_LIMIT)
    if comm is None and not fill:
        return pl.pallas_call(
            body, name=name, grid=grid, in_specs=in_specs, out_specs=out_specs, out_shape=out_shape,
            scratch_shapes=list(scratch), compiler_params=pltpu.CompilerParams(**params))
    single = not isinstance(out_specs, (list, tuple))
    o_specs = [out_specs] if single else list(out_specs)
    o_shape = [out_shape] if single else list(out_shape)
    n_in, n_out, n_scr, n_fill = len(in_specs), len(o_specs), len(scratch), len(fill)
    srcs = list(comm.srcs) if comm else []
    c_shapes = list(comm.out_shapes) if comm else []
    n_cin, n_cout = len(srcs), len(c_shapes)
    total = int(np.prod(grid))

    def wrapped(*refs):
        ins = refs[:n_in]
        refs = refs[n_in + n_fill:]
        cins, outs, couts = refs[:n_cin], refs[n_cin:n_cin + n_out], refs[n_cin + n_out:n_cin + n_out + n_cout]
        rest = refs[n_cin + n_out + n_cout:]
        scr, sems = rest[:n_scr], rest[n_scr:]
        if comm is None:
            body(*ins, *outs, *scr)
            return
        step = pl.program_id(0)
        for ax in range(1, len(grid)):
            step = step * grid[ax] + pl.program_id(ax)
        phases = comm.phases(cins, couts, *sems, total)
        for at, action in phases:
            if at == 0:
                pl.when(step == 0)(action)
        body(*ins, *outs, *scr)
        for at, action in phases:
            if at > 0:
                pl.when(step == at)(action)

    any_spec = pl.BlockSpec(memory_space=pl.ANY)
    sems = [] if comm is None else [pltpu.SemaphoreType.DMA((comm.n_sems,)), pltpu.SemaphoreType.DMA((comm.n_sems,)),
                                    pltpu.SemaphoreType.DMA((max(comm.n_local, 1),))]
    call = pl.pallas_call(
        wrapped, name=name, grid=grid, in_specs=list(in_specs) + [any_spec] * (n_fill + n_cin),
        out_specs=o_specs + [any_spec] * n_cout, out_shape=o_shape + c_shapes,
        scratch_shapes=list(scratch) + sems, input_output_aliases={n_in + j: j for j in range(n_fill)},
        compiler_params=pltpu.CompilerParams(has_side_effects=comm is not None, **params))

    def run(*args):
        res = call(*args, *fill, *srcs)
        own = res[0] if single else res[:n_out]
        return (own, res[n_out:]) if comm is not None else own

    return run


def _sds(shape, dtype):
    return jax.ShapeDtypeStruct(shape, dtype)


def _mm(name, a, b, *, dims, grid, a_spec, b_spec, o_spec, out_shape, nred=1, red_axis=0,
        add=None, add_spec=None, add_scale=1.0, comm=None, norm=None, inner=None, post=None):
    has_add = add is not None
    n_extra = (1 if has_add else 0) + (1 if post else 0) + (2 if norm else 0)
    n_out = 3 if norm else 1
    acc_shape = tuple(s for s in o_spec.block_shape if s is not None)

    def body(*refs):
        a_ref, b_ref = refs[:2]
        extra = refs[2:2 + n_extra]
        outs = refs[2 + n_extra:2 + n_extra + n_out]
        rest = refs[2 + n_extra + n_out:]
        o_ref = outs[0]
        if inner is None:
            prod = lax.dot_general(a_ref[...].astype(BF16), b_ref[...].astype(BF16), dims,
                                   preferred_element_type=F32)
        else:
            nb, a_mode = inner
            width = a_ref.shape[-1] // nb
            prod = None
            for d in range(nb):
                a_d = a_ref[d] if a_mode == "lead" else a_ref[:, d * width:(d + 1) * width]
                part = lax.dot_general(a_d.astype(BF16), b_ref[d].astype(BF16), dims, preferred_element_type=F32)
                prod = part if prod is None else prod + part

        def finish(val):
            if has_add:
                val = val + add_scale * extra[0][...].astype(F32)
            if post:
                val = post[0](val, extra[1 if has_add else 0][...])
            o_ref[...] = val.astype(o_ref.dtype)
            if norm:
                xhat, _ = _ln_stats(val)
                y = xhat * extra[-2][...] + extra[-1][...]
                outs[1][...] = y
                outs[2][...] = y.astype(BF16)

        if nred == 1:
            finish(prod)
        else:
            acc = rest[0]
            k = pl.program_id(red_axis)

            @pl.when(k == 0)
            def _():
                acc[...] = prod

            @pl.when(k > 0)
            def _():
                acc[...] += prod

            @pl.when(k == nred - 1)
            def _():
                finish(acc[...])

    in_specs = [a_spec, b_spec] + ([add_spec] if has_add else [])
    scratch = [pltpu.VMEM(acc_shape, F32)] if nred > 1 else []
    args = (a, b) + ((add,) if has_add else ())
    if post:
        in_specs.append(o_spec)
        args += (post[1],)
    if norm:
        vec = pl.BlockSpec((1, out_shape.shape[1]), lambda *_: (0, 0))
        in_specs += [vec, vec]
        args += tuple(norm)
        o_spec = [o_spec, o_spec, o_spec]
        out_shape = [out_shape, out_shape, _sds(out_shape.shape, BF16)]
    return _call(body, name, grid, in_specs, o_spec, out_shape, scratch, comm=comm)(*args)


def _mm_nn(name, a, b, out_dtype, blocked_out=False, comm=None, blocks=None, residual=None, norm=None):
    TM = TM_MM if out_dtype == BF16 else TM_MM_F32
    m, k = a.shape
    _, _, n = b.shape
    nb = b.shape[0] if blocks is None else blocks[1]
    first = 0 if blocks is None else blocks[0]
    if blocked_out:
        o_spec = pl.BlockSpec((None, TM, n), lambda d, i: (d, i, 0))
        out_shape = _sds((nb, m, n), out_dtype)
    else:
        o_spec = pl.BlockSpec((TM, n), lambda d, i: (i, d))
        out_shape = _sds((m, nb * n), out_dtype)
    return _mm(name, a, b, dims=NN, grid=(nb, m // TM),
               a_spec=pl.BlockSpec((TM, k), lambda d, i: (i, 0)),
               b_spec=pl.BlockSpec((None, k, n), lambda d, i: (d + first, 0, 0)),
               o_spec=o_spec, out_shape=out_shape, comm=comm, norm=norm,
               add=residual, add_spec=o_spec if residual is not None else None, add_scale=DN_ALPHA)


def _mm_nn_red(name, a, b, out_dtype, comm=None, residual=None, norm=None):
    nb, m, k = a.shape
    n = b.shape[2]
    o_spec = pl.BlockSpec((TM, n), lambda i: (i, 0))
    return _mm(name, a, b, dims=NN, grid=(m // TM,), inner=(nb, "lead"),
               a_spec=pl.BlockSpec((nb, TM, k), lambda i: (0, i, 0)),
               b_spec=pl.BlockSpec((nb, k, n), lambda i: (0, 0, 0)),
               o_spec=o_spec, out_shape=_sds((m, n), out_dtype), comm=comm, norm=norm,
               add=residual, add_spec=o_spec if residual is not None else None, add_scale=DN_ALPHA)


def _mm_tn(name, a, g, nb, out_dtype, a_blocked=False, g_mode="cols", comm=None):
    TM = TM_MM
    if a_blocked:
        _, m, ka = a.shape
        a_spec = pl.BlockSpec((None, TM, ka), lambda d, i: (d, i, 0))
    else:
        m, ka = a.shape
        a_spec = pl.BlockSpec((TM, ka), lambda d, i: (i, 0))
    if g_mode == "cols":
        n = g.shape[1] // nb
        g_spec = pl.BlockSpec((TM, n), lambda d, i: (i, d))
    elif g_mode == "blocked":
        n = g.shape[2]
        g_spec = pl.BlockSpec((None, TM, n), lambda d, i: (d, i, 0))
    else:
        n = g.shape[1]
        g_spec = pl.BlockSpec((TM, n), lambda d, i: (i, 0))
    nred = m // TM
    return _mm(name, a, g, dims=TN, grid=(nb, nred), nred=nred, red_axis=1,
               a_spec=a_spec, b_spec=g_spec,
               o_spec=pl.BlockSpec((None, ka, n), lambda d, i: (d, 0, 0)),
               out_shape=_sds((nb, ka, n), out_dtype), comm=comm)


def _mm_nt_red(name, g, w, out_dtype, g_blocked, add=None, add_scale=1.0, comm=None, blocks=None, post=None):
    _, k, n = w.shape
    nb = w.shape[0] if blocks is None else blocks[1]
    first = 0 if blocks is None else blocks[0]
    tm = TM_MM_F32 if nb * k * n <= 4 * 1024 * 768 else TM
    if g_blocked:
        m = g.shape[1]
        g_spec = pl.BlockSpec((nb, tm, n), lambda i: (0, i, 0))
    else:
        m = g.shape[0]
        g_spec = pl.BlockSpec((tm, nb * n), lambda i: (i, 0))
    o_spec = pl.BlockSpec((tm, k), lambda i: (i, 0))
    return _mm(name, g, w, dims=NT, grid=(m // tm,), inner=(nb, "lead" if g_blocked else "cols"),
               a_spec=g_spec, b_spec=pl.BlockSpec((nb, k, n), lambda i: (first // nb, 0, 0)),
               o_spec=o_spec, out_shape=_sds((m, k), out_dtype),
               add=add, add_spec=o_spec if add is not None else None, add_scale=add_scale, comm=comm, post=post)


def _mm_nt_out(name, g, w, out_dtype):
    TM = TM_MM
    m, n = g.shape
    nb, k, _ = w.shape
    return _mm(name, g, w, dims=NT, grid=(nb, m // TM),
               a_spec=pl.BlockSpec((TM, n), lambda d, i: (i, 0)),
               b_spec=pl.BlockSpec((None, k, n), lambda d, i: (d, 0, 0)),
               o_spec=pl.BlockSpec((None, TM, k), lambda d, i: (d, i, 0)),
               out_shape=_sds((nb, m, k), out_dtype))


def _ln_stats(r):
    mu = jnp.mean(r, axis=-1, keepdims=True)
    xc = r - mu
    var = jnp.mean(xc * xc, axis=-1, keepdims=True)
    rstd = lax.rsqrt(var + LN_EPS)
    return xc * rstd, rstd


def _ln_bwd(name, r, dy, gain):
    n, d = r.shape

    def body(r_ref, dy_ref, g_ref, dr_ref, drb_ref, dg_ref, db_ref):
        i = pl.program_id(0)
        xhat, rstd = _ln_stats(r_ref[...])
        dy_v = dy_ref[...]
        dxh = dy_v * g_ref[...]
        m1 = jnp.mean(dxh, axis=-1, keepdims=True)
        m2 = jnp.mean(dxh * xhat, axis=-1, keepdims=True)
        dr = rstd * (dxh - m1 - xhat * m2)
        dr_ref[...] = dr
        drb_ref[...] = dr.astype(BF16)
        dg = jnp.sum(dy_v * xhat, axis=0, keepdims=True)
        db = jnp.sum(dy_v, axis=0, keepdims=True)

        @pl.when(i == 0)
        def _():
            dg_ref[...] = dg
            db_ref[...] = db

        @pl.when(i > 0)
        def _():
            dg_ref[...] += dg
            db_ref[...] += db

    row = pl.BlockSpec((TM, d), lambda i: (i, 0))
    vec = pl.BlockSpec((1, d), lambda i: (0, 0))
    return _call(body, name, (n // TM,), [row, row, vec], [row, row, vec, vec],
                 [_sds((n, d), F32), _sds((n, d), BF16), _sds((1, d), F32), _sds((1, d), F32)])(r, dy, gain)


def _loss_head(y, target):
    n, d = y.shape

    def body(y_ref, t_ref, l_ref, dy_ref):
        i = pl.program_id(0)
        e = y_ref[...] - t_ref[...]
        dy_ref[...] = e * (1.0 / d)
        part = jnp.sum(jnp.sum(e * e, axis=1, keepdims=True), axis=0, keepdims=True) * (0.5 / d)
        part = jnp.broadcast_to(part, (8, 128))

        @pl.when(i == 0)
        def _():
            l_ref[...] = part

        @pl.when(i > 0)
        def _():
            l_ref[...] += part

    row = pl.BlockSpec((TM, d), lambda i: (i, 0))
    return _call(body, "loss_head", (n // TM,), [row, row],
                 [pl.BlockSpec((8, 128), lambda i: (0, 0)), row],
                 [_sds((8, 128), F32), _sds((n, d), F32)])(y, target)


def _group_expand_matrix():
    e = np.zeros((D_MODEL // SSM_GROUP, N_STATE), np.float32)
    for g in range(D_MODEL // SSM_GROUP):
        e[g, g * SSM_STATE:(g + 1) * SSM_STATE] = 1.0
    return jnp.asarray(e)


def _discretise(lr, li, ldt, br, bi, expand):
    dt = jnp.dot(jnp.exp(ldt), expand, precision=HIGHEST, preferred_element_type=F32)
    mag = jnp.exp(lr * dt)
    th = li * dt
    ab_r = mag * jnp.cos(th)
    ab_i = mag * jnp.sin(th)
    den = lr * lr + li * li
    nr = ab_r - 1.0
    co_r = (nr * lr + ab_i * li) / den
    co_i = (ab_i * lr - nr * li) / den
    bb_r = co_r * br - co_i * bi
    bb_i = co_r * bi + co_i * br
    return ab_r, ab_i, bb_r, bb_i


def _cmul(ar, ai, br, bi):
    return ar * br - ai * bi, ar * bi + ai * br


def _s5_prep(lr, li, ldt, br, bi):
    s = N_STATE
    expand = _group_expand_matrix()

    def body(lr_ref, li_ref, ldt_ref, br_ref, bi_ref, e_ref, abr_ref, abi_ref, bbr_ref, bbi_ref, tab_ref):
        ab_r, ab_i, bb_r, bb_i = _discretise(lr_ref[...], li_ref[...], ldt_ref[...], br_ref[...],
                                             bi_ref[...], e_ref[...])
        abr_ref[...] = ab_r
        abi_ref[...] = ab_i
        bbr_ref[...] = bb_r
        bbi_ref[...] = bb_i
        row = lax.broadcasted_iota(jnp.int32, (8, s), 0)
        for base, sign in ((0, 1.0), (8, -1.0)):
            p = [None] * 9
            p[1] = (ab_r, sign * ab_i)
            p[2] = _cmul(*p[1], *p[1])
            p[3] = _cmul(*p[2], *p[1])
            p[4] = _cmul(*p[2], *p[2])
            p[5] = _cmul(*p[4], *p[1])
            p[6] = _cmul(*p[4], *p[2])
            p[7] = _cmul(*p[4], *p[3])
            p[8] = _cmul(*p[4], *p[4])
            for j, k in enumerate((1, 2, 4)):
                keep = (row >= k) if base == 0 else (row < 8 - k)
                tab_ref[base + 2 * j] = jnp.where(keep, jnp.broadcast_to(p[k][0], (8, s)), 0.0)
                tab_ref[base + 2 * j + 1] = jnp.where(keep, jnp.broadcast_to(p[k][1], (8, s)), 0.0)
            pw_r = jnp.zeros((8, s), F32)
            pw_i = jnp.zeros((8, s), F32)
            for i in range(8):
                e = i + 1 if base == 0 else 8 - i
                pw_r = jnp.where(row == i, jnp.broadcast_to(p[e][0], (8, s)), pw_r)
                pw_i = jnp.where(row == i, jnp.broadcast_to(p[e][1], (8, s)), pw_i)
            tab_ref[base + 6] = pw_r
            tab_ref[base + 7] = pw_i

    def full(shape):
        return pl.BlockSpec(shape, lambda i: (0,) * len(shape))

    g = D_MODEL // SSM_GROUP
    return _call(body, "s5_prep", (1,),
                 [full((1, s)), full((1, s)), full((1, g)), full((16, s)), full((16, s)), full((g, s))],
                 [full((1, s)), full((1, s)), full((16, s)), full((16, s)), full((16, 8, s))],
                 [_sds((1, s), F32), _sds((1, s), F32), _sds((16, s), F32), _sds((16, s), F32),
                  _sds((16, 8, s), F32)])(lr, li, ldt, br, bi, expand)


def _s5_prep_bwd(lr, li, ldt, br, bi, d_abr, d_abi, d_bbr, d_bbi):
    s = N_STATE
    g = D_MODEL // SSM_GROUP
    expand = _group_expand_matrix()

    def body(lr_ref, li_ref, ldt_ref, br_ref, bi_ref, e_ref, c1, c2, c3, c4, o1, o2, o3, o4, o5):
        e_val = e_ref[...]
        _, vjp = jax.vjp(lambda a, b, c, d, e: _discretise(a, b, c, d, e, e_val),
                         lr_ref[...], li_ref[...], ldt_ref[...], br_ref[...], bi_ref[...])
        grads = vjp((c1[...], c2[...], c3[...], c4[...]))
        for o, gr in zip((o1, o2, o3, o4, o5), grads):
            o[...] = gr

    def full(shape):
        return pl.BlockSpec(shape, lambda i: (0,) * len(shape))

    return _call(body, "s5_prep_bwd", (1,),
                 [full((1, s)), full((1, s)), full((1, g)), full((16, s)), full((16, s)), full((g, s)),
                  full((1, s)), full((1, s)), full((16, s)), full((16, s))],
                 [full((1, s)), full((1, s)), full((1, g)), full((16, s)), full((16, s))],
                 [_sds((1, s), F32), _sds((1, s), F32), _sds((1, g), F32), _sds((16, s), F32),
                  _sds((16, s), F32)])(lr, li, ldt, br, bi, expand, d_abr, d_abi, d_bbr, d_bbi)


def _scan8(vr, vi, tab_ref, base, reverse):
    for j, k in enumerate((1, 2, 4)):
        mr = tab_ref[base + 2 * j]
        mi = tab_ref[base + 2 * j + 1]
        shift = 8 - k if reverse else k
        sr = pltpu.roll(vr, shift, 0)
        si = pltpu.roll(vi, shift, 0)
        vr, vi = vr + mr * sr - mi * si, vi + mr * si + mi * sr
    return vr, vi


def _s5_scan_fwd(x, bbr, bbi, cr, ci, dvec, tab, seq, comm=None):
    n, d = x.shape
    nt = seq // T_SCAN
    nblk = T_SCAN // 8
    st = CHUNK_ST

    def body(x_ref, bbr_ref, bbi_ref, cr_ref, ci_ref, d_ref, tab_ref, y_ref, hr_ref, hi_ref, car_r, car_i):
        t = pl.program_id(2)

        @pl.when(t == 0)
        def _():
            car_r[...] = jnp.zeros_like(car_r)
            car_i[...] = jnp.zeros_like(car_i)

        u = x_ref[...]
        ub = u.astype(BF16)
        hr_ref[...] = jnp.dot(ub, bbr_ref[...], preferred_element_type=F32)
        hi_ref[...] = jnp.dot(ub, bbi_ref[...], preferred_element_type=F32)

        def step(i, carry):
            sl = pl.ds(pl.multiple_of(i * 8, 8), 8)
            vr, vi = _scan8(hr_ref[sl, :], hi_ref[sl, :], tab_ref, 0, False)
            c_r = jnp.broadcast_to(car_r[...], (8, st))
            c_i = jnp.broadcast_to(car_i[...], (8, st))
            pr = tab_ref[6]
            pi = tab_ref[7]
            vr, vi = vr + pr * c_r - pi * c_i, vi + pr * c_i + pi * c_r
            hr_ref[sl, :] = vr
            hi_ref[sl, :] = vi
            car_r[...] = vr[7:8, :]
            car_i[...] = vi[7:8, :]
            return carry

        lax.fori_loop(0, nblk, step, 0, unroll=4)
        y = lax.dot_general(hr_ref[...].astype(BF16), cr_ref[...], NT, preferred_element_type=F32)
        y = y - lax.dot_general(hi_ref[...].astype(BF16), ci_ref[...], NT, preferred_element_type=F32)
        y_ref[...] = y + d_ref[...] * u

    row = lambda c, b, t: (b * nt + t, c)
    mat = pl.BlockSpec((None, CHUNK_CH, st), lambda c, b, t: (c, 0, 0))
    return _call(
        body, "s5_scan_fwd", (SSM_CHUNKS, n // seq, nt),
        [pl.BlockSpec((T_SCAN, CHUNK_CH), row), mat, mat, mat, mat,
         pl.BlockSpec((1, CHUNK_CH), lambda c, b, t: (0, c)),
         pl.BlockSpec((16, 8, st), lambda c, b, t: (0, 0, c))],
        [pl.BlockSpec((T_SCAN, CHUNK_CH), row), pl.BlockSpec((T_SCAN, st), row), pl.BlockSpec((T_SCAN, st), row)],
        [_sds((n, d), F32), _sds((n, N_STATE), F32), _sds((n, N_STATE), F32)],
        [pltpu.VMEM((1, st), F32), pltpu.VMEM((1, st), F32)], comm=comm,
    )(x, bbr, bbi, cr, ci, dvec, tab)


def _s5_scan_bwd(x, dy, dres, hr, hi, bbr, bbi, cr, ci, dvec, tab, seq, comm=None):
    n, d = x.shape
    nt = seq // T_SCAN
    nblk = T_SCAN // 8
    st = CHUNK_ST

    def body(x_ref, dy_ref, dres_ref, hr_ref, hi_ref, hpr_ref, hpi_ref, bbr_ref, bbi_ref, cr_ref, ci_ref,
             d_ref, tab_ref, dx_ref, dcr_ref, dci_ref, dbr_ref, dbi_ref, dar_ref, dai_ref, dd_ref,
             g_r, g_i, car_r, car_i):
        b = pl.program_id(1)
        tg = pl.program_id(2)
        first = jnp.logical_and(b == 0, tg == 0)

        @pl.when(tg == 0)
        def _():
            car_r[...] = jnp.zeros_like(car_r)
            car_i[...] = jnp.zeros_like(car_i)

        u = x_ref[...]
        dyv = dy_ref[...]
        ub = u.astype(BF16)
        dyb = dyv.astype(BF16)
        g_r[...] = jnp.dot(dyb, cr_ref[...], preferred_element_type=F32)
        g_i[...] = -jnp.dot(dyb, ci_ref[...], preferred_element_type=F32)

        def step(ii, carry):
            i = nblk - 1 - ii
            sl = pl.ds(pl.multiple_of(i * 8, 8), 8)
            vr, vi = _scan8(g_r[sl, :], g_i[sl, :], tab_ref, 8, True)
            c_r = jnp.broadcast_to(car_r[...], (8, st))
            c_i = jnp.broadcast_to(car_i[...], (8, st))
            pr = tab_ref[14]
            pi = tab_ref[15]
            vr, vi = vr + pr * c_r - pi * c_i, vi + pr * c_i + pi * c_r
            g_r[sl, :] = vr
            g_i[sl, :] = vi
            car_r[...] = vr[0:1, :]
            car_i[...] = vi[0:1, :]
            return carry

        lax.fori_loop(0, nblk, step, 0, unroll=4)
        gr = g_r[...]
        gi = g_i[...]
        grb = gr.astype(BF16)
        gib = gi.astype(BF16)
        du = lax.dot_general(grb, bbr_ref[...], NT, preferred_element_type=F32)
        du = du + lax.dot_general(gib, bbi_ref[...], NT, preferred_element_type=F32)
        dx_ref[...] = DN_ALPHA * dres_ref[...] + du + d_ref[...] * dyv

        h_r = hr_ref[...]
        h_i = hi_ref[...]
        t_orig = nt - 1 - tg
        row0 = lax.broadcasted_iota(jnp.int32, (8, st), 0) == 0
        halo_ok = t_orig > 0

        def previous(h, halo_ref):
            top = jnp.broadcast_to(jnp.where(halo_ok, halo_ref[7:8, :], 0.0), (8, st))
            rolled = pltpu.roll(h, 1, 0)
            return jnp.concatenate([jnp.where(row0, top, rolled[:8, :]), rolled[8:, :]], axis=0)

        hp_r = previous(h_r, hpr_ref)
        hp_i = previous(h_i, hpi_ref)
        dar = jnp.sum(gr * hp_r + gi * hp_i, axis=0, keepdims=True)
        dai = jnp.sum(gi * hp_r - gr * hp_i, axis=0, keepdims=True)
        dcr = lax.dot_general(dyb, h_r.astype(BF16), TN, preferred_element_type=F32)
        dci = lax.dot_general(dyb, h_i.astype(BF16), TN, preferred_element_type=F32)
        dbr = lax.dot_general(ub, grb, TN, preferred_element_type=F32)
        dbi = lax.dot_general(ub, gib, TN, preferred_element_type=F32)
        dd = jnp.sum(dyv * u, axis=0, keepdims=True)

        @pl.when(first)
        def _():
            dcr_ref[...] = dcr
            dci_ref[...] = dci
            dbr_ref[...] = dbr
            dbi_ref[...] = dbi
            dar_ref[...] = dar
            dai_ref[...] = dai
            dd_ref[...] = dd

        @pl.when(jnp.logical_not(first))
        def _():
            dcr_ref[...] += dcr
            dci_ref[...] += dci
            dbr_ref[...] += dbr
            dbi_ref[...] += dbi
            dar_ref[...] += dar
            dai_ref[...] += dai
            dd_ref[...] += dd

    row = lambda c, b, t: (b * nt + (nt - 1 - t), c)
    halo = lambda c, b, t: (jnp.maximum((b * nt + (nt - 1 - t)) * (T_SCAN // 8) - 1, 0), c)
    mat = pl.BlockSpec((None, CHUNK_CH, st), lambda c, b, t: (c, 0, 0))
    chan = pl.BlockSpec((T_SCAN, CHUNK_CH), row)
    state = pl.BlockSpec((T_SCAN, st), row)
    return _call(
        body, "s5_scan_bwd", (SSM_CHUNKS, n // seq, nt),
        [chan, chan, chan, state, state, pl.BlockSpec((8, st), halo), pl.BlockSpec((8, st), halo),
         mat, mat, mat, mat, pl.BlockSpec((1, CHUNK_CH), lambda c, b, t: (0, c)),
         pl.BlockSpec((16, 8, st), lambda c, b, t: (0, 0, c))],
        [chan, mat, mat, mat, mat, pl.BlockSpec((1, st), lambda c, b, t: (0, c)),
         pl.BlockSpec((1, st), lambda c, b, t: (0, c)), pl.BlockSpec((1, CHUNK_CH), lambda c, b, t: (0, c))],
        [_sds((n, d), F32)] + [_sds((SSM_CHUNKS, CHUNK_CH, st), F32)] * 4
        + [_sds((1, N_STATE), F32), _sds((1, N_STATE), F32), _sds((1, d), F32)],
        [pltpu.VMEM((T_SCAN, st), F32), pltpu.VMEM((T_SCAN, st), F32), pltpu.VMEM((1, st), F32),
         pltpu.VMEM((1, st), F32)], comm=comm,
    )(x, dy, dres, hr, hi, hr, hi, bbr, bbi, cr, ci, dvec, tab)


_GELU_C = math.sqrt(2.0 / math.pi)


def _gelu_parts(y):
    t = jnp.tanh(_GELU_C * (y + 0.044715 * (y * y * y)))
    return 0.5 * (1.0 + t), t


def _glu_fwd(y, w_glu, b_glu):
    n, d = y.shape

    def body(y_ref, w_ref, b_ref, y2_ref, z_ref, gg_ref):
        yv = y_ref[...]
        cdf, _ = _gelu_parts(yv)
        y2 = yv * cdf
        z = jnp.dot(y2.astype(BF16), w_ref[...], preferred_element_type=F32) + b_ref[...]
        y2_ref[...] = y2.astype(BF16)
        z_ref[...] = z
        gg_ref[...] = (y2 * jax.nn.sigmoid(z)).astype(BF16)

    row = pl.BlockSpec((TM, d), lambda i: (i, 0))
    return _call(body, "glu_fwd", (n // TM,),
                 [row, pl.BlockSpec((d, d), lambda i: (0, 0)), pl.BlockSpec((1, d), lambda i: (0, 0))],
                 [row, row, row], [_sds((n, d), BF16), _sds((n, d), F32), _sds((n, d), BF16)])(y, w_glu, b_glu)


def _glu_bwd_gate(y, z, dgg):
    n, d = y.shape

    def body(y_ref, z_ref, dgg_ref, dz_ref, db_ref, t_ref):
        i = pl.program_id(0)
        yv = y_ref[...]
        cdf, _ = _gelu_parts(yv)
        y2 = yv * cdf
        s = jax.nn.sigmoid(z_ref[...])
        dg = dgg_ref[...]
        dz = dg * y2 * s * (1.0 - s)
        dz_ref[...] = dz.astype(BF16)
        t_ref[...] = dg * s
        db = jnp.sum(dz, axis=0, keepdims=True)

        @pl.when(i == 0)
        def _():
            db_ref[...] = db

        @pl.when(i > 0)
        def _():
            db_ref[...] += db

    row = pl.BlockSpec((TM, d), lambda i: (i, 0))
    vec = pl.BlockSpec((1, d), lambda i: (0, 0))
    return _call(body, "glu_bwd_gate", (n // TM,), [row, row, row], [row, vec, row],
                 [_sds((n, d), BF16), _sds((1, d), F32), _sds((n, d), F32)])(y, z, dgg)


def _gelu_bwd(dy2, y):
    cdf, t = _gelu_parts(y)
    dinner = _GELU_C * (1.0 + 3.0 * 0.044715 * y * y)
    return dy2 * (cdf + 0.5 * y * (1.0 - t * t) * dinner)


HALO = 16


def _shift_down(x, halo, k):
    out = pltpu.roll(x, k, 0)
    row = lax.broadcasted_iota(jnp.int32, (8, x.shape[1]), 0)
    top = jnp.where(row < k, pltpu.roll(halo[HALO - 8:HALO, :], k, 0), out[:8, :])
    return jnp.concatenate([top, out[8:, :]], axis=0)


def _conv3(x, halo, w, b):
    x1 = _shift_down(x, halo, 1)
    x2 = _shift_down(x, halo, 2)
    return b + w[0:1, :] * x + w[1:2, :] * x1 + w[2:3, :] * x2, x1, x2


def _conv_gate_fwd(hc, conv_w, conv_b, seq, comm=None):
    _, nb, n, c = hc.shape
    tiles_per_seq = seq // TM

    def body(x_ref, halo_ref, w_ref, b_ref, o_ref, cv_ref):
        i = pl.program_id(1)
        start = (i % tiles_per_seq) == 0
        cv = []
        for h in range(2):
            halo = jnp.where(start, 0.0, halo_ref[h].astype(F32))
            cv.append(_conv3(x_ref[h].astype(F32), halo, w_ref[h], b_ref[h])[0])
            cv_ref[h] = cv[h].astype(BF16)
        o_ref[...] = (cv[1] * jax.nn.sigmoid(cv[1]) * cv[0]).astype(BF16)

    both = pl.BlockSpec((2, None, TM, c), lambda d, i: (0, d, i, 0))
    return _call(
        body, "conv_gate_fwd", (nb, n // TM),
        [both,
         pl.BlockSpec((2, None, HALO, c), lambda d, i: (0, d, jnp.maximum(i * (TM // HALO) - 1, 0), 0)),
         pl.BlockSpec((2, None, 3, c), lambda d, i: (0, d, 0, 0)),
         pl.BlockSpec((2, None, 1, c), lambda d, i: (0, d, 0, 0))],
        [pl.BlockSpec((None, TM, c), lambda d, i: (d, i, 0)), both],
        [_sds((nb, n, c), BF16), _sds((2, nb, n, c), BF16)], comm=comm)(hc, hc, conv_w, conv_b)


def _conv_gate_bwd(hc, cv, dact, conv_w, seq, comm=None):
    _, nb, n, c = hc.shape
    tiles_per_seq = seq // TM
    last_halo = n // HALO - 1
    ext = TM + HALO

    def body(x_ref, cv_ref, cvn_ref, da_ref, dan_ref, w_ref, dx_ref, dw_ref):
        i = pl.program_id(1)
        end = (i % tiles_per_seq) == tiles_per_seq - 1
        row = lax.broadcasted_iota(jnp.int32, (ext, c), 0)
        cv = [jnp.concatenate([cv_ref[h], cvn_ref[h]], axis=0).astype(F32) for h in range(2)]
        da = jnp.concatenate([da_ref[...], dan_ref[...]], axis=0).astype(F32)
        da = jnp.where(jnp.logical_and(end, row >= TM), 0.0, da)
        sg = jax.nn.sigmoid(cv[1])
        silu = cv[1] * sg
        dcv = [da * silu, da * cv[0] * (sg + silu * (1.0 - sg))]
        for h in range(2):
            w = w_ref[h]
            g = dcv[h]
            g0 = g[:TM, :]
            g1 = pltpu.roll(g, ext - 1, 0)[:TM, :]
            g2 = pltpu.roll(g, ext - 2, 0)[:TM, :]
            dx_ref[h] = (w[0:1, :] * g0 + w[1:2, :] * g1 + w[2:3, :] * g2).astype(BF16)
            x = x_ref[h].astype(F32)
            parts = [jnp.sum(g0 * x, axis=0, keepdims=True),
                     jnp.sum(g1 * x, axis=0, keepdims=True),
                     jnp.sum(g2 * x, axis=0, keepdims=True),
                     jnp.sum(g0, axis=0, keepdims=True)]
            upd = jnp.concatenate(parts + [jnp.zeros((4, c), F32)], axis=0)

            @pl.when(i == 0)
            def _():
                dw_ref[h] = upd

            @pl.when(i > 0)
            def _():
                dw_ref[h] += upd

    nxt = lambda i: jnp.minimum((i + 1) * (TM // HALO), last_halo)
    return _call(
        body, "conv_gate_bwd", (nb, n // TM),
        [pl.BlockSpec((2, None, TM, c), lambda d, i: (0, d, i, 0)),
         pl.BlockSpec((2, None, TM, c), lambda d, i: (0, d, i, 0)),
         pl.BlockSpec((2, None, HALO, c), lambda d, i: (0, d, nxt(i), 0)),
         pl.BlockSpec((None, TM, c), lambda d, i: (d, i, 0)),
         pl.BlockSpec((None, HALO, c), lambda d, i: (d, nxt(i), 0)),
         pl.BlockSpec((2, None, 3, c), lambda d, i: (0, d, 0, 0))],
        [pl.BlockSpec((2, None, TM, c), lambda d, i: (0, d, i, 0)),
         pl.BlockSpec((2, None, 8, c), lambda d, i: (0, d, 0, 0))],
        [_sds((2, nb, n, c), BF16), _sds((2, nb, 8, c), F32)], comm=comm)(hc, cv, cv, dact, dact, conv_w)


def _t5_bucket_np(dist):
    exact = REL_BUCKETS // 2
    d = np.maximum(dist, 1).astype(np.float32)
    large = exact + (np.log(d / exact) / math.log(REL_MAX_DIST / exact) * (REL_BUCKETS - exact)).astype(np.int64)
    large = np.minimum(large, REL_BUCKETS - 1)
    return np.where(dist < exact, dist, large).astype(np.int32)


def _bucket_table(dil):
    steps = np.arange(BAND)[:, None] + BAND - np.arange(2 * BAND)[None, :]
    return _t5_bucket_np(np.maximum(steps, 0) * dil).reshape(1, BAND * 2 * BAND).astype(np.float32)


def _bias_expand(rel_t, dil):
    nk = BAND * 2 * BAND
    bucket = jnp.asarray(_bucket_table(dil))

    def body(r_ref, bk_ref, o_ref):
        ids = lax.broadcasted_iota(jnp.int32, (REL_BUCKETS, nk), 0).astype(F32)
        onehot = (ids == bk_ref[...]).astype(F32)
        o_ref[...] = jnp.dot(r_ref[...], onehot, precision=HIGHEST, preferred_element_type=F32)

    return _call(body, "bias_expand", (1,),
                 [pl.BlockSpec((N_HEADS, REL_BUCKETS), lambda i: (0, 0)), pl.BlockSpec((1, nk), lambda i: (0, 0))],
                 pl.BlockSpec((N_HEADS, nk), lambda i: (0, 0)), _sds((N_HEADS, nk), F32))(rel_t, bucket)


def _bias_reduce(dbias, dil):
    nk = BAND * 2 * BAND
    bucket = jnp.asarray(_bucket_table(dil))

    def body(g_ref, bk_ref, o_ref):
        ids = lax.broadcasted_iota(jnp.int32, (REL_BUCKETS, nk), 0).astype(F32)
        onehot = (ids == bk_ref[...]).astype(F32)
        o_ref[...] = lax.dot_general(g_ref[...], onehot, NT, precision=HIGHEST, preferred_element_type=F32)

    return _call(body, "bias_reduce", (1,),
                 [pl.BlockSpec((N_HEADS, nk), lambda i: (0, 0)), pl.BlockSpec((1, nk), lambda i: (0, 0))],
                 pl.BlockSpec((N_HEADS, REL_BUCKETS), lambda i: (0, 0)),
                 _sds((N_HEADS, REL_BUCKETS), F32))(dbias, bucket)


def _valid_mask(n):
    qi = lax.broadcasted_iota(jnp.int32, (BAND, 2 * BAND), 0)
    kj = lax.broadcasted_iota(jnp.int32, (BAND, 2 * BAND), 1)
    steps = qi + BAND - kj
    in_band = jnp.logical_and(steps >= 0, steps <= BAND)
    return jnp.logical_and(in_band, jnp.logical_or(n > 0, kj >= BAND))


ATTN_COLS = {1: 1024, 4: 512, 16: 128}


def _residue(dil, r):
    return slice(None) if dil == 1 else pl.ds(r, BAND, stride=dil)


def _stack_heads(x, first):
    return jnp.concatenate([jnp.where(first, x, 0.0), jnp.where(first, 0.0, x)], axis=0)


def _attn_fwd(q, k, v, bias, g, dil, bsz, seq, comm=None, merge=()):
    n = q.shape[0]
    rows = BAND * dil
    nch = seq // rows
    w = N_HEADS * HEAD_DIM
    wc = ATTN_COLS[dil]
    ncol = w // wc
    hpc = wc // HEAD_DIM
    nj = wc // 128
    slab = lambda k: pltpu.VMEM((k * nj, rows, 128), F32)

    def attend(q_ref, kc_ref, kp_ref, vc_ref, vp_ref, b_ref, others, o_ref, l_ref, qf, kf, vf, of, lf):
        ch = pl.program_id(2)
        valid = _valid_mask(ch)
        valid = jnp.concatenate([valid, valid], axis=0)
        lane = lax.broadcasted_iota(jnp.int32, (BAND, 128), 1)
        first = lane < HEAD_DIM
        for j in range(nj):
            sl = slice(128 * j, 128 * (j + 1))
            qf[j] = q_ref[:, sl].astype(F32)
            kf[j] = kp_ref[:, sl].astype(F32)
            kf[nj + j] = kc_ref[:, sl].astype(F32)
            vf[j] = vp_ref[:, sl].astype(F32)
            vf[nj + j] = vc_ref[:, sl].astype(F32)
        for r in range(dil):
            rr = _residue(dil, r)
            for j in range(nj):
                q2 = qf.at[j][rr, :]
                k2 = jnp.concatenate([kf.at[j][rr, :], kf.at[nj + j][rr, :]], axis=0).astype(BF16)
                v2 = jnp.concatenate([vf.at[j][rr, :], vf.at[nj + j][rr, :]], axis=0).astype(BF16)
                qs = _stack_heads(q2, first).astype(BF16)
                s = lax.dot_general(qs, k2, NT, preferred_element_type=F32) * (HEAD_DIM ** -0.5)
                s = jnp.where(valid, s + b_ref[2 * j:2 * j + 2].reshape(2 * BAND, 2 * BAND), NEG_BIG)
                m = jnp.max(s, axis=1, keepdims=True)
                p = jnp.exp(s - m)
                l = jnp.sum(p, axis=1, keepdims=True)
                pv = jnp.dot(p.astype(BF16), v2, preferred_element_type=F32) / l
                lse = jnp.broadcast_to(m + jnp.log(l), (2 * BAND, 128))
                of.at[j][rr, :] = jnp.where(first, pv[:BAND], pv[BAND:])
                lf.at[j][rr, :] = jnp.where(first, lse[:BAND], lse[BAND:])
        for j in range(nj):
            sl = slice(128 * j, 128 * (j + 1))
            if merge:
                o0, l0, o1, l1 = (r[:, sl] for r in others)
                lc = lf[j]
                m = jnp.maximum(jnp.maximum(l0, l1), lc)
                wa, wb, wc_ = jnp.exp(l0 - m), jnp.exp(l1 - m), jnp.exp(lc - m)
                tot = wa + wb + wc_
                o_ref[:, sl] = (wa * o0 + wb * o1 + wc_ * of[j]) / tot
                l_ref[:, sl] = m + jnp.log(tot)
            else:
                o_ref[:, sl] = of[j]
                l_ref[:, sl] = lf[j]

    def body(q_ref, kc_ref, kp_ref, vc_ref, vp_ref, b_ref, *rest):
        others, rest = rest[:len(merge)], rest[len(merge):]
        attend(q_ref, kc_ref, kp_ref, vc_ref, vp_ref, b_ref, others, *rest)

    base = g * ncol
    cur = lambda c, b, ch: (b * nch + ch, base + c)
    prev = lambda c, b, ch: (b * nch + jnp.maximum(ch - 1, 0), base + c)
    blk = lambda im: pl.BlockSpec((rows, wc), im)
    out_blk = pl.BlockSpec((rows, wc), lambda c, b, ch: (b * nch + ch, c))
    return _call(
        body, "attn_fwd_d%d" % dil, (ncol, bsz, nch),
        [blk(cur), blk(cur), blk(prev), blk(cur), blk(prev),
         pl.BlockSpec((hpc, BAND, 2 * BAND), lambda c, b, ch: (c, 0, 0))] + [out_blk] * len(merge),
        [out_blk, out_blk],
        [_sds((n, w), F32)] * 2,
        [slab(1), slab(2), slab(2), slab(1), slab(1)], comm=comm,
    )(q, k, k, v, v, bias, *merge)


def _attn_bwd(q, k, v, do, o, lse, bias, g, dil, bsz, seq, comm=None, into=()):
    n = q.shape[0]
    rows = BAND * dil
    nch = seq // rows
    w = N_HEADS * HEAD_DIM
    wc = ATTN_COLS[dil]
    ncol = w // wc
    hpc = wc // HEAD_DIM
    nj = wc // 128
    slab = lambda k: pltpu.VMEM((k * nj, rows, 128), F32)
    scale = HEAD_DIM ** -0.5

    def body(q_ref, kc_ref, kp_ref, vc_ref, vp_ref, do_ref, o_ref, l_ref, b_ref,
             dq_ref, dk_ref, dv_ref, db_ref, qf, kf, vf, dof, of, lf, dqf, dkf, dvf):
        b = pl.program_id(1)
        ch = pl.program_id(2)
        cur = ch % 2
        prv = 1 - cur

        @pl.when(jnp.logical_and(b == 0, ch == 0))
        def _():
            db_ref[...] = jnp.zeros_like(db_ref)

        @pl.when(ch == 0)
        def _():
            for j in range(nj):
                dkf[nj + j] = jnp.zeros((rows, 128), F32)
                dvf[nj + j] = jnp.zeros((rows, 128), F32)

        @pl.when(ch < nch)
        def _():
            valid = _valid_mask(ch)
            valid = jnp.concatenate([valid, valid], axis=0)
            first = lax.broadcasted_iota(jnp.int32, (BAND, 128), 1) < HEAD_DIM
            for j in range(nj):
                sl = slice(128 * j, 128 * (j + 1))
                qf[j] = q_ref[:, sl].astype(F32)
                kf[j] = kp_ref[:, sl].astype(F32)
                kf[nj + j] = kc_ref[:, sl].astype(F32)
                vf[j] = vp_ref[:, sl].astype(F32)
                vf[nj + j] = vc_ref[:, sl].astype(F32)
                dof[j] = do_ref[:, sl]
                of[j] = o_ref[:, sl]
                lf[j] = l_ref[:, sl]
            for r in range(dil):
                rr = _residue(dil, r)
                for j in range(nj):
                    k2 = jnp.concatenate([kf.at[j][rr, :], kf.at[nj + j][rr, :]], axis=0).astype(BF16)
                    v2 = jnp.concatenate([vf.at[j][rr, :], vf.at[nj + j][rr, :]], axis=0).astype(BF16)
                    do2 = dof.at[j][rr, :]
                    lse2 = lf.at[j][rr, :]
                    qs = _stack_heads(qf.at[j][rr, :], first).astype(BF16)
                    dos = _stack_heads(do2, first)
                    delta = jnp.sum(dos * jnp.concatenate([of.at[j][rr, :]] * 2, axis=0), axis=1, keepdims=True)
                    dos = dos.astype(BF16)
                    lse_col = jnp.concatenate([lse2[:, 0:1], lse2[:, HEAD_DIM:HEAD_DIM + 1]], axis=0)
                    s = lax.dot_general(qs, k2, NT, preferred_element_type=F32) * scale
                    s = jnp.where(valid, s + b_ref[2 * j:2 * j + 2].reshape(2 * BAND, 2 * BAND), NEG_BIG)
                    p = jnp.exp(s - lse_col)
                    dp = lax.dot_general(dos, v2, NT, preferred_element_type=F32)
                    ds = p * (dp - delta)
                    db_ref[2 * j:2 * j + 2] += ds.reshape(2, BAND, 2 * BAND)
                    dsb = ds.astype(BF16)
                    dq2 = jnp.dot(dsb, k2, preferred_element_type=F32) * scale
                    dk2 = lax.dot_general(dsb, qs, TN, preferred_element_type=F32) * scale
                    dv2 = lax.dot_general(p.astype(BF16), dos, TN, preferred_element_type=F32)
                    dqf.at[j][rr, :] = jnp.where(first, dq2[:BAND], dq2[BAND:])
                    dkf.at[prv * nj + j][rr, :] += dk2[:BAND, :]
                    dvf.at[prv * nj + j][rr, :] += dv2[:BAND, :]
                    dkf.at[cur * nj + j][rr, :] = dk2[BAND:, :]
                    dvf.at[cur * nj + j][rr, :] = dv2[BAND:, :]
            for j in range(nj):
                dq_ref[:, 128 * j:128 * (j + 1)] = dqf[j].astype(BF16)

        for j in range(nj):
            sl = slice(128 * j, 128 * (j + 1))
            dk_ref[:, sl] = dkf[prv * nj + j].astype(BF16)
            dv_ref[:, sl] = dvf[prv * nj + j].astype(BF16)

    last = nch - 1
    base = g * ncol
    cur3 = lambda c, b, ch: (b * nch + jnp.minimum(ch, last), base + c)
    prev3 = lambda c, b, ch: (b * nch + jnp.maximum(jnp.minimum(ch, last) - 1, 0), base + c)
    cur1 = lambda c, b, ch: (b * nch + jnp.minimum(ch, last), c)
    lag3 = lambda c, b, ch: (b * nch + jnp.maximum(ch - 1, 0), base + c)
    blk = lambda im: pl.BlockSpec((rows, wc), im)
    bias_blk = pl.BlockSpec((hpc, BAND, 2 * BAND), lambda c, b, ch: (c, 0, 0))
    out = _sds(q.shape, BF16)
    return _call(
        body, "attn_bwd_d%d" % dil, (ncol, bsz, nch + 1),
        [blk(cur3), blk(cur3), blk(prev3), blk(cur3), blk(prev3), blk(cur1), blk(cur1), blk(cur1), bias_blk],
        [blk(cur3), blk(lag3), blk(lag3), bias_blk],
        [out, out, out, _sds((N_HEADS, BAND, 2 * BAND), F32)],
        [slab(1), slab(2), slab(2), slab(1), slab(1), slab(1), slab(1), slab(2), slab(2)], comm=comm, fill=into,
    )(q, k, k, v, v, do, o, lse, bias)


def _adamw(name, parts, w, m, v):
    nl, r, c = w.shape
    nparts = parts[0].shape[0]
    tr = r
    for cand in (256, 352, 128):
        if r % cand == 0 and r > cand:
            tr = cand
            break
    c1 = 1.0 - ADAM_B1 ** ADAM_STEP
    c2 = 1.0 - ADAM_B2 ** ADAM_STEP

    def body(*refs):
        p_refs = refs[:nl]
        w_ref, m_ref, v_ref, g_ref, d_ref, nm_ref, nv_ref = refs[nl:]
        layer = pl.program_id(0)
        for l in range(nl):
            @pl.when(layer == l)
            def _():
                g = p_refs[l][0].astype(F32)
                for dev in range(1, nparts):
                    g = g + p_refs[l][dev].astype(F32)
                nm = ADAM_B1 * m_ref[...] + (1.0 - ADAM_B1) * g
                nv = ADAM_B2 * v_ref[...] + (1.0 - ADAM_B2) * (g * g)
                m_hat = nm / c1
                v_hat = nv / c2
                g_ref[...] = g
                d_ref[...] = -ADAM_LR * (m_hat / (jnp.sqrt(v_hat) + ADAM_EPS) + ADAM_WD * w_ref[...])
                nm_ref[...] = nm
                nv_ref[...] = nv

    def part_spec(l):
        return pl.BlockSpec((nparts, tr, c), lambda layer, i: (0, jnp.where(layer == l, i, 0), 0))

    row = pl.BlockSpec((None, tr, c), lambda layer, i: (layer, i, 0))
    return _call(body, name, (nl, r // tr), [part_spec(l) for l in range(nl)] + [row, row, row],
                 [row] * 4, [_sds((nl, r, c), F32)] * 4)(*parts, w, m, v)


def _exchange(name, srcs, out_shapes, items):
    n_in = len(srcs)
    n_out = len(out_shapes)
    n_items = len(items)

    def body(*refs):
        ins = refs[:n_in]
        outs = refs[n_in:n_in + n_out]
        send_sems, recv_sems, local_sems = refs[n_in + n_out:]
        x, y, c = lax.axis_index("x"), lax.axis_index("y"), lax.axis_index("c")
        me = 4 * x + 2 * y + c

        def pick(ref, sel, peer):
            idx = tuple(peer if s == "peer" else me if s == "me" else s for s in sel)
            return ref.at[idx] if idx else ref

        copies = []
        for it, (si, ssel, oi, osel) in enumerate(items):
            local = pltpu.make_async_copy(pick(ins[si], ssel, me), pick(outs[oi], osel, me), local_sems.at[it])
            local.start()
            copies.append(local)
            for j in range(1, N_DEV):
                px = 1 - x if j & 4 else x
                py = 1 - y if j & 2 else y
                pc = 1 - c if j & 1 else c
                peer = 4 * px + 2 * py + pc
                sem = it * (N_DEV - 1) + j - 1
                cp = pltpu.make_async_remote_copy(
                    src_ref=pick(ins[si], ssel, peer), dst_ref=pick(outs[oi], osel, peer),
                    send_sem=send_sems.at[sem], recv_sem=recv_sems.at[sem],
                    device_id=(px, py, pc), device_id_type=pl.DeviceIdType.MESH)
                cp.start()
                copies.append(cp)
        for cp in copies:
            cp.wait()

    any_spec = pl.BlockSpec(memory_space=pl.ANY)
    return pl.pallas_call(
        body, name=name, in_specs=[any_spec] * n_in, out_specs=[any_spec] * n_out, out_shape=list(out_shapes),
        scratch_shapes=[pltpu.SemaphoreType.DMA((n_items * (N_DEV - 1),)),
                        pltpu.SemaphoreType.DMA((n_items * (N_DEV - 1),)),
                        pltpu.SemaphoreType.DMA((n_items,))],
        compiler_params=pltpu.CompilerParams(has_side_effects=True),
    )(*srcs)


def _sel(ref, sel, slot=None):
    idx = tuple(slot if s == "slot" else s for s in sel)
    return ref.at[idx] if idx else ref


def _gather_plan(srcs, out_shapes, items):
    per = N_DEV - 1
    n_items = len(items)

    def phases(ins, outs, send_sems, recv_sems, local_sems, total):
        x, y, c = lax.axis_index("x"), lax.axis_index("y"), lax.axis_index("c")
        sibling = (x, y, 1 - c)
        chips = [(1 - x, y), (x, 1 - y), (1 - x, 1 - y)]
        slot_of = lambda px, py, pc: 4 * px + 2 * py + pc

        def copy(it, k, block, to, src=None):
            _, _, oi, osel = items[it]
            dst = _sel(outs[oi], osel, slot_of(*block))
            return pltpu.make_async_remote_copy(
                src_ref=dst if src is None else src, dst_ref=dst,
                send_sem=send_sems.at[it * per + k], recv_sem=recv_sems.at[it * per + k],
                device_id=to, device_id_type=pl.DeviceIdType.MESH)

        def own_copies(it):
            si, ssel, oi, osel = items[it]
            src = _sel(ins[si], ssel)
            mine = pltpu.make_async_copy(src, _sel(outs[oi], osel, slot_of(x, y, c)), local_sems.at[it])
            first = [copy(it, 0, (x, y, c), sibling, src=src)]
            first += [copy(it, 1 + j, (x, y, c), (*chip, c), src=src) for j, chip in enumerate(chips)]
            return mine, first

        def start():
            for it in range(n_items):
                mine, first = own_copies(it)
                mine.start()
                for cp in first:
                    cp.start()

        def forward(it):
            def go():
                for j, chip in enumerate(chips):
                    copy(it, 1 + j, (*chip, c), (x, y, c)).wait_recv()
                    copy(it, 4 + j, (*chip, c), sibling).start()
            return go

        def finish():
            for it in range(n_items):
                copy(it, 0, sibling, (x, y, c)).wait_recv()
                for j, chip in enumerate(chips):
                    copy(it, 4 + j, (*chip, 1 - c), (x, y, c)).wait_recv()
            for it in range(n_items):
                mine, first = own_copies(it)
                for cp in first:
                    cp.wait_send()
                for j, chip in enumerate(chips):
                    copy(it, 4 + j, (*chip, c), sibling).wait_send()
                mine.wait()

        lo = max(3 * total // 4, 1)
        acts = [(0, start)]
        for it in range(n_items):
            acts.append((min(lo + it * (total - 1 - lo) // n_items, total - 2), forward(it)))
        acts.append((total - 1, finish))
        return acts

    return _Plan(srcs, out_shapes, n_items * per, n_items, phases)


def _scatter_plan(srcs, out_shapes, items):
    per = N_DEV - 1
    n_items = len(items)

    def phases(ins, outs, send_sems, recv_sems, local_sems, total):
        x, y, c = lax.axis_index("x"), lax.axis_index("y"), lax.axis_index("c")
        me = 4 * x + 2 * y + c

        def copies():
            out = []
            for it, (si, oi, osel) in enumerate(items):
                dst = _sel(outs[oi], osel, me)
                out.append(pltpu.make_async_copy(ins[si].at[me], dst, local_sems.at[it]))
                for j in range(1, N_DEV):
                    px = 1 - x if j & 4 else x
                    py = 1 - y if j & 2 else y
                    pc = 1 - c if j & 1 else c
                    out.append(pltpu.make_async_remote_copy(
                        src_ref=ins[si].at[4 * px + 2 * py + pc], dst_ref=dst,
                        send_sem=send_sems.at[it * per + j - 1], recv_sem=recv_sems.at[it * per + j - 1],
                        device_id=(px, py, pc), device_id_type=pl.DeviceIdType.MESH))
            return out

        def start():
            for cp in copies():
                cp.start()

        def finish():
            for cp in copies():
                cp.wait()

        return [(0, start), (total - 1, finish)]

    return _Plan(srcs, out_shapes, n_items * per, n_items, phases)


def _sum_slots(parts):
    _, r, c = parts.shape

    def body(p_ref, o_ref):
        acc = p_ref[0]
        for dev in range(1, N_DEV):
            acc = acc + p_ref[dev]
        o_ref[...] = acc

    return _call(body, "sum_slots", (1,), [pl.BlockSpec((N_DEV, r, c), lambda i: (0, 0, 0))],
                 pl.BlockSpec((r, c), lambda i: (0, 0)), _sds((r, c), F32))(parts)


def _pack(arrays):
    rows = []
    for a in arrays:
        flat = a.reshape(-1).astype(F32)
        pad = (-flat.shape[0]) % 1024
        if pad:
            flat = jnp.concatenate([flat, jnp.zeros((pad,), F32)])
        rows.append(flat.reshape(-1, 128))
    return jnp.concatenate(rows, axis=0)


def _unpack(buf, shapes):
    out = []
    r0 = 0
    for shp in shapes:
        size = int(np.prod(shp))
        nrows = -(-size // 1024) * 8
        out.append(buf[r0:r0 + nrows].reshape(-1)[:size].reshape(shp))
        r0 += nrows
    return out


def _block_diag_chunks(val):
    eye = jnp.eye(8, dtype=val.dtype)
    return jnp.einsum("cjhp,jk->cjhkp", val, eye).reshape(SSM_CHUNKS, CHUNK_CH, CHUNK_ST)


def _diag_blocks(dense):
    d5 = dense.reshape(SSM_CHUNKS, 8, SSM_GROUP, 8, SSM_STATE)
    return jnp.stack([d5[:, j, :, j, :] for j in range(8)], axis=1)


def _carried(res, comm):
    return res if comm is not None else (res, None)


def _ffn_fwd(tag, hin, hin_b, w_up, conv_w, conv_b, w_down4, gain, bias, seq, comm_up=None, comm_conv=None,
             comm_down=None):
    hc, got_up = _carried(_mm_nn("ffn_up" + tag, hin_b, w_up, FFN_HIDDEN, blocked_out=True, comm=comm_up), comm_up)
    nb, n, c = hc.shape
    hc = hc.reshape(2, nb // 2, n, c)
    (act, cv), got_conv = _carried(_conv_gate_fwd(hc, conv_w, conv_b, seq, comm=comm_conv), comm_conv)
    (r, hout, hout_b), got_down = _carried(
        _mm_nn_red("ffn_down" + tag, act, w_down4, F32, comm=comm_down, residual=hin, norm=(gain, bias)), comm_down)
    return hout, hout_b, (hin_b, hc, cv, act, r), (got_up, got_conv, got_down)


def _ffn_bwd(tag, saved, dh, w_up, conv_w, conv_b, w_down4, gain, seq, comm_conv=None, comm_dw=None,
             comm_dx=None):
    hin_b, hc, cv, act, r = saved
    dr, dr_b, dgain, dbias = _ln_bwd("ffn_norm_bwd" + tag, r, dh, gain)
    d_wdown = _mm_tn("ffn_down_dw" + tag, act, dr_b, act.shape[0], BF16, a_blocked=True, g_mode="shared")
    dact = _mm_nt_out("ffn_down_dx" + tag, dr_b, w_down4, FFN_HIDDEN)
    (dhc, dconv), got_conv = _carried(_conv_gate_bwd(hc, cv, dact, conv_w, seq, comm=comm_conv), comm_conv)
    dhc8 = dhc.reshape(2 * dhc.shape[1], dhc.shape[2], dhc.shape[3])
    d_wup, got_dw = _carried(_mm_tn("ffn_up_dw" + tag, dhc8, hin_b, dhc8.shape[0], BF16, a_blocked=True,
                                    g_mode="shared", comm=comm_dw), comm_dw)
    dhin, got_dx = _carried(_mm_nt_red("ffn_up_dx" + tag, dhc8, w_up, F32, g_blocked=True, add=dr,
                                       add_scale=DN_ALPHA, comm=comm_dx), comm_dx)
    return dhin, d_wup, d_wdown, dconv, dgain, dbias, (got_conv, got_dw, got_dx)


def _local_step(x, target, p, wfull, bsz, seq, ex=None):
    n = bsz * seq
    xf = x.reshape(n, D_MODEL)
    tf = target.reshape(n, D_MODEL)

    lr = p["s5_lam_re"].reshape(1, N_STATE)
    li = p["s5_lam_im"].reshape(1, N_STATE)
    ldt = p["s5_log_dt"].reshape(1, D_MODEL // SSM_GROUP)
    br_t = p["s5_b_re"].reshape(N_STATE, SSM_GROUP).T
    bi_t = p["s5_b_im"].reshape(N_STATE, SSM_GROUP).T
    ab_r, ab_i, bb_r, bb_i, tab = _s5_prep(lr, li, ldt, br_t, bi_t)

    def bb_chunks(bb):
        return _block_diag_chunks(bb.reshape(SSM_GROUP, SSM_CHUNKS, 8, SSM_STATE).transpose(1, 2, 0, 3))

    def c_chunks(cc):
        return _block_diag_chunks(cc.reshape(SSM_CHUNKS, 8, SSM_GROUP, SSM_STATE))

    bbr_c, bbi_c = bb_chunks(bb_r).astype(BF16), bb_chunks(bb_i).astype(BF16)
    cr_c, ci_c = c_chunks(p["s5_c_re"]).astype(BF16), c_chunks(p["s5_c_im"]).astype(BF16)
    dvec = p["s5_d"].reshape(1, D_MODEL)

    if ex is None:
        y, h_r, h_i = _s5_scan_fwd(xf, bbr_c, bbi_c, cr_c, ci_c, dvec, tab, seq)
    else:
        (y, h_r, h_i), gathered = _s5_scan_fwd(xf, bbr_c, bbi_c, cr_c, ci_c, dvec, tab, seq,
                                               comm=ex.gather_first())
        p, wfull = ex.take_first(gathered, p)
    ln_g = p["ln_gain"].reshape(4, 1, D_MODEL)
    ln_b = p["ln_bias"].reshape(4, 1, D_MODEL)
    y2, z, gg = _glu_fwd(y, wfull["s5_w_glu"], p["s5_b_glu"].reshape(1, D_MODEL))
    r1, h1, h1b = _mm_nn("s5_out", gg, wfull["s5_w_out"][None], F32, residual=xf, norm=(ln_g[0], ln_b[0]))
    plans = [ex.gather_attn(k) if ex else None for k in ("kv0", "q", "kv1")]
    h2, h2b, ffn0_saved, got = _ffn_fwd(
        "0", h1, h1b, wfull["ffn_w_up"][0], p["conv_w"][0], p["conv_b"][0], wfull["ffn_w_down"][0],
        ln_g[1], ln_b[1], seq, comm_up=plans[0], comm_conv=plans[1], comm_down=plans[2])
    if ex is not None:
        wfull = ex.take_attn({"kv0": got[0][0], "q": got[1][0], "kv1": got[2][0]}, wfull)

    plan = ex.gather_layer1("up", 0) if ex else None
    kmat, got = _carried(_mm_nn("attn_k", h2b, wfull["attn_w_kv"], BF16, blocks=(0, 4), comm=plan), plan)
    halves = {"up": [got[0]], "down": []} if ex else None
    vmat = _mm_nn("attn_v", h2b, wfull["attn_w_kv"], BF16, blocks=(4, 4))
    plan = ex.gather_attn("out") if ex else None
    qmat, got = _carried(_mm_nn("attn_q", h2b, wfull["attn_w_q"], BF16, comm=plan), plan)
    if ex is not None:
        wfull = ex.take_attn({"out": got[0]}, wfull)
    biases, outs, lses = [], [], []
    for g, dil in enumerate(DILATIONS):
        rel_t = p["rel_bias"][:, g * N_HEADS:(g + 1) * N_HEADS].T
        biases.append(_bias_expand(rel_t, dil).reshape(N_HEADS, BAND, 2 * BAND))
        which, half = {1: ("down", 0), 4: ("down", 1), 16: ("up", 1)}[dil]
        plan = ex.gather_layer1(which, half) if ex else None
        merge = (outs[0], lses[0], outs[1], lses[1]) if dil == DILATIONS[-1] else ()
        (o_g, l_g), got = _carried(_attn_fwd(qmat, kmat, vmat, biases[g], g, dil, bsz, seq, comm=plan,
                                             merge=merge), plan)
        if plan is not None:
            halves[which].append(got[0])
        outs.append(o_g)
        lses.append(l_g)
    if ex is not None:
        wfull = ex.take_layer1(halves, wfull)
    o, lse = outs[-1], lses[-1]
    r3, h3, h3b = _mm_nn("attn_out", o, wfull["attn_w_out"][None], F32, residual=h2, norm=(ln_g[2], ln_b[2]))
    h4, _, ffn1_saved, _ = _ffn_fwd("1", h3, h3b, wfull["ffn_w_up"][1], p["conv_w"][1], p["conv_b"][1],
                                    wfull["ffn_w_down"][1], ln_g[3], ln_b[3], seq)
    loss_tile, dh4 = _loss_head(h4, tf)

    grads = {}
    dh3, grads["ffn_w_up1"], grads["ffn_w_down1"], dconv1, dg3, db3, _ = _ffn_bwd(
        "1", ffn1_saved, dh4, wfull["ffn_w_up"][1], p["conv_w"][1], p["conv_b"][1], wfull["ffn_w_down"][1],
        ln_g[3], seq)
    dr3, dr3b, dg2, db2 = _ln_bwd("attn_norm_bwd", r3, dh3, ln_g[2])
    grads["attn_w_out"] = _mm_tn("attn_out_dw", o, dr3b, 1, BF16, g_mode="shared")
    do = _mm_nt_red("attn_out_dx", dr3b, wfull["attn_w_out"][None], F32, g_blocked=False)
    dqkv, drel = (), []
    for g, dil in enumerate(DILATIONS):
        plan = ex.scatter(grads, ("ffn_w_up1", "ffn_w_down1")) if ex and dil == DILATIONS[-1] else None
        res, got = _carried(_attn_bwd(qmat, kmat, vmat, do, o, lse, biases[g], g, dil, bsz, seq, comm=plan,
                                      into=dqkv), plan)
        if plan is not None:
            ex.keep(plan, got)
        dqkv, db_g = tuple(res[:3]), res[3]
        drel.append(_bias_reduce(db_g.reshape(N_HEADS, BAND * 2 * BAND), dil).T)
    dq, dk, dv = dqkv
    grads["rel_bias"] = jnp.concatenate(drel, axis=1)
    grads["attn_w_q"] = _mm_tn("attn_q_dw", h2b, dq, N_DEV, BF16)
    grads["attn_w_kv"] = jnp.concatenate([_mm_tn("attn_k_dw", h2b, dk, 4, BF16),
                                          _mm_tn("attn_v_dw", h2b, dv, 4, BF16)], axis=0)
    dh2 = _mm_nt_red("attn_q_dx", dq, wfull["attn_w_q"], F32, g_blocked=False, add=dr3, add_scale=DN_ALPHA)
    dh2 = _mm_nt_red("attn_k_dx", dk, wfull["attn_w_kv"], F32, g_blocked=False, add=dh2, blocks=(0, 4))
    dh2 = _mm_nt_red("attn_v_dx", dv, wfull["attn_w_kv"], F32, g_blocked=False, add=dh2, blocks=(4, 4))

    plans = [ex.scatter(grads, names) if ex else None
             for names in (("attn_w_kv#0", "attn_w_out"), ("attn_w_kv#1",), ("attn_w_q",))]
    dh1, grads["ffn_w_up0"], grads["ffn_w_down0"], dconv0, dg1, db1, got = _ffn_bwd(
        "0", ffn0_saved, dh2, wfull["ffn_w_up"][0], p["conv_w"][0], p["conv_b"][0], wfull["ffn_w_down"][0],
        ln_g[1], seq, comm_conv=plans[0], comm_dw=plans[1], comm_dx=plans[2])
    if ex is not None:
        for plan, outs in zip(plans, got):
            ex.keep(plan, outs)
    dr1, dr1b, dg0, db0 = _ln_bwd("s5_norm_bwd", r1, dh1, ln_g[0])
    grads["s5_w_out"] = _mm_tn("s5_out_dw", gg, dr1b, 1, BF16, g_mode="shared")
    dgg = _mm_nt_red("s5_out_dx", dr1b, wfull["s5_w_out"][None], F32, g_blocked=False)
    dz, d_bglu, dy2_direct = _glu_bwd_gate(y, z, dgg)
    grads["s5_w_glu"] = _mm_tn("s5_glu_dw", y2, dz, 1, BF16, g_mode="shared")
    dy = _mm_nt_red("s5_glu_dx", dz, wfull["s5_w_glu"][None], F32, g_blocked=False, add=dy2_direct,
                    post=(_gelu_bwd, y))
    plan_last = ex.scatter(grads, ("ffn_w_up0", "ffn_w_down0", "s5_w_out", "s5_w_glu")) if ex else None
    res = _s5_scan_bwd(xf, dy, dr1, h_r, h_i, bbr_c, bbi_c, cr_c, ci_c, dvec, tab, seq, comm=plan_last)
    if ex is not None:
        res, got = res
        ex.keep(plan_last, got)
    dx, dcr, dci, dbr, dbi, dar, dai, dd = res

    def bb_from_chunks(dense):
        return _diag_blocks(dense).transpose(2, 0, 1, 3).reshape(SSM_GROUP, N_STATE)

    d_lr, d_li, d_ldt, d_br, d_bi = _s5_prep_bwd(lr, li, ldt, br_t, bi_t, dar, dai,
                                                 bb_from_chunks(dbr), bb_from_chunks(dbi))
    grads["s5_lam_re"] = d_lr.reshape(p["s5_lam_re"].shape)
    grads["s5_lam_im"] = d_li.reshape(p["s5_lam_im"].shape)
    grads["s5_log_dt"] = d_ldt.reshape(p["s5_log_dt"].shape)
    grads["s5_b_re"] = d_br.T.reshape(p["s5_b_re"].shape)
    grads["s5_b_im"] = d_bi.T.reshape(p["s5_b_im"].shape)
    grads["s5_c_re"] = _diag_blocks(dcr).reshape(p["s5_c_re"].shape)
    grads["s5_c_im"] = -_diag_blocks(dci).reshape(p["s5_c_im"].shape)
    grads["s5_d"] = dd.reshape(p["s5_d"].shape)
    grads["s5_b_glu"] = d_bglu.reshape(1, D_MODEL)
    dconv = jnp.stack([dconv0, dconv1]).reshape(2, N_DEV, 8, -1)
    grads["ffn_conv_w"] = dconv[:, :, 0:3, :].transpose(0, 2, 1, 3)
    grads["ffn_conv_b"] = dconv[:, :, 3, :]
    grads["ln_gain"] = jnp.stack([dg0, dg1, dg2, dg3]).reshape(2, 2, D_MODEL)
    grads["ln_bias"] = jnp.stack([db0, db1, db2, db3]).reshape(2, 2, D_MODEL)
    return loss_tile[0, 0], dx.reshape(x.shape), grads


SMALL_REPLICATED = ("s5_lam_re", "s5_lam_im", "s5_log_dt", "s5_b_re", "s5_b_im", "s5_c_re", "s5_c_im", "s5_d",
                    "rel_bias", "ffn_conv_b")
SMALL_SHARDED = ("s5_b_glu", "ffn_conv_w", "ln_gain", "ln_bias")
BIG = ("s5_w_glu", "s5_w_out", "attn_w_kv", "attn_w_q", "attn_w_out", "ffn_w_up", "ffn_w_down")
WEIGHTS = ("s5_lam_re", "s5_lam_im", "s5_log_dt", "s5_b_re", "s5_b_im", "s5_c_re", "s5_c_im", "s5_d", "s5_w_glu",
           "s5_b_glu", "s5_w_out", "attn_w_kv", "attn_w_q", "attn_w_out", "rel_bias", "ffn_w_up", "ffn_conv_w",
           "ffn_conv_b", "ffn_w_down", "ln_gain", "ln_bias")


class _Exchanges:
    def __init__(self, w):
        self.w = w
        self.parts = {}
        self.up = w["ffn_w_up"].astype(BF16)
        self.down = w["ffn_w_down"].astype(BF16)

    def gather_first(self):
        w = self.w
        srcs = [w["s5_w_glu"][0].astype(BF16), w["s5_w_out"][0].astype(BF16),
                _pack([w[k] for k in SMALL_SHARDED]), self.up, self.down]
        outs = [_sds((N_DEV,) + (s.shape[1:] if i in (3, 4) else s.shape), s.dtype) for i, s in enumerate(srcs)]
        items = [(i, (0,) if i in (3, 4) else (), i, ("slot",)) for i in range(len(srcs))]
        return _gather_plan(srcs, outs, items)

    def gather_attn(self, piece):
        w = self.w
        if piece in ("kv0", "kv1"):
            src, rows = w["attn_w_kv"].astype(BF16), w["attn_w_kv"].shape[0] // 2
            return _gather_plan([src], [_sds((N_DEV, rows, src.shape[1]), BF16)],
                                [(0, (pl.ds(int(piece[2]) * rows, rows),), 0, ("slot",))])
        src = (w["attn_w_q"] if piece == "q" else w["attn_w_out"])[0].astype(BF16)
        return _gather_plan([src], [_sds((N_DEV,) + src.shape, BF16)], [(0, (), 0, ("slot",))])

    def take_attn(self, pieces, wfull):
        wfull = dict(wfull)
        if "kv0" in pieces:
            wfull["attn_w_kv"] = jnp.concatenate([pieces["kv0"], pieces["kv1"]], axis=1)
            wfull["attn_w_q"] = pieces["q"]
        if "out" in pieces:
            wfull["attn_w_out"] = pieces["out"].reshape(D_MODEL, D_MODEL)
        return wfull

    def take_first(self, g, p):
        d = D_MODEL
        w = self.w
        wfull = {
            "s5_w_glu": g[0].reshape(d, d),
            "s5_w_out": g[1].reshape(d, d),
            "ffn_w_up": [g[3], None],
            "ffn_w_down": [g[4].reshape(4, -1, d), None],
        }
        smalls = [_unpack(g[2][dev], [w[k].shape for k in SMALL_SHARDED]) for dev in range(N_DEV)]
        c_ff = w["ffn_conv_w"].shape[2]
        conv_w = jnp.stack([s[1] for s in smalls], axis=2)
        p = dict(p)
        p.update(s5_b_glu=jnp.concatenate([s[0] for s in smalls], axis=1),
                 ln_gain=jnp.concatenate([s[2] for s in smalls], axis=2),
                 ln_bias=jnp.concatenate([s[3] for s in smalls], axis=2),
                 conv_w=conv_w.transpose(0, 2, 1, 3).reshape(2, 2, 4, 3, c_ff),
                 conv_b=w["ffn_conv_b"].reshape(2, 2, 4, 1, c_ff))
        return p, wfull

    def gather_layer1(self, which, half):
        src = self.up if which == "up" else self.down
        rows = src.shape[1] // 2
        return _gather_plan([src], [_sds((N_DEV, rows) + src.shape[2:], BF16)],
                            [(0, (1, pl.ds(half * rows, rows)), 0, ("slot",))])

    def take_layer1(self, halves, wfull):
        up = jnp.concatenate(halves["up"], axis=1)
        down = jnp.concatenate(halves["down"], axis=1)
        wfull = dict(wfull)
        wfull["ffn_w_up"] = [wfull["ffn_w_up"][0], up]
        wfull["ffn_w_down"] = [wfull["ffn_w_down"][0], down.reshape(4, -1, D_MODEL)]
        return wfull

    def scatter(self, grads, names):
        srcs = []
        for k in names:
            name, _, half = k.partition("#")
            g = grads[name].reshape(N_DEV, -1, grads[name].shape[-1])
            if half:
                rows = g.shape[1] // 2
                g = g[:, int(half) * rows:(int(half) + 1) * rows]
            srcs.append(g)
        plan = _scatter_plan(srcs, [_sds(s.shape, BF16) for s in srcs],
                             [(i, i, ("slot",)) for i in range(len(names))])
        plan.names = names
        return plan

    def keep(self, plan, outs):
        for k, o in zip(plan.names, outs):
            self.parts[k] = o


def kernel(x, s5_lam_re, s5_lam_im, s5_log_dt, s5_b_re, s5_b_im, s5_c_re, s5_c_im, s5_d, s5_w_glu, s5_b_glu, s5_w_out, attn_w_kv, attn_w_q, attn_w_out, rel_bias, ffn_w_up, ffn_conv_w, ffn_conv_b, ffn_w_down, ln_gain, ln_bias, loss_target, m_s5_lam_re, m_s5_lam_im, m_s5_log_dt, m_s5_b_re, m_s5_b_im, m_s5_c_re, m_s5_c_im, m_s5_d, m_s5_w_glu, m_s5_b_glu, m_s5_w_out, m_attn_w_kv, m_attn_w_q, m_attn_w_out, m_rel_bias, m_ffn_w_up, m_ffn_conv_w, m_ffn_conv_b, m_ffn_w_down, m_ln_gain, m_ln_bias, v_s5_lam_re, v_s5_lam_im, v_s5_log_dt, v_s5_b_re, v_s5_b_im, v_s5_c_re, v_s5_c_im, v_s5_d, v_s5_w_glu, v_s5_b_glu, v_s5_w_out, v_attn_w_kv, v_attn_w_q, v_attn_w_out, v_rel_bias, v_ffn_w_up, v_ffn_conv_w, v_ffn_conv_b, v_ffn_w_down, v_ln_gain, v_ln_bias):
    w = dict(s5_lam_re=s5_lam_re, s5_lam_im=s5_lam_im, s5_log_dt=s5_log_dt, s5_b_re=s5_b_re, s5_b_im=s5_b_im,
             s5_c_re=s5_c_re, s5_c_im=s5_c_im, s5_d=s5_d, s5_w_glu=s5_w_glu, s5_b_glu=s5_b_glu, s5_w_out=s5_w_out,
             attn_w_kv=attn_w_kv, attn_w_q=attn_w_q, attn_w_out=attn_w_out, rel_bias=rel_bias, ffn_w_up=ffn_w_up,
             ffn_conv_w=ffn_conv_w, ffn_conv_b=ffn_conv_b, ffn_w_down=ffn_w_down, ln_gain=ln_gain, ln_bias=ln_bias)
    mom = dict(s5_lam_re=m_s5_lam_re, s5_lam_im=m_s5_lam_im, s5_log_dt=m_s5_log_dt, s5_b_re=m_s5_b_re,
               s5_b_im=m_s5_b_im, s5_c_re=m_s5_c_re, s5_c_im=m_s5_c_im, s5_d=m_s5_d, s5_w_glu=m_s5_w_glu,
               s5_b_glu=m_s5_b_glu, s5_w_out=m_s5_w_out, attn_w_kv=m_attn_w_kv, attn_w_q=m_attn_w_q,
               attn_w_out=m_attn_w_out, rel_bias=m_rel_bias, ffn_w_up=m_ffn_w_up, ffn_conv_w=m_ffn_conv_w,
               ffn_conv_b=m_ffn_conv_b, ffn_w_down=m_ffn_w_down, ln_gain=m_ln_gain, ln_bias=m_ln_bias)
    var = dict(s5_lam_re=v_s5_lam_re, s5_lam_im=v_s5_lam_im, s5_log_dt=v_s5_log_dt, s5_b_re=v_s5_b_re,
               s5_b_im=v_s5_b_im, s5_c_re=v_s5_c_re, s5_c_im=v_s5_c_im, s5_d=v_s5_d, s5_w_glu=v_s5_w_glu,
               s5_b_glu=v_s5_b_glu, s5_w_out=v_s5_w_out, attn_w_kv=v_attn_w_kv, attn_w_q=v_attn_w_q,
               attn_w_out=v_attn_w_out, rel_bias=v_rel_bias, ffn_w_up=v_ffn_w_up, ffn_conv_w=v_ffn_conv_w,
               ffn_conv_b=v_ffn_conv_b, ffn_w_down=v_ffn_w_down, ln_gain=v_ln_gain, ln_bias=v_ln_bias)
    bsz, seq, _ = x.shape
    me = 4 * lax.axis_index("x") + 2 * lax.axis_index("y") + lax.axis_index("c")

    ex = _Exchanges(w)
    loss_part, grad_x, g = _local_step(x, loss_target, dict(w), None, bsz, seq, ex=ex)
    loss = lax.psum(loss_part, ("x", "y", "c"))

    d = D_MODEL
    results = {}
    for name in BIG:
        shard = w[name]
        if name == "ffn_w_up":
            parts = [ex.parts[name + "0"], ex.parts[name + "1"]]
            outs4 = _adamw("adamw_" + name, parts, *(jnp.swapaxes(a, 1, 2) for a in (shard, mom[name], var[name])))
            results[name] = tuple(jnp.swapaxes(o, 1, 2) for o in outs4)
            continue
        if name == "ffn_w_down":
            parts, r3 = [ex.parts[name + "0"], ex.parts[name + "1"]], shard.shape
        elif name == "attn_w_kv":
            parts = [jnp.concatenate([ex.parts[name + "#0"], ex.parts[name + "#1"]], axis=1)]
            r3 = (1, -1, shard.shape[-1])
        else:
            parts, r3 = [ex.parts[name]], (1, -1, shard.shape[-1])
        outs4 = _adamw("adamw_" + name, parts, shard.reshape(r3), mom[name].reshape(r3), var[name].reshape(r3))
        results[name] = tuple(o.reshape(shard.shape) for o in outs4)

    small_names = SMALL_REPLICATED + SMALL_SHARDED
    packed = _pack([g[k] for k in small_names])
    pad = (-packed.shape[0]) % (8 * N_DEV)
    if pad:
        packed = jnp.concatenate([packed, jnp.zeros((pad, 128), F32)], axis=0)
    piece = packed.shape[0] // N_DEV
    slots = _exchange("reduce_small", [packed.reshape(N_DEV, piece, 128)], [_sds((N_DEV, piece, 128), F32)],
                      [(0, ("peer",), 0, ("me",))])[0]
    summed = _exchange("gather_small", [_sum_slots(slots)], [_sds((N_DEV, piece, 128), F32)],
                       [(0, (), 0, ("me",))])[0]
    total = _unpack(summed.reshape(N_DEV * piece, 128), [g[k].shape for k in small_names])
    gsum = dict(zip(small_names, total))
    mine = {k: gsum[k] for k in SMALL_REPLICATED}
    mine["s5_b_glu"] = lax.dynamic_slice_in_dim(gsum["s5_b_glu"], me * (d // N_DEV), d // N_DEV, axis=1)
    mine["ffn_conv_w"] = lax.dynamic_index_in_dim(gsum["ffn_conv_w"], me, axis=2, keepdims=False)
    mine["ln_gain"] = lax.dynamic_slice_in_dim(gsum["ln_gain"], me * (d // N_DEV), d // N_DEV, axis=2)
    mine["ln_bias"] = lax.dynamic_slice_in_dim(gsum["ln_bias"], me * (d // N_DEV), d // N_DEV, axis=2)
    gp = _pack([mine[k] for k in small_names])
    outs4 = _adamw("adamw_small", [gp[None]], _pack([w[k] for k in small_names])[None],
                   _pack([mom[k] for k in small_names])[None], _pack([var[k] for k in small_names])[None])
    outs4 = [o[0] for o in outs4]
    shapes = [w[k].shape for k in small_names]
    unpacked = [_unpack(o, shapes) for o in outs4]
    for i, k in enumerate(small_names):
        results[k] = tuple(unpacked[j][i] for j in range(4))

    out = [loss, grad_x]
    for j in range(4):
        out.extend(results[k][j] for k in WEIGHTS)
    return tuple(out)
```

```python
import math

import numpy as np
import jax
import jax.numpy as jnp
from jax import lax
from jax.experimental import pallas as pl
from jax.experimental.pallas import tpu as pltpu

F32 = jnp.float32
BF16 = jnp.bfloat16
HIGHEST = lax.Precision.HIGHEST

N_DEV = 8
D_MODEL = 1024
SSM_GROUP = 16
SSM_STATE = 64
SSM_CHUNKS = 8
CHUNK_CH = D_MODEL // SSM_CHUNKS
CHUNK_ST = CHUNK_CH // SSM_GROUP * SSM_STATE
N_STATE = D_MODEL // SSM_GROUP * SSM_STATE
HEAD_DIM = 64
N_HEADS = 16
BAND = 128
DILATIONS = (1, 4, 16)
REL_BUCKETS = 32
REL_MAX_DIST = 2048
NEG_BIG = -1e30
DN_ALPHA = (2.0 * 2) ** 0.25
LN_EPS = 1e-5
ADAM_LR = 0.001
ADAM_B1 = 0.9
ADAM_B2 = 0.999
ADAM_EPS = 1e-08
ADAM_WD = 0.01
ADAM_STEP = 10

TM = 512
TM_MM = 2048
TM_MM_F32 = 1024
T_SCAN = 1024
VMEM_LIMIT = 56 * 1024 * 1024
FFN_HIDDEN = BF16

NN = (((1,), (0,)), ((), ()))
NT = (((1,), (1,)), ((), ()))
TN = (((0,), (0,)), ((), ()))


class _Plan:
    def __init__(self, srcs, out_shapes, n_sems, n_local, phases):
        self.srcs, self.out_shapes, self.n_sems, self.n_local, self.phases = srcs, out_shapes, n_sems, n_local, phases


def _call(body, name, grid, in_specs, out_specs, out_shape, scratch=(), comm=None, fill=()):
    params = dict(dimension_semantics=("arbitrary",) * len(grid), vmem_limit_bytes=VMEM_LIMIT)
    if comm is None and not fill:
        return pl.pallas_call(
            body, name=name, grid=grid, in_specs=in_specs, out_specs=out_specs, out_shape=out_shape,
            scratch_shapes=list(scratch), compiler_params=pltpu.CompilerParams(**params))
    single = not isinstance(out_specs, (list, tuple))
    o_specs = [out_specs] if single else list(out_specs)
    o_shape = [out_shape] if single else list(out_shape)
    n_in, n_out, n_scr, n_fill = len(in_specs), len(o_specs), len(scratch), len(fill)
    srcs = list(comm.srcs) if comm else []
    c_shapes = list(comm.out_shapes) if comm else []
    n_cin, n_cout = len(srcs), len(c_shapes)
    total = int(np.prod(grid))

    def wrapped(*refs):
        ins = refs[:n_in]
        refs = refs[n_in + n_fill:]
        cins, outs, couts = refs[:n_cin], refs[n_cin:n_cin + n_out], refs[n_cin + n_out:n_cin + n_out + n_cout]
        rest = refs[n_cin + n_out + n_cout:]
        scr, sems = rest[:n_scr], rest[n_scr:]
        if comm is None:
            body(*ins, *outs, *scr)
            return
        step = pl.program_id(0)
        for ax in range(1, len(grid)):
            step = step * grid[ax] + pl.program_id(ax)
        phases = comm.phases(cins, couts, *sems, total)
        for at, action in phases:
            if at == 0:
                pl.when(step == 0)(action)
        body(*ins, *outs, *scr)
        for at, action in phases:
            if at > 0:
                pl.when(step == at)(action)

    any_spec = pl.BlockSpec(memory_space=pl.ANY)
    sems = [] if comm is None else [pltpu.SemaphoreType.DMA((comm.n_sems,)), pltpu.SemaphoreType.DMA((comm.n_sems,)),
                                    pltpu.SemaphoreType.DMA((max(comm.n_local, 1),))]
    call = pl.pallas_call(
        wrapped, name=name, grid=grid, in_specs=list(in_specs) + [any_spec] * (n_fill + n_cin),
        out_specs=o_specs + [any_spec] * n_cout, out_shape=o_shape + c_shapes,
        scratch_shapes=list(scratch) + sems, input_output_aliases={n_in + j: j for j in range(n_fill)},
        compiler_params=pltpu.CompilerParams(has_side_effects=comm is not None, **params))

    def run(*args):
        res = call(*args, *fill, *srcs)
        own = res[0] if single else res[:n_out]
        return (own, res[n_out:]) if comm is not None else own

    return run


def _sds(shape, dtype):
    return jax.ShapeDtypeStruct(shape, dtype)


def _mm(name, a, b, *, dims, grid, a_spec, b_spec, o_spec, out_shape, nred=1, red_axis=0,
        add=None, add_spec=None, add_scale=1.0, comm=None, norm=None, inner=None, post=None):
    has_add = add is not None
    n_extra = (1 if has_add else 0) + (1 if post else 0) + (2 if norm else 0)
    n_out = 3 if norm else 1
    acc_shape = tuple(s for s in o_spec.block_shape if s is not None)

    def body(*refs):
        a_ref, b_ref = refs[:2]
        extra = refs[2:2 + n_extra]
        outs = refs[2 + n_extra:2 + n_extra + n_out]
        rest = refs[2 + n_extra + n_out:]
        o_ref = outs[0]
        if inner is None:
            prod = lax.dot_general(a_ref[...].astype(BF16), b_ref[...].astype(BF16), dims,
                                   preferred_element_type=F32)
        else:
            nb, a_mode = inner
            width = a_ref.shape[-1] // nb
            prod = None
            for d in range(nb):
                a_d = a_ref[d] if a_mode == "lead" else a_ref[:, d * width:(d + 1) * width]
                part = lax.dot_general(a_d.astype(BF16), b_ref[d].astype(BF16), dims, preferred_element_type=F32)
                prod = part if prod is None else prod + part

        def finish(val):
            if has_add:
                val = val + add_scale * extra[0][...].astype(F32)
            if post:
                val = post[0](val, extra[1 if has_add else 0][...])
            o_ref[...] = val.astype(o_ref.dtype)
            if norm:
                xhat, _ = _ln_stats(val)
                y = xhat * extra[-2][...] + extra[-1][...]
                outs[1][...] = y
                outs[2][...] = y.astype(BF16)

        if nred == 1:
            finish(prod)
        else:
            acc = rest[0]
            k = pl.program_id(red_axis)

            @pl.when(k == 0)
            def _():
                acc[...] = prod

            @pl.when(k > 0)
            def _():
                acc[...] += prod

            @pl.when(k == nred - 1)
            def _():
                finish(acc[...])

    in_specs = [a_spec, b_spec] + ([add_spec] if has_add else [])
    scratch = [pltpu.VMEM(acc_shape, F32)] if nred > 1 else []
    args = (a, b) + ((add,) if has_add else ())
    if post:
        in_specs.append(o_spec)
        args += (post[1],)
    if norm:
        vec = pl.BlockSpec((1, out_shape.shape[1]), lambda *_: (0, 0))
        in_specs += [vec, vec]
        args += tuple(norm)
        o_spec = [o_spec, o_spec, o_spec]
        out_shape = [out_shape, out_shape, _sds(out_shape.shape, BF16)]
    return _call(body, name, grid, in_specs, o_spec, out_shape, scratch, comm=comm)(*args)


def _mm_nn(name, a, b, out_dtype, blocked_out=False, comm=None, blocks=None, residual=None, norm=None):
    TM = TM_MM if out_dtype == BF16 else TM_MM_F32
    m, k = a.shape
    _, _, n = b.shape
    nb = b.shape[0] if blocks is None else blocks[1]
    first = 0 if blocks is None else blocks[0]
    if blocked_out:
        o_spec = pl.BlockSpec((None, TM, n), lambda d, i: (d, i, 0))
        out_shape = _sds((nb, m, n), out_dtype)
    else:
        o_spec = pl.BlockSpec((TM, n), lambda d, i: (i, d))
        out_shape = _sds((m, nb * n), out_dtype)
    return _mm(name, a, b, dims=NN, grid=(nb, m // TM),
               a_spec=pl.BlockSpec((TM, k), lambda d, i: (i, 0)),
               b_spec=pl.BlockSpec((None, k, n), lambda d, i: (d + first, 0, 0)),
               o_spec=o_spec, out_shape=out_shape, comm=comm, norm=norm,
               add=residual, add_spec=o_spec if residual is not None else None, add_scale=DN_ALPHA)


def _mm_nn_red(name, a, b, out_dtype, comm=None, residual=None, norm=None):
    nb, m, k = a.shape
    n = b.shape[2]
    o_spec = pl.BlockSpec((TM, n), lambda i: (i, 0))
    return _mm(name, a, b, dims=NN, grid=(m // TM,), inner=(nb, "lead"),
               a_spec=pl.BlockSpec((nb, TM, k), lambda i: (0, i, 0)),
               b_spec=pl.BlockSpec((nb, k, n), lambda i: (0, 0, 0)),
               o_spec=o_spec, out_shape=_sds((m, n), out_dtype), comm=comm, norm=norm,
               add=residual, add_spec=o_spec if residual is not None else None, add_scale=DN_ALPHA)


def _mm_tn(name, a, g, nb, out_dtype, a_blocked=False, g_mode="cols", comm=None):
    TM = TM_MM
    if a_blocked:
        _, m, ka = a.shape
        a_spec = pl.BlockSpec((None, TM, ka), lambda d, i: (d, i, 0))
    else:
        m, ka = a.shape
        a_spec = pl.BlockSpec((TM, ka), lambda d, i: (i, 0))
    if g_mode == "cols":
        n = g.shape[1] // nb
        g_spec = pl.BlockSpec((TM, n), lambda d, i: (i, d))
    elif g_mode == "blocked":
        n = g.shape[2]
        g_spec = pl.BlockSpec((None, TM, n), lambda d, i: (d, i, 0))
    else:
        n = g.shape[1]
        g_spec = pl.BlockSpec((TM, n), lambda d, i: (i, 0))
    nred = m // TM
    return _mm(name, a, g, dims=TN, grid=(nb, nred), nred=nred, red_axis=1,
               a_spec=a_spec, b_spec=g_spec,
               o_spec=pl.BlockSpec((None, ka, n), lambda d, i: (d, 0, 0)),
               out_shape=_sds((nb, ka, n), out_dtype), comm=comm)


def _mm_nt_red(name, g, w, out_dtype, g_blocked, add=None, add_scale=1.0, comm=None, blocks=None, post=None):
    _, k, n = w.shape
    nb = w.shape[0] if blocks is None else blocks[1]
    first = 0 if blocks is None else blocks[0]
    tm = TM_MM_F32 if nb * k * n <= 4 * 1024 * 768 else TM
    if g_blocked:
        m = g.shape[1]
        g_spec = pl.BlockSpec((nb, tm, n), lambda i: (0, i, 0))
    else:
        m = g.shape[0]
        g_spec = pl.BlockSpec((tm, nb * n), lambda i: (i, 0))
    o_spec = pl.BlockSpec((tm, k), lambda i: (i, 0))
    return _mm(name, g, w, dims=NT, grid=(m // tm,), inner=(nb, "lead" if g_blocked else "cols"),
               a_spec=g_spec, b_spec=pl.BlockSpec((nb, k, n), lambda i: (first // nb, 0, 0)),
               o_spec=o_spec, out_shape=_sds((m, k), out_dtype),
               add=add, add_spec=o_spec if add is not None else None, add_scale=add_scale, comm=comm, post=post)


def _mm_nt_out(name, g, w, out_dtype):
    TM = TM_MM
    m, n = g.shape
    nb, k, _ = w.shape
    return _mm(name, g, w, dims=NT, grid=(nb, m // TM),
               a_spec=pl.BlockSpec((TM, n), lambda d, i: (i, 0)),
               b_spec=pl.BlockSpec((None, k, n), lambda d, i: (d, 0, 0)),
               o_spec=pl.BlockSpec((None, TM, k), lambda d, i: (d, i, 0)),
               out_shape=_sds((nb, m, k), out_dtype))


def _ln_stats(r):
    mu = jnp.mean(r, axis=-1, keepdims=True)
    xc = r - mu
    var = jnp.mean(xc * xc, axis=-1, keepdims=True)
    rstd = lax.rsqrt(var + LN_EPS)
    return xc * rstd, rstd


def _ln_bwd(name, r, dy, gain):
    n, d = r.shape

    def body(r_ref, dy_ref, g_ref, dr_ref, drb_ref, dg_ref, db_ref):
        i = pl.program_id(0)
        xhat, rstd = _ln_stats(r_ref[...])
        dy_v = dy_ref[...]
        dxh = dy_v * g_ref[...]
        m1 = jnp.mean(dxh, axis=-1, keepdims=True)
        m2 = jnp.mean(dxh * xhat, axis=-1, keepdims=True)
        dr = rstd * (dxh - m1 - xhat * m2)
        dr_ref[...] = dr
        drb_ref[...] = dr.astype(BF16)
        dg = jnp.sum(dy_v * xhat, axis=0, keepdims=True)
        db = jnp.sum(dy_v, axis=0, keepdims=True)

        @pl.when(i == 0)
        def _():
            dg_ref[...] = dg
            db_ref[...] = db

        @pl.when(i > 0)
        def _():
            dg_ref[...] += dg
            db_ref[...] += db

    row = pl.BlockSpec((TM_MM_F32, d), lambda i: (i, 0))
    vec = pl.BlockSpec((1, d), lambda i: (0, 0))
    return _call(body, name, (n // TM_MM_F32,), [row, row, vec], [row, row, vec, vec],
                 [_sds((n, d), F32), _sds((n, d), BF16), _sds((1, d), F32), _sds((1, d), F32)])(r, dy, gain)


def _loss_head(y, target):
    n, d = y.shape

    def body(y_ref, t_ref, l_ref, dy_ref):
        i = pl.program_id(0)
        e = y_ref[...] - t_ref[...]
        dy_ref[...] = e * (1.0 / d)
        part = jnp.sum(jnp.sum(e * e, axis=1, keepdims=True), axis=0, keepdims=True) * (0.5 / d)
        part = jnp.broadcast_to(part, (8, 128))

        @pl.when(i == 0)
        def _():
            l_ref[...] = part

        @pl.when(i > 0)
        def _():
            l_ref[...] += part

    row = pl.BlockSpec((TM, d), lambda i: (i, 0))
    return _call(body, "loss_head", (n // TM,), [row, row],
                 [pl.BlockSpec((8, 128), lambda i: (0, 0)), row],
                 [_sds((8, 128), F32), _sds((n, d), F32)])(y, target)


def _group_expand_matrix():
    e = np.zeros((D_MODEL // SSM_GROUP, N_STATE), np.float32)
    for g in range(D_MODEL // SSM_GROUP):
        e[g, g * SSM_STATE:(g + 1) * SSM_STATE] = 1.0
    return jnp.asarray(e)


def _discretise(lr, li, ldt, br, bi, expand):
    dt = jnp.dot(jnp.exp(ldt), expand, precision=HIGHEST, preferred_element_type=F32)
    mag = jnp.exp(lr * dt)
    th = li * dt
    ab_r = mag * jnp.cos(th)
    ab_i = mag * jnp.sin(th)
    den = lr * lr + li * li
    nr = ab_r - 1.0
    co_r = (nr * lr + ab_i * li) / den
    co_i = (ab_i * lr - nr * li) / den
    bb_r = co_r * br - co_i * bi
    bb_i = co_r * bi + co_i * br
    return ab_r, ab_i, bb_r, bb_i


def _cmul(ar, ai, br, bi):
    return ar * br - ai * bi, ar * bi + ai * br


def _s5_prep(lr, li, ldt, br, bi):
    s = N_STATE
    expand = _group_expand_matrix()

    def body(lr_ref, li_ref, ldt_ref, br_ref, bi_ref, e_ref, abr_ref, abi_ref, bbr_ref, bbi_ref, tab_ref):
        ab_r, ab_i, bb_r, bb_i = _discretise(lr_ref[...], li_ref[...], ldt_ref[...], br_ref[...],
                                             bi_ref[...], e_ref[...])
        abr_ref[...] = ab_r
        abi_ref[...] = ab_i
        bbr_ref[...] = bb_r
        bbi_ref[...] = bb_i
        row = lax.broadcasted_iota(jnp.int32, (8, s), 0)
        for base, sign in ((0, 1.0), (8, -1.0)):
            p = [None] * 9
            p[1] = (ab_r, sign * ab_i)
            p[2] = _cmul(*p[1], *p[1])
            p[3] = _cmul(*p[2], *p[1])
            p[4] = _cmul(*p[2], *p[2])
            p[5] = _cmul(*p[4], *p[1])
            p[6] = _cmul(*p[4], *p[2])
            p[7] = _cmul(*p[4], *p[3])
            p[8] = _cmul(*p[4], *p[4])
            for j, k in enumerate((1, 2, 4)):
                keep = (row >= k) if base == 0 else (row < 8 - k)
                tab_ref[base + 2 * j] = jnp.where(keep, jnp.broadcast_to(p[k][0], (8, s)), 0.0)
                tab_ref[base + 2 * j + 1] = jnp.where(keep, jnp.broadcast_to(p[k][1], (8, s)), 0.0)
            pw_r = jnp.zeros((8, s), F32)
            pw_i = jnp.zeros((8, s), F32)
            for i in range(8):
                e = i + 1 if base == 0 else 8 - i
                pw_r = jnp.where(row == i, jnp.broadcast_to(p[e][0], (8, s)), pw_r)
                pw_i = jnp.where(row == i, jnp.broadcast_to(p[e][1], (8, s)), pw_i)
            tab_ref[base + 6] = pw_r
            tab_ref[base + 7] = pw_i

    def full(shape):
        return pl.BlockSpec(shape, lambda i: (0,) * len(shape))

    g = D_MODEL // SSM_GROUP
    return _call(body, "s5_prep", (1,),
                 [full((1, s)), full((1, s)), full((1, g)), full((16, s)), full((16, s)), full((g, s))],
                 [full((1, s)), full((1, s)), full((16, s)), full((16, s)), full((16, 8, s))],
                 [_sds((1, s), F32), _sds((1, s), F32), _sds((16, s), F32), _sds((16, s), F32),
                  _sds((16, 8, s), F32)])(lr, li, ldt, br, bi, expand)


def _s5_prep_bwd(lr, li, ldt, br, bi, d_abr, d_abi, d_bbr, d_bbi):
    s = N_STATE
    g = D_MODEL // SSM_GROUP
    expand = _group_expand_matrix()

    def body(lr_ref, li_ref, ldt_ref, br_ref, bi_ref, e_ref, c1, c2, c3, c4, o1, o2, o3, o4, o5):
        e_val = e_ref[...]
        _, vjp = jax.vjp(lambda a, b, c, d, e: _discretise(a, b, c, d, e, e_val),
                         lr_ref[...], li_ref[...], ldt_ref[...], br_ref[...], bi_ref[...])
        grads = vjp((c1[...], c2[...], c3[...], c4[...]))
        for o, gr in zip((o1, o2, o3, o4, o5), grads):
            o[...] = gr

    def full(shape):
        return pl.BlockSpec(shape, lambda i: (0,) * len(shape))

    return _call(body, "s5_prep_bwd", (1,),
                 [full((1, s)), full((1, s)), full((1, g)), full((16, s)), full((16, s)), full((g, s)),
                  full((1, s)), full((1, s)), full((16, s)), full((16, s))],
                 [full((1, s)), full((1, s)), full((1, g)), full((16, s)), full((16, s))],
                 [_sds((1, s), F32), _sds((1, s), F32), _sds((1, g), F32), _sds((16, s), F32),
                  _sds((16, s), F32)])(lr, li, ldt, br, bi, expand, d_abr, d_abi, d_bbr, d_bbi)


def _scan8(vr, vi, tab_ref, base, reverse):
    for j, k in enumerate((1, 2, 4)):
        mr = tab_ref[base + 2 * j]
        mi = tab_ref[base + 2 * j + 1]
        shift = 8 - k if reverse else k
        sr = pltpu.roll(vr, shift, 0)
        si = pltpu.roll(vi, shift, 0)
        vr, vi = vr + mr * sr - mi * si, vi + mr * si + mi * sr
    return vr, vi


def _s5_scan_fwd(x, bbr, bbi, cr, ci, dvec, tab, seq, comm=None):
    n, d = x.shape
    nt = seq // T_SCAN
    nblk = T_SCAN // 8
    st = CHUNK_ST

    def body(x_ref, bbr_ref, bbi_ref, cr_ref, ci_ref, d_ref, tab_ref, y_ref, hr_ref, hi_ref, car_r, car_i):
        t = pl.program_id(2)

        @pl.when(t == 0)
        def _():
            car_r[...] = jnp.zeros_like(car_r)
            car_i[...] = jnp.zeros_like(car_i)

        u = x_ref[...]
        ub = u.astype(BF16)
        hr_ref[...] = jnp.dot(ub, bbr_ref[...], preferred_element_type=F32)
        hi_ref[...] = jnp.dot(ub, bbi_ref[...], preferred_element_type=F32)

        def step(i, carry):
            sl = pl.ds(pl.multiple_of(i * 8, 8), 8)
            vr, vi = _scan8(hr_ref[sl, :], hi_ref[sl, :], tab_ref, 0, False)
            c_r = jnp.broadcast_to(car_r[...], (8, st))
            c_i = jnp.broadcast_to(car_i[...], (8, st))
            pr = tab_ref[6]
            pi = tab_ref[7]
            vr, vi = vr + pr * c_r - pi * c_i, vi + pr * c_i + pi * c_r
            hr_ref[sl, :] = vr
            hi_ref[sl, :] = vi
            car_r[...] = vr[7:8, :]
            car_i[...] = vi[7:8, :]
            return carry

        lax.fori_loop(0, nblk, step, 0, unroll=4)
        y = lax.dot_general(hr_ref[...].astype(BF16), cr_ref[...], NT, preferred_element_type=F32)
        y = y - lax.dot_general(hi_ref[...].astype(BF16), ci_ref[...], NT, preferred_element_type=F32)
        y_ref[...] = y + d_ref[...] * u

    row = lambda c, b, t: (b * nt + t, c)
    mat = pl.BlockSpec((None, CHUNK_CH, st), lambda c, b, t: (c, 0, 0))
    return _call(
        body, "s5_scan_fwd", (SSM_CHUNKS, n // seq, nt),
        [pl.BlockSpec((T_SCAN, CHUNK_CH), row), mat, mat, mat, mat,
         pl.BlockSpec((1, CHUNK_CH), lambda c, b, t: (0, c)),
         pl.BlockSpec((16, 8, st), lambda c, b, t: (0, 0, c))],
        [pl.BlockSpec((T_SCAN, CHUNK_CH), row), pl.BlockSpec((T_SCAN, st), row), pl.BlockSpec((T_SCAN, st), row)],
        [_sds((n, d), F32), _sds((n, N_STATE), F32), _sds((n, N_STATE), F32)],
        [pltpu.VMEM((1, st), F32), pltpu.VMEM((1, st), F32)], comm=comm,
    )(x, bbr, bbi, cr, ci, dvec, tab)


def _s5_scan_bwd(x, dy, dres, hr, hi, bbr, bbi, cr, ci, dvec, tab, seq, comm=None):
    n, d = x.shape
    nt = seq // T_SCAN
    nblk = T_SCAN // 8
    st = CHUNK_ST

    def body(x_ref, dy_ref, dres_ref, hr_ref, hi_ref, hpr_ref, hpi_ref, bbr_ref, bbi_ref, cr_ref, ci_ref,
             d_ref, tab_ref, dx_ref, dcr_ref, dci_ref, dbr_ref, dbi_ref, dar_ref, dai_ref, dd_ref,
             g_r, g_i, car_r, car_i):
        b = pl.program_id(1)
        tg = pl.program_id(2)
        first = jnp.logical_and(b == 0, tg == 0)

        @pl.when(tg == 0)
        def _():
            car_r[...] = jnp.zeros_like(car_r)
            car_i[...] = jnp.zeros_like(car_i)

        u = x_ref[...]
        dyv = dy_ref[...]
        ub = u.astype(BF16)
        dyb = dyv.astype(BF16)
        g_r[...] = jnp.dot(dyb, cr_ref[...], preferred_element_type=F32)
        g_i[...] = -jnp.dot(dyb, ci_ref[...], preferred_element_type=F32)

        def step(ii, carry):
            i = nblk - 1 - ii
            sl = pl.ds(pl.multiple_of(i * 8, 8), 8)
            vr, vi = _scan8(g_r[sl, :], g_i[sl, :], tab_ref, 8, True)
            c_r = jnp.broadcast_to(car_r[...], (8, st))
            c_i = jnp.broadcast_to(car_i[...], (8, st))
            pr = tab_ref[14]
            pi = tab_ref[15]
            vr, vi = vr + pr * c_r - pi * c_i, vi + pr * c_i + pi * c_r
            g_r[sl, :] = vr
            g_i[sl, :] = vi
            car_r[...] = vr[0:1, :]
            car_i[...] = vi[0:1, :]
            return carry

        lax.fori_loop(0, nblk, step, 0, unroll=4)
        gr = g_r[...]
        gi = g_i[...]
        grb = gr.astype(BF16)
        gib = gi.astype(BF16)
        du = lax.dot_general(grb, bbr_ref[...], NT, preferred_element_type=F32)
        du = du + lax.dot_general(gib, bbi_ref[...], NT, preferred_element_type=F32)
        dx_ref[...] = DN_ALPHA * dres_ref[...] + du + d_ref[...] * dyv

        h_r = hr_ref[...]
        h_i = hi_ref[...]
        t_orig = nt - 1 - tg
        row0 = lax.broadcasted_iota(jnp.int32, (8, st), 0) == 0
        halo_ok = t_orig > 0

        def previous(h, halo_ref):
            top = jnp.broadcast_to(jnp.where(halo_ok, halo_ref[7:8, :], 0.0), (8, st))
            rolled = pltpu.roll(h, 1, 0)
            return jnp.concatenate([jnp.where(row0, top, rolled[:8, :]), rolled[8:, :]], axis=0)

        hp_r = previous(h_r, hpr_ref)
        hp_i = previous(h_i, hpi_ref)
        dar = jnp.sum(gr * hp_r + gi * hp_i, axis=0, keepdims=True)
        dai = jnp.sum(gi * hp_r - gr * hp_i, axis=0, keepdims=True)
        dcr = lax.dot_general(dyb, h_r.astype(BF16), TN, preferred_element_type=F32)
        dci = lax.dot_general(dyb, h_i.astype(BF16), TN, preferred_element_type=F32)
        dbr = lax.dot_general(ub, grb, TN, preferred_element_type=F32)
        dbi = lax.dot_general(ub, gib, TN, preferred_element_type=F32)
        dd = jnp.sum(dyv * u, axis=0, keepdims=True)

        @pl.when(first)
        def _():
            dcr_ref[...] = dcr
            dci_ref[...] = dci
            dbr_ref[...] = dbr
            dbi_ref[...] = dbi
            dar_ref[...] = dar
            dai_ref[...] = dai
            dd_ref[...] = dd

        @pl.when(jnp.logical_not(first))
        def _():
            dcr_ref[...] += dcr
            dci_ref[...] += dci
            dbr_ref[...] += dbr
            dbi_ref[...] += dbi
            dar_ref[...] += dar
            dai_ref[...] += dai
            dd_ref[...] += dd

    row = lambda c, b, t: (b * nt + (nt - 1 - t), c)
    halo = lambda c, b, t: (jnp.maximum((b * nt + (nt - 1 - t)) * (T_SCAN // 8) - 1, 0), c)
    mat = pl.BlockSpec((None, CHUNK_CH, st), lambda c, b, t: (c, 0, 0))
    chan = pl.BlockSpec((T_SCAN, CHUNK_CH), row)
    state = pl.BlockSpec((T_SCAN, st), row)
    return _call(
        body, "s5_scan_bwd", (SSM_CHUNKS, n // seq, nt),
        [chan, chan, chan, state, state, pl.BlockSpec((8, st), halo), pl.BlockSpec((8, st), halo),
         mat, mat, mat, mat, pl.BlockSpec((1, CHUNK_CH), lambda c, b, t: (0, c)),
         pl.BlockSpec((16, 8, st), lambda c, b, t: (0, 0, c))],
        [chan, mat, mat, mat, mat, pl.BlockSpec((1, st), lambda c, b, t: (0, c)),
         pl.BlockSpec((1, st), lambda c, b, t: (0, c)), pl.BlockSpec((1, CHUNK_CH), lambda c, b, t: (0, c))],
        [_sds((n, d), F32)] + [_sds((SSM_CHUNKS, CHUNK_CH, st), F32)] * 4
        + [_sds((1, N_STATE), F32), _sds((1, N_STATE), F32), _sds((1, d), F32)],
        [pltpu.VMEM((T_SCAN, st), F32), pltpu.VMEM((T_SCAN, st), F32), pltpu.VMEM((1, st), F32),
         pltpu.VMEM((1, st), F32)], comm=comm,
    )(x, dy, dres, hr, hi, hr, hi, bbr, bbi, cr, ci, dvec, tab)


_GELU_C = math.sqrt(2.0 / math.pi)


def _gelu_parts(y):
    t = jnp.tanh(_GELU_C * (y + 0.044715 * (y * y * y)))
    return 0.5 * (1.0 + t), t


def _glu_fwd(y, w_glu, b_glu):
    n, d = y.shape

    def body(y_ref, w_ref, b_ref, y2_ref, z_ref, gg_ref):
        yv = y_ref[...]
        cdf, _ = _gelu_parts(yv)
        y2 = yv * cdf
        z = jnp.dot(y2.astype(BF16), w_ref[...], preferred_element_type=F32) + b_ref[...]
        y2_ref[...] = y2.astype(BF16)
        z_ref[...] = z
        gg_ref[...] = (y2 * jax.nn.sigmoid(z)).astype(BF16)

    row = pl.BlockSpec((TM_MM_F32, d), lambda i: (i, 0))
    return _call(body, "glu_fwd", (n // TM_MM_F32,),
                 [row, pl.BlockSpec((d, d), lambda i: (0, 0)), pl.BlockSpec((1, d), lambda i: (0, 0))],
                 [row, row, row], [_sds((n, d), BF16), _sds((n, d), F32), _sds((n, d), BF16)])(y, w_glu, b_glu)


def _glu_bwd_gate(y, z, dgg):
    n, d = y.shape

    def body(y_ref, z_ref, dgg_ref, dz_ref, db_ref, t_ref):
        i = pl.program_id(0)
        yv = y_ref[...]
        cdf, _ = _gelu_parts(yv)
        y2 = yv * cdf
        s = jax.nn.sigmoid(z_ref[...])
        dg = dgg_ref[...]
        dz = dg * y2 * s * (1.0 - s)
        dz_ref[...] = dz.astype(BF16)
        t_ref[...] = dg * s
        db = jnp.sum(dz, axis=0, keepdims=True)

        @pl.when(i == 0)
        def _():
            db_ref[...] = db

        @pl.when(i > 0)
        def _():
            db_ref[...] += db

    row = pl.BlockSpec((TM, d), lambda i: (i, 0))
    vec = pl.BlockSpec((1, d), lambda i: (0, 0))
    return _call(body, "glu_bwd_gate", (n // TM,), [row, row, row], [row, vec, row],
                 [_sds((n, d), BF16), _sds((1, d), F32), _sds((n, d), F32)])(y, z, dgg)


def _gelu_bwd(dy2, y):
    cdf, t = _gelu_parts(y)
    dinner = _GELU_C * (1.0 + 3.0 * 0.044715 * y * y)
    return dy2 * (cdf + 0.5 * y * (1.0 - t * t) * dinner)


HALO = 16


def _shift_down(x, halo, k):
    out = pltpu.roll(x, k, 0)
    row = lax.broadcasted_iota(jnp.int32, (8, x.shape[1]), 0)
    top = jnp.where(row < k, pltpu.roll(halo[HALO - 8:HALO, :], k, 0), out[:8, :])
    return jnp.concatenate([top, out[8:, :]], axis=0)


def _conv3(x, halo, w, b):
    x1 = _shift_down(x, halo, 1)
    x2 = _shift_down(x, halo, 2)
    return b + w[0:1, :] * x + w[1:2, :] * x1 + w[2:3, :] * x2, x1, x2


def _conv_gate_fwd(hc, conv_w, conv_b, seq, comm=None):
    _, nb, n, c = hc.shape
    tiles_per_seq = seq // TM

    def body(x_ref, halo_ref, w_ref, b_ref, o_ref, cv_ref):
        i = pl.program_id(1)
        start = (i % tiles_per_seq) == 0
        cv = []
        for h in range(2):
            halo = jnp.where(start, 0.0, halo_ref[h].astype(F32))
            cv.append(_conv3(x_ref[h].astype(F32), halo, w_ref[h], b_ref[h])[0])
            cv_ref[h] = cv[h].astype(BF16)
        o_ref[...] = (cv[1] * jax.nn.sigmoid(cv[1]) * cv[0]).astype(BF16)

    both = pl.BlockSpec((2, None, TM, c), lambda d, i: (0, d, i, 0))
    return _call(
        body, "conv_gate_fwd", (nb, n // TM),
        [both,
         pl.BlockSpec((2, None, HALO, c), lambda d, i: (0, d, jnp.maximum(i * (TM // HALO) - 1, 0), 0)),
         pl.BlockSpec((2, None, 3, c), lambda d, i: (0, d, 0, 0)),
         pl.BlockSpec((2, None, 1, c), lambda d, i: (0, d, 0, 0))],
        [pl.BlockSpec((None, TM, c), lambda d, i: (d, i, 0)), both],
        [_sds((nb, n, c), BF16), _sds((2, nb, n, c), BF16)], comm=comm)(hc, hc, conv_w, conv_b)


def _conv_gate_bwd(hc, cv, dact, conv_w, seq, comm=None):
    _, nb, n, c = hc.shape
    tiles_per_seq = seq // TM
    last_halo = n // HALO - 1
    ext = TM + HALO

    def body(x_ref, cv_ref, cvn_ref, da_ref, dan_ref, w_ref, dx_ref, dw_ref):
        i = pl.program_id(1)
        end = (i % tiles_per_seq) == tiles_per_seq - 1
        row = lax.broadcasted_iota(jnp.int32, (ext, c), 0)
        cv = [jnp.concatenate([cv_ref[h], cvn_ref[h]], axis=0).astype(F32) for h in range(2)]
        da = jnp.concatenate([da_ref[...], dan_ref[...]], axis=0).astype(F32)
        da = jnp.where(jnp.logical_and(end, row >= TM), 0.0, da)
        sg = jax.nn.sigmoid(cv[1])
        silu = cv[1] * sg
        dcv = [da * silu, da * cv[0] * (sg + silu * (1.0 - sg))]
        for h in range(2):
            w = w_ref[h]
            g = dcv[h]
            g0 = g[:TM, :]
            g1 = pltpu.roll(g, ext - 1, 0)[:TM, :]
            g2 = pltpu.roll(g, ext - 2, 0)[:TM, :]
            dx_ref[h] = (w[0:1, :] * g0 + w[1:2, :] * g1 + w[2:3, :] * g2).astype(BF16)
            x = x_ref[h].astype(F32)
            parts = [jnp.sum(g0 * x, axis=0, keepdims=True),
                     jnp.sum(g1 * x, axis=0, keepdims=True),
                     jnp.sum(g2 * x, axis=0, keepdims=True),
                     jnp.sum(g0, axis=0, keepdims=True)]
            upd = jnp.concatenate(parts + [jnp.zeros((4, c), F32)], axis=0)

            @pl.when(i == 0)
            def _():
                dw_ref[h] = upd

            @pl.when(i > 0)
            def _():
                dw_ref[h] += upd

    nxt = lambda i: jnp.minimum((i + 1) * (TM // HALO), last_halo)
    return _call(
        body, "conv_gate_bwd", (nb, n // TM),
        [pl.BlockSpec((2, None, TM, c), lambda d, i: (0, d, i, 0)),
         pl.BlockSpec((2, None, TM, c), lambda d, i: (0, d, i, 0)),
         pl.BlockSpec((2, None, HALO, c), lambda d, i: (0, d, nxt(i), 0)),
         pl.BlockSpec((None, TM, c), lambda d, i: (d, i, 0)),
         pl.BlockSpec((None, HALO, c), lambda d, i: (d, nxt(i), 0)),
         pl.BlockSpec((2, None, 3, c), lambda d, i: (0, d, 0, 0))],
        [pl.BlockSpec((2, None, TM, c), lambda d, i: (0, d, i, 0)),
         pl.BlockSpec((2, None, 8, c), lambda d, i: (0, d, 0, 0))],
        [_sds((2, nb, n, c), BF16), _sds((2, nb, 8, c), F32)], comm=comm)(hc, cv, cv, dact, dact, conv_w)


def _t5_bucket_np(dist):
    exact = REL_BUCKETS // 2
    d = np.maximum(dist, 1).astype(np.float32)
    large = exact + (np.log(d / exact) / math.log(REL_MAX_DIST / exact) * (REL_BUCKETS - exact)).astype(np.int64)
    large = np.minimum(large, REL_BUCKETS - 1)
    return np.where(dist < exact, dist, large).astype(np.int32)


def _bucket_table(dil):
    steps = np.arange(BAND)[:, None] + BAND - np.arange(2 * BAND)[None, :]
    return _t5_bucket_np(np.maximum(steps, 0) * dil).reshape(1, BAND * 2 * BAND).astype(np.float32)


def _bias_expand(rel_t, dil):
    nk = BAND * 2 * BAND
    bucket = jnp.asarray(_bucket_table(dil))

    def body(r_ref, bk_ref, o_ref):
        ids = lax.broadcasted_iota(jnp.int32, (REL_BUCKETS, nk), 0).astype(F32)
        onehot = (ids == bk_ref[...]).astype(F32)
        o_ref[...] = jnp.dot(r_ref[...], onehot, precision=HIGHEST, preferred_element_type=F32)

    return _call(body, "bias_expand", (1,),
                 [pl.BlockSpec((N_HEADS, REL_BUCKETS), lambda i: (0, 0)), pl.BlockSpec((1, nk), lambda i: (0, 0))],
                 pl.BlockSpec((N_HEADS, nk), lambda i: (0, 0)), _sds((N_HEADS, nk), F32))(rel_t, bucket)


def _bias_reduce(dbias, dil):
    nk = BAND * 2 * BAND
    bucket = jnp.asarray(_bucket_table(dil))

    def body(g_ref, bk_ref, o_ref):
        ids = lax.broadcasted_iota(jnp.int32, (REL_BUCKETS, nk), 0).astype(F32)
        onehot = (ids == bk_ref[...]).astype(F32)
        o_ref[...] = lax.dot_general(g_ref[...], onehot, NT, precision=HIGHEST, preferred_element_type=F32)

    return _call(body, "bias_reduce", (1,),
                 [pl.BlockSpec((N_HEADS, nk), lambda i: (0, 0)), pl.BlockSpec((1, nk), lambda i: (0, 0))],
                 pl.BlockSpec((N_HEADS, REL_BUCKETS), lambda i: (0, 0)),
                 _sds((N_HEADS, REL_BUCKETS), F32))(dbias, bucket)


def _valid_mask(n):
    qi = lax.broadcasted_iota(jnp.int32, (BAND, 2 * BAND), 0)
    kj = lax.broadcasted_iota(jnp.int32, (BAND, 2 * BAND), 1)
    steps = qi + BAND - kj
    in_band = jnp.logical_and(steps >= 0, steps <= BAND)
    return jnp.logical_and(in_band, jnp.logical_or(n > 0, kj >= BAND))


ATTN_COLS = {1: 1024, 4: 512, 16: 128}


def _residue(dil, r):
    return slice(None) if dil == 1 else pl.ds(r, BAND, stride=dil)


def _stack_heads(x, first):
    return jnp.concatenate([jnp.where(first, x, 0.0), jnp.where(first, 0.0, x)], axis=0)


def _attn_fwd(q, k, v, bias, g, dil, bsz, seq, comm=None, merge=()):
    n = q.shape[0]
    rows = BAND * dil
    nch = seq // rows
    w = N_HEADS * HEAD_DIM
    wc = ATTN_COLS[dil]
    ncol = w // wc
    hpc = wc // HEAD_DIM
    nj = wc // 128
    slab = lambda k: pltpu.VMEM((k * nj, rows, 128), F32)

    def attend(q_ref, kc_ref, kp_ref, vc_ref, vp_ref, b_ref, others, o_ref, l_ref, qf, kf, vf, of, lf):
        ch = pl.program_id(2)
        valid = _valid_mask(ch)
        valid = jnp.concatenate([valid, valid], axis=0)
        lane = lax.broadcasted_iota(jnp.int32, (BAND, 128), 1)
        first = lane < HEAD_DIM
        for j in range(nj):
            sl = slice(128 * j, 128 * (j + 1))
            qf[j] = q_ref[:, sl].astype(F32)
            kf[j] = kp_ref[:, sl].astype(F32)
            kf[nj + j] = kc_ref[:, sl].astype(F32)
            vf[j] = vp_ref[:, sl].astype(F32)
            vf[nj + j] = vc_ref[:, sl].astype(F32)
        for r in range(dil):
            rr = _residue(dil, r)
            for j in range(nj):
                q2 = qf.at[j][rr, :]
                k2 = jnp.concatenate([kf.at[j][rr, :], kf.at[nj + j][rr, :]], axis=0).astype(BF16)
                v2 = jnp.concatenate([vf.at[j][rr, :], vf.at[nj + j][rr, :]], axis=0).astype(BF16)
                qs = _stack_heads(q2, first).astype(BF16)
                s = lax.dot_general(qs, k2, NT, preferred_element_type=F32) * (HEAD_DIM ** -0.5)
                s = jnp.where(valid, s + b_ref[2 * j:2 * j + 2].reshape(2 * BAND, 2 * BAND), NEG_BIG)
                m = jnp.max(s, axis=1, keepdims=True)
                p = jnp.exp(s - m)
                l = jnp.sum(p, axis=1, keepdims=True)
                pv = jnp.dot(p.astype(BF16), v2, preferred_element_type=F32) / l
                lse = jnp.broadcast_to(m + jnp.log(l), (2 * BAND, 128))
                of.at[j][rr, :] = jnp.where(first, pv[:BAND], pv[BAND:])
                lf.at[j][rr, :] = jnp.where(first, lse[:BAND], lse[BAND:])
        for j in range(nj):
            sl = slice(128 * j, 128 * (j + 1))
            if merge:
                o0, l0, o1, l1 = (r[:, sl] for r in others)
                lc = lf[j]
                m = jnp.maximum(jnp.maximum(l0, l1), lc)
                wa, wb, wc_ = jnp.exp(l0 - m), jnp.exp(l1 - m), jnp.exp(lc - m)
                tot = wa + wb + wc_
                o_ref[:, sl] = (wa * o0 + wb * o1 + wc_ * of[j]) / tot
                l_ref[:, sl] = m + jnp.log(tot)
            else:
                o_ref[:, sl] = of[j]
                l_ref[:, sl] = lf[j]

    def body(q_ref, kc_ref, kp_ref, vc_ref, vp_ref, b_ref, *rest):
        others, rest = rest[:len(merge)], rest[len(merge):]
        attend(q_ref, kc_ref, kp_ref, vc_ref, vp_ref, b_ref, others, *rest)

    base = g * ncol
    cur = lambda c, b, ch: (b * nch + ch, base + c)
    prev = lambda c, b, ch: (b * nch + jnp.maximum(ch - 1, 0), base + c)
    blk = lambda im: pl.BlockSpec((rows, wc), im)
    out_blk = pl.BlockSpec((rows, wc), lambda c, b, ch: (b * nch + ch, c))
    return _call(
        body, "attn_fwd_d%d" % dil, (ncol, bsz, nch),
        [blk(cur), blk(cur), blk(prev), blk(cur), blk(prev),
         pl.BlockSpec((hpc, BAND, 2 * BAND), lambda c, b, ch: (c, 0, 0))] + [out_blk] * len(merge),
        [out_blk, out_blk],
        [_sds((n, w), F32)] * 2,
        [slab(1), slab(2), slab(2), slab(1), slab(1)], comm=comm,
    )(q, k, k, v, v, bias, *merge)


def _attn_bwd(q, k, v, do, o, lse, bias, g, dil, bsz, seq, comm=None, into=()):
    n = q.shape[0]
    rows = BAND * dil
    nch = seq // rows
    w = N_HEADS * HEAD_DIM
    wc = ATTN_COLS[dil]
    ncol = w // wc
    hpc = wc // HEAD_DIM
    nj = wc // 128
    slab = lambda k: pltpu.VMEM((k * nj, rows, 128), F32)
    scale = HEAD_DIM ** -0.5

    def body(q_ref, kc_ref, kp_ref, vc_ref, vp_ref, do_ref, o_ref, l_ref, b_ref,
             dq_ref, dk_ref, dv_ref, db_ref, qf, kf, vf, dof, of, lf, dqf, dkf, dvf):
        b = pl.program_id(1)
        ch = pl.program_id(2)
        cur = ch % 2
        prv = 1 - cur

        @pl.when(jnp.logical_and(b == 0, ch == 0))
        def _():
            db_ref[...] = jnp.zeros_like(db_ref)

        @pl.when(ch == 0)
        def _():
            for j in range(nj):
                dkf[nj + j] = jnp.zeros((rows, 128), F32)
                dvf[nj + j] = jnp.zeros((rows, 128), F32)

        @pl.when(ch < nch)
        def _():
            valid = _valid_mask(ch)
            valid = jnp.concatenate([valid, valid], axis=0)
            first = lax.broadcasted_iota(jnp.int32, (BAND, 128), 1) < HEAD_DIM
            for j in range(nj):
                sl = slice(128 * j, 128 * (j + 1))
                qf[j] = q_ref[:, sl].astype(F32)
                kf[j] = kp_ref[:, sl].astype(F32)
                kf[nj + j] = kc_ref[:, sl].astype(F32)
                vf[j] = vp_ref[:, sl].astype(F32)
                vf[nj + j] = vc_ref[:, sl].astype(F32)
                dof[j] = do_ref[:, sl]
                of[j] = o_ref[:, sl]
                lf[j] = l_ref[:, sl]
            for r in range(dil):
                rr = _residue(dil, r)
                for j in range(nj):
                    k2 = jnp.concatenate([kf.at[j][rr, :], kf.at[nj + j][rr, :]], axis=0).astype(BF16)
                    v2 = jnp.concatenate([vf.at[j][rr, :], vf.at[nj + j][rr, :]], axis=0).astype(BF16)
                    do2 = dof.at[j][rr, :]
                    lse2 = lf.at[j][rr, :]
                    qs = _stack_heads(qf.at[j][rr, :], first).astype(BF16)
                    dos = _stack_heads(do2, first)
                    delta = jnp.sum(dos * jnp.concatenate([of.at[j][rr, :]] * 2, axis=0), axis=1, keepdims=True)
                    dos = dos.astype(BF16)
                    lse_col = jnp.concatenate([lse2[:, 0:1], lse2[:, HEAD_DIM:HEAD_DIM + 1]], axis=0)
                    s = lax.dot_general(qs, k2, NT, preferred_element_type=F32) * scale
                    s = jnp.where(valid, s + b_ref[2 * j:2 * j + 2].reshape(2 * BAND, 2 * BAND), NEG_BIG)
                    p = jnp.exp(s - lse_col)
                    dp = lax.dot_general(dos, v2, NT, preferred_element_type=F32)
                    ds = p * (dp - delta)
                    db_ref[2 * j:2 * j + 2] += ds.reshape(2, BAND, 2 * BAND)
                    dsb = ds.astype(BF16)
                    dq2 = jnp.dot(dsb, k2, preferred_element_type=F32) * scale
                    dk2 = lax.dot_general(dsb, qs, TN, preferred_element_type=F32) * scale
                    dv2 = lax.dot_general(p.astype(BF16), dos, TN, preferred_element_type=F32)
                    dqf.at[j][rr, :] = jnp.where(first, dq2[:BAND], dq2[BAND:])
                    dkf.at[prv * nj + j][rr, :] += dk2[:BAND, :]
                    dvf.at[prv * nj + j][rr, :] += dv2[:BAND, :]
                    dkf.at[cur * nj + j][rr, :] = dk2[BAND:, :]
                    dvf.at[cur * nj + j][rr, :] = dv2[BAND:, :]
            for j in range(nj):
                dq_ref[:, 128 * j:128 * (j + 1)] = dqf[j].astype(BF16)

        for j in range(nj):
            sl = slice(128 * j, 128 * (j + 1))
            dk_ref[:, sl] = dkf[prv * nj + j].astype(BF16)
            dv_ref[:, sl] = dvf[prv * nj + j].astype(BF16)

    last = nch - 1
    base = g * ncol
    cur3 = lambda c, b, ch: (b * nch + jnp.minimum(ch, last), base + c)
    prev3 = lambda c, b, ch: (b * nch + jnp.maximum(jnp.minimum(ch, last) - 1, 0), base + c)
    cur1 = lambda c, b, ch: (b * nch + jnp.minimum(ch, last), c)
    lag3 = lambda c, b, ch: (b * nch + jnp.maximum(ch - 1, 0), base + c)
    blk = lambda im: pl.BlockSpec((rows, wc), im)
    bias_blk = pl.BlockSpec((hpc, BAND, 2 * BAND), lambda c, b, ch: (c, 0, 0))
    out = _sds(q.shape, BF16)
    return _call(
        body, "attn_bwd_d%d" % dil, (ncol, bsz, nch + 1),
        [blk(cur3), blk(cur3), blk(prev3), blk(cur3), blk(prev3), blk(cur1), blk(cur1), blk(cur1), bias_blk],
        [blk(cur3), blk(lag3), blk(lag3), bias_blk],
        [out, out, out, _sds((N_HEADS, BAND, 2 * BAND), F32)],
        [slab(1), slab(2), slab(2), slab(1), slab(1), slab(1), slab(1), slab(2), slab(2)], comm=comm, fill=into,
    )(q, k, k, v, v, do, o, lse, bias)


def _adamw(name, parts, w, m, v):
    nl, r, c = w.shape
    nparts = parts[0].shape[0]
    tr = r
    for cand in (256, 352, 128):
        if r % cand == 0 and r > cand:
            tr = cand
            break
    c1 = 1.0 - ADAM_B1 ** ADAM_STEP
    c2 = 1.0 - ADAM_B2 ** ADAM_STEP

    def body(*refs):
        p_refs = refs[:nl]
        w_ref, m_ref, v_ref, g_ref, d_ref, nm_ref, nv_ref = refs[nl:]
        layer = pl.program_id(0)
        for l in range(nl):
            @pl.when(layer == l)
            def _():
                g = p_refs[l][0].astype(F32)
                for dev in range(1, nparts):
                    g = g + p_refs[l][dev].astype(F32)
                nm = ADAM_B1 * m_ref[...] + (1.0 - ADAM_B1) * g
                nv = ADAM_B2 * v_ref[...] + (1.0 - ADAM_B2) * (g * g)
                m_hat = nm / c1
                v_hat = nv / c2
                g_ref[...] = g
                d_ref[...] = -ADAM_LR * (m_hat / (jnp.sqrt(v_hat) + ADAM_EPS) + ADAM_WD * w_ref[...])
                nm_ref[...] = nm
                nv_ref[...] = nv

    def part_spec(l):
        return pl.BlockSpec((nparts, tr, c), lambda layer, i: (0, jnp.where(layer == l, i, 0), 0))

    row = pl.BlockSpec((None, tr, c), lambda layer, i: (layer, i, 0))
    return _call(body, name, (nl, r // tr), [part_spec(l) for l in range(nl)] + [row, row, row],
                 [row] * 4, [_sds((nl, r, c), F32)] * 4)(*parts, w, m, v)


def _exchange(name, srcs, out_shapes, items):
    n_in = len(srcs)
    n_out = len(out_shapes)
    n_items = len(items)

    def body(*refs):
        ins = refs[:n_in]
        outs = refs[n_in:n_in + n_out]
        send_sems, recv_sems, local_sems = refs[n_in + n_out:]
        x, y, c = lax.axis_index("x"), lax.axis_index("y"), lax.axis_index("c")
        me = 4 * x + 2 * y + c

        def pick(ref, sel, peer):
            idx = tuple(peer if s == "peer" else me if s == "me" else s for s in sel)
            return ref.at[idx] if idx else ref

        copies = []
        for it, (si, ssel, oi, osel) in enumerate(items):
            local = pltpu.make_async_copy(pick(ins[si], ssel, me), pick(outs[oi], osel, me), local_sems.at[it])
            local.start()
            copies.append(local)
            for j in range(1, N_DEV):
                px = 1 - x if j & 4 else x
                py = 1 - y if j & 2 else y
                pc = 1 - c if j & 1 else c
                peer = 4 * px + 2 * py + pc
                sem = it * (N_DEV - 1) + j - 1
                cp = pltpu.make_async_remote_copy(
                    src_ref=pick(ins[si], ssel, peer), dst_ref=pick(outs[oi], osel, peer),
                    send_sem=send_sems.at[sem], recv_sem=recv_sems.at[sem],
                    device_id=(px, py, pc), device_id_type=pl.DeviceIdType.MESH)
                cp.start()
                copies.append(cp)
        for cp in copies:
            cp.wait()

    any_spec = pl.BlockSpec(memory_space=pl.ANY)
    return pl.pallas_call(
        body, name=name, in_specs=[any_spec] * n_in, out_specs=[any_spec] * n_out, out_shape=list(out_shapes),
        scratch_shapes=[pltpu.SemaphoreType.DMA((n_items * (N_DEV - 1),)),
                        pltpu.SemaphoreType.DMA((n_items * (N_DEV - 1),)),
                        pltpu.SemaphoreType.DMA((n_items,))],
        compiler_params=pltpu.CompilerParams(has_side_effects=True),
    )(*srcs)


def _sel(ref, sel, slot=None):
    idx = tuple(slot if s == "slot" else s for s in sel)
    return ref.at[idx] if idx else ref


def _gather_plan(srcs, out_shapes, items):
    per = N_DEV - 1
    n_items = len(items)

    def phases(ins, outs, send_sems, recv_sems, local_sems, total):
        x, y, c = lax.axis_index("x"), lax.axis_index("y"), lax.axis_index("c")
        sibling = (x, y, 1 - c)
        chips = [(1 - x, y), (x, 1 - y), (1 - x, 1 - y)]
        slot_of = lambda px, py, pc: 4 * px + 2 * py + pc

        def copy(it, k, block, to, src=None):
            _, _, oi, osel = items[it]
            dst = _sel(outs[oi], osel, slot_of(*block))
            return pltpu.make_async_remote_copy(
                src_ref=dst if src is None else src, dst_ref=dst,
                send_sem=send_sems.at[it * per + k], recv_sem=recv_sems.at[it * per + k],
                device_id=to, device_id_type=pl.DeviceIdType.MESH)

        def own_copies(it):
            si, ssel, oi, osel = items[it]
            src = _sel(ins[si], ssel)
            mine = pltpu.make_async_copy(src, _sel(outs[oi], osel, slot_of(x, y, c)), local_sems.at[it])
            first = [copy(it, 0, (x, y, c), sibling, src=src)]
            first += [copy(it, 1 + j, (x, y, c), (*chip, c), src=src) for j, chip in enumerate(chips)]
            return mine, first

        def start():
            for it in range(n_items):
                mine, first = own_copies(it)
                mine.start()
                for cp in first:
                    cp.start()

        def forward(it):
            def go():
                for j, chip in enumerate(chips):
                    copy(it, 1 + j, (*chip, c), (x, y, c)).wait_recv()
                    copy(it, 4 + j, (*chip, c), sibling).start()
            return go

        def finish():
            for it in range(n_items):
                copy(it, 0, sibling, (x, y, c)).wait_recv()
                for j, chip in enumerate(chips):
                    copy(it, 4 + j, (*chip, 1 - c), (x, y, c)).wait_recv()
            for it in range(n_items):
                mine, first = own_copies(it)
                for cp in first:
                    cp.wait_send()
                for j, chip in enumerate(chips):
                    copy(it, 4 + j, (*chip, c), sibling).wait_send()
                mine.wait()

        lo = max(3 * total // 4, 1)
        acts = [(0, start)]
        for it in range(n_items):
            acts.append((min(lo + it * (total - 1 - lo) // n_items, total - 2), forward(it)))
        acts.append((total - 1, finish))
        return acts

    return _Plan(srcs, out_shapes, n_items * per, n_items, phases)


def _scatter_plan(srcs, out_shapes, items):
    per = N_DEV - 1
    n_items = len(items)

    def phases(ins, outs, send_sems, recv_sems, local_sems, total):
        x, y, c = lax.axis_index("x"), lax.axis_index("y"), lax.axis_index("c")
        me = 4 * x + 2 * y + c

        def copies():
            out = []
            for it, (si, oi, osel) in enumerate(items):
                dst = _sel(outs[oi], osel, me)
                out.append(pltpu.make_async_copy(ins[si].at[me], dst, local_sems.at[it]))
                for j in range(1, N_DEV):
                    px = 1 - x if j & 4 else x
                    py = 1 - y if j & 2 else y
                    pc = 1 - c if j & 1 else c
                    out.append(pltpu.make_async_remote_copy(
                        src_ref=ins[si].at[4 * px + 2 * py + pc], dst_ref=dst,
                        send_sem=send_sems.at[it * per + j - 1], recv_sem=recv_sems.at[it * per + j - 1],
                        device_id=(px, py, pc), device_id_type=pl.DeviceIdType.MESH))
            return out

        def start():
            for cp in copies():
                cp.start()

        def finish():
            for cp in copies():
                cp.wait()

        return [(0, start), (total - 1, finish)]

    return _Plan(srcs, out_shapes, n_items * per, n_items, phases)


def _sum_slots(parts):
    _, r, c = parts.shape

    def body(p_ref, o_ref):
        acc = p_ref[0]
        for dev in range(1, N_DEV):
            acc = acc + p_ref[dev]
        o_ref[...] = acc

    return _call(body, "sum_slots", (1,), [pl.BlockSpec((N_DEV, r, c), lambda i: (0, 0, 0))],
                 pl.BlockSpec((r, c), lambda i: (0, 0)), _sds((r, c), F32))(parts)


def _pack(arrays):
    rows = []
    for a in arrays:
        flat = a.reshape(-1).astype(F32)
        pad = (-flat.shape[0]) % 1024
        if pad:
            flat = jnp.concatenate([flat, jnp.zeros((pad,), F32)])
        rows.append(flat.reshape(-1, 128))
    return jnp.concatenate(rows, axis=0)


def _unpack(buf, shapes):
    out = []
    r0 = 0
    for shp in shapes:
        size = int(np.prod(shp))
        nrows = -(-size // 1024) * 8
        out.append(buf[r0:r0 + nrows].reshape(-1)[:size].reshape(shp))
        r0 += nrows
    return out


def _block_diag_chunks(val):
    eye = jnp.eye(8, dtype=val.dtype)
    return jnp.einsum("cjhp,jk->cjhkp", val, eye).reshape(SSM_CHUNKS, CHUNK_CH, CHUNK_ST)


def _diag_blocks(dense):
    d5 = dense.reshape(SSM_CHUNKS, 8, SSM_GROUP, 8, SSM_STATE)
    return jnp.stack([d5[:, j, :, j, :] for j in range(8)], axis=1)


def _carried(res, comm):
    return res if comm is not None else (res, None)


def _ffn_fwd(tag, hin, hin_b, w_up, conv_w, conv_b, w_down4, gain, bias, seq, comm_up=None, comm_conv=None,
             comm_down=None):
    hc, got_up = _carried(_mm_nn("ffn_up" + tag, hin_b, w_up, FFN_HIDDEN, blocked_out=True, comm=comm_up), comm_up)
    nb, n, c = hc.shape
    hc = hc.reshape(2, nb // 2, n, c)
    (act, cv), got_conv = _carried(_conv_gate_fwd(hc, conv_w, conv_b, seq, comm=comm_conv), comm_conv)
    (r, hout, hout_b), got_down = _carried(
        _mm_nn_red("ffn_down" + tag, act, w_down4, F32, comm=comm_down, residual=hin, norm=(gain, bias)), comm_down)
    return hout, hout_b, (hin_b, hc, cv, act, r), (got_up, got_conv, got_down)


def _ffn_bwd(tag, saved, dh, w_up, conv_w, conv_b, w_down4, gain, seq, comm_conv=None, comm_dw=None,
             comm_dx=None):
    hin_b, hc, cv, act, r = saved
    dr, dr_b, dgain, dbias = _ln_bwd("ffn_norm_bwd" + tag, r, dh, gain)
    d_wdown = _mm_tn("ffn_down_dw" + tag, act, dr_b, act.shape[0], BF16, a_blocked=True, g_mode="shared")
    dact = _mm_nt_out("ffn_down_dx" + tag, dr_b, w_down4, FFN_HIDDEN)
    (dhc, dconv), got_conv = _carried(_conv_gate_bwd(hc, cv, dact, conv_w, seq, comm=comm_conv), comm_conv)
    dhc8 = dhc.reshape(2 * dhc.shape[1], dhc.shape[2], dhc.shape[3])
    d_wup, got_dw = _carried(_mm_tn("ffn_up_dw" + tag, dhc8, hin_b, dhc8.shape[0], BF16, a_blocked=True,
                                    g_mode="shared", comm=comm_dw), comm_dw)
    dhin, got_dx = _carried(_mm_nt_red("ffn_up_dx" + tag, dhc8, w_up, F32, g_blocked=True, add=dr,
                                       add_scale=DN_ALPHA, comm=comm_dx), comm_dx)
    return dhin, d_wup, d_wdown, dconv, dgain, dbias, (got_conv, got_dw, got_dx)


def _local_step(x, target, p, wfull, bsz, seq, ex=None):
    n = bsz * seq
    xf = x.reshape(n, D_MODEL)
    tf = target.reshape(n, D_MODEL)

    lr = p["s5_lam_re"].reshape(1, N_STATE)
    li = p["s5_lam_im"].reshape(1, N_STATE)
    ldt = p["s5_log_dt"].reshape(1, D_MODEL // SSM_GROUP)
    br_t = p["s5_b_re"].reshape(N_STATE, SSM_GROUP).T
    bi_t = p["s5_b_im"].reshape(N_STATE, SSM_GROUP).T
    ab_r, ab_i, bb_r, bb_i, tab = _s5_prep(lr, li, ldt, br_t, bi_t)

    def bb_chunks(bb):
        return _block_diag_chunks(bb.reshape(SSM_GROUP, SSM_CHUNKS, 8, SSM_STATE).transpose(1, 2, 0, 3))

    def c_chunks(cc):
        return _block_diag_chunks(cc.reshape(SSM_CHUNKS, 8, SSM_GROUP, SSM_STATE))

    bbr_c, bbi_c = bb_chunks(bb_r).astype(BF16), bb_chunks(bb_i).astype(BF16)
    cr_c, ci_c = c_chunks(p["s5_c_re"]).astype(BF16), c_chunks(p["s5_c_im"]).astype(BF16)
    dvec = p["s5_d"].reshape(1, D_MODEL)

    if ex is None:
        y, h_r, h_i = _s5_scan_fwd(xf, bbr_c, bbi_c, cr_c, ci_c, dvec, tab, seq)
    else:
        (y, h_r, h_i), gathered = _s5_scan_fwd(xf, bbr_c, bbi_c, cr_c, ci_c, dvec, tab, seq,
                                               comm=ex.gather_first())
        p, wfull = ex.take_first(gathered, p)
    ln_g = p["ln_gain"].reshape(4, 1, D_MODEL)
    ln_b = p["ln_bias"].reshape(4, 1, D_MODEL)
    y2, z, gg = _glu_fwd(y, wfull["s5_w_glu"], p["s5_b_glu"].reshape(1, D_MODEL))
    r1, h1, h1b = _mm_nn("s5_out", gg, wfull["s5_w_out"][None], F32, residual=xf, norm=(ln_g[0], ln_b[0]))
    plans = [ex.gather_attn(k) if ex else None for k in ("kv0", "q", "kv1")]
    h2, h2b, ffn0_saved, got = _ffn_fwd(
        "0", h1, h1b, wfull["ffn_w_up"][0], p["conv_w"][0], p["conv_b"][0], wfull["ffn_w_down"][0],
        ln_g[1], ln_b[1], seq, comm_up=plans[0], comm_conv=plans[1], comm_down=plans[2])
    if ex is not None:
        wfull = ex.take_attn({"kv0": got[0][0], "q": got[1][0], "kv1": got[2][0]}, wfull)

    plan = ex.gather_layer1("up", 0) if ex else None
    kmat, got = _carried(_mm_nn("attn_k", h2b, wfull["attn_w_kv"], BF16, blocks=(0, 4), comm=plan), plan)
    halves = {"up": [got[0]], "down": []} if ex else None
    vmat = _mm_nn("attn_v", h2b, wfull["attn_w_kv"], BF16, blocks=(4, 4))
    plan = ex.gather_attn("out") if ex else None
    qmat, got = _carried(_mm_nn("attn_q", h2b, wfull["attn_w_q"], BF16, comm=plan), plan)
    if ex is not None:
        wfull = ex.take_attn({"out": got[0]}, wfull)
    biases, outs, lses = [], [], []
    for g, dil in enumerate(DILATIONS):
        rel_t = p["rel_bias"][:, g * N_HEADS:(g + 1) * N_HEADS].T
        biases.append(_bias_expand(rel_t, dil).reshape(N_HEADS, BAND, 2 * BAND))
        which, half = {1: ("down", 0), 4: ("down", 1), 16: ("up", 1)}[dil]
        plan = ex.gather_layer1(which, half) if ex else None
        merge = (outs[0], lses[0], outs[1], lses[1]) if dil == DILATIONS[-1] else ()
        (o_g, l_g), got = _carried(_attn_fwd(qmat, kmat, vmat, biases[g], g, dil, bsz, seq, comm=plan,
                                             merge=merge), plan)
        if plan is not None:
            halves[which].append(got[0])
        outs.append(o_g)
        lses.append(l_g)
    if ex is not None:
        wfull = ex.take_layer1(halves, wfull)
    o, lse = outs[-1], lses[-1]
    r3, h3, h3b = _mm_nn("attn_out", o, wfull["attn_w_out"][None], F32, residual=h2, norm=(ln_g[2], ln_b[2]))
    h4, _, ffn1_saved, _ = _ffn_fwd("1", h3, h3b, wfull["ffn_w_up"][1], p["conv_w"][1], p["conv_b"][1],
                                    wfull["ffn_w_down"][1], ln_g[3], ln_b[3], seq)
    loss_tile, dh4 = _loss_head(h4, tf)

    grads = {}
    dh3, grads["ffn_w_up1"], grads["ffn_w_down1"], dconv1, dg3, db3, _ = _ffn_bwd(
        "1", ffn1_saved, dh4, wfull["ffn_w_up"][1], p["conv_w"][1], p["conv_b"][1], wfull["ffn_w_down"][1],
        ln_g[3], seq)
    dr3, dr3b, dg2, db2 = _ln_bwd("attn_norm_bwd", r3, dh3, ln_g[2])
    grads["attn_w_out"] = _mm_tn("attn_out_dw", o, dr3b, 1, BF16, g_mode="shared")
    do = _mm_nt_red("attn_out_dx", dr3b, wfull["attn_w_out"][None], F32, g_blocked=False)
    dqkv, drel = (), []
    for g, dil in enumerate(DILATIONS):
        plan = ex.scatter(grads, ("ffn_w_up1", "ffn_w_down1")) if ex and dil == DILATIONS[-1] else None
        res, got = _carried(_attn_bwd(qmat, kmat, vmat, do, o, lse, biases[g], g, dil, bsz, seq, comm=plan,
                                      into=dqkv), plan)
        if plan is not None:
            ex.keep(plan, got)
        dqkv, db_g = tuple(res[:3]), res[3]
        drel.append(_bias_reduce(db_g.reshape(N_HEADS, BAND * 2 * BAND), dil).T)
    dq, dk, dv = dqkv
    grads["rel_bias"] = jnp.concatenate(drel, axis=1)
    grads["attn_w_q"] = _mm_tn("attn_q_dw", h2b, dq, N_DEV, BF16)
    grads["attn_w_kv"] = jnp.concatenate([_mm_tn("attn_k_dw", h2b, dk, 4, BF16),
                                          _mm_tn("attn_v_dw", h2b, dv, 4, BF16)], axis=0)
    dh2 = _mm_nt_red("attn_q_dx", dq, wfull["attn_w_q"], F32, g_blocked=False, add=dr3, add_scale=DN_ALPHA)
    dh2 = _mm_nt_red("attn_k_dx", dk, wfull["attn_w_kv"], F32, g_blocked=False, add=dh2, blocks=(0, 4))
    dh2 = _mm_nt_red("attn_v_dx", dv, wfull["attn_w_kv"], F32, g_blocked=False, add=dh2, blocks=(4, 4))

    plans = [ex.scatter(grads, names) if ex else None
             for names in (("attn_w_kv#0", "attn_w_out"), ("attn_w_kv#1",), ("attn_w_q",))]
    dh1, grads["ffn_w_up0"], grads["ffn_w_down0"], dconv0, dg1, db1, got = _ffn_bwd(
        "0", ffn0_saved, dh2, wfull["ffn_w_up"][0], p["conv_w"][0], p["conv_b"][0], wfull["ffn_w_down"][0],
        ln_g[1], seq, comm_conv=plans[0], comm_dw=plans[1], comm_dx=plans[2])
    if ex is not None:
        for plan, outs in zip(plans, got):
            ex.keep(plan, outs)
    dr1, dr1b, dg0, db0 = _ln_bwd("s5_norm_bwd", r1, dh1, ln_g[0])
    grads["s5_w_out"] = _mm_tn("s5_out_dw", gg, dr1b, 1, BF16, g_mode="shared")
    dgg = _mm_nt_red("s5_out_dx", dr1b, wfull["s5_w_out"][None], F32, g_blocked=False)
    dz, d_bglu, dy2_direct = _glu_bwd_gate(y, z, dgg)
    grads["s5_w_glu"] = _mm_tn("s5_glu_dw", y2, dz, 1, BF16, g_mode="shared")
    dy = _mm_nt_red("s5_glu_dx", dz, wfull["s5_w_glu"][None], F32, g_blocked=False, add=dy2_direct,
                    post=(_gelu_bwd, y))
    plan_last = ex.scatter(grads, ("ffn_w_up0", "ffn_w_down0", "s5_w_out", "s5_w_glu")) if ex else None
    res = _s5_scan_bwd(xf, dy, dr1, h_r, h_i, bbr_c, bbi_c, cr_c, ci_c, dvec, tab, seq, comm=plan_last)
    if ex is not None:
        res, got = res
        ex.keep(plan_last, got)
    dx, dcr, dci, dbr, dbi, dar, dai, dd = res

    def bb_from_chunks(dense):
        return _diag_blocks(dense).transpose(2, 0, 1, 3).reshape(SSM_GROUP, N_STATE)

    d_lr, d_li, d_ldt, d_br, d_bi = _s5_prep_bwd(lr, li, ldt, br_t, bi_t, dar, dai,
                                                 bb_from_chunks(dbr), bb_from_chunks(dbi))
    grads["s5_lam_re"] = d_lr.reshape(p["s5_lam_re"].shape)
    grads["s5_lam_im"] = d_li.reshape(p["s5_lam_im"].shape)
    grads["s5_log_dt"] = d_ldt.reshape(p["s5_log_dt"].shape)
    grads["s5_b_re"] = d_br.T.reshape(p["s5_b_re"].shape)
    grads["s5_b_im"] = d_bi.T.reshape(p["s5_b_im"].shape)
    grads["s5_c_re"] = _diag_blocks(dcr).reshape(p["s5_c_re"].shape)
    grads["s5_c_im"] = -_diag_blocks(dci).reshape(p["s5_c_im"].shape)
    grads["s5_d"] = dd.reshape(p["s5_d"].shape)
    grads["s5_b_glu"] = d_bglu.reshape(1, D_MODEL)
    dconv = jnp.stack([dconv0, dconv1]).reshape(2, N_DEV, 8, -1)
    grads["ffn_conv_w"] = dconv[:, :, 0:3, :].transpose(0, 2, 1, 3)
    grads["ffn_conv_b"] = dconv[:, :, 3, :]
    grads["ln_gain"] = jnp.stack([dg0, dg1, dg2, dg3]).reshape(2, 2, D_MODEL)
    grads["ln_bias"] = jnp.stack([db0, db1, db2, db3]).reshape(2, 2, D_MODEL)
    return loss_tile[0, 0], dx.reshape(x.shape), grads


SMALL_REPLICATED = ("s5_lam_re", "s5_lam_im", "s5_log_dt", "s5_b_re", "s5_b_im", "s5_c_re", "s5_c_im", "s5_d",
                    "rel_bias", "ffn_conv_b")
SMALL_SHARDED = ("s5_b_glu", "ffn_conv_w", "ln_gain", "ln_bias")
BIG = ("s5_w_glu", "s5_w_out", "attn_w_kv", "attn_w_q", "attn_w_out", "ffn_w_up", "ffn_w_down")
WEIGHTS = ("s5_lam_re", "s5_lam_im", "s5_log_dt", "s5_b_re", "s5_b_im", "s5_c_re", "s5_c_im", "s5_d", "s5_w_glu",
           "s5_b_glu", "s5_w_out", "attn_w_kv", "attn_w_q", "attn_w_out", "rel_bias", "ffn_w_up", "ffn_conv_w",
           "ffn_conv_b", "ffn_w_down", "ln_gain", "ln_bias")


class _Exchanges:
    def __init__(self, w):
        self.w = w
        self.parts = {}
        self.up = w["ffn_w_up"].astype(BF16)
        self.down = w["ffn_w_down"].astype(BF16)

    def gather_first(self):
        w = self.w
        srcs = [w["s5_w_glu"][0].astype(BF16), w["s5_w_out"][0].astype(BF16),
                _pack([w[k] for k in SMALL_SHARDED]), self.up, self.down]
        outs = [_sds((N_DEV,) + (s.shape[1:] if i in (3, 4) else s.shape), s.dtype) for i, s in enumerate(srcs)]
        items = [(i, (0,) if i in (3, 4) else (), i, ("slot",)) for i in range(len(srcs))]
        return _gather_plan(srcs, outs, items)

    def gather_attn(self, piece):
        w = self.w
        if piece in ("kv0", "kv1"):
            src, rows = w["attn_w_kv"].astype(BF16), w["attn_w_kv"].shape[0] // 2
            return _gather_plan([src], [_sds((N_DEV, rows, src.shape[1]), BF16)],
                                [(0, (pl.ds(int(piece[2]) * rows, rows),), 0, ("slot",))])
        src = (w["attn_w_q"] if piece == "q" else w["attn_w_out"])[0].astype(BF16)
        return _gather_plan([src], [_sds((N_DEV,) + src.shape, BF16)], [(0, (), 0, ("slot",))])

    def take_attn(self, pieces, wfull):
        wfull = dict(wfull)
        if "kv0" in pieces:
            wfull["attn_w_kv"] = jnp.concatenate([pieces["kv0"], pieces["kv1"]], axis=1)
            wfull["attn_w_q"] = pieces["q"]
        if "out" in pieces:
            wfull["attn_w_out"] = pieces["out"].reshape(D_MODEL, D_MODEL)
        return wfull

    def take_first(self, g, p):
        d = D_MODEL
        w = self.w
        wfull = {
            "s5_w_glu": g[0].reshape(d, d),
            "s5_w_out": g[1].reshape(d, d),
            "ffn_w_up": [g[3], None],
            "ffn_w_down": [g[4].reshape(4, -1, d), None],
        }
        smalls = [_unpack(g[2][dev], [w[k].shape for k in SMALL_SHARDED]) for dev in range(N_DEV)]
        c_ff = w["ffn_conv_w"].shape[2]
        conv_w = jnp.stack([s[1] for s in smalls], axis=2)
        p = dict(p)
        p.update(s5_b_glu=jnp.concatenate([s[0] for s in smalls], axis=1),
                 ln_gain=jnp.concatenate([s[2] for s in smalls], axis=2),
                 ln_bias=jnp.concatenate([s[3] for s in smalls], axis=2),
                 conv_w=conv_w.transpose(0, 2, 1, 3).reshape(2, 2, 4, 3, c_ff),
                 conv_b=w["ffn_conv_b"].reshape(2, 2, 4, 1, c_ff))
        return p, wfull

    def gather_layer1(self, which, half):
        src = self.up if which == "up" else self.down
        rows = src.shape[1] // 2
        return _gather_plan([src], [_sds((N_DEV, rows) + src.shape[2:], BF16)],
                            [(0, (1, pl.ds(half * rows, rows)), 0, ("slot",))])

    def take_layer1(self, halves, wfull):
        up = jnp.concatenate(halves["up"], axis=1)
        down = jnp.concatenate(halves["down"], axis=1)
        wfull = dict(wfull)
        wfull["ffn_w_up"] = [wfull["ffn_w_up"][0], up]
        wfull["ffn_w_down"] = [wfull["ffn_w_down"][0], down.reshape(4, -1, D_MODEL)]
        return wfull

    def scatter(self, grads, names):
        srcs = []
        for k in names:
            name, _, half = k.partition("#")
            g = grads[name].reshape(N_DEV, -1, grads[name].shape[-1])
            if half:
                rows = g.shape[1] // 2
                g = g[:, int(half) * rows:(int(half) + 1) * rows]
            srcs.append(g)
        plan = _scatter_plan(srcs, [_sds(s.shape, BF16) for s in srcs],
                             [(i, i, ("slot",)) for i in range(len(names))])
        plan.names = names
        return plan

    def keep(self, plan, outs):
        for k, o in zip(plan.names, outs):
            self.parts[k] = o


def kernel(x, s5_lam_re, s5_lam_im, s5_log_dt, s5_b_re, s5_b_im, s5_c_re, s5_c_im, s5_d, s5_w_glu, s5_b_glu, s5_w_out, attn_w_kv, attn_w_q, attn_w_out, rel_bias, ffn_w_up, ffn_conv_w, ffn_conv_b, ffn_w_down, ln_gain, ln_bias, loss_target, m_s5_lam_re, m_s5_lam_im, m_s5_log_dt, m_s5_b_re, m_s5_b_im, m_s5_c_re, m_s5_c_im, m_s5_d, m_s5_w_glu, m_s5_b_glu, m_s5_w_out, m_attn_w_kv, m_attn_w_q, m_attn_w_out, m_rel_bias, m_ffn_w_up, m_ffn_conv_w, m_ffn_conv_b, m_ffn_w_down, m_ln_gain, m_ln_bias, v_s5_lam_re, v_s5_lam_im, v_s5_log_dt, v_s5_b_re, v_s5_b_im, v_s5_c_re, v_s5_c_im, v_s5_d, v_s5_w_glu, v_s5_b_glu, v_s5_w_out, v_attn_w_kv, v_attn_w_q, v_attn_w_out, v_rel_bias, v_ffn_w_up, v_ffn_conv_w, v_ffn_conv_b, v_ffn_w_down, v_ln_gain, v_ln_bias):
    w = dict(s5_lam_re=s5_lam_re, s5_lam_im=s5_lam_im, s5_log_dt=s5_log_dt, s5_b_re=s5_b_re, s5_b_im=s5_b_im,
             s5_c_re=s5_c_re, s5_c_im=s5_c_im, s5_d=s5_d, s5_w_glu=s5_w_glu, s5_b_glu=s5_b_glu, s5_w_out=s5_w_out,
             attn_w_kv=attn_w_kv, attn_w_q=attn_w_q, attn_w_out=attn_w_out, rel_bias=rel_bias, ffn_w_up=ffn_w_up,
             ffn_conv_w=ffn_conv_w, ffn_conv_b=ffn_conv_b, ffn_w_down=ffn_w_down, ln_gain=ln_gain, ln_bias=ln_bias)
    mom = dict(s5_lam_re=m_s5_lam_re, s5_lam_im=m_s5_lam_im, s5_log_dt=m_s5_log_dt, s5_b_re=m_s5_b_re,
               s5_b_im=m_s5_b_im, s5_c_re=m_s5_c_re, s5_c_im=m_s5_c_im, s5_d=m_s5_d, s5_w_glu=m_s5_w_glu,
               s5_b_glu=m_s5_b_glu, s5_w_out=m_s5_w_out, attn_w_kv=m_attn_w_kv, attn_w_q=m_attn_w_q,
               attn_w_out=m_attn_w_out, rel_bias=m_rel_bias, ffn_w_up=m_ffn_w_up, ffn_conv_w=m_ffn_conv_w,
               ffn_conv_b=m_ffn_conv_b, ffn_w_down=m_ffn_w_down, ln_gain=m_ln_gain, ln_bias=m_ln_bias)
    var = dict(s5_lam_re=v_s5_lam_re, s5_lam_im=v_s5_lam_im, s5_log_dt=v_s5_log_dt, s5_b_re=v_s5_b_re,
               s5_b_im=v_s5_b_im, s5_c_re=v_s5_c_re, s5_c_im=v_s5_c_im, s5_d=v_s5_d, s5_w_glu=v_s5_w_glu,
               s5_b_glu=v_s5_b_glu, s5_w_out=v_s5_w_out, attn_w_kv=v_attn_w_kv, attn_w_q=v_attn_w_q,
               attn_w_out=v_attn_w_out, rel_bias=v_rel_bias, ffn_w_up=v_ffn_w_up, ffn_conv_w=v_ffn_conv_w,
               ffn_conv_b=v_ffn_conv_b, ffn_w_down=v_ffn_w_down, ln_gain=v_ln_gain, ln_bias=v_ln_bias)
    bsz, seq, _ = x.shape
    me = 4 * lax.axis_index("x") + 2 * lax.axis_index("y") + lax.axis_index("c")

    ex = _Exchanges(w)
    loss_part, grad_x, g = _local_step(x, loss_target, dict(w), None, bsz, seq, ex=ex)
    loss = lax.psum(loss_part, ("x", "y", "c"))

    d = D_MODEL
    results = {}
    for name in BIG:
        shard = w[name]
        if name == "ffn_w_up":
            parts = [ex.parts[name + "0"], ex.parts[name + "1"]]
            outs4 = _adamw("adamw_" + name, parts, *(jnp.swapaxes(a, 1, 2) for a in (shard, mom[name], var[name])))
            results[name] = tuple(jnp.swapaxes(o, 1, 2) for o in outs4)
            continue
        if name == "ffn_w_down":
            parts, r3 = [ex.parts[name + "0"], ex.parts[name + "1"]], shard.shape
        elif name == "attn_w_kv":
            parts = [jnp.concatenate([ex.parts[name + "#0"], ex.parts[name + "#1"]], axis=1)]
            r3 = (1, -1, shard.shape[-1])
        else:
            parts, r3 = [ex.parts[name]], (1, -1, shard.shape[-1])
        outs4 = _adamw("adamw_" + name, parts, shard.reshape(r3), mom[name].reshape(r3), var[name].reshape(r3))
        results[name] = tuple(o.reshape(shard.shape) for o in outs4)

    small_names = SMALL_REPLICATED + SMALL_SHARDED
    packed = _pack([g[k] for k in small_names])
    pad = (-packed.shape[0]) % (8 * N_DEV)
    if pad:
        packed = jnp.concatenate([packed, jnp.zeros((pad, 128), F32)], axis=0)
    piece = packed.shape[0] // N_DEV
    slots = _exchange("reduce_small", [packed.reshape(N_DEV, piece, 128)], [_sds((N_DEV, piece, 128), F32)],
                      [(0, ("peer",), 0, ("me",))])[0]
    summed = _exchange("gather_small", [_sum_slots(slots)], [_sds((N_DEV, piece, 128), F32)],
                       [(0, (), 0, ("me",))])[0]
    total = _unpack(summed.reshape(N_DEV * piece, 128), [g[k].shape for k in small_names])
    gsum = dict(zip(small_names, total))
    mine = {k: gsum[k] for k in SMALL_REPLICATED}
    mine["s5_b_glu"] = lax.dynamic_slice_in_dim(gsum["s5_b_glu"], me * (d // N_DEV), d // N_DEV, axis=1)
    mine["ffn_conv_w"] = lax.dynamic_index_in_dim(gsum["ffn_conv_w"], me, axis=2, keepdims=False)
    mine["ln_gain"] = lax.dynamic_slice_in_dim(gsum["ln_gain"], me * (d // N_DEV), d // N_DEV, axis=2)
    mine["ln_bias"] = lax.dynamic_slice_in_dim(gsum["ln_bias"], me * (d // N_DEV), d // N_DEV, axis=2)
    gp = _pack([mine[k] for k in small_names])
    outs4 = _adamw("adamw_small", [gp[None]], _pack([w[k] for k in small_names])[None],
                   _pack([mom[k] for k in small_names])[None], _pack([var[k] for k in small_names])[None])
    outs4 = [o[0] for o in outs4]
    shapes = [w[k].shape for k in small_names]
    unpacked = [_unpack(o, shapes) for o in outs4]
    for i, k in enumerate(small_names):
        results[k] = tuple(unpacked[j][i] for j in range(4))

    out = [loss, grad_x]
    for j in range(4):
        out.extend(results[k][j] for k in WEIGHTS)
    return tuple(out)
```

```python
import math

import numpy as np
import jax
import jax.numpy as jnp
from jax import lax
from jax.experimental import pallas as pl
from jax.experimental.pallas import tpu as pltpu

F32 = jnp.float32
BF16 = jnp.bfloat16
HIGHEST = lax.Precision.HIGHEST

N_DEV = 8
D_MODEL = 1024
SSM_GROUP = 16
SSM_STATE = 64
SSM_CHUNKS = 8
CHUNK_CH = D_MODEL // SSM_CHUNKS
CHUNK_ST = CHUNK_CH // SSM_GROUP * SSM_STATE
N_STATE = D_MODEL // SSM_GROUP * SSM_STATE
HEAD_DIM = 64
N_HEADS = 16
BAND = 128
DILATIONS = (1, 4, 16)
REL_BUCKETS = 32
REL_MAX_DIST = 2048
NEG_BIG = -1e30
DN_ALPHA = (2.0 * 2) ** 0.25
LN_EPS = 1e-5
ADAM_LR = 0.001
ADAM_B1 = 0.9
ADAM_B2 = 0.999
ADAM_EPS = 1e-08
ADAM_WD = 0.01
ADAM_STEP = 10

TM = 512
TM_MM = 2048
TM_MM_F32 = 1024
T_SCAN = 1024
VMEM_LIMIT = 56 * 1024 * 1024
FFN_HIDDEN = BF16

NN = (((1,), (0,)), ((), ()))
NT = (((1,), (1,)), ((), ()))
TN = (((0,), (0,)), ((), ()))


class _Plan:
    def __init__(self, srcs, out_shapes, n_sems, n_local, phases):
        self.srcs, self.out_shapes, self.n_sems, self.n_local, self.phases = srcs, out_shapes, n_sems, n_local, phases


def _call(body, name, grid, in_specs, out_specs, out_shape, scratch=(), comm=None, fill=()):
    params = dict(dimension_semantics=("arbitrary",) * len(grid), vmem_limit_bytes=VMEM_LIMIT)
    if comm is None and not fill:
        return pl.pallas_call(
            body, name=name, grid=grid, in_specs=in_specs, out_specs=out_specs, out_shape=out_shape,
            scratch_shapes=list(scratch), compiler_params=pltpu.CompilerParams(**params))
    single = not isinstance(out_specs, (list, tuple))
    o_specs = [out_specs] if single else list(out_specs)
    o_shape = [out_shape] if single else list(out_shape)
    n_in, n_out, n_scr, n_fill = len(in_specs), len(o_specs), len(scratch), len(fill)
    srcs = list(comm.srcs) if comm else []
    c_shapes = list(comm.out_shapes) if comm else []
    n_cin, n_cout = len(srcs), len(c_shapes)
    total = int(np.prod(grid))

    def wrapped(*refs):
        ins = refs[:n_in]
        refs = refs[n_in + n_fill:]
        cins, outs, couts = refs[:n_cin], refs[n_cin:n_cin + n_out], refs[n_cin + n_out:n_cin + n_out + n_cout]
        rest = refs[n_cin + n_out + n_cout:]
        scr, sems = rest[:n_scr], rest[n_scr:]
        if comm is None:
            body(*ins, *outs, *scr)
            return
        step = pl.program_id(0)
        for ax in range(1, len(grid)):
            step = step * grid[ax] + pl.program_id(ax)
        phases = comm.phases(cins, couts, *sems, total)
        for at, action in phases:
            if at == 0:
                pl.when(step == 0)(action)
        body(*ins, *outs, *scr)
        for at, action in phases:
            if at > 0:
                pl.when(step == at)(action)

    any_spec = pl.BlockSpec(memory_space=pl.ANY)
    sems = [] if comm is None else [pltpu.SemaphoreType.DMA((comm.n_sems,)), pltpu.SemaphoreType.DMA((comm.n_sems,)),
                                    pltpu.SemaphoreType.DMA((max(comm.n_local, 1),))]
    call = pl.pallas_call(
        wrapped, name=name, grid=grid, in_specs=list(in_specs) + [any_spec] * (n_fill + n_cin),
        out_specs=o_specs + [any_spec] * n_cout, out_shape=o_shape + c_shapes,
        scratch_shapes=list(scratch) + sems, input_output_aliases={n_in + j: j for j in range(n_fill)},
        compiler_params=pltpu.CompilerParams(has_side_effects=comm is not None, **params))

    def run(*args):
        res = call(*args, *fill, *srcs)
        own = res[0] if single else res[:n_out]
        return (own, res[n_out:]) if comm is not None else own

    return run


def _sds(shape, dtype):
    return jax.ShapeDtypeStruct(shape, dtype)


def _mm(name, a, b, *, dims, grid, a_spec, b_spec, o_spec, out_shape, nred=1, red_axis=0,
        add=None, add_spec=None, add_scale=1.0, comm=None, norm=None, inner=None, post=None):
    has_add = add is not None
    n_extra = (1 if has_add else 0) + (1 if post else 0) + (2 if norm else 0)
    n_out = 3 if norm else 1
    acc_shape = tuple(s for s in o_spec.block_shape if s is not None)

    def body(*refs):
        a_ref, b_ref = refs[:2]
        extra = refs[2:2 + n_extra]
        outs = refs[2 + n_extra:2 + n_extra + n_out]
        rest = refs[2 + n_extra + n_out:]
        o_ref = outs[0]
        if inner is None:
            prod = lax.dot_general(a_ref[...].astype(BF16), b_ref[...].astype(BF16), dims,
                                   preferred_element_type=F32)
        else:
            nb, a_mode = inner
            width = a_ref.shape[-1] // nb
            prod = None
            for d in range(nb):
                a_d = a_ref[d] if a_mode == "lead" else a_ref[:, d * width:(d + 1) * width]
                part = lax.dot_general(a_d.astype(BF16), b_ref[d].astype(BF16), dims, preferred_element_type=F32)
                prod = part if prod is None else prod + part

        def finish(val):
            if has_add:
                val = val + add_scale * extra[0][...].astype(F32)
            if post:
                val = post[0](val, extra[1 if has_add else 0][...])
            o_ref[...] = val.astype(o_ref.dtype)
            if norm:
                xhat, _ = _ln_stats(val)
                y = xhat * extra[-2][...] + extra[-1][...]
                outs[1][...] = y
                outs[2][...] = y.astype(BF16)

        if nred == 1:
            finish(prod)
        else:
            acc = rest[0]
            k = pl.program_id(red_axis)

            @pl.when(k == 0)
            def _():
                acc[...] = prod

            @pl.when(k > 0)
            def _():
                acc[...] += prod

            @pl.when(k == nred - 1)
            def _():
                finish(acc[...])

    in_specs = [a_spec, b_spec] + ([add_spec] if has_add else [])
    scratch = [pltpu.VMEM(acc_shape, F32)] if nred > 1 else []
    args = (a, b) + ((add,) if has_add else ())
    if post:
        in_specs.append(o_spec)
        args += (post[1],)
    if norm:
        vec = pl.BlockSpec((1, out_shape.shape[1]), lambda *_: (0, 0))
        in_specs += [vec, vec]
        args += tuple(norm)
        o_spec = [o_spec, o_spec, o_spec]
        out_shape = [out_shape, out_shape, _sds(out_shape.shape, BF16)]
    return _call(body, name, grid, in_specs, o_spec, out_shape, scratch, comm=comm)(*args)


def _mm_nn(name, a, b, out_dtype, blocked_out=False, comm=None, blocks=None, residual=None, norm=None):
    TM = TM_MM if out_dtype == BF16 else TM_MM_F32
    m, k = a.shape
    _, _, n = b.shape
    nb = b.shape[0] if blocks is None else blocks[1]
    first = 0 if blocks is None else blocks[0]
    if blocked_out:
        o_spec = pl.BlockSpec((None, TM, n), lambda d, i: (d, i, 0))
        out_shape = _sds((nb, m, n), out_dtype)
    else:
        o_spec = pl.BlockSpec((TM, n), lambda d, i: (i, d))
        out_shape = _sds((m, nb * n), out_dtype)
    return _mm(name, a, b, dims=NN, grid=(nb, m // TM),
               a_spec=pl.BlockSpec((TM, k), lambda d, i: (i, 0)),
               b_spec=pl.BlockSpec((None, k, n), lambda d, i: (d + first, 0, 0)),
               o_spec=o_spec, out_shape=out_shape, comm=comm, norm=norm,
               add=residual, add_spec=o_spec if residual is not None else None, add_scale=DN_ALPHA)


def _mm_nn_red(name, a, b, out_dtype, comm=None, residual=None, norm=None):
    nb, m, k = a.shape
    n = b.shape[2]
    o_spec = pl.BlockSpec((TM, n), lambda i: (i, 0))
    return _mm(name, a, b, dims=NN, grid=(m // TM,), inner=(nb, "lead"),
               a_spec=pl.BlockSpec((nb, TM, k), lambda i: (0, i, 0)),
               b_spec=pl.BlockSpec((nb, k, n), lambda i: (0, 0, 0)),
               o_spec=o_spec, out_shape=_sds((m, n), out_dtype), comm=comm, norm=norm,
               add=residual, add_spec=o_spec if residual is not None else None, add_scale=DN_ALPHA)


def _mm_tn(name, a, g, nb, out_dtype, a_blocked=False, g_mode="cols", comm=None):
    TM = TM_MM
    if a_blocked:
        _, m, ka = a.shape
        a_spec = pl.BlockSpec((None, TM, ka), lambda d, i: (d, i, 0))
    else:
        m, ka = a.shape
        a_spec = pl.BlockSpec((TM, ka), lambda d, i: (i, 0))
    if g_mode == "cols":
        n = g.shape[1] // nb
        g_spec = pl.BlockSpec((TM, n), lambda d, i: (i, d))
    elif g_mode == "blocked":
        n = g.shape[2]
        g_spec = pl.BlockSpec((None, TM, n), lambda d, i: (d, i, 0))
    else:
        n = g.shape[1]
        g_spec = pl.BlockSpec((TM, n), lambda d, i: (i, 0))
    nred = m // TM
    return _mm(name, a, g, dims=TN, grid=(nb, nred), nred=nred, red_axis=1,
               a_spec=a_spec, b_spec=g_spec,
               o_spec=pl.BlockSpec((None, ka, n), lambda d, i: (d, 0, 0)),
               out_shape=_sds((nb, ka, n), out_dtype), comm=comm)


def _mm_nt_red(name, g, w, out_dtype, g_blocked, add=None, add_scale=1.0, comm=None, blocks=None, post=None):
    _, k, n = w.shape
    nb = w.shape[0] if blocks is None else blocks[1]
    first = 0 if blocks is None else blocks[0]
    tm = TM_MM_F32 if nb * k * n <= 4 * 1024 * 768 else TM
    if g_blocked:
        m = g.shape[1]
        g_spec = pl.BlockSpec((nb, tm, n), lambda i: (0, i, 0))
    else:
        m = g.shape[0]
        g_spec = pl.BlockSpec((tm, nb * n), lambda i: (i, 0))
    o_spec = pl.BlockSpec((tm, k), lambda i: (i, 0))
    return _mm(name, g, w, dims=NT, grid=(m // tm,), inner=(nb, "lead" if g_blocked else "cols"),
               a_spec=g_spec, b_spec=pl.BlockSpec((nb, k, n), lambda i: (first // nb, 0, 0)),
               o_spec=o_spec, out_shape=_sds((m, k), out_dtype),
               add=add, add_spec=o_spec if add is not None else None, add_scale=add_scale, comm=comm, post=post)


def _mm_nt_out(name, g, w, out_dtype):
    TM = TM_MM
    m, n = g.shape
    nb, k, _ = w.shape
    return _mm(name, g, w, dims=NT, grid=(nb, m // TM),
               a_spec=pl.BlockSpec((TM, n), lambda d, i: (i, 0)),
               b_spec=pl.BlockSpec((None, k, n), lambda d, i: (d, 0, 0)),
               o_spec=pl.BlockSpec((None, TM, k), lambda d, i: (d, i, 0)),
               out_shape=_sds((nb, m, k), out_dtype))


def _ln_stats(r):
    mu = jnp.mean(r, axis=-1, keepdims=True)
    xc = r - mu
    var = jnp.mean(xc * xc, axis=-1, keepdims=True)
    rstd = lax.rsqrt(var + LN_EPS)
    return xc * rstd, rstd


def _ln_bwd(name, r, dy, gain):
    n, d = r.shape

    def body(r_ref, dy_ref, g_ref, dr_ref, drb_ref, dg_ref, db_ref):
        i = pl.program_id(0)
        xhat, rstd = _ln_stats(r_ref[...])
        dy_v = dy_ref[...]
        dxh = dy_v * g_ref[...]
        m1 = jnp.mean(dxh, axis=-1, keepdims=True)
        m2 = jnp.mean(dxh * xhat, axis=-1, keepdims=True)
        dr = rstd * (dxh - m1 - xhat * m2)
        dr_ref[...] = dr
        drb_ref[...] = dr.astype(BF16)
        dg = jnp.sum(dy_v * xhat, axis=0, keepdims=True)
        db = jnp.sum(dy_v, axis=0, keepdims=True)

        @pl.when(i == 0)
        def _():
            dg_ref[...] = dg
            db_ref[...] = db

        @pl.when(i > 0)
        def _():
            dg_ref[...] += dg
            db_ref[...] += db

    row = pl.BlockSpec((TM_MM_F32, d), lambda i: (i, 0))
    vec = pl.BlockSpec((1, d), lambda i: (0, 0))
    return _call(body, name, (n // TM_MM_F32,), [row, row, vec], [row, row, vec, vec],
                 [_sds((n, d), F32), _sds((n, d), BF16), _sds((1, d), F32), _sds((1, d), F32)])(r, dy, gain)


def _loss_head(y, target):
    n, d = y.shape

    def body(y_ref, t_ref, l_ref, dy_ref):
        i = pl.program_id(0)
        e = y_ref[...] - t_ref[...]
        dy_ref[...] = e * (1.0 / d)
        part = jnp.sum(jnp.sum(e * e, axis=1, keepdims=True), axis=0, keepdims=True) * (0.5 / d)
        part = jnp.broadcast_to(part, (8, 128))

        @pl.when(i == 0)
        def _():
            l_ref[...] = part

        @pl.when(i > 0)
        def _():
            l_ref[...] += part

    row = pl.BlockSpec((TM, d), lambda i: (i, 0))
    return _call(body, "loss_head", (n // TM,), [row, row],
                 [pl.BlockSpec((8, 128), lambda i: (0, 0)), row],
                 [_sds((8, 128), F32), _sds((n, d), F32)])(y, target)


def _group_expand_matrix():
    e = np.zeros((D_MODEL // SSM_GROUP, N_STATE), np.float32)
    for g in range(D_MODEL // SSM_GROUP):
        e[g, g * SSM_STATE:(g + 1) * SSM_STATE] = 1.0
    return jnp.asarray(e)


def _discretise(lr, li, ldt, br, bi, expand):
    dt = jnp.dot(jnp.exp(ldt), expand, precision=HIGHEST, preferred_element_type=F32)
    mag = jnp.exp(lr * dt)
    th = li * dt
    ab_r = mag * jnp.cos(th)
    ab_i = mag * jnp.sin(th)
    den = lr * lr + li * li
    nr = ab_r - 1.0
    co_r = (nr * lr + ab_i * li) / den
    co_i = (ab_i * lr - nr * li) / den
    bb_r = co_r * br - co_i * bi
    bb_i = co_r * bi + co_i * br
    return ab_r, ab_i, bb_r, bb_i


def _cmul(ar, ai, br, bi):
    return ar * br - ai * bi, ar * bi + ai * br


def _s5_prep(lr, li, ldt, br, bi):
    s = N_STATE
    expand = _group_expand_matrix()

    def body(lr_ref, li_ref, ldt_ref, br_ref, bi_ref, e_ref, abr_ref, abi_ref, bbr_ref, bbi_ref, tab_ref):
        ab_r, ab_i, bb_r, bb_i = _discretise(lr_ref[...], li_ref[...], ldt_ref[...], br_ref[...],
                                             bi_ref[...], e_ref[...])
        abr_ref[...] = ab_r
        abi_ref[...] = ab_i
        bbr_ref[...] = bb_r
        bbi_ref[...] = bb_i
        row = lax.broadcasted_iota(jnp.int32, (8, s), 0)
        for base, sign in ((0, 1.0), (8, -1.0)):
            p = [None] * 9
            p[1] = (ab_r, sign * ab_i)
            p[2] = _cmul(*p[1], *p[1])
            p[3] = _cmul(*p[2], *p[1])
            p[4] = _cmul(*p[2], *p[2])
            p[5] = _cmul(*p[4], *p[1])
            p[6] = _cmul(*p[4], *p[2])
            p[7] = _cmul(*p[4], *p[3])
            p[8] = _cmul(*p[4], *p[4])
            for j, k in enumerate((1, 2, 4)):
                keep = (row >= k) if base == 0 else (row < 8 - k)
                tab_ref[base + 2 * j] = jnp.where(keep, jnp.broadcast_to(p[k][0], (8, s)), 0.0)
                tab_ref[base + 2 * j + 1] = jnp.where(keep, jnp.broadcast_to(p[k][1], (8, s)), 0.0)
            pw_r = jnp.zeros((8, s), F32)
            pw_i = jnp.zeros((8, s), F32)
            for i in range(8):
                e = i + 1 if base == 0 else 8 - i
                pw_r = jnp.where(row == i, jnp.broadcast_to(p[e][0], (8, s)), pw_r)
                pw_i = jnp.where(row == i, jnp.broadcast_to(p[e][1], (8, s)), pw_i)
            tab_ref[base + 6] = pw_r
            tab_ref[base + 7] = pw_i

    def full(shape):
        return pl.BlockSpec(shape, lambda i: (0,) * len(shape))

    g = D_MODEL // SSM_GROUP
    return _call(body, "s5_prep", (1,),
                 [full((1, s)), full((1, s)), full((1, g)), full((16, s)), full((16, s)), full((g, s))],
                 [full((1, s)), full((1, s)), full((16, s)), full((16, s)), full((16, 8, s))],
                 [_sds((1, s), F32), _sds((1, s), F32), _sds((16, s), F32), _sds((16, s), F32),
                  _sds((16, 8, s), F32)])(lr, li, ldt, br, bi, expand)


def _s5_prep_bwd(lr, li, ldt, br, bi, d_abr, d_abi, d_bbr, d_bbi):
    s = N_STATE
    g = D_MODEL // SSM_GROUP
    expand = _group_expand_matrix()

    def body(lr_ref, li_ref, ldt_ref, br_ref, bi_ref, e_ref, c1, c2, c3, c4, o1, o2, o3, o4, o5):
        e_val = e_ref[...]
        _, vjp = jax.vjp(lambda a, b, c, d, e: _discretise(a, b, c, d, e, e_val),
                         lr_ref[...], li_ref[...], ldt_ref[...], br_ref[...], bi_ref[...])
        grads = vjp((c1[...], c2[...], c3[...], c4[...]))
        for o, gr in zip((o1, o2, o3, o4, o5), grads):
            o[...] = gr

    def full(shape):
        return pl.BlockSpec(shape, lambda i: (0,) * len(shape))

    return _call(body, "s5_prep_bwd", (1,),
                 [full((1, s)), full((1, s)), full((1, g)), full((16, s)), full((16, s)), full((g, s)),
                  full((1, s)), full((1, s)), full((16, s)), full((16, s))],
                 [full((1, s)), full((1, s)), full((1, g)), full((16, s)), full((16, s))],
                 [_sds((1, s), F32), _sds((1, s), F32), _sds((1, g), F32), _sds((16, s), F32),
                  _sds((16, s), F32)])(lr, li, ldt, br, bi, expand, d_abr, d_abi, d_bbr, d_bbi)


def _scan8(vr, vi, tab_ref, base, reverse):
    for j, k in enumerate((1, 2, 4)):
        mr = tab_ref[base + 2 * j]
        mi = tab_ref[base + 2 * j + 1]
        shift = 8 - k if reverse else k
        sr = pltpu.roll(vr, shift, 0)
        si = pltpu.roll(vi, shift, 0)
        vr, vi = vr + mr * sr - mi * si, vi + mr * si + mi * sr
    return vr, vi


def _s5_scan_fwd(x, bbr, bbi, cr, ci, dvec, tab, seq, comm=None):
    n, d = x.shape
    nt = seq // T_SCAN
    nblk = T_SCAN // 8
    st = CHUNK_ST

    def body(x_ref, bbr_ref, bbi_ref, cr_ref, ci_ref, d_ref, tab_ref, y_ref, hr_ref, hi_ref, car_r, car_i):
        t = pl.program_id(2)

        @pl.when(t == 0)
        def _():
            car_r[...] = jnp.zeros_like(car_r)
            car_i[...] = jnp.zeros_like(car_i)

        u = x_ref[...]
        ub = u.astype(BF16)
        hr_ref[...] = jnp.dot(ub, bbr_ref[...], preferred_element_type=F32)
        hi_ref[...] = jnp.dot(ub, bbi_ref[...], preferred_element_type=F32)

        def step(i, carry):
            sl = pl.ds(pl.multiple_of(i * 8, 8), 8)
            vr, vi = _scan8(hr_ref[sl, :], hi_ref[sl, :], tab_ref, 0, False)
            c_r = jnp.broadcast_to(car_r[...], (8, st))
            c_i = jnp.broadcast_to(car_i[...], (8, st))
            pr = tab_ref[6]
            pi = tab_ref[7]
            vr, vi = vr + pr * c_r - pi * c_i, vi + pr * c_i + pi * c_r
            hr_ref[sl, :] = vr
            hi_ref[sl, :] = vi
            car_r[...] = vr[7:8, :]
            car_i[...] = vi[7:8, :]
            return carry

        lax.fori_loop(0, nblk, step, 0, unroll=4)
        y = lax.dot_general(hr_ref[...].astype(BF16), cr_ref[...], NT, preferred_element_type=F32)
        y = y - lax.dot_general(hi_ref[...].astype(BF16), ci_ref[...], NT, preferred_element_type=F32)
        y_ref[...] = y + d_ref[...] * u

    row = lambda c, b, t: (b * nt + t, c)
    mat = pl.BlockSpec((None, CHUNK_CH, st), lambda c, b, t: (c, 0, 0))
    return _call(
        body, "s5_scan_fwd", (SSM_CHUNKS, n // seq, nt),
        [pl.BlockSpec((T_SCAN, CHUNK_CH), row), mat, mat, mat, mat,
         pl.BlockSpec((1, CHUNK_CH), lambda c, b, t: (0, c)),
         pl.BlockSpec((16, 8, st), lambda c, b, t: (0, 0, c))],
        [pl.BlockSpec((T_SCAN, CHUNK_CH), row), pl.BlockSpec((T_SCAN, st), row), pl.BlockSpec((T_SCAN, st), row)],
        [_sds((n, d), F32), _sds((n, N_STATE), F32), _sds((n, N_STATE), F32)],
        [pltpu.VMEM((1, st), F32), pltpu.VMEM((1, st), F32)], comm=comm,
    )(x, bbr, bbi, cr, ci, dvec, tab)


def _s5_scan_bwd(x, dy, dres, hr, hi, bbr, bbi, cr, ci, dvec, tab, seq, comm=None):
    n, d = x.shape
    nt = seq // T_SCAN
    nblk = T_SCAN // 8
    st = CHUNK_ST

    def body(x_ref, dy_ref, dres_ref, hr_ref, hi_ref, hpr_ref, hpi_ref, bbr_ref, bbi_ref, cr_ref, ci_ref,
             d_ref, tab_ref, dx_ref, dcr_ref, dci_ref, dbr_ref, dbi_ref, dar_ref, dai_ref, dd_ref,
             g_r, g_i, car_r, car_i):
        b = pl.program_id(1)
        tg = pl.program_id(2)
        first = jnp.logical_and(b == 0, tg == 0)

        @pl.when(tg == 0)
        def _():
            car_r[...] = jnp.zeros_like(car_r)
            car_i[...] = jnp.zeros_like(car_i)

        u = x_ref[...]
        dyv = dy_ref[...]
        ub = u.astype(BF16)
        dyb = dyv.astype(BF16)
        g_r[...] = jnp.dot(dyb, cr_ref[...], preferred_element_type=F32)
        g_i[...] = -jnp.dot(dyb, ci_ref[...], preferred_element_type=F32)

        def step(ii, carry):
            i = nblk - 1 - ii
            sl = pl.ds(pl.multiple_of(i * 8, 8), 8)
            vr, vi = _scan8(g_r[sl, :], g_i[sl, :], tab_ref, 8, True)
            c_r = jnp.broadcast_to(car_r[...], (8, st))
            c_i = jnp.broadcast_to(car_i[...], (8, st))
            pr = tab_ref[14]
            pi = tab_ref[15]
            vr, vi = vr + pr * c_r - pi * c_i, vi + pr * c_i + pi * c_r
            g_r[sl, :] = vr
            g_i[sl, :] = vi
            car_r[...] = vr[0:1, :]
            car_i[...] = vi[0:1, :]
            return carry

        lax.fori_loop(0, nblk, step, 0, unroll=4)
        gr = g_r[...]
        gi = g_i[...]
        grb = gr.astype(BF16)
        gib = gi.astype(BF16)
        du = lax.dot_general(grb, bbr_ref[...], NT, preferred_element_type=F32)
        du = du + lax.dot_general(gib, bbi_ref[...], NT, preferred_element_type=F32)
        dx_ref[...] = DN_ALPHA * dres_ref[...] + du + d_ref[...] * dyv

        h_r = hr_ref[...]
        h_i = hi_ref[...]
        t_orig = nt - 1 - tg
        row0 = lax.broadcasted_iota(jnp.int32, (8, st), 0) == 0
        halo_ok = t_orig > 0

        def previous(h, halo_ref):
            top = jnp.broadcast_to(jnp.where(halo_ok, halo_ref[7:8, :], 0.0), (8, st))
            rolled = pltpu.roll(h, 1, 0)
            return jnp.concatenate([jnp.where(row0, top, rolled[:8, :]), rolled[8:, :]], axis=0)

        hp_r = previous(h_r, hpr_ref)
        hp_i = previous(h_i, hpi_ref)
        dar = jnp.sum(gr * hp_r + gi * hp_i, axis=0, keepdims=True)
        dai = jnp.sum(gi * hp_r - gr * hp_i, axis=0, keepdims=True)
        dcr = lax.dot_general(dyb, h_r.astype(BF16), TN, preferred_element_type=F32)
        dci = lax.dot_general(dyb, h_i.astype(BF16), TN, preferred_element_type=F32)
        dbr = lax.dot_general(ub, grb, TN, preferred_element_type=F32)
        dbi = lax.dot_general(ub, gib, TN, preferred_element_type=F32)
        dd = jnp.sum(dyv * u, axis=0, keepdims=True)

        @pl.when(first)
        def _():
            dcr_ref[...] = dcr
            dci_ref[...] = dci
            dbr_ref[...] = dbr
            dbi_ref[...] = dbi
            dar_ref[...] = dar
            dai_ref[...] = dai
            dd_ref[...] = dd

        @pl.when(jnp.logical_not(first))
        def _():
            dcr_ref[...] += dcr
            dci_ref[...] += dci
            dbr_ref[...] += dbr
            dbi_ref[...] += dbi
            dar_ref[...] += dar
            dai_ref[...] += dai
            dd_ref[...] += dd

    row = lambda c, b, t: (b * nt + (nt - 1 - t), c)
    halo = lambda c, b, t: (jnp.maximum((b * nt + (nt - 1 - t)) * (T_SCAN // 8) - 1, 0), c)
    mat = pl.BlockSpec((None, CHUNK_CH, st), lambda c, b, t: (c, 0, 0))
    chan = pl.BlockSpec((T_SCAN, CHUNK_CH), row)
    state = pl.BlockSpec((T_SCAN, st), row)
    return _call(
        body, "s5_scan_bwd", (SSM_CHUNKS, n // seq, nt),
        [chan, chan, chan, state, state, pl.BlockSpec((8, st), halo), pl.BlockSpec((8, st), halo),
         mat, mat, mat, mat, pl.BlockSpec((1, CHUNK_CH), lambda c, b, t: (0, c)),
         pl.BlockSpec((16, 8, st), lambda c, b, t: (0, 0, c))],
        [chan, mat, mat, mat, mat, pl.BlockSpec((1, st), lambda c, b, t: (0, c)),
         pl.BlockSpec((1, st), lambda c, b, t: (0, c)), pl.BlockSpec((1, CHUNK_CH), lambda c, b, t: (0, c))],
        [_sds((n, d), F32)] + [_sds((SSM_CHUNKS, CHUNK_CH, st), F32)] * 4
        + [_sds((1, N_STATE), F32), _sds((1, N_STATE), F32), _sds((1, d), F32)],
        [pltpu.VMEM((T_SCAN, st), F32), pltpu.VMEM((T_SCAN, st), F32), pltpu.VMEM((1, st), F32),
         pltpu.VMEM((1, st), F32)], comm=comm,
    )(x, dy, dres, hr, hi, hr, hi, bbr, bbi, cr, ci, dvec, tab)


_GELU_C = math.sqrt(2.0 / math.pi)


def _gelu_parts(y):
    t = jnp.tanh(_GELU_C * (y + 0.044715 * (y * y * y)))
    return 0.5 * (1.0 + t), t


def _glu_fwd(y, w_glu, b_glu):
    n, d = y.shape

    def body(y_ref, w_ref, b_ref, y2_ref, z_ref, gg_ref):
        yv = y_ref[...]
        cdf, _ = _gelu_parts(yv)
        y2 = yv * cdf
        z = jnp.dot(y2.astype(BF16), w_ref[...], preferred_element_type=F32) + b_ref[...]
        y2_ref[...] = y2.astype(BF16)
        z_ref[...] = z
        gg_ref[...] = (y2 * jax.nn.sigmoid(z)).astype(BF16)

    row = pl.BlockSpec((TM_MM_F32, d), lambda i: (i, 0))
    return _call(body, "glu_fwd", (n // TM_MM_F32,),
                 [row, pl.BlockSpec((d, d), lambda i: (0, 0)), pl.BlockSpec((1, d), lambda i: (0, 0))],
                 [row, row, row], [_sds((n, d), BF16), _sds((n, d), F32), _sds((n, d), BF16)])(y, w_glu, b_glu)


def _glu_bwd_gate(y, z, dgg, comm=None):
    n, d = y.shape

    def body(y_ref, z_ref, dgg_ref, dz_ref, db_ref, t_ref):
        i = pl.program_id(0)
        yv = y_ref[...]
        cdf, _ = _gelu_parts(yv)
        y2 = yv * cdf
        s = jax.nn.sigmoid(z_ref[...])
        dg = dgg_ref[...]
        dz = dg * y2 * s * (1.0 - s)
        dz_ref[...] = dz.astype(BF16)
        t_ref[...] = dg * s
        db = jnp.sum(dz, axis=0, keepdims=True)

        @pl.when(i == 0)
        def _():
            db_ref[...] = db

        @pl.when(i > 0)
        def _():
            db_ref[...] += db

    row = pl.BlockSpec((TM, d), lambda i: (i, 0))
    vec = pl.BlockSpec((1, d), lambda i: (0, 0))
    return _call(body, "glu_bwd_gate", (n // TM,), [row, row, row], [row, vec, row],
                 [_sds((n, d), BF16), _sds((1, d), F32), _sds((n, d), F32)], comm=comm)(y, z, dgg)


def _gelu_bwd(dy2, y):
    cdf, t = _gelu_parts(y)
    dinner = _GELU_C * (1.0 + 3.0 * 0.044715 * y * y)
    return dy2 * (cdf + 0.5 * y * (1.0 - t * t) * dinner)


HALO = 16


def _shift_down(x, halo, k):
    out = pltpu.roll(x, k, 0)
    row = lax.broadcasted_iota(jnp.int32, (8, x.shape[1]), 0)
    top = jnp.where(row < k, pltpu.roll(halo[HALO - 8:HALO, :], k, 0), out[:8, :])
    return jnp.concatenate([top, out[8:, :]], axis=0)


def _conv3(x, halo, w, b):
    x1 = _shift_down(x, halo, 1)
    x2 = _shift_down(x, halo, 2)
    return b + w[0:1, :] * x + w[1:2, :] * x1 + w[2:3, :] * x2, x1, x2


def _conv_gate_fwd(hc, conv_w, conv_b, seq, comm=None):
    _, nb, n, c = hc.shape
    tiles_per_seq = seq // TM

    def body(x_ref, halo_ref, w_ref, b_ref, o_ref, cv_ref):
        i = pl.program_id(1)
        start = (i % tiles_per_seq) == 0
        cv = []
        for h in range(2):
            halo = jnp.where(start, 0.0, halo_ref[h].astype(F32))
            cv.append(_conv3(x_ref[h].astype(F32), halo, w_ref[h], b_ref[h])[0])
            cv_ref[h] = cv[h].astype(BF16)
        o_ref[...] = (cv[1] * jax.nn.sigmoid(cv[1]) * cv[0]).astype(BF16)

    both = pl.BlockSpec((2, None, TM, c), lambda d, i: (0, d, i, 0))
    return _call(
        body, "conv_gate_fwd", (nb, n // TM),
        [both,
         pl.BlockSpec((2, None, HALO, c), lambda d, i: (0, d, jnp.maximum(i * (TM // HALO) - 1, 0), 0)),
         pl.BlockSpec((2, None, 3, c), lambda d, i: (0, d, 0, 0)),
         pl.BlockSpec((2, None, 1, c), lambda d, i: (0, d, 0, 0))],
        [pl.BlockSpec((None, TM, c), lambda d, i: (d, i, 0)), both],
        [_sds((nb, n, c), BF16), _sds((2, nb, n, c), BF16)], comm=comm)(hc, hc, conv_w, conv_b)


def _conv_gate_bwd(hc, cv, dact, conv_w, seq, comm=None):
    _, nb, n, c = hc.shape
    tiles_per_seq = seq // TM
    last_halo = n // HALO - 1
    ext = TM + HALO

    def body(x_ref, cv_ref, cvn_ref, da_ref, dan_ref, w_ref, dx_ref, dw_ref):
        i = pl.program_id(1)
        end = (i % tiles_per_seq) == tiles_per_seq - 1
        row = lax.broadcasted_iota(jnp.int32, (ext, c), 0)
        cv = [jnp.concatenate([cv_ref[h], cvn_ref[h]], axis=0).astype(F32) for h in range(2)]
        da = jnp.concatenate([da_ref[...], dan_ref[...]], axis=0).astype(F32)
        da = jnp.where(jnp.logical_and(end, row >= TM), 0.0, da)
        sg = jax.nn.sigmoid(cv[1])
        silu = cv[1] * sg
        dcv = [da * silu, da * cv[0] * (sg + silu * (1.0 - sg))]
        for h in range(2):
            w = w_ref[h]
            g = dcv[h]
            g0 = g[:TM, :]
            g1 = pltpu.roll(g, ext - 1, 0)[:TM, :]
            g2 = pltpu.roll(g, ext - 2, 0)[:TM, :]
            dx_ref[h] = (w[0:1, :] * g0 + w[1:2, :] * g1 + w[2:3, :] * g2).astype(BF16)
            x = x_ref[h].astype(F32)
            parts = [jnp.sum(g0 * x, axis=0, keepdims=True),
                     jnp.sum(g1 * x, axis=0, keepdims=True),
                     jnp.sum(g2 * x, axis=0, keepdims=True),
                     jnp.sum(g0, axis=0, keepdims=True)]
            upd = jnp.concatenate(parts + [jnp.zeros((4, c), F32)], axis=0)

            @pl.when(i == 0)
            def _():
                dw_ref[h] = upd

            @pl.when(i > 0)
            def _():
                dw_ref[h] += upd

    nxt = lambda i: jnp.minimum((i + 1) * (TM // HALO), last_halo)
    return _call(
        body, "conv_gate_bwd", (nb, n // TM),
        [pl.BlockSpec((2, None, TM, c), lambda d, i: (0, d, i, 0)),
         pl.BlockSpec((2, None, TM, c), lambda d, i: (0, d, i, 0)),
         pl.BlockSpec((2, None, HALO, c), lambda d, i: (0, d, nxt(i), 0)),
         pl.BlockSpec((None, TM, c), lambda d, i: (d, i, 0)),
         pl.BlockSpec((None, HALO, c), lambda d, i: (d, nxt(i), 0)),
         pl.BlockSpec((2, None, 3, c), lambda d, i: (0, d, 0, 0))],
        [pl.BlockSpec((2, None, TM, c), lambda d, i: (0, d, i, 0)),
         pl.BlockSpec((2, None, 8, c), lambda d, i: (0, d, 0, 0))],
        [_sds((2, nb, n, c), BF16), _sds((2, nb, 8, c), F32)], comm=comm)(hc, cv, cv, dact, dact, conv_w)


def _t5_bucket_np(dist):
    exact = REL_BUCKETS // 2
    d = np.maximum(dist, 1).astype(np.float32)
    large = exact + (np.log(d / exact) / math.log(REL_MAX_DIST / exact) * (REL_BUCKETS - exact)).astype(np.int64)
    large = np.minimum(large, REL_BUCKETS - 1)
    return np.where(dist < exact, dist, large).astype(np.int32)


def _bucket_table(dil):
    steps = np.arange(BAND)[:, None] + BAND - np.arange(2 * BAND)[None, :]
    return _t5_bucket_np(np.maximum(steps, 0) * dil).reshape(1, BAND * 2 * BAND).astype(np.float32)


def _bias_expand(rel_t, dil):
    nk = BAND * 2 * BAND
    bucket = jnp.asarray(_bucket_table(dil))

    def body(r_ref, bk_ref, o_ref):
        ids = lax.broadcasted_iota(jnp.int32, (REL_BUCKETS, nk), 0).astype(F32)
        onehot = (ids == bk_ref[...]).astype(F32)
        o_ref[...] = jnp.dot(r_ref[...], onehot, precision=HIGHEST, preferred_element_type=F32)

    return _call(body, "bias_expand", (1,),
                 [pl.BlockSpec((N_HEADS, REL_BUCKETS), lambda i: (0, 0)), pl.BlockSpec((1, nk), lambda i: (0, 0))],
                 pl.BlockSpec((N_HEADS, nk), lambda i: (0, 0)), _sds((N_HEADS, nk), F32))(rel_t, bucket)


def _bias_reduce(dbias, dil):
    nk = BAND * 2 * BAND
    bucket = jnp.asarray(_bucket_table(dil))

    def body(g_ref, bk_ref, o_ref):
        ids = lax.broadcasted_iota(jnp.int32, (REL_BUCKETS, nk), 0).astype(F32)
        onehot = (ids == bk_ref[...]).astype(F32)
        o_ref[...] = lax.dot_general(g_ref[...], onehot, NT, precision=HIGHEST, preferred_element_type=F32)

    return _call(body, "bias_reduce", (1,),
                 [pl.BlockSpec((N_HEADS, nk), lambda i: (0, 0)), pl.BlockSpec((1, nk), lambda i: (0, 0))],
                 pl.BlockSpec((N_HEADS, REL_BUCKETS), lambda i: (0, 0)),
                 _sds((N_HEADS, REL_BUCKETS), F32))(dbias, bucket)


def _valid_mask(n):
    qi = lax.broadcasted_iota(jnp.int32, (BAND, 2 * BAND), 0)
    kj = lax.broadcasted_iota(jnp.int32, (BAND, 2 * BAND), 1)
    steps = qi + BAND - kj
    in_band = jnp.logical_and(steps >= 0, steps <= BAND)
    return jnp.logical_and(in_band, jnp.logical_or(n > 0, kj >= BAND))


ATTN_COLS = {1: 1024, 4: 512, 16: 128}


def _residue(dil, r):
    return slice(None) if dil == 1 else pl.ds(r, BAND, stride=dil)


def _stack_heads(x, first):
    return jnp.concatenate([jnp.where(first, x, 0.0), jnp.where(first, 0.0, x)], axis=0)


def _attn_fwd(q, k, v, bias, g, dil, bsz, seq, comm=None, merge=()):
    n = q.shape[0]
    rows = BAND * dil
    nch = seq // rows
    w = N_HEADS * HEAD_DIM
    wc = ATTN_COLS[dil]
    ncol = w // wc
    hpc = wc // HEAD_DIM
    nj = wc // 128
    slab = lambda k: pltpu.VMEM((k * nj, rows, 128), F32)

    def attend(q_ref, kc_ref, kp_ref, vc_ref, vp_ref, b_ref, others, o_ref, l_ref, qf, kf, vf, of, lf):
        ch = pl.program_id(2)
        valid = _valid_mask(ch)
        valid = jnp.concatenate([valid, valid], axis=0)
        lane = lax.broadcasted_iota(jnp.int32, (BAND, 128), 1)
        first = lane < HEAD_DIM
        for j in range(nj):
            sl = slice(128 * j, 128 * (j + 1))
            qf[j] = q_ref[:, sl].astype(F32)
            kf[j] = kp_ref[:, sl].astype(F32)
            kf[nj + j] = kc_ref[:, sl].astype(F32)
            vf[j] = vp_ref[:, sl].astype(F32)
            vf[nj + j] = vc_ref[:, sl].astype(F32)
        for r in range(dil):
            rr = _residue(dil, r)
            for j in range(nj):
                q2 = qf.at[j][rr, :]
                k2 = jnp.concatenate([kf.at[j][rr, :], kf.at[nj + j][rr, :]], axis=0).astype(BF16)
                v2 = jnp.concatenate([vf.at[j][rr, :], vf.at[nj + j][rr, :]], axis=0).astype(BF16)
                qs = _stack_heads(q2, first).astype(BF16)
                s = lax.dot_general(qs, k2, NT, preferred_element_type=F32) * (HEAD_DIM ** -0.5)
                s = jnp.where(valid, s + b_ref[2 * j:2 * j + 2].reshape(2 * BAND, 2 * BAND), NEG_BIG)
                m = jnp.max(s, axis=1, keepdims=True)
                p = jnp.exp(s - m)
                l = jnp.sum(p, axis=1, keepdims=True)
                pv = jnp.dot(p.astype(BF16), v2, preferred_element_type=F32) / l
                lse = jnp.broadcast_to(m + jnp.log(l), (2 * BAND, 128))
                of.at[j][rr, :] = jnp.where(first, pv[:BAND], pv[BAND:])
                lf.at[j][rr, :] = jnp.where(first, lse[:BAND], lse[BAND:])
        for j in range(nj):
            sl = slice(128 * j, 128 * (j + 1))
            if merge:
                o0, l0, o1, l1 = (r[:, sl] for r in others)
                lc = lf[j]
                m = jnp.maximum(jnp.maximum(l0, l1), lc)
                wa, wb, wc_ = jnp.exp(l0 - m), jnp.exp(l1 - m), jnp.exp(lc - m)
                tot = wa + wb + wc_
                o_ref[:, sl] = (wa * o0 + wb * o1 + wc_ * of[j]) / tot
                l_ref[:, sl] = m + jnp.log(tot)
            else:
                o_ref[:, sl] = of[j]
                l_ref[:, sl] = lf[j]

    def body(q_ref, kc_ref, kp_ref, vc_ref, vp_ref, b_ref, *rest):
        others, rest = rest[:len(merge)], rest[len(merge):]
        attend(q_ref, kc_ref, kp_ref, vc_ref, vp_ref, b_ref, others, *rest)

    base = g * ncol
    cur = lambda c, b, ch: (b * nch + ch, base + c)
    prev = lambda c, b, ch: (b * nch + jnp.maximum(ch - 1, 0), base + c)
    blk = lambda im: pl.BlockSpec((rows, wc), im)
    out_blk = pl.BlockSpec((rows, wc), lambda c, b, ch: (b * nch + ch, c))
    return _call(
        body, "attn_fwd_d%d" % dil, (ncol, bsz, nch),
        [blk(cur), blk(cur), blk(prev), blk(cur), blk(prev),
         pl.BlockSpec((hpc, BAND, 2 * BAND), lambda c, b, ch: (c, 0, 0))] + [out_blk] * len(merge),
        [out_blk, out_blk],
        [_sds((n, w), F32)] * 2,
        [slab(1), slab(2), slab(2), slab(1), slab(1)], comm=comm,
    )(q, k, k, v, v, bias, *merge)


def _attn_bwd(q, k, v, do, o, lse, bias, g, dil, bsz, seq, comm=None, into=()):
    n = q.shape[0]
    rows = BAND * dil
    nch = seq // rows
    w = N_HEADS * HEAD_DIM
    wc = ATTN_COLS[dil]
    ncol = w // wc
    hpc = wc // HEAD_DIM
    nj = wc // 128
    slab = lambda k: pltpu.VMEM((k * nj, rows, 128), F32)
    scale = HEAD_DIM ** -0.5

    def body(q_ref, kc_ref, kp_ref, vc_ref, vp_ref, do_ref, o_ref, l_ref, b_ref,
             dq_ref, dk_ref, dv_ref, db_ref, qf, kf, vf, dof, of, lf, dqf, dkf, dvf):
        b = pl.program_id(1)
        ch = pl.program_id(2)
        cur = ch % 2
        prv = 1 - cur

        @pl.when(jnp.logical_and(b == 0, ch == 0))
        def _():
            db_ref[...] = jnp.zeros_like(db_ref)

        @pl.when(ch == 0)
        def _():
            for j in range(nj):
                dkf[nj + j] = jnp.zeros((rows, 128), F32)
                dvf[nj + j] = jnp.zeros((rows, 128), F32)

        @pl.when(ch < nch)
        def _():
            valid = _valid_mask(ch)
            valid = jnp.concatenate([valid, valid], axis=0)
            first = lax.broadcasted_iota(jnp.int32, (BAND, 128), 1) < HEAD_DIM
            for j in range(nj):
                sl = slice(128 * j, 128 * (j + 1))
                qf[j] = q_ref[:, sl].astype(F32)
                kf[j] = kp_ref[:, sl].astype(F32)
                kf[nj + j] = kc_ref[:, sl].astype(F32)
                vf[j] = vp_ref[:, sl].astype(F32)
                vf[nj + j] = vc_ref[:, sl].astype(F32)
                dof[j] = do_ref[:, sl]
                of[j] = o_ref[:, sl]
                lf[j] = l_ref[:, sl]
            for r in range(dil):
                rr = _residue(dil, r)
                for j in range(nj):
                    k2 = jnp.concatenate([kf.at[j][rr, :], kf.at[nj + j][rr, :]], axis=0).astype(BF16)
                    v2 = jnp.concatenate([vf.at[j][rr, :], vf.at[nj + j][rr, :]], axis=0).astype(BF16)
                    do2 = dof.at[j][rr, :]
                    lse2 = lf.at[j][rr, :]
                    qs = _stack_heads(qf.at[j][rr, :], first).astype(BF16)
                    dos = _stack_heads(do2, first)
                    delta = jnp.sum(dos * jnp.concatenate([of.at[j][rr, :]] * 2, axis=0), axis=1, keepdims=True)
                    dos = dos.astype(BF16)
                    lse_col = jnp.concatenate([lse2[:, 0:1], lse2[:, HEAD_DIM:HEAD_DIM + 1]], axis=0)
                    s = lax.dot_general(qs, k2, NT, preferred_element_type=F32) * scale
                    s = jnp.where(valid, s + b_ref[2 * j:2 * j + 2].reshape(2 * BAND, 2 * BAND), NEG_BIG)
                    p = jnp.exp(s - lse_col)
                    dp = lax.dot_general(dos, v2, NT, preferred_element_type=F32)
                    ds = p * (dp - delta)
                    db_ref[2 * j:2 * j + 2] += ds.reshape(2, BAND, 2 * BAND)
                    dsb = ds.astype(BF16)
                    dq2 = jnp.dot(dsb, k2, preferred_element_type=F32) * scale
                    dk2 = lax.dot_general(dsb, qs, TN, preferred_element_type=F32) * scale
                    dv2 = lax.dot_general(p.astype(BF16), dos, TN, preferred_element_type=F32)
                    dqf.at[j][rr, :] = jnp.where(first, dq2[:BAND], dq2[BAND:])
                    dkf.at[prv * nj + j][rr, :] += dk2[:BAND, :]
                    dvf.at[prv * nj + j][rr, :] += dv2[:BAND, :]
                    dkf.at[cur * nj + j][rr, :] = dk2[BAND:, :]
                    dvf.at[cur * nj + j][rr, :] = dv2[BAND:, :]
            for j in range(nj):
                dq_ref[:, 128 * j:128 * (j + 1)] = dqf[j].astype(BF16)

        for j in range(nj):
            sl = slice(128 * j, 128 * (j + 1))
            dk_ref[:, sl] = dkf[prv * nj + j].astype(BF16)
            dv_ref[:, sl] = dvf[prv * nj + j].astype(BF16)

    last = nch - 1
    base = g * ncol
    cur3 = lambda c, b, ch: (b * nch + jnp.minimum(ch, last), base + c)
    prev3 = lambda c, b, ch: (b * nch + jnp.maximum(jnp.minimum(ch, last) - 1, 0), base + c)
    cur1 = lambda c, b, ch: (b * nch + jnp.minimum(ch, last), c)
    lag3 = lambda c, b, ch: (b * nch + jnp.maximum(ch - 1, 0), base + c)
    blk = lambda im: pl.BlockSpec((rows, wc), im)
    bias_blk = pl.BlockSpec((hpc, BAND, 2 * BAND), lambda c, b, ch: (c, 0, 0))
    out = _sds(q.shape, BF16)
    return _call(
        body, "attn_bwd_d%d" % dil, (ncol, bsz, nch + 1),
        [blk(cur3), blk(cur3), blk(prev3), blk(cur3), blk(prev3), blk(cur1), blk(cur1), blk(cur1), bias_blk],
        [blk(cur3), blk(lag3), blk(lag3), bias_blk],
        [out, out, out, _sds((N_HEADS, BAND, 2 * BAND), F32)],
        [slab(1), slab(2), slab(2), slab(1), slab(1), slab(1), slab(1), slab(2), slab(2)], comm=comm, fill=into,
    )(q, k, k, v, v, do, o, lse, bias)


def _adamw(name, parts, w, m, v):
    nl, r, c = w.shape
    nparts = parts[0].shape[0]
    tr = r
    for cand in (256, 352, 128):
        if r % cand == 0 and r > cand:
            tr = cand
            break
    c1 = 1.0 - ADAM_B1 ** ADAM_STEP
    c2 = 1.0 - ADAM_B2 ** ADAM_STEP

    def body(*refs):
        p_refs = refs[:nl]
        w_ref, m_ref, v_ref, g_ref, d_ref, nm_ref, nv_ref = refs[nl:]
        layer = pl.program_id(0)
        for l in range(nl):
            @pl.when(layer == l)
            def _():
                g = p_refs[l][0].astype(F32)
                for dev in range(1, nparts):
                    g = g + p_refs[l][dev].astype(F32)
                nm = ADAM_B1 * m_ref[...] + (1.0 - ADAM_B1) * g
                nv = ADAM_B2 * v_ref[...] + (1.0 - ADAM_B2) * (g * g)
                m_hat = nm / c1
                v_hat = nv / c2
                g_ref[...] = g
                d_ref[...] = -ADAM_LR * (m_hat / (jnp.sqrt(v_hat) + ADAM_EPS) + ADAM_WD * w_ref[...])
                nm_ref[...] = nm
                nv_ref[...] = nv

    def part_spec(l):
        return pl.BlockSpec((nparts, tr, c), lambda layer, i: (0, jnp.where(layer == l, i, 0), 0))

    row = pl.BlockSpec((None, tr, c), lambda layer, i: (layer, i, 0))
    return _call(body, name, (nl, r // tr), [part_spec(l) for l in range(nl)] + [row, row, row],
                 [row] * 4, [_sds((nl, r, c), F32)] * 4)(*parts, w, m, v)


def _exchange(name, srcs, out_shapes, items):
    n_in = len(srcs)
    n_out = len(out_shapes)
    n_items = len(items)

    def body(*refs):
        ins = refs[:n_in]
        outs = refs[n_in:n_in + n_out]
        send_sems, recv_sems, local_sems = refs[n_in + n_out:]
        x, y, c = lax.axis_index("x"), lax.axis_index("y"), lax.axis_index("c")
        me = 4 * x + 2 * y + c

        def pick(ref, sel, peer):
            idx = tuple(peer if s == "peer" else me if s == "me" else s for s in sel)
            return ref.at[idx] if idx else ref

        copies = []
        for it, (si, ssel, oi, osel) in enumerate(items):
            local = pltpu.make_async_copy(pick(ins[si], ssel, me), pick(outs[oi], osel, me), local_sems.at[it])
            local.start()
            copies.append(local)
            for j in range(1, N_DEV):
                px = 1 - x if j & 4 else x
                py = 1 - y if j & 2 else y
                pc = 1 - c if j & 1 else c
                peer = 4 * px + 2 * py + pc
                sem = it * (N_DEV - 1) + j - 1
                cp = pltpu.make_async_remote_copy(
                    src_ref=pick(ins[si], ssel, peer), dst_ref=pick(outs[oi], osel, peer),
                    send_sem=send_sems.at[sem], recv_sem=recv_sems.at[sem],
                    device_id=(px, py, pc), device_id_type=pl.DeviceIdType.MESH)
                cp.start()
                copies.append(cp)
        for cp in copies:
            cp.wait()

    any_spec = pl.BlockSpec(memory_space=pl.ANY)
    return pl.pallas_call(
        body, name=name, in_specs=[any_spec] * n_in, out_specs=[any_spec] * n_out, out_shape=list(out_shapes),
        scratch_shapes=[pltpu.SemaphoreType.DMA((n_items * (N_DEV - 1),)),
                        pltpu.SemaphoreType.DMA((n_items * (N_DEV - 1),)),
                        pltpu.SemaphoreType.DMA((n_items,))],
        compiler_params=pltpu.CompilerParams(has_side_effects=True),
    )(*srcs)


def _sel(ref, sel, slot=None):
    idx = tuple(slot if s == "slot" else s for s in sel)
    return ref.at[idx] if idx else ref


def _gather_plan(srcs, out_shapes, items):
    per = N_DEV - 1
    n_items = len(items)

    def phases(ins, outs, send_sems, recv_sems, local_sems, total):
        x, y, c = lax.axis_index("x"), lax.axis_index("y"), lax.axis_index("c")
        sibling = (x, y, 1 - c)
        chips = [(1 - x, y), (x, 1 - y), (1 - x, 1 - y)]
        slot_of = lambda px, py, pc: 4 * px + 2 * py + pc

        def copy(it, k, block, to, src=None):
            _, _, oi, osel = items[it]
            dst = _sel(outs[oi], osel, slot_of(*block))
            return pltpu.make_async_remote_copy(
                src_ref=dst if src is None else src, dst_ref=dst,
                send_sem=send_sems.at[it * per + k], recv_sem=recv_sems.at[it * per + k],
                device_id=to, device_id_type=pl.DeviceIdType.MESH)

        def own_copies(it):
            si, ssel, oi, osel = items[it]
            src = _sel(ins[si], ssel)
            mine = pltpu.make_async_copy(src, _sel(outs[oi], osel, slot_of(x, y, c)), local_sems.at[it])
            first = [copy(it, 0, (x, y, c), sibling, src=src)]
            first += [copy(it, 1 + j, (x, y, c), (*chip, c), src=src) for j, chip in enumerate(chips)]
            return mine, first

        def start():
            for it in range(n_items):
                mine, first = own_copies(it)
                mine.start()
                for cp in first:
                    cp.start()

        def forward(it):
            def go():
                for j, chip in enumerate(chips):
                    copy(it, 1 + j, (*chip, c), (x, y, c)).wait_recv()
                    copy(it, 4 + j, (*chip, c), sibling).start()
            return go

        def finish():
            for it in range(n_items):
                copy(it, 0, sibling, (x, y, c)).wait_recv()
                for j, chip in enumerate(chips):
                    copy(it, 4 + j, (*chip, 1 - c), (x, y, c)).wait_recv()
            for it in range(n_items):
                mine, first = own_copies(it)
                for cp in first:
                    cp.wait_send()
                for j, chip in enumerate(chips):
                    copy(it, 4 + j, (*chip, c), sibling).wait_send()
                mine.wait()

        lo = max(3 * total // 4, 1)
        acts = [(0, start)]
        for it in range(n_items):
            acts.append((min(lo + it * (total - 1 - lo) // n_items, total - 2), forward(it)))
        acts.append((total - 1, finish))
        return acts

    return _Plan(srcs, out_shapes, n_items * per, n_items, phases)


def _scatter_plan(srcs, out_shapes, items):
    per = N_DEV - 1
    n_items = len(items)

    def phases(ins, outs, send_sems, recv_sems, local_sems, total):
        x, y, c = lax.axis_index("x"), lax.axis_index("y"), lax.axis_index("c")
        me = 4 * x + 2 * y + c

        def copies():
            out = []
            for it, (si, oi, osel) in enumerate(items):
                dst = _sel(outs[oi], osel, me)
                out.append(pltpu.make_async_copy(ins[si].at[me], dst, local_sems.at[it]))
                for j in range(1, N_DEV):
                    px = 1 - x if j & 4 else x
                    py = 1 - y if j & 2 else y
                    pc = 1 - c if j & 1 else c
                    out.append(pltpu.make_async_remote_copy(
                        src_ref=ins[si].at[4 * px + 2 * py + pc], dst_ref=dst,
                        send_sem=send_sems.at[it * per + j - 1], recv_sem=recv_sems.at[it * per + j - 1],
                        device_id=(px, py, pc), device_id_type=pl.DeviceIdType.MESH))
            return out

        def start():
            for cp in copies():
                cp.start()

        def finish():
            for cp in copies():
                cp.wait()

        return [(0, start), (total - 1, finish)]

    return _Plan(srcs, out_shapes, n_items * per, n_items, phases)


def _sum_slots(parts):
    _, r, c = parts.shape

    def body(p_ref, o_ref):
        acc = p_ref[0]
        for dev in range(1, N_DEV):
            acc = acc + p_ref[dev]
        o_ref[...] = acc

    return _call(body, "sum_slots", (1,), [pl.BlockSpec((N_DEV, r, c), lambda i: (0, 0, 0))],
                 pl.BlockSpec((r, c), lambda i: (0, 0)), _sds((r, c), F32))(parts)


def _pack(arrays):
    rows = []
    for a in arrays:
        flat = a.reshape(-1).astype(F32)
        pad = (-flat.shape[0]) % 1024
        if pad:
            flat = jnp.concatenate([flat, jnp.zeros((pad,), F32)])
        rows.append(flat.reshape(-1, 128))
    return jnp.concatenate(rows, axis=0)


def _unpack(buf, shapes):
    out = []
    r0 = 0
    for shp in shapes:
        size = int(np.prod(shp))
        nrows = -(-size // 1024) * 8
        out.append(buf[r0:r0 + nrows].reshape(-1)[:size].reshape(shp))
        r0 += nrows
    return out


def _block_diag_chunks(val):
    eye = jnp.eye(8, dtype=val.dtype)
    return jnp.einsum("cjhp,jk->cjhkp", val, eye).reshape(SSM_CHUNKS, CHUNK_CH, CHUNK_ST)


def _diag_blocks(dense):
    d5 = dense.reshape(SSM_CHUNKS, 8, SSM_GROUP, 8, SSM_STATE)
    return jnp.stack([d5[:, j, :, j, :] for j in range(8)], axis=1)


def _carried(res, comm):
    return res if comm is not None else (res, None)


def _ffn_fwd(tag, hin, hin_b, w_up, conv_w, conv_b, w_down4, gain, bias, seq, comm_up=None, comm_conv=None,
             comm_down=None):
    hc, got_up = _carried(_mm_nn("ffn_up" + tag, hin_b, w_up, FFN_HIDDEN, blocked_out=True, comm=comm_up), comm_up)
    nb, n, c = hc.shape
    hc = hc.reshape(2, nb // 2, n, c)
    (act, cv), got_conv = _carried(_conv_gate_fwd(hc, conv_w, conv_b, seq, comm=comm_conv), comm_conv)
    (r, hout, hout_b), got_down = _carried(
        _mm_nn_red("ffn_down" + tag, act, w_down4, F32, comm=comm_down, residual=hin, norm=(gain, bias)), comm_down)
    return hout, hout_b, (hin_b, hc, cv, act, r), (got_up, got_conv, got_down)


def _ffn_bwd(tag, saved, dh, w_up, conv_w, conv_b, w_down4, gain, seq, comm_conv=None, comm_dw=None,
             comm_dx=None):
    hin_b, hc, cv, act, r = saved
    dr, dr_b, dgain, dbias = _ln_bwd("ffn_norm_bwd" + tag, r, dh, gain)
    d_wdown = _mm_tn("ffn_down_dw" + tag, act, dr_b, act.shape[0], BF16, a_blocked=True, g_mode="shared")
    dact = _mm_nt_out("ffn_down_dx" + tag, dr_b, w_down4, FFN_HIDDEN)
    (dhc, dconv), got_conv = _carried(_conv_gate_bwd(hc, cv, dact, conv_w, seq, comm=comm_conv), comm_conv)
    dhc8 = dhc.reshape(2 * dhc.shape[1], dhc.shape[2], dhc.shape[3])
    d_wup, got_dw = _carried(_mm_tn("ffn_up_dw" + tag, dhc8, hin_b, dhc8.shape[0], BF16, a_blocked=True,
                                    g_mode="shared", comm=comm_dw), comm_dw)
    dhin, got_dx = _carried(_mm_nt_red("ffn_up_dx" + tag, dhc8, w_up, F32, g_blocked=True, add=dr,
                                       add_scale=DN_ALPHA, comm=comm_dx), comm_dx)
    return dhin, d_wup, d_wdown, dconv, dgain, dbias, (got_conv, got_dw, got_dx)


def _local_step(x, target, p, wfull, bsz, seq, ex=None):
    n = bsz * seq
    xf = x.reshape(n, D_MODEL)
    tf = target.reshape(n, D_MODEL)

    lr = p["s5_lam_re"].reshape(1, N_STATE)
    li = p["s5_lam_im"].reshape(1, N_STATE)
    ldt = p["s5_log_dt"].reshape(1, D_MODEL // SSM_GROUP)
    br_t = p["s5_b_re"].reshape(N_STATE, SSM_GROUP).T
    bi_t = p["s5_b_im"].reshape(N_STATE, SSM_GROUP).T
    ab_r, ab_i, bb_r, bb_i, tab = _s5_prep(lr, li, ldt, br_t, bi_t)

    def bb_chunks(bb):
        return _block_diag_chunks(bb.reshape(SSM_GROUP, SSM_CHUNKS, 8, SSM_STATE).transpose(1, 2, 0, 3))

    def c_chunks(cc):
        return _block_diag_chunks(cc.reshape(SSM_CHUNKS, 8, SSM_GROUP, SSM_STATE))

    bbr_c, bbi_c = bb_chunks(bb_r).astype(BF16), bb_chunks(bb_i).astype(BF16)
    cr_c, ci_c = c_chunks(p["s5_c_re"]).astype(BF16), c_chunks(p["s5_c_im"]).astype(BF16)
    dvec = p["s5_d"].reshape(1, D_MODEL)

    if ex is None:
        y, h_r, h_i = _s5_scan_fwd(xf, bbr_c, bbi_c, cr_c, ci_c, dvec, tab, seq)
    else:
        (y, h_r, h_i), gathered = _s5_scan_fwd(xf, bbr_c, bbi_c, cr_c, ci_c, dvec, tab, seq,
                                               comm=ex.gather_first())
        p, wfull = ex.take_first(gathered, p)
    ln_g = p["ln_gain"].reshape(4, 1, D_MODEL)
    ln_b = p["ln_bias"].reshape(4, 1, D_MODEL)
    y2, z, gg = _glu_fwd(y, wfull["s5_w_glu"], p["s5_b_glu"].reshape(1, D_MODEL))
    r1, h1, h1b = _mm_nn("s5_out", gg, wfull["s5_w_out"][None], F32, residual=xf, norm=(ln_g[0], ln_b[0]))
    plans = [ex.gather_attn(k) if ex else None for k in ("kv0", "q", "kv1")]
    h2, h2b, ffn0_saved, got = _ffn_fwd(
        "0", h1, h1b, wfull["ffn_w_up"][0], p["conv_w"][0], p["conv_b"][0], wfull["ffn_w_down"][0],
        ln_g[1], ln_b[1], seq, comm_up=plans[0], comm_conv=plans[1], comm_down=plans[2])
    if ex is not None:
        wfull = ex.take_attn({"kv0": got[0][0], "q": got[1][0], "kv1": got[2][0]}, wfull)

    plan = ex.gather_layer1("up", 0) if ex else None
    kmat, got = _carried(_mm_nn("attn_k", h2b, wfull["attn_w_kv"], BF16, blocks=(0, 4), comm=plan), plan)
    halves = {"up": [got[0]], "down": []} if ex else None
    vmat = _mm_nn("attn_v", h2b, wfull["attn_w_kv"], BF16, blocks=(4, 4))
    plan = ex.gather_attn("out") if ex else None
    qmat, got = _carried(_mm_nn("attn_q", h2b, wfull["attn_w_q"], BF16, comm=plan), plan)
    if ex is not None:
        wfull = ex.take_attn({"out": got[0]}, wfull)
    biases, outs, lses = [], [], []
    for g, dil in enumerate(DILATIONS):
        rel_t = p["rel_bias"][:, g * N_HEADS:(g + 1) * N_HEADS].T
        biases.append(_bias_expand(rel_t, dil).reshape(N_HEADS, BAND, 2 * BAND))
        which, half = {1: ("down", 0), 4: ("down", 1), 16: ("up", 1)}[dil]
        plan = ex.gather_layer1(which, half) if ex else None
        merge = (outs[0], lses[0], outs[1], lses[1]) if dil == DILATIONS[-1] else ()
        (o_g, l_g), got = _carried(_attn_fwd(qmat, kmat, vmat, biases[g], g, dil, bsz, seq, comm=plan,
                                             merge=merge), plan)
        if plan is not None:
            halves[which].append(got[0])
        outs.append(o_g)
        lses.append(l_g)
    if ex is not None:
        wfull = ex.take_layer1(halves, wfull)
    o, lse = outs[-1], lses[-1]
    r3, h3, h3b = _mm_nn("attn_out", o, wfull["attn_w_out"][None], F32, residual=h2, norm=(ln_g[2], ln_b[2]))
    h4, _, ffn1_saved, _ = _ffn_fwd("1", h3, h3b, wfull["ffn_w_up"][1], p["conv_w"][1], p["conv_b"][1],
                                    wfull["ffn_w_down"][1], ln_g[3], ln_b[3], seq)
    loss_tile, dh4 = _loss_head(h4, tf)

    grads = {}
    dh3, grads["ffn_w_up1"], grads["ffn_w_down1"], dconv1, dg3, db3, _ = _ffn_bwd(
        "1", ffn1_saved, dh4, wfull["ffn_w_up"][1], p["conv_w"][1], p["conv_b"][1], wfull["ffn_w_down"][1],
        ln_g[3], seq)
    dr3, dr3b, dg2, db2 = _ln_bwd("attn_norm_bwd", r3, dh3, ln_g[2])
    grads["attn_w_out"] = _mm_tn("attn_out_dw", o, dr3b, 1, BF16, g_mode="shared")
    do = _mm_nt_red("attn_out_dx", dr3b, wfull["attn_w_out"][None], F32, g_blocked=False)
    dqkv, drel = (), []
    for g, dil in enumerate(DILATIONS):
        plan = ex.scatter(grads, ("ffn_w_up1", "ffn_w_down1")) if ex and dil == DILATIONS[-1] else None
        res, got = _carried(_attn_bwd(qmat, kmat, vmat, do, o, lse, biases[g], g, dil, bsz, seq, comm=plan,
                                      into=dqkv), plan)
        if plan is not None:
            ex.keep(plan, got)
        dqkv, db_g = tuple(res[:3]), res[3]
        drel.append(_bias_reduce(db_g.reshape(N_HEADS, BAND * 2 * BAND), dil).T)
    dq, dk, dv = dqkv
    grads["rel_bias"] = jnp.concatenate(drel, axis=1)
    grads["attn_w_q"] = _mm_tn("attn_q_dw", h2b, dq, N_DEV, BF16)
    grads["attn_w_kv"] = jnp.concatenate([_mm_tn("attn_k_dw", h2b, dk, 4, BF16),
                                          _mm_tn("attn_v_dw", h2b, dv, 4, BF16)], axis=0)
    dh2 = _mm_nt_red("attn_q_dx", dq, wfull["attn_w_q"], F32, g_blocked=False, add=dr3, add_scale=DN_ALPHA)
    dh2 = _mm_nt_red("attn_k_dx", dk, wfull["attn_w_kv"], F32, g_blocked=False, add=dh2, blocks=(0, 4))
    dh2 = _mm_nt_red("attn_v_dx", dv, wfull["attn_w_kv"], F32, g_blocked=False, add=dh2, blocks=(4, 4))

    plans = [ex.scatter(grads, names) if ex else None
             for names in (("attn_w_kv#0", "attn_w_out"), ("attn_w_kv#1",), ("attn_w_q",))]
    dh1, grads["ffn_w_up0"], grads["ffn_w_down0"], dconv0, dg1, db1, got = _ffn_bwd(
        "0", ffn0_saved, dh2, wfull["ffn_w_up"][0], p["conv_w"][0], p["conv_b"][0], wfull["ffn_w_down"][0],
        ln_g[1], seq, comm_conv=plans[0], comm_dw=plans[1], comm_dx=plans[2])
    if ex is not None:
        for plan, outs in zip(plans, got):
            ex.keep(plan, outs)
    dr1, dr1b, dg0, db0 = _ln_bwd("s5_norm_bwd", r1, dh1, ln_g[0])
    grads["s5_w_out"] = _mm_tn("s5_out_dw", gg, dr1b, 1, BF16, g_mode="shared")
    dgg = _mm_nt_red("s5_out_dx", dr1b, wfull["s5_w_out"][None], F32, g_blocked=False)
    plan = ex.scatter(grads, ("s5_w_out",)) if ex else None
    (dz, d_bglu, dy2_direct), got = _carried(_glu_bwd_gate(y, z, dgg, comm=plan), plan)
    if ex is not None:
        ex.keep(plan, got)
    grads["s5_w_glu"] = _mm_tn("s5_glu_dw", y2, dz, 1, BF16, g_mode="shared")
    plan = ex.scatter(grads, ("s5_w_glu",)) if ex else None
    dy, got = _carried(_mm_nt_red("s5_glu_dx", dz, wfull["s5_w_glu"][None], F32, g_blocked=False, add=dy2_direct,
                                  post=(_gelu_bwd, y), comm=plan), plan)
    if ex is not None:
        ex.keep(plan, got)
    plan_last = ex.scatter(grads, ("ffn_w_up0", "ffn_w_down0")) if ex else None
    res = _s5_scan_bwd(xf, dy, dr1, h_r, h_i, bbr_c, bbi_c, cr_c, ci_c, dvec, tab, seq, comm=plan_last)
    if ex is not None:
        res, got = res
        ex.keep(plan_last, got)
    dx, dcr, dci, dbr, dbi, dar, dai, dd = res

    def bb_from_chunks(dense):
        return _diag_blocks(dense).transpose(2, 0, 1, 3).reshape(SSM_GROUP, N_STATE)

    d_lr, d_li, d_ldt, d_br, d_bi = _s5_prep_bwd(lr, li, ldt, br_t, bi_t, dar, dai,
                                                 bb_from_chunks(dbr), bb_from_chunks(dbi))
    grads["s5_lam_re"] = d_lr.reshape(p["s5_lam_re"].shape)
    grads["s5_lam_im"] = d_li.reshape(p["s5_lam_im"].shape)
    grads["s5_log_dt"] = d_ldt.reshape(p["s5_log_dt"].shape)
    grads["s5_b_re"] = d_br.T.reshape(p["s5_b_re"].shape)
    grads["s5_b_im"] = d_bi.T.reshape(p["s5_b_im"].shape)
    grads["s5_c_re"] = _diag_blocks(dcr).reshape(p["s5_c_re"].shape)
    grads["s5_c_im"] = -_diag_blocks(dci).reshape(p["s5_c_im"].shape)
    grads["s5_d"] = dd.reshape(p["s5_d"].shape)
    grads["s5_b_glu"] = d_bglu.reshape(1, D_MODEL)
    dconv = jnp.stack([dconv0, dconv1]).reshape(2, N_DEV, 8, -1)
    grads["ffn_conv_w"] = dconv[:, :, 0:3, :].transpose(0, 2, 1, 3)
    grads["ffn_conv_b"] = dconv[:, :, 3, :]
    grads["ln_gain"] = jnp.stack([dg0, dg1, dg2, dg3]).reshape(2, 2, D_MODEL)
    grads["ln_bias"] = jnp.stack([db0, db1, db2, db3]).reshape(2, 2, D_MODEL)
    return loss_tile[0, 0], dx.reshape(x.shape), grads


SMALL_REPLICATED = ("s5_lam_re", "s5_lam_im", "s5_log_dt", "s5_b_re", "s5_b_im", "s5_c_re", "s5_c_im", "s5_d",
                    "rel_bias", "ffn_conv_b")
SMALL_SHARDED = ("s5_b_glu", "ffn_conv_w", "ln_gain", "ln_bias")
BIG = ("s5_w_glu", "s5_w_out", "attn_w_kv", "attn_w_q", "attn_w_out", "ffn_w_up", "ffn_w_down")
WEIGHTS = ("s5_lam_re", "s5_lam_im", "s5_log_dt", "s5_b_re", "s5_b_im", "s5_c_re", "s5_c_im", "s5_d", "s5_w_glu",
           "s5_b_glu", "s5_w_out", "attn_w_kv", "attn_w_q", "attn_w_out", "rel_bias", "ffn_w_up", "ffn_conv_w",
           "ffn_conv_b", "ffn_w_down", "ln_gain", "ln_bias")


class _Exchanges:
    def __init__(self, w):
        self.w = w
        self.parts = {}
        self.up = w["ffn_w_up"].astype(BF16)
        self.down = w["ffn_w_down"].astype(BF16)

    def gather_first(self):
        w = self.w
        srcs = [w["s5_w_glu"][0].astype(BF16), w["s5_w_out"][0].astype(BF16),
                _pack([w[k] for k in SMALL_SHARDED]), self.up, self.down]
        outs = [_sds((N_DEV,) + (s.shape[1:] if i in (3, 4) else s.shape), s.dtype) for i, s in enumerate(srcs)]
        items = [(i, (0,) if i in (3, 4) else (), i, ("slot",)) for i in range(len(srcs))]
        return _gather_plan(srcs, outs, items)

    def gather_attn(self, piece):
        w = self.w
        if piece in ("kv0", "kv1"):
            src, rows = w["attn_w_kv"].astype(BF16), w["attn_w_kv"].shape[0] // 2
            return _gather_plan([src], [_sds((N_DEV, rows, src.shape[1]), BF16)],
                                [(0, (pl.ds(int(piece[2]) * rows, rows),), 0, ("slot",))])
        src = (w["attn_w_q"] if piece == "q" else w["attn_w_out"])[0].astype(BF16)
        return _gather_plan([src], [_sds((N_DEV,) + src.shape, BF16)], [(0, (), 0, ("slot",))])

    def take_attn(self, pieces, wfull):
        wfull = dict(wfull)
        if "kv0" in pieces:
            wfull["attn_w_kv"] = jnp.concatenate([pieces["kv0"], pieces["kv1"]], axis=1)
            wfull["attn_w_q"] = pieces["q"]
        if "out" in pieces:
            wfull["attn_w_out"] = pieces["out"].reshape(D_MODEL, D_MODEL)
        return wfull

    def take_first(self, g, p):
        d = D_MODEL
        w = self.w
        wfull = {
            "s5_w_glu": g[0].reshape(d, d),
            "s5_w_out": g[1].reshape(d, d),
            "ffn_w_up": [g[3], None],
            "ffn_w_down": [g[4].reshape(4, -1, d), None],
        }
        smalls = [_unpack(g[2][dev], [w[k].shape for k in SMALL_SHARDED]) for dev in range(N_DEV)]
        c_ff = w["ffn_conv_w"].shape[2]
        conv_w = jnp.stack([s[1] for s in smalls], axis=2)
        p = dict(p)
        p.update(s5_b_glu=jnp.concatenate([s[0] for s in smalls], axis=1),
                 ln_gain=jnp.concatenate([s[2] for s in smalls], axis=2),
                 ln_bias=jnp.concatenate([s[3] for s in smalls], axis=2),
                 conv_w=conv_w.transpose(0, 2, 1, 3).reshape(2, 2, 4, 3, c_ff),
                 conv_b=w["ffn_conv_b"].reshape(2, 2, 4, 1, c_ff))
        return p, wfull

    def gather_layer1(self, which, half):
        src = self.up if which == "up" else self.down
        rows = src.shape[1] // 2
        return _gather_plan([src], [_sds((N_DEV, rows) + src.shape[2:], BF16)],
                            [(0, (1, pl.ds(half * rows, rows)), 0, ("slot",))])

    def take_layer1(self, halves, wfull):
        up = jnp.concatenate(halves["up"], axis=1)
        down = jnp.concatenate(halves["down"], axis=1)
        wfull = dict(wfull)
        wfull["ffn_w_up"] = [wfull["ffn_w_up"][0], up]
        wfull["ffn_w_down"] = [wfull["ffn_w_down"][0], down.reshape(4, -1, D_MODEL)]
        return wfull

    def scatter(self, grads, names):
        srcs = []
        for k in names:
            name, _, half = k.partition("#")
            g = grads[name].reshape(N_DEV, -1, grads[name].shape[-1])
            if half:
                rows = g.shape[1] // 2
                g = g[:, int(half) * rows:(int(half) + 1) * rows]
            srcs.append(g)
        plan = _scatter_plan(srcs, [_sds(s.shape, BF16) for s in srcs],
                             [(i, i, ("slot",)) for i in range(len(names))])
        plan.names = names
        return plan

    def keep(self, plan, outs):
        for k, o in zip(plan.names, outs):
            self.parts[k] = o


def kernel(x, s5_lam_re, s5_lam_im, s5_log_dt, s5_b_re, s5_b_im, s5_c_re, s5_c_im, s5_d, s5_w_glu, s5_b_glu, s5_w_out, attn_w_kv, attn_w_q, attn_w_out, rel_bias, ffn_w_up, ffn_conv_w, ffn_conv_b, ffn_w_down, ln_gain, ln_bias, loss_target, m_s5_lam_re, m_s5_lam_im, m_s5_log_dt, m_s5_b_re, m_s5_b_im, m_s5_c_re, m_s5_c_im, m_s5_d, m_s5_w_glu, m_s5_b_glu, m_s5_w_out, m_attn_w_kv, m_attn_w_q, m_attn_w_out, m_rel_bias, m_ffn_w_up, m_ffn_conv_w, m_ffn_conv_b, m_ffn_w_down, m_ln_gain, m_ln_bias, v_s5_lam_re, v_s5_lam_im, v_s5_log_dt, v_s5_b_re, v_s5_b_im, v_s5_c_re, v_s5_c_im, v_s5_d, v_s5_w_glu, v_s5_b_glu, v_s5_w_out, v_attn_w_kv, v_attn_w_q, v_attn_w_out, v_rel_bias, v_ffn_w_up, v_ffn_conv_w, v_ffn_conv_b, v_ffn_w_down, v_ln_gain, v_ln_bias):
    w = dict(s5_lam_re=s5_lam_re, s5_lam_im=s5_lam_im, s5_log_dt=s5_log_dt, s5_b_re=s5_b_re, s5_b_im=s5_b_im,
             s5_c_re=s5_c_re, s5_c_im=s5_c_im, s5_d=s5_d, s5_w_glu=s5_w_glu, s5_b_glu=s5_b_glu, s5_w_out=s5_w_out,
             attn_w_kv=attn_w_kv, attn_w_q=attn_w_q, attn_w_out=attn_w_out, rel_bias=rel_bias, ffn_w_up=ffn_w_up,
             ffn_conv_w=ffn_conv_w, ffn_conv_b=ffn_conv_b, ffn_w_down=ffn_w_down, ln_gain=ln_gain, ln_bias=ln_bias)
    mom = dict(s5_lam_re=m_s5_lam_re, s5_lam_im=m_s5_lam_im, s5_log_dt=m_s5_log_dt, s5_b_re=m_s5_b_re,
               s5_b_im=m_s5_b_im, s5_c_re=m_s5_c_re, s5_c_im=m_s5_c_im, s5_d=m_s5_d, s5_w_glu=m_s5_w_glu,
               s5_b_glu=m_s5_b_glu, s5_w_out=m_s5_w_out, attn_w_kv=m_attn_w_kv, attn_w_q=m_attn_w_q,
               attn_w_out=m_attn_w_out, rel_bias=m_rel_bias, ffn_w_up=m_ffn_w_up, ffn_conv_w=m_ffn_conv_w,
               ffn_conv_b=m_ffn_conv_b, ffn_w_down=m_ffn_w_down, ln_gain=m_ln_gain, ln_bias=m_ln_bias)
    var = dict(s5_lam_re=v_s5_lam_re, s5_lam_im=v_s5_lam_im, s5_log_dt=v_s5_log_dt, s5_b_re=v_s5_b_re,
               s5_b_im=v_s5_b_im, s5_c_re=v_s5_c_re, s5_c_im=v_s5_c_im, s5_d=v_s5_d, s5_w_glu=v_s5_w_glu,
               s5_b_glu=v_s5_b_glu, s5_w_out=v_s5_w_out, attn_w_kv=v_attn_w_kv, attn_w_q=v_attn_w_q,
               attn_w_out=v_attn_w_out, rel_bias=v_rel_bias, ffn_w_up=v_ffn_w_up, ffn_conv_w=v_ffn_conv_w,
               ffn_conv_b=v_ffn_conv_b, ffn_w_down=v_ffn_w_down, ln_gain=v_ln_gain, ln_bias=v_ln_bias)
    bsz, seq, _ = x.shape
    me = 4 * lax.axis_index("x") + 2 * lax.axis_index("y") + lax.axis_index("c")

    ex = _Exchanges(w)
    loss_part, grad_x, g = _local_step(x, loss_target, dict(w), None, bsz, seq, ex=ex)
    loss = lax.psum(loss_part, ("x", "y", "c"))

    d = D_MODEL
    results = {}
    for name in BIG:
        shard = w[name]
        if name == "ffn_w_up":
            parts = [ex.parts[name + "0"], ex.parts[name + "1"]]
            outs4 = _adamw("adamw_" + name, parts, *(jnp.swapaxes(a, 1, 2) for a in (shard, mom[name], var[name])))
            results[name] = tuple(jnp.swapaxes(o, 1, 2) for o in outs4)
            continue
        if name == "ffn_w_down":
            parts, r3 = [ex.parts[name + "0"], ex.parts[name + "1"]], shard.shape
        elif name == "attn_w_kv":
            parts = [jnp.concatenate([ex.parts[name + "#0"], ex.parts[name + "#1"]], axis=1)]
            r3 = (1, -1, shard.shape[-1])
        else:
            parts, r3 = [ex.parts[name]], (1, -1, shard.shape[-1])
        outs4 = _adamw("adamw_" + name, parts, shard.reshape(r3), mom[name].reshape(r3), var[name].reshape(r3))
        results[name] = tuple(o.reshape(shard.shape) for o in outs4)

    small_names = SMALL_REPLICATED + SMALL_SHARDED
    packed = _pack([g[k] for k in small_names])
    pad = (-packed.shape[0]) % (8 * N_DEV)
    if pad:
        packed = jnp.concatenate([packed, jnp.zeros((pad, 128), F32)], axis=0)
    piece = packed.shape[0] // N_DEV
    slots = _exchange("reduce_small", [packed.reshape(N_DEV, piece, 128)], [_sds((N_DEV, piece, 128), F32)],
                      [(0, ("peer",), 0, ("me",))])[0]
    summed = _exchange("gather_small", [_sum_slots(slots)], [_sds((N_DEV, piece, 128), F32)],
                       [(0, (), 0, ("me",))])[0]
    total = _unpack(summed.reshape(N_DEV * piece, 128), [g[k].shape for k in small_names])
    gsum = dict(zip(small_names, total))
    mine = {k: gsum[k] for k in SMALL_REPLICATED}
    mine["s5_b_glu"] = lax.dynamic_slice_in_dim(gsum["s5_b_glu"], me * (d // N_DEV), d // N_DEV, axis=1)
    mine["ffn_conv_w"] = lax.dynamic_index_in_dim(gsum["ffn_conv_w"], me, axis=2, keepdims=False)
    mine["ln_gain"] = lax.dynamic_slice_in_dim(gsum["ln_gain"], me * (d // N_DEV), d // N_DEV, axis=2)
    mine["ln_bias"] = lax.dynamic_slice_in_dim(gsum["ln_bias"], me * (d // N_DEV), d // N_DEV, axis=2)
    gp = _pack([mine[k] for k in small_names])
    outs4 = _adamw("adamw_small", [gp[None]], _pack([w[k] for k in small_names])[None],
                   _pack([mom[k] for k in small_names])[None], _pack([var[k] for k in small_names])[None])
    outs4 = [o[0] for o in outs4]
    shapes = [w[k].shape for k in small_names]
    unpacked = [_unpack(o, shapes) for o in outs4]
    for i, k in enumerate(small_names):
        results[k] = tuple(unpacked[j][i] for j in range(4))

    out = [loss, grad_x]
    for j in range(4):
        out.extend(results[k][j] for k in WEIGHTS)
    return tuple(out)
```

```python
import math

import numpy as np
import jax
import jax.numpy as jnp
from jax import lax
from jax.experimental import pallas as pl
from jax.experimental.pallas import tpu as pltpu

F32 = jnp.float32
BF16 = jnp.bfloat16
HIGHEST = lax.Precision.HIGHEST

N_DEV = 8
D_MODEL = 1024
SSM_GROUP = 16
SSM_STATE = 64
SSM_CHUNKS = 8
CHUNK_CH = D_MODEL // SSM_CHUNKS
CHUNK_ST = CHUNK_CH // SSM_GROUP * SSM_STATE
N_STATE = D_MODEL // SSM_GROUP * SSM_STATE
HEAD_DIM = 64
N_HEADS = 16
BAND = 128
DILATIONS = (1, 4, 16)
REL_BUCKETS = 32
REL_MAX_DIST = 2048
NEG_BIG = -1e30
DN_ALPHA = (2.0 * 2) ** 0.25
LN_EPS = 1e-5
ADAM_LR = 0.001
ADAM_B1 = 0.9
ADAM_B2 = 0.999
ADAM_EPS = 1e-08
ADAM_WD = 0.01
ADAM_STEP = 10

TM = 512
TM_MM = 2048
TM_MM_F32 = 1024
T_SCAN = 1024
VMEM_LIMIT = 56 * 1024 * 1024
FFN_HIDDEN = BF16

NN = (((1,), (0,)), ((), ()))
NT = (((1,), (1,)), ((), ()))
TN = (((0,), (0,)), ((), ()))


class _Plan:
    def __init__(self, srcs, out_shapes, n_sems, n_local, phases):
        self.srcs, self.out_shapes, self.n_sems, self.n_local, self.phases = srcs, out_shapes, n_sems, n_local, phases


def _call(body, name, grid, in_specs, out_specs, out_shape, scratch=(), comm=None, fill=()):
    params = dict(dimension_semantics=("arbitrary",) * len(grid), vmem_limit_bytes=VMEM_LIMIT)
    if comm is None and not fill:
        return pl.pallas_call(
            body, name=name, grid=grid, in_specs=in_specs, out_specs=out_specs, out_shape=out_shape,
            scratch_shapes=list(scratch), compiler_params=pltpu.CompilerParams(**params))
    single = not isinstance(out_specs, (list, tuple))
    o_specs = [out_specs] if single else list(out_specs)
    o_shape = [out_shape] if single else list(out_shape)
    n_in, n_out, n_scr, n_fill = len(in_specs), len(o_specs), len(scratch), len(fill)
    srcs = list(comm.srcs) if comm else []
    c_shapes = list(comm.out_shapes) if comm else []
    n_cin, n_cout = len(srcs), len(c_shapes)
    total = int(np.prod(grid))

    def wrapped(*refs):
        ins = refs[:n_in]
        refs = refs[n_in + n_fill:]
        cins, outs, couts = refs[:n_cin], refs[n_cin:n_cin + n_out], refs[n_cin + n_out:n_cin + n_out + n_cout]
        rest = refs[n_cin + n_out + n_cout:]
        scr, sems = rest[:n_scr], rest[n_scr:]
        if comm is None:
            body(*ins, *outs, *scr)
            return
        step = pl.program_id(0)
        for ax in range(1, len(grid)):
            step = step * grid[ax] + pl.program_id(ax)
        phases = comm.phases(cins, couts, *sems, total)
        for at, action in phases:
            if at == 0:
                pl.when(step == 0)(action)
        body(*ins, *outs, *scr)
        for at, action in phases:
            if at > 0:
                pl.when(step == at)(action)

    any_spec = pl.BlockSpec(memory_space=pl.ANY)
    sems = [] if comm is None else [pltpu.SemaphoreType.DMA((comm.n_sems,)), pltpu.SemaphoreType.DMA((comm.n_sems,)),
                                    pltpu.SemaphoreType.DMA((max(comm.n_local, 1),))]
    call = pl.pallas_call(
        wrapped, name=name, grid=grid, in_specs=list(in_specs) + [any_spec] * (n_fill + n_cin),
        out_specs=o_specs + [any_spec] * n_cout, out_shape=o_shape + c_shapes,
        scratch_shapes=list(scratch) + sems, input_output_aliases={n_in + j: j for j in range(n_fill)},
        compiler_params=pltpu.CompilerParams(has_side_effects=comm is not None, **params))

    def run(*args):
        res = call(*args, *fill, *srcs)
        own = res[0] if single else res[:n_out]
        return (own, res[n_out:]) if comm is not None else own

    return run


def _sds(shape, dtype):
    return jax.ShapeDtypeStruct(shape, dtype)


def _mm(name, a, b, *, dims, grid, a_spec, b_spec, o_spec, out_shape, nred=1, red_axis=0,
        add=None, add_spec=None, add_scale=1.0, comm=None, norm=None, inner=None, post=None):
    has_add = add is not None
    n_extra = (1 if has_add else 0) + (1 if post else 0) + (2 if norm else 0)
    n_out = 3 if norm else 1
    acc_shape = tuple(s for s in o_spec.block_shape if s is not None)

    def body(*refs):
        a_ref, b_ref = refs[:2]
        extra = refs[2:2 + n_extra]
        outs = refs[2 + n_extra:2 + n_extra + n_out]
        rest = refs[2 + n_extra + n_out:]
        o_ref = outs[0]
        if inner is None:
            prod = lax.dot_general(a_ref[...].astype(BF16), b_ref[...].astype(BF16), dims,
                                   preferred_element_type=F32)
        else:
            nb, a_mode = inner
            width = a_ref.shape[-1] // nb
            prod = None
            for d in range(nb):
                a_d = a_ref[d] if a_mode == "lead" else a_ref[:, d * width:(d + 1) * width]
                part = lax.dot_general(a_d.astype(BF16), b_ref[d].astype(BF16), dims, preferred_element_type=F32)
                prod = part if prod is None else prod + part

        def finish(val):
            if has_add:
                val = val + add_scale * extra[0][...].astype(F32)
            if post:
                val = post[0](val, extra[1 if has_add else 0][...])
            o_ref[...] = val.astype(o_ref.dtype)
            if norm:
                xhat, _ = _ln_stats(val)
                y = xhat * extra[-2][...] + extra[-1][...]
                outs[1][...] = y
                outs[2][...] = y.astype(BF16)

        if nred == 1:
            finish(prod)
        else:
            acc = rest[0]
            k = pl.program_id(red_axis)

            @pl.when(k == 0)
            def _():
                acc[...] = prod

            @pl.when(k > 0)
            def _():
                acc[...] += prod

            @pl.when(k == nred - 1)
            def _():
                finish(acc[...])

    in_specs = [a_spec, b_spec] + ([add_spec] if has_add else [])
    scratch = [pltpu.VMEM(acc_shape, F32)] if nred > 1 else []
    args = (a, b) + ((add,) if has_add else ())
    if post:
        in_specs.append(o_spec)
        args += (post[1],)
    if norm:
        vec = pl.BlockSpec((1, out_shape.shape[1]), lambda *_: (0, 0))
        in_specs += [vec, vec]
        args += tuple(norm)
        o_spec = [o_spec, o_spec, o_spec]
        out_shape = [out_shape, out_shape, _sds(out_shape.shape, BF16)]
    return _call(body, name, grid, in_specs, o_spec, out_shape, scratch, comm=comm)(*args)


def _mm_nn(name, a, b, out_dtype, blocked_out=False, comm=None, blocks=None, residual=None, norm=None):
    TM = TM_MM if out_dtype == BF16 else TM_MM_F32
    m, k = a.shape
    _, _, n = b.shape
    nb = b.shape[0] if blocks is None else blocks[1]
    first = 0 if blocks is None else blocks[0]
    if blocked_out:
        o_spec = pl.BlockSpec((None, TM, n), lambda d, i: (d, i, 0))
        out_shape = _sds((nb, m, n), out_dtype)
    else:
        o_spec = pl.BlockSpec((TM, n), lambda d, i: (i, d))
        out_shape = _sds((m, nb * n), out_dtype)
    return _mm(name, a, b, dims=NN, grid=(nb, m // TM),
               a_spec=pl.BlockSpec((TM, k), lambda d, i: (i, 0)),
               b_spec=pl.BlockSpec((None, k, n), lambda d, i: (d + first, 0, 0)),
               o_spec=o_spec, out_shape=out_shape, comm=comm, norm=norm,
               add=residual, add_spec=o_spec if residual is not None else None, add_scale=DN_ALPHA)


def _mm_nn_red(name, a, b, out_dtype, comm=None, residual=None, norm=None):
    nb, m, k = a.shape
    n = b.shape[2]
    o_spec = pl.BlockSpec((TM, n), lambda i: (i, 0))
    return _mm(name, a, b, dims=NN, grid=(m // TM,), inner=(nb, "lead"),
               a_spec=pl.BlockSpec((nb, TM, k), lambda i: (0, i, 0)),
               b_spec=pl.BlockSpec((nb, k, n), lambda i: (0, 0, 0)),
               o_spec=o_spec, out_shape=_sds((m, n), out_dtype), comm=comm, norm=norm,
               add=residual, add_spec=o_spec if residual is not None else None, add_scale=DN_ALPHA)


def _mm_tn(name, a, g, nb, out_dtype, a_blocked=False, g_mode="cols", comm=None):
    TM = TM_MM
    if a_blocked:
        _, m, ka = a.shape
        a_spec = pl.BlockSpec((None, TM, ka), lambda d, i: (d, i, 0))
    else:
        m, ka = a.shape
        a_spec = pl.BlockSpec((TM, ka), lambda d, i: (i, 0))
    if g_mode == "cols":
        n = g.shape[1] // nb
        g_spec = pl.BlockSpec((TM, n), lambda d, i: (i, d))
    elif g_mode == "blocked":
        n = g.shape[2]
        g_spec = pl.BlockSpec((None, TM, n), lambda d, i: (d, i, 0))
    else:
        n = g.shape[1]
        g_spec = pl.BlockSpec((TM, n), lambda d, i: (i, 0))
    nred = m // TM
    return _mm(name, a, g, dims=TN, grid=(nb, nred), nred=nred, red_axis=1,
               a_spec=a_spec, b_spec=g_spec,
               o_spec=pl.BlockSpec((None, ka, n), lambda d, i: (d, 0, 0)),
               out_shape=_sds((nb, ka, n), out_dtype), comm=comm)


def _mm_nt_red(name, g, w, out_dtype, g_blocked, add=None, add_scale=1.0, comm=None, blocks=None, post=None):
    _, k, n = w.shape
    nb = w.shape[0] if blocks is None else blocks[1]
    first = 0 if blocks is None else blocks[0]
    tm = TM_MM_F32 if nb * k * n <= 4 * 1024 * 768 else TM
    if g_blocked:
        m = g.shape[1]
        g_spec = pl.BlockSpec((nb, tm, n), lambda i: (0, i, 0))
    else:
        m = g.shape[0]
        g_spec = pl.BlockSpec((tm, nb * n), lambda i: (i, 0))
    o_spec = pl.BlockSpec((tm, k), lambda i: (i, 0))
    return _mm(name, g, w, dims=NT, grid=(m // tm,), inner=(nb, "lead" if g_blocked else "cols"),
               a_spec=g_spec, b_spec=pl.BlockSpec((nb, k, n), lambda i: (first // nb, 0, 0)),
               o_spec=o_spec, out_shape=_sds((m, k), out_dtype),
               add=add, add_spec=o_spec if add is not None else None, add_scale=add_scale, comm=comm, post=post)


def _mm_nt_out(name, g, w, out_dtype):
    TM = TM_MM
    m, n = g.shape
    nb, k, _ = w.shape
    return _mm(name, g, w, dims=NT, grid=(nb, m // TM),
               a_spec=pl.BlockSpec((TM, n), lambda d, i: (i, 0)),
               b_spec=pl.BlockSpec((None, k, n), lambda d, i: (d, 0, 0)),
               o_spec=pl.BlockSpec((None, TM, k), lambda d, i: (d, i, 0)),
               out_shape=_sds((nb, m, k), out_dtype))


def _ln_stats(r):
    mu = jnp.mean(r, axis=-1, keepdims=True)
    xc = r - mu
    var = jnp.mean(xc * xc, axis=-1, keepdims=True)
    rstd = lax.rsqrt(var + LN_EPS)
    return xc * rstd, rstd


def _ln_bwd(name, r, dy, gain):
    n, d = r.shape

    def body(r_ref, dy_ref, g_ref, dr_ref, drb_ref, dg_ref, db_ref):
        i = pl.program_id(0)
        xhat, rstd = _ln_stats(r_ref[...])
        dy_v = dy_ref[...]
        dxh = dy_v * g_ref[...]
        m1 = jnp.mean(dxh, axis=-1, keepdims=True)
        m2 = jnp.mean(dxh * xhat, axis=-1, keepdims=True)
        dr = rstd * (dxh - m1 - xhat * m2)
        dr_ref[...] = dr
        drb_ref[...] = dr.astype(BF16)
        dg = jnp.sum(dy_v * xhat, axis=0, keepdims=True)
        db = jnp.sum(dy_v, axis=0, keepdims=True)

        @pl.when(i == 0)
        def _():
            dg_ref[...] = dg
            db_ref[...] = db

        @pl.when(i > 0)
        def _():
            dg_ref[...] += dg
            db_ref[...] += db

    row = pl.BlockSpec((TM_MM_F32, d), lambda i: (i, 0))
    vec = pl.BlockSpec((1, d), lambda i: (0, 0))
    return _call(body, name, (n // TM_MM_F32,), [row, row, vec], [row, row, vec, vec],
                 [_sds((n, d), F32), _sds((n, d), BF16), _sds((1, d), F32), _sds((1, d), F32)])(r, dy, gain)


def _loss_head(y, target):
    n, d = y.shape

    def body(y_ref, t_ref, l_ref, dy_ref):
        i = pl.program_id(0)
        e = y_ref[...] - t_ref[...]
        dy_ref[...] = e * (1.0 / d)
        part = jnp.sum(jnp.sum(e * e, axis=1, keepdims=True), axis=0, keepdims=True) * (0.5 / d)
        part = jnp.broadcast_to(part, (8, 128))

        @pl.when(i == 0)
        def _():
            l_ref[...] = part

        @pl.when(i > 0)
        def _():
            l_ref[...] += part

    row = pl.BlockSpec((TM, d), lambda i: (i, 0))
    return _call(body, "loss_head", (n // TM,), [row, row],
                 [pl.BlockSpec((8, 128), lambda i: (0, 0)), row],
                 [_sds((8, 128), F32), _sds((n, d), F32)])(y, target)


def _group_expand_matrix():
    e = np.zeros((D_MODEL // SSM_GROUP, N_STATE), np.float32)
    for g in range(D_MODEL // SSM_GROUP):
        e[g, g * SSM_STATE:(g + 1) * SSM_STATE] = 1.0
    return jnp.asarray(e)


def _discretise(lr, li, ldt, br, bi, expand):
    dt = jnp.dot(jnp.exp(ldt), expand, precision=HIGHEST, preferred_element_type=F32)
    mag = jnp.exp(lr * dt)
    th = li * dt
    ab_r = mag * jnp.cos(th)
    ab_i = mag * jnp.sin(th)
    den = lr * lr + li * li
    nr = ab_r - 1.0
    co_r = (nr * lr + ab_i * li) / den
    co_i = (ab_i * lr - nr * li) / den
    bb_r = co_r * br - co_i * bi
    bb_i = co_r * bi + co_i * br
    return ab_r, ab_i, bb_r, bb_i


def _cmul(ar, ai, br, bi):
    return ar * br - ai * bi, ar * bi + ai * br


def _s5_prep(lr, li, ldt, br, bi):
    s = N_STATE
    expand = _group_expand_matrix()

    def body(lr_ref, li_ref, ldt_ref, br_ref, bi_ref, e_ref, abr_ref, abi_ref, bbr_ref, bbi_ref, tab_ref):
        ab_r, ab_i, bb_r, bb_i = _discretise(lr_ref[...], li_ref[...], ldt_ref[...], br_ref[...],
                                             bi_ref[...], e_ref[...])
        abr_ref[...] = ab_r
        abi_ref[...] = ab_i
        bbr_ref[...] = bb_r
        bbi_ref[...] = bb_i
        row = lax.broadcasted_iota(jnp.int32, (8, s), 0)
        for base, sign in ((0, 1.0), (8, -1.0)):
            p = [None] * 9
            p[1] = (ab_r, sign * ab_i)
            p[2] = _cmul(*p[1], *p[1])
            p[3] = _cmul(*p[2], *p[1])
            p[4] = _cmul(*p[2], *p[2])
            p[5] = _cmul(*p[4], *p[1])
            p[6] = _cmul(*p[4], *p[2])
            p[7] = _cmul(*p[4], *p[3])
            p[8] = _cmul(*p[4], *p[4])
            for j, k in enumerate((1, 2, 4)):
                keep = (row >= k) if base == 0 else (row < 8 - k)
                tab_ref[base + 2 * j] = jnp.where(keep, jnp.broadcast_to(p[k][0], (8, s)), 0.0)
                tab_ref[base + 2 * j + 1] = jnp.where(keep, jnp.broadcast_to(p[k][1], (8, s)), 0.0)
            pw_r = jnp.zeros((8, s), F32)
            pw_i = jnp.zeros((8, s), F32)
            for i in range(8):
                e = i + 1 if base == 0 else 8 - i
                pw_r = jnp.where(row == i, jnp.broadcast_to(p[e][0], (8, s)), pw_r)
                pw_i = jnp.where(row == i, jnp.broadcast_to(p[e][1], (8, s)), pw_i)
            tab_ref[base + 6] = pw_r
            tab_ref[base + 7] = pw_i

    def full(shape):
        return pl.BlockSpec(shape, lambda i: (0,) * len(shape))

    g = D_MODEL // SSM_GROUP
    return _call(body, "s5_prep", (1,),
                 [full((1, s)), full((1, s)), full((1, g)), full((16, s)), full((16, s)), full((g, s))],
                 [full((1, s)), full((1, s)), full((16, s)), full((16, s)), full((16, 8, s))],
                 [_sds((1, s), F32), _sds((1, s), F32), _sds((16, s), F32), _sds((16, s), F32),
                  _sds((16, 8, s), F32)])(lr, li, ldt, br, bi, expand)


def _s5_prep_bwd(lr, li, ldt, br, bi, d_abr, d_abi, d_bbr, d_bbi):
    s = N_STATE
    g = D_MODEL // SSM_GROUP
    expand = _group_expand_matrix()

    def body(lr_ref, li_ref, ldt_ref, br_ref, bi_ref, e_ref, c1, c2, c3, c4, o1, o2, o3, o4, o5):
        e_val = e_ref[...]
        _, vjp = jax.vjp(lambda a, b, c, d, e: _discretise(a, b, c, d, e, e_val),
                         lr_ref[...], li_ref[...], ldt_ref[...], br_ref[...], bi_ref[...])
        grads = vjp((c1[...], c2[...], c3[...], c4[...]))
        for o, gr in zip((o1, o2, o3, o4, o5), grads):
            o[...] = gr

    def full(shape):
        return pl.BlockSpec(shape, lambda i: (0,) * len(shape))

    return _call(body, "s5_prep_bwd", (1,),
                 [full((1, s)), full((1, s)), full((1, g)), full((16, s)), full((16, s)), full((g, s)),
                  full((1, s)), full((1, s)), full((16, s)), full((16, s))],
                 [full((1, s)), full((1, s)), full((1, g)), full((16, s)), full((16, s))],
                 [_sds((1, s), F32), _sds((1, s), F32), _sds((1, g), F32), _sds((16, s), F32),
                  _sds((16, s), F32)])(lr, li, ldt, br, bi, expand, d_abr, d_abi, d_bbr, d_bbi)


def _scan8(vr, vi, tab_ref, base, reverse):
    for j, k in enumerate((1, 2, 4)):
        mr = tab_ref[base + 2 * j]
        mi = tab_ref[base + 2 * j + 1]
        shift = 8 - k if reverse else k
        sr = pltpu.roll(vr, shift, 0)
        si = pltpu.roll(vi, shift, 0)
        vr, vi = vr + mr * sr - mi * si, vi + mr * si + mi * sr
    return vr, vi


def _s5_scan_fwd(x, bbr, bbi, cr, ci, dvec, tab, seq, comm=None):
    n, d = x.shape
    nt = seq // T_SCAN
    nblk = T_SCAN // 8
    st = CHUNK_ST

    def body(x_ref, bbr_ref, bbi_ref, cr_ref, ci_ref, d_ref, tab_ref, y_ref, hr_ref, hi_ref, car_r, car_i):
        t = pl.program_id(2)

        @pl.when(t == 0)
        def _():
            car_r[...] = jnp.zeros_like(car_r)
            car_i[...] = jnp.zeros_like(car_i)

        u = x_ref[...]
        ub = u.astype(BF16)
        hr_ref[...] = jnp.dot(ub, bbr_ref[...], preferred_element_type=F32)
        hi_ref[...] = jnp.dot(ub, bbi_ref[...], preferred_element_type=F32)

        def step(i, carry):
            sl = pl.ds(pl.multiple_of(i * 8, 8), 8)
            vr, vi = _scan8(hr_ref[sl, :], hi_ref[sl, :], tab_ref, 0, False)
            c_r = jnp.broadcast_to(car_r[...], (8, st))
            c_i = jnp.broadcast_to(car_i[...], (8, st))
            pr = tab_ref[6]
            pi = tab_ref[7]
            vr, vi = vr + pr * c_r - pi * c_i, vi + pr * c_i + pi * c_r
            hr_ref[sl, :] = vr
            hi_ref[sl, :] = vi
            car_r[...] = vr[7:8, :]
            car_i[...] = vi[7:8, :]
            return carry

        lax.fori_loop(0, nblk, step, 0, unroll=4)
        y = lax.dot_general(hr_ref[...].astype(BF16), cr_ref[...], NT, preferred_element_type=F32)
        y = y - lax.dot_general(hi_ref[...].astype(BF16), ci_ref[...], NT, preferred_element_type=F32)
        y_ref[...] = y + d_ref[...] * u

    row = lambda c, b, t: (b * nt + t, c)
    mat = pl.BlockSpec((None, CHUNK_CH, st), lambda c, b, t: (c, 0, 0))
    return _call(
        body, "s5_scan_fwd", (SSM_CHUNKS, n // seq, nt),
        [pl.BlockSpec((T_SCAN, CHUNK_CH), row), mat, mat, mat, mat,
         pl.BlockSpec((1, CHUNK_CH), lambda c, b, t: (0, c)),
         pl.BlockSpec((16, 8, st), lambda c, b, t: (0, 0, c))],
        [pl.BlockSpec((T_SCAN, CHUNK_CH), row), pl.BlockSpec((T_SCAN, st), row), pl.BlockSpec((T_SCAN, st), row)],
        [_sds((n, d), F32), _sds((n, N_STATE), F32), _sds((n, N_STATE), F32)],
        [pltpu.VMEM((1, st), F32), pltpu.VMEM((1, st), F32)], comm=comm,
    )(x, bbr, bbi, cr, ci, dvec, tab)


def _s5_scan_bwd(x, dy, dres, hr, hi, bbr, bbi, cr, ci, dvec, tab, seq, comm=None):
    n, d = x.shape
    nt = seq // T_SCAN
    nblk = T_SCAN // 8
    st = CHUNK_ST

    def body(x_ref, dy_ref, dres_ref, hr_ref, hi_ref, hpr_ref, hpi_ref, bbr_ref, bbi_ref, cr_ref, ci_ref,
             d_ref, tab_ref, dx_ref, dcr_ref, dci_ref, dbr_ref, dbi_ref, dar_ref, dai_ref, dd_ref,
             g_r, g_i, car_r, car_i):
        b = pl.program_id(1)
        tg = pl.program_id(2)
        first = jnp.logical_and(b == 0, tg == 0)

        @pl.when(tg == 0)
        def _():
            car_r[...] = jnp.zeros_like(car_r)
            car_i[...] = jnp.zeros_like(car_i)

        u = x_ref[...]
        dyv = dy_ref[...]
        ub = u.astype(BF16)
        dyb = dyv.astype(BF16)
        g_r[...] = jnp.dot(dyb, cr_ref[...], preferred_element_type=F32)
        g_i[...] = -jnp.dot(dyb, ci_ref[...], preferred_element_type=F32)

        def step(ii, carry):
            i = nblk - 1 - ii
            sl = pl.ds(pl.multiple_of(i * 8, 8), 8)
            vr, vi = _scan8(g_r[sl, :], g_i[sl, :], tab_ref, 8, True)
            c_r = jnp.broadcast_to(car_r[...], (8, st))
            c_i = jnp.broadcast_to(car_i[...], (8, st))
            pr = tab_ref[14]
            pi = tab_ref[15]
            vr, vi = vr + pr * c_r - pi * c_i, vi + pr * c_i + pi * c_r
            g_r[sl, :] = vr
            g_i[sl, :] = vi
            car_r[...] = vr[0:1, :]
            car_i[...] = vi[0:1, :]
            return carry

        lax.fori_loop(0, nblk, step, 0, unroll=4)
        gr = g_r[...]
        gi = g_i[...]
        grb = gr.astype(BF16)
        gib = gi.astype(BF16)
        du = lax.dot_general(grb, bbr_ref[...], NT, preferred_element_type=F32)
        du = du + lax.dot_general(gib, bbi_ref[...], NT, preferred_element_type=F32)
        dx_ref[...] = DN_ALPHA * dres_ref[...] + du + d_ref[...] * dyv

        h_r = hr_ref[...]
        h_i = hi_ref[...]
        t_orig = nt - 1 - tg
        row0 = lax.broadcasted_iota(jnp.int32, (8, st), 0) == 0
        halo_ok = t_orig > 0

        def previous(h, halo_ref):
            top = jnp.broadcast_to(jnp.where(halo_ok, halo_ref[7:8, :], 0.0), (8, st))
            rolled = pltpu.roll(h, 1, 0)
            return jnp.concatenate([jnp.where(row0, top, rolled[:8, :]), rolled[8:, :]], axis=0)

        hp_r = previous(h_r, hpr_ref)
        hp_i = previous(h_i, hpi_ref)
        dar = jnp.sum(gr * hp_r + gi * hp_i, axis=0, keepdims=True)
        dai = jnp.sum(gi * hp_r - gr * hp_i, axis=0, keepdims=True)
        dcr = lax.dot_general(dyb, h_r.astype(BF16), TN, preferred_element_type=F32)
        dci = lax.dot_general(dyb, h_i.astype(BF16), TN, preferred_element_type=F32)
        dbr = lax.dot_general(ub, grb, TN, preferred_element_type=F32)
        dbi = lax.dot_general(ub, gib, TN, preferred_element_type=F32)
        dd = jnp.sum(dyv * u, axis=0, keepdims=True)

        @pl.when(first)
        def _():
            dcr_ref[...] = dcr
            dci_ref[...] = dci
            dbr_ref[...] = dbr
            dbi_ref[...] = dbi
            dar_ref[...] = dar
            dai_ref[...] = dai
            dd_ref[...] = dd

        @pl.when(jnp.logical_not(first))
        def _():
            dcr_ref[...] += dcr
            dci_ref[...] += dci
            dbr_ref[...] += dbr
            dbi_ref[...] += dbi
            dar_ref[...] += dar
            dai_ref[...] += dai
            dd_ref[...] += dd

    row = lambda c, b, t: (b * nt + (nt - 1 - t), c)
    halo = lambda c, b, t: (jnp.maximum((b * nt + (nt - 1 - t)) * (T_SCAN // 8) - 1, 0), c)
    mat = pl.BlockSpec((None, CHUNK_CH, st), lambda c, b, t: (c, 0, 0))
    chan = pl.BlockSpec((T_SCAN, CHUNK_CH), row)
    state = pl.BlockSpec((T_SCAN, st), row)
    return _call(
        body, "s5_scan_bwd", (SSM_CHUNKS, n // seq, nt),
        [chan, chan, chan, state, state, pl.BlockSpec((8, st), halo), pl.BlockSpec((8, st), halo),
         mat, mat, mat, mat, pl.BlockSpec((1, CHUNK_CH), lambda c, b, t: (0, c)),
         pl.BlockSpec((16, 8, st), lambda c, b, t: (0, 0, c))],
        [chan, mat, mat, mat, mat, pl.BlockSpec((1, st), lambda c, b, t: (0, c)),
         pl.BlockSpec((1, st), lambda c, b, t: (0, c)), pl.BlockSpec((1, CHUNK_CH), lambda c, b, t: (0, c))],
        [_sds((n, d), F32)] + [_sds((SSM_CHUNKS, CHUNK_CH, st), F32)] * 4
        + [_sds((1, N_STATE), F32), _sds((1, N_STATE), F32), _sds((1, d), F32)],
        [pltpu.VMEM((T_SCAN, st), F32), pltpu.VMEM((T_SCAN, st), F32), pltpu.VMEM((1, st), F32),
         pltpu.VMEM((1, st), F32)], comm=comm,
    )(x, dy, dres, hr, hi, hr, hi, bbr, bbi, cr, ci, dvec, tab)


_GELU_C = math.sqrt(2.0 / math.pi)


def _gelu_parts(y):
    t = jnp.tanh(_GELU_C * (y + 0.044715 * (y * y * y)))
    return 0.5 * (1.0 + t), t


def _glu_fwd(y, w_glu, b_glu):
    n, d = y.shape

    def body(y_ref, w_ref, b_ref, y2_ref, z_ref, gg_ref):
        yv = y_ref[...]
        cdf, _ = _gelu_parts(yv)
        y2 = yv * cdf
        z = jnp.dot(y2.astype(BF16), w_ref[...], preferred_element_type=F32) + b_ref[...]
        y2_ref[...] = y2.astype(BF16)
        z_ref[...] = z
        gg_ref[...] = (y2 * jax.nn.sigmoid(z)).astype(BF16)

    row = pl.BlockSpec((TM_MM_F32, d), lambda i: (i, 0))
    return _call(body, "glu_fwd", (n // TM_MM_F32,),
                 [row, pl.BlockSpec((d, d), lambda i: (0, 0)), pl.BlockSpec((1, d), lambda i: (0, 0))],
                 [row, row, row], [_sds((n, d), BF16), _sds((n, d), F32), _sds((n, d), BF16)])(y, w_glu, b_glu)


def _glu_bwd_gate(y, z, dgg, comm=None):
    n, d = y.shape

    def body(y_ref, z_ref, dgg_ref, dz_ref, db_ref, t_ref):
        i = pl.program_id(0)
        yv = y_ref[...]
        cdf, _ = _gelu_parts(yv)
        y2 = yv * cdf
        s = jax.nn.sigmoid(z_ref[...])
        dg = dgg_ref[...]
        dz = dg * y2 * s * (1.0 - s)
        dz_ref[...] = dz.astype(BF16)
        t_ref[...] = dg * s
        db = jnp.sum(dz, axis=0, keepdims=True)

        @pl.when(i == 0)
        def _():
            db_ref[...] = db

        @pl.when(i > 0)
        def _():
            db_ref[...] += db

    row = pl.BlockSpec((TM, d), lambda i: (i, 0))
    vec = pl.BlockSpec((1, d), lambda i: (0, 0))
    return _call(body, "glu_bwd_gate", (n // TM,), [row, row, row], [row, vec, row],
                 [_sds((n, d), BF16), _sds((1, d), F32), _sds((n, d), F32)], comm=comm)(y, z, dgg)


def _gelu_bwd(dy2, y):
    cdf, t = _gelu_parts(y)
    dinner = _GELU_C * (1.0 + 3.0 * 0.044715 * y * y)
    return dy2 * (cdf + 0.5 * y * (1.0 - t * t) * dinner)


HALO = 16


def _shift_down(x, halo, k):
    out = pltpu.roll(x, k, 0)
    row = lax.broadcasted_iota(jnp.int32, (8, x.shape[1]), 0)
    top = jnp.where(row < k, pltpu.roll(halo[HALO - 8:HALO, :], k, 0), out[:8, :])
    return jnp.concatenate([top, out[8:, :]], axis=0)


def _conv3(x, halo, w, b):
    x1 = _shift_down(x, halo, 1)
    x2 = _shift_down(x, halo, 2)
    return b + w[0:1, :] * x + w[1:2, :] * x1 + w[2:3, :] * x2, x1, x2


def _conv_gate_fwd(hc, conv_w, conv_b, seq, comm=None):
    _, nb, n, c = hc.shape
    tiles_per_seq = seq // TM

    def body(x_ref, halo_ref, w_ref, b_ref, o_ref, cv_ref):
        i = pl.program_id(1)
        start = (i % tiles_per_seq) == 0
        cv = []
        for h in range(2):
            halo = jnp.where(start, 0.0, halo_ref[h].astype(F32))
            cv.append(_conv3(x_ref[h].astype(F32), halo, w_ref[h], b_ref[h])[0])
            cv_ref[h] = cv[h].astype(BF16)
        o_ref[...] = (cv[1] * jax.nn.sigmoid(cv[1]) * cv[0]).astype(BF16)

    both = pl.BlockSpec((2, None, TM, c), lambda d, i: (0, d, i, 0))
    return _call(
        body, "conv_gate_fwd", (nb, n // TM),
        [both,
         pl.BlockSpec((2, None, HALO, c), lambda d, i: (0, d, jnp.maximum(i * (TM // HALO) - 1, 0), 0)),
         pl.BlockSpec((2, None, 3, c), lambda d, i: (0, d, 0, 0)),
         pl.BlockSpec((2, None, 1, c), lambda d, i: (0, d, 0, 0))],
        [pl.BlockSpec((None, TM, c), lambda d, i: (d, i, 0)), both],
        [_sds((nb, n, c), BF16), _sds((2, nb, n, c), BF16)], comm=comm)(hc, hc, conv_w, conv_b)


def _conv_gate_bwd(hc, cv, dact, conv_w, seq, comm=None):
    _, nb, n, c = hc.shape
    tiles_per_seq = seq // TM
    last_halo = n // HALO - 1
    ext = TM + HALO

    def body(x_ref, cv_ref, cvn_ref, da_ref, dan_ref, w_ref, dx_ref, dw_ref):
        i = pl.program_id(1)
        end = (i % tiles_per_seq) == tiles_per_seq - 1
        row = lax.broadcasted_iota(jnp.int32, (ext, c), 0)
        cv = [jnp.concatenate([cv_ref[h], cvn_ref[h]], axis=0).astype(F32) for h in range(2)]
        da = jnp.concatenate([da_ref[...], dan_ref[...]], axis=0).astype(F32)
        da = jnp.where(jnp.logical_and(end, row >= TM), 0.0, da)
        sg = jax.nn.sigmoid(cv[1])
        silu = cv[1] * sg
        dcv = [da * silu, da * cv[0] * (sg + silu * (1.0 - sg))]
        for h in range(2):
            w = w_ref[h]
            g = dcv[h]
            g0 = g[:TM, :]
            g1 = pltpu.roll(g, ext - 1, 0)[:TM, :]
            g2 = pltpu.roll(g, ext - 2, 0)[:TM, :]
            dx_ref[h] = (w[0:1, :] * g0 + w[1:2, :] * g1 + w[2:3, :] * g2).astype(BF16)
            x = x_ref[h].astype(F32)
            parts = [jnp.sum(g0 * x, axis=0, keepdims=True),
                     jnp.sum(g1 * x, axis=0, keepdims=True),
                     jnp.sum(g2 * x, axis=0, keepdims=True),
                     jnp.sum(g0, axis=0, keepdims=True)]
            upd = jnp.concatenate(parts + [jnp.zeros((4, c), F32)], axis=0)

            @pl.when(i == 0)
            def _():
                dw_ref[h] = upd

            @pl.when(i > 0)
            def _():
                dw_ref[h] += upd

    nxt = lambda i: jnp.minimum((i + 1) * (TM // HALO), last_halo)
    return _call(
        body, "conv_gate_bwd", (nb, n // TM),
        [pl.BlockSpec((2, None, TM, c), lambda d, i: (0, d, i, 0)),
         pl.BlockSpec((2, None, TM, c), lambda d, i: (0, d, i, 0)),
         pl.BlockSpec((2, None, HALO, c), lambda d, i: (0, d, nxt(i), 0)),
         pl.BlockSpec((None, TM, c), lambda d, i: (d, i, 0)),
         pl.BlockSpec((None, HALO, c), lambda d, i: (d, nxt(i), 0)),
         pl.BlockSpec((2, None, 3, c), lambda d, i: (0, d, 0, 0))],
        [pl.BlockSpec((2, None, TM, c), lambda d, i: (0, d, i, 0)),
         pl.BlockSpec((2, None, 8, c), lambda d, i: (0, d, 0, 0))],
        [_sds((2, nb, n, c), BF16), _sds((2, nb, 8, c), F32)], comm=comm)(hc, cv, cv, dact, dact, conv_w)


def _t5_bucket_np(dist):
    exact = REL_BUCKETS // 2
    d = np.maximum(dist, 1).astype(np.float32)
    large = exact + (np.log(d / exact) / math.log(REL_MAX_DIST / exact) * (REL_BUCKETS - exact)).astype(np.int64)
    large = np.minimum(large, REL_BUCKETS - 1)
    return np.where(dist < exact, dist, large).astype(np.int32)


def _bucket_table(dil):
    steps = np.arange(BAND)[:, None] + BAND - np.arange(2 * BAND)[None, :]
    return _t5_bucket_np(np.maximum(steps, 0) * dil).reshape(1, BAND * 2 * BAND).astype(np.float32)


def _bias_expand(rel_t, dil):
    nk = BAND * 2 * BAND
    bucket = jnp.asarray(_bucket_table(dil))

    def body(r_ref, bk_ref, o_ref):
        ids = lax.broadcasted_iota(jnp.int32, (REL_BUCKETS, nk), 0).astype(F32)
        onehot = (ids == bk_ref[...]).astype(F32)
        o_ref[...] = jnp.dot(r_ref[...], onehot, precision=HIGHEST, preferred_element_type=F32)

    return _call(body, "bias_expand", (1,),
                 [pl.BlockSpec((N_HEADS, REL_BUCKETS), lambda i: (0, 0)), pl.BlockSpec((1, nk), lambda i: (0, 0))],
                 pl.BlockSpec((N_HEADS, nk), lambda i: (0, 0)), _sds((N_HEADS, nk), F32))(rel_t, bucket)


def _bias_reduce(dbias, dil):
    nk = BAND * 2 * BAND
    bucket = jnp.asarray(_bucket_table(dil))

    def body(g_ref, bk_ref, o_ref):
        ids = lax.broadcasted_iota(jnp.int32, (REL_BUCKETS, nk), 0).astype(F32)
        onehot = (ids == bk_ref[...]).astype(F32)
        o_ref[...] = lax.dot_general(g_ref[...], onehot, NT, precision=HIGHEST, preferred_element_type=F32)

    return _call(body, "bias_reduce", (1,),
                 [pl.BlockSpec((N_HEADS, nk), lambda i: (0, 0)), pl.BlockSpec((1, nk), lambda i: (0, 0))],
                 pl.BlockSpec((N_HEADS, REL_BUCKETS), lambda i: (0, 0)),
                 _sds((N_HEADS, REL_BUCKETS), F32))(dbias, bucket)


def _valid_mask(n):
    qi = lax.broadcasted_iota(jnp.int32, (BAND, 2 * BAND), 0)
    kj = lax.broadcasted_iota(jnp.int32, (BAND, 2 * BAND), 1)
    steps = qi + BAND - kj
    in_band = jnp.logical_and(steps >= 0, steps <= BAND)
    return jnp.logical_and(in_band, jnp.logical_or(n > 0, kj >= BAND))


ATTN_COLS = {1: 1024, 4: 512, 16: 128}


def _residue(dil, r):
    return slice(None) if dil == 1 else pl.ds(r, BAND, stride=dil)


def _stack_heads(x, first):
    return jnp.concatenate([jnp.where(first, x, 0.0), jnp.where(first, 0.0, x)], axis=0)


def _attn_fwd(q, k, v, bias, g, dil, bsz, seq, comm=None, merge=()):
    n = q.shape[0]
    rows = BAND * dil
    nch = seq // rows
    w = N_HEADS * HEAD_DIM
    wc = ATTN_COLS[dil]
    ncol = w // wc
    hpc = wc // HEAD_DIM
    nj = wc // 128
    slab = lambda k: pltpu.VMEM((k * nj, rows, 128), F32)

    def attend(q_ref, kc_ref, kp_ref, vc_ref, vp_ref, b_ref, others, o_ref, l_ref, ob_ref, qf, kf, vf, of, lf):
        ch = pl.program_id(2)
        valid = _valid_mask(ch)
        valid = jnp.concatenate([valid, valid], axis=0)
        lane = lax.broadcasted_iota(jnp.int32, (BAND, 128), 1)
        first = lane < HEAD_DIM
        for j in range(nj):
            sl = slice(128 * j, 128 * (j + 1))
            qf[j] = q_ref[:, sl].astype(F32)
            kf[j] = kp_ref[:, sl].astype(F32)
            kf[nj + j] = kc_ref[:, sl].astype(F32)
            vf[j] = vp_ref[:, sl].astype(F32)
            vf[nj + j] = vc_ref[:, sl].astype(F32)
        for r in range(dil):
            rr = _residue(dil, r)
            for j in range(nj):
                q2 = qf.at[j][rr, :]
                k2 = jnp.concatenate([kf.at[j][rr, :], kf.at[nj + j][rr, :]], axis=0).astype(BF16)
                v2 = jnp.concatenate([vf.at[j][rr, :], vf.at[nj + j][rr, :]], axis=0).astype(BF16)
                qs = _stack_heads(q2, first).astype(BF16)
                s = lax.dot_general(qs, k2, NT, preferred_element_type=F32) * (HEAD_DIM ** -0.5)
                s = jnp.where(valid, s + b_ref[2 * j:2 * j + 2].reshape(2 * BAND, 2 * BAND), NEG_BIG)
                m = jnp.max(s, axis=1, keepdims=True)
                p = jnp.exp(s - m)
                l = jnp.sum(p, axis=1, keepdims=True)
                pv = jnp.dot(p.astype(BF16), v2, preferred_element_type=F32) / l
                lse = jnp.broadcast_to(m + jnp.log(l), (2 * BAND, 128))
                of.at[j][rr, :] = jnp.where(first, pv[:BAND], pv[BAND:])
                lf.at[j][rr, :] = jnp.where(first, lse[:BAND], lse[BAND:])
        for j in range(nj):
            sl = slice(128 * j, 128 * (j + 1))
            if merge:
                o0, l0, o1, l1 = (r[:, sl] for r in others)
                lc = lf[j]
                m = jnp.maximum(jnp.maximum(l0, l1), lc)
                wa, wb, wc_ = jnp.exp(l0 - m), jnp.exp(l1 - m), jnp.exp(lc - m)
                tot = wa + wb + wc_
                merged = (wa * o0 + wb * o1 + wc_ * of[j]) / tot
                o_ref[:, sl] = merged
                ob_ref[:, sl] = merged.astype(BF16)
                l_ref[:, sl] = m + jnp.log(tot)
            else:
                o_ref[:, sl] = of[j]
                l_ref[:, sl] = lf[j]

    def body(q_ref, kc_ref, kp_ref, vc_ref, vp_ref, b_ref, *rest):
        others, rest = rest[:len(merge)], rest[len(merge):]
        outs_, scr = (rest[:3], rest[3:]) if merge else (rest[:2] + (None,), rest[2:])
        attend(q_ref, kc_ref, kp_ref, vc_ref, vp_ref, b_ref, others, *outs_, *scr)

    base = g * ncol
    cur = lambda c, b, ch: (b * nch + ch, base + c)
    prev = lambda c, b, ch: (b * nch + jnp.maximum(ch - 1, 0), base + c)
    blk = lambda im: pl.BlockSpec((rows, wc), im)
    out_blk = pl.BlockSpec((rows, wc), lambda c, b, ch: (b * nch + ch, c))
    return _call(
        body, "attn_fwd_d%d" % dil, (ncol, bsz, nch),
        [blk(cur), blk(cur), blk(prev), blk(cur), blk(prev),
         pl.BlockSpec((hpc, BAND, 2 * BAND), lambda c, b, ch: (c, 0, 0))] + [out_blk] * len(merge),
        [out_blk, out_blk] + ([out_blk] if merge else []),
        [_sds((n, w), F32)] * 2 + ([_sds((n, w), BF16)] if merge else []),
        [slab(1), slab(2), slab(2), slab(1), slab(1)], comm=comm,
    )(q, k, k, v, v, bias, *merge)


def _attn_bwd(q, k, v, do, o, lse, bias, g, dil, bsz, seq, comm=None, into=()):
    n = q.shape[0]
    rows = BAND * dil
    nch = seq // rows
    w = N_HEADS * HEAD_DIM
    wc = ATTN_COLS[dil]
    ncol = w // wc
    hpc = wc // HEAD_DIM
    nj = wc // 128
    slab = lambda k: pltpu.VMEM((k * nj, rows, 128), F32)
    scale = HEAD_DIM ** -0.5

    def body(q_ref, kc_ref, kp_ref, vc_ref, vp_ref, do_ref, o_ref, l_ref, b_ref,
             dq_ref, dk_ref, dv_ref, db_ref, qf, kf, vf, dof, of, lf, dqf, dkf, dvf):
        b = pl.program_id(1)
        ch = pl.program_id(2)
        cur = ch % 2
        prv = 1 - cur

        @pl.when(jnp.logical_and(b == 0, ch == 0))
        def _():
            db_ref[...] = jnp.zeros_like(db_ref)

        @pl.when(ch == 0)
        def _():
            for j in range(nj):
                dkf[nj + j] = jnp.zeros((rows, 128), F32)
                dvf[nj + j] = jnp.zeros((rows, 128), F32)

        @pl.when(ch < nch)
        def _():
            valid = _valid_mask(ch)
            valid = jnp.concatenate([valid, valid], axis=0)
            first = lax.broadcasted_iota(jnp.int32, (BAND, 128), 1) < HEAD_DIM
            for j in range(nj):
                sl = slice(128 * j, 128 * (j + 1))
                qf[j] = q_ref[:, sl].astype(F32)
                kf[j] = kp_ref[:, sl].astype(F32)
                kf[nj + j] = kc_ref[:, sl].astype(F32)
                vf[j] = vp_ref[:, sl].astype(F32)
                vf[nj + j] = vc_ref[:, sl].astype(F32)
                dof[j] = do_ref[:, sl]
                of[j] = o_ref[:, sl]
                lf[j] = l_ref[:, sl]
            for r in range(dil):
                rr = _residue(dil, r)
                for j in range(nj):
                    k2 = jnp.concatenate([kf.at[j][rr, :], kf.at[nj + j][rr, :]], axis=0).astype(BF16)
                    v2 = jnp.concatenate([vf.at[j][rr, :], vf.at[nj + j][rr, :]], axis=0).astype(BF16)
                    do2 = dof.at[j][rr, :]
                    lse2 = lf.at[j][rr, :]
                    qs = _stack_heads(qf.at[j][rr, :], first).astype(BF16)
                    dos = _stack_heads(do2, first)
                    delta = jnp.sum(dos * jnp.concatenate([of.at[j][rr, :]] * 2, axis=0), axis=1, keepdims=True)
                    dos = dos.astype(BF16)
                    lse_col = jnp.concatenate([lse2[:, 0:1], lse2[:, HEAD_DIM:HEAD_DIM + 1]], axis=0)
                    s = lax.dot_general(qs, k2, NT, preferred_element_type=F32) * scale
                    s = jnp.where(valid, s + b_ref[2 * j:2 * j + 2].reshape(2 * BAND, 2 * BAND), NEG_BIG)
                    p = jnp.exp(s - lse_col)
                    dp = lax.dot_general(dos, v2, NT, preferred_element_type=F32)
                    ds = p * (dp - delta)
                    db_ref[2 * j:2 * j + 2] += ds.reshape(2, BAND, 2 * BAND)
                    dsb = ds.astype(BF16)
                    dq2 = jnp.dot(dsb, k2, preferred_element_type=F32) * scale
                    dk2 = lax.dot_general(dsb, qs, TN, preferred_element_type=F32) * scale
                    dv2 = lax.dot_general(p.astype(BF16), dos, TN, preferred_element_type=F32)
                    dqf.at[j][rr, :] = jnp.where(first, dq2[:BAND], dq2[BAND:])
                    dkf.at[prv * nj + j][rr, :] += dk2[:BAND, :]
                    dvf.at[prv * nj + j][rr, :] += dv2[:BAND, :]
                    dkf.at[cur * nj + j][rr, :] = dk2[BAND:, :]
                    dvf.at[cur * nj + j][rr, :] = dv2[BAND:, :]
            for j in range(nj):
                dq_ref[:, 128 * j:128 * (j + 1)] = dqf[j].astype(BF16)

        for j in range(nj):
            sl = slice(128 * j, 128 * (j + 1))
            dk_ref[:, sl] = dkf[prv * nj + j].astype(BF16)
            dv_ref[:, sl] = dvf[prv * nj + j].astype(BF16)

    last = nch - 1
    base = g * ncol
    cur3 = lambda c, b, ch: (b * nch + jnp.minimum(ch, last), base + c)
    prev3 = lambda c, b, ch: (b * nch + jnp.maximum(jnp.minimum(ch, last) - 1, 0), base + c)
    cur1 = lambda c, b, ch: (b * nch + jnp.minimum(ch, last), c)
    lag3 = lambda c, b, ch: (b * nch + jnp.maximum(ch - 1, 0), base + c)
    blk = lambda im: pl.BlockSpec((rows, wc), im)
    bias_blk = pl.BlockSpec((hpc, BAND, 2 * BAND), lambda c, b, ch: (c, 0, 0))
    out = _sds(q.shape, BF16)
    return _call(
        body, "attn_bwd_d%d" % dil, (ncol, bsz, nch + 1),
        [blk(cur3), blk(cur3), blk(prev3), blk(cur3), blk(prev3), blk(cur1), blk(cur1), blk(cur1), bias_blk],
        [blk(cur3), blk(lag3), blk(lag3), bias_blk],
        [out, out, out, _sds((N_HEADS, BAND, 2 * BAND), F32)],
        [slab(1), slab(2), slab(2), slab(1), slab(1), slab(1), slab(1), slab(2), slab(2)], comm=comm, fill=into,
    )(q, k, k, v, v, do, o, lse, bias)


def _adamw(name, parts, w, m, v):
    nl, r, c = w.shape
    nparts = parts[0].shape[0]
    tr = r
    for cand in (256, 352, 128):
        if r % cand == 0 and r > cand:
            tr = cand
            break
    c1 = 1.0 - ADAM_B1 ** ADAM_STEP
    c2 = 1.0 - ADAM_B2 ** ADAM_STEP

    def body(*refs):
        p_refs = refs[:nl]
        w_ref, m_ref, v_ref, g_ref, d_ref, nm_ref, nv_ref = refs[nl:]
        layer = pl.program_id(0)
        for l in range(nl):
            @pl.when(layer == l)
            def _():
                g = p_refs[l][0].astype(F32)
                for dev in range(1, nparts):
                    g = g + p_refs[l][dev].astype(F32)
                nm = ADAM_B1 * m_ref[...] + (1.0 - ADAM_B1) * g
                nv = ADAM_B2 * v_ref[...] + (1.0 - ADAM_B2) * (g * g)
                m_hat = nm / c1
                v_hat = nv / c2
                g_ref[...] = g
                d_ref[...] = -ADAM_LR * (m_hat / (jnp.sqrt(v_hat) + ADAM_EPS) + ADAM_WD * w_ref[...])
                nm_ref[...] = nm
                nv_ref[...] = nv

    def part_spec(l):
        return pl.BlockSpec((nparts, tr, c), lambda layer, i: (0, jnp.where(layer == l, i, 0), 0))

    row = pl.BlockSpec((None, tr, c), lambda layer, i: (layer, i, 0))
    return _call(body, name, (nl, r // tr), [part_spec(l) for l in range(nl)] + [row, row, row],
                 [row] * 4, [_sds((nl, r, c), F32)] * 4)(*parts, w, m, v)


def _exchange(name, srcs, out_shapes, items):
    n_in = len(srcs)
    n_out = len(out_shapes)
    n_items = len(items)

    def body(*refs):
        ins = refs[:n_in]
        outs = refs[n_in:n_in + n_out]
        send_sems, recv_sems, local_sems = refs[n_in + n_out:]
        x, y, c = lax.axis_index("x"), lax.axis_index("y"), lax.axis_index("c")
        me = 4 * x + 2 * y + c

        def pick(ref, sel, peer):
            idx = tuple(peer if s == "peer" else me if s == "me" else s for s in sel)
            return ref.at[idx] if idx else ref

        copies = []
        for it, (si, ssel, oi, osel) in enumerate(items):
            local = pltpu.make_async_copy(pick(ins[si], ssel, me), pick(outs[oi], osel, me), local_sems.at[it])
            local.start()
            copies.append(local)
            for j in range(1, N_DEV):
                px = 1 - x if j & 4 else x
                py = 1 - y if j & 2 else y
                pc = 1 - c if j & 1 else c
                peer = 4 * px + 2 * py + pc
                sem = it * (N_DEV - 1) + j - 1
                cp = pltpu.make_async_remote_copy(
                    src_ref=pick(ins[si], ssel, peer), dst_ref=pick(outs[oi], osel, peer),
                    send_sem=send_sems.at[sem], recv_sem=recv_sems.at[sem],
                    device_id=(px, py, pc), device_id_type=pl.DeviceIdType.MESH)
                cp.start()
                copies.append(cp)
        for cp in copies:
            cp.wait()

    any_spec = pl.BlockSpec(memory_space=pl.ANY)
    return pl.pallas_call(
        body, name=name, in_specs=[any_spec] * n_in, out_specs=[any_spec] * n_out, out_shape=list(out_shapes),
        scratch_shapes=[pltpu.SemaphoreType.DMA((n_items * (N_DEV - 1),)),
                        pltpu.SemaphoreType.DMA((n_items * (N_DEV - 1),)),
                        pltpu.SemaphoreType.DMA((n_items,))],
        compiler_params=pltpu.CompilerParams(has_side_effects=True),
    )(*srcs)


def _sel(ref, sel, slot=None):
    idx = tuple(slot if s == "slot" else s for s in sel)
    return ref.at[idx] if idx else ref


def _gather_plan(srcs, out_shapes, items):
    per = N_DEV - 1
    n_items = len(items)

    def phases(ins, outs, send_sems, recv_sems, local_sems, total):
        x, y, c = lax.axis_index("x"), lax.axis_index("y"), lax.axis_index("c")
        sibling = (x, y, 1 - c)
        chips = [(1 - x, y), (x, 1 - y), (1 - x, 1 - y)]
        slot_of = lambda px, py, pc: 4 * px + 2 * py + pc

        def copy(it, k, block, to, src=None):
            _, _, oi, osel = items[it]
            dst = _sel(outs[oi], osel, slot_of(*block))
            return pltpu.make_async_remote_copy(
                src_ref=dst if src is None else src, dst_ref=dst,
                send_sem=send_sems.at[it * per + k], recv_sem=recv_sems.at[it * per + k],
                device_id=to, device_id_type=pl.DeviceIdType.MESH)

        def own_copies(it):
            si, ssel, oi, osel = items[it]
            src = _sel(ins[si], ssel)
            mine = pltpu.make_async_copy(src, _sel(outs[oi], osel, slot_of(x, y, c)), local_sems.at[it])
            first = [copy(it, 0, (x, y, c), sibling, src=src)]
            first += [copy(it, 1 + j, (x, y, c), (*chip, c), src=src) for j, chip in enumerate(chips)]
            return mine, first

        def start():
            for it in range(n_items):
                mine, first = own_copies(it)
                mine.start()
                for cp in first:
                    cp.start()

        def forward(it):
            def go():
                for j, chip in enumerate(chips):
                    copy(it, 1 + j, (*chip, c), (x, y, c)).wait_recv()
                    copy(it, 4 + j, (*chip, c), sibling).start()
            return go

        def finish():
            for it in range(n_items):
                copy(it, 0, sibling, (x, y, c)).wait_recv()
                for j, chip in enumerate(chips):
                    copy(it, 4 + j, (*chip, 1 - c), (x, y, c)).wait_recv()
            for it in range(n_items):
                mine, first = own_copies(it)
                for cp in first:
                    cp.wait_send()
                for j, chip in enumerate(chips):
                    copy(it, 4 + j, (*chip, c), sibling).wait_send()
                mine.wait()

        lo = max(3 * total // 4, 1)
        acts = [(0, start)]
        for it in range(n_items):
            acts.append((min(lo + it * (total - 1 - lo) // n_items, total - 2), forward(it)))
        acts.append((total - 1, finish))
        return acts

    return _Plan(srcs, out_shapes, n_items * per, n_items, phases)


def _scatter_plan(srcs, out_shapes, items):
    per = N_DEV - 1
    n_items = len(items)

    def phases(ins, outs, send_sems, recv_sems, local_sems, total):
        x, y, c = lax.axis_index("x"), lax.axis_index("y"), lax.axis_index("c")
        me = 4 * x + 2 * y + c

        def copies():
            out = []
            for it, (si, oi, osel) in enumerate(items):
                dst = _sel(outs[oi], osel, me)
                out.append(pltpu.make_async_copy(ins[si].at[me], dst, local_sems.at[it]))
                for j in range(1, N_DEV):
                    px = 1 - x if j & 4 else x
                    py = 1 - y if j & 2 else y
                    pc = 1 - c if j & 1 else c
                    out.append(pltpu.make_async_remote_copy(
                        src_ref=ins[si].at[4 * px + 2 * py + pc], dst_ref=dst,
                        send_sem=send_sems.at[it * per + j - 1], recv_sem=recv_sems.at[it * per + j - 1],
                        device_id=(px, py, pc), device_id_type=pl.DeviceIdType.MESH))
            return out

        def start():
            for cp in copies():
                cp.start()

        def finish():
            for cp in copies():
                cp.wait()

        return [(0, start), (total - 1, finish)]

    return _Plan(srcs, out_shapes, n_items * per, n_items, phases)


def _sum_slots(parts):
    _, r, c = parts.shape

    def body(p_ref, o_ref):
        acc = p_ref[0]
        for dev in range(1, N_DEV):
            acc = acc + p_ref[dev]
        o_ref[...] = acc

    return _call(body, "sum_slots", (1,), [pl.BlockSpec((N_DEV, r, c), lambda i: (0, 0, 0))],
                 pl.BlockSpec((r, c), lambda i: (0, 0)), _sds((r, c), F32))(parts)


def _pack(arrays):
    rows = []
    for a in arrays:
        flat = a.reshape(-1).astype(F32)
        pad = (-flat.shape[0]) % 1024
        if pad:
            flat = jnp.concatenate([flat, jnp.zeros((pad,), F32)])
        rows.append(flat.reshape(-1, 128))
    return jnp.concatenate(rows, axis=0)


def _unpack(buf, shapes):
    out = []
    r0 = 0
    for shp in shapes:
        size = int(np.prod(shp))
        nrows = -(-size // 1024) * 8
        out.append(buf[r0:r0 + nrows].reshape(-1)[:size].reshape(shp))
        r0 += nrows
    return out


def _block_diag_chunks(val):
    eye = jnp.eye(8, dtype=val.dtype)
    return jnp.einsum("cjhp,jk->cjhkp", val, eye).reshape(SSM_CHUNKS, CHUNK_CH, CHUNK_ST)


def _diag_blocks(dense):
    d5 = dense.reshape(SSM_CHUNKS, 8, SSM_GROUP, 8, SSM_STATE)
    return jnp.stack([d5[:, j, :, j, :] for j in range(8)], axis=1)


def _carried(res, comm):
    return res if comm is not None else (res, None)


def _ffn_fwd(tag, hin, hin_b, w_up, conv_w, conv_b, w_down4, gain, bias, seq, comm_up=None, comm_conv=None,
             comm_down=None):
    hc, got_up = _carried(_mm_nn("ffn_up" + tag, hin_b, w_up, FFN_HIDDEN, blocked_out=True, comm=comm_up), comm_up)
    nb, n, c = hc.shape
    hc = hc.reshape(2, nb // 2, n, c)
    (act, cv), got_conv = _carried(_conv_gate_fwd(hc, conv_w, conv_b, seq, comm=comm_conv), comm_conv)
    (r, hout, hout_b), got_down = _carried(
        _mm_nn_red("ffn_down" + tag, act, w_down4, F32, comm=comm_down, residual=hin, norm=(gain, bias)), comm_down)
    return hout, hout_b, (hin_b, hc, cv, act, r), (got_up, got_conv, got_down)


def _ffn_bwd(tag, saved, dh, w_up, conv_w, conv_b, w_down4, gain, seq, comm_conv=None, comm_dw=None,
             comm_dx=None):
    hin_b, hc, cv, act, r = saved
    dr, dr_b, dgain, dbias = _ln_bwd("ffn_norm_bwd" + tag, r, dh, gain)
    d_wdown = _mm_tn("ffn_down_dw" + tag, act, dr_b, act.shape[0], BF16, a_blocked=True, g_mode="shared")
    dact = _mm_nt_out("ffn_down_dx" + tag, dr_b, w_down4, FFN_HIDDEN)
    (dhc, dconv), got_conv = _carried(_conv_gate_bwd(hc, cv, dact, conv_w, seq, comm=comm_conv), comm_conv)
    dhc8 = dhc.reshape(2 * dhc.shape[1], dhc.shape[2], dhc.shape[3])
    d_wup, got_dw = _carried(_mm_tn("ffn_up_dw" + tag, dhc8, hin_b, dhc8.shape[0], BF16, a_blocked=True,
                                    g_mode="shared", comm=comm_dw), comm_dw)
    dhin, got_dx = _carried(_mm_nt_red("ffn_up_dx" + tag, dhc8, w_up, F32, g_blocked=True, add=dr,
                                       add_scale=DN_ALPHA, comm=comm_dx), comm_dx)
    return dhin, d_wup, d_wdown, dconv, dgain, dbias, (got_conv, got_dw, got_dx)


def _local_step(x, target, p, wfull, bsz, seq, ex=None):
    n = bsz * seq
    xf = x.reshape(n, D_MODEL)
    tf = target.reshape(n, D_MODEL)

    lr = p["s5_lam_re"].reshape(1, N_STATE)
    li = p["s5_lam_im"].reshape(1, N_STATE)
    ldt = p["s5_log_dt"].reshape(1, D_MODEL // SSM_GROUP)
    br_t = p["s5_b_re"].reshape(N_STATE, SSM_GROUP).T
    bi_t = p["s5_b_im"].reshape(N_STATE, SSM_GROUP).T
    ab_r, ab_i, bb_r, bb_i, tab = _s5_prep(lr, li, ldt, br_t, bi_t)

    def bb_chunks(bb):
        return _block_diag_chunks(bb.reshape(SSM_GROUP, SSM_CHUNKS, 8, SSM_STATE).transpose(1, 2, 0, 3))

    def c_chunks(cc):
        return _block_diag_chunks(cc.reshape(SSM_CHUNKS, 8, SSM_GROUP, SSM_STATE))

    bbr_c, bbi_c = bb_chunks(bb_r).astype(BF16), bb_chunks(bb_i).astype(BF16)
    cr_c, ci_c = c_chunks(p["s5_c_re"]).astype(BF16), c_chunks(p["s5_c_im"]).astype(BF16)
    dvec = p["s5_d"].reshape(1, D_MODEL)

    if ex is None:
        y, h_r, h_i = _s5_scan_fwd(xf, bbr_c, bbi_c, cr_c, ci_c, dvec, tab, seq)
    else:
        (y, h_r, h_i), gathered = _s5_scan_fwd(xf, bbr_c, bbi_c, cr_c, ci_c, dvec, tab, seq,
                                               comm=ex.gather_first())
        p, wfull = ex.take_first(gathered, p)
    ln_g = p["ln_gain"].reshape(4, 1, D_MODEL)
    ln_b = p["ln_bias"].reshape(4, 1, D_MODEL)
    y2, z, gg = _glu_fwd(y, wfull["s5_w_glu"], p["s5_b_glu"].reshape(1, D_MODEL))
    r1, h1, h1b = _mm_nn("s5_out", gg, wfull["s5_w_out"][None], F32, residual=xf, norm=(ln_g[0], ln_b[0]))
    plans = [ex.gather_attn(k) if ex else None for k in ("kv0", "q", "kv1")]
    h2, h2b, ffn0_saved, got = _ffn_fwd(
        "0", h1, h1b, wfull["ffn_w_up"][0], p["conv_w"][0], p["conv_b"][0], wfull["ffn_w_down"][0],
        ln_g[1], ln_b[1], seq, comm_up=plans[0], comm_conv=plans[1], comm_down=plans[2])
    if ex is not None:
        wfull = ex.take_attn({"kv0": got[0][0], "q": got[1][0], "kv1": got[2][0]}, wfull)

    plan = ex.gather_layer1("up", 0) if ex else None
    kmat, got = _carried(_mm_nn("attn_k", h2b, wfull["attn_w_kv"], BF16, blocks=(0, 4), comm=plan), plan)
    halves = {"up": [got[0]], "down": []} if ex else None
    vmat = _mm_nn("attn_v", h2b, wfull["attn_w_kv"], BF16, blocks=(4, 4))
    plan = ex.gather_attn("out") if ex else None
    qmat, got = _carried(_mm_nn("attn_q", h2b, wfull["attn_w_q"], BF16, comm=plan), plan)
    if ex is not None:
        wfull = ex.take_attn({"out": got[0]}, wfull)
    biases, outs, lses = [], [], []
    for g, dil in enumerate(DILATIONS):
        rel_t = p["rel_bias"][:, g * N_HEADS:(g + 1) * N_HEADS].T
        biases.append(_bias_expand(rel_t, dil).reshape(N_HEADS, BAND, 2 * BAND))
        which, half = {1: ("down", 0), 4: ("down", 1), 16: ("up", 1)}[dil]
        plan = ex.gather_layer1(which, half) if ex else None
        merge = (outs[0], lses[0], outs[1], lses[1]) if dil == DILATIONS[-1] else ()
        res, got = _carried(_attn_fwd(qmat, kmat, vmat, biases[g], g, dil, bsz, seq, comm=plan,
                                      merge=merge), plan)
        if plan is not None:
            halves[which].append(got[0])
        outs.append(res[0])
        lses.append(res[1])
    if ex is not None:
        wfull = ex.take_layer1(halves, wfull)
    o, lse, o_b = res
    r3, h3, h3b = _mm_nn("attn_out", o_b, wfull["attn_w_out"][None], F32, residual=h2, norm=(ln_g[2], ln_b[2]))
    h4, _, ffn1_saved, _ = _ffn_fwd("1", h3, h3b, wfull["ffn_w_up"][1], p["conv_w"][1], p["conv_b"][1],
                                    wfull["ffn_w_down"][1], ln_g[3], ln_b[3], seq)
    loss_tile, dh4 = _loss_head(h4, tf)

    grads = {}
    dh3, grads["ffn_w_up1"], grads["ffn_w_down1"], dconv1, dg3, db3, _ = _ffn_bwd(
        "1", ffn1_saved, dh4, wfull["ffn_w_up"][1], p["conv_w"][1], p["conv_b"][1], wfull["ffn_w_down"][1],
        ln_g[3], seq)
    dr3, dr3b, dg2, db2 = _ln_bwd("attn_norm_bwd", r3, dh3, ln_g[2])
    grads["attn_w_out"] = _mm_tn("attn_out_dw", o_b, dr3b, 1, BF16, g_mode="shared")
    do = _mm_nt_red("attn_out_dx", dr3b, wfull["attn_w_out"][None], F32, g_blocked=False)
    dqkv, drel = (), []
    for g, dil in enumerate(DILATIONS):
        plan = ex.scatter(grads, ("ffn_w_up1", "ffn_w_down1")) if ex and dil == DILATIONS[-1] else None
        res, got = _carried(_attn_bwd(qmat, kmat, vmat, do, o, lse, biases[g], g, dil, bsz, seq, comm=plan,
                                      into=dqkv), plan)
        if plan is not None:
            ex.keep(plan, got)
        dqkv, db_g = tuple(res[:3]), res[3]
        drel.append(_bias_reduce(db_g.reshape(N_HEADS, BAND * 2 * BAND), dil).T)
    dq, dk, dv = dqkv
    grads["rel_bias"] = jnp.concatenate(drel, axis=1)
    grads["attn_w_q"] = _mm_tn("attn_q_dw", h2b, dq, N_DEV, BF16)
    grads["attn_w_kv"] = jnp.concatenate([_mm_tn("attn_k_dw", h2b, dk, 4, BF16),
                                          _mm_tn("attn_v_dw", h2b, dv, 4, BF16)], axis=0)
    dh2 = _mm_nt_red("attn_q_dx", dq, wfull["attn_w_q"], F32, g_blocked=False, add=dr3, add_scale=DN_ALPHA)
    dh2 = _mm_nt_red("attn_k_dx", dk, wfull["attn_w_kv"], F32, g_blocked=False, add=dh2, blocks=(0, 4))
    dh2 = _mm_nt_red("attn_v_dx", dv, wfull["attn_w_kv"], F32, g_blocked=False, add=dh2, blocks=(4, 4))

    plans = [ex.scatter(grads, names) if ex else None
             for names in (("attn_w_kv#0", "attn_w_out"), ("attn_w_kv#1",), ("attn_w_q",))]
    dh1, grads["ffn_w_up0"], grads["ffn_w_down0"], dconv0, dg1, db1, got = _ffn_bwd(
        "0", ffn0_saved, dh2, wfull["ffn_w_up"][0], p["conv_w"][0], p["conv_b"][0], wfull["ffn_w_down"][0],
        ln_g[1], seq, comm_conv=plans[0], comm_dw=plans[1], comm_dx=plans[2])
    if ex is not None:
        for plan, outs in zip(plans, got):
            ex.keep(plan, outs)
    dr1, dr1b, dg0, db0 = _ln_bwd("s5_norm_bwd", r1, dh1, ln_g[0])
    grads["s5_w_out"] = _mm_tn("s5_out_dw", gg, dr1b, 1, BF16, g_mode="shared")
    dgg = _mm_nt_red("s5_out_dx", dr1b, wfull["s5_w_out"][None], F32, g_blocked=False)
    plan = ex.scatter(grads, ("s5_w_out",)) if ex else None
    (dz, d_bglu, dy2_direct), got = _carried(_glu_bwd_gate(y, z, dgg, comm=plan), plan)
    if ex is not None:
        ex.keep(plan, got)
    grads["s5_w_glu"] = _mm_tn("s5_glu_dw", y2, dz, 1, BF16, g_mode="shared")
    plan = ex.scatter(grads, ("s5_w_glu",)) if ex else None
    dy, got = _carried(_mm_nt_red("s5_glu_dx", dz, wfull["s5_w_glu"][None], F32, g_blocked=False, add=dy2_direct,
                                  post=(_gelu_bwd, y), comm=plan), plan)
    if ex is not None:
        ex.keep(plan, got)
    plan_last = ex.scatter(grads, ("ffn_w_up0", "ffn_w_down0")) if ex else None
    res = _s5_scan_bwd(xf, dy, dr1, h_r, h_i, bbr_c, bbi_c, cr_c, ci_c, dvec, tab, seq, comm=plan_last)
    if ex is not None:
        res, got = res
        ex.keep(plan_last, got)
    dx, dcr, dci, dbr, dbi, dar, dai, dd = res

    def bb_from_chunks(dense):
        return _diag_blocks(dense).transpose(2, 0, 1, 3).reshape(SSM_GROUP, N_STATE)

    d_lr, d_li, d_ldt, d_br, d_bi = _s5_prep_bwd(lr, li, ldt, br_t, bi_t, dar, dai,
                                                 bb_from_chunks(dbr), bb_from_chunks(dbi))
    grads["s5_lam_re"] = d_lr.reshape(p["s5_lam_re"].shape)
    grads["s5_lam_im"] = d_li.reshape(p["s5_lam_im"].shape)
    grads["s5_log_dt"] = d_ldt.reshape(p["s5_log_dt"].shape)
    grads["s5_b_re"] = d_br.T.reshape(p["s5_b_re"].shape)
    grads["s5_b_im"] = d_bi.T.reshape(p["s5_b_im"].shape)
    grads["s5_c_re"] = _diag_blocks(dcr).reshape(p["s5_c_re"].shape)
    grads["s5_c_im"] = -_diag_blocks(dci).reshape(p["s5_c_im"].shape)
    grads["s5_d"] = dd.reshape(p["s5_d"].shape)
    grads["s5_b_glu"] = d_bglu.reshape(1, D_MODEL)
    dconv = jnp.stack([dconv0, dconv1]).reshape(2, N_DEV, 8, -1)
    grads["ffn_conv_w"] = dconv[:, :, 0:3, :].transpose(0, 2, 1, 3)
    grads["ffn_conv_b"] = dconv[:, :, 3, :]
    grads["ln_gain"] = jnp.stack([dg0, dg1, dg2, dg3]).reshape(2, 2, D_MODEL)
    grads["ln_bias"] = jnp.stack([db0, db1, db2, db3]).reshape(2, 2, D_MODEL)
    return loss_tile[0, 0], dx.reshape(x.shape), grads


SMALL_REPLICATED = ("s5_lam_re", "s5_lam_im", "s5_log_dt", "s5_b_re", "s5_b_im", "s5_c_re", "s5_c_im", "s5_d",
                    "rel_bias", "ffn_conv_b")
SMALL_SHARDED = ("s5_b_glu", "ffn_conv_w", "ln_gain", "ln_bias")
BIG = ("s5_w_glu", "s5_w_out", "attn_w_kv", "attn_w_q", "attn_w_out", "ffn_w_up", "ffn_w_down")
WEIGHTS = ("s5_lam_re", "s5_lam_im", "s5_log_dt", "s5_b_re", "s5_b_im", "s5_c_re", "s5_c_im", "s5_d", "s5_w_glu",
           "s5_b_glu", "s5_w_out", "attn_w_kv", "attn_w_q", "attn_w_out", "rel_bias", "ffn_w_up", "ffn_conv_w",
           "ffn_conv_b", "ffn_w_down", "ln_gain", "ln_bias")


class _Exchanges:
    def __init__(self, w):
        self.w = w
        self.parts = {}
        self.up = w["ffn_w_up"].astype(BF16)
        self.down = w["ffn_w_down"].astype(BF16)

    def gather_first(self):
        w = self.w
        srcs = [w["s5_w_glu"][0].astype(BF16), w["s5_w_out"][0].astype(BF16),
                _pack([w[k] for k in SMALL_SHARDED]), self.up, self.down]
        outs = [_sds((N_DEV,) + (s.shape[1:] if i in (3, 4) else s.shape), s.dtype) for i, s in enumerate(srcs)]
        items = [(i, (0,) if i in (3, 4) else (), i, ("slot",)) for i in range(len(srcs))]
        return _gather_plan(srcs, outs, items)

    def gather_attn(self, piece):
        w = self.w
        if piece in ("kv0", "kv1"):
            src, rows = w["attn_w_kv"].astype(BF16), w["attn_w_kv"].shape[0] // 2
            return _gather_plan([src], [_sds((N_DEV, rows, src.shape[1]), BF16)],
                                [(0, (pl.ds(int(piece[2]) * rows, rows),), 0, ("slot",))])
        src = (w["attn_w_q"] if piece == "q" else w["attn_w_out"])[0].astype(BF16)
        return _gather_plan([src], [_sds((N_DEV,) + src.shape, BF16)], [(0, (), 0, ("slot",))])

    def take_attn(self, pieces, wfull):
        wfull = dict(wfull)
        if "kv0" in pieces:
            wfull["attn_w_kv"] = jnp.concatenate([pieces["kv0"], pieces["kv1"]], axis=1)
            wfull["attn_w_q"] = pieces["q"]
        if "out" in pieces:
            wfull["attn_w_out"] = pieces["out"].reshape(D_MODEL, D_MODEL)
        return wfull

    def take_first(self, g, p):
        d = D_MODEL
        w = self.w
        wfull = {
            "s5_w_glu": g[0].reshape(d, d),
            "s5_w_out": g[1].reshape(d, d),
            "ffn_w_up": [g[3], None],
            "ffn_w_down": [g[4].reshape(4, -1, d), None],
        }
        smalls = [_unpack(g[2][dev], [w[k].shape for k in SMALL_SHARDED]) for dev in range(N_DEV)]
        c_ff = w["ffn_conv_w"].shape[2]
        conv_w = jnp.stack([s[1] for s in smalls], axis=2)
        p = dict(p)
        p.update(s5_b_glu=jnp.concatenate([s[0] for s in smalls], axis=1),
                 ln_gain=jnp.concatenate([s[2] for s in smalls], axis=2),
                 ln_bias=jnp.concatenate([s[3] for s in smalls], axis=2),
                 conv_w=conv_w.transpose(0, 2, 1, 3).reshape(2, 2, 4, 3, c_ff),
                 conv_b=w["ffn_conv_b"].reshape(2, 2, 4, 1, c_ff))
        return p, wfull

    def gather_layer1(self, which, half):
        src = self.up if which == "up" else self.down
        rows = src.shape[1] // 2
        return _gather_plan([src], [_sds((N_DEV, rows) + src.shape[2:], BF16)],
                            [(0, (1, pl.ds(half * rows, rows)), 0, ("slot",))])

    def take_layer1(self, halves, wfull):
        up = jnp.concatenate(halves["up"], axis=1)
        down = jnp.concatenate(halves["down"], axis=1)
        wfull = dict(wfull)
        wfull["ffn_w_up"] = [wfull["ffn_w_up"][0], up]
        wfull["ffn_w_down"] = [wfull["ffn_w_down"][0], down.reshape(4, -1, D_MODEL)]
        return wfull

    def scatter(self, grads, names):
        srcs = []
        for k in names:
            name, _, half = k.partition("#")
            g = grads[name].reshape(N_DEV, -1, grads[name].shape[-1])
            if half:
                rows = g.shape[1] // 2
                g = g[:, int(half) * rows:(int(half) + 1) * rows]
            srcs.append(g)
        plan = _scatter_plan(srcs, [_sds(s.shape, BF16) for s in srcs],
                             [(i, i, ("slot",)) for i in range(len(names))])
        plan.names = names
        return plan

    def keep(self, plan, outs):
        for k, o in zip(plan.names, outs):
            self.parts[k] = o


def kernel(x, s5_lam_re, s5_lam_im, s5_log_dt, s5_b_re, s5_b_im, s5_c_re, s5_c_im, s5_d, s5_w_glu, s5_b_glu, s5_w_out, attn_w_kv, attn_w_q, attn_w_out, rel_bias, ffn_w_up, ffn_conv_w, ffn_conv_b, ffn_w_down, ln_gain, ln_bias, loss_target, m_s5_lam_re, m_s5_lam_im, m_s5_log_dt, m_s5_b_re, m_s5_b_im, m_s5_c_re, m_s5_c_im, m_s5_d, m_s5_w_glu, m_s5_b_glu, m_s5_w_out, m_attn_w_kv, m_attn_w_q, m_attn_w_out, m_rel_bias, m_ffn_w_up, m_ffn_conv_w, m_ffn_conv_b, m_ffn_w_down, m_ln_gain, m_ln_bias, v_s5_lam_re, v_s5_lam_im, v_s5_log_dt, v_s5_b_re, v_s5_b_im, v_s5_c_re, v_s5_c_im, v_s5_d, v_s5_w_glu, v_s5_b_glu, v_s5_w_out, v_attn_w_kv, v_attn_w_q, v_attn_w_out, v_rel_bias, v_ffn_w_up, v_ffn_conv_w, v_ffn_conv_b, v_ffn_w_down, v_ln_gain, v_ln_bias):
    w = dict(s5_lam_re=s5_lam_re, s5_lam_im=s5_lam_im, s5_log_dt=s5_log_dt, s5_b_re=s5_b_re, s5_b_im=s5_b_im,
             s5_c_re=s5_c_re, s5_c_im=s5_c_im, s5_d=s5_d, s5_w_glu=s5_w_glu, s5_b_glu=s5_b_glu, s5_w_out=s5_w_out,
             attn_w_kv=attn_w_kv, attn_w_q=attn_w_q, attn_w_out=attn_w_out, rel_bias=rel_bias, ffn_w_up=ffn_w_up,
             ffn_conv_w=ffn_conv_w, ffn_conv_b=ffn_conv_b, ffn_w_down=ffn_w_down, ln_gain=ln_gain, ln_bias=ln_bias)
    mom = dict(s5_lam_re=m_s5_lam_re, s5_lam_im=m_s5_lam_im, s5_log_dt=m_s5_log_dt, s5_b_re=m_s5_b_re,
               s5_b_im=m_s5_b_im, s5_c_re=m_s5_c_re, s5_c_im=m_s5_c_im, s5_d=m_s5_d, s5_w_glu=m_s5_w_glu,
               s5_b_glu=m_s5_b_glu, s5_w_out=m_s5_w_out, attn_w_kv=m_attn_w_kv, attn_w_q=m_attn_w_q,
               attn_w_out=m_attn_w_out, rel_bias=m_rel_bias, ffn_w_up=m_ffn_w_up, ffn_conv_w=m_ffn_conv_w,
               ffn_conv_b=m_ffn_conv_b, ffn_w_down=m_ffn_w_down, ln_gain=m_ln_gain, ln_bias=m_ln_bias)
    var = dict(s5_lam_re=v_s5_lam_re, s5_lam_im=v_s5_lam_im, s5_log_dt=v_s5_log_dt, s5_b_re=v_s5_b_re,
               s5_b_im=v_s5_b_im, s5_c_re=v_s5_c_re, s5_c_im=v_s5_c_im, s5_d=v_s5_d, s5_w_glu=v_s5_w_glu,
               s5_b_glu=v_s5_b_glu, s5_w_out=v_s5_w_out, attn_w_kv=v_attn_w_kv, attn_w_q=v_attn_w_q,
               attn_w_out=v_attn_w_out, rel_bias=v_rel_bias, ffn_w_up=v_ffn_w_up, ffn_conv_w=v_ffn_conv_w,
               ffn_conv_b=v_ffn_conv_b, ffn_w_down=v_ffn_w_down, ln_gain=v_ln_gain, ln_bias=v_ln_bias)
    bsz, seq, _ = x.shape
    me = 4 * lax.axis_index("x") + 2 * lax.axis_index("y") + lax.axis_index("c")

    ex = _Exchanges(w)
    loss_part, grad_x, g = _local_step(x, loss_target, dict(w), None, bsz, seq, ex=ex)
    loss = lax.psum(loss_part, ("x", "y", "c"))

    d = D_MODEL
    results = {}
    for name in BIG:
        shard = w[name]
        if name == "ffn_w_up":
            parts = [ex.parts[name + "0"], ex.parts[name + "1"]]
            outs4 = _adamw("adamw_" + name, parts, *(jnp.swapaxes(a, 1, 2) for a in (shard, mom[name], var[name])))
            results[name] = tuple(jnp.swapaxes(o, 1, 2) for o in outs4)
            continue
        if name == "ffn_w_down":
            parts, r3 = [ex.parts[name + "0"], ex.parts[name + "1"]], shard.shape
        elif name == "attn_w_kv":
            parts = [jnp.concatenate([ex.parts[name + "#0"], ex.parts[name + "#1"]], axis=1)]
            r3 = (1, -1, shard.shape[-1])
        else:
            parts, r3 = [ex.parts[name]], (1, -1, shard.shape[-1])
        outs4 = _adamw("adamw_" + name, parts, shard.reshape(r3), mom[name].reshape(r3), var[name].reshape(r3))
        results[name] = tuple(o.reshape(shard.shape) for o in outs4)

    small_names = SMALL_REPLICATED + SMALL_SHARDED
    packed = _pack([g[k] for k in small_names])
    pad = (-packed.shape[0]) % (8 * N_DEV)
    if pad:
        packed = jnp.concatenate([packed, jnp.zeros((pad, 128), F32)], axis=0)
    piece = packed.shape[0] // N_DEV
    slots = _exchange("reduce_small", [packed.reshape(N_DEV, piece, 128)], [_sds((N_DEV, piece, 128), F32)],
                      [(0, ("peer",), 0, ("me",))])[0]
    summed = _exchange("gather_small", [_sum_slots(slots)], [_sds((N_DEV, piece, 128), F32)],
                       [(0, (), 0, ("me",))])[0]
    total = _unpack(summed.reshape(N_DEV * piece, 128), [g[k].shape for k in small_names])
    gsum = dict(zip(small_names, total))
    mine = {k: gsum[k] for k in SMALL_REPLICATED}
    mine["s5_b_glu"] = lax.dynamic_slice_in_dim(gsum["s5_b_glu"], me * (d // N_DEV), d // N_DEV, axis=1)
    mine["ffn_conv_w"] = lax.dynamic_index_in_dim(gsum["ffn_conv_w"], me, axis=2, keepdims=False)
    mine["ln_gain"] = lax.dynamic_slice_in_dim(gsum["ln_gain"], me * (d // N_DEV), d // N_DEV, axis=2)
    mine["ln_bias"] = lax.dynamic_slice_in_dim(gsum["ln_bias"], me * (d // N_DEV), d // N_DEV, axis=2)
    gp = _pack([mine[k] for k in small_names])
    outs4 = _adamw("adamw_small", [gp[None]], _pack([w[k] for k in small_names])[None],
                   _pack([mom[k] for k in small_names])[None], _pack([var[k] for k in small_names])[None])
    outs4 = [o[0] for o in outs4]
    shapes = [w[k].shape for k in small_names]
    unpacked = [_unpack(o, shapes) for o in outs4]
    for i, k in enumerate(small_names):
        results[k] = tuple(unpacked[j][i] for j in range(4))

    out = [loss, grad_x]
    for j in range(4):
        out.extend(results[k][j] for k in WEIGHTS)
    return tuple(out)
```
